```python
import math
import jax, jax.numpy as jnp
from jax import lax
import numpy as np

D_MODEL = 2048
BATCH = 8
SEQ = 8192
DEPTH = 1

D_SSM = 1024
SSM_GROUP = 16
N_SSM_GROUPS = D_SSM // SSM_GROUP
SSM_STATE = 64
DT_MIN = 0.001
DT_MAX = 0.1
N_Q_HEADS = 16
N_KV_HEADS = 4
HEAD_DIM = 64
Q_PER_KV = N_Q_HEADS // N_KV_HEADS
D_ATTN = N_Q_HEADS * HEAD_DIM
D_KV = N_KV_HEADS * HEAD_DIM
WINDOW = 128
BLOCK = 128
N_BUCKETS = 32
MAX_DISTANCE = 128
N_BRANCHES = 2
D_IN = D_SSM + D_SSM + D_ATTN + D_KV + D_KV + D_ATTN + N_BRANCHES * D_MODEL
DEEPNORM_ALPHA = (2.0 * DEPTH) ** 0.25
DEEPNORM_BETA = (8.0 * DEPTH) ** -0.25
LN_EPS = 1e-5
NEG_INF = -1e30

kernel_name = "hybrid_s5_swa_sink_gated_deepnorm"


def _split_columns(proj):
    sizes = (D_SSM, D_SSM, D_ATTN, D_KV, D_KV, D_ATTN, N_BRANCHES * D_MODEL)
    points = []
    acc = 0
    for s in sizes[:-1]:
        acc += s
        points.append(acc)
    return jnp.split(proj, points, axis=-1)


def _layer_norm(x, gain, bias):
    xf = x.astype(jnp.float32)
    mu = jnp.mean(xf, axis=-1, keepdims=True)
    var = jnp.mean(jnp.square(xf - mu), axis=-1, keepdims=True)
    y = (xf - mu) * lax.rsqrt(var + LN_EPS) * gain.astype(jnp.float32) + bias.astype(jnp.float32)
    return y.astype(x.dtype)


def _t5_causal_bucket(dist):
    max_exact = N_BUCKETS // 2
    is_small = dist < max_exact
    d = jnp.maximum(dist, 1).astype(jnp.float32)
    large = max_exact + (jnp.log(d / max_exact) / math.log(MAX_DISTANCE / max_exact)
                         * (N_BUCKETS - max_exact)).astype(jnp.int32)
    large = jnp.minimum(large, N_BUCKETS - 1)
    return jnp.where(is_small, dist, large)


def _band_bias_and_mask(rel_bias_table, n_blocks):
    i = jnp.arange(BLOCK)[:, None]
    j = jnp.arange(2 * BLOCK)[None, :]
    dist = BLOCK + i - j
    band_ok = (dist >= 0) & (dist < WINDOW)
    bucket = _t5_causal_bucket(jnp.clip(dist, 0, None))
    bias = rel_bias_table.astype(jnp.float32)[bucket]
    bias = jnp.transpose(bias, (2, 0, 1)).reshape(N_KV_HEADS, Q_PER_KV, BLOCK, 2 * BLOCK)
    n = jnp.arange(n_blocks)[:, None, None]
    key_abs = n * BLOCK - BLOCK + j[None]
    mask = band_ok[None] & (key_abs >= 0)
    return bias, mask[None, :, None, None]


def _sliding_window_gqa(q, k, v, sinks, rel_bias_table):
    b, s, _ = q.shape
    nb = s // BLOCK
    q = q.reshape(b, nb, BLOCK, N_KV_HEADS, Q_PER_KV, HEAD_DIM)

    def band(t):
        t = t.reshape(b, s, N_KV_HEADS, HEAD_DIM)
        t = jnp.pad(t, ((0, 0), (BLOCK, 0), (0, 0), (0, 0))).reshape(b, nb + 1, BLOCK, N_KV_HEADS, HEAD_DIM)
        return jnp.concatenate([t[:, :-1], t[:, 1:]], axis=2)

    kb, vb = band(k), band(v)
    bias, mask = _band_bias_and_mask(rel_bias_table, nb)
    logits = jnp.einsum("bnqkgd,bnskd->bnkgqs", q, kb).astype(jnp.float32) * (HEAD_DIM ** -0.5)
    logits = jnp.where(mask, logits + bias, NEG_INF)
    sink = sinks.astype(jnp.float32).reshape(N_KV_HEADS, Q_PER_KV)[None, None, :, :, None, None]
    m = jnp.maximum(jnp.max(logits, axis=-1, keepdims=True), sink)
    p = jnp.exp(logits - m)
    p = p / (jnp.sum(p, axis=-1, keepdims=True) + jnp.exp(sink - m))
    out = jnp.einsum("bnkgqs,bnskd->bnqkgd", p.astype(vb.dtype), vb)
    return out.reshape(b, s, D_ATTN)


def _s5_scan_op(e1, e2):
    a1, b1 = e1
    a2, b2 = e2
    return a1 * a2, a2 * b1 + b2


def _s5_ssm(u, lam_re, lam_im, b_re, b_im, c_re, c_im, d_skip, log_step):
    b, s, _ = u.shape
    f32 = jnp.float32
    step = jnp.exp(log_step.astype(f32))[:, None]
    lam = lax.complex(lam_re.astype(f32), lam_im.astype(f32))
    lam_bar = jnp.exp(lam * step)
    b_cplx = lax.complex(b_re.astype(f32), b_im.astype(f32))
    b_bar = ((lam_bar - 1.0) / lam)[..., None] * b_cplx
    ug = u.astype(f32).reshape(b, s, N_SSM_GROUPS, SSM_GROUP)
    bu = lax.complex(jnp.einsum("bsgh,gph->sbgp", ug, jnp.real(b_bar)),
                     jnp.einsum("bsgh,gph->sbgp", ug, jnp.imag(b_bar)))
    a = jnp.broadcast_to(lam_bar[None, None], (s, 1, N_SSM_GROUPS, SSM_STATE))
    _, states = lax.associative_scan(_s5_scan_op, (a, bu), axis=0)
    y = (jnp.einsum("sbgp,ghp->bsgh", jnp.real(states), c_re.astype(f32))
         - jnp.einsum("sbgp,ghp->bsgh", jnp.imag(states), c_im.astype(f32)))
    y = y + d_skip.astype(f32).reshape(N_SSM_GROUPS, SSM_GROUP) * ug
    return y.reshape(b, s, D_SSM)


def _fwd_setup_inputs(seed: int = 0) -> dict:
    key = jax.random.key(seed)
    ks = jax.random.split(key, 20)
    f32 = jnp.float32
    x = jax.random.normal(ks[0], (BATCH, SEQ, D_MODEL), f32)
    w_in = jax.random.normal(ks[1], (DEPTH, D_MODEL, D_IN), f32) * D_MODEL ** -0.5
    n_idx = jnp.arange(SSM_STATE, dtype=f32)
    ssm_lambda_re = -0.5 + 0.01 * jax.random.normal(ks[2], (DEPTH, N_SSM_GROUPS, SSM_STATE), f32)
    ssm_lambda_im = math.pi * n_idx + 0.01 * jax.random.normal(ks[3], (DEPTH, N_SSM_GROUPS, SSM_STATE), f32)
    ssm_b_re = jax.random.normal(ks[4], (DEPTH, N_SSM_GROUPS, SSM_STATE, SSM_GROUP), f32) * (2.0 * SSM_GROUP) ** -0.5
    ssm_b_im = jax.random.normal(ks[5], (DEPTH, N_SSM_GROUPS, SSM_STATE, SSM_GROUP), f32) * (2.0 * SSM_GROUP) ** -0.5
    ssm_c_re = jax.random.normal(ks[6], (DEPTH, N_SSM_GROUPS, SSM_GROUP, SSM_STATE), f32) * SSM_STATE ** -0.5
    ssm_c_im = jax.random.normal(ks[7], (DEPTH, N_SSM_GROUPS, SSM_GROUP, SSM_STATE), f32) * SSM_STATE ** -0.5
    ssm_d = jax.random.normal(ks[8], (DEPTH, D_SSM), f32)
    ssm_log_step = jax.random.uniform(ks[9], (DEPTH, N_SSM_GROUPS), f32,
                                      minval=math.log(DT_MIN), maxval=math.log(DT_MAX))
    w_glu = jax.random.normal(ks[10], (DEPTH, D_SSM, 2 * D_SSM), f32) * D_SSM ** -0.5
    attn_sinks = jax.random.normal(ks[11], (DEPTH, N_Q_HEADS), f32)
    rel_bias_table = 0.5 * jax.random.normal(ks[12], (N_BUCKETS, N_Q_HEADS), f32)
    w_branch_ssm = jax.random.normal(ks[13], (DEPTH, D_SSM, D_MODEL), f32) * D_SSM ** -0.5 * DEEPNORM_BETA
    w_branch_attn = jax.random.normal(ks[14], (DEPTH, D_ATTN, D_MODEL), f32) * D_ATTN ** -0.5 * DEEPNORM_BETA
    w_out = jax.random.normal(ks[15], (DEPTH, D_MODEL, D_MODEL), f32) * D_MODEL ** -0.5 * DEEPNORM_BETA
    ln_gain = 1.0 + 0.02 * jax.random.normal(ks[16], (DEPTH, D_MODEL), f32)
    ln_bias = 0.02 * jax.random.normal(ks[17], (DEPTH, D_MODEL), f32)
    return {"x": x, "w_in": w_in, "ssm_lambda_re": ssm_lambda_re, "ssm_lambda_im": ssm_lambda_im,
            "ssm_b_re": ssm_b_re, "ssm_b_im": ssm_b_im, "ssm_c_re": ssm_c_re, "ssm_c_im": ssm_c_im,
            "ssm_d": ssm_d, "ssm_log_step": ssm_log_step, "w_glu": w_glu, "attn_sinks": attn_sinks,
            "rel_bias_table": rel_bias_table, "w_branch_ssm": w_branch_ssm, "w_branch_attn": w_branch_attn,
            "w_out": w_out, "ln_gain": ln_gain, "ln_bias": ln_bias}


def _fwd_reference(x, w_in, ssm_lambda_re, ssm_lambda_im, ssm_b_re, ssm_b_im, ssm_c_re, ssm_c_im,
              ssm_d, ssm_log_step, w_glu, attn_sinks, rel_bias_table, w_branch_ssm, w_branch_attn,
              w_out, ln_gain, ln_bias):
    for layer in range(DEPTH):
        proj = jnp.einsum("bsd,de->bse", x, w_in[layer])
        u_ssm, z_ssm, q, k, v, z_attn, gate_logits = _split_columns(proj)

        y_ssm = _s5_ssm(u_ssm, ssm_lambda_re[layer], ssm_lambda_im[layer], ssm_b_re[layer], ssm_b_im[layer],
                        ssm_c_re[layer], ssm_c_im[layer], ssm_d[layer], ssm_log_step[layer])
        glu_in = jax.nn.gelu(y_ssm, approximate=False)
        glu_a, glu_b = jnp.split(jnp.einsum("bsc,ce->bse", glu_in, w_glu[layer].astype(jnp.float32)), 2, axis=-1)
        h_ssm = (glu_a * jax.nn.sigmoid(glu_b)).astype(x.dtype) * jax.nn.silu(z_ssm)

        h_attn = _sliding_window_gqa(q, k, v, attn_sinks[layer], rel_bias_table) * jax.nn.silu(z_attn)

        gates = jax.nn.sigmoid(gate_logits.astype(jnp.float32)).astype(x.dtype)
        gate_ssm, gate_attn = jnp.split(gates, 2, axis=-1)
        merged = (gate_ssm * jnp.einsum("bsc,cd->bsd", h_ssm, w_branch_ssm[layer])
                  + gate_attn * jnp.einsum("bsc,cd->bsd", h_attn, w_branch_attn[layer]))
        out = jnp.einsum("bsd,de->bse", merged, w_out[layer])

        x = _layer_norm(DEEPNORM_ALPHA * x + out.astype(x.dtype), ln_gain[layer], ln_bias[layer])
    return x


import jax as _jax
import jax.numpy as _jnp

TWIN_FORMAT = 'train_step'
FWD_PARAMS = ['x', 'w_in', 'ssm_lambda_re', 'ssm_lambda_im', 'ssm_b_re', 'ssm_b_im', 'ssm_c_re', 'ssm_c_im', 'ssm_d', 'ssm_log_step', 'w_glu', 'attn_sinks', 'rel_bias_table', 'w_branch_ssm', 'w_branch_attn', 'w_out', 'ln_gain', 'ln_bias']
TWIN_WEIGHTS = ['w_in', 'ssm_lambda_re', 'ssm_lambda_im', 'ssm_b_re', 'ssm_b_im', 'ssm_c_re', 'ssm_c_im', 'ssm_d', 'ssm_log_step', 'w_glu', 'attn_sinks', 'rel_bias_table', 'w_branch_ssm', 'w_branch_attn', 'w_out', 'ln_gain', 'ln_bias']
TWIN_DIFF_INPUT = 'x'
TWIN_INPUTS = ['x', 'w_in', 'ssm_lambda_re', 'ssm_lambda_im', 'ssm_b_re', 'ssm_b_im', 'ssm_c_re', 'ssm_c_im', 'ssm_d', 'ssm_log_step', 'w_glu', 'attn_sinks', 'rel_bias_table', 'w_branch_ssm', 'w_branch_attn', 'w_out', 'ln_gain', 'ln_bias', 'loss_target', 'm_w_in', 'm_ssm_lambda_re', 'm_ssm_lambda_im', 'm_ssm_b_re', 'm_ssm_b_im', 'm_ssm_c_re', 'm_ssm_c_im', 'm_ssm_d', 'm_ssm_log_step', 'm_w_glu', 'm_attn_sinks', 'm_rel_bias_table', 'm_w_branch_ssm', 'm_w_branch_attn', 'm_w_out', 'm_ln_gain', 'm_ln_bias', 'v_w_in', 'v_ssm_lambda_re', 'v_ssm_lambda_im', 'v_ssm_b_re', 'v_ssm_b_im', 'v_ssm_c_re', 'v_ssm_c_im', 'v_ssm_d', 'v_ssm_log_step', 'v_w_glu', 'v_attn_sinks', 'v_rel_bias_table', 'v_w_branch_ssm', 'v_w_branch_attn', 'v_w_out', 'v_ln_gain', 'v_ln_bias']
TWIN_OUTPUTS = ['loss', 'grad_x', 'grad_w_in', 'grad_ssm_lambda_re', 'grad_ssm_lambda_im', 'grad_ssm_b_re', 'grad_ssm_b_im', 'grad_ssm_c_re', 'grad_ssm_c_im', 'grad_ssm_d', 'grad_ssm_log_step', 'grad_w_glu', 'grad_attn_sinks', 'grad_rel_bias_table', 'grad_w_branch_ssm', 'grad_w_branch_attn', 'grad_w_out', 'grad_ln_gain', 'grad_ln_bias', 'delta_w_in', 'delta_ssm_lambda_re', 'delta_ssm_lambda_im', 'delta_ssm_b_re', 'delta_ssm_b_im', 'delta_ssm_c_re', 'delta_ssm_c_im', 'delta_ssm_d', 'delta_ssm_log_step', 'delta_w_glu', 'delta_attn_sinks', 'delta_rel_bias_table', 'delta_w_branch_ssm', 'delta_w_branch_attn', 'delta_w_out', 'delta_ln_gain', 'delta_ln_bias', 'new_m_w_in', 'new_m_ssm_lambda_re', 'new_m_ssm_lambda_im', 'new_m_ssm_b_re', 'new_m_ssm_b_im', 'new_m_ssm_c_re', 'new_m_ssm_c_im', 'new_m_ssm_d', 'new_m_ssm_log_step', 'new_m_w_glu', 'new_m_attn_sinks', 'new_m_rel_bias_table', 'new_m_w_branch_ssm', 'new_m_w_branch_attn', 'new_m_w_out', 'new_m_ln_gain', 'new_m_ln_bias', 'new_v_w_in', 'new_v_ssm_lambda_re', 'new_v_ssm_lambda_im', 'new_v_ssm_b_re', 'new_v_ssm_b_im', 'new_v_ssm_c_re', 'new_v_ssm_c_im', 'new_v_ssm_d', 'new_v_ssm_log_step', 'new_v_w_glu', 'new_v_attn_sinks', 'new_v_rel_bias_table', 'new_v_w_branch_ssm', 'new_v_w_branch_attn', 'new_v_w_out', 'new_v_ln_gain', 'new_v_ln_bias']
TWIN_LEAF_KINDS = {'loss': 'loss', 'grad_x': 'grad_x', 'grad_w_in': 'grad_w', 'grad_ssm_lambda_re': 'grad_w', 'grad_ssm_lambda_im': 'grad_w', 'grad_ssm_b_re': 'grad_w', 'grad_ssm_b_im': 'grad_w', 'grad_ssm_c_re': 'grad_w', 'grad_ssm_c_im': 'grad_w', 'grad_ssm_d': 'grad_w', 'grad_ssm_log_step': 'grad_w', 'grad_w_glu': 'grad_w', 'grad_attn_sinks': 'grad_w', 'grad_rel_bias_table': 'grad_w', 'grad_w_branch_ssm': 'grad_w', 'grad_w_branch_attn': 'grad_w', 'grad_w_out': 'grad_w', 'grad_ln_gain': 'grad_w', 'grad_ln_bias': 'grad_w', 'delta_w_in': 'delta_w', 'delta_ssm_lambda_re': 'delta_w', 'delta_ssm_lambda_im': 'delta_w', 'delta_ssm_b_re': 'delta_w', 'delta_ssm_b_im': 'delta_w', 'delta_ssm_c_re': 'delta_w', 'delta_ssm_c_im': 'delta_w', 'delta_ssm_d': 'delta_w', 'delta_ssm_log_step': 'delta_w', 'delta_w_glu': 'delta_w', 'delta_attn_sinks': 'delta_w', 'delta_rel_bias_table': 'delta_w', 'delta_w_branch_ssm': 'delta_w', 'delta_w_branch_attn': 'delta_w', 'delta_w_out': 'delta_w', 'delta_ln_gain': 'delta_w', 'delta_ln_bias': 'delta_w', 'new_m_w_in': 'new_m', 'new_m_ssm_lambda_re': 'new_m', 'new_m_ssm_lambda_im': 'new_m', 'new_m_ssm_b_re': 'new_m', 'new_m_ssm_b_im': 'new_m', 'new_m_ssm_c_re': 'new_m', 'new_m_ssm_c_im': 'new_m', 'new_m_ssm_d': 'new_m', 'new_m_ssm_log_step': 'new_m', 'new_m_w_glu': 'new_m', 'new_m_attn_sinks': 'new_m', 'new_m_rel_bias_table': 'new_m', 'new_m_w_branch_ssm': 'new_m', 'new_m_w_branch_attn': 'new_m', 'new_m_w_out': 'new_m', 'new_m_ln_gain': 'new_m', 'new_m_ln_bias': 'new_m', 'new_v_w_in': 'new_v', 'new_v_ssm_lambda_re': 'new_v', 'new_v_ssm_lambda_im': 'new_v', 'new_v_ssm_b_re': 'new_v', 'new_v_ssm_b_im': 'new_v', 'new_v_ssm_c_re': 'new_v', 'new_v_ssm_c_im': 'new_v', 'new_v_ssm_d': 'new_v', 'new_v_ssm_log_step': 'new_v', 'new_v_w_glu': 'new_v', 'new_v_attn_sinks': 'new_v', 'new_v_rel_bias_table': 'new_v', 'new_v_w_branch_ssm': 'new_v', 'new_v_w_branch_attn': 'new_v', 'new_v_w_out': 'new_v', 'new_v_ln_gain': 'new_v', 'new_v_ln_bias': 'new_v'}


def _forward(args):
    return _fwd_reference(*[args[k] for k in FWD_PARAMS])


def _output_shape():
    def fwd():
        inp = _fwd_setup_inputs(0)
        return _fwd_reference(*[inp[k] for k in FWD_PARAMS])
    out = _jax.eval_shape(fwd)
    return out.shape, out.dtype

N_MICROBATCH = 1
ADAM_LR = 0.001
ADAM_B1 = 0.9
ADAM_B2 = 0.999
ADAM_EPS = 1e-08
ADAM_WD = 0.01
ADAM_STEP = 10
PER_EXAMPLE_BATCH_AXIS = {'x': 0, 'loss_target': 0}
SHARED_INPUTS = []
_WEIGHT_DTYPES = {'w_in': _jnp.float32, 'ssm_lambda_re': _jnp.float32, 'ssm_lambda_im': _jnp.float32, 'ssm_b_re': _jnp.float32, 'ssm_b_im': _jnp.float32, 'ssm_c_re': _jnp.float32, 'ssm_c_im': _jnp.float32, 'ssm_d': _jnp.float32, 'ssm_log_step': _jnp.float32, 'w_glu': _jnp.float32, 'attn_sinks': _jnp.float32, 'rel_bias_table': _jnp.float32, 'w_branch_ssm': _jnp.float32, 'w_branch_attn': _jnp.float32, 'w_out': _jnp.float32, 'ln_gain': _jnp.float32, 'ln_bias': _jnp.float32}
MOMENT_SCALE = {'w_in': 3.922544e-03, 'ssm_lambda_re': 5.451980e-04, 'ssm_lambda_im': 4.369740e-04, 'ssm_b_re': 2.941071e-04, 'ssm_b_im': 2.991662e-04, 'ssm_c_re': 4.115270e-04, 'ssm_c_im': 4.198642e-04, 'ssm_d': 6.741252e-03, 'ssm_log_step': 5.835063e-01, 'w_glu': 4.586775e-03, 'attn_sinks': 3.625891e-03, 'rel_bias_table': 4.438700e-03, 'w_branch_ssm': 7.381688e-03, 'w_branch_attn': 4.410537e-03, 'w_out': 8.235580e-03, 'ln_gain': 3.196861e+01, 'ln_bias': 4.665965e-01}


def _to_microbatches(a, axis):
    t = _jnp.moveaxis(a, axis, 0)
    t = t.reshape((N_MICROBATCH, t.shape[0] // N_MICROBATCH) + t.shape[1:])
    return _jnp.moveaxis(t, 1, axis + 1)


def setup_inputs(seed: int = 0) -> dict:
    inp = _fwd_setup_inputs(seed)
    key = _jax.random.fold_in(_jax.random.key(seed), 7919)
    shape, _ = _output_shape()
    out = dict(inp)
    out["loss_target"] = _jax.random.normal(_jax.random.fold_in(key, 0), shape, _jnp.float32)
    for i, name in enumerate(TWIN_WEIGHTS):
        w = inp[name].astype(_jnp.float32)
        if MOMENT_SCALE is None:
            s = _jnp.sqrt(_jnp.mean(_jnp.square(w)) + 1e-30)
        else:
            s = MOMENT_SCALE[name]
        km, kv = _jax.random.split(_jax.random.fold_in(key, i + 1))
        out[name] = w
        out["m_" + name] = s * _jax.random.normal(km, w.shape, _jnp.float32)
        out["v_" + name] = (s * s) * _jax.random.uniform(kv, w.shape, _jnp.float32, 0.5, 1.5)
    if N_MICROBATCH > 1:
        for name, axis in PER_EXAMPLE_BATCH_AXIS.items():
            out[name] = _to_microbatches(out[name], axis)
    return {'x': out['x'], 'w_in': out['w_in'], 'ssm_lambda_re': out['ssm_lambda_re'], 'ssm_lambda_im': out['ssm_lambda_im'], 'ssm_b_re': out['ssm_b_re'], 'ssm_b_im': out['ssm_b_im'], 'ssm_c_re': out['ssm_c_re'], 'ssm_c_im': out['ssm_c_im'], 'ssm_d': out['ssm_d'], 'ssm_log_step': out['ssm_log_step'], 'w_glu': out['w_glu'], 'attn_sinks': out['attn_sinks'], 'rel_bias_table': out['rel_bias_table'], 'w_branch_ssm': out['w_branch_ssm'], 'w_branch_attn': out['w_branch_attn'], 'w_out': out['w_out'], 'ln_gain': out['ln_gain'], 'ln_bias': out['ln_bias'], 'loss_target': out['loss_target'], 'm_w_in': out['m_w_in'], 'm_ssm_lambda_re': out['m_ssm_lambda_re'], 'm_ssm_lambda_im': out['m_ssm_lambda_im'], 'm_ssm_b_re': out['m_ssm_b_re'], 'm_ssm_b_im': out['m_ssm_b_im'], 'm_ssm_c_re': out['m_ssm_c_re'], 'm_ssm_c_im': out['m_ssm_c_im'], 'm_ssm_d': out['m_ssm_d'], 'm_ssm_log_step': out['m_ssm_log_step'], 'm_w_glu': out['m_w_glu'], 'm_attn_sinks': out['m_attn_sinks'], 'm_rel_bias_table': out['m_rel_bias_table'], 'm_w_branch_ssm': out['m_w_branch_ssm'], 'm_w_branch_attn': out['m_w_branch_attn'], 'm_w_out': out['m_w_out'], 'm_ln_gain': out['m_ln_gain'], 'm_ln_bias': out['m_ln_bias'], 'v_w_in': out['v_w_in'], 'v_ssm_lambda_re': out['v_ssm_lambda_re'], 'v_ssm_lambda_im': out['v_ssm_lambda_im'], 'v_ssm_b_re': out['v_ssm_b_re'], 'v_ssm_b_im': out['v_ssm_b_im'], 'v_ssm_c_re': out['v_ssm_c_re'], 'v_ssm_c_im': out['v_ssm_c_im'], 'v_ssm_d': out['v_ssm_d'], 'v_ssm_log_step': out['v_ssm_log_step'], 'v_w_glu': out['v_w_glu'], 'v_attn_sinks': out['v_attn_sinks'], 'v_rel_bias_table': out['v_rel_bias_table'], 'v_w_branch_ssm': out['v_w_branch_ssm'], 'v_w_branch_attn': out['v_w_branch_attn'], 'v_w_out': out['v_w_out'], 'v_ln_gain': out['v_ln_gain'], 'v_ln_bias': out['v_ln_bias']}


def _loss(weights, diff, rest, loss_target):
    with _jax.named_scope("forward"):
        args = {**rest, TWIN_DIFF_INPUT: diff, **{k: w.astype(_WEIGHT_DTYPES[k]) for k, w in weights.items()}}
        y = _forward(args)
    with _jax.named_scope("loss_head"):
        err = _jnp.square(y.astype(_jnp.float32) - loss_target)
        return 0.5 * _jnp.sum(_jnp.mean(err, axis=-1)) if err.ndim else 0.5 * err


def _adamw(w, g, m, v):
    m = ADAM_B1 * m + (1.0 - ADAM_B1) * g
    v = ADAM_B2 * v + (1.0 - ADAM_B2) * _jnp.square(g)
    m_hat = m / (1.0 - ADAM_B1 ** ADAM_STEP)
    v_hat = v / (1.0 - ADAM_B2 ** ADAM_STEP)
    delta = -ADAM_LR * (m_hat / (_jnp.sqrt(v_hat) + ADAM_EPS) + ADAM_WD * w)
    return delta, m, v


def reference(x, w_in, ssm_lambda_re, ssm_lambda_im, ssm_b_re, ssm_b_im, ssm_c_re, ssm_c_im, ssm_d, ssm_log_step, w_glu, attn_sinks, rel_bias_table, w_branch_ssm, w_branch_attn, w_out, ln_gain, ln_bias, loss_target, m_w_in, m_ssm_lambda_re, m_ssm_lambda_im, m_ssm_b_re, m_ssm_b_im, m_ssm_c_re, m_ssm_c_im, m_ssm_d, m_ssm_log_step, m_w_glu, m_attn_sinks, m_rel_bias_table, m_w_branch_ssm, m_w_branch_attn, m_w_out, m_ln_gain, m_ln_bias, v_w_in, v_ssm_lambda_re, v_ssm_lambda_im, v_ssm_b_re, v_ssm_b_im, v_ssm_c_re, v_ssm_c_im, v_ssm_d, v_ssm_log_step, v_w_glu, v_attn_sinks, v_rel_bias_table, v_w_branch_ssm, v_w_branch_attn, v_w_out, v_ln_gain, v_ln_bias):
    given = dict(x=x, w_in=w_in, ssm_lambda_re=ssm_lambda_re, ssm_lambda_im=ssm_lambda_im, ssm_b_re=ssm_b_re, ssm_b_im=ssm_b_im, ssm_c_re=ssm_c_re, ssm_c_im=ssm_c_im, ssm_d=ssm_d, ssm_log_step=ssm_log_step, w_glu=w_glu, attn_sinks=attn_sinks, rel_bias_table=rel_bias_table, w_branch_ssm=w_branch_ssm, w_branch_attn=w_branch_attn, w_out=w_out, ln_gain=ln_gain, ln_bias=ln_bias, loss_target=loss_target, m_w_in=m_w_in, m_ssm_lambda_re=m_ssm_lambda_re, m_ssm_lambda_im=m_ssm_lambda_im, m_ssm_b_re=m_ssm_b_re, m_ssm_b_im=m_ssm_b_im, m_ssm_c_re=m_ssm_c_re, m_ssm_c_im=m_ssm_c_im, m_ssm_d=m_ssm_d, m_ssm_log_step=m_ssm_log_step, m_w_glu=m_w_glu, m_attn_sinks=m_attn_sinks, m_rel_bias_table=m_rel_bias_table, m_w_branch_ssm=m_w_branch_ssm, m_w_branch_attn=m_w_branch_attn, m_w_out=m_w_out, m_ln_gain=m_ln_gain, m_ln_bias=m_ln_bias, v_w_in=v_w_in, v_ssm_lambda_re=v_ssm_lambda_re, v_ssm_lambda_im=v_ssm_lambda_im, v_ssm_b_re=v_ssm_b_re, v_ssm_b_im=v_ssm_b_im, v_ssm_c_re=v_ssm_c_re, v_ssm_c_im=v_ssm_c_im, v_ssm_d=v_ssm_d, v_ssm_log_step=v_ssm_log_step, v_w_glu=v_w_glu, v_attn_sinks=v_attn_sinks, v_rel_bias_table=v_rel_bias_table, v_w_branch_ssm=v_w_branch_ssm, v_w_branch_attn=v_w_branch_attn, v_w_out=v_w_out, v_ln_gain=v_ln_gain, v_ln_bias=v_ln_bias)
    weights = {n: given[n] for n in TWIN_WEIGHTS}
    shared = {n: given[n] for n in SHARED_INPUTS}
    per_example = {n: given[n] for n in ['x']}
    grad_fn = _jax.value_and_grad(_loss, argnums=(0, 1))

    def one_microbatch(ex, loss_target):
        ex = dict(ex)
        diff = ex.pop(TWIN_DIFF_INPUT)
        return grad_fn(weights, diff, {**shared, **ex}, loss_target)

    if N_MICROBATCH == 1:
        loss, (grad_w, grad_x) = one_microbatch(per_example, given["loss_target"])
    else:
        def body(carry, xs):
            loss_sum, grad_sum = carry
            l_k, (gw_k, gx_k) = one_microbatch(xs[0], xs[1])
            with _jax.named_scope("update"):
                return (loss_sum + l_k, _jax.tree.map(_jnp.add, grad_sum, gw_k)), gx_k

        init = (_jnp.zeros((), _jnp.float32), _jax.tree.map(_jnp.zeros_like, weights))
        (loss, grad_w), grad_x = _jax.lax.scan(body, init, (per_example, given["loss_target"]))
    with _jax.named_scope("update"):
        delta_w, new_m, new_v = {}, {}, {}
        for n in TWIN_WEIGHTS:
            delta_w[n], new_m[n], new_v[n] = _adamw(weights[n], grad_w[n], given["m_" + n], given["v_" + n])
    return (loss, grad_x, *[grad_w[n] for n in TWIN_WEIGHTS], *[delta_w[n] for n in TWIN_WEIGHTS],
            *[new_m[n] for n in TWIN_WEIGHTS], *[new_v[n] for n in TWIN_WEIGHTS])
```

```python
import functools
import math

import numpy as np
import jax
import jax.numpy as jnp
from jax import lax
from jax.experimental import pallas as pl
from jax.experimental.pallas import tpu as pltpu

F32 = jnp.float32
BF16 = jnp.bfloat16

D_MODEL = 2048
D_SSM = 1024
SSM_GROUP = 16
N_GROUPS = 64
SSM_STATE = 64
N_LANES = N_GROUPS * SSM_STATE
N_Q_HEADS = 16
N_KV_HEADS = 4
HEAD_DIM = 64
Q_PER_KV = 4
D_ATTN = 1024
D_KV = 256
WINDOW = 128
BLOCK = 128
N_BUCKETS = 32
MAX_DISTANCE = 128
D_IN = 8704
ALPHA = 2.0 ** 0.25
LN_EPS = 1e-5
NEG_INF = -1e30
ATTN_SCALE = HEAD_DIM ** -0.5

C_U, C_ZS, C_Q, C_K, C_V, C_ZA, C_GL = 0, 1024, 2048, 3072, 3328, 3584, 4608

ADAM_LR = 0.001
ADAM_B1 = 0.9
ADAM_B2 = 0.999
ADAM_EPS = 1e-08
ADAM_WD = 0.01
ADAM_STEP = 10

N_CHIPS = 4
MESH = pl.DeviceIdType.MESH

SSM_CHUNK = 256
SEG_LEN = SSM_CHUNK // 8
SLAB_LANES = 512
N_SLABS = N_LANES // SLAB_LANES
SLAB_CH = D_SSM // N_SLABS

VMEM_LIMIT = 60 * 1024 * 1024


def _params(sem=None, **kw):
    return pltpu.CompilerParams(dimension_semantics=sem, vmem_limit_bytes=VMEM_LIMIT, **kw)


_DIMS = {"nn": (((1,), (0,)), ((), ())), "nt": (((1,), (1,)), ((), ())), "tn": (((0,), (0,)), ((), ()))}


def _mm_body(*refs, dims, nk, res_scale):
    if res_scale is None:
        a_ref, b_ref, o_ref, acc_ref = refs
        r_ref = None
    else:
        a_ref, b_ref, r_ref, o_ref, acc_ref = refs
    k = pl.program_id(2)
    part = lax.dot_general(a_ref[...].astype(BF16), b_ref[...].astype(BF16), _DIMS[dims],
                           preferred_element_type=F32)

    def finish(acc):
        if r_ref is not None:
            acc = acc + res_scale * r_ref[...]
        o_ref[...] = acc.astype(o_ref.dtype)

    if nk == 1:
        finish(part)
    else:
        @pl.when(k == 0)
        def _():
            acc_ref[...] = part

        @pl.when(k > 0)
        def _():
            acc_ref[...] += part

        @pl.when(k == nk - 1)
        def _():
            finish(acc_ref[...])


def _matmul(a, b, dims, *, name, out_dtype=F32, tm=512, tn=512, tk=None, res=None, res_scale=None):
    if dims == "nn":
        (m, kk), n = a.shape, b.shape[1]
    elif dims == "nt":
        (m, kk), n = a.shape, b.shape[0]
    else:
        (kk, m), n = a.shape, b.shape[1]
    tm, tn = min(tm, m), min(tn, n)
    tk = kk if tk is None else min(tk, kk)
    assert m % tm == 0 and n % tn == 0 and kk % tk == 0, (name, m, n, kk, tm, tn, tk)
    nk = kk // tk
    a_spec = {"nn": pl.BlockSpec((tm, tk), lambda i, j, k: (i, k)),
              "nt": pl.BlockSpec((tm, tk), lambda i, j, k: (i, k)),
              "tn": pl.BlockSpec((tk, tm), lambda i, j, k: (k, i))}[dims]
    b_spec = {"nn": pl.BlockSpec((tk, tn), lambda i, j, k: (k, j)),
              "nt": pl.BlockSpec((tn, tk), lambda i, j, k: (j, k)),
              "tn": pl.BlockSpec((tk, tn), lambda i, j, k: (k, j))}[dims]
    in_specs, args = [a_spec, b_spec], [a, b]
    if res is not None:
        in_specs.append(pl.BlockSpec((tm, tn), lambda i, j, k: (i, j)))
        args.append(res)
    return pl.pallas_call(
        functools.partial(_mm_body, dims=dims, nk=nk, res_scale=res_scale if res is not None else None),
        name=name,
        grid=(m // tm, n // tn, nk),
        in_specs=in_specs,
        out_specs=pl.BlockSpec((tm, tn), lambda i, j, k: (i, j)),
        out_shape=jax.ShapeDtypeStruct((m, n), out_dtype),
        scratch_shapes=[pltpu.VMEM((tm, tn), F32)],
        compiler_params=_params(("parallel", "parallel", "arbitrary")),
    )(*args)


def _sigmoid(v):
    return 1.0 / (1.0 + jnp.exp(-v))


def _silu_and_grad(z):
    s = _sigmoid(z)
    return z * s, s * (1.0 + z * (1.0 - s))


def _cast_body(x_ref, o_ref):
    o_ref[...] = x_ref[...].astype(o_ref.dtype)


def _cast(x, dtype, *, name, rows=512):
    m, n = x.shape
    rows = min(rows, m)
    return pl.pallas_call(
        functools.partial(_cast_body), name=name, grid=(m // rows,),
        in_specs=[pl.BlockSpec((rows, n), lambda i: (i, 0))],
        out_specs=pl.BlockSpec((rows, n), lambda i: (i, 0)),
        out_shape=jax.ShapeDtypeStruct((m, n), dtype),
        compiler_params=_params(("parallel",)),
    )(x)


def _lam_bar(lr, li, ls):
    step = jnp.exp(ls)
    er = jnp.exp(lr * step)
    return step, er * jnp.cos(li * step), er * jnp.sin(li * step)


def _ssm_pow_body(lr_ref, li_ref, ls_ref, pr_ref, pi_ref):
    _, ar, ai = _lam_bar(lr_ref[...], li_ref[...], ls_ref[...])
    cr, ci = ar, ai
    for i in range(SEG_LEN):
        pr_ref[pl.ds(i, 1), :] = cr
        pi_ref[pl.ds(i, 1), :] = ci
        cr, ci = cr * ar - ci * ai, cr * ai + ci * ar


def _ssm_bbar_body(lr_ref, li_ref, ls_ref, br_ref, bi_ref, or_ref, oi_ref):
    lr, li = lr_ref[...], li_ref[...]
    _, ar, ai = _lam_bar(lr, li, ls_ref[...])
    d = lr * lr + li * li
    ir, ii = lr / d, -li / d
    nr, ni = ar - 1.0, ai
    cr, ci = nr * ir - ni * ii, nr * ii + ni * ir
    br, bi = br_ref[...], bi_ref[...]
    or_ref[...] = cr * br - ci * bi
    oi_ref[...] = cr * bi + ci * br


def _ssm_param_bwd_body(lr_ref, li_ref, ls_ref, br_ref, bi_ref, glr_ref, gli_ref, gbr_ref, gbi_ref,
                        dlr_ref, dli_ref, dls_ref, dbr_ref, dbi_ref):
    lr, li = lr_ref[...], li_ref[...]
    step, ar, ai = _lam_bar(lr, li, ls_ref[...])
    d = lr * lr + li * li
    ir, ii = lr / d, -li / d
    nr, ni = ar - 1.0, ai
    cr, ci = nr * ir - ni * ii, nr * ii + ni * ir
    br, bi, gbr, gbi = br_ref[...], bi_ref[...], gbr_ref[...], gbi_ref[...]
    dbr_ref[...] = cr * gbr + ci * gbi
    dbi_ref[...] = cr * gbi - ci * gbr
    gcr = jnp.sum(br * gbr + bi * gbi, axis=1, keepdims=True)
    gci = jnp.sum(br * gbi - bi * gbr, axis=1, keepdims=True)
    gnr, gni = ir * gcr + ii * gci, ir * gci - ii * gcr
    gir, gii = nr * gcr + ni * gci, nr * gci - ni * gcr
    gtr, gti = glr_ref[...] + gnr, gli_ref[...] + gni
    i2r, i2i = ir * ir - ii * ii, 2.0 * ir * ii
    g1r, g1i = -(i2r * gir + i2i * gii), -(i2r * gii - i2i * gir)
    g2r, g2i = step * (ar * gtr + ai * gti), step * (ar * gti - ai * gtr)
    mr, mi = lr * ar - li * ai, lr * ai + li * ar
    dlr_ref[...] = g1r + g2r
    dli_ref[...] = g1i + g2i
    dls_ref[...] = (mr * gtr + mi * gti) * step


def _whole(shape):
    return pl.BlockSpec(shape, lambda *_: (0,) * len(shape))


def _ssm_prepare(lam_re, lam_im, b_re, b_im, c_re, c_im, log_step):
    row = lambda a: a.reshape(1, N_LANES)
    col = lambda a: a.reshape(N_LANES, 1)
    ls = jnp.repeat(log_step.reshape(N_GROUPS), SSM_STATE)
    p_re, p_im = pl.pallas_call(
        functools.partial(_ssm_pow_body), name="ssm_pow",
        in_specs=[_whole((1, N_LANES))] * 3, out_specs=[_whole((SEG_LEN, N_LANES))] * 2,
        out_shape=[jax.ShapeDtypeStruct((SEG_LEN, N_LANES), F32)] * 2, grid=(1,),
    )(row(lam_re), row(lam_im), row(ls))
    bb_re, bb_im = pl.pallas_call(
        functools.partial(_ssm_bbar_body), name="ssm_bbar",
        in_specs=[_whole((N_LANES, 1))] * 3 + [_whole((N_LANES, SSM_GROUP))] * 2,
        out_specs=[_whole((N_LANES, SSM_GROUP))] * 2,
        out_shape=[jax.ShapeDtypeStruct((N_LANES, SSM_GROUP), F32)] * 2, grid=(1,),
    )(col(lam_re), col(lam_im), col(ls), b_re.reshape(N_LANES, SSM_GROUP), b_im.reshape(N_LANES, SSM_GROUP))
    eye = jnp.eye(8, dtype=F32)

    def b_slabs(bb):
        t = bb.reshape(N_SLABS, 8, SSM_STATE, SSM_GROUP).transpose(0, 1, 3, 2)
        return (t[:, :, :, None, :] * eye[None, :, None, :, None]).reshape(N_SLABS, SLAB_CH, SLAB_LANES)

    def c_slabs(c):
        t = c.reshape(N_SLABS, 8, SSM_GROUP, SSM_STATE).transpose(0, 1, 3, 2)
        return (t[:, :, :, None, :] * eye[None, :, None, :, None]).reshape(N_SLABS, SLAB_LANES, SLAB_CH)

    b_bd = jnp.concatenate([b_slabs(bb_re), b_slabs(bb_im)], axis=2).astype(BF16)
    c_bd = jnp.concatenate([c_slabs(c_re.reshape(N_GROUPS, SSM_GROUP, SSM_STATE)),
                            -c_slabs(c_im.reshape(N_GROUPS, SSM_GROUP, SSM_STATE))], axis=1).astype(BF16)
    return p_re, p_im, b_bd, c_bd


def _diag_blocks_b(g):
    t = g.reshape(N_SLABS, 8, SSM_GROUP, 8, SSM_STATE)
    t = jnp.stack([t[:, i, :, i, :] for i in range(8)], axis=1)
    return t.transpose(0, 1, 3, 2).reshape(N_LANES, SSM_GROUP)


def _diag_blocks_c(g):
    t = g.reshape(N_SLABS, 8, SSM_STATE, 8, SSM_GROUP)
    t = jnp.stack([t[:, i, :, i, :] for i in range(8)], axis=1)
    return t.transpose(0, 1, 3, 2).reshape(N_GROUPS, SSM_GROUP, SSM_STATE)


def _ssm_param_grads(lam_re, lam_im, b_re, b_im, log_step, g_lam_re, g_lam_im, g_bbd, g_cbd):
    col = lambda a: a.reshape(N_LANES, 1)
    ls = jnp.repeat(log_step.reshape(N_GROUPS), SSM_STATE)
    gbr = _diag_blocks_b(g_bbd[:, :, :SLAB_LANES])
    gbi = _diag_blocks_b(g_bbd[:, :, SLAB_LANES:])
    outs = pl.pallas_call(
        functools.partial(_ssm_param_bwd_body), name="ssm_param_bwd", grid=(1,),
        in_specs=[_whole((N_LANES, 1))] * 3 + [_whole((N_LANES, SSM_GROUP))] * 2 + [_whole((N_LANES, 1))] * 2
        + [_whole((N_LANES, SSM_GROUP))] * 2,
        out_specs=[_whole((N_LANES, 1))] * 3 + [_whole((N_LANES, SSM_GROUP))] * 2,
        out_shape=[jax.ShapeDtypeStruct((N_LANES, 1), F32)] * 3 + [jax.ShapeDtypeStruct((N_LANES, SSM_GROUP), F32)] * 2,
    )(col(lam_re), col(lam_im), col(ls), b_re.reshape(N_LANES, SSM_GROUP), b_im.reshape(N_LANES, SSM_GROUP),
      col(g_lam_re), col(g_lam_im), gbr, gbi)
    dlr, dli, dls, dbr, dbi = outs
    d_c_re = _diag_blocks_c(g_cbd[:, :SLAB_LANES, :])
    d_c_im = -_diag_blocks_c(g_cbd[:, SLAB_LANES:, :])
    return (dlr.reshape(1, N_GROUPS, SSM_STATE), dli.reshape(1, N_GROUPS, SSM_STATE),
            dbr.reshape(1, N_GROUPS, SSM_STATE, SSM_GROUP), dbi.reshape(1, N_GROUPS, SSM_STATE, SSM_GROUP),
            d_c_re[None], d_c_im[None], dls.reshape(N_GROUPS, SSM_STATE).sum(axis=1).reshape(1, N_GROUPS))


def _bcast8(v):
    return jnp.broadcast_to(v, (8, v.shape[1]))


SLAB_BLOCKS = SLAB_LANES // 128


def _scan_buffer():
    return pltpu.VMEM((N_LANES // 128, SSM_CHUNK, 128), F32)


def _seg_get(ref, k, i):
    return jnp.concatenate([ref.at[SLAB_BLOCKS * k + j][pl.ds(i, 8, stride=SEG_LEN), :]
                            for j in range(SLAB_BLOCKS)], axis=1)


def _seg_put(ref, k, i, val):
    for j in range(SLAB_BLOCKS):
        ref.at[SLAB_BLOCKS * k + j][pl.ds(i, 8, stride=SEG_LEN), :] = val[:, 128 * j:128 * (j + 1)]


def _slab_get(ref, k):
    return jnp.concatenate([ref[SLAB_BLOCKS * k + j] for j in range(SLAB_BLOCKS)], axis=1)


def _slab_put(ref, k, val):
    for j in range(SLAB_BLOCKS):
        ref[SLAB_BLOCKS * k + j] = val[:, 128 * j:128 * (j + 1)]


def _scan_forward(s_re, s_im, p_re, p_im, car_re, car_im, sp_re=None, sp_im=None):
    for k in range(N_SLABS):
        ln = pl.ds(k * SLAB_LANES, SLAB_LANES)
        ar, ai = _bcast8(p_re[pl.ds(0, 1), ln]), _bcast8(p_im[pl.ds(0, 1), ln])

        def step(i, s, k=k, ar=ar, ai=ai):
            sr, si = s
            nr = ar * sr - ai * si + _seg_get(s_re, k, i)
            ni = ar * si + ai * sr + _seg_get(s_im, k, i)
            _seg_put(s_re, k, i, nr)
            _seg_put(s_im, k, i, ni)
            return nr, ni

        zero = jnp.zeros((8, SLAB_LANES), F32)
        er, ei = lax.fori_loop(0, SEG_LEN, step, (zero, zero))
        lr, li = p_re[pl.ds(SEG_LEN - 1, 1), ln], p_im[pl.ds(SEG_LEN - 1, 1), ln]
        cr, ci = car_re[pl.ds(0, 1), ln], car_im[pl.ds(0, 1), ln]
        rows_r, rows_i = [], []
        for r in range(8):
            rows_r.append(cr)
            rows_i.append(ci)
            cr, ci = er[r:r + 1] + lr * cr - li * ci, ei[r:r + 1] + lr * ci + li * cr
        pr8, pi8 = jnp.concatenate(rows_r, axis=0), jnp.concatenate(rows_i, axis=0)
        car_re[:, ln] = _bcast8(cr)
        car_im[:, ln] = _bcast8(ci)
        if sp_re is not None:
            sp_re[:, ln] = pr8
            sp_im[:, ln] = pi8

        def fix(i, _, k=k, ln=ln, pr8=pr8, pi8=pi8):
            qr, qi = _bcast8(p_re[pl.ds(i, 1), ln]), _bcast8(p_im[pl.ds(i, 1), ln])
            _seg_put(s_re, k, i, _seg_get(s_re, k, i) + qr * pr8 - qi * pi8)
            _seg_put(s_im, k, i, _seg_get(s_im, k, i) + qr * pi8 + qi * pr8)
            return 0

        lax.fori_loop(0, SEG_LEN, fix, 0)


def _scan_backward(g_re, g_im, s_re, s_im, sp_re, sp_im, p_re, p_im, car_re, car_im, acc_re, acc_im):
    for k in range(N_SLABS):
        ln = pl.ds(k * SLAB_LANES, SLAB_LANES)
        ar, ai = _bcast8(p_re[pl.ds(0, 1), ln]), -_bcast8(p_im[pl.ds(0, 1), ln])

        def step(j, s, k=k, ar=ar, ai=ai):
            i = SEG_LEN - 1 - j
            sr, si = s
            nr = ar * sr - ai * si + _seg_get(g_re, k, i)
            ni = ar * si + ai * sr + _seg_get(g_im, k, i)
            _seg_put(g_re, k, i, nr)
            _seg_put(g_im, k, i, ni)
            return nr, ni

        zero = jnp.zeros((8, SLAB_LANES), F32)
        er, ei = lax.fori_loop(0, SEG_LEN, step, (zero, zero))
        lr, li = p_re[pl.ds(SEG_LEN - 1, 1), ln], -p_im[pl.ds(SEG_LEN - 1, 1), ln]
        cr, ci = car_re[pl.ds(0, 1), ln], car_im[pl.ds(0, 1), ln]
        rows_r, rows_i = [None] * 8, [None] * 8
        for r in range(7, -1, -1):
            rows_r[r], rows_i[r] = cr, ci
            cr, ci = er[r:r + 1] + lr * cr - li * ci, ei[r:r + 1] + lr * ci + li * cr
        nr8, ni8 = jnp.concatenate(rows_r, axis=0), jnp.concatenate(rows_i, axis=0)
        car_re[:, ln] = _bcast8(cr)
        car_im[:, ln] = _bcast8(ci)

        def fix(i, acc, k=k, ln=ln, nr8=nr8, ni8=ni8):
            qr = _bcast8(p_re[pl.ds(SEG_LEN - 1 - i, 1), ln])
            qi = -_bcast8(p_im[pl.ds(SEG_LEN - 1 - i, 1), ln])
            gr = _seg_get(g_re, k, i) + qr * nr8 - qi * ni8
            gi = _seg_get(g_im, k, i) + qr * ni8 + qi * nr8
            _seg_put(g_re, k, i, gr)
            _seg_put(g_im, k, i, gi)
            return gr, gi

        def prod(xr, xi, gr, gi):
            return xr * gr + xi * gi, xr * gi - xi * gr

        gr, gi = fix(0, None)
        a_r, a_i = prod(sp_re[:, ln], sp_im[:, ln], gr, gi)

        def fix_acc(i, acc, k=k, fix=fix):
            gr, gi = fix(i, None)
            dr, di = prod(_seg_get(s_re, k, i - 1), _seg_get(s_im, k, i - 1), gr, gi)
            return acc[0] + dr, acc[1] + di

        a_r, a_i = lax.fori_loop(1, SEG_LEN, fix_acc, (a_r, a_i))
        acc_re[:, ln] += a_r
        acc_im[:, ln] += a_i


def _gelu_and_grad(y):
    cdf = 0.5 * (1.0 + lax.erf(y * (2.0 ** -0.5)))
    pdf = jnp.exp(-0.5 * y * y) * (1.0 / math.sqrt(2.0 * math.pi))
    return y * cdf, cdf + y * pdf


def _ssm_fwd_body(u_ref, bbd_ref, cbd_ref, pre_ref, pim_ref, d_ref,
                  y_ref, gin_ref, cre_out, cim_out, s_re, s_im, car_re, car_im):
    c = pl.program_id(0)

    @pl.when(c == 0)
    def _():
        car_re[...] = jnp.zeros_like(car_re)
        car_im[...] = jnp.zeros_like(car_im)

    cre_out[...] = car_re[...]
    cim_out[...] = car_im[...]
    for k in range(N_SLABS):
        uk = u_ref[:, pl.ds(k * SLAB_CH, SLAB_CH)].astype(BF16)
        bu = jnp.dot(uk, bbd_ref[k], preferred_element_type=F32)
        _slab_put(s_re, k, bu[:, :SLAB_LANES])
        _slab_put(s_im, k, bu[:, SLAB_LANES:])
    _scan_forward(s_re, s_im, pre_ref, pim_ref, car_re, car_im)
    for k in range(N_SLABS):
        ch = pl.ds(k * SLAB_CH, SLAB_CH)
        y = (jnp.dot(_slab_get(s_re, k).astype(BF16), cbd_ref[k, :SLAB_LANES, :], preferred_element_type=F32)
             + jnp.dot(_slab_get(s_im, k).astype(BF16), cbd_ref[k, SLAB_LANES:, :], preferred_element_type=F32)
             + d_ref[:, ch] * u_ref[:, ch])
        y_ref[:, ch] = y
        gin_ref[:, ch] = _gelu_and_grad(y)[0].astype(BF16)


def _ssm_forward(proj, p_re, p_im, b_bd, c_bd, d_skip):
    t = proj.shape[0]
    nc = t // SSM_CHUNK
    return pl.pallas_call(
        functools.partial(_ssm_fwd_body), name="ssm_fwd", grid=(nc,),
        in_specs=[pl.BlockSpec((SSM_CHUNK, D_SSM), lambda c: (c, C_U // D_SSM)),
                  _whole(b_bd.shape), _whole(c_bd.shape), _whole(p_re.shape), _whole(p_im.shape),
                  _whole((1, D_SSM))],
        out_specs=[pl.BlockSpec((SSM_CHUNK, D_SSM), lambda c: (c, 0)),
                   pl.BlockSpec((SSM_CHUNK, D_SSM), lambda c: (c, 0)),
                   pl.BlockSpec((None, 8, N_LANES), lambda c: (c, 0, 0)),
                   pl.BlockSpec((None, 8, N_LANES), lambda c: (c, 0, 0))],
        out_shape=[jax.ShapeDtypeStruct((t, D_SSM), F32), jax.ShapeDtypeStruct((t, D_SSM), BF16),
                   jax.ShapeDtypeStruct((nc, 8, N_LANES), F32), jax.ShapeDtypeStruct((nc, 8, N_LANES), F32)],
        scratch_shapes=[_scan_buffer(), _scan_buffer(),
                        pltpu.VMEM((8, N_LANES), F32), pltpu.VMEM((8, N_LANES), F32)],
        compiler_params=_params(("arbitrary",)),
    )(proj, b_bd, c_bd, p_re, p_im, d_skip)


def _ssm_bwd_body(u_ref, dgin_ref, y_ref, cre_in, cim_in, bbd_ref, cbd_ref, pre_ref, pim_ref, d_ref, dproj_in,
                  du_ref, gb_ref, gc_ref, glr_ref, gli_ref, gd_ref,
                  s_re, s_im, g_re, g_im, sp_re, sp_im, car_re, car_im, gcar_re, gcar_im, acc_re, acc_im):
    del dproj_in
    c = pl.program_id(0)
    nc = pl.num_programs(0)

    @pl.when(c == 0)
    def _():
        gcar_re[...] = jnp.zeros_like(gcar_re)
        gcar_im[...] = jnp.zeros_like(gcar_im)
        acc_re[...] = jnp.zeros_like(acc_re)
        acc_im[...] = jnp.zeros_like(acc_im)
        gb_ref[...] = jnp.zeros_like(gb_ref)
        gc_ref[...] = jnp.zeros_like(gc_ref)
        gd_ref[...] = jnp.zeros_like(gd_ref)

    car_re[...] = cre_in[...]
    car_im[...] = cim_in[...]
    dy = dgin_ref[...] * _gelu_and_grad(y_ref[...])[1]
    gd_ref[...] += jnp.sum(dy * u_ref[...], axis=0, keepdims=True)
    for k in range(N_SLABS):
        ln = pl.ds(k * SLAB_LANES, SLAB_LANES)
        ch = slice(k * SLAB_CH, (k + 1) * SLAB_CH)
        uk = u_ref[:, ch].astype(BF16)
        bu = jnp.dot(uk, bbd_ref[k], preferred_element_type=F32)
        _slab_put(s_re, k, bu[:, :SLAB_LANES])
        _slab_put(s_im, k, bu[:, SLAB_LANES:])
        ds = lax.dot_general(dy[:, ch].astype(BF16), cbd_ref[k], _DIMS["nt"], preferred_element_type=F32)
        _slab_put(g_re, k, ds[:, :SLAB_LANES])
        _slab_put(g_im, k, ds[:, SLAB_LANES:])
    _scan_forward(s_re, s_im, pre_ref, pim_ref, car_re, car_im, sp_re, sp_im)
    _scan_backward(g_re, g_im, s_re, s_im, sp_re, sp_im, pre_ref, pim_ref, gcar_re, gcar_im, acc_re, acc_im)
    for k in range(N_SLABS):
        ln = pl.ds(k * SLAB_LANES, SLAB_LANES)
        ch = slice(k * SLAB_CH, (k + 1) * SLAB_CH)
        uk = u_ref[:, ch].astype(BF16)
        dyk = dy[:, ch].astype(BF16)
        sr, si = _slab_get(s_re, k).astype(BF16), _slab_get(s_im, k).astype(BF16)
        gr, gi = _slab_get(g_re, k).astype(BF16), _slab_get(g_im, k).astype(BF16)
        gc_ref[k, :SLAB_LANES, :] += lax.dot_general(sr, dyk, _DIMS["tn"], preferred_element_type=F32)
        gc_ref[k, SLAB_LANES:, :] += lax.dot_general(si, dyk, _DIMS["tn"], preferred_element_type=F32)
        gb_ref[k, :, :SLAB_LANES] += lax.dot_general(uk, gr, _DIMS["tn"], preferred_element_type=F32)
        gb_ref[k, :, SLAB_LANES:] += lax.dot_general(uk, gi, _DIMS["tn"], preferred_element_type=F32)
        du = (lax.dot_general(gr, bbd_ref[k, :, :SLAB_LANES], _DIMS["nt"], preferred_element_type=F32)
              + lax.dot_general(gi, bbd_ref[k, :, SLAB_LANES:], _DIMS["nt"], preferred_element_type=F32)
              + d_ref[:, ch] * dy[:, ch])
        du_ref[:, ch] = du.astype(du_ref.dtype)

    @pl.when(c == nc - 1)
    def _():
        glr_ref[...] = jnp.sum(acc_re[...], axis=0, keepdims=True)
        gli_ref[...] = jnp.sum(acc_im[...], axis=0, keepdims=True)


def _ssm_backward(proj, dg_in, y_ssm, car_re, car_im, p_re, p_im, b_bd, c_bd, d_skip, dproj):
    t = proj.shape[0]
    nc = t // SSM_CHUNK
    rev = lambda c: nc - 1 - c
    big = _scan_buffer
    small = lambda: pltpu.VMEM((8, N_LANES), F32)
    outs = pl.pallas_call(
        functools.partial(_ssm_bwd_body), name="ssm_bwd", grid=(nc,),
        in_specs=[pl.BlockSpec((SSM_CHUNK, D_SSM), lambda c: (rev(c), C_U // D_SSM)),
                  pl.BlockSpec((SSM_CHUNK, D_SSM), lambda c: (rev(c), 0)),
                  pl.BlockSpec((SSM_CHUNK, D_SSM), lambda c: (rev(c), 0)),
                  pl.BlockSpec((None, 8, N_LANES), lambda c: (rev(c), 0, 0)),
                  pl.BlockSpec((None, 8, N_LANES), lambda c: (rev(c), 0, 0)),
                  _whole(b_bd.shape), _whole(c_bd.shape), _whole(p_re.shape), _whole(p_im.shape),
                  _whole((1, D_SSM)), pl.BlockSpec(memory_space=pl.ANY)],
        out_specs=[pl.BlockSpec((SSM_CHUNK, D_SSM), lambda c: (rev(c), C_U // D_SSM)),
                   _whole(b_bd.shape), _whole(c_bd.shape), _whole((1, N_LANES)), _whole((1, N_LANES)),
                   _whole((1, D_SSM))],
        out_shape=[jax.ShapeDtypeStruct(dproj.shape, dproj.dtype),
                   jax.ShapeDtypeStruct(b_bd.shape, F32), jax.ShapeDtypeStruct(c_bd.shape, F32),
                   jax.ShapeDtypeStruct((1, N_LANES), F32), jax.ShapeDtypeStruct((1, N_LANES), F32),
                   jax.ShapeDtypeStruct((1, D_SSM), F32)],
        scratch_shapes=[big(), big(), big(), big()] + [small() for _ in range(8)],
        input_output_aliases={10: 0},
        compiler_params=_params(("arbitrary",)),
    )(proj, dg_in, y_ssm, car_re, car_im, b_bd, c_bd, p_re, p_im, d_skip, dproj)
    return outs


def _bucket_table():
    i = np.arange(BLOCK)[:, None]
    j = np.arange(2 * BLOCK)[None, :]
    dist = BLOCK + i - j
    ok = (dist >= 0) & (dist < WINDOW)
    max_exact = N_BUCKETS // 2
    d = np.maximum(dist, 1).astype(np.float32)
    large = max_exact + (np.log(d / max_exact) / math.log(MAX_DISTANCE / max_exact)
                         * (N_BUCKETS - max_exact)).astype(np.int32)
    large = np.minimum(large, N_BUCKETS - 1)
    bucket = np.where(dist < max_exact, dist, large)
    return np.where(ok, bucket, -1).astype(np.int32)


def _bias_build_body(table_ref, bucket_ref, o_ref):
    h = pl.program_id(0)
    bucket = bucket_ref[...]
    acc = jnp.full(bucket.shape, NEG_INF, F32)
    for b in range(N_BUCKETS):
        acc = jnp.where(bucket == b, table_ref[b, h], acc)
    o_ref[...] = acc


def _bias_build(rel_bias_table, bucket):
    return pl.pallas_call(
        functools.partial(_bias_build_body), name="bias_build", grid=(N_Q_HEADS,),
        in_specs=[pl.BlockSpec(memory_space=pltpu.SMEM), _whole(bucket.shape)],
        out_specs=pl.BlockSpec((None, BLOCK, 2 * BLOCK), lambda h: (h, 0, 0)),
        out_shape=jax.ShapeDtypeStruct((N_Q_HEADS, BLOCK, 2 * BLOCK), F32),
        compiler_params=_params(("arbitrary",)),
    )(rel_bias_table, bucket)


def _bias_grad_body(g_ref, bucket_ref, o_ref):
    bucket = bucket_ref[...]
    lane = lax.broadcasted_iota(jnp.int32, (N_BUCKETS, 128), 1)

    def head(h, out):
        g = g_ref[h]
        rows = [jnp.sum(jnp.where(bucket == b, g, 0.0), axis=0, keepdims=True) for b in range(N_BUCKETS)]
        colsum = jnp.sum(jnp.concatenate(rows, axis=0), axis=1, keepdims=True)
        return jnp.where(lane == h, colsum, out)

    o_ref[...] = lax.fori_loop(0, N_Q_HEADS, head, jnp.zeros((N_BUCKETS, 128), F32))


def _bias_grad(g_bias, bucket):
    out = pl.pallas_call(
        functools.partial(_bias_grad_body), name="bias_grad", grid=(1,),
        in_specs=[_whole(g_bias.shape), _whole(bucket.shape)],
        out_specs=_whole((N_BUCKETS, 128)),
        out_shape=jax.ShapeDtypeStruct((N_BUCKETS, 128), F32),
        compiler_params=_params(("arbitrary",)),
    )(g_bias, bucket)
    return out[:, :N_Q_HEADS]


def _head_logits(qh, kk, bias_h, first_block):
    s = lax.dot_general(qh, kk, _DIMS["nt"], preferred_element_type=F32) * ATTN_SCALE + bias_h
    col = lax.broadcasted_iota(jnp.int32, s.shape, 1)
    return jnp.where(jnp.logical_and(first_block, col < BLOCK), NEG_INF, s)


def _attn_fwd_body(sink_ref, q_ref, kp_ref, kc_ref, vp_ref, vc_ref, bias_ref, o_ref, lse_ref):
    n = pl.program_id(0)
    outs, lses = [], []
    for kv in range(N_KV_HEADS):
        cs = slice(kv * HEAD_DIM, (kv + 1) * HEAD_DIM)
        kk = jnp.concatenate([kp_ref[:, cs], kc_ref[:, cs]], axis=0).astype(BF16)
        vv = jnp.concatenate([vp_ref[:, cs], vc_ref[:, cs]], axis=0).astype(BF16)
        for g in range(Q_PER_KV):
            h = kv * Q_PER_KV + g
            qh = q_ref[:, h * HEAD_DIM:(h + 1) * HEAD_DIM].astype(BF16)
            s = _head_logits(qh, kk, bias_ref[h], n == 0)
            sink = sink_ref[0, h]
            m = jnp.maximum(jnp.max(s, axis=1, keepdims=True), sink)
            p = jnp.exp(s - m)
            den = jnp.sum(p, axis=1, keepdims=True) + jnp.exp(sink - m)
            p = p / den
            outs.append(jnp.dot(p.astype(BF16), vv, preferred_element_type=F32))
            lses.append(m + jnp.log(den))
    o_ref[...] = jnp.concatenate(outs, axis=1)
    lse_ref[...] = jnp.concatenate(lses, axis=1)


def _attn_specs(nb):
    prev = lambda n: jnp.maximum(jnp.minimum(n, nb - 1) - 1, 0)
    cur = lambda n: jnp.minimum(n, nb - 1)
    return [pl.BlockSpec((BLOCK, D_ATTN), lambda n: (cur(n), C_Q // D_ATTN)),
            pl.BlockSpec((BLOCK, D_KV), lambda n: (prev(n), C_K // D_KV)),
            pl.BlockSpec((BLOCK, D_KV), lambda n: (cur(n), C_K // D_KV)),
            pl.BlockSpec((BLOCK, D_KV), lambda n: (prev(n), C_V // D_KV)),
            pl.BlockSpec((BLOCK, D_KV), lambda n: (cur(n), C_V // D_KV))]


def _attn_forward(proj, sinks, bias):
    t = proj.shape[0]
    nb = t // BLOCK
    return pl.pallas_call(
        functools.partial(_attn_fwd_body), name="attn_fwd", grid=(nb,),
        in_specs=[pl.BlockSpec(memory_space=pltpu.SMEM)] + _attn_specs(nb) + [_whole(bias.shape)],
        out_specs=[pl.BlockSpec((BLOCK, D_ATTN), lambda n: (n, 0)),
                   pl.BlockSpec((BLOCK, N_Q_HEADS), lambda n: (n, 0))],
        out_shape=[jax.ShapeDtypeStruct((t, D_ATTN), F32), jax.ShapeDtypeStruct((t, N_Q_HEADS), F32)],
        compiler_params=_params(("parallel",)),
    )(sinks, proj, proj, proj, proj, proj, bias)


def _attn_bwd_body(sink_ref, q_ref, kp_ref, kc_ref, vp_ref, vc_ref, bias_ref, do_ref, o_ref, lse_ref, dproj_in,
                   dq_ref, dkv_ref, gbias_ref, gsink_ref, carry_ref, *, nb):
    del dproj_in
    n = pl.program_id(0)

    @pl.when(n == 0)
    def _():
        gbias_ref[...] = jnp.zeros_like(gbias_ref)
        gsink_ref[...] = jnp.zeros_like(gsink_ref)
        carry_ref[...] = jnp.zeros_like(carry_ref)

    @pl.when(n < nb)
    def _():
        lane = lax.broadcasted_iota(jnp.int32, (1, 128), 1)
        dqs, dks, dvs = [], [], []
        gsink = jnp.zeros((1, 128), F32)
        for kv in range(N_KV_HEADS):
            cs = slice(kv * HEAD_DIM, (kv + 1) * HEAD_DIM)
            kk = jnp.concatenate([kp_ref[:, cs], kc_ref[:, cs]], axis=0).astype(BF16)
            vv = jnp.concatenate([vp_ref[:, cs], vc_ref[:, cs]], axis=0).astype(BF16)
            dk = jnp.zeros((2 * BLOCK, HEAD_DIM), F32)
            dv = jnp.zeros((2 * BLOCK, HEAD_DIM), F32)
            for g in range(Q_PER_KV):
                h = kv * Q_PER_KV + g
                hs = slice(h * HEAD_DIM, (h + 1) * HEAD_DIM)
                qh = q_ref[:, hs].astype(BF16)
                s = _head_logits(qh, kk, bias_ref[h], n == 0)
                lse = lse_ref[:, h:h + 1]
                p = jnp.exp(s - lse)
                do = do_ref[:, hs]
                delta = jnp.sum(do * o_ref[:, hs], axis=1, keepdims=True)
                dob = do.astype(BF16)
                dp = lax.dot_general(dob, vv, _DIMS["nt"], preferred_element_type=F32)
                dl = p * (dp - delta)
                gbias_ref[h] += dl
                psink = jnp.exp(sink_ref[0, h] - lse)
                gsink = gsink + jnp.where(lane == h, -jnp.sum(psink * delta), 0.0)
                dlb = dl.astype(BF16)
                dqs.append(jnp.dot(dlb, kk, preferred_element_type=F32) * ATTN_SCALE)
                dk = dk + lax.dot_general(dlb, qh, _DIMS["tn"], preferred_element_type=F32) * ATTN_SCALE
                dv = dv + lax.dot_general(p.astype(BF16), dob, _DIMS["tn"], preferred_element_type=F32)
            dks.append(dk)
            dvs.append(dv)
        dq_ref[...] = jnp.concatenate(dqs, axis=1).astype(dq_ref.dtype)
        gsink_ref[...] += gsink
        dkv = jnp.concatenate(dks + dvs, axis=1)
        dkv_ref[...] = (carry_ref[...] + dkv[:BLOCK]).astype(dkv_ref.dtype)
        carry_ref[...] = dkv[BLOCK:]

    @pl.when(n == nb)
    def _():
        dkv_ref[...] = carry_ref[...].astype(dkv_ref.dtype)


def _attn_backward(proj, sinks, bias, d_attn, attn, lse, dproj):
    t = proj.shape[0]
    nb = t // BLOCK
    cur = lambda n: jnp.minimum(n, nb - 1)
    return pl.pallas_call(
        functools.partial(_attn_bwd_body, nb=nb), name="attn_bwd", grid=(nb + 1,),
        in_specs=[pl.BlockSpec(memory_space=pltpu.SMEM)] + _attn_specs(nb) + [
            _whole(bias.shape),
            pl.BlockSpec((BLOCK, D_ATTN), lambda n: (cur(n), 0)),
            pl.BlockSpec((BLOCK, D_ATTN), lambda n: (cur(n), 0)),
            pl.BlockSpec((BLOCK, N_Q_HEADS), lambda n: (cur(n), 0)),
            pl.BlockSpec(memory_space=pl.ANY)],
        out_specs=[pl.BlockSpec((BLOCK, D_ATTN), lambda n: (cur(n), C_Q // D_ATTN)),
                   pl.BlockSpec((BLOCK, 2 * D_KV), lambda n: (jnp.maximum(n - 1, 0), 0)),
                   _whole(bias.shape), _whole((1, 128))],
        out_shape=[jax.ShapeDtypeStruct(dproj.shape, dproj.dtype), jax.ShapeDtypeStruct((t, 2 * D_KV), dproj.dtype),
                   jax.ShapeDtypeStruct(bias.shape, F32), jax.ShapeDtypeStruct((1, 128), F32)],
        scratch_shapes=[pltpu.VMEM((BLOCK, 2 * D_KV), F32)],
        input_output_aliases={10: 0},
        compiler_params=_params(("arbitrary",)),
    )(sinks, proj, proj, proj, proj, proj, bias, d_attn, attn, lse, dproj)


ROWS = 256


def _rowwise(body, name, t, ins, outs, aliases=None):
    rows = min(ROWS, t)

    def col_spec(w, c0):
        if c0 % w == 0:
            return pl.BlockSpec((rows, w), lambda i: (i, c0 // w))
        return pl.BlockSpec((pl.Element(rows), pl.Element(w)), lambda i: (i * rows, c0))

    in_specs, args = [], []
    for a, w, c0 in ins:
        args.append(a)
        if w is None:
            in_specs.append(pl.BlockSpec(memory_space=pl.ANY) if c0 == "any" else _whole(a.shape))
        else:
            in_specs.append(col_spec(w, c0))
    out_specs, out_shape = [], []
    for shape, dtype, w, c0 in outs:
        out_shape.append(jax.ShapeDtypeStruct(shape, dtype))
        out_specs.append(_whole(shape) if w is None else col_spec(w, c0))
    accum = any(o[2] is None for o in outs)
    return pl.pallas_call(
        body, name=name, grid=(t // rows,), in_specs=in_specs, out_specs=out_specs, out_shape=out_shape,
        input_output_aliases=aliases or {},
        compiler_params=_params(("arbitrary",) if accum else ("parallel",)),
    )(*args)


def _ssm_gate_fwd_body(glu_ref, z_ref, h_ref):
    a, b = glu_ref[:, :D_SSM], glu_ref[:, D_SSM:]
    h_ref[...] = ((a * _sigmoid(b)) * _silu_and_grad(z_ref[...])[0]).astype(h_ref.dtype)


def _ssm_gate_bwd_body(dh_ref, glu_ref, z_ref, dproj_in, dglu_ref, dz_ref):
    del dproj_in
    a, b = glu_ref[:, :D_SSM], glu_ref[:, D_SSM:]
    sb = _sigmoid(b)
    silu, dsilu = _silu_and_grad(z_ref[...])
    dh = dh_ref[...]
    dg = dh * silu
    dz_ref[...] = (dh * (a * sb) * dsilu).astype(dz_ref.dtype)
    dglu_ref[:, :D_SSM] = (dg * sb).astype(dglu_ref.dtype)
    dglu_ref[:, D_SSM:] = (dg * a * sb * (1.0 - sb)).astype(dglu_ref.dtype)


def _attn_gate_fwd_body(attn_ref, z_ref, h_ref):
    h_ref[...] = (attn_ref[...] * _silu_and_grad(z_ref[...])[0]).astype(h_ref.dtype)


def _attn_gate_bwd_body(dh_ref, attn_ref, z_ref, dproj_in, dattn_ref, dz_ref):
    del dproj_in
    silu, dsilu = _silu_and_grad(z_ref[...])
    dh = dh_ref[...]
    dattn_ref[...] = dh * silu
    dz_ref[...] = (dh * attn_ref[...] * dsilu).astype(dz_ref.dtype)


def _merge_fwd_body(bs_ref, ba_ref, gl_ref, m_ref):
    gs = _sigmoid(gl_ref[:, :D_MODEL])
    ga = _sigmoid(gl_ref[:, D_MODEL:])
    m_ref[...] = (gs * bs_ref[...] + ga * ba_ref[...]).astype(m_ref.dtype)


def _merge_bwd_body(dm_ref, bs_ref, ba_ref, gl_ref, dbs_ref, dba_ref, dgl_ref):
    gs = _sigmoid(gl_ref[:, :D_MODEL])
    ga = _sigmoid(gl_ref[:, D_MODEL:])
    dm = dm_ref[...]
    dbs_ref[...] = (dm * gs).astype(dbs_ref.dtype)
    dba_ref[...] = (dm * ga).astype(dba_ref.dtype)
    dgl_ref[:, :D_MODEL] = (dm * bs_ref[...] * gs * (1.0 - gs)).astype(dgl_ref.dtype)
    dgl_ref[:, D_MODEL:] = (dm * ba_ref[...] * ga * (1.0 - ga)).astype(dgl_ref.dtype)


def _ln_loss_body(x_ref, o_ref, tgt_ref, gain_ref, bias_ref, dr_ref, loss_ref, dgain_ref, dbias_ref):
    @pl.when(pl.program_id(0) == 0)
    def _():
        loss_ref[...] = jnp.zeros_like(loss_ref)
        dgain_ref[...] = jnp.zeros_like(dgain_ref)
        dbias_ref[...] = jnp.zeros_like(dbias_ref)

    r = ALPHA * x_ref[...] + o_ref[...]
    mu = jnp.mean(r, axis=1, keepdims=True)
    rc = r - mu
    var = jnp.mean(rc * rc, axis=1, keepdims=True)
    rstd = lax.rsqrt(var + LN_EPS)
    xhat = rc * rstd
    gain = gain_ref[...]
    err = xhat * gain + bias_ref[...] - tgt_ref[...]
    loss_ref[...] += 0.5 * jnp.sum(jnp.mean(err * err, axis=1, keepdims=True), axis=0, keepdims=True)
    dy = err * (1.0 / D_MODEL)
    dgain_ref[...] += jnp.sum(dy * xhat, axis=0, keepdims=True)
    dbias_ref[...] += jnp.sum(dy, axis=0, keepdims=True)
    dxhat = dy * gain
    m1 = jnp.mean(dxhat, axis=1, keepdims=True)
    m2 = jnp.mean(dxhat * xhat, axis=1, keepdims=True)
    dr_ref[...] = rstd * (dxhat - m1 - xhat * m2)


def _place_body(piece_ref, dproj_in, o_ref):
    del dproj_in
    o_ref[...] = piece_ref[...]


def _adamw_body(*refs, n_parts):
    w_ref, m_ref, v_ref = refs[:3]
    parts = refs[3:3 + n_parts]
    g_ref, d_ref, nm_ref, nv_ref = refs[3 + n_parts:]
    g = parts[0][...].astype(F32)
    for p in parts[1:]:
        g = g + p[...].astype(F32)
    m = ADAM_B1 * m_ref[...] + (1.0 - ADAM_B1) * g
    v = ADAM_B2 * v_ref[...] + (1.0 - ADAM_B2) * (g * g)
    m_hat = m / (1.0 - ADAM_B1 ** ADAM_STEP)
    v_hat = v / (1.0 - ADAM_B2 ** ADAM_STEP)
    g_ref[...] = g
    d_ref[...] = -ADAM_LR * (m_hat / (jnp.sqrt(v_hat) + ADAM_EPS) + ADAM_WD * w_ref[...])
    nm_ref[...] = m
    nv_ref[...] = v


def _adamw(w, m, v, parts, *, name, rows=256):
    shape = w.shape
    w2, m2, v2 = (a.reshape(-1, shape[-1]) for a in (w, m, v))
    parts = [p.reshape(w2.shape) for p in parts]
    r, c = w2.shape
    rows = rows if r % rows == 0 else r
    spec = pl.BlockSpec((rows, c), lambda i: (i, 0))
    outs = pl.pallas_call(
        functools.partial(_adamw_body, n_parts=len(parts)), name=name, grid=(r // rows,),
        in_specs=[spec] * (3 + len(parts)), out_specs=[spec] * 4,
        out_shape=[jax.ShapeDtypeStruct((r, c), F32)] * 4,
        compiler_params=_params(("parallel",)),
    )(w2, m2, v2, *parts)
    return tuple(o.reshape(shape) for o in outs)


BIG = ("w_in", "w_glu", "w_branch_ssm", "w_branch_attn", "w_out")
SHARD_AXIS = dict(w_in=1, w_glu=1, w_branch_ssm=1, w_branch_attn=1, w_out=0)
HBM = pl.BlockSpec(memory_space=pl.ANY)


def _position():
    x, y, c = lax.axis_index("x"), lax.axis_index("y"), lax.axis_index("c")
    other_chips = [(1 - x, y), (x, 1 - y), (1 - x, 1 - y)]
    return x, y, c, other_chips


def _window(ref, axis, shard, n_shards, half=None):
    rows, cols = ref.shape[-2:]
    sel = [slice(None), slice(None)]
    size = ref.shape[-2 + axis] // n_shards
    sel[axis] = pl.ds(pl.multiple_of(shard * size, 128), size)
    if half is not None:
        hsize = ref.shape[-1 - axis] // 2
        sel[1 - axis] = pl.ds(pl.multiple_of(half * hsize, 128), hsize)
    return ref.at[tuple(sel)]


def _half(ref, axis, half):
    hsize = ref.shape[-1 - axis] // 2
    sel = [slice(None), slice(None)]
    sel[1 - axis] = pl.ds(pl.multiple_of(half * hsize, 128), hsize)
    return ref.at[tuple(sel)]


def _remote(src, dst, send_sem, recv_sem, device):
    return pltpu.make_async_remote_copy(src_ref=src, dst_ref=dst, send_sem=send_sem, recv_sem=recv_sem,
                                        device_id=device, device_id_type=MESH)


def _gather_body(*refs, axes):
    n = len(axes)
    shards, outs = refs[:n], refs[n:2 * n]
    send_sems, recv_sems, fsend_sems, frecv_sems, local_sems = refs[2 * n:]
    x, y, c, chips = _position()
    me = 2 * x + y
    local, sends, passes = [], [], []
    for w, ax in enumerate(axes):
        cp = pltpu.make_async_copy(shards[w], _window(outs[w], ax, me, N_CHIPS), local_sems.at[w])
        cp.start()
        local.append(cp)
    for w, ax in enumerate(axes):
        for r, (px, py) in enumerate(chips):
            cp = _remote(_half(shards[w], ax, c), _window(outs[w], ax, me, N_CHIPS, c),
                         send_sems.at[3 * w + r], recv_sems.at[3 * w + r], (px, py, c))
            cp.start()
            sends.append(cp)
    for w, ax in enumerate(axes):
        for r, (px, py) in enumerate(chips):
            landed = _window(outs[w], ax, 2 * px + py, N_CHIPS, c)
            _remote(landed, landed, send_sems.at[3 * w + r], recv_sems.at[3 * w + r], (px, py, c)).wait_recv()
            cp = _remote(landed, landed, fsend_sems.at[3 * w + r], frecv_sems.at[3 * w + r], (x, y, 1 - c))
            cp.start()
            passes.append(cp)
    for w, ax in enumerate(axes):
        for r, (px, py) in enumerate(chips):
            other = _window(outs[w], ax, 2 * px + py, N_CHIPS, 1 - c)
            _remote(other, other, fsend_sems.at[3 * w + r], frecv_sems.at[3 * w + r], (x, y, 1 - c)).wait_recv()
    for cp in sends + passes:
        cp.wait_send()
    for cp in local:
        cp.wait()


def _all_gather(shards):
    axes = tuple(SHARD_AXIS[k] for k in BIG)
    out_shape = []
    for k in BIG:
        s = list(shards[k].shape)
        s[SHARD_AXIS[k]] *= N_CHIPS
        out_shape.append(jax.ShapeDtypeStruct(tuple(s), shards[k].dtype))
    n = len(BIG)
    outs = pl.pallas_call(
        functools.partial(_gather_body, axes=axes), name="gather_weights",
        in_specs=[HBM] * n, out_specs=[HBM] * n, out_shape=out_shape,
        scratch_shapes=[pltpu.SemaphoreType.DMA((3 * n,))] * 4 + [pltpu.SemaphoreType.DMA((n,))],
    )(*[shards[k] for k in BIG])
    return dict(zip(BIG, outs))


def _pair_exchange_body(*refs, axes):
    n = len(axes)
    grads, mine, theirs = refs[:n], refs[n:2 * n], refs[2 * n:3 * n]
    send_sems, recv_sems, local_sems = refs[3 * n:]
    x, y, c, _ = _position()
    copies, local = [], []
    for w, ax in enumerate(axes):
        cp = _remote(_half(grads[w], ax, 1 - c), theirs[w], send_sems.at[w], recv_sems.at[w], (x, y, 1 - c))
        cp.start()
        copies.append(cp)
        lc = pltpu.make_async_copy(_half(grads[w], ax, c), mine[w], local_sems.at[w])
        lc.start()
        local.append(lc)
    for cp in copies:
        cp.wait_recv()
    for cp in copies:
        cp.wait_send()
    for lc in local:
        lc.wait()


def _chip_exchange_body(*refs, axes):
    n = len(axes)
    sums, slots = refs[:n], refs[n:2 * n]
    send_sems, recv_sems, local_sems = refs[2 * n:]
    x, y, c, chips = _position()
    me = 2 * x + y
    copies, local = [], []
    for w, ax in enumerate(axes):
        for r, (px, py) in enumerate(chips):
            cp = _remote(_window(sums[w], ax, 2 * px + py, N_CHIPS), slots[w].at[r],
                         send_sems.at[3 * w + r], recv_sems.at[3 * w + r], (px, py, c))
            cp.start()
            copies.append(cp)
        lc = pltpu.make_async_copy(_window(sums[w], ax, me, N_CHIPS), slots[w].at[3], local_sems.at[w])
        lc.start()
        local.append(lc)
    for cp in copies:
        cp.wait_recv()
    for cp in copies:
        cp.wait_send()
    for lc in local:
        lc.wait()


def _pair_share_body(*refs, axes):
    n = len(axes)
    halves, outs = refs[:n], refs[n:2 * n]
    send_sems, recv_sems, local_sems = refs[2 * n:]
    x, y, c, _ = _position()
    copies, local = [], []
    for w, ax in enumerate(axes):
        cp = _remote(halves[w], _half(outs[w], ax, c), send_sems.at[w], recv_sems.at[w], (x, y, 1 - c))
        cp.start()
        copies.append(cp)
        lc = pltpu.make_async_copy(halves[w], _half(outs[w], ax, c), local_sems.at[w])
        lc.start()
        local.append(lc)
    for w, ax in enumerate(axes):
        other = _half(outs[w], ax, 1 - c)
        _remote(other, other, send_sems.at[w], recv_sems.at[w], (x, y, 1 - c)).wait_recv()
    for cp in copies:
        cp.wait_send()
    for lc in local:
        lc.wait()


def _add_body(*refs, n_parts):
    acc = refs[0][...]
    for p in refs[1:n_parts]:
        acc = acc + p[...]
    refs[n_parts][...] = acc


def _add(parts, *, name, rows=256):
    shape = parts[0][0].shape[1:] if isinstance(parts[0], tuple) else parts[0].shape
    r, c = shape
    rows = min(rows, r)
    in_specs, args = [], []
    for p in parts:
        if isinstance(p, tuple):
            args.append(p[0])
            in_specs.append(pl.BlockSpec((None, rows, c), lambda i, s=p[1]: (s, i, 0)))
        else:
            args.append(p)
            in_specs.append(pl.BlockSpec((rows, c), lambda i: (i, 0)))
    return pl.pallas_call(
        functools.partial(_add_body, n_parts=len(parts)), name=name, grid=(r // rows,),
        in_specs=in_specs, out_specs=pl.BlockSpec((rows, c), lambda i: (i, 0)),
        out_shape=jax.ShapeDtypeStruct((r, c), F32),
        compiler_params=_params(("parallel",)),
    )(*args)


def _comm_call(body, name, ins, out_shape, n_sem_arrays, n_sems):
    axes = tuple(SHARD_AXIS[k] for k in BIG)
    return pl.pallas_call(
        functools.partial(body, axes=axes), name=name,
        in_specs=[HBM] * len(ins), out_specs=[HBM] * len(out_shape), out_shape=out_shape,
        scratch_shapes=[pltpu.SemaphoreType.DMA((n_sems,))] * n_sem_arrays + [pltpu.SemaphoreType.DMA((len(BIG),))],
    )(*ins)


def _reduce_grads(grads):
    n = len(BIG)
    f32 = lambda s: jax.ShapeDtypeStruct(tuple(s), F32)

    def half_shape(shape, ax):
        s = list(shape)
        s[1 - ax] //= 2
        return s

    def shard_shape(shape, ax):
        s = list(shape)
        s[ax] //= N_CHIPS
        return s

    full = [grads[k].shape for k in BIG]
    axes = [SHARD_AXIS[k] for k in BIG]
    halves = [half_shape(s, ax) for s, ax in zip(full, axes)]
    outs = _comm_call(_pair_exchange_body, "pair_exchange", [grads[k] for k in BIG],
                      [f32(s) for s in halves] * 2, 2, n)
    pair = [_add([outs[w], outs[n + w]], name="pair_add_" + BIG[w]) for w in range(n)]
    quarter = [shard_shape(s, ax) for s, ax in zip(halves, axes)]
    slots = _comm_call(_chip_exchange_body, "chip_exchange", pair, [f32([4] + s) for s in quarter], 2, 3 * n)
    reduced = [_add([(slots[w], 3), (slots[w], 0), (slots[w], 1), (slots[w], 2)], name="chip_add_" + BIG[w])
               for w in range(n)]
    shards = _comm_call(_pair_share_body, "pair_share", reduced,
                        [f32(shard_shape(s, ax)) for s, ax in zip(full, axes)], 2, n)
    return dict(zip(BIG, shards))


SMALL = (("ssm_lambda_re", (1, 64, 64)), ("ssm_lambda_im", (1, 64, 64)), ("ssm_b_re", (1, 64, 64, 16)),
         ("ssm_b_im", (1, 64, 64, 16)), ("ssm_c_re", (1, 64, 16, 64)), ("ssm_c_im", (1, 64, 16, 64)),
         ("ssm_d", (1, 1024)), ("ssm_log_step", (1, 64)), ("attn_sinks", (1, 16)), ("rel_bias_table", (32, 16)),
         ("ln_gain", (1, 2048)), ("ln_bias", (1, 2048)))
SMALL_SIZE = sum(int(np.prod(s)) for _, s in SMALL)
PACK_ROWS = -(-(SMALL_SIZE + 1) // (8 * 128)) * 8


def _pack(values, extra=None):
    flat = [values[k].reshape(-1).astype(F32) for k, _ in SMALL]
    flat.append(jnp.zeros((1,), F32) if extra is None else extra.reshape(1))
    flat.append(jnp.zeros((PACK_ROWS * 128 - SMALL_SIZE - 1,), F32))
    return jnp.concatenate(flat).reshape(PACK_ROWS, 128)


def _unpack(packed):
    flat = packed.reshape(-1)
    out, off = {}, 0
    for k, s in SMALL:
        size = int(np.prod(s))
        out[k] = flat[off:off + size].reshape(s)
        off += size
    return out, flat[off]


def _small_allreduce_body(p_ref, o_ref, buf, send_sems, recv_sems):
    x, y, c, _ = _position()
    me = 4 * x + 2 * y + c
    buf[0] = p_ref[...]
    copies = []
    for r in range(1, 8):
        peer = tuple(1 - v if (r >> s) & 1 else v for v, s in ((x, 2), (y, 1), (c, 0)))
        cp = _remote(p_ref, buf.at[r], send_sems.at[r - 1], recv_sems.at[r - 1], peer)
        cp.start()
        copies.append(cp)
    for cp in copies:
        cp.wait_recv()
    for cp in copies:
        cp.wait_send()
    acc = buf[me]
    for s in range(1, 8):
        acc = acc + buf[jnp.bitwise_xor(me, s)]
    o_ref[...] = acc


def _small_allreduce(packed):
    vm = pl.BlockSpec(memory_space=pltpu.VMEM)
    return pl.pallas_call(
        functools.partial(_small_allreduce_body), name="small_allreduce",
        in_specs=[vm], out_specs=vm, out_shape=jax.ShapeDtypeStruct(packed.shape, F32),
        scratch_shapes=[pltpu.VMEM((8,) + packed.shape, F32), pltpu.SemaphoreType.DMA((7,)),
                        pltpu.SemaphoreType.DMA((7,))],
        compiler_params=pltpu.CompilerParams(vmem_limit_bytes=VMEM_LIMIT),
    )(packed)


def _local_step(x, target, w_in, w_glu, w_bs, w_ba, w_out, lam_re, lam_im, b_re, b_im, c_re, c_im, d_skip,
                log_step, sinks, rel_bias_table, ln_gain, ln_bias):
    t = x.shape[0]
    bucket = jnp.asarray(_bucket_table())
    p_re, p_im, b_bd, c_bd = _ssm_prepare(lam_re, lam_im, b_re, b_im, c_re, c_im, log_step)
    bias = _bias_build(rel_bias_table, bucket)

    xb = _cast(x, BF16, name="cast_x")
    proj = _matmul(xb, w_in, "nn", name="proj", tm=1024, tn=512)
    y_ssm, g_in, car_re, car_im = _ssm_forward(proj, p_re, p_im, b_bd, c_bd, d_skip)
    glu = _matmul(g_in, w_glu, "nn", name="glu", tm=1024, tn=512)
    (h_ssm,) = _rowwise(functools.partial(_ssm_gate_fwd_body), "ssm_gate_fwd", t,
                        [(glu, 2 * D_SSM, 0), (proj, D_SSM, C_ZS)], [((t, D_SSM), BF16, D_SSM, 0)])
    attn, lse = _attn_forward(proj, sinks, bias)
    (h_attn,) = _rowwise(functools.partial(_attn_gate_fwd_body), "attn_gate_fwd", t,
                         [(attn, D_ATTN, 0), (proj, D_ATTN, C_ZA)], [((t, D_ATTN), BF16, D_ATTN, 0)])
    bs = _matmul(h_ssm, w_bs, "nn", name="branch_ssm", tm=1024, tn=512)
    ba = _matmul(h_attn, w_ba, "nn", name="branch_attn", tm=1024, tn=512)
    gl_in = (proj, 2 * D_MODEL, C_GL)
    (merged,) = _rowwise(functools.partial(_merge_fwd_body), "merge_fwd", t,
                         [(bs, D_MODEL, 0), (ba, D_MODEL, 0), gl_in], [((t, D_MODEL), BF16, D_MODEL, 0)])
    out = _matmul(merged, w_out, "nn", name="out_proj", tm=1024, tn=512)
    dr, loss, g_gain, g_bias = _rowwise(
        functools.partial(_ln_loss_body), "ln_loss", t,
        [(x, D_MODEL, 0), (out, D_MODEL, 0), (target, D_MODEL, 0), (ln_gain, None, 0), (ln_bias, None, 0)],
        [((t, D_MODEL), F32, D_MODEL, 0), ((1, 1), F32, None, 0), ((1, D_MODEL), F32, None, 0),
         ((1, D_MODEL), F32, None, 0)])

    drb = _cast(dr, BF16, name="cast_dr")
    g_w_out = _matmul(merged, drb, "tn", name="g_w_out", tm=1024, tn=512, tk=1024)
    d_merged = _matmul(drb, w_out, "nt", name="d_merged", tm=1024, tn=512)
    d_bs, d_ba, dproj = _rowwise(
        functools.partial(_merge_bwd_body), "merge_bwd", t,
        [(d_merged, D_MODEL, 0), (bs, D_MODEL, 0), (ba, D_MODEL, 0), gl_in],
        [((t, D_MODEL), BF16, D_MODEL, 0), ((t, D_MODEL), BF16, D_MODEL, 0),
         ((t, D_IN), BF16, 2 * D_MODEL, C_GL)])
    g_w_bs = _matmul(h_ssm, d_bs, "tn", name="g_w_branch_ssm", tm=1024, tn=512, tk=1024)
    d_h_ssm = _matmul(d_bs, w_bs, "nt", name="d_h_ssm", tm=1024, tn=512)
    g_w_ba = _matmul(h_attn, d_ba, "tn", name="g_w_branch_attn", tm=1024, tn=512, tk=1024)
    d_h_attn = _matmul(d_ba, w_ba, "nt", name="d_h_attn", tm=1024, tn=512)

    d_attn, dproj = _rowwise(
        functools.partial(_attn_gate_bwd_body), "attn_gate_bwd", t,
        [(d_h_attn, D_ATTN, 0), (attn, D_ATTN, 0), (proj, D_ATTN, C_ZA), (dproj, None, "any")],
        [((t, D_ATTN), F32, D_ATTN, 0), ((t, D_IN), BF16, D_ATTN, C_ZA)], aliases={3: 1})
    dproj, d_kv, g_bias_full, g_sinks = _attn_backward(proj, sinks, bias, d_attn, attn, lse, dproj)
    (dproj,) = _rowwise(functools.partial(_place_body), "place_dkv", t,
                        [(d_kv, 2 * D_KV, 0), (dproj, None, "any")],
                        [((t, D_IN), BF16, 2 * D_KV, C_K)], aliases={1: 0})
    g_table = _bias_grad(g_bias_full, bucket)

    d_glu, dproj = _rowwise(
        functools.partial(_ssm_gate_bwd_body), "ssm_gate_bwd", t,
        [(d_h_ssm, D_SSM, 0), (glu, 2 * D_SSM, 0), (proj, D_SSM, C_ZS), (dproj, None, "any")],
        [((t, 2 * D_SSM), BF16, 2 * D_SSM, 0), ((t, D_IN), BF16, D_SSM, C_ZS)], aliases={3: 1})
    g_w_glu = _matmul(g_in, d_glu, "tn", name="g_w_glu", tm=1024, tn=512, tk=1024)
    d_g_in = _matmul(d_glu, w_glu, "nt", name="d_g_in", tm=1024, tn=512)
    dproj, g_bbd, g_cbd, g_lam_re, g_lam_im, g_d = _ssm_backward(
        proj, d_g_in, y_ssm, car_re, car_im, p_re, p_im, b_bd, c_bd, d_skip, dproj)
    g_lr, g_li, g_br, g_bi, g_cr, g_ci, g_ls = _ssm_param_grads(
        lam_re, lam_im, b_re, b_im, log_step, g_lam_re, g_lam_im, g_bbd, g_cbd)

    g_w_in = _matmul(xb, dproj, "tn", name="g_w_in", tm=1024, tn=512, tk=1024)
    grad_x = _matmul(dproj, w_in, "nt", name="grad_x", tm=1024, tn=512, tk=2176, res=dr, res_scale=ALPHA)

    big = dict(w_in=g_w_in, w_glu=g_w_glu, w_branch_ssm=g_w_bs, w_branch_attn=g_w_ba, w_out=g_w_out)
    small = dict(ssm_lambda_re=g_lr, ssm_lambda_im=g_li, ssm_b_re=g_br, ssm_b_im=g_bi, ssm_c_re=g_cr,
                 ssm_c_im=g_ci, ssm_d=g_d, ssm_log_step=g_ls, attn_sinks=g_sinks[:, :N_Q_HEADS],
                 rel_bias_table=g_table, ln_gain=g_gain, ln_bias=g_bias)
    return loss, grad_x, big, small


WEIGHTS = ("w_in", "ssm_lambda_re", "ssm_lambda_im", "ssm_b_re", "ssm_b_im", "ssm_c_re", "ssm_c_im", "ssm_d",
           "ssm_log_step", "w_glu", "attn_sinks", "rel_bias_table", "w_branch_ssm", "w_branch_attn", "w_out",
           "ln_gain", "ln_bias")


def kernel(x, w_in, ssm_lambda_re, ssm_lambda_im, ssm_b_re, ssm_b_im, ssm_c_re, ssm_c_im, ssm_d, ssm_log_step, w_glu, attn_sinks, rel_bias_table, w_branch_ssm, w_branch_attn, w_out, ln_gain, ln_bias, loss_target, m_w_in, m_ssm_lambda_re, m_ssm_lambda_im, m_ssm_b_re, m_ssm_b_im, m_ssm_c_re, m_ssm_c_im, m_ssm_d, m_ssm_log_step, m_w_glu, m_attn_sinks, m_rel_bias_table, m_w_branch_ssm, m_w_branch_attn, m_w_out, m_ln_gain, m_ln_bias, v_w_in, v_ssm_lambda_re, v_ssm_lambda_im, v_ssm_b_re, v_ssm_b_im, v_ssm_c_re, v_ssm_c_im, v_ssm_d, v_ssm_log_step, v_w_glu, v_attn_sinks, v_rel_bias_table, v_w_branch_ssm, v_w_branch_attn, v_w_out, v_ln_gain, v_ln_bias):
    given = dict(locals())
    w = {k: given[k] for k in WEIGHTS}
    m = {k: given["m_" + k] for k in WEIGHTS}
    v = {k: given["v_" + k] for k in WEIGHTS}

    shards = {k: _cast(w[k][0], BF16, name="cast_" + k) for k in BIG}
    full = _all_gather(shards)
    loss, grad_x, g_big, g_small = _local_step(
        x[0], loss_target[0], full["w_in"], full["w_glu"], full["w_branch_ssm"], full["w_branch_attn"],
        full["w_out"], ssm_lambda_re[0], ssm_lambda_im[0], ssm_b_re[0], ssm_b_im[0], ssm_c_re[0], ssm_c_im[0],
        ssm_d, ssm_log_step, attn_sinks, rel_bias_table, ln_gain, ln_bias)

    g_shard = _reduce_grads(g_big)
    g_packed = _small_allreduce(_pack(g_small, loss))
    loss_sum = _unpack(g_packed)[1]

    grad, delta, new_m, new_v = {}, {}, {}, {}
    for k in BIG:
        grad[k], delta[k], new_m[k], new_v[k] = _adamw(w[k], m[k], v[k], [g_shard[k]], name="adamw_" + k)
    gs, ds, ms, vs = _adamw(_pack(w), _pack(m), _pack(v), [g_packed], name="adamw_small")
    for dst, packed in ((grad, gs), (delta, ds), (new_m, ms), (new_v, vs)):
        dst.update(_unpack(packed)[0])

    return (loss_sum, grad_x[None], *[grad[k] for k in WEIGHTS], *[delta[k] for k in WEIGHTS],
            *[new_m[k] for k in WEIGHTS], *[new_v[k] for k in WEIGHTS])
```

```python
import functools
import math

import numpy as np
import jax
import jax.numpy as jnp
from jax import lax
from jax.experimental import pallas as pl
from jax.experimental.pallas import tpu as pltpu

F32 = jnp.float32
BF16 = jnp.bfloat16

D_MODEL = 2048
D_SSM = 1024
SSM_GROUP = 16
N_GROUPS = 64
SSM_STATE = 64
N_LANES = N_GROUPS * SSM_STATE
N_Q_HEADS = 16
N_KV_HEADS = 4
HEAD_DIM = 64
Q_PER_KV = 4
D_ATTN = 1024
D_KV = 256
WINDOW = 128
BLOCK = 128
N_BUCKETS = 32
MAX_DISTANCE = 128
D_IN = 8704
ALPHA = 2.0 ** 0.25
LN_EPS = 1e-5
NEG_INF = -1e30
ATTN_SCALE = HEAD_DIM ** -0.5

C_U, C_ZS, C_Q, C_K, C_V, C_ZA, C_GL = 0, 1024, 2048, 3072, 3328, 3584, 4608

ADAM_LR = 0.001
ADAM_B1 = 0.9
ADAM_B2 = 0.999
ADAM_EPS = 1e-08
ADAM_WD = 0.01
ADAM_STEP = 10

N_CHIPS = 4
MESH = pl.DeviceIdType.MESH

SSM_CHUNK = 256
SEG_LEN = SSM_CHUNK // 8
SLAB_LANES = 512
N_SLABS = N_LANES // SLAB_LANES
SLAB_CH = D_SSM // N_SLABS

VMEM_LIMIT = 60 * 1024 * 1024


def _params(sem=None, **kw):
    return pltpu.CompilerParams(dimension_semantics=sem, vmem_limit_bytes=VMEM_LIMIT, **kw)


_DIMS = {"nn": (((1,), (0,)), ((), ())), "nt": (((1,), (1,)), ((), ())), "tn": (((0,), (0,)), ((), ()))}


def _mm_body(*refs, dims, nk, res_scale):
    if res_scale is None:
        a_ref, b_ref, o_ref, acc_ref = refs
        r_ref = None
    else:
        a_ref, b_ref, r_ref, o_ref, acc_ref = refs
    k = pl.program_id(2)
    part = lax.dot_general(a_ref[...].astype(BF16), b_ref[...].astype(BF16), _DIMS[dims],
                           preferred_element_type=F32)

    def finish(acc):
        if r_ref is not None:
            acc = acc + res_scale * r_ref[...]
        o_ref[...] = acc.astype(o_ref.dtype)

    if nk == 1:
        finish(part)
    else:
        @pl.when(k == 0)
        def _():
            acc_ref[...] = part

        @pl.when(k > 0)
        def _():
            acc_ref[...] += part

        @pl.when(k == nk - 1)
        def _():
            finish(acc_ref[...])


def _matmul(a, b, dims, *, name, out_dtype=F32, tm=512, tn=512, tk=None, res=None, res_scale=None):
    if dims == "nn":
        (m, kk), n = a.shape, b.shape[1]
    elif dims == "nt":
        (m, kk), n = a.shape, b.shape[0]
    else:
        (kk, m), n = a.shape, b.shape[1]
    tm, tn = min(tm, m), min(tn, n)
    tk = kk if tk is None else min(tk, kk)
    assert m % tm == 0 and n % tn == 0 and kk % tk == 0, (name, m, n, kk, tm, tn, tk)
    nk = kk // tk
    a_spec = {"nn": pl.BlockSpec((tm, tk), lambda i, j, k: (i, k)),
              "nt": pl.BlockSpec((tm, tk), lambda i, j, k: (i, k)),
              "tn": pl.BlockSpec((tk, tm), lambda i, j, k: (k, i))}[dims]
    b_spec = {"nn": pl.BlockSpec((tk, tn), lambda i, j, k: (k, j)),
              "nt": pl.BlockSpec((tn, tk), lambda i, j, k: (j, k)),
              "tn": pl.BlockSpec((tk, tn), lambda i, j, k: (k, j))}[dims]
    in_specs, args = [a_spec, b_spec], [a, b]
    if res is not None:
        in_specs.append(pl.BlockSpec((tm, tn), lambda i, j, k: (i, j)))
        args.append(res)
    return pl.pallas_call(
        functools.partial(_mm_body, dims=dims, nk=nk, res_scale=res_scale if res is not None else None),
        name=name,
        grid=(m // tm, n // tn, nk),
        in_specs=in_specs,
        out_specs=pl.BlockSpec((tm, tn), lambda i, j, k: (i, j)),
        out_shape=jax.ShapeDtypeStruct((m, n), out_dtype),
        scratch_shapes=[pltpu.VMEM((tm, tn), F32)],
        compiler_params=_params(("parallel", "parallel", "arbitrary")),
    )(*args)


def _sigmoid(v):
    return 1.0 / (1.0 + jnp.exp(-v))


def _silu_and_grad(z):
    s = _sigmoid(z)
    return z * s, s * (1.0 + z * (1.0 - s))


def _cast_body(x_ref, o_ref):
    o_ref[...] = x_ref[...].astype(o_ref.dtype)


def _cast(x, dtype, *, name, rows=512):
    m, n = x.shape
    rows = min(rows, m)
    return pl.pallas_call(
        functools.partial(_cast_body), name=name, grid=(m // rows,),
        in_specs=[pl.BlockSpec((rows, n), lambda i: (i, 0))],
        out_specs=pl.BlockSpec((rows, n), lambda i: (i, 0)),
        out_shape=jax.ShapeDtypeStruct((m, n), dtype),
        compiler_params=_params(("parallel",)),
    )(x)


def _lam_bar(lr, li, ls):
    step = jnp.exp(ls)
    er = jnp.exp(lr * step)
    return step, er * jnp.cos(li * step), er * jnp.sin(li * step)


def _ssm_pow_body(lr_ref, li_ref, ls_ref, pr_ref, pi_ref):
    _, ar, ai = _lam_bar(lr_ref[...], li_ref[...], ls_ref[...])
    cr, ci = ar, ai
    for i in range(SEG_LEN):
        pr_ref[pl.ds(i, 1), :] = cr
        pi_ref[pl.ds(i, 1), :] = ci
        cr, ci = cr * ar - ci * ai, cr * ai + ci * ar


def _ssm_bbar_body(lr_ref, li_ref, ls_ref, br_ref, bi_ref, or_ref, oi_ref):
    lr, li = lr_ref[...], li_ref[...]
    _, ar, ai = _lam_bar(lr, li, ls_ref[...])
    d = lr * lr + li * li
    ir, ii = lr / d, -li / d
    nr, ni = ar - 1.0, ai
    cr, ci = nr * ir - ni * ii, nr * ii + ni * ir
    br, bi = br_ref[...], bi_ref[...]
    or_ref[...] = cr * br - ci * bi
    oi_ref[...] = cr * bi + ci * br


def _ssm_param_bwd_body(lr_ref, li_ref, ls_ref, br_ref, bi_ref, glr_ref, gli_ref, gbr_ref, gbi_ref,
                        dlr_ref, dli_ref, dls_ref, dbr_ref, dbi_ref):
    lr, li = lr_ref[...], li_ref[...]
    step, ar, ai = _lam_bar(lr, li, ls_ref[...])
    d = lr * lr + li * li
    ir, ii = lr / d, -li / d
    nr, ni = ar - 1.0, ai
    cr, ci = nr * ir - ni * ii, nr * ii + ni * ir
    br, bi, gbr, gbi = br_ref[...], bi_ref[...], gbr_ref[...], gbi_ref[...]
    dbr_ref[...] = cr * gbr + ci * gbi
    dbi_ref[...] = cr * gbi - ci * gbr
    gcr = jnp.sum(br * gbr + bi * gbi, axis=1, keepdims=True)
    gci = jnp.sum(br * gbi - bi * gbr, axis=1, keepdims=True)
    gnr, gni = ir * gcr + ii * gci, ir * gci - ii * gcr
    gir, gii = nr * gcr + ni * gci, nr * gci - ni * gcr
    gtr, gti = glr_ref[...] + gnr, gli_ref[...] + gni
    i2r, i2i = ir * ir - ii * ii, 2.0 * ir * ii
    g1r, g1i = -(i2r * gir + i2i * gii), -(i2r * gii - i2i * gir)
    g2r, g2i = step * (ar * gtr + ai * gti), step * (ar * gti - ai * gtr)
    mr, mi = lr * ar - li * ai, lr * ai + li * ar
    dlr_ref[...] = g1r + g2r
    dli_ref[...] = g1i + g2i
    dls_ref[...] = (mr * gtr + mi * gti) * step


def _whole(shape):
    return pl.BlockSpec(shape, lambda *_: (0,) * len(shape))


def _ssm_prepare(lam_re, lam_im, b_re, b_im, c_re, c_im, log_step):
    row = lambda a: a.reshape(1, N_LANES)
    col = lambda a: a.reshape(N_LANES, 1)
    ls = jnp.repeat(log_step.reshape(N_GROUPS), SSM_STATE)
    p_re, p_im = pl.pallas_call(
        functools.partial(_ssm_pow_body), name="ssm_pow",
        in_specs=[_whole((1, N_LANES))] * 3, out_specs=[_whole((SEG_LEN, N_LANES))] * 2,
        out_shape=[jax.ShapeDtypeStruct((SEG_LEN, N_LANES), F32)] * 2, grid=(1,),
    )(row(lam_re), row(lam_im), row(ls))
    bb_re, bb_im = pl.pallas_call(
        functools.partial(_ssm_bbar_body), name="ssm_bbar",
        in_specs=[_whole((N_LANES, 1))] * 3 + [_whole((N_LANES, SSM_GROUP))] * 2,
        out_specs=[_whole((N_LANES, SSM_GROUP))] * 2,
        out_shape=[jax.ShapeDtypeStruct((N_LANES, SSM_GROUP), F32)] * 2, grid=(1,),
    )(col(lam_re), col(lam_im), col(ls), b_re.reshape(N_LANES, SSM_GROUP), b_im.reshape(N_LANES, SSM_GROUP))
    eye = jnp.eye(8, dtype=F32)

    def b_slabs(bb):
        t = bb.reshape(N_SLABS, 8, SSM_STATE, SSM_GROUP).transpose(0, 1, 3, 2)
        return (t[:, :, :, None, :] * eye[None, :, None, :, None]).reshape(N_SLABS, SLAB_CH, SLAB_LANES)

    def c_slabs(c):
        t = c.reshape(N_SLABS, 8, SSM_GROUP, SSM_STATE).transpose(0, 1, 3, 2)
        return (t[:, :, :, None, :] * eye[None, :, None, :, None]).reshape(N_SLABS, SLAB_LANES, SLAB_CH)

    b_bd = jnp.concatenate([b_slabs(bb_re), b_slabs(bb_im)], axis=2).astype(BF16)
    c_bd = jnp.concatenate([c_slabs(c_re.reshape(N_GROUPS, SSM_GROUP, SSM_STATE)),
                            -c_slabs(c_im.reshape(N_GROUPS, SSM_GROUP, SSM_STATE))], axis=1).astype(BF16)
    return p_re, p_im, b_bd, c_bd


def _diag_blocks_b(g):
    t = g.reshape(N_SLABS, 8, SSM_GROUP, 8, SSM_STATE)
    t = jnp.stack([t[:, i, :, i, :] for i in range(8)], axis=1)
    return t.transpose(0, 1, 3, 2).reshape(N_LANES, SSM_GROUP)


def _diag_blocks_c(g):
    t = g.reshape(N_SLABS, 8, SSM_STATE, 8, SSM_GROUP)
    t = jnp.stack([t[:, i, :, i, :] for i in range(8)], axis=1)
    return t.transpose(0, 1, 3, 2).reshape(N_GROUPS, SSM_GROUP, SSM_STATE)


def _ssm_param_grads(lam_re, lam_im, b_re, b_im, log_step, g_lam_re, g_lam_im, g_bbd, g_cbd):
    col = lambda a: a.reshape(N_LANES, 1)
    ls = jnp.repeat(log_step.reshape(N_GROUPS), SSM_STATE)
    gbr = _diag_blocks_b(g_bbd[:, :, :SLAB_LANES])
    gbi = _diag_blocks_b(g_bbd[:, :, SLAB_LANES:])
    outs = pl.pallas_call(
        functools.partial(_ssm_param_bwd_body), name="ssm_param_bwd", grid=(1,),
        in_specs=[_whole((N_LANES, 1))] * 3 + [_whole((N_LANES, SSM_GROUP))] * 2 + [_whole((N_LANES, 1))] * 2
        + [_whole((N_LANES, SSM_GROUP))] * 2,
        out_specs=[_whole((N_LANES, 1))] * 3 + [_whole((N_LANES, SSM_GROUP))] * 2,
        out_shape=[jax.ShapeDtypeStruct((N_LANES, 1), F32)] * 3 + [jax.ShapeDtypeStruct((N_LANES, SSM_GROUP), F32)] * 2,
    )(col(lam_re), col(lam_im), col(ls), b_re.reshape(N_LANES, SSM_GROUP), b_im.reshape(N_LANES, SSM_GROUP),
      col(g_lam_re), col(g_lam_im), gbr, gbi)
    dlr, dli, dls, dbr, dbi = outs
    d_c_re = _diag_blocks_c(g_cbd[:, :SLAB_LANES, :])
    d_c_im = -_diag_blocks_c(g_cbd[:, SLAB_LANES:, :])
    return (dlr.reshape(1, N_GROUPS, SSM_STATE), dli.reshape(1, N_GROUPS, SSM_STATE),
            dbr.reshape(1, N_GROUPS, SSM_STATE, SSM_GROUP), dbi.reshape(1, N_GROUPS, SSM_STATE, SSM_GROUP),
            d_c_re[None], d_c_im[None], dls.reshape(N_GROUPS, SSM_STATE).sum(axis=1).reshape(1, N_GROUPS))


def _bcast8(v):
    return jnp.broadcast_to(v, (8, v.shape[1]))


def _segment_permutation():
    p = np.zeros((SSM_CHUNK, SSM_CHUNK), np.float32)
    rows = np.arange(SSM_CHUNK)
    p[rows, (rows % 8) * SEG_LEN + rows // 8] = 1.0
    return p


def _permute_exact(perm, val, pieces):
    out, rest = None, val
    for n in range(pieces):
        part = rest.astype(BF16)
        moved = jnp.dot(perm, part, preferred_element_type=F32)
        out = moved if out is None else out + moved
        if n + 1 < pieces:
            rest = rest - part.astype(F32)
    return out


def _scan_buffer():
    return pltpu.VMEM((SSM_CHUNK, N_LANES), F32)


def _lanes(k):
    return pl.ds(k * SLAB_LANES, SLAB_LANES)


def _tile(i):
    return pl.ds(i * 8 if isinstance(i, int) else pl.multiple_of(i * 8, 8), 8)


def _seg_get(ref, k, i):
    return ref[_tile(i), _lanes(k)]


def _seg_put(ref, k, i, val):
    ref[_tile(i), _lanes(k)] = val


def _slab_get(ref, k):
    return ref[:, _lanes(k)]


def _slab_put(ref, k, val):
    ref[:, _lanes(k)] = val


def _scan_forward(s_re, s_im, p_re, p_im, car_re, car_im, sp_re=None, sp_im=None):
    for k in range(N_SLABS):
        ln = pl.ds(k * SLAB_LANES, SLAB_LANES)
        ar, ai = _bcast8(p_re[pl.ds(0, 1), ln]), _bcast8(p_im[pl.ds(0, 1), ln])

        def step(i, s, k=k, ar=ar, ai=ai):
            sr, si = s
            nr = ar * sr - ai * si + _seg_get(s_re, k, i)
            ni = ar * si + ai * sr + _seg_get(s_im, k, i)
            _seg_put(s_re, k, i, nr)
            _seg_put(s_im, k, i, ni)
            return nr, ni

        zero = jnp.zeros((8, SLAB_LANES), F32)
        er, ei = lax.fori_loop(0, SEG_LEN, step, (zero, zero), unroll=4)
        lr, li = p_re[pl.ds(SEG_LEN - 1, 1), ln], p_im[pl.ds(SEG_LEN - 1, 1), ln]
        cr, ci = car_re[pl.ds(0, 1), ln], car_im[pl.ds(0, 1), ln]
        rows_r, rows_i = [], []
        for r in range(8):
            rows_r.append(cr)
            rows_i.append(ci)
            cr, ci = er[r:r + 1] + lr * cr - li * ci, ei[r:r + 1] + lr * ci + li * cr
        pr8, pi8 = jnp.concatenate(rows_r, axis=0), jnp.concatenate(rows_i, axis=0)
        car_re[:, ln] = _bcast8(cr)
        car_im[:, ln] = _bcast8(ci)
        if sp_re is not None:
            sp_re[:, ln] = pr8
            sp_im[:, ln] = pi8

        def fix(i, _, k=k, ln=ln, pr8=pr8, pi8=pi8):
            qr, qi = _bcast8(p_re[pl.ds(i, 1), ln]), _bcast8(p_im[pl.ds(i, 1), ln])
            _seg_put(s_re, k, i, _seg_get(s_re, k, i) + qr * pr8 - qi * pi8)
            _seg_put(s_im, k, i, _seg_get(s_im, k, i) + qr * pi8 + qi * pr8)
            return 0

        lax.fori_loop(0, SEG_LEN, fix, 0, unroll=4)


def _scan_backward(g_re, g_im, s_re, s_im, sp_re, sp_im, p_re, p_im, car_re, car_im, acc_re, acc_im):
    for k in range(N_SLABS):
        ln = pl.ds(k * SLAB_LANES, SLAB_LANES)
        ar, ai = _bcast8(p_re[pl.ds(0, 1), ln]), -_bcast8(p_im[pl.ds(0, 1), ln])

        def step(j, s, k=k, ar=ar, ai=ai):
            i = SEG_LEN - 1 - j
            sr, si = s
            nr = ar * sr - ai * si + _seg_get(g_re, k, i)
            ni = ar * si + ai * sr + _seg_get(g_im, k, i)
            _seg_put(g_re, k, i, nr)
            _seg_put(g_im, k, i, ni)
            return nr, ni

        zero = jnp.zeros((8, SLAB_LANES), F32)
        er, ei = lax.fori_loop(0, SEG_LEN, step, (zero, zero), unroll=4)
        lr, li = p_re[pl.ds(SEG_LEN - 1, 1), ln], -p_im[pl.ds(SEG_LEN - 1, 1), ln]
        cr, ci = car_re[pl.ds(0, 1), ln], car_im[pl.ds(0, 1), ln]
        rows_r, rows_i = [None] * 8, [None] * 8
        for r in range(7, -1, -1):
            rows_r[r], rows_i[r] = cr, ci
            cr, ci = er[r:r + 1] + lr * cr - li * ci, ei[r:r + 1] + lr * ci + li * cr
        nr8, ni8 = jnp.concatenate(rows_r, axis=0), jnp.concatenate(rows_i, axis=0)
        car_re[:, ln] = _bcast8(cr)
        car_im[:, ln] = _bcast8(ci)

        def fix(i, acc, k=k, ln=ln, nr8=nr8, ni8=ni8):
            qr = _bcast8(p_re[pl.ds(SEG_LEN - 1 - i, 1), ln])
            qi = -_bcast8(p_im[pl.ds(SEG_LEN - 1 - i, 1), ln])
            gr = _seg_get(g_re, k, i) + qr * nr8 - qi * ni8
            gi = _seg_get(g_im, k, i) + qr * ni8 + qi * nr8
            _seg_put(g_re, k, i, gr)
            _seg_put(g_im, k, i, gi)
            return gr, gi

        def prod(xr, xi, gr, gi):
            return xr * gr + xi * gi, xr * gi - xi * gr

        gr, gi = fix(0, None)
        a_r, a_i = prod(sp_re[:, ln], sp_im[:, ln], gr, gi)

        def fix_acc(i, acc, k=k, fix=fix):
            gr, gi = fix(i, None)
            dr, di = prod(_seg_get(s_re, k, i - 1), _seg_get(s_im, k, i - 1), gr, gi)
            return acc[0] + dr, acc[1] + di

        a_r, a_i = lax.fori_loop(1, SEG_LEN, fix_acc, (a_r, a_i))
        acc_re[:, ln] += a_r
        acc_im[:, ln] += a_i


def _gelu_and_grad(y):
    cdf = 0.5 * (1.0 + lax.erf(y * (2.0 ** -0.5)))
    pdf = jnp.exp(-0.5 * y * y) * (1.0 / math.sqrt(2.0 * math.pi))
    return y * cdf, cdf + y * pdf


def _ssm_fwd_body(u_ref, bbd_ref, cbd_ref, pre_ref, pim_ref, d_ref, perm_ref, unperm_ref,
                  y_ref, gin_ref, cre_out, cim_out, s_re, s_im, car_re, car_im, yp):
    c = pl.program_id(0)

    @pl.when(c == 0)
    def _():
        car_re[...] = jnp.zeros_like(car_re)
        car_im[...] = jnp.zeros_like(car_im)

    cre_out[...] = car_re[...]
    cim_out[...] = car_im[...]
    u = u_ref[...]
    up = jnp.dot(perm_ref[...], u.astype(BF16), preferred_element_type=F32).astype(BF16)
    for k in range(N_SLABS):
        bu = jnp.dot(up[:, k * SLAB_CH:(k + 1) * SLAB_CH], bbd_ref[k], preferred_element_type=F32)
        _slab_put(s_re, k, bu[:, :SLAB_LANES])
        _slab_put(s_im, k, bu[:, SLAB_LANES:])
    _scan_forward(s_re, s_im, pre_ref, pim_ref, car_re, car_im)
    for k in range(N_SLABS):
        yp[:, pl.ds(k * SLAB_CH, SLAB_CH)] = (
            jnp.dot(_slab_get(s_re, k).astype(BF16), cbd_ref[k, :SLAB_LANES, :], preferred_element_type=F32)
            + jnp.dot(_slab_get(s_im, k).astype(BF16), cbd_ref[k, SLAB_LANES:, :], preferred_element_type=F32))
    y = _permute_exact(unperm_ref[...], yp[...], 3) + d_ref[...] * u
    y_ref[...] = y
    gin_ref[...] = _gelu_and_grad(y)[0].astype(BF16)


def _ssm_forward(proj, p_re, p_im, b_bd, c_bd, d_skip):
    t = proj.shape[0]
    nc = t // SSM_CHUNK
    perm = _segment_permutation()
    return pl.pallas_call(
        functools.partial(_ssm_fwd_body), name="ssm_fwd", grid=(nc,),
        in_specs=[pl.BlockSpec((SSM_CHUNK, D_SSM), lambda c: (c, C_U // D_SSM)),
                  _whole(b_bd.shape), _whole(c_bd.shape), _whole(p_re.shape), _whole(p_im.shape),
                  _whole((1, D_SSM)), _whole(perm.shape), _whole(perm.shape)],
        out_specs=[pl.BlockSpec((SSM_CHUNK, D_SSM), lambda c: (c, 0)),
                   pl.BlockSpec((SSM_CHUNK, D_SSM), lambda c: (c, 0)),
                   pl.BlockSpec((None, 8, N_LANES), lambda c: (c, 0, 0)),
                   pl.BlockSpec((None, 8, N_LANES), lambda c: (c, 0, 0))],
        out_shape=[jax.ShapeDtypeStruct((t, D_SSM), F32), jax.ShapeDtypeStruct((t, D_SSM), BF16),
                   jax.ShapeDtypeStruct((nc, 8, N_LANES), F32), jax.ShapeDtypeStruct((nc, 8, N_LANES), F32)],
        scratch_shapes=[_scan_buffer(), _scan_buffer(),
                        pltpu.VMEM((8, N_LANES), F32), pltpu.VMEM((8, N_LANES), F32),
                        pltpu.VMEM((SSM_CHUNK, D_SSM), F32)],
        compiler_params=_params(("arbitrary",)),
    )(proj, b_bd, c_bd, p_re, p_im, d_skip, jnp.asarray(perm, BF16), jnp.asarray(perm.T, BF16))


def _ssm_bwd_body(u_ref, dgin_ref, y_ref, cre_in, cim_in, bbd_ref, cbd_ref, pre_ref, pim_ref, d_ref, perm_ref,
                  unperm_ref, dproj_in,
                  du_ref, gb_ref, gc_ref, glr_ref, gli_ref, gd_ref,
                  s_re, s_im, g_re, g_im, sp_re, sp_im, car_re, car_im, gcar_re, gcar_im, acc_re, acc_im, dup):
    del dproj_in
    c = pl.program_id(0)
    nc = pl.num_programs(0)

    @pl.when(c == 0)
    def _():
        gcar_re[...] = jnp.zeros_like(gcar_re)
        gcar_im[...] = jnp.zeros_like(gcar_im)
        acc_re[...] = jnp.zeros_like(acc_re)
        acc_im[...] = jnp.zeros_like(acc_im)
        gb_ref[...] = jnp.zeros_like(gb_ref)
        gc_ref[...] = jnp.zeros_like(gc_ref)
        gd_ref[...] = jnp.zeros_like(gd_ref)

    car_re[...] = cre_in[...]
    car_im[...] = cim_in[...]
    u = u_ref[...]
    dy = dgin_ref[...] * _gelu_and_grad(y_ref[...])[1]
    gd_ref[...] += jnp.sum(dy * u, axis=0, keepdims=True)
    up = jnp.dot(perm_ref[...], u.astype(BF16), preferred_element_type=F32).astype(BF16)
    dyp = jnp.dot(perm_ref[...], dy.astype(BF16), preferred_element_type=F32).astype(BF16)
    for k in range(N_SLABS):
        ch = slice(k * SLAB_CH, (k + 1) * SLAB_CH)
        bu = jnp.dot(up[:, ch], bbd_ref[k], preferred_element_type=F32)
        _slab_put(s_re, k, bu[:, :SLAB_LANES])
        _slab_put(s_im, k, bu[:, SLAB_LANES:])
        ds = lax.dot_general(dyp[:, ch], cbd_ref[k], _DIMS["nt"], preferred_element_type=F32)
        _slab_put(g_re, k, ds[:, :SLAB_LANES])
        _slab_put(g_im, k, ds[:, SLAB_LANES:])
    _scan_forward(s_re, s_im, pre_ref, pim_ref, car_re, car_im, sp_re, sp_im)
    _scan_backward(g_re, g_im, s_re, s_im, sp_re, sp_im, pre_ref, pim_ref, gcar_re, gcar_im, acc_re, acc_im)
    for k in range(N_SLABS):
        ch = slice(k * SLAB_CH, (k + 1) * SLAB_CH)
        uk, dyk = up[:, ch], dyp[:, ch]
        sr, si = _slab_get(s_re, k).astype(BF16), _slab_get(s_im, k).astype(BF16)
        gr, gi = _slab_get(g_re, k).astype(BF16), _slab_get(g_im, k).astype(BF16)
        gc_ref[k, :SLAB_LANES, :] += lax.dot_general(sr, dyk, _DIMS["tn"], preferred_element_type=F32)
        gc_ref[k, SLAB_LANES:, :] += lax.dot_general(si, dyk, _DIMS["tn"], preferred_element_type=F32)
        gb_ref[k, :, :SLAB_LANES] += lax.dot_general(uk, gr, _DIMS["tn"], preferred_element_type=F32)
        gb_ref[k, :, SLAB_LANES:] += lax.dot_general(uk, gi, _DIMS["tn"], preferred_element_type=F32)
        dup[:, pl.ds(k * SLAB_CH, SLAB_CH)] = (
            lax.dot_general(gr, bbd_ref[k, :, :SLAB_LANES], _DIMS["nt"], preferred_element_type=F32)
            + lax.dot_general(gi, bbd_ref[k, :, SLAB_LANES:], _DIMS["nt"], preferred_element_type=F32))
    du = _permute_exact(unperm_ref[...], dup[...], 2) + d_ref[...] * dy
    du_ref[...] = du.astype(du_ref.dtype)

    @pl.when(c == nc - 1)
    def _():
        glr_ref[...] = jnp.sum(acc_re[...], axis=0, keepdims=True)
        gli_ref[...] = jnp.sum(acc_im[...], axis=0, keepdims=True)


def _ssm_backward(proj, dg_in, y_ssm, car_re, car_im, p_re, p_im, b_bd, c_bd, d_skip, dproj):
    t = proj.shape[0]
    nc = t // SSM_CHUNK
    rev = lambda c: nc - 1 - c
    big = _scan_buffer
    small = lambda: pltpu.VMEM((8, N_LANES), F32)
    perm = _segment_permutation()
    outs = pl.pallas_call(
        functools.partial(_ssm_bwd_body), name="ssm_bwd", grid=(nc,),
        in_specs=[pl.BlockSpec((SSM_CHUNK, D_SSM), lambda c: (rev(c), C_U // D_SSM)),
                  pl.BlockSpec((SSM_CHUNK, D_SSM), lambda c: (rev(c), 0)),
                  pl.BlockSpec((SSM_CHUNK, D_SSM), lambda c: (rev(c), 0)),
                  pl.BlockSpec((None, 8, N_LANES), lambda c: (rev(c), 0, 0)),
                  pl.BlockSpec((None, 8, N_LANES), lambda c: (rev(c), 0, 0)),
                  _whole(b_bd.shape), _whole(c_bd.shape), _whole(p_re.shape), _whole(p_im.shape),
                  _whole((1, D_SSM)), _whole(perm.shape), _whole(perm.shape), pl.BlockSpec(memory_space=pl.ANY)],
        out_specs=[pl.BlockSpec((SSM_CHUNK, D_SSM), lambda c: (rev(c), C_U // D_SSM)),
                   _whole(b_bd.shape), _whole(c_bd.shape), _whole((1, N_LANES)), _whole((1, N_LANES)),
                   _whole((1, D_SSM))],
        out_shape=[jax.ShapeDtypeStruct(dproj.shape, dproj.dtype),
                   jax.ShapeDtypeStruct(b_bd.shape, F32), jax.ShapeDtypeStruct(c_bd.shape, F32),
                   jax.ShapeDtypeStruct((1, N_LANES), F32), jax.ShapeDtypeStruct((1, N_LANES), F32),
                   jax.ShapeDtypeStruct((1, D_SSM), F32)],
        scratch_shapes=[big(), big(), big(), big()] + [small() for _ in range(8)]
        + [pltpu.VMEM((SSM_CHUNK, D_SSM), F32)],
        input_output_aliases={12: 0},
        compiler_params=_params(("arbitrary",)),
    )(proj, dg_in, y_ssm, car_re, car_im, b_bd, c_bd, p_re, p_im, d_skip, jnp.asarray(perm, BF16),
      jnp.asarray(perm.T, BF16), dproj)
    return outs


def _bucket_table():
    i = np.arange(BLOCK)[:, None]
    j = np.arange(2 * BLOCK)[None, :]
    dist = BLOCK + i - j
    ok = (dist >= 0) & (dist < WINDOW)
    max_exact = N_BUCKETS // 2
    d = np.maximum(dist, 1).astype(np.float32)
    large = max_exact + (np.log(d / max_exact) / math.log(MAX_DISTANCE / max_exact)
                         * (N_BUCKETS - max_exact)).astype(np.int32)
    large = np.minimum(large, N_BUCKETS - 1)
    bucket = np.where(dist < max_exact, dist, large)
    return np.where(ok, bucket, -1).astype(np.int32)


def _bias_build_body(table_ref, bucket_ref, o_ref):
    h = pl.program_id(0)
    bucket = bucket_ref[...]
    acc = jnp.full(bucket.shape, NEG_INF, F32)
    for b in range(N_BUCKETS):
        acc = jnp.where(bucket == b, table_ref[b, h], acc)
    o_ref[...] = acc


def _bias_build(rel_bias_table, bucket):
    return pl.pallas_call(
        functools.partial(_bias_build_body), name="bias_build", grid=(N_Q_HEADS,),
        in_specs=[pl.BlockSpec(memory_space=pltpu.SMEM), _whole(bucket.shape)],
        out_specs=pl.BlockSpec((None, BLOCK, 2 * BLOCK), lambda h: (h, 0, 0)),
        out_shape=jax.ShapeDtypeStruct((N_Q_HEADS, BLOCK, 2 * BLOCK), F32),
        compiler_params=_params(("arbitrary",)),
    )(rel_bias_table, bucket)


def _bias_grad_body(g_ref, bucket_ref, o_ref):
    bucket = bucket_ref[...]
    lane = lax.broadcasted_iota(jnp.int32, (N_BUCKETS, 128), 1)

    def head(h, out):
        g = g_ref[h]
        rows = [jnp.sum(jnp.where(bucket == b, g, 0.0), axis=0, keepdims=True) for b in range(N_BUCKETS)]
        colsum = jnp.sum(jnp.concatenate(rows, axis=0), axis=1, keepdims=True)
        return jnp.where(lane == h, colsum, out)

    o_ref[...] = lax.fori_loop(0, N_Q_HEADS, head, jnp.zeros((N_BUCKETS, 128), F32))


def _bias_grad(g_bias, bucket):
    out = pl.pallas_call(
        functools.partial(_bias_grad_body), name="bias_grad", grid=(1,),
        in_specs=[_whole(g_bias.shape), _whole(bucket.shape)],
        out_specs=_whole((N_BUCKETS, 128)),
        out_shape=jax.ShapeDtypeStruct((N_BUCKETS, 128), F32),
        compiler_params=_params(("arbitrary",)),
    )(g_bias, bucket)
    return out[:, :N_Q_HEADS]


def _head_logits(qh, kk, bias_h, first_block):
    s = lax.dot_general(qh, kk, _DIMS["nt"], preferred_element_type=F32) * ATTN_SCALE + bias_h
    col = lax.broadcasted_iota(jnp.int32, s.shape, 1)
    return jnp.where(jnp.logical_and(first_block, col < BLOCK), NEG_INF, s)


def _attn_fwd_body(sink_ref, q_ref, kp_ref, kc_ref, vp_ref, vc_ref, bias_ref, o_ref, lse_ref):
    n = pl.program_id(0)
    outs, lses = [], []
    for kv in range(N_KV_HEADS):
        cs = slice(kv * HEAD_DIM, (kv + 1) * HEAD_DIM)
        kk = jnp.concatenate([kp_ref[:, cs], kc_ref[:, cs]], axis=0).astype(BF16)
        vv = jnp.concatenate([vp_ref[:, cs], vc_ref[:, cs]], axis=0).astype(BF16)
        for g in range(Q_PER_KV):
            h = kv * Q_PER_KV + g
            qh = q_ref[:, h * HEAD_DIM:(h + 1) * HEAD_DIM].astype(BF16)
            s = _head_logits(qh, kk, bias_ref[h], n == 0)
            sink = sink_ref[0, h]
            m = jnp.maximum(jnp.max(s, axis=1, keepdims=True), sink)
            p = jnp.exp(s - m)
            den = jnp.sum(p, axis=1, keepdims=True) + jnp.exp(sink - m)
            p = p / den
            outs.append(jnp.dot(p.astype(BF16), vv, preferred_element_type=F32))
            lses.append(m + jnp.log(den))
    o_ref[...] = jnp.concatenate(outs, axis=1)
    lse_ref[...] = jnp.concatenate(lses, axis=1)


def _attn_specs(nb):
    prev = lambda n: jnp.maximum(jnp.minimum(n, nb - 1) - 1, 0)
    cur = lambda n: jnp.minimum(n, nb - 1)
    return [pl.BlockSpec((BLOCK, D_ATTN), lambda n: (cur(n), C_Q // D_ATTN)),
            pl.BlockSpec((BLOCK, D_KV), lambda n: (prev(n), C_K // D_KV)),
            pl.BlockSpec((BLOCK, D_KV), lambda n: (cur(n), C_K // D_KV)),
            pl.BlockSpec((BLOCK, D_KV), lambda n: (prev(n), C_V // D_KV)),
            pl.BlockSpec((BLOCK, D_KV), lambda n: (cur(n), C_V // D_KV))]


def _attn_forward(proj, sinks, bias):
    t = proj.shape[0]
    nb = t // BLOCK
    return pl.pallas_call(
        functools.partial(_attn_fwd_body), name="attn_fwd", grid=(nb,),
        in_specs=[pl.BlockSpec(memory_space=pltpu.SMEM)] + _attn_specs(nb) + [_whole(bias.shape)],
        out_specs=[pl.BlockSpec((BLOCK, D_ATTN), lambda n: (n, 0)),
                   pl.BlockSpec((BLOCK, N_Q_HEADS), lambda n: (n, 0))],
        out_shape=[jax.ShapeDtypeStruct((t, D_ATTN), F32), jax.ShapeDtypeStruct((t, N_Q_HEADS), F32)],
        compiler_params=_params(("parallel",)),
    )(sinks, proj, proj, proj, proj, proj, bias)


def _attn_bwd_body(sink_ref, q_ref, kp_ref, kc_ref, vp_ref, vc_ref, bias_ref, do_ref, o_ref, lse_ref, dproj_in,
                   dq_ref, dkv_ref, gbias_ref, gsink_ref, carry_ref, *, nb):
    del dproj_in
    n = pl.program_id(0)

    @pl.when(n == 0)
    def _():
        gbias_ref[...] = jnp.zeros_like(gbias_ref)
        gsink_ref[...] = jnp.zeros_like(gsink_ref)
        carry_ref[...] = jnp.zeros_like(carry_ref)

    @pl.when(n < nb)
    def _():
        lane = lax.broadcasted_iota(jnp.int32, (1, 128), 1)
        dqs, dks, dvs = [], [], []
        gsink = jnp.zeros((1, 128), F32)
        for kv in range(N_KV_HEADS):
            cs = slice(kv * HEAD_DIM, (kv + 1) * HEAD_DIM)
            kk = jnp.concatenate([kp_ref[:, cs], kc_ref[:, cs]], axis=0).astype(BF16)
            vv = jnp.concatenate([vp_ref[:, cs], vc_ref[:, cs]], axis=0).astype(BF16)
            dk = jnp.zeros((2 * BLOCK, HEAD_DIM), F32)
            dv = jnp.zeros((2 * BLOCK, HEAD_DIM), F32)
            for g in range(Q_PER_KV):
                h = kv * Q_PER_KV + g
                hs = slice(h * HEAD_DIM, (h + 1) * HEAD_DIM)
                qh = q_ref[:, hs].astype(BF16)
                s = _head_logits(qh, kk, bias_ref[h], n == 0)
                lse = lse_ref[:, h:h + 1]
                p = jnp.exp(s - lse)
                do = do_ref[:, hs]
                delta = jnp.sum(do * o_ref[:, hs], axis=1, keepdims=True)
                dob = do.astype(BF16)
                dp = lax.dot_general(dob, vv, _DIMS["nt"], preferred_element_type=F32)
                dl = p * (dp - delta)
                gbias_ref[h] += dl
                psink = jnp.exp(sink_ref[0, h] - lse)
                gsink = gsink + jnp.where(lane == h, -jnp.sum(psink * delta), 0.0)
                dlb = dl.astype(BF16)
                dqs.append(jnp.dot(dlb, kk, preferred_element_type=F32) * ATTN_SCALE)
                dk = dk + lax.dot_general(dlb, qh, _DIMS["tn"], preferred_element_type=F32) * ATTN_SCALE
                dv = dv + lax.dot_general(p.astype(BF16), dob, _DIMS["tn"], preferred_element_type=F32)
            dks.append(dk)
            dvs.append(dv)
        dq_ref[...] = jnp.concatenate(dqs, axis=1).astype(dq_ref.dtype)
        gsink_ref[...] += gsink
        dkv = jnp.concatenate(dks + dvs, axis=1)
        dkv_ref[...] = (carry_ref[...] + dkv[:BLOCK]).astype(dkv_ref.dtype)
        carry_ref[...] = dkv[BLOCK:]

    @pl.when(n == nb)
    def _():
        dkv_ref[...] = carry_ref[...].astype(dkv_ref.dtype)


def _attn_backward(proj, sinks, bias, d_attn, attn, lse, dproj):
    t = proj.shape[0]
    nb = t // BLOCK
    cur = lambda n: jnp.minimum(n, nb - 1)
    return pl.pallas_call(
        functools.partial(_attn_bwd_body, nb=nb), name="attn_bwd", grid=(nb + 1,),
        in_specs=[pl.BlockSpec(memory_space=pltpu.SMEM)] + _attn_specs(nb) + [
            _whole(bias.shape),
            pl.BlockSpec((BLOCK, D_ATTN), lambda n: (cur(n), 0)),
            pl.BlockSpec((BLOCK, D_ATTN), lambda n: (cur(n), 0)),
            pl.BlockSpec((BLOCK, N_Q_HEADS), lambda n: (cur(n), 0)),
            pl.BlockSpec(memory_space=pl.ANY)],
        out_specs=[pl.BlockSpec((BLOCK, D_ATTN), lambda n: (cur(n), C_Q // D_ATTN)),
                   pl.BlockSpec((BLOCK, 2 * D_KV), lambda n: (jnp.maximum(n - 1, 0), 0)),
                   _whole(bias.shape), _whole((1, 128))],
        out_shape=[jax.ShapeDtypeStruct(dproj.shape, dproj.dtype), jax.ShapeDtypeStruct((t, 2 * D_KV), dproj.dtype),
                   jax.ShapeDtypeStruct(bias.shape, F32), jax.ShapeDtypeStruct((1, 128), F32)],
        scratch_shapes=[pltpu.VMEM((BLOCK, 2 * D_KV), F32)],
        input_output_aliases={10: 0},
        compiler_params=_params(("arbitrary",)),
    )(sinks, proj, proj, proj, proj, proj, bias, d_attn, attn, lse, dproj)


ROWS = 256


def _rowwise(body, name, t, ins, outs, aliases=None):
    rows = min(ROWS, t)

    def col_spec(w, c0):
        if c0 % w == 0:
            return pl.BlockSpec((rows, w), lambda i: (i, c0 // w))
        return pl.BlockSpec((pl.Element(rows), pl.Element(w)), lambda i: (i * rows, c0))

    in_specs, args = [], []
    for a, w, c0 in ins:
        args.append(a)
        if w is None:
            in_specs.append(pl.BlockSpec(memory_space=pl.ANY) if c0 == "any" else _whole(a.shape))
        else:
            in_specs.append(col_spec(w, c0))
    out_specs, out_shape = [], []
    for shape, dtype, w, c0 in outs:
        out_shape.append(jax.ShapeDtypeStruct(shape, dtype))
        out_specs.append(_whole(shape) if w is None else col_spec(w, c0))
    accum = any(o[2] is None for o in outs)
    return pl.pallas_call(
        body, name=name, grid=(t // rows,), in_specs=in_specs, out_specs=out_specs, out_shape=out_shape,
        input_output_aliases=aliases or {},
        compiler_params=_params(("arbitrary",) if accum else ("parallel",)),
    )(*args)


def _ssm_gate_fwd_body(glu_ref, z_ref, h_ref):
    a, b = glu_ref[:, :D_SSM], glu_ref[:, D_SSM:]
    h_ref[...] = ((a * _sigmoid(b)) * _silu_and_grad(z_ref[...])[0]).astype(h_ref.dtype)


def _ssm_gate_bwd_body(dh_ref, glu_ref, z_ref, dproj_in, dglu_ref, dz_ref):
    del dproj_in
    a, b = glu_ref[:, :D_SSM], glu_ref[:, D_SSM:]
    sb = _sigmoid(b)
    silu, dsilu = _silu_and_grad(z_ref[...])
    dh = dh_ref[...]
    dg = dh * silu
    dz_ref[...] = (dh * (a * sb) * dsilu).astype(dz_ref.dtype)
    dglu_ref[:, :D_SSM] = (dg * sb).astype(dglu_ref.dtype)
    dglu_ref[:, D_SSM:] = (dg * a * sb * (1.0 - sb)).astype(dglu_ref.dtype)


def _attn_gate_fwd_body(attn_ref, z_ref, h_ref):
    h_ref[...] = (attn_ref[...] * _silu_and_grad(z_ref[...])[0]).astype(h_ref.dtype)


def _attn_gate_bwd_body(dh_ref, attn_ref, z_ref, dproj_in, dattn_ref, dz_ref):
    del dproj_in
    silu, dsilu = _silu_and_grad(z_ref[...])
    dh = dh_ref[...]
    dattn_ref[...] = dh * silu
    dz_ref[...] = (dh * attn_ref[...] * dsilu).astype(dz_ref.dtype)


def _merge_fwd_body(bs_ref, ba_ref, gl_ref, m_ref):
    gs = _sigmoid(gl_ref[:, :D_MODEL])
    ga = _sigmoid(gl_ref[:, D_MODEL:])
    m_ref[...] = (gs * bs_ref[...] + ga * ba_ref[...]).astype(m_ref.dtype)


def _merge_bwd_body(dm_ref, bs_ref, ba_ref, gl_ref, dbs_ref, dba_ref, dgl_ref):
    gs = _sigmoid(gl_ref[:, :D_MODEL])
    ga = _sigmoid(gl_ref[:, D_MODEL:])
    dm = dm_ref[...]
    dbs_ref[...] = (dm * gs).astype(dbs_ref.dtype)
    dba_ref[...] = (dm * ga).astype(dba_ref.dtype)
    dgl_ref[:, :D_MODEL] = (dm * bs_ref[...] * gs * (1.0 - gs)).astype(dgl_ref.dtype)
    dgl_ref[:, D_MODEL:] = (dm * ba_ref[...] * ga * (1.0 - ga)).astype(dgl_ref.dtype)


def _ln_loss_body(x_ref, o_ref, tgt_ref, gain_ref, bias_ref, dr_ref, loss_ref, dgain_ref, dbias_ref):
    @pl.when(pl.program_id(0) == 0)
    def _():
        loss_ref[...] = jnp.zeros_like(loss_ref)
        dgain_ref[...] = jnp.zeros_like(dgain_ref)
        dbias_ref[...] = jnp.zeros_like(dbias_ref)

    r = ALPHA * x_ref[...] + o_ref[...]
    mu = jnp.mean(r, axis=1, keepdims=True)
    rc = r - mu
    var = jnp.mean(rc * rc, axis=1, keepdims=True)
    rstd = lax.rsqrt(var + LN_EPS)
    xhat = rc * rstd
    gain = gain_ref[...]
    err = xhat * gain + bias_ref[...] - tgt_ref[...]
    loss_ref[...] += 0.5 * jnp.sum(jnp.mean(err * err, axis=1, keepdims=True), axis=0, keepdims=True)
    dy = err * (1.0 / D_MODEL)
    dgain_ref[...] += jnp.sum(dy * xhat, axis=0, keepdims=True)
    dbias_ref[...] += jnp.sum(dy, axis=0, keepdims=True)
    dxhat = dy * gain
    m1 = jnp.mean(dxhat, axis=1, keepdims=True)
    m2 = jnp.mean(dxhat * xhat, axis=1, keepdims=True)
    dr_ref[...] = rstd * (dxhat - m1 - xhat * m2)


def _place_body(piece_ref, dproj_in, o_ref):
    del dproj_in
    o_ref[...] = piece_ref[...]


def _adamw_body(*refs, n_parts):
    w_ref, m_ref, v_ref = refs[:3]
    parts = refs[3:3 + n_parts]
    g_ref, d_ref, nm_ref, nv_ref = refs[3 + n_parts:]
    g = parts[0][...].astype(F32)
    for p in parts[1:]:
        g = g + p[...].astype(F32)
    m = ADAM_B1 * m_ref[...] + (1.0 - ADAM_B1) * g
    v = ADAM_B2 * v_ref[...] + (1.0 - ADAM_B2) * (g * g)
    m_hat = m / (1.0 - ADAM_B1 ** ADAM_STEP)
    v_hat = v / (1.0 - ADAM_B2 ** ADAM_STEP)
    g_ref[...] = g
    d_ref[...] = -ADAM_LR * (m_hat / (jnp.sqrt(v_hat) + ADAM_EPS) + ADAM_WD * w_ref[...])
    nm_ref[...] = m
    nv_ref[...] = v


def _adamw(w, m, v, parts, *, name, rows=256):
    shape = w.shape
    w2, m2, v2 = (a.reshape(-1, shape[-1]) for a in (w, m, v))
    parts = [p.reshape(w2.shape) for p in parts]
    r, c = w2.shape
    rows = rows if r % rows == 0 else r
    spec = pl.BlockSpec((rows, c), lambda i: (i, 0))
    outs = pl.pallas_call(
        functools.partial(_adamw_body, n_parts=len(parts)), name=name, grid=(r // rows,),
        in_specs=[spec] * (3 + len(parts)), out_specs=[spec] * 4,
        out_shape=[jax.ShapeDtypeStruct((r, c), F32)] * 4,
        compiler_params=_params(("parallel",)),
    )(w2, m2, v2, *parts)
    return tuple(o.reshape(shape) for o in outs)


BIG = ("w_in", "w_glu", "w_branch_ssm", "w_branch_attn", "w_out")
SHARD_AXIS = dict(w_in=1, w_glu=1, w_branch_ssm=1, w_branch_attn=1, w_out=0)
HBM = pl.BlockSpec(memory_space=pl.ANY)


def _position():
    x, y, c = lax.axis_index("x"), lax.axis_index("y"), lax.axis_index("c")
    other_chips = [(1 - x, y), (x, 1 - y), (1 - x, 1 - y)]
    return x, y, c, other_chips


def _window(ref, axis, shard, n_shards, half=None):
    rows, cols = ref.shape[-2:]
    sel = [slice(None), slice(None)]
    size = ref.shape[-2 + axis] // n_shards
    sel[axis] = pl.ds(pl.multiple_of(shard * size, 128), size)
    if half is not None:
        hsize = ref.shape[-1 - axis] // 2
        sel[1 - axis] = pl.ds(pl.multiple_of(half * hsize, 128), hsize)
    return ref.at[tuple(sel)]


def _half(ref, axis, half):
    hsize = ref.shape[-1 - axis] // 2
    sel = [slice(None), slice(None)]
    sel[1 - axis] = pl.ds(pl.multiple_of(half * hsize, 128), hsize)
    return ref.at[tuple(sel)]


def _remote(src, dst, send_sem, recv_sem, device):
    return pltpu.make_async_remote_copy(src_ref=src, dst_ref=dst, send_sem=send_sem, recv_sem=recv_sem,
                                        device_id=device, device_id_type=MESH)


def _gather_body(*refs, axes):
    n = len(axes)
    shards, outs = refs[:n], refs[n:2 * n]
    send_sems, recv_sems, fsend_sems, frecv_sems, local_sems = refs[2 * n:]
    x, y, c, chips = _position()
    me = 2 * x + y
    local, sends, passes = [], [], []
    for w, ax in enumerate(axes):
        cp = pltpu.make_async_copy(shards[w], _window(outs[w], ax, me, N_CHIPS), local_sems.at[w])
        cp.start()
        local.append(cp)
    for w, ax in enumerate(axes):
        for r, (px, py) in enumerate(chips):
            cp = _remote(_half(shards[w], ax, c), _window(outs[w], ax, me, N_CHIPS, c),
                         send_sems.at[3 * w + r], recv_sems.at[3 * w + r], (px, py, c))
            cp.start()
            sends.append(cp)
    for w, ax in enumerate(axes):
        for r, (px, py) in enumerate(chips):
            landed = _window(outs[w], ax, 2 * px + py, N_CHIPS, c)
            _remote(landed, landed, send_sems.at[3 * w + r], recv_sems.at[3 * w + r], (px, py, c)).wait_recv()
            cp = _remote(landed, landed, fsend_sems.at[3 * w + r], frecv_sems.at[3 * w + r], (x, y, 1 - c))
            cp.start()
            passes.append(cp)
    for w, ax in enumerate(axes):
        for r, (px, py) in enumerate(chips):
            other = _window(outs[w], ax, 2 * px + py, N_CHIPS, 1 - c)
            _remote(other, other, fsend_sems.at[3 * w + r], frecv_sems.at[3 * w + r], (x, y, 1 - c)).wait_recv()
    for cp in sends + passes:
        cp.wait_send()
    for cp in local:
        cp.wait()


def _all_gather(shards):
    axes = tuple(SHARD_AXIS[k] for k in BIG)
    out_shape = []
    for k in BIG:
        s = list(shards[k].shape)
        s[SHARD_AXIS[k]] *= N_CHIPS
        out_shape.append(jax.ShapeDtypeStruct(tuple(s), shards[k].dtype))
    n = len(BIG)
    outs = pl.pallas_call(
        functools.partial(_gather_body, axes=axes), name="gather_weights",
        in_specs=[HBM] * n, out_specs=[HBM] * n, out_shape=out_shape,
        scratch_shapes=[pltpu.SemaphoreType.DMA((3 * n,))] * 4 + [pltpu.SemaphoreType.DMA((n,))],
    )(*[shards[k] for k in BIG])
    return dict(zip(BIG, outs))


def _pair_exchange_body(*refs, axes):
    n = len(axes)
    grads, mine, theirs = refs[:n], refs[n:2 * n], refs[2 * n:3 * n]
    send_sems, recv_sems, local_sems = refs[3 * n:]
    x, y, c, _ = _position()
    copies, local = [], []
    for w, ax in enumerate(axes):
        cp = _remote(_half(grads[w], ax, 1 - c), theirs[w], send_sems.at[w], recv_sems.at[w], (x, y, 1 - c))
        cp.start()
        copies.append(cp)
        lc = pltpu.make_async_copy(_half(grads[w], ax, c), mine[w], local_sems.at[w])
        lc.start()
        local.append(lc)
    for cp in copies:
        cp.wait_recv()
    for cp in copies:
        cp.wait_send()
    for lc in local:
        lc.wait()


def _chip_exchange_body(*refs, axes):
    n = len(axes)
    sums, slots = refs[:n], refs[n:2 * n]
    send_sems, recv_sems, local_sems = refs[2 * n:]
    x, y, c, chips = _position()
    me = 2 * x + y
    copies, local = [], []
    for w, ax in enumerate(axes):
        for r, (px, py) in enumerate(chips):
            cp = _remote(_window(sums[w], ax, 2 * px + py, N_CHIPS), slots[w].at[r],
                         send_sems.at[3 * w + r], recv_sems.at[3 * w + r], (px, py, c))
            cp.start()
            copies.append(cp)
        lc = pltpu.make_async_copy(_window(sums[w], ax, me, N_CHIPS), slots[w].at[3], local_sems.at[w])
        lc.start()
        local.append(lc)
    for cp in copies:
        cp.wait_recv()
    for cp in copies:
        cp.wait_send()
    for lc in local:
        lc.wait()


def _pair_share_body(*refs, axes):
    n = len(axes)
    halves, outs = refs[:n], refs[n:2 * n]
    send_sems, recv_sems, local_sems = refs[2 * n:]
    x, y, c, _ = _position()
    copies, local = [], []
    for w, ax in enumerate(axes):
        cp = _remote(halves[w], _half(outs[w], ax, c), send_sems.at[w], recv_sems.at[w], (x, y, 1 - c))
        cp.start()
        copies.append(cp)
        lc = pltpu.make_async_copy(halves[w], _half(outs[w], ax, c), local_sems.at[w])
        lc.start()
        local.append(lc)
    for w, ax in enumerate(axes):
        other = _half(outs[w], ax, 1 - c)
        _remote(other, other, send_sems.at[w], recv_sems.at[w], (x, y, 1 - c)).wait_recv()
    for cp in copies:
        cp.wait_send()
    for lc in local:
        lc.wait()


def _add_body(*refs, n_parts):
    acc = refs[0][...]
    for p in refs[1:n_parts]:
        acc = acc + p[...]
    refs[n_parts][...] = acc


def _add(parts, *, name, rows=256):
    shape = parts[0][0].shape[1:] if isinstance(parts[0], tuple) else parts[0].shape
    r, c = shape
    rows = min(rows, r)
    in_specs, args = [], []
    for p in parts:
        if isinstance(p, tuple):
            args.append(p[0])
            in_specs.append(pl.BlockSpec((None, rows, c), lambda i, s=p[1]: (s, i, 0)))
        else:
            args.append(p)
            in_specs.append(pl.BlockSpec((rows, c), lambda i: (i, 0)))
    return pl.pallas_call(
        functools.partial(_add_body, n_parts=len(parts)), name=name, grid=(r // rows,),
        in_specs=in_specs, out_specs=pl.BlockSpec((rows, c), lambda i: (i, 0)),
        out_shape=jax.ShapeDtypeStruct((r, c), F32),
        compiler_params=_params(("parallel",)),
    )(*args)


def _comm_call(body, name, ins, out_shape, n_sem_arrays, n_sems):
    axes = tuple(SHARD_AXIS[k] for k in BIG)
    return pl.pallas_call(
        functools.partial(body, axes=axes), name=name,
        in_specs=[HBM] * len(ins), out_specs=[HBM] * len(out_shape), out_shape=out_shape,
        scratch_shapes=[pltpu.SemaphoreType.DMA((n_sems,))] * n_sem_arrays + [pltpu.SemaphoreType.DMA((len(BIG),))],
    )(*ins)


def _reduce_grads(grads):
    n = len(BIG)
    f32 = lambda s: jax.ShapeDtypeStruct(tuple(s), F32)

    def half_shape(shape, ax):
        s = list(shape)
        s[1 - ax] //= 2
        return s

    def shard_shape(shape, ax):
        s = list(shape)
        s[ax] //= N_CHIPS
        return s

    full = [grads[k].shape for k in BIG]
    axes = [SHARD_AXIS[k] for k in BIG]
    halves = [half_shape(s, ax) for s, ax in zip(full, axes)]
    outs = _comm_call(_pair_exchange_body, "pair_exchange", [grads[k] for k in BIG],
                      [f32(s) for s in halves] * 2, 2, n)
    pair = [_add([outs[w], outs[n + w]], name="pair_add_" + BIG[w]) for w in range(n)]
    quarter = [shard_shape(s, ax) for s, ax in zip(halves, axes)]
    slots = _comm_call(_chip_exchange_body, "chip_exchange", pair, [f32([4] + s) for s in quarter], 2, 3 * n)
    reduced = [_add([(slots[w], 3), (slots[w], 0), (slots[w], 1), (slots[w], 2)], name="chip_add_" + BIG[w])
               for w in range(n)]
    shards = _comm_call(_pair_share_body, "pair_share", reduced,
                        [f32(shard_shape(s, ax)) for s, ax in zip(full, axes)], 2, n)
    return dict(zip(BIG, shards))


SMALL = (("ssm_lambda_re", (1, 64, 64)), ("ssm_lambda_im", (1, 64, 64)), ("ssm_b_re", (1, 64, 64, 16)),
         ("ssm_b_im", (1, 64, 64, 16)), ("ssm_c_re", (1, 64, 16, 64)), ("ssm_c_im", (1, 64, 16, 64)),
         ("ssm_d", (1, 1024)), ("ssm_log_step", (1, 64)), ("attn_sinks", (1, 16)), ("rel_bias_table", (32, 16)),
         ("ln_gain", (1, 2048)), ("ln_bias", (1, 2048)))
SMALL_SIZE = sum(int(np.prod(s)) for _, s in SMALL)
PACK_ROWS = -(-(SMALL_SIZE + 1) // (8 * 128)) * 8


def _pack(values, extra=None):
    flat = [values[k].reshape(-1).astype(F32) for k, _ in SMALL]
    flat.append(jnp.zeros((1,), F32) if extra is None else extra.reshape(1))
    flat.append(jnp.zeros((PACK_ROWS * 128 - SMALL_SIZE - 1,), F32))
    return jnp.concatenate(flat).reshape(PACK_ROWS, 128)


def _unpack(packed):
    flat = packed.reshape(-1)
    out, off = {}, 0
    for k, s in SMALL:
        size = int(np.prod(s))
        out[k] = flat[off:off + size].reshape(s)
        off += size
    return out, flat[off]


def _small_allreduce_body(p_ref, o_ref, buf, send_sems, recv_sems):
    x, y, c, _ = _position()
    me = 4 * x + 2 * y + c
    buf[0] = p_ref[...]
    copies = []
    for r in range(1, 8):
        peer = tuple(1 - v if (r >> s) & 1 else v for v, s in ((x, 2), (y, 1), (c, 0)))
        cp = _remote(p_ref, buf.at[r], send_sems.at[r - 1], recv_sems.at[r - 1], peer)
        cp.start()
        copies.append(cp)
    for cp in copies:
        cp.wait_recv()
    for cp in copies:
        cp.wait_send()
    acc = buf[me]
    for s in range(1, 8):
        acc = acc + buf[jnp.bitwise_xor(me, s)]
    o_ref[...] = acc


def _small_allreduce(packed):
    vm = pl.BlockSpec(memory_space=pltpu.VMEM)
    return pl.pallas_call(
        functools.partial(_small_allreduce_body), name="small_allreduce",
        in_specs=[vm], out_specs=vm, out_shape=jax.ShapeDtypeStruct(packed.shape, F32),
        scratch_shapes=[pltpu.VMEM((8,) + packed.shape, F32), pltpu.SemaphoreType.DMA((7,)),
                        pltpu.SemaphoreType.DMA((7,))],
        compiler_params=pltpu.CompilerParams(vmem_limit_bytes=VMEM_LIMIT),
    )(packed)


def _local_step(x, target, w_in, w_glu, w_bs, w_ba, w_out, lam_re, lam_im, b_re, b_im, c_re, c_im, d_skip,
                log_step, sinks, rel_bias_table, ln_gain, ln_bias):
    t = x.shape[0]
    bucket = jnp.asarray(_bucket_table())
    p_re, p_im, b_bd, c_bd = _ssm_prepare(lam_re, lam_im, b_re, b_im, c_re, c_im, log_step)
    bias = _bias_build(rel_bias_table, bucket)

    xb = _cast(x, BF16, name="cast_x")
    proj = _matmul(xb, w_in, "nn", name="proj", tm=1024, tn=512)
    y_ssm, g_in, car_re, car_im = _ssm_forward(proj, p_re, p_im, b_bd, c_bd, d_skip)
    glu = _matmul(g_in, w_glu, "nn", name="glu", tm=1024, tn=512)
    (h_ssm,) = _rowwise(functools.partial(_ssm_gate_fwd_body), "ssm_gate_fwd", t,
                        [(glu, 2 * D_SSM, 0), (proj, D_SSM, C_ZS)], [((t, D_SSM), BF16, D_SSM, 0)])
    attn, lse = _attn_forward(proj, sinks, bias)
    (h_attn,) = _rowwise(functools.partial(_attn_gate_fwd_body), "attn_gate_fwd", t,
                         [(attn, D_ATTN, 0), (proj, D_ATTN, C_ZA)], [((t, D_ATTN), BF16, D_ATTN, 0)])
    bs = _matmul(h_ssm, w_bs, "nn", name="branch_ssm", tm=1024, tn=512)
    ba = _matmul(h_attn, w_ba, "nn", name="branch_attn", tm=1024, tn=512)
    gl_in = (proj, 2 * D_MODEL, C_GL)
    (merged,) = _rowwise(functools.partial(_merge_fwd_body), "merge_fwd", t,
                         [(bs, D_MODEL, 0), (ba, D_MODEL, 0), gl_in], [((t, D_MODEL), BF16, D_MODEL, 0)])
    out = _matmul(merged, w_out, "nn", name="out_proj", tm=1024, tn=512)
    dr, loss, g_gain, g_bias = _rowwise(
        functools.partial(_ln_loss_body), "ln_loss", t,
        [(x, D_MODEL, 0), (out, D_MODEL, 0), (target, D_MODEL, 0), (ln_gain, None, 0), (ln_bias, None, 0)],
        [((t, D_MODEL), F32, D_MODEL, 0), ((1, 1), F32, None, 0), ((1, D_MODEL), F32, None, 0),
         ((1, D_MODEL), F32, None, 0)])

    drb = _cast(dr, BF16, name="cast_dr")
    g_w_out = _matmul(merged, drb, "tn", name="g_w_out", tm=1024, tn=512, tk=1024)
    d_merged = _matmul(drb, w_out, "nt", name="d_merged", tm=1024, tn=512)
    d_bs, d_ba, dproj = _rowwise(
        functools.partial(_merge_bwd_body), "merge_bwd", t,
        [(d_merged, D_MODEL, 0), (bs, D_MODEL, 0), (ba, D_MODEL, 0), gl_in],
        [((t, D_MODEL), BF16, D_MODEL, 0), ((t, D_MODEL), BF16, D_MODEL, 0),
         ((t, D_IN), BF16, 2 * D_MODEL, C_GL)])
    g_w_bs = _matmul(h_ssm, d_bs, "tn", name="g_w_branch_ssm", tm=1024, tn=512, tk=1024)
    d_h_ssm = _matmul(d_bs, w_bs, "nt", name="d_h_ssm", tm=1024, tn=512)
    g_w_ba = _matmul(h_attn, d_ba, "tn", name="g_w_branch_attn", tm=1024, tn=512, tk=1024)
    d_h_attn = _matmul(d_ba, w_ba, "nt", name="d_h_attn", tm=1024, tn=512)

    d_attn, dproj = _rowwise(
        functools.partial(_attn_gate_bwd_body), "attn_gate_bwd", t,
        [(d_h_attn, D_ATTN, 0), (attn, D_ATTN, 0), (proj, D_ATTN, C_ZA), (dproj, None, "any")],
        [((t, D_ATTN), F32, D_ATTN, 0), ((t, D_IN), BF16, D_ATTN, C_ZA)], aliases={3: 1})
    dproj, d_kv, g_bias_full, g_sinks = _attn_backward(proj, sinks, bias, d_attn, attn, lse, dproj)
    (dproj,) = _rowwise(functools.partial(_place_body), "place_dkv", t,
                        [(d_kv, 2 * D_KV, 0), (dproj, None, "any")],
                        [((t, D_IN), BF16, 2 * D_KV, C_K)], aliases={1: 0})
    g_table = _bias_grad(g_bias_full, bucket)

    d_glu, dproj = _rowwise(
        functools.partial(_ssm_gate_bwd_body), "ssm_gate_bwd", t,
        [(d_h_ssm, D_SSM, 0), (glu, 2 * D_SSM, 0), (proj, D_SSM, C_ZS), (dproj, None, "any")],
        [((t, 2 * D_SSM), BF16, 2 * D_SSM, 0), ((t, D_IN), BF16, D_SSM, C_ZS)], aliases={3: 1})
    g_w_glu = _matmul(g_in, d_glu, "tn", name="g_w_glu", tm=1024, tn=512, tk=1024)
    d_g_in = _matmul(d_glu, w_glu, "nt", name="d_g_in", tm=1024, tn=512)
    dproj, g_bbd, g_cbd, g_lam_re, g_lam_im, g_d = _ssm_backward(
        proj, d_g_in, y_ssm, car_re, car_im, p_re, p_im, b_bd, c_bd, d_skip, dproj)
    g_lr, g_li, g_br, g_bi, g_cr, g_ci, g_ls = _ssm_param_grads(
        lam_re, lam_im, b_re, b_im, log_step, g_lam_re, g_lam_im, g_bbd, g_cbd)

    g_w_in = _matmul(xb, dproj, "tn", name="g_w_in", tm=1024, tn=512, tk=1024)
    grad_x = _matmul(dproj, w_in, "nt", name="grad_x", tm=1024, tn=512, tk=2176, res=dr, res_scale=ALPHA)

    big = dict(w_in=g_w_in, w_glu=g_w_glu, w_branch_ssm=g_w_bs, w_branch_attn=g_w_ba, w_out=g_w_out)
    small = dict(ssm_lambda_re=g_lr, ssm_lambda_im=g_li, ssm_b_re=g_br, ssm_b_im=g_bi, ssm_c_re=g_cr,
                 ssm_c_im=g_ci, ssm_d=g_d, ssm_log_step=g_ls, attn_sinks=g_sinks[:, :N_Q_HEADS],
                 rel_bias_table=g_table, ln_gain=g_gain, ln_bias=g_bias)
    return loss, grad_x, big, small


WEIGHTS = ("w_in", "ssm_lambda_re", "ssm_lambda_im", "ssm_b_re", "ssm_b_im", "ssm_c_re", "ssm_c_im", "ssm_d",
           "ssm_log_step", "w_glu", "attn_sinks", "rel_bias_table", "w_branch_ssm", "w_branch_attn", "w_out",
           "ln_gain", "ln_bias")


def kernel(x, w_in, ssm_lambda_re, ssm_lambda_im, ssm_b_re, ssm_b_im, ssm_c_re, ssm_c_im, ssm_d, ssm_log_step, w_glu, attn_sinks, rel_bias_table, w_branch_ssm, w_branch_attn, w_out, ln_gain, ln_bias, loss_target, m_w_in, m_ssm_lambda_re, m_ssm_lambda_im, m_ssm_b_re, m_ssm_b_im, m_ssm_c_re, m_ssm_c_im, m_ssm_d, m_ssm_log_step, m_w_glu, m_attn_sinks, m_rel_bias_table, m_w_branch_ssm, m_w_branch_attn, m_w_out, m_ln_gain, m_ln_bias, v_w_in, v_ssm_lambda_re, v_ssm_lambda_im, v_ssm_b_re, v_ssm_b_im, v_ssm_c_re, v_ssm_c_im, v_ssm_d, v_ssm_log_step, v_w_glu, v_attn_sinks, v_rel_bias_table, v_w_branch_ssm, v_w_branch_attn, v_w_out, v_ln_gain, v_ln_bias):
    given = dict(locals())
    w = {k: given[k] for k in WEIGHTS}
    m = {k: given["m_" + k] for k in WEIGHTS}
    v = {k: given["v_" + k] for k in WEIGHTS}

    shards = {k: _cast(w[k][0], BF16, name="cast_" + k) for k in BIG}
    full = _all_gather(shards)
    loss, grad_x, g_big, g_small = _local_step(
        x[0], loss_target[0], full["w_in"], full["w_glu"], full["w_branch_ssm"], full["w_branch_attn"],
        full["w_out"], ssm_lambda_re[0], ssm_lambda_im[0], ssm_b_re[0], ssm_b_im[0], ssm_c_re[0], ssm_c_im[0],
        ssm_d, ssm_log_step, attn_sinks, rel_bias_table, ln_gain, ln_bias)

    g_shard = _reduce_grads(g_big)
    g_packed = _small_allreduce(_pack(g_small, loss))
    loss_sum = _unpack(g_packed)[1]

    grad, delta, new_m, new_v = {}, {}, {}, {}
    for k in BIG:
        grad[k], delta[k], new_m[k], new_v[k] = _adamw(w[k], m[k], v[k], [g_shard[k]], name="adamw_" + k)
    gs, ds, ms, vs = _adamw(_pack(w), _pack(m), _pack(v), [g_packed], name="adamw_small")
    for dst, packed in ((grad, gs), (delta, ds), (new_m, ms), (new_v, vs)):
        dst.update(_unpack(packed)[0])

    return (loss_sum, grad_x[None], *[grad[k] for k in WEIGHTS], *[delta[k] for k in WEIGHTS],
            *[new_m[k] for k in WEIGHTS], *[new_v[k] for k in WEIGHTS])
```

```python
import functools
import math

import numpy as np
import jax
import jax.numpy as jnp
from jax import lax
from jax.experimental import pallas as pl
from jax.experimental.pallas import tpu as pltpu

F32 = jnp.float32
BF16 = jnp.bfloat16

D_MODEL = 2048
D_SSM = 1024
SSM_GROUP = 16
N_GROUPS = 64
SSM_STATE = 64
N_LANES = N_GROUPS * SSM_STATE
N_Q_HEADS = 16
N_KV_HEADS = 4
HEAD_DIM = 64
Q_PER_KV = 4
D_ATTN = 1024
D_KV = 256
WINDOW = 128
BLOCK = 128
N_BUCKETS = 32
MAX_DISTANCE = 128
D_IN = 8704
ALPHA = 2.0 ** 0.25
LN_EPS = 1e-5
NEG_INF = -1e30
ATTN_SCALE = HEAD_DIM ** -0.5

C_U, C_ZS, C_Q, C_K, C_V, C_ZA, C_GL = 0, 1024, 2048, 3072, 3328, 3584, 4608

ADAM_LR = 0.001
ADAM_B1 = 0.9
ADAM_B2 = 0.999
ADAM_EPS = 1e-08
ADAM_WD = 0.01
ADAM_STEP = 10

N_CHIPS = 4
MESH = pl.DeviceIdType.MESH

SSM_CHUNK = 256
SEG_LEN = SSM_CHUNK // 8
SLAB_LANES = 512
N_SLABS = N_LANES // SLAB_LANES
SLAB_CH = D_SSM // N_SLABS

VMEM_LIMIT = 60 * 1024 * 1024


def _params(sem=None, **kw):
    return pltpu.CompilerParams(dimension_semantics=sem, vmem_limit_bytes=VMEM_LIMIT, **kw)


_DIMS = {"nn": (((1,), (0,)), ((), ())), "nt": (((1,), (1,)), ((), ())), "tn": (((0,), (0,)), ((), ()))}


def _mm_body(*refs, dims, nk, res_scale):
    if res_scale is None:
        a_ref, b_ref, o_ref, acc_ref = refs
        r_ref = None
    else:
        a_ref, b_ref, r_ref, o_ref, acc_ref = refs
    k = pl.program_id(2)
    part = lax.dot_general(a_ref[...].astype(BF16), b_ref[...].astype(BF16), _DIMS[dims],
                           preferred_element_type=F32)

    def finish(acc):
        if r_ref is not None:
            acc = acc + res_scale * r_ref[...]
        o_ref[...] = acc.astype(o_ref.dtype)

    if nk == 1:
        finish(part)
    else:
        @pl.when(k == 0)
        def _():
            acc_ref[...] = part

        @pl.when(k > 0)
        def _():
            acc_ref[...] += part

        @pl.when(k == nk - 1)
        def _():
            finish(acc_ref[...])


def _matmul(a, b, dims, *, name, out_dtype=F32, tm=512, tn=512, tk=None, res=None, res_scale=None):
    if dims == "nn":
        (m, kk), n = a.shape, b.shape[1]
    elif dims == "nt":
        (m, kk), n = a.shape, b.shape[0]
    else:
        (kk, m), n = a.shape, b.shape[1]
    tm, tn = min(tm, m), min(tn, n)
    tk = kk if tk is None else min(tk, kk)
    assert m % tm == 0 and n % tn == 0 and kk % tk == 0, (name, m, n, kk, tm, tn, tk)
    nk = kk // tk
    a_spec = {"nn": pl.BlockSpec((tm, tk), lambda i, j, k: (i, k)),
              "nt": pl.BlockSpec((tm, tk), lambda i, j, k: (i, k)),
              "tn": pl.BlockSpec((tk, tm), lambda i, j, k: (k, i))}[dims]
    b_spec = {"nn": pl.BlockSpec((tk, tn), lambda i, j, k: (k, j)),
              "nt": pl.BlockSpec((tn, tk), lambda i, j, k: (j, k)),
              "tn": pl.BlockSpec((tk, tn), lambda i, j, k: (k, j))}[dims]
    in_specs, args = [a_spec, b_spec], [a, b]
    if res is not None:
        in_specs.append(pl.BlockSpec((tm, tn), lambda i, j, k: (i, j)))
        args.append(res)
    return pl.pallas_call(
        functools.partial(_mm_body, dims=dims, nk=nk, res_scale=res_scale if res is not None else None),
        name=name,
        grid=(m // tm, n // tn, nk),
        in_specs=in_specs,
        out_specs=pl.BlockSpec((tm, tn), lambda i, j, k: (i, j)),
        out_shape=jax.ShapeDtypeStruct((m, n), out_dtype),
        scratch_shapes=[pltpu.VMEM((tm, tn), F32)],
        compiler_params=_params(("parallel", "parallel", "arbitrary")),
    )(*args)


def _sigmoid(v):
    return 1.0 / (1.0 + jnp.exp(-v))


def _silu_and_grad(z):
    s = _sigmoid(z)
    return z * s, s * (1.0 + z * (1.0 - s))


def _cast_body(x_ref, o_ref):
    o_ref[...] = x_ref[...].astype(o_ref.dtype)


def _cast(x, dtype, *, name, rows=512):
    m, n = x.shape
    rows = min(rows, m)
    return pl.pallas_call(
        functools.partial(_cast_body), name=name, grid=(m // rows,),
        in_specs=[pl.BlockSpec((rows, n), lambda i: (i, 0))],
        out_specs=pl.BlockSpec((rows, n), lambda i: (i, 0)),
        out_shape=jax.ShapeDtypeStruct((m, n), dtype),
        compiler_params=_params(("parallel",)),
    )(x)


def _lam_bar(lr, li, ls):
    step = jnp.exp(ls)
    er = jnp.exp(lr * step)
    return step, er * jnp.cos(li * step), er * jnp.sin(li * step)


def _ssm_pow_body(lr_ref, li_ref, ls_ref, pr_ref, pi_ref):
    _, ar, ai = _lam_bar(lr_ref[...], li_ref[...], ls_ref[...])
    cr, ci = ar, ai
    for i in range(SEG_LEN):
        pr_ref[pl.ds(i, 1), :] = cr
        pi_ref[pl.ds(i, 1), :] = ci
        cr, ci = cr * ar - ci * ai, cr * ai + ci * ar


def _ssm_bbar_body(lr_ref, li_ref, ls_ref, br_ref, bi_ref, or_ref, oi_ref):
    lr, li = lr_ref[...], li_ref[...]
    _, ar, ai = _lam_bar(lr, li, ls_ref[...])
    d = lr * lr + li * li
    ir, ii = lr / d, -li / d
    nr, ni = ar - 1.0, ai
    cr, ci = nr * ir - ni * ii, nr * ii + ni * ir
    br, bi = br_ref[...], bi_ref[...]
    or_ref[...] = cr * br - ci * bi
    oi_ref[...] = cr * bi + ci * br


def _ssm_param_bwd_body(lr_ref, li_ref, ls_ref, br_ref, bi_ref, glr_ref, gli_ref, gbr_ref, gbi_ref,
                        dlr_ref, dli_ref, dls_ref, dbr_ref, dbi_ref):
    lr, li = lr_ref[...], li_ref[...]
    step, ar, ai = _lam_bar(lr, li, ls_ref[...])
    d = lr * lr + li * li
    ir, ii = lr / d, -li / d
    nr, ni = ar - 1.0, ai
    cr, ci = nr * ir - ni * ii, nr * ii + ni * ir
    br, bi, gbr, gbi = br_ref[...], bi_ref[...], gbr_ref[...], gbi_ref[...]
    dbr_ref[...] = cr * gbr + ci * gbi
    dbi_ref[...] = cr * gbi - ci * gbr
    gcr = jnp.sum(br * gbr + bi * gbi, axis=1, keepdims=True)
    gci = jnp.sum(br * gbi - bi * gbr, axis=1, keepdims=True)
    gnr, gni = ir * gcr + ii * gci, ir * gci - ii * gcr
    gir, gii = nr * gcr + ni * gci, nr * gci - ni * gcr
    gtr, gti = glr_ref[...] + gnr, gli_ref[...] + gni
    i2r, i2i = ir * ir - ii * ii, 2.0 * ir * ii
    g1r, g1i = -(i2r * gir + i2i * gii), -(i2r * gii - i2i * gir)
    g2r, g2i = step * (ar * gtr + ai * gti), step * (ar * gti - ai * gtr)
    mr, mi = lr * ar - li * ai, lr * ai + li * ar
    dlr_ref[...] = g1r + g2r
    dli_ref[...] = g1i + g2i
    dls_ref[...] = (mr * gtr + mi * gti) * step


def _whole(shape):
    return pl.BlockSpec(shape, lambda *_: (0,) * len(shape))


def _ssm_prepare(lam_re, lam_im, b_re, b_im, c_re, c_im, log_step):
    row = lambda a: a.reshape(1, N_LANES)
    col = lambda a: a.reshape(N_LANES, 1)
    ls = jnp.repeat(log_step.reshape(N_GROUPS), SSM_STATE)
    p_re, p_im = pl.pallas_call(
        functools.partial(_ssm_pow_body), name="ssm_pow",
        in_specs=[_whole((1, N_LANES))] * 3, out_specs=[_whole((SEG_LEN, N_LANES))] * 2,
        out_shape=[jax.ShapeDtypeStruct((SEG_LEN, N_LANES), F32)] * 2, grid=(1,),
    )(row(lam_re), row(lam_im), row(ls))
    bb_re, bb_im = pl.pallas_call(
        functools.partial(_ssm_bbar_body), name="ssm_bbar",
        in_specs=[_whole((N_LANES, 1))] * 3 + [_whole((N_LANES, SSM_GROUP))] * 2,
        out_specs=[_whole((N_LANES, SSM_GROUP))] * 2,
        out_shape=[jax.ShapeDtypeStruct((N_LANES, SSM_GROUP), F32)] * 2, grid=(1,),
    )(col(lam_re), col(lam_im), col(ls), b_re.reshape(N_LANES, SSM_GROUP), b_im.reshape(N_LANES, SSM_GROUP))
    eye = jnp.eye(8, dtype=F32)

    def b_slabs(bb):
        t = bb.reshape(N_SLABS, 8, SSM_STATE, SSM_GROUP).transpose(0, 1, 3, 2)
        return (t[:, :, :, None, :] * eye[None, :, None, :, None]).reshape(N_SLABS, SLAB_CH, SLAB_LANES)

    def c_slabs(c):
        t = c.reshape(N_SLABS, 8, SSM_GROUP, SSM_STATE).transpose(0, 1, 3, 2)
        return (t[:, :, :, None, :] * eye[None, :, None, :, None]).reshape(N_SLABS, SLAB_LANES, SLAB_CH)

    b_bd = jnp.concatenate([b_slabs(bb_re), b_slabs(bb_im)], axis=2).astype(BF16)
    c_bd = jnp.concatenate([c_slabs(c_re.reshape(N_GROUPS, SSM_GROUP, SSM_STATE)),
                            -c_slabs(c_im.reshape(N_GROUPS, SSM_GROUP, SSM_STATE))], axis=1).astype(BF16)
    return p_re, p_im, b_bd, c_bd


def _diag_blocks_b(g):
    t = g.reshape(N_SLABS, 8, SSM_GROUP, 8, SSM_STATE)
    t = jnp.stack([t[:, i, :, i, :] for i in range(8)], axis=1)
    return t.transpose(0, 1, 3, 2).reshape(N_LANES, SSM_GROUP)


def _diag_blocks_c(g):
    t = g.reshape(N_SLABS, 8, SSM_STATE, 8, SSM_GROUP)
    t = jnp.stack([t[:, i, :, i, :] for i in range(8)], axis=1)
    return t.transpose(0, 1, 3, 2).reshape(N_GROUPS, SSM_GROUP, SSM_STATE)


def _ssm_param_grads(lam_re, lam_im, b_re, b_im, log_step, g_lam_re, g_lam_im, g_bbd, g_cbd):
    col = lambda a: a.reshape(N_LANES, 1)
    ls = jnp.repeat(log_step.reshape(N_GROUPS), SSM_STATE)
    gbr = _diag_blocks_b(g_bbd[:, :, :SLAB_LANES])
    gbi = _diag_blocks_b(g_bbd[:, :, SLAB_LANES:])
    outs = pl.pallas_call(
        functools.partial(_ssm_param_bwd_body), name="ssm_param_bwd", grid=(1,),
        in_specs=[_whole((N_LANES, 1))] * 3 + [_whole((N_LANES, SSM_GROUP))] * 2 + [_whole((N_LANES, 1))] * 2
        + [_whole((N_LANES, SSM_GROUP))] * 2,
        out_specs=[_whole((N_LANES, 1))] * 3 + [_whole((N_LANES, SSM_GROUP))] * 2,
        out_shape=[jax.ShapeDtypeStruct((N_LANES, 1), F32)] * 3 + [jax.ShapeDtypeStruct((N_LANES, SSM_GROUP), F32)] * 2,
    )(col(lam_re), col(lam_im), col(ls), b_re.reshape(N_LANES, SSM_GROUP), b_im.reshape(N_LANES, SSM_GROUP),
      col(g_lam_re), col(g_lam_im), gbr, gbi)
    dlr, dli, dls, dbr, dbi = outs
    d_c_re = _diag_blocks_c(g_cbd[:, :SLAB_LANES, :])
    d_c_im = -_diag_blocks_c(g_cbd[:, SLAB_LANES:, :])
    return (dlr.reshape(1, N_GROUPS, SSM_STATE), dli.reshape(1, N_GROUPS, SSM_STATE),
            dbr.reshape(1, N_GROUPS, SSM_STATE, SSM_GROUP), dbi.reshape(1, N_GROUPS, SSM_STATE, SSM_GROUP),
            d_c_re[None], d_c_im[None], dls.reshape(N_GROUPS, SSM_STATE).sum(axis=1).reshape(1, N_GROUPS))


def _bcast8(v):
    return jnp.broadcast_to(v, (8, v.shape[1]))


def _segment_permutation():
    p = np.zeros((SSM_CHUNK, SSM_CHUNK), np.float32)
    rows = np.arange(SSM_CHUNK)
    p[rows, (rows % 8) * SEG_LEN + rows // 8] = 1.0
    return p


def _permute_exact(perm, val, pieces):
    out, rest = None, val
    for n in range(pieces):
        part = rest.astype(BF16)
        moved = jnp.dot(perm, part, preferred_element_type=F32)
        out = moved if out is None else out + moved
        if n + 1 < pieces:
            rest = rest - part.astype(F32)
    return out


def _scan_buffer():
    return pltpu.VMEM((SSM_CHUNK, N_LANES), F32)


def _lanes(k):
    return pl.ds(k * SLAB_LANES, SLAB_LANES)


def _tile(i):
    return pl.ds(i * 8 if isinstance(i, int) else pl.multiple_of(i * 8, 8), 8)


def _seg_get(ref, k, i):
    return ref[_tile(i), _lanes(k)]


def _seg_put(ref, k, i, val):
    ref[_tile(i), _lanes(k)] = val


def _slab_get(ref, k):
    return ref[:, _lanes(k)]


def _slab_put(ref, k, val):
    ref[:, _lanes(k)] = val


def _scan_forward(s_re, s_im, p_re, p_im, car_re, car_im, sp_re=None, sp_im=None):
    for k in range(N_SLABS):
        ln = pl.ds(k * SLAB_LANES, SLAB_LANES)
        ar, ai = _bcast8(p_re[pl.ds(0, 1), ln]), _bcast8(p_im[pl.ds(0, 1), ln])

        def step(i, s, k=k, ar=ar, ai=ai):
            sr, si = s
            nr = ar * sr - ai * si + _seg_get(s_re, k, i)
            ni = ar * si + ai * sr + _seg_get(s_im, k, i)
            _seg_put(s_re, k, i, nr)
            _seg_put(s_im, k, i, ni)
            return nr, ni

        zero = jnp.zeros((8, SLAB_LANES), F32)
        er, ei = lax.fori_loop(0, SEG_LEN, step, (zero, zero), unroll=4)
        lr, li = p_re[pl.ds(SEG_LEN - 1, 1), ln], p_im[pl.ds(SEG_LEN - 1, 1), ln]
        cr, ci = car_re[pl.ds(0, 1), ln], car_im[pl.ds(0, 1), ln]
        rows_r, rows_i = [], []
        for r in range(8):
            rows_r.append(cr)
            rows_i.append(ci)
            cr, ci = er[r:r + 1] + lr * cr - li * ci, ei[r:r + 1] + lr * ci + li * cr
        pr8, pi8 = jnp.concatenate(rows_r, axis=0), jnp.concatenate(rows_i, axis=0)
        car_re[:, ln] = _bcast8(cr)
        car_im[:, ln] = _bcast8(ci)
        if sp_re is not None:
            sp_re[:, ln] = pr8
            sp_im[:, ln] = pi8

        def fix(i, _, k=k, ln=ln, pr8=pr8, pi8=pi8):
            qr, qi = _bcast8(p_re[pl.ds(i, 1), ln]), _bcast8(p_im[pl.ds(i, 1), ln])
            _seg_put(s_re, k, i, _seg_get(s_re, k, i) + qr * pr8 - qi * pi8)
            _seg_put(s_im, k, i, _seg_get(s_im, k, i) + qr * pi8 + qi * pr8)
            return 0

        lax.fori_loop(0, SEG_LEN, fix, 0, unroll=4)


def _scan_backward(g_re, g_im, s_re, s_im, sp_re, sp_im, p_re, p_im, car_re, car_im, acc_re, acc_im):
    for k in range(N_SLABS):
        ln = pl.ds(k * SLAB_LANES, SLAB_LANES)
        ar, ai = _bcast8(p_re[pl.ds(0, 1), ln]), -_bcast8(p_im[pl.ds(0, 1), ln])

        def step(j, s, k=k, ar=ar, ai=ai):
            i = SEG_LEN - 1 - j
            sr, si = s
            nr = ar * sr - ai * si + _seg_get(g_re, k, i)
            ni = ar * si + ai * sr + _seg_get(g_im, k, i)
            _seg_put(g_re, k, i, nr)
            _seg_put(g_im, k, i, ni)
            return nr, ni

        zero = jnp.zeros((8, SLAB_LANES), F32)
        er, ei = lax.fori_loop(0, SEG_LEN, step, (zero, zero), unroll=4)
        lr, li = p_re[pl.ds(SEG_LEN - 1, 1), ln], -p_im[pl.ds(SEG_LEN - 1, 1), ln]
        cr, ci = car_re[pl.ds(0, 1), ln], car_im[pl.ds(0, 1), ln]
        rows_r, rows_i = [None] * 8, [None] * 8
        for r in range(7, -1, -1):
            rows_r[r], rows_i[r] = cr, ci
            cr, ci = er[r:r + 1] + lr * cr - li * ci, ei[r:r + 1] + lr * ci + li * cr
        nr8, ni8 = jnp.concatenate(rows_r, axis=0), jnp.concatenate(rows_i, axis=0)
        car_re[:, ln] = _bcast8(cr)
        car_im[:, ln] = _bcast8(ci)

        def fix(i, acc, k=k, ln=ln, nr8=nr8, ni8=ni8):
            qr = _bcast8(p_re[pl.ds(SEG_LEN - 1 - i, 1), ln])
            qi = -_bcast8(p_im[pl.ds(SEG_LEN - 1 - i, 1), ln])
            gr = _seg_get(g_re, k, i) + qr * nr8 - qi * ni8
            gi = _seg_get(g_im, k, i) + qr * ni8 + qi * nr8
            _seg_put(g_re, k, i, gr)
            _seg_put(g_im, k, i, gi)
            return gr, gi

        def prod(xr, xi, gr, gi):
            return xr * gr + xi * gi, xr * gi - xi * gr

        gr, gi = fix(0, None)
        a_r, a_i = prod(sp_re[:, ln], sp_im[:, ln], gr, gi)

        def fix_acc(i, acc, k=k, fix=fix):
            gr, gi = fix(i, None)
            dr, di = prod(_seg_get(s_re, k, i - 1), _seg_get(s_im, k, i - 1), gr, gi)
            return acc[0] + dr, acc[1] + di

        a_r, a_i = lax.fori_loop(1, SEG_LEN, fix_acc, (a_r, a_i))
        acc_re[:, ln] += a_r
        acc_im[:, ln] += a_i


def _gelu_and_grad(y):
    cdf = 0.5 * (1.0 + lax.erf(y * (2.0 ** -0.5)))
    pdf = jnp.exp(-0.5 * y * y) * (1.0 / math.sqrt(2.0 * math.pi))
    return y * cdf, cdf + y * pdf


def _ssm_fwd_body(u_ref, bbd_ref, cbd_ref, pre_ref, pim_ref, d_ref, perm_ref, unperm_ref,
                  y_ref, gin_ref, cre_out, cim_out, s_re, s_im, car_re, car_im, yp):
    c = pl.program_id(0)

    @pl.when(c == 0)
    def _():
        car_re[...] = jnp.zeros_like(car_re)
        car_im[...] = jnp.zeros_like(car_im)

    cre_out[...] = car_re[...]
    cim_out[...] = car_im[...]
    u = u_ref[...]
    up = jnp.dot(perm_ref[...], u.astype(BF16), preferred_element_type=F32).astype(BF16)
    for k in range(N_SLABS):
        bu = jnp.dot(up[:, k * SLAB_CH:(k + 1) * SLAB_CH], bbd_ref[k], preferred_element_type=F32)
        _slab_put(s_re, k, bu[:, :SLAB_LANES])
        _slab_put(s_im, k, bu[:, SLAB_LANES:])
    _scan_forward(s_re, s_im, pre_ref, pim_ref, car_re, car_im)
    for k in range(N_SLABS):
        yp[:, pl.ds(k * SLAB_CH, SLAB_CH)] = (
            jnp.dot(_slab_get(s_re, k).astype(BF16), cbd_ref[k, :SLAB_LANES, :], preferred_element_type=F32)
            + jnp.dot(_slab_get(s_im, k).astype(BF16), cbd_ref[k, SLAB_LANES:, :], preferred_element_type=F32))
    y = _permute_exact(unperm_ref[...], yp[...], 3) + d_ref[...] * u
    y_ref[...] = y
    gin_ref[...] = _gelu_and_grad(y)[0].astype(BF16)


def _ssm_forward(proj, p_re, p_im, b_bd, c_bd, d_skip):
    t = proj.shape[0]
    nc = t // SSM_CHUNK
    perm = _segment_permutation()
    return pl.pallas_call(
        functools.partial(_ssm_fwd_body), name="ssm_fwd", grid=(nc,),
        in_specs=[pl.BlockSpec((SSM_CHUNK, D_SSM), lambda c: (c, C_U // D_SSM)),
                  _whole(b_bd.shape), _whole(c_bd.shape), _whole(p_re.shape), _whole(p_im.shape),
                  _whole((1, D_SSM)), _whole(perm.shape), _whole(perm.shape)],
        out_specs=[pl.BlockSpec((SSM_CHUNK, D_SSM), lambda c: (c, 0)),
                   pl.BlockSpec((SSM_CHUNK, D_SSM), lambda c: (c, 0)),
                   pl.BlockSpec((None, 8, N_LANES), lambda c: (c, 0, 0)),
                   pl.BlockSpec((None, 8, N_LANES), lambda c: (c, 0, 0))],
        out_shape=[jax.ShapeDtypeStruct((t, D_SSM), F32), jax.ShapeDtypeStruct((t, D_SSM), BF16),
                   jax.ShapeDtypeStruct((nc, 8, N_LANES), F32), jax.ShapeDtypeStruct((nc, 8, N_LANES), F32)],
        scratch_shapes=[_scan_buffer(), _scan_buffer(),
                        pltpu.VMEM((8, N_LANES), F32), pltpu.VMEM((8, N_LANES), F32),
                        pltpu.VMEM((SSM_CHUNK, D_SSM), F32)],
        compiler_params=_params(("arbitrary",)),
    )(proj, b_bd, c_bd, p_re, p_im, d_skip, jnp.asarray(perm, BF16), jnp.asarray(perm.T, BF16))


def _ssm_bwd_body(u_ref, dgin_ref, y_ref, cre_in, cim_in, bbd_ref, cbd_ref, pre_ref, pim_ref, d_ref, perm_ref,
                  unperm_ref, dproj_in,
                  du_ref, gb_ref, gc_ref, glr_ref, gli_ref, gd_ref,
                  s_re, s_im, g_re, g_im, sp_re, sp_im, car_re, car_im, gcar_re, gcar_im, acc_re, acc_im, dup):
    del dproj_in
    c = pl.program_id(0)
    nc = pl.num_programs(0)

    @pl.when(c == 0)
    def _():
        gcar_re[...] = jnp.zeros_like(gcar_re)
        gcar_im[...] = jnp.zeros_like(gcar_im)
        acc_re[...] = jnp.zeros_like(acc_re)
        acc_im[...] = jnp.zeros_like(acc_im)
        gb_ref[...] = jnp.zeros_like(gb_ref)
        gc_ref[...] = jnp.zeros_like(gc_ref)
        gd_ref[...] = jnp.zeros_like(gd_ref)

    car_re[...] = cre_in[...]
    car_im[...] = cim_in[...]
    u = u_ref[...]
    dy = dgin_ref[...] * _gelu_and_grad(y_ref[...])[1]
    gd_ref[...] += jnp.sum(dy * u, axis=0, keepdims=True)
    up = jnp.dot(perm_ref[...], u.astype(BF16), preferred_element_type=F32).astype(BF16)
    dyp = jnp.dot(perm_ref[...], dy.astype(BF16), preferred_element_type=F32).astype(BF16)
    for k in range(N_SLABS):
        ch = slice(k * SLAB_CH, (k + 1) * SLAB_CH)
        bu = jnp.dot(up[:, ch], bbd_ref[k], preferred_element_type=F32)
        _slab_put(s_re, k, bu[:, :SLAB_LANES])
        _slab_put(s_im, k, bu[:, SLAB_LANES:])
        ds = lax.dot_general(dyp[:, ch], cbd_ref[k], _DIMS["nt"], preferred_element_type=F32)
        _slab_put(g_re, k, ds[:, :SLAB_LANES])
        _slab_put(g_im, k, ds[:, SLAB_LANES:])
    _scan_forward(s_re, s_im, pre_ref, pim_ref, car_re, car_im, sp_re, sp_im)
    _scan_backward(g_re, g_im, s_re, s_im, sp_re, sp_im, pre_ref, pim_ref, gcar_re, gcar_im, acc_re, acc_im)
    for k in range(N_SLABS):
        ch = slice(k * SLAB_CH, (k + 1) * SLAB_CH)
        uk, dyk = up[:, ch], dyp[:, ch]
        sr, si = _slab_get(s_re, k).astype(BF16), _slab_get(s_im, k).astype(BF16)
        gr, gi = _slab_get(g_re, k).astype(BF16), _slab_get(g_im, k).astype(BF16)
        gc_ref[k, :SLAB_LANES, :] += lax.dot_general(sr, dyk, _DIMS["tn"], preferred_element_type=F32)
        gc_ref[k, SLAB_LANES:, :] += lax.dot_general(si, dyk, _DIMS["tn"], preferred_element_type=F32)
        gb_ref[k, :, :SLAB_LANES] += lax.dot_general(uk, gr, _DIMS["tn"], preferred_element_type=F32)
        gb_ref[k, :, SLAB_LANES:] += lax.dot_general(uk, gi, _DIMS["tn"], preferred_element_type=F32)
        dup[:, pl.ds(k * SLAB_CH, SLAB_CH)] = (
            lax.dot_general(gr, bbd_ref[k, :, :SLAB_LANES], _DIMS["nt"], preferred_element_type=F32)
            + lax.dot_general(gi, bbd_ref[k, :, SLAB_LANES:], _DIMS["nt"], preferred_element_type=F32))
    du = _permute_exact(unperm_ref[...], dup[...], 2) + d_ref[...] * dy
    du_ref[...] = du.astype(du_ref.dtype)

    @pl.when(c == nc - 1)
    def _():
        glr_ref[...] = jnp.sum(acc_re[...], axis=0, keepdims=True)
        gli_ref[...] = jnp.sum(acc_im[...], axis=0, keepdims=True)


def _ssm_backward(proj, dg_in, y_ssm, car_re, car_im, p_re, p_im, b_bd, c_bd, d_skip, dproj):
    t = proj.shape[0]
    nc = t // SSM_CHUNK
    rev = lambda c: nc - 1 - c
    big = _scan_buffer
    small = lambda: pltpu.VMEM((8, N_LANES), F32)
    perm = _segment_permutation()
    outs = pl.pallas_call(
        functools.partial(_ssm_bwd_body), name="ssm_bwd", grid=(nc,),
        in_specs=[pl.BlockSpec((SSM_CHUNK, D_SSM), lambda c: (rev(c), C_U // D_SSM)),
                  pl.BlockSpec((SSM_CHUNK, D_SSM), lambda c: (rev(c), 0)),
                  pl.BlockSpec((SSM_CHUNK, D_SSM), lambda c: (rev(c), 0)),
                  pl.BlockSpec((None, 8, N_LANES), lambda c: (rev(c), 0, 0)),
                  pl.BlockSpec((None, 8, N_LANES), lambda c: (rev(c), 0, 0)),
                  _whole(b_bd.shape), _whole(c_bd.shape), _whole(p_re.shape), _whole(p_im.shape),
                  _whole((1, D_SSM)), _whole(perm.shape), _whole(perm.shape), pl.BlockSpec(memory_space=pl.ANY)],
        out_specs=[pl.BlockSpec((SSM_CHUNK, D_SSM), lambda c: (rev(c), C_U // D_SSM)),
                   _whole(b_bd.shape), _whole(c_bd.shape), _whole((1, N_LANES)), _whole((1, N_LANES)),
                   _whole((1, D_SSM))],
        out_shape=[jax.ShapeDtypeStruct(dproj.shape, dproj.dtype),
                   jax.ShapeDtypeStruct(b_bd.shape, F32), jax.ShapeDtypeStruct(c_bd.shape, F32),
                   jax.ShapeDtypeStruct((1, N_LANES), F32), jax.ShapeDtypeStruct((1, N_LANES), F32),
                   jax.ShapeDtypeStruct((1, D_SSM), F32)],
        scratch_shapes=[big(), big(), big(), big()] + [small() for _ in range(8)]
        + [pltpu.VMEM((SSM_CHUNK, D_SSM), F32)],
        input_output_aliases={12: 0},
        compiler_params=_params(("arbitrary",)),
    )(proj, dg_in, y_ssm, car_re, car_im, b_bd, c_bd, p_re, p_im, d_skip, jnp.asarray(perm, BF16),
      jnp.asarray(perm.T, BF16), dproj)
    return outs


def _bucket_table():
    i = np.arange(BLOCK)[:, None]
    j = np.arange(2 * BLOCK)[None, :]
    dist = BLOCK + i - j
    ok = (dist >= 0) & (dist < WINDOW)
    max_exact = N_BUCKETS // 2
    d = np.maximum(dist, 1).astype(np.float32)
    large = max_exact + (np.log(d / max_exact) / math.log(MAX_DISTANCE / max_exact)
                         * (N_BUCKETS - max_exact)).astype(np.int32)
    large = np.minimum(large, N_BUCKETS - 1)
    bucket = np.where(dist < max_exact, dist, large)
    return np.where(ok, bucket, -1).astype(np.int32)


def _bias_build_body(table_ref, bucket_ref, o_ref):
    h = pl.program_id(0)
    bucket = bucket_ref[...]
    acc = jnp.full(bucket.shape, NEG_INF, F32)
    for b in range(N_BUCKETS):
        acc = jnp.where(bucket == b, table_ref[b, h], acc)
    o_ref[...] = acc


def _bias_build(rel_bias_table, bucket):
    return pl.pallas_call(
        functools.partial(_bias_build_body), name="bias_build", grid=(N_Q_HEADS,),
        in_specs=[pl.BlockSpec(memory_space=pltpu.SMEM), _whole(bucket.shape)],
        out_specs=pl.BlockSpec((None, BLOCK, 2 * BLOCK), lambda h: (h, 0, 0)),
        out_shape=jax.ShapeDtypeStruct((N_Q_HEADS, BLOCK, 2 * BLOCK), F32),
        compiler_params=_params(("arbitrary",)),
    )(rel_bias_table, bucket)


def _bias_grad_body(g_ref, bucket_ref, o_ref):
    bucket = bucket_ref[...]
    lane = lax.broadcasted_iota(jnp.int32, (N_BUCKETS, 128), 1)

    def head(h, out):
        g = g_ref[h]
        rows = [jnp.sum(jnp.where(bucket == b, g, 0.0), axis=0, keepdims=True) for b in range(N_BUCKETS)]
        colsum = jnp.sum(jnp.concatenate(rows, axis=0), axis=1, keepdims=True)
        return jnp.where(lane == h, colsum, out)

    o_ref[...] = lax.fori_loop(0, N_Q_HEADS, head, jnp.zeros((N_BUCKETS, 128), F32))


def _bias_grad(g_bias, bucket):
    out = pl.pallas_call(
        functools.partial(_bias_grad_body), name="bias_grad", grid=(1,),
        in_specs=[_whole(g_bias.shape), _whole(bucket.shape)],
        out_specs=_whole((N_BUCKETS, 128)),
        out_shape=jax.ShapeDtypeStruct((N_BUCKETS, 128), F32),
        compiler_params=_params(("arbitrary",)),
    )(g_bias, bucket)
    return out[:, :N_Q_HEADS]


def _head_logits(qh, kk, bias_h, first_block):
    s = lax.dot_general(qh, kk, _DIMS["nt"], preferred_element_type=F32) * ATTN_SCALE + bias_h
    col = lax.broadcasted_iota(jnp.int32, s.shape, 1)
    return jnp.where(jnp.logical_and(first_block, col < BLOCK), NEG_INF, s)


def _attn_fwd_body(sink_ref, q_ref, kp_ref, kc_ref, vp_ref, vc_ref, bias_ref, o_ref, lse_ref):
    n = pl.program_id(0)
    outs, lses = [], []
    for kv in range(N_KV_HEADS):
        cs = slice(kv * HEAD_DIM, (kv + 1) * HEAD_DIM)
        kk = jnp.concatenate([kp_ref[:, cs], kc_ref[:, cs]], axis=0).astype(BF16)
        vv = jnp.concatenate([vp_ref[:, cs], vc_ref[:, cs]], axis=0).astype(BF16)
        for g in range(Q_PER_KV):
            h = kv * Q_PER_KV + g
            qh = q_ref[:, h * HEAD_DIM:(h + 1) * HEAD_DIM].astype(BF16)
            s = _head_logits(qh, kk, bias_ref[h], n == 0)
            sink = sink_ref[0, h]
            m = jnp.maximum(jnp.max(s, axis=1, keepdims=True), sink)
            p = jnp.exp(s - m)
            den = jnp.sum(p, axis=1, keepdims=True) + jnp.exp(sink - m)
            p = p / den
            outs.append(jnp.dot(p.astype(BF16), vv, preferred_element_type=F32))
            lses.append(m + jnp.log(den))
    o_ref[...] = jnp.concatenate(outs, axis=1)
    lse_ref[...] = jnp.concatenate(lses, axis=1)


def _attn_specs(nb):
    prev = lambda n: jnp.maximum(jnp.minimum(n, nb - 1) - 1, 0)
    cur = lambda n: jnp.minimum(n, nb - 1)
    return [pl.BlockSpec((BLOCK, D_ATTN), lambda n: (cur(n), C_Q // D_ATTN)),
            pl.BlockSpec((BLOCK, D_KV), lambda n: (prev(n), C_K // D_KV)),
            pl.BlockSpec((BLOCK, D_KV), lambda n: (cur(n), C_K // D_KV)),
            pl.BlockSpec((BLOCK, D_KV), lambda n: (prev(n), C_V // D_KV)),
            pl.BlockSpec((BLOCK, D_KV), lambda n: (cur(n), C_V // D_KV))]


def _attn_forward(proj, sinks, bias):
    t = proj.shape[0]
    nb = t // BLOCK
    return pl.pallas_call(
        functools.partial(_attn_fwd_body), name="attn_fwd", grid=(nb,),
        in_specs=[pl.BlockSpec(memory_space=pltpu.SMEM)] + _attn_specs(nb) + [_whole(bias.shape)],
        out_specs=[pl.BlockSpec((BLOCK, D_ATTN), lambda n: (n, 0)),
                   pl.BlockSpec((BLOCK, N_Q_HEADS), lambda n: (n, 0))],
        out_shape=[jax.ShapeDtypeStruct((t, D_ATTN), F32), jax.ShapeDtypeStruct((t, N_Q_HEADS), F32)],
        compiler_params=_params(("parallel",)),
    )(sinks, proj, proj, proj, proj, proj, bias)


def _attn_bwd_body(sink_ref, q_ref, kp_ref, kc_ref, vp_ref, vc_ref, bias_ref, do_ref, o_ref, lse_ref, dproj_in,
                   dq_ref, dkv_ref, gbias_ref, gsink_ref, carry_ref, *, nb):
    del dproj_in
    n = pl.program_id(0)

    @pl.when(n == 0)
    def _():
        gbias_ref[...] = jnp.zeros_like(gbias_ref)
        gsink_ref[...] = jnp.zeros_like(gsink_ref)
        carry_ref[...] = jnp.zeros_like(carry_ref)

    @pl.when(n < nb)
    def _():
        lane = lax.broadcasted_iota(jnp.int32, (1, 128), 1)
        dqs, dks, dvs = [], [], []
        gsink = jnp.zeros((1, 128), F32)
        for kv in range(N_KV_HEADS):
            cs = slice(kv * HEAD_DIM, (kv + 1) * HEAD_DIM)
            kk = jnp.concatenate([kp_ref[:, cs], kc_ref[:, cs]], axis=0).astype(BF16)
            vv = jnp.concatenate([vp_ref[:, cs], vc_ref[:, cs]], axis=0).astype(BF16)
            dk = jnp.zeros((2 * BLOCK, HEAD_DIM), F32)
            dv = jnp.zeros((2 * BLOCK, HEAD_DIM), F32)
            for g in range(Q_PER_KV):
                h = kv * Q_PER_KV + g
                hs = slice(h * HEAD_DIM, (h + 1) * HEAD_DIM)
                qh = q_ref[:, hs].astype(BF16)
                s = _head_logits(qh, kk, bias_ref[h], n == 0)
                lse = lse_ref[:, h:h + 1]
                p = jnp.exp(s - lse)
                do = do_ref[:, hs]
                delta = jnp.sum(do * o_ref[:, hs], axis=1, keepdims=True)
                dob = do.astype(BF16)
                dp = lax.dot_general(dob, vv, _DIMS["nt"], preferred_element_type=F32)
                dl = p * (dp - delta)
                gbias_ref[h] += dl
                psink = jnp.exp(sink_ref[0, h] - lse)
                gsink = gsink + jnp.where(lane == h, -jnp.sum(psink * delta), 0.0)
                dlb = dl.astype(BF16)
                dqs.append(jnp.dot(dlb, kk, preferred_element_type=F32) * ATTN_SCALE)
                dk = dk + lax.dot_general(dlb, qh, _DIMS["tn"], preferred_element_type=F32) * ATTN_SCALE
                dv = dv + lax.dot_general(p.astype(BF16), dob, _DIMS["tn"], preferred_element_type=F32)
            dks.append(dk)
            dvs.append(dv)
        dq_ref[...] = jnp.concatenate(dqs, axis=1).astype(dq_ref.dtype)
        gsink_ref[...] += gsink
        dkv = jnp.concatenate(dks + dvs, axis=1)
        dkv_ref[...] = (carry_ref[...] + dkv[:BLOCK]).astype(dkv_ref.dtype)
        carry_ref[...] = dkv[BLOCK:]

    @pl.when(n == nb)
    def _():
        dkv_ref[...] = carry_ref[...].astype(dkv_ref.dtype)


def _attn_backward(proj, sinks, bias, d_attn, attn, lse, dproj):
    t = proj.shape[0]
    nb = t // BLOCK
    cur = lambda n: jnp.minimum(n, nb - 1)
    return pl.pallas_call(
        functools.partial(_attn_bwd_body, nb=nb), name="attn_bwd", grid=(nb + 1,),
        in_specs=[pl.BlockSpec(memory_space=pltpu.SMEM)] + _attn_specs(nb) + [
            _whole(bias.shape),
            pl.BlockSpec((BLOCK, D_ATTN), lambda n: (cur(n), 0)),
            pl.BlockSpec((BLOCK, D_ATTN), lambda n: (cur(n), 0)),
            pl.BlockSpec((BLOCK, N_Q_HEADS), lambda n: (cur(n), 0)),
            pl.BlockSpec(memory_space=pl.ANY)],
        out_specs=[pl.BlockSpec((BLOCK, D_ATTN), lambda n: (cur(n), C_Q // D_ATTN)),
                   pl.BlockSpec((BLOCK, 2 * D_KV), lambda n: (jnp.maximum(n - 1, 0), 0)),
                   _whole(bias.shape), _whole((1, 128))],
        out_shape=[jax.ShapeDtypeStruct(dproj.shape, dproj.dtype), jax.ShapeDtypeStruct((t, 2 * D_KV), dproj.dtype),
                   jax.ShapeDtypeStruct(bias.shape, F32), jax.ShapeDtypeStruct((1, 128), F32)],
        scratch_shapes=[pltpu.VMEM((BLOCK, 2 * D_KV), F32)],
        input_output_aliases={10: 0},
        compiler_params=_params(("arbitrary",)),
    )(sinks, proj, proj, proj, proj, proj, bias, d_attn, attn, lse, dproj)


ROWS = 256


def _rowwise(body, name, t, ins, outs, aliases=None):
    rows = min(ROWS, t)

    def col_spec(w, c0):
        if c0 % w == 0:
            return pl.BlockSpec((rows, w), lambda i: (i, c0 // w))
        return pl.BlockSpec((pl.Element(rows), pl.Element(w)), lambda i: (i * rows, c0))

    in_specs, args = [], []
    for a, w, c0 in ins:
        args.append(a)
        if w is None:
            in_specs.append(pl.BlockSpec(memory_space=pl.ANY) if c0 == "any" else _whole(a.shape))
        else:
            in_specs.append(col_spec(w, c0))
    out_specs, out_shape = [], []
    for shape, dtype, w, c0 in outs:
        out_shape.append(jax.ShapeDtypeStruct(shape, dtype))
        out_specs.append(_whole(shape) if w is None else col_spec(w, c0))
    accum = any(o[2] is None for o in outs)
    return pl.pallas_call(
        body, name=name, grid=(t // rows,), in_specs=in_specs, out_specs=out_specs, out_shape=out_shape,
        input_output_aliases=aliases or {},
        compiler_params=_params(("arbitrary",) if accum else ("parallel",)),
    )(*args)


def _ssm_gate_fwd_body(glu_ref, z_ref, h_ref):
    a, b = glu_ref[:, :D_SSM], glu_ref[:, D_SSM:]
    h_ref[...] = ((a * _sigmoid(b)) * _silu_and_grad(z_ref[...])[0]).astype(h_ref.dtype)


def _ssm_gate_bwd_body(dh_ref, glu_ref, z_ref, dproj_in, dglu_ref, dz_ref):
    del dproj_in
    a, b = glu_ref[:, :D_SSM], glu_ref[:, D_SSM:]
    sb = _sigmoid(b)
    silu, dsilu = _silu_and_grad(z_ref[...])
    dh = dh_ref[...]
    dg = dh * silu
    dz_ref[...] = (dh * (a * sb) * dsilu).astype(dz_ref.dtype)
    dglu_ref[:, :D_SSM] = (dg * sb).astype(dglu_ref.dtype)
    dglu_ref[:, D_SSM:] = (dg * a * sb * (1.0 - sb)).astype(dglu_ref.dtype)


def _attn_gate_fwd_body(attn_ref, z_ref, h_ref):
    h_ref[...] = (attn_ref[...] * _silu_and_grad(z_ref[...])[0]).astype(h_ref.dtype)


def _attn_gate_bwd_body(dh_ref, attn_ref, z_ref, dproj_in, dattn_ref, dz_ref):
    del dproj_in
    silu, dsilu = _silu_and_grad(z_ref[...])
    dh = dh_ref[...]
    dattn_ref[...] = dh * silu
    dz_ref[...] = (dh * attn_ref[...] * dsilu).astype(dz_ref.dtype)


def _merge_fwd_body(bs_ref, ba_ref, gl_ref, m_ref):
    gs = _sigmoid(gl_ref[:, :D_MODEL])
    ga = _sigmoid(gl_ref[:, D_MODEL:])
    m_ref[...] = (gs * bs_ref[...] + ga * ba_ref[...]).astype(m_ref.dtype)


def _merge_bwd_body(dm_ref, bs_ref, ba_ref, gl_ref, dbs_ref, dba_ref, dgl_ref):
    gs = _sigmoid(gl_ref[:, :D_MODEL])
    ga = _sigmoid(gl_ref[:, D_MODEL:])
    dm = dm_ref[...]
    dbs_ref[...] = (dm * gs).astype(dbs_ref.dtype)
    dba_ref[...] = (dm * ga).astype(dba_ref.dtype)
    dgl_ref[:, :D_MODEL] = (dm * bs_ref[...] * gs * (1.0 - gs)).astype(dgl_ref.dtype)
    dgl_ref[:, D_MODEL:] = (dm * ba_ref[...] * ga * (1.0 - ga)).astype(dgl_ref.dtype)


def _ln_loss_body(x_ref, o_ref, tgt_ref, gain_ref, bias_ref, dr_ref, loss_ref, dgain_ref, dbias_ref):
    @pl.when(pl.program_id(0) == 0)
    def _():
        loss_ref[...] = jnp.zeros_like(loss_ref)
        dgain_ref[...] = jnp.zeros_like(dgain_ref)
        dbias_ref[...] = jnp.zeros_like(dbias_ref)

    r = ALPHA * x_ref[...] + o_ref[...]
    mu = jnp.mean(r, axis=1, keepdims=True)
    rc = r - mu
    var = jnp.mean(rc * rc, axis=1, keepdims=True)
    rstd = lax.rsqrt(var + LN_EPS)
    xhat = rc * rstd
    gain = gain_ref[...]
    err = xhat * gain + bias_ref[...] - tgt_ref[...]
    loss_ref[...] += 0.5 * jnp.sum(jnp.mean(err * err, axis=1, keepdims=True), axis=0, keepdims=True)
    dy = err * (1.0 / D_MODEL)
    dgain_ref[...] += jnp.sum(dy * xhat, axis=0, keepdims=True)
    dbias_ref[...] += jnp.sum(dy, axis=0, keepdims=True)
    dxhat = dy * gain
    m1 = jnp.mean(dxhat, axis=1, keepdims=True)
    m2 = jnp.mean(dxhat * xhat, axis=1, keepdims=True)
    dr_ref[...] = rstd * (dxhat - m1 - xhat * m2)


def _place_body(piece_ref, dproj_in, o_ref):
    del dproj_in
    o_ref[...] = piece_ref[...]


def _adamw_update(w_ref, m_ref, v_ref, g, g_ref, d_ref, nm_ref, nv_ref):
    m = ADAM_B1 * m_ref[...] + (1.0 - ADAM_B1) * g
    v = ADAM_B2 * v_ref[...] + (1.0 - ADAM_B2) * (g * g)
    m_hat = m / (1.0 - ADAM_B1 ** ADAM_STEP)
    v_hat = v / (1.0 - ADAM_B2 ** ADAM_STEP)
    g_ref[...] = g
    d_ref[...] = -ADAM_LR * (m_hat / (jnp.sqrt(v_hat) + ADAM_EPS) + ADAM_WD * w_ref[...])
    nm_ref[...] = m
    nv_ref[...] = v


def _adamw_body(*refs, n_parts):
    w_ref, m_ref, v_ref = refs[:3]
    parts = refs[3:3 + n_parts]
    g = parts[0][...].astype(F32)
    for p in parts[1:]:
        g = g + p[...].astype(F32)
    _adamw_update(w_ref, m_ref, v_ref, g, *refs[3 + n_parts:])


def _adamw_shard_body(c_ref, w_ref, m_ref, v_ref, mine_ref, other_ref, g_ref, d_ref, nm_ref, nv_ref, *, nth):
    in_mine = pl.program_id(0) // nth == c_ref[0]
    g = jnp.where(in_mine, mine_ref[...], other_ref[...])
    _adamw_update(w_ref, m_ref, v_ref, g, g_ref, d_ref, nm_ref, nv_ref)


def _adamw_shard(w, m, v, halves, core, *, name, rows):
    shape = w.shape
    w2, m2, v2 = (a.reshape(-1, shape[-1]) for a in (w, m, v))
    r, c = w2.shape
    nth = r // 2 // rows
    assert 2 * nth * rows == r and halves[0].shape == (r // 2, c)
    spec = pl.BlockSpec((rows, c), lambda i, cr: (i, 0))
    half_spec = pl.BlockSpec((rows, c), lambda i, cr: (i % nth, 0))
    outs = pl.pallas_call(
        functools.partial(_adamw_shard_body, nth=nth), name=name,
        grid_spec=pltpu.PrefetchScalarGridSpec(
            num_scalar_prefetch=1, grid=(r // rows,),
            in_specs=[spec] * 3 + [half_spec] * 2, out_specs=[spec] * 4),
        out_shape=[jax.ShapeDtypeStruct((r, c), F32)] * 4,
        compiler_params=_params(("parallel",)),
    )(core, w2, m2, v2, *halves)
    return tuple(o.reshape(shape) for o in outs)


def _adamw(w, m, v, parts, *, name, rows=256):
    shape = w.shape
    w2, m2, v2 = (a.reshape(-1, shape[-1]) for a in (w, m, v))
    parts = [p.reshape(w2.shape) for p in parts]
    r, c = w2.shape
    rows = rows if r % rows == 0 else r
    spec = pl.BlockSpec((rows, c), lambda i: (i, 0))
    outs = pl.pallas_call(
        functools.partial(_adamw_body, n_parts=len(parts)), name=name, grid=(r // rows,),
        in_specs=[spec] * (3 + len(parts)), out_specs=[spec] * 4,
        out_shape=[jax.ShapeDtypeStruct((r, c), F32)] * 4,
        compiler_params=_params(("parallel",)),
    )(w2, m2, v2, *parts)
    return tuple(o.reshape(shape) for o in outs)


BIG = ("w_in", "w_glu", "w_branch_ssm", "w_branch_attn", "w_out")
SHARD_AXIS = dict(w_in=1, w_glu=1, w_branch_ssm=1, w_branch_attn=1, w_out=0)
HBM = pl.BlockSpec(memory_space=pl.ANY)


def _position():
    x, y, c = lax.axis_index("x"), lax.axis_index("y"), lax.axis_index("c")
    other_chips = [(1 - x, y), (x, 1 - y), (1 - x, 1 - y)]
    return x, y, c, other_chips


def _window(ref, axis, shard, n_shards, half=None):
    rows, cols = ref.shape[-2:]
    sel = [slice(None), slice(None)]
    size = ref.shape[-2 + axis] // n_shards
    sel[axis] = pl.ds(pl.multiple_of(shard * size, 128), size)
    if half is not None:
        hsize = ref.shape[-1 - axis] // 2
        sel[1 - axis] = pl.ds(pl.multiple_of(half * hsize, 128), hsize)
    return ref.at[tuple(sel)]


def _half(ref, axis, half):
    hsize = ref.shape[-1 - axis] // 2
    sel = [slice(None), slice(None)]
    sel[1 - axis] = pl.ds(pl.multiple_of(half * hsize, 128), hsize)
    return ref.at[tuple(sel)]


def _remote(src, dst, send_sem, recv_sem, device):
    return pltpu.make_async_remote_copy(src_ref=src, dst_ref=dst, send_sem=send_sem, recv_sem=recv_sem,
                                        device_id=device, device_id_type=MESH)


def _gather_body(*refs, axes):
    n = len(axes)
    shards, outs = refs[:n], refs[n:2 * n]
    send_sems, recv_sems, fsend_sems, frecv_sems, local_sems = refs[2 * n:]
    x, y, c, chips = _position()
    me = 2 * x + y
    local, sends, passes = [], [], []
    for w, ax in enumerate(axes):
        cp = pltpu.make_async_copy(shards[w], _window(outs[w], ax, me, N_CHIPS), local_sems.at[w])
        cp.start()
        local.append(cp)
    for w, ax in enumerate(axes):
        for r, (px, py) in enumerate(chips):
            cp = _remote(_half(shards[w], ax, c), _window(outs[w], ax, me, N_CHIPS, c),
                         send_sems.at[3 * w + r], recv_sems.at[3 * w + r], (px, py, c))
            cp.start()
            sends.append(cp)
    for w, ax in enumerate(axes):
        for r, (px, py) in enumerate(chips):
            landed = _window(outs[w], ax, 2 * px + py, N_CHIPS, c)
            _remote(landed, landed, send_sems.at[3 * w + r], recv_sems.at[3 * w + r], (px, py, c)).wait_recv()
            cp = _remote(landed, landed, fsend_sems.at[3 * w + r], frecv_sems.at[3 * w + r], (x, y, 1 - c))
            cp.start()
            passes.append(cp)
    for w, ax in enumerate(axes):
        for r, (px, py) in enumerate(chips):
            other = _window(outs[w], ax, 2 * px + py, N_CHIPS, 1 - c)
            _remote(other, other, fsend_sems.at[3 * w + r], frecv_sems.at[3 * w + r], (x, y, 1 - c)).wait_recv()
    for cp in sends + passes:
        cp.wait_send()
    for cp in local:
        cp.wait()


def _all_gather(shards):
    axes = tuple(SHARD_AXIS[k] for k in BIG)
    out_shape = []
    for k in BIG:
        s = list(shards[k].shape)
        s[SHARD_AXIS[k]] *= N_CHIPS
        out_shape.append(jax.ShapeDtypeStruct(tuple(s), shards[k].dtype))
    n = len(BIG)
    outs = pl.pallas_call(
        functools.partial(_gather_body, axes=axes), name="gather_weights",
        in_specs=[HBM] * n, out_specs=[HBM] * n, out_shape=out_shape,
        scratch_shapes=[pltpu.SemaphoreType.DMA((3 * n,))] * 4 + [pltpu.SemaphoreType.DMA((n,))],
    )(*[shards[k] for k in BIG])
    return dict(zip(BIG, outs))


def _half_tile(n, h, nt, axis):
    return h * nt + n if axis == 1 else 2 * n + h


def _sibling_stream(n, nt, stage, recv, send_sems, recv_sems, credit, produce, consume):
    x, y, c, _ = _position()
    sibling = (x, y, 1 - c)

    def copy(slot):
        return _remote(stage.at[slot], recv.at[slot], send_sems.at[slot], recv_sems.at[slot], sibling)

    @pl.when(n < nt)
    def _():
        slot = n % 2

        @pl.when(n >= 2)
        def _():
            copy(slot).wait_send()
            pl.semaphore_wait(credit, 1)

        stage[slot] = produce().astype(stage.dtype)
        copy(slot).start()

    @pl.when(n >= 1)
    def _():
        slot = (n - 1) % 2
        copy(slot).wait_recv()
        consume(recv[slot])

        @pl.when(n - 1 < nt - 2)
        def _():
            pl.semaphore_signal(credit, 1, device_id=sibling, device_id_type=MESH)

    @pl.when(n == nt)
    def _():
        for slot in range(min(2, nt)):
            copy(slot).wait_send()


def _pair_reduce_body(c_ref, mine_ref, theirs_ref, out_ref, stage, recv, send_sems, recv_sems, credit, *, nt):
    del c_ref

    def consume(got):
        out_ref[...] = (mine_ref[...] + got.astype(F32)).astype(out_ref.dtype)

    _sibling_stream(pl.program_id(0), nt, stage, recv, send_sems, recv_sems, credit,
                    lambda: theirs_ref[...], consume)


def _pair_reduce(grad, core, axis, *, name, rows):
    r, c = grad.shape
    nt = r // 2 // rows
    assert nt * rows * 2 == r and (axis == 1 or rows == r // (2 * N_CHIPS))
    tile = lambda n, h: _half_tile(n, h, nt, axis)
    return pl.pallas_call(
        functools.partial(_pair_reduce_body, nt=nt), name=name,
        grid_spec=pltpu.PrefetchScalarGridSpec(
            num_scalar_prefetch=1, grid=(nt + 1,),
            in_specs=[pl.BlockSpec((rows, c), lambda n, cr: (tile(jnp.maximum(n - 1, 0), cr[0]), 0)),
                      pl.BlockSpec((rows, c), lambda n, cr: (tile(jnp.minimum(n, nt - 1), 1 - cr[0]), 0))],
            out_specs=pl.BlockSpec((rows, c), lambda n, cr: (jnp.maximum(n - 1, 0), 0)),
            scratch_shapes=[pltpu.VMEM((2, rows, c), BF16), pltpu.VMEM((2, rows, c), BF16),
                            pltpu.SemaphoreType.DMA((2,)), pltpu.SemaphoreType.DMA((2,)),
                            pltpu.SemaphoreType.REGULAR]),
        out_shape=jax.ShapeDtypeStruct((r // 2, c), BF16),
        compiler_params=_params(("arbitrary",)),
    )(core, grad, grad)


def _chip_add_share_body(c_ref, s0, s1, s2, s3, mine_ref, other_ref, stage, recv, send_sems, recv_sems, credit, *, nt):
    del c_ref

    def produce():
        total = s3[...].astype(F32) + s0[...].astype(F32) + s1[...].astype(F32) + s2[...].astype(F32)
        mine_ref[...] = total
        return total

    def consume(got):
        other_ref[...] = got

    _sibling_stream(pl.program_id(0), nt, stage, recv, send_sems, recv_sems, credit, produce, consume)


def _chip_add_share(slots, core, *, name, rows):
    _, r, c = slots.shape
    nt = r // rows
    assert nt * rows == r
    send = lambda j: pl.BlockSpec((None, rows, c), lambda n, cr: (j, jnp.minimum(n, nt - 1), 0))
    return pl.pallas_call(
        functools.partial(_chip_add_share_body, nt=nt), name=name,
        grid_spec=pltpu.PrefetchScalarGridSpec(
            num_scalar_prefetch=1, grid=(nt + 1,),
            in_specs=[send(j) for j in range(4)],
            out_specs=[pl.BlockSpec((rows, c), lambda n, cr: (jnp.minimum(n, nt - 1), 0)),
                       pl.BlockSpec((rows, c), lambda n, cr: (jnp.maximum(n - 1, 0), 0))],
            scratch_shapes=[pltpu.VMEM((2, rows, c), F32), pltpu.VMEM((2, rows, c), F32),
                            pltpu.SemaphoreType.DMA((2,)), pltpu.SemaphoreType.DMA((2,)),
                            pltpu.SemaphoreType.REGULAR]),
        out_shape=[jax.ShapeDtypeStruct((r, c), F32)] * 2,
        compiler_params=_params(("arbitrary",)),
    )(core, slots, slots, slots, slots)


def _chip_exchange_body(*refs, axes):
    n = len(axes)
    sums, slots = refs[:n], refs[n:2 * n]
    send_sems, recv_sems, local_sems = refs[2 * n:]
    x, y, c, chips = _position()
    me = 2 * x + y
    copies, local = [], []
    for w, ax in enumerate(axes):
        for r, (px, py) in enumerate(chips):
            cp = _remote(_window(sums[w], ax, 2 * px + py, N_CHIPS), slots[w].at[r],
                         send_sems.at[3 * w + r], recv_sems.at[3 * w + r], (px, py, c))
            cp.start()
            copies.append(cp)
        lc = pltpu.make_async_copy(_window(sums[w], ax, me, N_CHIPS), slots[w].at[3], local_sems.at[w])
        lc.start()
        local.append(lc)
    for cp in copies:
        cp.wait_recv()
    for cp in copies:
        cp.wait_send()
    for lc in local:
        lc.wait()


def _comm_call(body, name, ins, out_shape, n_sem_arrays, n_sems):
    axes = tuple(SHARD_AXIS[k] for k in BIG)
    return pl.pallas_call(
        functools.partial(body, axes=axes), name=name,
        in_specs=[HBM] * len(ins), out_specs=[HBM] * len(out_shape), out_shape=out_shape,
        scratch_shapes=[pltpu.SemaphoreType.DMA((n_sems,))] * n_sem_arrays + [pltpu.SemaphoreType.DMA((len(BIG),))],
    )(*ins)


PAIR_ROWS = dict(w_in=64, w_glu=128, w_branch_ssm=128, w_branch_attn=128, w_out=256)
SHARE_ROWS = dict(w_in=128, w_glu=128, w_branch_ssm=128, w_branch_attn=128, w_out=64)


def _reduce_grads(grads):
    n = len(BIG)
    core = lax.axis_index("c").astype(jnp.int32).reshape(1)
    pair = [_pair_reduce(grads[k], core, SHARD_AXIS[k], name="pair_reduce_" + k, rows=PAIR_ROWS[k]) for k in BIG]
    slot_shapes = []
    for k, p in zip(BIG, pair):
        s = list(p.shape)
        s[SHARD_AXIS[k]] //= N_CHIPS
        slot_shapes.append(jax.ShapeDtypeStruct((4, *s), BF16))
    slots = _comm_call(_chip_exchange_body, "chip_exchange", pair, slot_shapes, 2, 3 * n)
    return {k: _chip_add_share(slots[w], core, name="chip_add_share_" + k, rows=SHARE_ROWS[k])
            for w, k in enumerate(BIG)}


SMALL = (("ssm_lambda_re", (1, 64, 64)), ("ssm_lambda_im", (1, 64, 64)), ("ssm_b_re", (1, 64, 64, 16)),
         ("ssm_b_im", (1, 64, 64, 16)), ("ssm_c_re", (1, 64, 16, 64)), ("ssm_c_im", (1, 64, 16, 64)),
         ("ssm_d", (1, 1024)), ("ssm_log_step", (1, 64)), ("attn_sinks", (1, 16)), ("rel_bias_table", (32, 16)),
         ("ln_gain", (1, 2048)), ("ln_bias", (1, 2048)))
SMALL_SIZE = sum(int(np.prod(s)) for _, s in SMALL)
PACK_ROWS = -(-(SMALL_SIZE + 1) // (8 * 128)) * 8


def _pack(values, extra=None):
    flat = [values[k].reshape(-1).astype(F32) for k, _ in SMALL]
    flat.append(jnp.zeros((1,), F32) if extra is None else extra.reshape(1))
    flat.append(jnp.zeros((PACK_ROWS * 128 - SMALL_SIZE - 1,), F32))
    return jnp.concatenate(flat).reshape(PACK_ROWS, 128)


def _unpack(packed):
    flat = packed.reshape(-1)
    out, off = {}, 0
    for k, s in SMALL:
        size = int(np.prod(s))
        out[k] = flat[off:off + size].reshape(s)
        off += size
    return out, flat[off]


def _small_allreduce_body(p_ref, o_ref, buf, send_sems, recv_sems):
    x, y, c, _ = _position()
    me = 4 * x + 2 * y + c
    buf[0] = p_ref[...]
    copies = []
    for r in range(1, 8):
        peer = tuple(1 - v if (r >> s) & 1 else v for v, s in ((x, 2), (y, 1), (c, 0)))
        cp = _remote(p_ref, buf.at[r], send_sems.at[r - 1], recv_sems.at[r - 1], peer)
        cp.start()
        copies.append(cp)
    for cp in copies:
        cp.wait_recv()
    for cp in copies:
        cp.wait_send()
    acc = buf[me]
    for s in range(1, 8):
        acc = acc + buf[jnp.bitwise_xor(me, s)]
    o_ref[...] = acc


def _small_allreduce(packed):
    vm = pl.BlockSpec(memory_space=pltpu.VMEM)
    return pl.pallas_call(
        functools.partial(_small_allreduce_body), name="small_allreduce",
        in_specs=[vm], out_specs=vm, out_shape=jax.ShapeDtypeStruct(packed.shape, F32),
        scratch_shapes=[pltpu.VMEM((8,) + packed.shape, F32), pltpu.SemaphoreType.DMA((7,)),
                        pltpu.SemaphoreType.DMA((7,))],
        compiler_params=pltpu.CompilerParams(vmem_limit_bytes=VMEM_LIMIT),
    )(packed)


def _local_step(x, target, w_in, w_glu, w_bs, w_ba, w_out, lam_re, lam_im, b_re, b_im, c_re, c_im, d_skip,
                log_step, sinks, rel_bias_table, ln_gain, ln_bias):
    t = x.shape[0]
    bucket = jnp.asarray(_bucket_table())
    p_re, p_im, b_bd, c_bd = _ssm_prepare(lam_re, lam_im, b_re, b_im, c_re, c_im, log_step)
    bias = _bias_build(rel_bias_table, bucket)

    xb = _cast(x, BF16, name="cast_x")
    proj = _matmul(xb, w_in, "nn", name="proj", tm=1024, tn=512)
    y_ssm, g_in, car_re, car_im = _ssm_forward(proj, p_re, p_im, b_bd, c_bd, d_skip)
    glu = _matmul(g_in, w_glu, "nn", name="glu", tm=1024, tn=512)
    (h_ssm,) = _rowwise(functools.partial(_ssm_gate_fwd_body), "ssm_gate_fwd", t,
                        [(glu, 2 * D_SSM, 0), (proj, D_SSM, C_ZS)], [((t, D_SSM), BF16, D_SSM, 0)])
    attn, lse = _attn_forward(proj, sinks, bias)
    (h_attn,) = _rowwise(functools.partial(_attn_gate_fwd_body), "attn_gate_fwd", t,
                         [(attn, D_ATTN, 0), (proj, D_ATTN, C_ZA)], [((t, D_ATTN), BF16, D_ATTN, 0)])
    bs = _matmul(h_ssm, w_bs, "nn", name="branch_ssm", tm=1024, tn=512)
    ba = _matmul(h_attn, w_ba, "nn", name="branch_attn", tm=1024, tn=512)
    gl_in = (proj, 2 * D_MODEL, C_GL)
    (merged,) = _rowwise(functools.partial(_merge_fwd_body), "merge_fwd", t,
                         [(bs, D_MODEL, 0), (ba, D_MODEL, 0), gl_in], [((t, D_MODEL), BF16, D_MODEL, 0)])
    out = _matmul(merged, w_out, "nn", name="out_proj", tm=1024, tn=512)
    dr, loss, g_gain, g_bias = _rowwise(
        functools.partial(_ln_loss_body), "ln_loss", t,
        [(x, D_MODEL, 0), (out, D_MODEL, 0), (target, D_MODEL, 0), (ln_gain, None, 0), (ln_bias, None, 0)],
        [((t, D_MODEL), F32, D_MODEL, 0), ((1, 1), F32, None, 0), ((1, D_MODEL), F32, None, 0),
         ((1, D_MODEL), F32, None, 0)])

    drb = _cast(dr, BF16, name="cast_dr")
    g_w_out = _matmul(merged, drb, "tn", name="g_w_out", tm=1024, tn=512, tk=1024)
    d_merged = _matmul(drb, w_out, "nt", name="d_merged", tm=1024, tn=512)
    d_bs, d_ba, dproj = _rowwise(
        functools.partial(_merge_bwd_body), "merge_bwd", t,
        [(d_merged, D_MODEL, 0), (bs, D_MODEL, 0), (ba, D_MODEL, 0), gl_in],
        [((t, D_MODEL), BF16, D_MODEL, 0), ((t, D_MODEL), BF16, D_MODEL, 0),
         ((t, D_IN), BF16, 2 * D_MODEL, C_GL)])
    g_w_bs = _matmul(h_ssm, d_bs, "tn", name="g_w_branch_ssm", tm=1024, tn=512, tk=1024)
    d_h_ssm = _matmul(d_bs, w_bs, "nt", name="d_h_ssm", tm=1024, tn=512)
    g_w_ba = _matmul(h_attn, d_ba, "tn", name="g_w_branch_attn", tm=1024, tn=512, tk=1024)
    d_h_attn = _matmul(d_ba, w_ba, "nt", name="d_h_attn", tm=1024, tn=512)

    d_attn, dproj = _rowwise(
        functools.partial(_attn_gate_bwd_body), "attn_gate_bwd", t,
        [(d_h_attn, D_ATTN, 0), (attn, D_ATTN, 0), (proj, D_ATTN, C_ZA), (dproj, None, "any")],
        [((t, D_ATTN), F32, D_ATTN, 0), ((t, D_IN), BF16, D_ATTN, C_ZA)], aliases={3: 1})
    dproj, d_kv, g_bias_full, g_sinks = _attn_backward(proj, sinks, bias, d_attn, attn, lse, dproj)
    (dproj,) = _rowwise(functools.partial(_place_body), "place_dkv", t,
                        [(d_kv, 2 * D_KV, 0), (dproj, None, "any")],
                        [((t, D_IN), BF16, 2 * D_KV, C_K)], aliases={1: 0})
    g_table = _bias_grad(g_bias_full, bucket)

    d_glu, dproj = _rowwise(
        functools.partial(_ssm_gate_bwd_body), "ssm_gate_bwd", t,
        [(d_h_ssm, D_SSM, 0), (glu, 2 * D_SSM, 0), (proj, D_SSM, C_ZS), (dproj, None, "any")],
        [((t, 2 * D_SSM), BF16, 2 * D_SSM, 0), ((t, D_IN), BF16, D_SSM, C_ZS)], aliases={3: 1})
    g_w_glu = _matmul(g_in, d_glu, "tn", name="g_w_glu", tm=1024, tn=512, tk=1024)
    d_g_in = _matmul(d_glu, w_glu, "nt", name="d_g_in", tm=1024, tn=512)
    dproj, g_bbd, g_cbd, g_lam_re, g_lam_im, g_d = _ssm_backward(
        proj, d_g_in, y_ssm, car_re, car_im, p_re, p_im, b_bd, c_bd, d_skip, dproj)
    g_lr, g_li, g_br, g_bi, g_cr, g_ci, g_ls = _ssm_param_grads(
        lam_re, lam_im, b_re, b_im, log_step, g_lam_re, g_lam_im, g_bbd, g_cbd)

    g_w_in = _matmul(xb, dproj, "tn", name="g_w_in", tm=1024, tn=512, tk=1024)
    grad_x = _matmul(dproj, w_in, "nt", name="grad_x", tm=1024, tn=512, tk=2176, res=dr, res_scale=ALPHA)

    big = dict(w_in=g_w_in, w_glu=g_w_glu, w_branch_ssm=g_w_bs, w_branch_attn=g_w_ba, w_out=g_w_out)
    small = dict(ssm_lambda_re=g_lr, ssm_lambda_im=g_li, ssm_b_re=g_br, ssm_b_im=g_bi, ssm_c_re=g_cr,
                 ssm_c_im=g_ci, ssm_d=g_d, ssm_log_step=g_ls, attn_sinks=g_sinks[:, :N_Q_HEADS],
                 rel_bias_table=g_table, ln_gain=g_gain, ln_bias=g_bias)
    return loss, grad_x, big, small


WEIGHTS = ("w_in", "ssm_lambda_re", "ssm_lambda_im", "ssm_b_re", "ssm_b_im", "ssm_c_re", "ssm_c_im", "ssm_d",
           "ssm_log_step", "w_glu", "attn_sinks", "rel_bias_table", "w_branch_ssm", "w_branch_attn", "w_out",
           "ln_gain", "ln_bias")


def kernel(x, w_in, ssm_lambda_re, ssm_lambda_im, ssm_b_re, ssm_b_im, ssm_c_re, ssm_c_im, ssm_d, ssm_log_step, w_glu, attn_sinks, rel_bias_table, w_branch_ssm, w_branch_attn, w_out, ln_gain, ln_bias, loss_target, m_w_in, m_ssm_lambda_re, m_ssm_lambda_im, m_ssm_b_re, m_ssm_b_im, m_ssm_c_re, m_ssm_c_im, m_ssm_d, m_ssm_log_step, m_w_glu, m_attn_sinks, m_rel_bias_table, m_w_branch_ssm, m_w_branch_attn, m_w_out, m_ln_gain, m_ln_bias, v_w_in, v_ssm_lambda_re, v_ssm_lambda_im, v_ssm_b_re, v_ssm_b_im, v_ssm_c_re, v_ssm_c_im, v_ssm_d, v_ssm_log_step, v_w_glu, v_attn_sinks, v_rel_bias_table, v_w_branch_ssm, v_w_branch_attn, v_w_out, v_ln_gain, v_ln_bias):
    given = dict(locals())
    w = {k: given[k] for k in WEIGHTS}
    m = {k: given["m_" + k] for k in WEIGHTS}
    v = {k: given["v_" + k] for k in WEIGHTS}

    shards = {k: _cast(w[k][0], BF16, name="cast_" + k) for k in BIG}
    full = _all_gather(shards)
    loss, grad_x, g_big, g_small = _local_step(
        x[0], loss_target[0], full["w_in"], full["w_glu"], full["w_branch_ssm"], full["w_branch_attn"],
        full["w_out"], ssm_lambda_re[0], ssm_lambda_im[0], ssm_b_re[0], ssm_b_im[0], ssm_c_re[0], ssm_c_im[0],
        ssm_d, ssm_log_step, attn_sinks, rel_bias_table, ln_gain, ln_bias)

    g_shard = _reduce_grads(g_big)
    g_packed = _small_allreduce(_pack(g_small, loss))
    loss_sum = _unpack(g_packed)[1]

    grad, delta, new_m, new_v = {}, {}, {}, {}
    core = lax.axis_index("c").astype(jnp.int32).reshape(1)
    for k in BIG:
        grad[k], delta[k], new_m[k], new_v[k] = _adamw_shard(w[k], m[k], v[k], g_shard[k], core,
                                                             name="adamw_" + k, rows=SHARE_ROWS[k])
    gs, ds, ms, vs = _adamw(_pack(w), _pack(m), _pack(v), [g_packed], name="adamw_small")
    for dst, packed in ((grad, gs), (delta, ds), (new_m, ms), (new_v, vs)):
        dst.update(_unpack(packed)[0])

    return (loss_sum, grad_x[None], *[grad[k] for k in WEIGHTS], *[delta[k] for k in WEIGHTS],
            *[new_m[k] for k in WEIGHTS], *[new_v[k] for k in WEIGHTS])
```

```python
import functools
import math

import numpy as np
import jax
import jax.numpy as jnp
from jax import lax
from jax.experimental import pallas as pl
from jax.experimental.pallas import tpu as pltpu

F32 = jnp.float32
BF16 = jnp.bfloat16

D_MODEL = 2048
D_SSM = 1024
SSM_GROUP = 16
N_GROUPS = 64
SSM_STATE = 64
N_LANES = N_GROUPS * SSM_STATE
N_Q_HEADS = 16
N_KV_HEADS = 4
HEAD_DIM = 64
Q_PER_KV = 4
D_ATTN = 1024
D_KV = 256
WINDOW = 128
BLOCK = 128
N_BUCKETS = 32
MAX_DISTANCE = 128
D_IN = 8704
ALPHA = 2.0 ** 0.25
LN_EPS = 1e-5
NEG_INF = -1e30
ATTN_SCALE = HEAD_DIM ** -0.5

C_U, C_ZS, C_Q, C_K, C_V, C_ZA, C_GL = 0, 1024, 2048, 3072, 3328, 3584, 4608

ADAM_LR = 0.001
ADAM_B1 = 0.9
ADAM_B2 = 0.999
ADAM_EPS = 1e-08
ADAM_WD = 0.01
ADAM_STEP = 10

N_CHIPS = 4
MESH = pl.DeviceIdType.MESH

SSM_CHUNK = 256
SEG_LEN = SSM_CHUNK // 8
SLAB_LANES = 512
N_SLABS = N_LANES // SLAB_LANES
SLAB_CH = D_SSM // N_SLABS

VMEM_LIMIT = 60 * 1024 * 1024


def _params(sem=None, **kw):
    return pltpu.CompilerParams(dimension_semantics=sem, vmem_limit_bytes=VMEM_LIMIT, **kw)


_DIMS = {"nn": (((1,), (0,)), ((), ())), "nt": (((1,), (1,)), ((), ())), "tn": (((0,), (0,)), ((), ()))}


def _mm_body(*refs, dims, nk, res_scale):
    if res_scale is None:
        a_ref, b_ref, o_ref, acc_ref = refs
        r_ref = None
    else:
        a_ref, b_ref, r_ref, o_ref, acc_ref = refs
    k = pl.program_id(2)
    part = lax.dot_general(a_ref[...].astype(BF16), b_ref[...].astype(BF16), _DIMS[dims],
                           preferred_element_type=F32)

    def finish(acc):
        if r_ref is not None:
            acc = acc + res_scale * r_ref[...]
        o_ref[...] = acc.astype(o_ref.dtype)

    if nk == 1:
        finish(part)
    else:
        @pl.when(k == 0)
        def _():
            acc_ref[...] = part

        @pl.when(k > 0)
        def _():
            acc_ref[...] += part

        @pl.when(k == nk - 1)
        def _():
            finish(acc_ref[...])


def _matmul(a, b, dims, *, name, out_dtype=F32, tm=512, tn=512, tk=None, res=None, res_scale=None):
    if dims == "nn":
        (m, kk), n = a.shape, b.shape[1]
    elif dims == "nt":
        (m, kk), n = a.shape, b.shape[0]
    else:
        (kk, m), n = a.shape, b.shape[1]
    tm, tn = min(tm, m), min(tn, n)
    tk = kk if tk is None else min(tk, kk)
    assert m % tm == 0 and n % tn == 0 and kk % tk == 0, (name, m, n, kk, tm, tn, tk)
    nk = kk // tk
    a_spec = {"nn": pl.BlockSpec((tm, tk), lambda i, j, k: (i, k)),
              "nt": pl.BlockSpec((tm, tk), lambda i, j, k: (i, k)),
              "tn": pl.BlockSpec((tk, tm), lambda i, j, k: (k, i))}[dims]
    b_spec = {"nn": pl.BlockSpec((tk, tn), lambda i, j, k: (k, j)),
              "nt": pl.BlockSpec((tn, tk), lambda i, j, k: (j, k)),
              "tn": pl.BlockSpec((tk, tn), lambda i, j, k: (k, j))}[dims]
    in_specs, args = [a_spec, b_spec], [a, b]
    if res is not None:
        in_specs.append(pl.BlockSpec((tm, tn), lambda i, j, k: (i, j)))
        args.append(res)
    return pl.pallas_call(
        functools.partial(_mm_body, dims=dims, nk=nk, res_scale=res_scale if res is not None else None),
        name=name,
        grid=(m // tm, n // tn, nk),
        in_specs=in_specs,
        out_specs=pl.BlockSpec((tm, tn), lambda i, j, k: (i, j)),
        out_shape=jax.ShapeDtypeStruct((m, n), out_dtype),
        scratch_shapes=[pltpu.VMEM((tm, tn), F32)],
        compiler_params=_params(("parallel", "parallel", "arbitrary")),
    )(*args)


def _sigmoid(v):
    return 1.0 / (1.0 + jnp.exp(-v))


def _silu_and_grad(z):
    s = _sigmoid(z)
    return z * s, s * (1.0 + z * (1.0 - s))


def _cast_body(x_ref, o_ref):
    o_ref[...] = x_ref[...].astype(o_ref.dtype)


def _cast_and_transpose_body(x_ref, o_ref, ot_ref):
    xb = x_ref[...].astype(BF16)
    o_ref[...] = xb
    ot_ref[...] = xb.T


def _cast_and_transpose(x, *, name, rows=512):
    m, n = x.shape
    rows = min(rows, m)
    return pl.pallas_call(
        functools.partial(_cast_and_transpose_body), name=name, grid=(m // rows,),
        in_specs=[pl.BlockSpec((rows, n), lambda i: (i, 0))],
        out_specs=[pl.BlockSpec((rows, n), lambda i: (i, 0)), pl.BlockSpec((n, rows), lambda i: (0, i))],
        out_shape=[jax.ShapeDtypeStruct((m, n), BF16), jax.ShapeDtypeStruct((n, m), BF16)],
        compiler_params=_params(("parallel",)),
    )(x)


def _cast(x, dtype, *, name, rows=512):
    m, n = x.shape
    rows = min(rows, m)
    return pl.pallas_call(
        functools.partial(_cast_body), name=name, grid=(m // rows,),
        in_specs=[pl.BlockSpec((rows, n), lambda i: (i, 0))],
        out_specs=pl.BlockSpec((rows, n), lambda i: (i, 0)),
        out_shape=jax.ShapeDtypeStruct((m, n), dtype),
        compiler_params=_params(("parallel",)),
    )(x)


def _lam_bar(lr, li, ls):
    step = jnp.exp(ls)
    er = jnp.exp(lr * step)
    return step, er * jnp.cos(li * step), er * jnp.sin(li * step)


def _ssm_pow_body(lr_ref, li_ref, ls_ref, pr_ref, pi_ref):
    _, ar, ai = _lam_bar(lr_ref[...], li_ref[...], ls_ref[...])
    cr, ci = ar, ai
    for i in range(SEG_LEN):
        pr_ref[pl.ds(i, 1), :] = cr
        pi_ref[pl.ds(i, 1), :] = ci
        cr, ci = cr * ar - ci * ai, cr * ai + ci * ar


def _ssm_bbar_body(lr_ref, li_ref, ls_ref, br_ref, bi_ref, or_ref, oi_ref):
    lr, li = lr_ref[...], li_ref[...]
    _, ar, ai = _lam_bar(lr, li, ls_ref[...])
    d = lr * lr + li * li
    ir, ii = lr / d, -li / d
    nr, ni = ar - 1.0, ai
    cr, ci = nr * ir - ni * ii, nr * ii + ni * ir
    br, bi = br_ref[...], bi_ref[...]
    or_ref[...] = cr * br - ci * bi
    oi_ref[...] = cr * bi + ci * br


def _ssm_param_bwd_body(lr_ref, li_ref, ls_ref, br_ref, bi_ref, glr_ref, gli_ref, gbr_ref, gbi_ref,
                        dlr_ref, dli_ref, dls_ref, dbr_ref, dbi_ref):
    lr, li = lr_ref[...], li_ref[...]
    step, ar, ai = _lam_bar(lr, li, ls_ref[...])
    d = lr * lr + li * li
    ir, ii = lr / d, -li / d
    nr, ni = ar - 1.0, ai
    cr, ci = nr * ir - ni * ii, nr * ii + ni * ir
    br, bi, gbr, gbi = br_ref[...], bi_ref[...], gbr_ref[...], gbi_ref[...]
    dbr_ref[...] = cr * gbr + ci * gbi
    dbi_ref[...] = cr * gbi - ci * gbr
    gcr = jnp.sum(br * gbr + bi * gbi, axis=1, keepdims=True)
    gci = jnp.sum(br * gbi - bi * gbr, axis=1, keepdims=True)
    gnr, gni = ir * gcr + ii * gci, ir * gci - ii * gcr
    gir, gii = nr * gcr + ni * gci, nr * gci - ni * gcr
    gtr, gti = glr_ref[...] + gnr, gli_ref[...] + gni
    i2r, i2i = ir * ir - ii * ii, 2.0 * ir * ii
    g1r, g1i = -(i2r * gir + i2i * gii), -(i2r * gii - i2i * gir)
    g2r, g2i = step * (ar * gtr + ai * gti), step * (ar * gti - ai * gtr)
    mr, mi = lr * ar - li * ai, lr * ai + li * ar
    dlr_ref[...] = g1r + g2r
    dli_ref[...] = g1i + g2i
    dls_ref[...] = (mr * gtr + mi * gti) * step


def _whole(shape):
    return pl.BlockSpec(shape, lambda *_: (0,) * len(shape))


def _ssm_prepare(lam_re, lam_im, b_re, b_im, c_re, c_im, log_step):
    row = lambda a: a.reshape(1, N_LANES)
    col = lambda a: a.reshape(N_LANES, 1)
    ls = jnp.repeat(log_step.reshape(N_GROUPS), SSM_STATE)
    p_re, p_im = pl.pallas_call(
        functools.partial(_ssm_pow_body), name="ssm_pow",
        in_specs=[_whole((1, N_LANES))] * 3, out_specs=[_whole((SEG_LEN, N_LANES))] * 2,
        out_shape=[jax.ShapeDtypeStruct((SEG_LEN, N_LANES), F32)] * 2, grid=(1,),
    )(row(lam_re), row(lam_im), row(ls))
    bb_re, bb_im = pl.pallas_call(
        functools.partial(_ssm_bbar_body), name="ssm_bbar",
        in_specs=[_whole((N_LANES, 1))] * 3 + [_whole((N_LANES, SSM_GROUP))] * 2,
        out_specs=[_whole((N_LANES, SSM_GROUP))] * 2,
        out_shape=[jax.ShapeDtypeStruct((N_LANES, SSM_GROUP), F32)] * 2, grid=(1,),
    )(col(lam_re), col(lam_im), col(ls), b_re.reshape(N_LANES, SSM_GROUP), b_im.reshape(N_LANES, SSM_GROUP))
    eye = jnp.eye(8, dtype=F32)

    def b_slabs(bb):
        t = bb.reshape(N_SLABS, 8, SSM_STATE, SSM_GROUP).transpose(0, 1, 3, 2)
        return (t[:, :, :, None, :] * eye[None, :, None, :, None]).reshape(N_SLABS, SLAB_CH, SLAB_LANES)

    def c_slabs(c):
        t = c.reshape(N_SLABS, 8, SSM_GROUP, SSM_STATE).transpose(0, 1, 3, 2)
        return (t[:, :, :, None, :] * eye[None, :, None, :, None]).reshape(N_SLABS, SLAB_LANES, SLAB_CH)

    b_bd = jnp.concatenate([b_slabs(bb_re), b_slabs(bb_im)], axis=2).astype(BF16)
    c_bd = jnp.concatenate([c_slabs(c_re.reshape(N_GROUPS, SSM_GROUP, SSM_STATE)),
                            -c_slabs(c_im.reshape(N_GROUPS, SSM_GROUP, SSM_STATE))], axis=1).astype(BF16)
    return p_re, p_im, b_bd, c_bd


def _diag_blocks_b(g):
    t = g.reshape(N_SLABS, 8, SSM_GROUP, 8, SSM_STATE)
    t = jnp.stack([t[:, i, :, i, :] for i in range(8)], axis=1)
    return t.transpose(0, 1, 3, 2).reshape(N_LANES, SSM_GROUP)


def _diag_blocks_c(g):
    t = g.reshape(N_SLABS, 8, SSM_STATE, 8, SSM_GROUP)
    t = jnp.stack([t[:, i, :, i, :] for i in range(8)], axis=1)
    return t.transpose(0, 1, 3, 2).reshape(N_GROUPS, SSM_GROUP, SSM_STATE)


def _ssm_param_grads(lam_re, lam_im, b_re, b_im, log_step, g_lam_re, g_lam_im, g_bbd, g_cbd):
    col = lambda a: a.reshape(N_LANES, 1)
    ls = jnp.repeat(log_step.reshape(N_GROUPS), SSM_STATE)
    gbr = _diag_blocks_b(g_bbd[:, :, :SLAB_LANES])
    gbi = _diag_blocks_b(g_bbd[:, :, SLAB_LANES:])
    outs = pl.pallas_call(
        functools.partial(_ssm_param_bwd_body), name="ssm_param_bwd", grid=(1,),
        in_specs=[_whole((N_LANES, 1))] * 3 + [_whole((N_LANES, SSM_GROUP))] * 2 + [_whole((N_LANES, 1))] * 2
        + [_whole((N_LANES, SSM_GROUP))] * 2,
        out_specs=[_whole((N_LANES, 1))] * 3 + [_whole((N_LANES, SSM_GROUP))] * 2,
        out_shape=[jax.ShapeDtypeStruct((N_LANES, 1), F32)] * 3 + [jax.ShapeDtypeStruct((N_LANES, SSM_GROUP), F32)] * 2,
    )(col(lam_re), col(lam_im), col(ls), b_re.reshape(N_LANES, SSM_GROUP), b_im.reshape(N_LANES, SSM_GROUP),
      col(g_lam_re), col(g_lam_im), gbr, gbi)
    dlr, dli, dls, dbr, dbi = outs
    d_c_re = _diag_blocks_c(g_cbd[:, :SLAB_LANES, :])
    d_c_im = -_diag_blocks_c(g_cbd[:, SLAB_LANES:, :])
    return (dlr.reshape(1, N_GROUPS, SSM_STATE), dli.reshape(1, N_GROUPS, SSM_STATE),
            dbr.reshape(1, N_GROUPS, SSM_STATE, SSM_GROUP), dbi.reshape(1, N_GROUPS, SSM_STATE, SSM_GROUP),
            d_c_re[None], d_c_im[None], dls.reshape(N_GROUPS, SSM_STATE).sum(axis=1).reshape(1, N_GROUPS))


def _bcast8(v):
    return jnp.broadcast_to(v, (8, v.shape[1]))


def _segment_permutation():
    p = np.zeros((SSM_CHUNK, SSM_CHUNK), np.float32)
    rows = np.arange(SSM_CHUNK)
    p[rows, (rows % 8) * SEG_LEN + rows // 8] = 1.0
    return p


def _permute_exact(perm, val, pieces):
    out, rest = None, val
    for n in range(pieces):
        part = rest.astype(BF16)
        moved = jnp.dot(perm, part, preferred_element_type=F32)
        out = moved if out is None else out + moved
        if n + 1 < pieces:
            rest = rest - part.astype(F32)
    return out


def _scan_buffer():
    return pltpu.VMEM((SSM_CHUNK, N_LANES), F32)


def _lanes(k):
    return pl.ds(k * SLAB_LANES, SLAB_LANES)


def _tile(i):
    return pl.ds(i * 8 if isinstance(i, int) else pl.multiple_of(i * 8, 8), 8)


def _seg_get(ref, k, i):
    return ref[_tile(i), _lanes(k)]


def _seg_put(ref, k, i, val):
    ref[_tile(i), _lanes(k)] = val


def _slab_get(ref, k):
    return ref[:, _lanes(k)]


def _slab_put(ref, k, val):
    ref[:, _lanes(k)] = val


def _scan_forward(s_re, s_im, p_re, p_im, car_re, car_im, sp_re=None, sp_im=None):
    for k in range(N_SLABS):
        ln = pl.ds(k * SLAB_LANES, SLAB_LANES)
        ar, ai = _bcast8(p_re[pl.ds(0, 1), ln]), _bcast8(p_im[pl.ds(0, 1), ln])

        def step(i, s, k=k, ar=ar, ai=ai):
            sr, si = s
            nr = ar * sr - ai * si + _seg_get(s_re, k, i)
            ni = ar * si + ai * sr + _seg_get(s_im, k, i)
            _seg_put(s_re, k, i, nr)
            _seg_put(s_im, k, i, ni)
            return nr, ni

        zero = jnp.zeros((8, SLAB_LANES), F32)
        er, ei = lax.fori_loop(0, SEG_LEN, step, (zero, zero), unroll=4)
        lr, li = p_re[pl.ds(SEG_LEN - 1, 1), ln], p_im[pl.ds(SEG_LEN - 1, 1), ln]
        cr, ci = car_re[pl.ds(0, 1), ln], car_im[pl.ds(0, 1), ln]
        rows_r, rows_i = [], []
        for r in range(8):
            rows_r.append(cr)
            rows_i.append(ci)
            cr, ci = er[r:r + 1] + lr * cr - li * ci, ei[r:r + 1] + lr * ci + li * cr
        pr8, pi8 = jnp.concatenate(rows_r, axis=0), jnp.concatenate(rows_i, axis=0)
        car_re[:, ln] = _bcast8(cr)
        car_im[:, ln] = _bcast8(ci)
        if sp_re is not None:
            sp_re[:, ln] = pr8
            sp_im[:, ln] = pi8

        def fix(i, _, k=k, ln=ln, pr8=pr8, pi8=pi8):
            qr, qi = _bcast8(p_re[pl.ds(i, 1), ln]), _bcast8(p_im[pl.ds(i, 1), ln])
            _seg_put(s_re, k, i, _seg_get(s_re, k, i) + qr * pr8 - qi * pi8)
            _seg_put(s_im, k, i, _seg_get(s_im, k, i) + qr * pi8 + qi * pr8)
            return 0

        lax.fori_loop(0, SEG_LEN, fix, 0, unroll=4)


def _scan_backward(g_re, g_im, s_re, s_im, sp_re, sp_im, p_re, p_im, car_re, car_im, acc_re, acc_im):
    for k in range(N_SLABS):
        ln = pl.ds(k * SLAB_LANES, SLAB_LANES)
        ar, ai = _bcast8(p_re[pl.ds(0, 1), ln]), -_bcast8(p_im[pl.ds(0, 1), ln])

        def step(j, s, k=k, ar=ar, ai=ai):
            i = SEG_LEN - 1 - j
            sr, si = s
            nr = ar * sr - ai * si + _seg_get(g_re, k, i)
            ni = ar * si + ai * sr + _seg_get(g_im, k, i)
            _seg_put(g_re, k, i, nr)
            _seg_put(g_im, k, i, ni)
            return nr, ni

        zero = jnp.zeros((8, SLAB_LANES), F32)
        er, ei = lax.fori_loop(0, SEG_LEN, step, (zero, zero), unroll=4)
        lr, li = p_re[pl.ds(SEG_LEN - 1, 1), ln], -p_im[pl.ds(SEG_LEN - 1, 1), ln]
        cr, ci = car_re[pl.ds(0, 1), ln], car_im[pl.ds(0, 1), ln]
        rows_r, rows_i = [None] * 8, [None] * 8
        for r in range(7, -1, -1):
            rows_r[r], rows_i[r] = cr, ci
            cr, ci = er[r:r + 1] + lr * cr - li * ci, ei[r:r + 1] + lr * ci + li * cr
        nr8, ni8 = jnp.concatenate(rows_r, axis=0), jnp.concatenate(rows_i, axis=0)
        car_re[:, ln] = _bcast8(cr)
        car_im[:, ln] = _bcast8(ci)

        def fix(i, acc, k=k, ln=ln, nr8=nr8, ni8=ni8):
            qr = _bcast8(p_re[pl.ds(SEG_LEN - 1 - i, 1), ln])
            qi = -_bcast8(p_im[pl.ds(SEG_LEN - 1 - i, 1), ln])
            gr = _seg_get(g_re, k, i) + qr * nr8 - qi * ni8
            gi = _seg_get(g_im, k, i) + qr * ni8 + qi * nr8
            _seg_put(g_re, k, i, gr)
            _seg_put(g_im, k, i, gi)
            return gr, gi

        def prod(xr, xi, gr, gi):
            return xr * gr + xi * gi, xr * gi - xi * gr

        gr, gi = fix(0, None)
        a_r, a_i = prod(sp_re[:, ln], sp_im[:, ln], gr, gi)

        def fix_acc(i, acc, k=k, fix=fix):
            gr, gi = fix(i, None)
            dr, di = prod(_seg_get(s_re, k, i - 1), _seg_get(s_im, k, i - 1), gr, gi)
            return acc[0] + dr, acc[1] + di

        a_r, a_i = lax.fori_loop(1, SEG_LEN, fix_acc, (a_r, a_i))
        acc_re[:, ln] += a_r
        acc_im[:, ln] += a_i


def _gelu_and_grad(y):
    cdf = 0.5 * (1.0 + lax.erf(y * (2.0 ** -0.5)))
    pdf = jnp.exp(-0.5 * y * y) * (1.0 / math.sqrt(2.0 * math.pi))
    return y * cdf, cdf + y * pdf


def _ssm_fwd_body(u_ref, bbd_ref, cbd_ref, pre_ref, pim_ref, d_ref, perm_ref, unperm_ref,
                  y_ref, gin_ref, cre_out, cim_out, s_re, s_im, car_re, car_im, yp):
    c = pl.program_id(0)

    @pl.when(c == 0)
    def _():
        car_re[...] = jnp.zeros_like(car_re)
        car_im[...] = jnp.zeros_like(car_im)

    cre_out[...] = car_re[...]
    cim_out[...] = car_im[...]
    u = u_ref[...]
    up = jnp.dot(perm_ref[...], u.astype(BF16), preferred_element_type=F32).astype(BF16)
    for k in range(N_SLABS):
        bu = jnp.dot(up[:, k * SLAB_CH:(k + 1) * SLAB_CH], bbd_ref[k], preferred_element_type=F32)
        _slab_put(s_re, k, bu[:, :SLAB_LANES])
        _slab_put(s_im, k, bu[:, SLAB_LANES:])
    _scan_forward(s_re, s_im, pre_ref, pim_ref, car_re, car_im)
    for k in range(N_SLABS):
        yp[:, pl.ds(k * SLAB_CH, SLAB_CH)] = (
            jnp.dot(_slab_get(s_re, k).astype(BF16), cbd_ref[k, :SLAB_LANES, :], preferred_element_type=F32)
            + jnp.dot(_slab_get(s_im, k).astype(BF16), cbd_ref[k, SLAB_LANES:, :], preferred_element_type=F32))
    y = _permute_exact(unperm_ref[...], yp[...], 3) + d_ref[...] * u
    y_ref[...] = y
    gin_ref[...] = _gelu_and_grad(y)[0].astype(BF16)


def _ssm_forward(proj, p_re, p_im, b_bd, c_bd, d_skip):
    t = proj.shape[0]
    nc = t // SSM_CHUNK
    perm = _segment_permutation()
    return pl.pallas_call(
        functools.partial(_ssm_fwd_body), name="ssm_fwd", grid=(nc,),
        in_specs=[pl.BlockSpec((SSM_CHUNK, D_SSM), lambda c: (c, C_U // D_SSM)),
                  _whole(b_bd.shape), _whole(c_bd.shape), _whole(p_re.shape), _whole(p_im.shape),
                  _whole((1, D_SSM)), _whole(perm.shape), _whole(perm.shape)],
        out_specs=[pl.BlockSpec((SSM_CHUNK, D_SSM), lambda c: (c, 0)),
                   pl.BlockSpec((SSM_CHUNK, D_SSM), lambda c: (c, 0)),
                   pl.BlockSpec((None, 8, N_LANES), lambda c: (c, 0, 0)),
                   pl.BlockSpec((None, 8, N_LANES), lambda c: (c, 0, 0))],
        out_shape=[jax.ShapeDtypeStruct((t, D_SSM), F32), jax.ShapeDtypeStruct((t, D_SSM), BF16),
                   jax.ShapeDtypeStruct((nc, 8, N_LANES), F32), jax.ShapeDtypeStruct((nc, 8, N_LANES), F32)],
        scratch_shapes=[_scan_buffer(), _scan_buffer(),
                        pltpu.VMEM((8, N_LANES), F32), pltpu.VMEM((8, N_LANES), F32),
                        pltpu.VMEM((SSM_CHUNK, D_SSM), F32)],
        compiler_params=_params(("arbitrary",)),
    )(proj, b_bd, c_bd, p_re, p_im, d_skip, jnp.asarray(perm, BF16), jnp.asarray(perm.T, BF16))


def _ssm_bwd_body(u_ref, dgin_ref, y_ref, cre_in, cim_in, bbd_ref, cbd_ref, pre_ref, pim_ref, d_ref, perm_ref,
                  unperm_ref, dproj_in,
                  du_ref, gb_ref, gc_ref, glr_ref, gli_ref, gd_ref,
                  s_re, s_im, g_re, g_im, sp_re, sp_im, car_re, car_im, gcar_re, gcar_im, acc_re, acc_im, dup):
    del dproj_in
    c = pl.program_id(0)
    nc = pl.num_programs(0)

    @pl.when(c == 0)
    def _():
        gcar_re[...] = jnp.zeros_like(gcar_re)
        gcar_im[...] = jnp.zeros_like(gcar_im)
        acc_re[...] = jnp.zeros_like(acc_re)
        acc_im[...] = jnp.zeros_like(acc_im)
        gb_ref[...] = jnp.zeros_like(gb_ref)
        gc_ref[...] = jnp.zeros_like(gc_ref)
        gd_ref[...] = jnp.zeros_like(gd_ref)

    car_re[...] = cre_in[...]
    car_im[...] = cim_in[...]
    u = u_ref[...]
    dy = dgin_ref[...] * _gelu_and_grad(y_ref[...])[1]
    gd_ref[...] += jnp.sum(dy * u, axis=0, keepdims=True)
    up = jnp.dot(perm_ref[...], u.astype(BF16), preferred_element_type=F32).astype(BF16)
    dyp = jnp.dot(perm_ref[...], dy.astype(BF16), preferred_element_type=F32).astype(BF16)
    for k in range(N_SLABS):
        ch = slice(k * SLAB_CH, (k + 1) * SLAB_CH)
        bu = jnp.dot(up[:, ch], bbd_ref[k], preferred_element_type=F32)
        _slab_put(s_re, k, bu[:, :SLAB_LANES])
        _slab_put(s_im, k, bu[:, SLAB_LANES:])
        ds = lax.dot_general(dyp[:, ch], cbd_ref[k], _DIMS["nt"], preferred_element_type=F32)
        _slab_put(g_re, k, ds[:, :SLAB_LANES])
        _slab_put(g_im, k, ds[:, SLAB_LANES:])
    _scan_forward(s_re, s_im, pre_ref, pim_ref, car_re, car_im, sp_re, sp_im)
    _scan_backward(g_re, g_im, s_re, s_im, sp_re, sp_im, pre_ref, pim_ref, gcar_re, gcar_im, acc_re, acc_im)
    for k in range(N_SLABS):
        ch = slice(k * SLAB_CH, (k + 1) * SLAB_CH)
        uk, dyk = up[:, ch], dyp[:, ch]
        sr, si = _slab_get(s_re, k).astype(BF16), _slab_get(s_im, k).astype(BF16)
        gr, gi = _slab_get(g_re, k).astype(BF16), _slab_get(g_im, k).astype(BF16)
        gc_ref[k, :SLAB_LANES, :] += lax.dot_general(sr, dyk, _DIMS["tn"], preferred_element_type=F32)
        gc_ref[k, SLAB_LANES:, :] += lax.dot_general(si, dyk, _DIMS["tn"], preferred_element_type=F32)
        gb_ref[k, :, :SLAB_LANES] += lax.dot_general(uk, gr, _DIMS["tn"], preferred_element_type=F32)
        gb_ref[k, :, SLAB_LANES:] += lax.dot_general(uk, gi, _DIMS["tn"], preferred_element_type=F32)
        dup[:, pl.ds(k * SLAB_CH, SLAB_CH)] = (
            lax.dot_general(gr, bbd_ref[k, :, :SLAB_LANES], _DIMS["nt"], preferred_element_type=F32)
            + lax.dot_general(gi, bbd_ref[k, :, SLAB_LANES:], _DIMS["nt"], preferred_element_type=F32))
    du = _permute_exact(unperm_ref[...], dup[...], 2) + d_ref[...] * dy
    du_ref[...] = du.astype(du_ref.dtype)

    @pl.when(c == nc - 1)
    def _():
        glr_ref[...] = jnp.sum(acc_re[...], axis=0, keepdims=True)
        gli_ref[...] = jnp.sum(acc_im[...], axis=0, keepdims=True)


def _ssm_backward(proj, dg_in, y_ssm, car_re, car_im, p_re, p_im, b_bd, c_bd, d_skip, dproj):
    t = proj.shape[0]
    nc = t // SSM_CHUNK
    rev = lambda c: nc - 1 - c
    big = _scan_buffer
    small = lambda: pltpu.VMEM((8, N_LANES), F32)
    perm = _segment_permutation()
    outs = pl.pallas_call(
        functools.partial(_ssm_bwd_body), name="ssm_bwd", grid=(nc,),
        in_specs=[pl.BlockSpec((SSM_CHUNK, D_SSM), lambda c: (rev(c), C_U // D_SSM)),
                  pl.BlockSpec((SSM_CHUNK, D_SSM), lambda c: (rev(c), 0)),
                  pl.BlockSpec((SSM_CHUNK, D_SSM), lambda c: (rev(c), 0)),
                  pl.BlockSpec((None, 8, N_LANES), lambda c: (rev(c), 0, 0)),
                  pl.BlockSpec((None, 8, N_LANES), lambda c: (rev(c), 0, 0)),
                  _whole(b_bd.shape), _whole(c_bd.shape), _whole(p_re.shape), _whole(p_im.shape),
                  _whole((1, D_SSM)), _whole(perm.shape), _whole(perm.shape), pl.BlockSpec(memory_space=pl.ANY)],
        out_specs=[pl.BlockSpec((SSM_CHUNK, D_SSM), lambda c: (rev(c), C_U // D_SSM)),
                   _whole(b_bd.shape), _whole(c_bd.shape), _whole((1, N_LANES)), _whole((1, N_LANES)),
                   _whole((1, D_SSM))],
        out_shape=[jax.ShapeDtypeStruct(dproj.shape, dproj.dtype),
                   jax.ShapeDtypeStruct(b_bd.shape, F32), jax.ShapeDtypeStruct(c_bd.shape, F32),
                   jax.ShapeDtypeStruct((1, N_LANES), F32), jax.ShapeDtypeStruct((1, N_LANES), F32),
                   jax.ShapeDtypeStruct((1, D_SSM), F32)],
        scratch_shapes=[big(), big(), big(), big()] + [small() for _ in range(8)]
        + [pltpu.VMEM((SSM_CHUNK, D_SSM), F32)],
        input_output_aliases={12: 0},
        compiler_params=_params(("arbitrary",)),
    )(proj, dg_in, y_ssm, car_re, car_im, b_bd, c_bd, p_re, p_im, d_skip, jnp.asarray(perm, BF16),
      jnp.asarray(perm.T, BF16), dproj)
    return outs


def _bucket_table():
    i = np.arange(BLOCK)[:, None]
    j = np.arange(2 * BLOCK)[None, :]
    dist = BLOCK + i - j
    ok = (dist >= 0) & (dist < WINDOW)
    max_exact = N_BUCKETS // 2
    d = np.maximum(dist, 1).astype(np.float32)
    large = max_exact + (np.log(d / max_exact) / math.log(MAX_DISTANCE / max_exact)
                         * (N_BUCKETS - max_exact)).astype(np.int32)
    large = np.minimum(large, N_BUCKETS - 1)
    bucket = np.where(dist < max_exact, dist, large)
    return np.where(ok, bucket, -1).astype(np.int32)


def _bias_build_body(table_ref, bucket_ref, o_ref):
    h = pl.program_id(0)
    bucket = bucket_ref[...]
    acc = jnp.full(bucket.shape, NEG_INF, F32)
    for b in range(N_BUCKETS):
        acc = jnp.where(bucket == b, table_ref[b, h], acc)
    o_ref[...] = acc


def _bias_build(rel_bias_table, bucket):
    return pl.pallas_call(
        functools.partial(_bias_build_body), name="bias_build", grid=(N_Q_HEADS,),
        in_specs=[pl.BlockSpec(memory_space=pltpu.SMEM), _whole(bucket.shape)],
        out_specs=pl.BlockSpec((None, BLOCK, 2 * BLOCK), lambda h: (h, 0, 0)),
        out_shape=jax.ShapeDtypeStruct((N_Q_HEADS, BLOCK, 2 * BLOCK), F32),
        compiler_params=_params(("arbitrary",)),
    )(rel_bias_table, bucket)


def _bias_grad_body(g_ref, bucket_ref, o_ref):
    bucket = bucket_ref[...]
    lane = lax.broadcasted_iota(jnp.int32, (N_BUCKETS, 128), 1)

    def head(h, out):
        g = g_ref[h]
        rows = [jnp.sum(jnp.where(bucket == b, g, 0.0), axis=0, keepdims=True) for b in range(N_BUCKETS)]
        colsum = jnp.sum(jnp.concatenate(rows, axis=0), axis=1, keepdims=True)
        return jnp.where(lane == h, colsum, out)

    o_ref[...] = lax.fori_loop(0, N_Q_HEADS, head, jnp.zeros((N_BUCKETS, 128), F32))


def _bias_grad(g_bias, bucket):
    out = pl.pallas_call(
        functools.partial(_bias_grad_body), name="bias_grad", grid=(1,),
        in_specs=[_whole(g_bias.shape), _whole(bucket.shape)],
        out_specs=_whole((N_BUCKETS, 128)),
        out_shape=jax.ShapeDtypeStruct((N_BUCKETS, 128), F32),
        compiler_params=_params(("arbitrary",)),
    )(g_bias, bucket)
    return out[:, :N_Q_HEADS]


def _head_logits(qh, kk, bias_h, first_block):
    s = lax.dot_general(qh, kk, _DIMS["nt"], preferred_element_type=F32) * ATTN_SCALE + bias_h
    col = lax.broadcasted_iota(jnp.int32, s.shape, 1)
    return jnp.where(jnp.logical_and(first_block, col < BLOCK), NEG_INF, s)


def _attn_fwd_body(sink_ref, q_ref, kp_ref, kc_ref, vp_ref, vc_ref, bias_ref, o_ref, lse_ref):
    n = pl.program_id(0)
    outs, lses = [], []
    for kv in range(N_KV_HEADS):
        cs = slice(kv * HEAD_DIM, (kv + 1) * HEAD_DIM)
        kk = jnp.concatenate([kp_ref[:, cs], kc_ref[:, cs]], axis=0).astype(BF16)
        vv = jnp.concatenate([vp_ref[:, cs], vc_ref[:, cs]], axis=0).astype(BF16)
        for g in range(Q_PER_KV):
            h = kv * Q_PER_KV + g
            qh = q_ref[:, h * HEAD_DIM:(h + 1) * HEAD_DIM].astype(BF16)
            s = _head_logits(qh, kk, bias_ref[h], n == 0)
            sink = sink_ref[0, h]
            m = jnp.maximum(jnp.max(s, axis=1, keepdims=True), sink)
            p = jnp.exp(s - m)
            den = jnp.sum(p, axis=1, keepdims=True) + jnp.exp(sink - m)
            p = p / den
            outs.append(jnp.dot(p.astype(BF16), vv, preferred_element_type=F32))
            lses.append(m + jnp.log(den))
    o_ref[...] = jnp.concatenate(outs, axis=1)
    lse_ref[...] = jnp.concatenate(lses, axis=1)


def _attn_specs(nb):
    prev = lambda n: jnp.maximum(jnp.minimum(n, nb - 1) - 1, 0)
    cur = lambda n: jnp.minimum(n, nb - 1)
    return [pl.BlockSpec((BLOCK, D_ATTN), lambda n: (cur(n), C_Q // D_ATTN)),
            pl.BlockSpec((BLOCK, D_KV), lambda n: (prev(n), C_K // D_KV)),
            pl.BlockSpec((BLOCK, D_KV), lambda n: (cur(n), C_K // D_KV)),
            pl.BlockSpec((BLOCK, D_KV), lambda n: (prev(n), C_V // D_KV)),
            pl.BlockSpec((BLOCK, D_KV), lambda n: (cur(n), C_V // D_KV))]


def _attn_forward(proj, sinks, bias):
    t = proj.shape[0]
    nb = t // BLOCK
    return pl.pallas_call(
        functools.partial(_attn_fwd_body), name="attn_fwd", grid=(nb,),
        in_specs=[pl.BlockSpec(memory_space=pltpu.SMEM)] + _attn_specs(nb) + [_whole(bias.shape)],
        out_specs=[pl.BlockSpec((BLOCK, D_ATTN), lambda n: (n, 0)),
                   pl.BlockSpec((BLOCK, N_Q_HEADS), lambda n: (n, 0))],
        out_shape=[jax.ShapeDtypeStruct((t, D_ATTN), F32), jax.ShapeDtypeStruct((t, N_Q_HEADS), F32)],
        compiler_params=_params(("parallel",)),
    )(sinks, proj, proj, proj, proj, proj, bias)


def _attn_bwd_body(sink_ref, q_ref, kp_ref, kc_ref, vp_ref, vc_ref, bias_ref, do_ref, o_ref, lse_ref, dproj_in,
                   dq_ref, dkv_ref, gbias_ref, gsink_ref, carry_ref, *, nb):
    del dproj_in
    n = pl.program_id(0)

    @pl.when(n == 0)
    def _():
        gbias_ref[...] = jnp.zeros_like(gbias_ref)
        gsink_ref[...] = jnp.zeros_like(gsink_ref)
        carry_ref[...] = jnp.zeros_like(carry_ref)

    @pl.when(n < nb)
    def _():
        lane = lax.broadcasted_iota(jnp.int32, (1, 128), 1)
        dqs, dks, dvs = [], [], []
        gsink = jnp.zeros((1, 128), F32)
        for kv in range(N_KV_HEADS):
            cs = slice(kv * HEAD_DIM, (kv + 1) * HEAD_DIM)
            kk = jnp.concatenate([kp_ref[:, cs], kc_ref[:, cs]], axis=0).astype(BF16)
            vv = jnp.concatenate([vp_ref[:, cs], vc_ref[:, cs]], axis=0).astype(BF16)
            dk = jnp.zeros((2 * BLOCK, HEAD_DIM), F32)
            dv = jnp.zeros((2 * BLOCK, HEAD_DIM), F32)
            for g in range(Q_PER_KV):
                h = kv * Q_PER_KV + g
                hs = slice(h * HEAD_DIM, (h + 1) * HEAD_DIM)
                qh = q_ref[:, hs].astype(BF16)
                s = _head_logits(qh, kk, bias_ref[h], n == 0)
                lse = lse_ref[:, h:h + 1]
                p = jnp.exp(s - lse)
                do = do_ref[:, hs]
                delta = jnp.sum(do * o_ref[:, hs], axis=1, keepdims=True)
                dob = do.astype(BF16)
                dp = lax.dot_general(dob, vv, _DIMS["nt"], preferred_element_type=F32)
                dl = p * (dp - delta)
                gbias_ref[h] += dl
                psink = jnp.exp(sink_ref[0, h] - lse)
                gsink = gsink + jnp.where(lane == h, -jnp.sum(psink * delta), 0.0)
                dlb = dl.astype(BF16)
                dqs.append(jnp.dot(dlb, kk, preferred_element_type=F32) * ATTN_SCALE)
                dk = dk + lax.dot_general(dlb, qh, _DIMS["tn"], preferred_element_type=F32) * ATTN_SCALE
                dv = dv + lax.dot_general(p.astype(BF16), dob, _DIMS["tn"], preferred_element_type=F32)
            dks.append(dk)
            dvs.append(dv)
        dq_ref[...] = jnp.concatenate(dqs, axis=1).astype(dq_ref.dtype)
        gsink_ref[...] += gsink
        dkv = jnp.concatenate(dks + dvs, axis=1)
        dkv_ref[...] = (carry_ref[...] + dkv[:BLOCK]).astype(dkv_ref.dtype)
        carry_ref[...] = dkv[BLOCK:]

    @pl.when(n == nb)
    def _():
        dkv_ref[...] = carry_ref[...].astype(dkv_ref.dtype)


def _attn_backward(proj, sinks, bias, d_attn, attn, lse, dproj):
    t = proj.shape[0]
    nb = t // BLOCK
    cur = lambda n: jnp.minimum(n, nb - 1)
    return pl.pallas_call(
        functools.partial(_attn_bwd_body, nb=nb), name="attn_bwd", grid=(nb + 1,),
        in_specs=[pl.BlockSpec(memory_space=pltpu.SMEM)] + _attn_specs(nb) + [
            _whole(bias.shape),
            pl.BlockSpec((BLOCK, D_ATTN), lambda n: (cur(n), 0)),
            pl.BlockSpec((BLOCK, D_ATTN), lambda n: (cur(n), 0)),
            pl.BlockSpec((BLOCK, N_Q_HEADS), lambda n: (cur(n), 0)),
            pl.BlockSpec(memory_space=pl.ANY)],
        out_specs=[pl.BlockSpec((BLOCK, D_ATTN), lambda n: (cur(n), C_Q // D_ATTN)),
                   pl.BlockSpec((BLOCK, 2 * D_KV), lambda n: (jnp.maximum(n - 1, 0), 0)),
                   _whole(bias.shape), _whole((1, 128))],
        out_shape=[jax.ShapeDtypeStruct(dproj.shape, dproj.dtype), jax.ShapeDtypeStruct((t, 2 * D_KV), dproj.dtype),
                   jax.ShapeDtypeStruct(bias.shape, F32), jax.ShapeDtypeStruct((1, 128), F32)],
        scratch_shapes=[pltpu.VMEM((BLOCK, 2 * D_KV), F32)],
        input_output_aliases={10: 0},
        compiler_params=_params(("arbitrary",)),
    )(sinks, proj, proj, proj, proj, proj, bias, d_attn, attn, lse, dproj)


ROWS = 256


def _rowwise(body, name, t, ins, outs, aliases=None):
    rows = min(ROWS, t)

    def col_spec(w, c0):
        if c0 % w == 0:
            return pl.BlockSpec((rows, w), lambda i: (i, c0 // w))
        return pl.BlockSpec((pl.Element(rows), pl.Element(w)), lambda i: (i * rows, c0))

    in_specs, args = [], []
    for a, w, c0 in ins:
        args.append(a)
        if w is None:
            in_specs.append(pl.BlockSpec(memory_space=pl.ANY) if c0 == "any" else _whole(a.shape))
        else:
            in_specs.append(col_spec(w, c0))
    out_specs, out_shape = [], []
    for shape, dtype, w, c0 in outs:
        out_shape.append(jax.ShapeDtypeStruct(shape, dtype))
        out_specs.append(_whole(shape) if w is None else col_spec(w, c0))
    accum = any(o[2] is None for o in outs)
    return pl.pallas_call(
        body, name=name, grid=(t // rows,), in_specs=in_specs, out_specs=out_specs, out_shape=out_shape,
        input_output_aliases=aliases or {},
        compiler_params=_params(("arbitrary",) if accum else ("parallel",)),
    )(*args)


def _ssm_gate_fwd_body(glu_ref, z_ref, h_ref):
    a, b = glu_ref[:, :D_SSM], glu_ref[:, D_SSM:]
    h_ref[...] = ((a * _sigmoid(b)) * _silu_and_grad(z_ref[...])[0]).astype(h_ref.dtype)


def _ssm_gate_bwd_body(dh_ref, glu_ref, z_ref, dproj_in, dglu_ref, dz_ref):
    del dproj_in
    a, b = glu_ref[:, :D_SSM], glu_ref[:, D_SSM:]
    sb = _sigmoid(b)
    silu, dsilu = _silu_and_grad(z_ref[...])
    dh = dh_ref[...]
    dg = dh * silu
    dz_ref[...] = (dh * (a * sb) * dsilu).astype(dz_ref.dtype)
    dglu_ref[:, :D_SSM] = (dg * sb).astype(dglu_ref.dtype)
    dglu_ref[:, D_SSM:] = (dg * a * sb * (1.0 - sb)).astype(dglu_ref.dtype)


def _attn_gate_fwd_body(attn_ref, z_ref, h_ref):
    h_ref[...] = (attn_ref[...] * _silu_and_grad(z_ref[...])[0]).astype(h_ref.dtype)


def _attn_gate_bwd_body(dh_ref, attn_ref, z_ref, dproj_in, dattn_ref, dz_ref):
    del dproj_in
    silu, dsilu = _silu_and_grad(z_ref[...])
    dh = dh_ref[...]
    dattn_ref[...] = dh * silu
    dz_ref[...] = (dh * attn_ref[...] * dsilu).astype(dz_ref.dtype)


def _merge_fwd_body(bs_ref, ba_ref, gl_ref, m_ref):
    gs = _sigmoid(gl_ref[:, :D_MODEL])
    ga = _sigmoid(gl_ref[:, D_MODEL:])
    m_ref[...] = (gs * bs_ref[...] + ga * ba_ref[...]).astype(m_ref.dtype)


def _merge_bwd_body(dm_ref, bs_ref, ba_ref, gl_ref, dbs_ref, dba_ref, dgl_ref):
    gs = _sigmoid(gl_ref[:, :D_MODEL])
    ga = _sigmoid(gl_ref[:, D_MODEL:])
    dm = dm_ref[...]
    dbs_ref[...] = (dm * gs).astype(dbs_ref.dtype)
    dba_ref[...] = (dm * ga).astype(dba_ref.dtype)
    dgl_ref[:, :D_MODEL] = (dm * bs_ref[...] * gs * (1.0 - gs)).astype(dgl_ref.dtype)
    dgl_ref[:, D_MODEL:] = (dm * ba_ref[...] * ga * (1.0 - ga)).astype(dgl_ref.dtype)


def _ln_loss_body(x_ref, o_ref, tgt_ref, gain_ref, bias_ref, dr_ref, drb_ref, loss_ref, dgain_ref, dbias_ref):
    @pl.when(pl.program_id(0) == 0)
    def _():
        loss_ref[...] = jnp.zeros_like(loss_ref)
        dgain_ref[...] = jnp.zeros_like(dgain_ref)
        dbias_ref[...] = jnp.zeros_like(dbias_ref)

    r = ALPHA * x_ref[...] + o_ref[...]
    mu = jnp.mean(r, axis=1, keepdims=True)
    rc = r - mu
    var = jnp.mean(rc * rc, axis=1, keepdims=True)
    rstd = lax.rsqrt(var + LN_EPS)
    xhat = rc * rstd
    gain = gain_ref[...]
    err = xhat * gain + bias_ref[...] - tgt_ref[...]
    loss_ref[...] += 0.5 * jnp.sum(jnp.mean(err * err, axis=1, keepdims=True), axis=0, keepdims=True)
    dy = err * (1.0 / D_MODEL)
    dgain_ref[...] += jnp.sum(dy * xhat, axis=0, keepdims=True)
    dbias_ref[...] += jnp.sum(dy, axis=0, keepdims=True)
    dxhat = dy * gain
    m1 = jnp.mean(dxhat, axis=1, keepdims=True)
    m2 = jnp.mean(dxhat * xhat, axis=1, keepdims=True)
    dr = rstd * (dxhat - m1 - xhat * m2)
    dr_ref[...] = dr
    drb_ref[...] = dr.astype(drb_ref.dtype)


def _place_body(piece_ref, dproj_in, o_ref):
    del dproj_in
    o_ref[...] = piece_ref[...]


def _adamw_update(w_ref, m_ref, v_ref, g, g_ref, d_ref, nm_ref, nv_ref):
    m = ADAM_B1 * m_ref[...] + (1.0 - ADAM_B1) * g
    v = ADAM_B2 * v_ref[...] + (1.0 - ADAM_B2) * (g * g)
    m_hat = m / (1.0 - ADAM_B1 ** ADAM_STEP)
    v_hat = v / (1.0 - ADAM_B2 ** ADAM_STEP)
    g_ref[...] = g
    d_ref[...] = -ADAM_LR * (m_hat / (jnp.sqrt(v_hat) + ADAM_EPS) + ADAM_WD * w_ref[...])
    nm_ref[...] = m
    nv_ref[...] = v


def _adamw_body(*refs, n_parts):
    w_ref, m_ref, v_ref = refs[:3]
    parts = refs[3:3 + n_parts]
    g = parts[0][...].astype(F32)
    for p in parts[1:]:
        g = g + p[...].astype(F32)
    _adamw_update(w_ref, m_ref, v_ref, g, *refs[3 + n_parts:])


def _adamw_shard_body(c_ref, w_ref, m_ref, v_ref, mine_ref, other_ref, g_ref, d_ref, nm_ref, nv_ref, *, nth):
    in_mine = pl.program_id(0) // nth == c_ref[0]
    g = jnp.where(in_mine, mine_ref[...], other_ref[...])
    _adamw_update(w_ref, m_ref, v_ref, g, g_ref, d_ref, nm_ref, nv_ref)


def _adamw_shard(w, m, v, halves, core, *, name, rows):
    shape = w.shape
    w2, m2, v2 = (a.reshape(-1, shape[-1]) for a in (w, m, v))
    r, c = w2.shape
    nth = r // 2 // rows
    assert 2 * nth * rows == r and halves[0].shape == (r // 2, c)
    spec = pl.BlockSpec((rows, c), lambda i, cr: (i, 0))
    half_spec = pl.BlockSpec((rows, c), lambda i, cr: (i % nth, 0))
    outs = pl.pallas_call(
        functools.partial(_adamw_shard_body, nth=nth), name=name,
        grid_spec=pltpu.PrefetchScalarGridSpec(
            num_scalar_prefetch=1, grid=(r // rows,),
            in_specs=[spec] * 3 + [half_spec] * 2, out_specs=[spec] * 4),
        out_shape=[jax.ShapeDtypeStruct((r, c), F32)] * 4,
        compiler_params=_params(("parallel",)),
    )(core, w2, m2, v2, *halves)
    return tuple(o.reshape(shape) for o in outs)


def _adamw(w, m, v, parts, *, name, rows=256):
    shape = w.shape
    w2, m2, v2 = (a.reshape(-1, shape[-1]) for a in (w, m, v))
    parts = [p.reshape(w2.shape) for p in parts]
    r, c = w2.shape
    rows = rows if r % rows == 0 else r
    spec = pl.BlockSpec((rows, c), lambda i: (i, 0))
    outs = pl.pallas_call(
        functools.partial(_adamw_body, n_parts=len(parts)), name=name, grid=(r // rows,),
        in_specs=[spec] * (3 + len(parts)), out_specs=[spec] * 4,
        out_shape=[jax.ShapeDtypeStruct((r, c), F32)] * 4,
        compiler_params=_params(("parallel",)),
    )(w2, m2, v2, *parts)
    return tuple(o.reshape(shape) for o in outs)


BIG = ("w_in", "w_glu", "w_branch_ssm", "w_branch_attn", "w_out")
SHARD_AXIS = dict(w_in=1, w_glu=1, w_branch_ssm=1, w_branch_attn=1, w_out=0)
HBM = pl.BlockSpec(memory_space=pl.ANY)


def _position():
    x, y, c = lax.axis_index("x"), lax.axis_index("y"), lax.axis_index("c")
    other_chips = [(1 - x, y), (x, 1 - y), (1 - x, 1 - y)]
    return x, y, c, other_chips


def _window(ref, axis, shard, n_shards, half=None):
    rows, cols = ref.shape[-2:]
    sel = [slice(None), slice(None)]
    size = ref.shape[-2 + axis] // n_shards
    sel[axis] = pl.ds(pl.multiple_of(shard * size, 128), size)
    if half is not None:
        hsize = ref.shape[-1 - axis] // 2
        sel[1 - axis] = pl.ds(pl.multiple_of(half * hsize, 128), hsize)
    return ref.at[tuple(sel)]


def _half(ref, axis, half):
    hsize = ref.shape[-1 - axis] // 2
    sel = [slice(None), slice(None)]
    sel[1 - axis] = pl.ds(pl.multiple_of(half * hsize, 128), hsize)
    return ref.at[tuple(sel)]


def _remote(src, dst, send_sem, recv_sem, device):
    return pltpu.make_async_remote_copy(src_ref=src, dst_ref=dst, send_sem=send_sem, recv_sem=recv_sem,
                                        device_id=device, device_id_type=MESH)


def _gather_body(*refs, axes):
    n = len(axes)
    shards, outs = refs[:n], refs[n:2 * n]
    send_sems, recv_sems, fsend_sems, frecv_sems, local_sems = refs[2 * n:]
    x, y, c, chips = _position()
    me = 2 * x + y
    local, sends, passes = [], [], []
    for w, ax in enumerate(axes):
        cp = pltpu.make_async_copy(shards[w], _window(outs[w], ax, me, N_CHIPS), local_sems.at[w])
        cp.start()
        local.append(cp)
    for w, ax in enumerate(axes):
        for r, (px, py) in enumerate(chips):
            cp = _remote(_half(shards[w], ax, c), _window(outs[w], ax, me, N_CHIPS, c),
                         send_sems.at[3 * w + r], recv_sems.at[3 * w + r], (px, py, c))
            cp.start()
            sends.append(cp)
    for w, ax in enumerate(axes):
        for r, (px, py) in enumerate(chips):
            landed = _window(outs[w], ax, 2 * px + py, N_CHIPS, c)
            _remote(landed, landed, send_sems.at[3 * w + r], recv_sems.at[3 * w + r], (px, py, c)).wait_recv()
            cp = _remote(landed, landed, fsend_sems.at[3 * w + r], frecv_sems.at[3 * w + r], (x, y, 1 - c))
            cp.start()
            passes.append(cp)
    for w, ax in enumerate(axes):
        for r, (px, py) in enumerate(chips):
            other = _window(outs[w], ax, 2 * px + py, N_CHIPS, 1 - c)
            _remote(other, other, fsend_sems.at[3 * w + r], frecv_sems.at[3 * w + r], (x, y, 1 - c)).wait_recv()
    for cp in sends + passes:
        cp.wait_send()
    for cp in local:
        cp.wait()


def _all_gather(shards):
    axes = tuple(SHARD_AXIS[k] for k in BIG)
    out_shape = []
    for k in BIG:
        s = list(shards[k].shape)
        s[SHARD_AXIS[k]] *= N_CHIPS
        out_shape.append(jax.ShapeDtypeStruct(tuple(s), shards[k].dtype))
    n = len(BIG)
    outs = pl.pallas_call(
        functools.partial(_gather_body, axes=axes), name="gather_weights",
        in_specs=[HBM] * n, out_specs=[HBM] * n, out_shape=out_shape,
        scratch_shapes=[pltpu.SemaphoreType.DMA((3 * n,))] * 4 + [pltpu.SemaphoreType.DMA((n,))],
    )(*[shards[k] for k in BIG])
    return dict(zip(BIG, outs))


def _half_tile(n, h, nt, axis):
    return h * nt + n if axis == 1 else 2 * n + h


def _sibling_stream(n, nt, stage, recv, send_sems, recv_sems, credit, produce, consume):
    x, y, c, _ = _position()
    sibling = (x, y, 1 - c)

    def copy(slot):
        return _remote(stage.at[slot], recv.at[slot], send_sems.at[slot], recv_sems.at[slot], sibling)

    @pl.when(n < nt)
    def _():
        slot = n % 2

        @pl.when(n >= 2)
        def _():
            copy(slot).wait_send()
            pl.semaphore_wait(credit, 1)

        stage[slot] = produce().astype(stage.dtype)
        copy(slot).start()

    @pl.when(n >= 1)
    def _():
        slot = (n - 1) % 2
        copy(slot).wait_recv()
        consume(recv[slot])

        @pl.when(n - 1 < nt - 2)
        def _():
            pl.semaphore_signal(credit, 1, device_id=sibling, device_id_type=MESH)

    @pl.when(n == nt)
    def _():
        for slot in range(min(2, nt)):
            copy(slot).wait_send()


def _pair_reduce_body(c_ref, mine_ref, theirs_ref, out_ref, stage, recv, send_sems, recv_sems, credit, *, nt):
    del c_ref

    def consume(got):
        out_ref[...] = (mine_ref[...] + got.astype(F32)).astype(out_ref.dtype)

    _sibling_stream(pl.program_id(0), nt, stage, recv, send_sems, recv_sems, credit,
                    lambda: theirs_ref[...], consume)


def _pair_reduce(grad, core, axis, *, name, rows):
    r, c = grad.shape
    nt = r // 2 // rows
    assert nt * rows * 2 == r and (axis == 1 or rows == r // (2 * N_CHIPS))
    tile = lambda n, h: _half_tile(n, h, nt, axis)
    return pl.pallas_call(
        functools.partial(_pair_reduce_body, nt=nt), name=name,
        grid_spec=pltpu.PrefetchScalarGridSpec(
            num_scalar_prefetch=1, grid=(nt + 1,),
            in_specs=[pl.BlockSpec((rows, c), lambda n, cr: (tile(jnp.maximum(n - 1, 0), cr[0]), 0)),
                      pl.BlockSpec((rows, c), lambda n, cr: (tile(jnp.minimum(n, nt - 1), 1 - cr[0]), 0))],
            out_specs=pl.BlockSpec((rows, c), lambda n, cr: (jnp.maximum(n - 1, 0), 0)),
            scratch_shapes=[pltpu.VMEM((2, rows, c), BF16), pltpu.VMEM((2, rows, c), BF16),
                            pltpu.SemaphoreType.DMA((2,)), pltpu.SemaphoreType.DMA((2,)),
                            pltpu.SemaphoreType.REGULAR]),
        out_shape=jax.ShapeDtypeStruct((r // 2, c), BF16),
        compiler_params=_params(("arbitrary",)),
    )(core, grad, grad)


def _chip_add_share_body(c_ref, s0, s1, s2, s3, mine_ref, other_ref, stage, recv, send_sems, recv_sems, credit, *, nt):
    del c_ref

    def produce():
        total = s3[...].astype(F32) + s0[...].astype(F32) + s1[...].astype(F32) + s2[...].astype(F32)
        mine_ref[...] = total
        return total

    def consume(got):
        other_ref[...] = got

    _sibling_stream(pl.program_id(0), nt, stage, recv, send_sems, recv_sems, credit, produce, consume)


def _chip_add_share(slots, core, *, name, rows):
    _, r, c = slots.shape
    nt = r // rows
    assert nt * rows == r
    send = lambda j: pl.BlockSpec((None, rows, c), lambda n, cr: (j, jnp.minimum(n, nt - 1), 0))
    return pl.pallas_call(
        functools.partial(_chip_add_share_body, nt=nt), name=name,
        grid_spec=pltpu.PrefetchScalarGridSpec(
            num_scalar_prefetch=1, grid=(nt + 1,),
            in_specs=[send(j) for j in range(4)],
            out_specs=[pl.BlockSpec((rows, c), lambda n, cr: (jnp.minimum(n, nt - 1), 0)),
                       pl.BlockSpec((rows, c), lambda n, cr: (jnp.maximum(n - 1, 0), 0))],
            scratch_shapes=[pltpu.VMEM((2, rows, c), F32), pltpu.VMEM((2, rows, c), F32),
                            pltpu.SemaphoreType.DMA((2,)), pltpu.SemaphoreType.DMA((2,)),
                            pltpu.SemaphoreType.REGULAR]),
        out_shape=[jax.ShapeDtypeStruct((r, c), F32)] * 2,
        compiler_params=_params(("arbitrary",)),
    )(core, slots, slots, slots, slots)


def _chip_exchange_body(*refs, axes):
    n = len(axes)
    sums, slots = refs[:n], refs[n:2 * n]
    send_sems, recv_sems, local_sems = refs[2 * n:]
    x, y, c, chips = _position()
    me = 2 * x + y
    copies, local = [], []
    for w, ax in enumerate(axes):
        for r, (px, py) in enumerate(chips):
            cp = _remote(_window(sums[w], ax, 2 * px + py, N_CHIPS), slots[w].at[r],
                         send_sems.at[3 * w + r], recv_sems.at[3 * w + r], (px, py, c))
            cp.start()
            copies.append(cp)
        lc = pltpu.make_async_copy(_window(sums[w], ax, me, N_CHIPS), slots[w].at[3], local_sems.at[w])
        lc.start()
        local.append(lc)
    for cp in copies:
        cp.wait_recv()
    for cp in copies:
        cp.wait_send()
    for lc in local:
        lc.wait()


def _comm_call(body, name, ins, out_shape, n_sem_arrays, n_sems):
    axes = tuple(SHARD_AXIS[k] for k in BIG)
    return pl.pallas_call(
        functools.partial(body, axes=axes), name=name,
        in_specs=[HBM] * len(ins), out_specs=[HBM] * len(out_shape), out_shape=out_shape,
        scratch_shapes=[pltpu.SemaphoreType.DMA((n_sems,))] * n_sem_arrays + [pltpu.SemaphoreType.DMA((len(BIG),))],
    )(*ins)


PAIR_ROWS = dict(w_in=64, w_glu=128, w_branch_ssm=128, w_branch_attn=128, w_out=256)
SHARE_ROWS = dict(w_in=128, w_glu=128, w_branch_ssm=128, w_branch_attn=128, w_out=64)


def _reduce_grads(grads):
    n = len(BIG)
    core = lax.axis_index("c").astype(jnp.int32).reshape(1)
    pair = [_pair_reduce(grads[k], core, SHARD_AXIS[k], name="pair_reduce_" + k, rows=PAIR_ROWS[k]) for k in BIG]
    slot_shapes = []
    for k, p in zip(BIG, pair):
        s = list(p.shape)
        s[SHARD_AXIS[k]] //= N_CHIPS
        slot_shapes.append(jax.ShapeDtypeStruct((4, *s), BF16))
    slots = _comm_call(_chip_exchange_body, "chip_exchange", pair, slot_shapes, 2, 3 * n)
    return {k: _chip_add_share(slots[w], core, name="chip_add_share_" + k, rows=SHARE_ROWS[k])
            for w, k in enumerate(BIG)}


SMALL = (("ssm_lambda_re", (1, 64, 64)), ("ssm_lambda_im", (1, 64, 64)), ("ssm_b_re", (1, 64, 64, 16)),
         ("ssm_b_im", (1, 64, 64, 16)), ("ssm_c_re", (1, 64, 16, 64)), ("ssm_c_im", (1, 64, 16, 64)),
         ("ssm_d", (1, 1024)), ("ssm_log_step", (1, 64)), ("attn_sinks", (1, 16)), ("rel_bias_table", (32, 16)),
         ("ln_gain", (1, 2048)), ("ln_bias", (1, 2048)))
SMALL_SIZE = sum(int(np.prod(s)) for _, s in SMALL)
PACK_ROWS = -(-(SMALL_SIZE + 1) // (8 * 128)) * 8


def _pack(values, extra=None):
    flat = [values[k].reshape(-1).astype(F32) for k, _ in SMALL]
    flat.append(jnp.zeros((1,), F32) if extra is None else extra.reshape(1))
    flat.append(jnp.zeros((PACK_ROWS * 128 - SMALL_SIZE - 1,), F32))
    return jnp.concatenate(flat).reshape(PACK_ROWS, 128)


def _unpack(packed):
    flat = packed.reshape(-1)
    out, off = {}, 0
    for k, s in SMALL:
        size = int(np.prod(s))
        out[k] = flat[off:off + size].reshape(s)
        off += size
    return out, flat[off]


def _small_allreduce_body(p_ref, o_ref, buf, send_sems, recv_sems):
    x, y, c, _ = _position()
    me = 4 * x + 2 * y + c
    buf[0] = p_ref[...]
    copies = []
    for r in range(1, 8):
        peer = tuple(1 - v if (r >> s) & 1 else v for v, s in ((x, 2), (y, 1), (c, 0)))
        cp = _remote(p_ref, buf.at[r], send_sems.at[r - 1], recv_sems.at[r - 1], peer)
        cp.start()
        copies.append(cp)
    for cp in copies:
        cp.wait_recv()
    for cp in copies:
        cp.wait_send()
    acc = buf[me]
    for s in range(1, 8):
        acc = acc + buf[jnp.bitwise_xor(me, s)]
    o_ref[...] = acc


def _small_allreduce(packed):
    vm = pl.BlockSpec(memory_space=pltpu.VMEM)
    return pl.pallas_call(
        functools.partial(_small_allreduce_body), name="small_allreduce",
        in_specs=[vm], out_specs=vm, out_shape=jax.ShapeDtypeStruct(packed.shape, F32),
        scratch_shapes=[pltpu.VMEM((8,) + packed.shape, F32), pltpu.SemaphoreType.DMA((7,)),
                        pltpu.SemaphoreType.DMA((7,))],
        compiler_params=pltpu.CompilerParams(vmem_limit_bytes=VMEM_LIMIT),
    )(packed)


def _local_step(x, target, w_in, w_glu, w_bs, w_ba, w_out, lam_re, lam_im, b_re, b_im, c_re, c_im, d_skip,
                log_step, sinks, rel_bias_table, ln_gain, ln_bias):
    t = x.shape[0]
    bucket = jnp.asarray(_bucket_table())
    p_re, p_im, b_bd, c_bd = _ssm_prepare(lam_re, lam_im, b_re, b_im, c_re, c_im, log_step)
    bias = _bias_build(rel_bias_table, bucket)

    xb, xbt = _cast_and_transpose(x, name="cast_x")
    proj = _matmul(xb, w_in, "nn", name="proj", tm=2048, tn=512)
    y_ssm, g_in, car_re, car_im = _ssm_forward(proj, p_re, p_im, b_bd, c_bd, d_skip)
    glu = _matmul(g_in, w_glu, "nn", name="glu", tm=1024, tn=2048)
    (h_ssm,) = _rowwise(functools.partial(_ssm_gate_fwd_body), "ssm_gate_fwd", t,
                        [(glu, 2 * D_SSM, 0), (proj, D_SSM, C_ZS)], [((t, D_SSM), BF16, D_SSM, 0)])
    attn, lse = _attn_forward(proj, sinks, bias)
    (h_attn,) = _rowwise(functools.partial(_attn_gate_fwd_body), "attn_gate_fwd", t,
                         [(attn, D_ATTN, 0), (proj, D_ATTN, C_ZA)], [((t, D_ATTN), BF16, D_ATTN, 0)])
    bs = _matmul(h_ssm, w_bs, "nn", name="branch_ssm", tm=1024, tn=2048)
    ba = _matmul(h_attn, w_ba, "nn", name="branch_attn", tm=1024, tn=2048)
    gl_in = (proj, 2 * D_MODEL, C_GL)
    (merged,) = _rowwise(functools.partial(_merge_fwd_body), "merge_fwd", t,
                         [(bs, D_MODEL, 0), (ba, D_MODEL, 0), gl_in], [((t, D_MODEL), BF16, D_MODEL, 0)])
    out = _matmul(merged, w_out, "nn", name="out_proj", tm=1024, tn=1024)
    dr, drb, loss, g_gain, g_bias = _rowwise(
        functools.partial(_ln_loss_body), "ln_loss", t,
        [(x, D_MODEL, 0), (out, D_MODEL, 0), (target, D_MODEL, 0), (ln_gain, None, 0), (ln_bias, None, 0)],
        [((t, D_MODEL), F32, D_MODEL, 0), ((t, D_MODEL), BF16, D_MODEL, 0), ((1, 1), F32, None, 0),
         ((1, D_MODEL), F32, None, 0), ((1, D_MODEL), F32, None, 0)])

    g_w_out = _matmul(merged, drb, "tn", name="g_w_out", tm=1024, tn=512, tk=1024)
    d_merged = _matmul(drb, w_out, "nt", name="d_merged", tm=1024, tn=1024)
    d_bs, d_ba, dproj = _rowwise(
        functools.partial(_merge_bwd_body), "merge_bwd", t,
        [(d_merged, D_MODEL, 0), (bs, D_MODEL, 0), (ba, D_MODEL, 0), gl_in],
        [((t, D_MODEL), BF16, D_MODEL, 0), ((t, D_MODEL), BF16, D_MODEL, 0),
         ((t, D_IN), BF16, 2 * D_MODEL, C_GL)])
    g_w_bs = _matmul(h_ssm, d_bs, "tn", name="g_w_branch_ssm", tm=1024, tn=512, tk=1024)
    d_h_ssm = _matmul(d_bs, w_bs, "nt", name="d_h_ssm", tm=1024, tn=1024)
    g_w_ba = _matmul(h_attn, d_ba, "tn", name="g_w_branch_attn", tm=1024, tn=512, tk=1024)
    d_h_attn = _matmul(d_ba, w_ba, "nt", name="d_h_attn", tm=1024, tn=1024)

    d_attn, dproj = _rowwise(
        functools.partial(_attn_gate_bwd_body), "attn_gate_bwd", t,
        [(d_h_attn, D_ATTN, 0), (attn, D_ATTN, 0), (proj, D_ATTN, C_ZA), (dproj, None, "any")],
        [((t, D_ATTN), F32, D_ATTN, 0), ((t, D_IN), BF16, D_ATTN, C_ZA)], aliases={3: 1})
    dproj, d_kv, g_bias_full, g_sinks = _attn_backward(proj, sinks, bias, d_attn, attn, lse, dproj)
    (dproj,) = _rowwise(functools.partial(_place_body), "place_dkv", t,
                        [(d_kv, 2 * D_KV, 0), (dproj, None, "any")],
                        [((t, D_IN), BF16, 2 * D_KV, C_K)], aliases={1: 0})
    g_table = _bias_grad(g_bias_full, bucket)

    d_glu, dproj = _rowwise(
        functools.partial(_ssm_gate_bwd_body), "ssm_gate_bwd", t,
        [(d_h_ssm, D_SSM, 0), (glu, 2 * D_SSM, 0), (proj, D_SSM, C_ZS), (dproj, None, "any")],
        [((t, 2 * D_SSM), BF16, 2 * D_SSM, 0), ((t, D_IN), BF16, D_SSM, C_ZS)], aliases={3: 1})
    g_w_glu = _matmul(g_in, d_glu, "tn", name="g_w_glu", tm=1024, tn=512, tk=1024)
    d_g_in = _matmul(d_glu, w_glu, "nt", name="d_g_in", tm=1024, tn=1024)
    dproj, g_bbd, g_cbd, g_lam_re, g_lam_im, g_d = _ssm_backward(
        proj, d_g_in, y_ssm, car_re, car_im, p_re, p_im, b_bd, c_bd, d_skip, dproj)
    g_lr, g_li, g_br, g_bi, g_cr, g_ci, g_ls = _ssm_param_grads(
        lam_re, lam_im, b_re, b_im, log_step, g_lam_re, g_lam_im, g_bbd, g_cbd)

    g_w_in = _matmul(xbt, dproj, "nn", name="g_w_in", tm=512, tn=512)
    grad_x = _matmul(dproj, w_in, "nt", name="grad_x", tm=512, tn=512, res=dr, res_scale=ALPHA)

    big = dict(w_in=g_w_in, w_glu=g_w_glu, w_branch_ssm=g_w_bs, w_branch_attn=g_w_ba, w_out=g_w_out)
    small = dict(ssm_lambda_re=g_lr, ssm_lambda_im=g_li, ssm_b_re=g_br, ssm_b_im=g_bi, ssm_c_re=g_cr,
                 ssm_c_im=g_ci, ssm_d=g_d, ssm_log_step=g_ls, attn_sinks=g_sinks[:, :N_Q_HEADS],
                 rel_bias_table=g_table, ln_gain=g_gain, ln_bias=g_bias)
    return loss, grad_x, big, small


WEIGHTS = ("w_in", "ssm_lambda_re", "ssm_lambda_im", "ssm_b_re", "ssm_b_im", "ssm_c_re", "ssm_c_im", "ssm_d",
           "ssm_log_step", "w_glu", "attn_sinks", "rel_bias_table", "w_branch_ssm", "w_branch_attn", "w_out",
           "ln_gain", "ln_bias")


def kernel(x, w_in, ssm_lambda_re, ssm_lambda_im, ssm_b_re, ssm_b_im, ssm_c_re, ssm_c_im, ssm_d, ssm_log_step, w_glu, attn_sinks, rel_bias_table, w_branch_ssm, w_branch_attn, w_out, ln_gain, ln_bias, loss_target, m_w_in, m_ssm_lambda_re, m_ssm_lambda_im, m_ssm_b_re, m_ssm_b_im, m_ssm_c_re, m_ssm_c_im, m_ssm_d, m_ssm_log_step, m_w_glu, m_attn_sinks, m_rel_bias_table, m_w_branch_ssm, m_w_branch_attn, m_w_out, m_ln_gain, m_ln_bias, v_w_in, v_ssm_lambda_re, v_ssm_lambda_im, v_ssm_b_re, v_ssm_b_im, v_ssm_c_re, v_ssm_c_im, v_ssm_d, v_ssm_log_step, v_w_glu, v_attn_sinks, v_rel_bias_table, v_w_branch_ssm, v_w_branch_attn, v_w_out, v_ln_gain, v_ln_bias):
    given = dict(locals())
    w = {k: given[k] for k in WEIGHTS}
    m = {k: given["m_" + k] for k in WEIGHTS}
    v = {k: given["v_" + k] for k in WEIGHTS}

    shards = {k: _cast(w[k][0], BF16, name="cast_" + k) for k in BIG}
    full = _all_gather(shards)
    loss, grad_x, g_big, g_small = _local_step(
        x[0], loss_target[0], full["w_in"], full["w_glu"], full["w_branch_ssm"], full["w_branch_attn"],
        full["w_out"], ssm_lambda_re[0], ssm_lambda_im[0], ssm_b_re[0], ssm_b_im[0], ssm_c_re[0], ssm_c_im[0],
        ssm_d, ssm_log_step, attn_sinks, rel_bias_table, ln_gain, ln_bias)

    g_shard = _reduce_grads(g_big)
    g_packed = _small_allreduce(_pack(g_small, loss))
    loss_sum = _unpack(g_packed)[1]

    grad, delta, new_m, new_v = {}, {}, {}, {}
    core = lax.axis_index("c").astype(jnp.int32).reshape(1)
    for k in BIG:
        grad[k], delta[k], new_m[k], new_v[k] = _adamw_shard(w[k], m[k], v[k], g_shard[k], core,
                                                             name="adamw_" + k, rows=SHARE_ROWS[k])
    gs, ds, ms, vs = _adamw(_pack(w), _pack(m), _pack(v), [g_packed], name="adamw_small")
    for dst, packed in ((grad, gs), (delta, ds), (new_m, ms), (new_v, vs)):
        dst.update(_unpack(packed)[0])

    return (loss_sum, grad_x[None], *[grad[k] for k in WEIGHTS], *[delta[k] for k in WEIGHTS],
            *[new_m[k] for k in WEIGHTS], *[new_v[k] for k in WEIGHTS])
```

```python
import functools
import math

import numpy as np
import jax
import jax.numpy as jnp
from jax import lax
from jax.experimental import pallas as pl
from jax.experimental.pallas import tpu as pltpu

F32 = jnp.float32
BF16 = jnp.bfloat16

D_MODEL = 2048
D_SSM = 1024
SSM_GROUP = 16
N_GROUPS = 64
SSM_STATE = 64
N_LANES = N_GROUPS * SSM_STATE
N_Q_HEADS = 16
N_KV_HEADS = 4
HEAD_DIM = 64
Q_PER_KV = 4
D_ATTN = 1024
D_KV = 256
WINDOW = 128
BLOCK = 128
N_BUCKETS = 32
MAX_DISTANCE = 128
D_IN = 8704
ALPHA = 2.0 ** 0.25
LN_EPS = 1e-5
NEG_INF = -1e30
ATTN_SCALE = HEAD_DIM ** -0.5

C_U, C_ZS, C_Q, C_K, C_V, C_ZA, C_GL = 0, 1024, 2048, 3072, 3328, 3584, 4608

ADAM_LR = 0.001
ADAM_B1 = 0.9
ADAM_B2 = 0.999
ADAM_EPS = 1e-08
ADAM_WD = 0.01
ADAM_STEP = 10

N_CHIPS = 4
MESH = pl.DeviceIdType.MESH

SSM_CHUNK = 256
SEG_LEN = SSM_CHUNK // 8
SLAB_LANES = 512
N_SLABS = N_LANES // SLAB_LANES
SLAB_CH = D_SSM // N_SLABS

VMEM_LIMIT = 60 * 1024 * 1024


def _params(sem=None, **kw):
    return pltpu.CompilerParams(dimension_semantics=sem, vmem_limit_bytes=VMEM_LIMIT, **kw)


_DIMS = {"nn": (((1,), (0,)), ((), ())), "nt": (((1,), (1,)), ((), ())), "tn": (((0,), (0,)), ((), ()))}


def _mm_body(*refs, dims, nk, res_scale):
    if res_scale is None:
        a_ref, b_ref, o_ref, acc_ref = refs
        r_ref = None
    else:
        a_ref, b_ref, r_ref, o_ref, acc_ref = refs
    k = pl.program_id(2)
    part = lax.dot_general(a_ref[...].astype(BF16), b_ref[...].astype(BF16), _DIMS[dims],
                           preferred_element_type=F32)

    def finish(acc):
        if r_ref is not None:
            acc = acc + res_scale * r_ref[...]
        o_ref[...] = acc.astype(o_ref.dtype)

    if nk == 1:
        finish(part)
    else:
        @pl.when(k == 0)
        def _():
            acc_ref[...] = part

        @pl.when(k > 0)
        def _():
            acc_ref[...] += part

        @pl.when(k == nk - 1)
        def _():
            finish(acc_ref[...])


def _matmul(a, b, dims, *, name, out_dtype=F32, tm=512, tn=512, tk=None, res=None, res_scale=None):
    if dims == "nn":
        (m, kk), n = a.shape, b.shape[1]
    elif dims == "nt":
        (m, kk), n = a.shape, b.shape[0]
    else:
        (kk, m), n = a.shape, b.shape[1]
    tm, tn = min(tm, m), min(tn, n)
    tk = kk if tk is None else min(tk, kk)
    assert m % tm == 0 and n % tn == 0 and kk % tk == 0, (name, m, n, kk, tm, tn, tk)
    nk = kk // tk
    a_spec = {"nn": pl.BlockSpec((tm, tk), lambda i, j, k: (i, k)),
              "nt": pl.BlockSpec((tm, tk), lambda i, j, k: (i, k)),
              "tn": pl.BlockSpec((tk, tm), lambda i, j, k: (k, i))}[dims]
    b_spec = {"nn": pl.BlockSpec((tk, tn), lambda i, j, k: (k, j)),
              "nt": pl.BlockSpec((tn, tk), lambda i, j, k: (j, k)),
              "tn": pl.BlockSpec((tk, tn), lambda i, j, k: (k, j))}[dims]
    in_specs, args = [a_spec, b_spec], [a, b]
    if res is not None:
        in_specs.append(pl.BlockSpec((tm, tn), lambda i, j, k: (i, j)))
        args.append(res)
    return pl.pallas_call(
        functools.partial(_mm_body, dims=dims, nk=nk, res_scale=res_scale if res is not None else None),
        name=name,
        grid=(m // tm, n // tn, nk),
        in_specs=in_specs,
        out_specs=pl.BlockSpec((tm, tn), lambda i, j, k: (i, j)),
        out_shape=jax.ShapeDtypeStruct((m, n), out_dtype),
        scratch_shapes=[pltpu.VMEM((tm, tn), F32)],
        compiler_params=_params(("parallel", "parallel", "arbitrary")),
    )(*args)


def _sigmoid(v):
    return 1.0 / (1.0 + jnp.exp(-v))


def _silu_and_grad(z):
    s = _sigmoid(z)
    return z * s, s * (1.0 + z * (1.0 - s))


def _cast_body(x_ref, o_ref):
    o_ref[...] = x_ref[...].astype(o_ref.dtype)


def _cast_and_transpose_body(x_ref, o_ref, ot_ref):
    xb = x_ref[...].astype(BF16)
    o_ref[...] = xb
    ot_ref[...] = xb.T


def _cast_and_transpose(x, *, name, rows=512):
    m, n = x.shape
    rows = min(rows, m)
    return pl.pallas_call(
        functools.partial(_cast_and_transpose_body), name=name, grid=(m // rows,),
        in_specs=[pl.BlockSpec((rows, n), lambda i: (i, 0))],
        out_specs=[pl.BlockSpec((rows, n), lambda i: (i, 0)), pl.BlockSpec((n, rows), lambda i: (0, i))],
        out_shape=[jax.ShapeDtypeStruct((m, n), BF16), jax.ShapeDtypeStruct((n, m), BF16)],
        compiler_params=_params(("parallel",)),
    )(x)


def _cast(x, dtype, *, name, rows=512):
    m, n = x.shape
    rows = min(rows, m)
    return pl.pallas_call(
        functools.partial(_cast_body), name=name, grid=(m // rows,),
        in_specs=[pl.BlockSpec((rows, n), lambda i: (i, 0))],
        out_specs=pl.BlockSpec((rows, n), lambda i: (i, 0)),
        out_shape=jax.ShapeDtypeStruct((m, n), dtype),
        compiler_params=_params(("parallel",)),
    )(x)


def _lam_bar(lr, li, ls):
    step = jnp.exp(ls)
    er = jnp.exp(lr * step)
    return step, er * jnp.cos(li * step), er * jnp.sin(li * step)


def _ssm_pow_body(lr_ref, li_ref, ls_ref, pr_ref, pi_ref):
    _, ar, ai = _lam_bar(lr_ref[...], li_ref[...], ls_ref[...])
    cr, ci = ar, ai
    for i in range(SEG_LEN):
        pr_ref[pl.ds(i, 1), :] = cr
        pi_ref[pl.ds(i, 1), :] = ci
        cr, ci = cr * ar - ci * ai, cr * ai + ci * ar


def _ssm_bbar_body(lr_ref, li_ref, ls_ref, br_ref, bi_ref, or_ref, oi_ref):
    lr, li = lr_ref[...], li_ref[...]
    _, ar, ai = _lam_bar(lr, li, ls_ref[...])
    d = lr * lr + li * li
    ir, ii = lr / d, -li / d
    nr, ni = ar - 1.0, ai
    cr, ci = nr * ir - ni * ii, nr * ii + ni * ir
    br, bi = br_ref[...], bi_ref[...]
    or_ref[...] = cr * br - ci * bi
    oi_ref[...] = cr * bi + ci * br


def _ssm_param_bwd_body(lr_ref, li_ref, ls_ref, br_ref, bi_ref, glr_ref, gli_ref, gbr_ref, gbi_ref,
                        dlr_ref, dli_ref, dls_ref, dbr_ref, dbi_ref):
    lr, li = lr_ref[...], li_ref[...]
    step, ar, ai = _lam_bar(lr, li, ls_ref[...])
    d = lr * lr + li * li
    ir, ii = lr / d, -li / d
    nr, ni = ar - 1.0, ai
    cr, ci = nr * ir - ni * ii, nr * ii + ni * ir
    br, bi, gbr, gbi = br_ref[...], bi_ref[...], gbr_ref[...], gbi_ref[...]
    dbr_ref[...] = cr * gbr + ci * gbi
    dbi_ref[...] = cr * gbi - ci * gbr
    gcr = jnp.sum(br * gbr + bi * gbi, axis=1, keepdims=True)
    gci = jnp.sum(br * gbi - bi * gbr, axis=1, keepdims=True)
    gnr, gni = ir * gcr + ii * gci, ir * gci - ii * gcr
    gir, gii = nr * gcr + ni * gci, nr * gci - ni * gcr
    gtr, gti = glr_ref[...] + gnr, gli_ref[...] + gni
    i2r, i2i = ir * ir - ii * ii, 2.0 * ir * ii
    g1r, g1i = -(i2r * gir + i2i * gii), -(i2r * gii - i2i * gir)
    g2r, g2i = step * (ar * gtr + ai * gti), step * (ar * gti - ai * gtr)
    mr, mi = lr * ar - li * ai, lr * ai + li * ar
    dlr_ref[...] = g1r + g2r
    dli_ref[...] = g1i + g2i
    dls_ref[...] = (mr * gtr + mi * gti) * step


def _whole(shape):
    return pl.BlockSpec(shape, lambda *_: (0,) * len(shape))


def _ssm_prepare(lam_re, lam_im, b_re, b_im, c_re, c_im, log_step):
    row = lambda a: a.reshape(1, N_LANES)
    col = lambda a: a.reshape(N_LANES, 1)
    ls = jnp.repeat(log_step.reshape(N_GROUPS), SSM_STATE)
    p_re, p_im = pl.pallas_call(
        functools.partial(_ssm_pow_body), name="ssm_pow",
        in_specs=[_whole((1, N_LANES))] * 3, out_specs=[_whole((SEG_LEN, N_LANES))] * 2,
        out_shape=[jax.ShapeDtypeStruct((SEG_LEN, N_LANES), F32)] * 2, grid=(1,),
    )(row(lam_re), row(lam_im), row(ls))
    bb_re, bb_im = pl.pallas_call(
        functools.partial(_ssm_bbar_body), name="ssm_bbar",
        in_specs=[_whole((N_LANES, 1))] * 3 + [_whole((N_LANES, SSM_GROUP))] * 2,
        out_specs=[_whole((N_LANES, SSM_GROUP))] * 2,
        out_shape=[jax.ShapeDtypeStruct((N_LANES, SSM_GROUP), F32)] * 2, grid=(1,),
    )(col(lam_re), col(lam_im), col(ls), b_re.reshape(N_LANES, SSM_GROUP), b_im.reshape(N_LANES, SSM_GROUP))
    eye = jnp.eye(8, dtype=F32)

    def b_slabs(bb):
        t = bb.reshape(N_SLABS, 8, SSM_STATE, SSM_GROUP).transpose(0, 1, 3, 2)
        return (t[:, :, :, None, :] * eye[None, :, None, :, None]).reshape(N_SLABS, SLAB_CH, SLAB_LANES)

    def c_slabs(c):
        t = c.reshape(N_SLABS, 8, SSM_GROUP, SSM_STATE).transpose(0, 1, 3, 2)
        return (t[:, :, :, None, :] * eye[None, :, None, :, None]).reshape(N_SLABS, SLAB_LANES, SLAB_CH)

    b_bd = jnp.concatenate([b_slabs(bb_re), b_slabs(bb_im)], axis=2).astype(BF16)
    c_bd = jnp.concatenate([c_slabs(c_re.reshape(N_GROUPS, SSM_GROUP, SSM_STATE)),
                            -c_slabs(c_im.reshape(N_GROUPS, SSM_GROUP, SSM_STATE))], axis=1).astype(BF16)
    return p_re, p_im, b_bd, c_bd


def _diag_blocks_b(g):
    t = g.reshape(N_SLABS, 8, SSM_GROUP, 8, SSM_STATE)
    t = jnp.stack([t[:, i, :, i, :] for i in range(8)], axis=1)
    return t.transpose(0, 1, 3, 2).reshape(N_LANES, SSM_GROUP)


def _diag_blocks_c(g):
    t = g.reshape(N_SLABS, 8, SSM_STATE, 8, SSM_GROUP)
    t = jnp.stack([t[:, i, :, i, :] for i in range(8)], axis=1)
    return t.transpose(0, 1, 3, 2).reshape(N_GROUPS, SSM_GROUP, SSM_STATE)


def _ssm_param_grads(lam_re, lam_im, b_re, b_im, log_step, g_lam_re, g_lam_im, g_bbd, g_cbd):
    col = lambda a: a.reshape(N_LANES, 1)
    ls = jnp.repeat(log_step.reshape(N_GROUPS), SSM_STATE)
    gbr = _diag_blocks_b(g_bbd[:, :, :SLAB_LANES])
    gbi = _diag_blocks_b(g_bbd[:, :, SLAB_LANES:])
    outs = pl.pallas_call(
        functools.partial(_ssm_param_bwd_body), name="ssm_param_bwd", grid=(1,),
        in_specs=[_whole((N_LANES, 1))] * 3 + [_whole((N_LANES, SSM_GROUP))] * 2 + [_whole((N_LANES, 1))] * 2
        + [_whole((N_LANES, SSM_GROUP))] * 2,
        out_specs=[_whole((N_LANES, 1))] * 3 + [_whole((N_LANES, SSM_GROUP))] * 2,
        out_shape=[jax.ShapeDtypeStruct((N_LANES, 1), F32)] * 3 + [jax.ShapeDtypeStruct((N_LANES, SSM_GROUP), F32)] * 2,
    )(col(lam_re), col(lam_im), col(ls), b_re.reshape(N_LANES, SSM_GROUP), b_im.reshape(N_LANES, SSM_GROUP),
      col(g_lam_re), col(g_lam_im), gbr, gbi)
    dlr, dli, dls, dbr, dbi = outs
    d_c_re = _diag_blocks_c(g_cbd[:, :SLAB_LANES, :])
    d_c_im = -_diag_blocks_c(g_cbd[:, SLAB_LANES:, :])
    return (dlr.reshape(1, N_GROUPS, SSM_STATE), dli.reshape(1, N_GROUPS, SSM_STATE),
            dbr.reshape(1, N_GROUPS, SSM_STATE, SSM_GROUP), dbi.reshape(1, N_GROUPS, SSM_STATE, SSM_GROUP),
            d_c_re[None], d_c_im[None], dls.reshape(N_GROUPS, SSM_STATE).sum(axis=1).reshape(1, N_GROUPS))


def _bcast8(v):
    return jnp.broadcast_to(v, (8, v.shape[1]))


def _segment_permutation():
    p = np.zeros((SSM_CHUNK, SSM_CHUNK), np.float32)
    rows = np.arange(SSM_CHUNK)
    p[rows, (rows % 8) * SEG_LEN + rows // 8] = 1.0
    return p


def _permute_exact(perm, val, pieces):
    out, rest = None, val
    for n in range(pieces):
        part = rest.astype(BF16)
        moved = jnp.dot(perm, part, preferred_element_type=F32)
        out = moved if out is None else out + moved
        if n + 1 < pieces:
            rest = rest - part.astype(F32)
    return out


def _scan_buffer():
    return pltpu.VMEM((SSM_CHUNK, N_LANES), F32)


def _lanes(k):
    return pl.ds(k * SLAB_LANES, SLAB_LANES)


def _tile(i):
    return pl.ds(i * 8 if isinstance(i, int) else pl.multiple_of(i * 8, 8), 8)


def _seg_get(ref, k, i):
    return ref[_tile(i), _lanes(k)]


def _seg_put(ref, k, i, val):
    ref[_tile(i), _lanes(k)] = val


def _slab_get(ref, k):
    return ref[:, _lanes(k)]


def _slab_put(ref, k, val):
    ref[:, _lanes(k)] = val


def _scan_forward(s_re, s_im, p_re, p_im, car_re, car_im, sp_re=None, sp_im=None):
    for k in range(N_SLABS):
        ln = pl.ds(k * SLAB_LANES, SLAB_LANES)
        ar, ai = _bcast8(p_re[pl.ds(0, 1), ln]), _bcast8(p_im[pl.ds(0, 1), ln])

        def step(i, s, k=k, ar=ar, ai=ai):
            sr, si = s
            nr = ar * sr - ai * si + _seg_get(s_re, k, i)
            ni = ar * si + ai * sr + _seg_get(s_im, k, i)
            _seg_put(s_re, k, i, nr)
            _seg_put(s_im, k, i, ni)
            return nr, ni

        zero = jnp.zeros((8, SLAB_LANES), F32)
        er, ei = lax.fori_loop(0, SEG_LEN, step, (zero, zero), unroll=4)
        lr, li = p_re[pl.ds(SEG_LEN - 1, 1), ln], p_im[pl.ds(SEG_LEN - 1, 1), ln]
        cr, ci = car_re[pl.ds(0, 1), ln], car_im[pl.ds(0, 1), ln]
        rows_r, rows_i = [], []
        for r in range(8):
            rows_r.append(cr)
            rows_i.append(ci)
            cr, ci = er[r:r + 1] + lr * cr - li * ci, ei[r:r + 1] + lr * ci + li * cr
        pr8, pi8 = jnp.concatenate(rows_r, axis=0), jnp.concatenate(rows_i, axis=0)
        car_re[:, ln] = _bcast8(cr)
        car_im[:, ln] = _bcast8(ci)
        if sp_re is not None:
            sp_re[:, ln] = pr8
            sp_im[:, ln] = pi8

        def fix(i, _, k=k, ln=ln, pr8=pr8, pi8=pi8):
            qr, qi = _bcast8(p_re[pl.ds(i, 1), ln]), _bcast8(p_im[pl.ds(i, 1), ln])
            _seg_put(s_re, k, i, _seg_get(s_re, k, i) + qr * pr8 - qi * pi8)
            _seg_put(s_im, k, i, _seg_get(s_im, k, i) + qr * pi8 + qi * pr8)
            return 0

        lax.fori_loop(0, SEG_LEN, fix, 0, unroll=4)


def _scan_backward(g_re, g_im, s_re, s_im, sp_re, sp_im, p_re, p_im, car_re, car_im, acc_re, acc_im):
    for k in range(N_SLABS):
        ln = pl.ds(k * SLAB_LANES, SLAB_LANES)
        ar, ai = _bcast8(p_re[pl.ds(0, 1), ln]), -_bcast8(p_im[pl.ds(0, 1), ln])

        def step(j, s, k=k, ar=ar, ai=ai):
            i = SEG_LEN - 1 - j
            sr, si = s
            nr = ar * sr - ai * si + _seg_get(g_re, k, i)
            ni = ar * si + ai * sr + _seg_get(g_im, k, i)
            _seg_put(g_re, k, i, nr)
            _seg_put(g_im, k, i, ni)
            return nr, ni

        zero = jnp.zeros((8, SLAB_LANES), F32)
        er, ei = lax.fori_loop(0, SEG_LEN, step, (zero, zero), unroll=4)
        lr, li = p_re[pl.ds(SEG_LEN - 1, 1), ln], -p_im[pl.ds(SEG_LEN - 1, 1), ln]
        cr, ci = car_re[pl.ds(0, 1), ln], car_im[pl.ds(0, 1), ln]
        rows_r, rows_i = [None] * 8, [None] * 8
        for r in range(7, -1, -1):
            rows_r[r], rows_i[r] = cr, ci
            cr, ci = er[r:r + 1] + lr * cr - li * ci, ei[r:r + 1] + lr * ci + li * cr
        nr8, ni8 = jnp.concatenate(rows_r, axis=0), jnp.concatenate(rows_i, axis=0)
        car_re[:, ln] = _bcast8(cr)
        car_im[:, ln] = _bcast8(ci)

        def fix(i, acc, k=k, ln=ln, nr8=nr8, ni8=ni8):
            qr = _bcast8(p_re[pl.ds(SEG_LEN - 1 - i, 1), ln])
            qi = -_bcast8(p_im[pl.ds(SEG_LEN - 1 - i, 1), ln])
            gr = _seg_get(g_re, k, i) + qr * nr8 - qi * ni8
            gi = _seg_get(g_im, k, i) + qr * ni8 + qi * nr8
            _seg_put(g_re, k, i, gr)
            _seg_put(g_im, k, i, gi)
            return gr, gi

        def prod(xr, xi, gr, gi):
            return xr * gr + xi * gi, xr * gi - xi * gr

        gr, gi = fix(0, None)
        a_r, a_i = prod(sp_re[:, ln], sp_im[:, ln], gr, gi)

        def fix_acc(i, acc, k=k, fix=fix):
            gr, gi = fix(i, None)
            dr, di = prod(_seg_get(s_re, k, i - 1), _seg_get(s_im, k, i - 1), gr, gi)
            return acc[0] + dr, acc[1] + di

        acc = (a_r, a_i)
        for i in range(1, 4):
            acc = fix_acc(i, acc)
        a_r, a_i = lax.fori_loop(4, SEG_LEN, fix_acc, acc, unroll=4)
        acc_re[:, ln] += a_r
        acc_im[:, ln] += a_i


def _gelu_and_grad(y):
    cdf = 0.5 * (1.0 + lax.erf(y * (2.0 ** -0.5)))
    pdf = jnp.exp(-0.5 * y * y) * (1.0 / math.sqrt(2.0 * math.pi))
    return y * cdf, cdf + y * pdf


def _ssm_fwd_body(u_ref, bbd_ref, cbd_ref, pre_ref, pim_ref, d_ref, perm_ref, unperm_ref,
                  y_ref, gin_ref, cre_out, cim_out, s_re, s_im, car_re, car_im, yp):
    c = pl.program_id(0)

    @pl.when(c == 0)
    def _():
        car_re[...] = jnp.zeros_like(car_re)
        car_im[...] = jnp.zeros_like(car_im)

    cre_out[...] = car_re[...]
    cim_out[...] = car_im[...]
    u = u_ref[...].astype(F32)
    up = jnp.dot(perm_ref[...], u_ref[...].astype(BF16), preferred_element_type=F32).astype(BF16)
    for k in range(N_SLABS):
        bu = jnp.dot(up[:, k * SLAB_CH:(k + 1) * SLAB_CH], bbd_ref[k], preferred_element_type=F32)
        _slab_put(s_re, k, bu[:, :SLAB_LANES])
        _slab_put(s_im, k, bu[:, SLAB_LANES:])
    _scan_forward(s_re, s_im, pre_ref, pim_ref, car_re, car_im)
    for k in range(N_SLABS):
        yp[:, pl.ds(k * SLAB_CH, SLAB_CH)] = (
            jnp.dot(_slab_get(s_re, k).astype(BF16), cbd_ref[k, :SLAB_LANES, :], preferred_element_type=F32)
            + jnp.dot(_slab_get(s_im, k).astype(BF16), cbd_ref[k, SLAB_LANES:, :], preferred_element_type=F32))
    y = _permute_exact(unperm_ref[...], yp[...], 3) + d_ref[...] * u
    y_ref[...] = y.astype(y_ref.dtype)
    gin_ref[...] = _gelu_and_grad(y)[0].astype(BF16)


def _ssm_forward(proj, p_re, p_im, b_bd, c_bd, d_skip):
    t = proj.shape[0]
    nc = t // SSM_CHUNK
    perm = _segment_permutation()
    return pl.pallas_call(
        functools.partial(_ssm_fwd_body), name="ssm_fwd", grid=(nc,),
        in_specs=[pl.BlockSpec((SSM_CHUNK, D_SSM), lambda c: (c, C_U // D_SSM)),
                  _whole(b_bd.shape), _whole(c_bd.shape), _whole(p_re.shape), _whole(p_im.shape),
                  _whole((1, D_SSM)), _whole(perm.shape), _whole(perm.shape)],
        out_specs=[pl.BlockSpec((SSM_CHUNK, D_SSM), lambda c: (c, 0)),
                   pl.BlockSpec((SSM_CHUNK, D_SSM), lambda c: (c, 0)),
                   pl.BlockSpec((None, 8, N_LANES), lambda c: (c, 0, 0)),
                   pl.BlockSpec((None, 8, N_LANES), lambda c: (c, 0, 0))],
        out_shape=[jax.ShapeDtypeStruct((t, D_SSM), BF16), jax.ShapeDtypeStruct((t, D_SSM), BF16),
                   jax.ShapeDtypeStruct((nc, 8, N_LANES), F32), jax.ShapeDtypeStruct((nc, 8, N_LANES), F32)],
        scratch_shapes=[_scan_buffer(), _scan_buffer(),
                        pltpu.VMEM((8, N_LANES), F32), pltpu.VMEM((8, N_LANES), F32),
                        pltpu.VMEM((SSM_CHUNK, D_SSM), F32)],
        compiler_params=_params(("arbitrary",)),
    )(proj, b_bd, c_bd, p_re, p_im, d_skip, jnp.asarray(perm, BF16), jnp.asarray(perm.T, BF16))


def _ssm_bwd_body(u_ref, dgin_ref, y_ref, cre_in, cim_in, bbd_ref, cbd_ref, pre_ref, pim_ref, d_ref, perm_ref,
                  unperm_ref, dproj_in,
                  du_ref, gb_ref, gc_ref, glr_ref, gli_ref, gd_ref,
                  s_re, s_im, g_re, g_im, sp_re, sp_im, car_re, car_im, gcar_re, gcar_im, acc_re, acc_im, dup):
    del dproj_in
    c = pl.program_id(0)
    nc = pl.num_programs(0)

    @pl.when(c == 0)
    def _():
        gcar_re[...] = jnp.zeros_like(gcar_re)
        gcar_im[...] = jnp.zeros_like(gcar_im)
        acc_re[...] = jnp.zeros_like(acc_re)
        acc_im[...] = jnp.zeros_like(acc_im)
        gb_ref[...] = jnp.zeros_like(gb_ref)
        gc_ref[...] = jnp.zeros_like(gc_ref)
        gd_ref[...] = jnp.zeros_like(gd_ref)

    car_re[...] = cre_in[...]
    car_im[...] = cim_in[...]
    u = u_ref[...].astype(F32)
    dy = dgin_ref[...].astype(F32) * _gelu_and_grad(y_ref[...].astype(F32))[1]
    gd_ref[...] += jnp.sum(dy * u, axis=0, keepdims=True)
    up = jnp.dot(perm_ref[...], u.astype(BF16), preferred_element_type=F32).astype(BF16)
    dyp = jnp.dot(perm_ref[...], dy.astype(BF16), preferred_element_type=F32).astype(BF16)
    for k in range(N_SLABS):
        ch = slice(k * SLAB_CH, (k + 1) * SLAB_CH)
        bu = jnp.dot(up[:, ch], bbd_ref[k], preferred_element_type=F32)
        _slab_put(s_re, k, bu[:, :SLAB_LANES])
        _slab_put(s_im, k, bu[:, SLAB_LANES:])
        ds = lax.dot_general(dyp[:, ch], cbd_ref[k], _DIMS["nt"], preferred_element_type=F32)
        _slab_put(g_re, k, ds[:, :SLAB_LANES])
        _slab_put(g_im, k, ds[:, SLAB_LANES:])
    _scan_forward(s_re, s_im, pre_ref, pim_ref, car_re, car_im, sp_re, sp_im)
    _scan_backward(g_re, g_im, s_re, s_im, sp_re, sp_im, pre_ref, pim_ref, gcar_re, gcar_im, acc_re, acc_im)
    for k in range(N_SLABS):
        ch = slice(k * SLAB_CH, (k + 1) * SLAB_CH)
        uk, dyk = up[:, ch], dyp[:, ch]
        sr, si = _slab_get(s_re, k).astype(BF16), _slab_get(s_im, k).astype(BF16)
        gr, gi = _slab_get(g_re, k).astype(BF16), _slab_get(g_im, k).astype(BF16)
        gc_ref[k, :SLAB_LANES, :] += lax.dot_general(sr, dyk, _DIMS["tn"], preferred_element_type=F32)
        gc_ref[k, SLAB_LANES:, :] += lax.dot_general(si, dyk, _DIMS["tn"], preferred_element_type=F32)
        gb_ref[k, :, :SLAB_LANES] += lax.dot_general(uk, gr, _DIMS["tn"], preferred_element_type=F32)
        gb_ref[k, :, SLAB_LANES:] += lax.dot_general(uk, gi, _DIMS["tn"], preferred_element_type=F32)
        dup[:, pl.ds(k * SLAB_CH, SLAB_CH)] = (
            lax.dot_general(gr, bbd_ref[k, :, :SLAB_LANES], _DIMS["nt"], preferred_element_type=F32)
            + lax.dot_general(gi, bbd_ref[k, :, SLAB_LANES:], _DIMS["nt"], preferred_element_type=F32))
    du = _permute_exact(unperm_ref[...], dup[...], 2) + d_ref[...] * dy
    du_ref[...] = du.astype(du_ref.dtype)

    @pl.when(c == nc - 1)
    def _():
        glr_ref[...] = jnp.sum(acc_re[...], axis=0, keepdims=True)
        gli_ref[...] = jnp.sum(acc_im[...], axis=0, keepdims=True)


def _ssm_backward(proj, dg_in, y_ssm, car_re, car_im, p_re, p_im, b_bd, c_bd, d_skip, dproj):
    t = proj.shape[0]
    nc = t // SSM_CHUNK
    rev = lambda c: nc - 1 - c
    big = _scan_buffer
    small = lambda: pltpu.VMEM((8, N_LANES), F32)
    perm = _segment_permutation()
    outs = pl.pallas_call(
        functools.partial(_ssm_bwd_body), name="ssm_bwd", grid=(nc,),
        in_specs=[pl.BlockSpec((SSM_CHUNK, D_SSM), lambda c: (rev(c), C_U // D_SSM)),
                  pl.BlockSpec((SSM_CHUNK, D_SSM), lambda c: (rev(c), 0)),
                  pl.BlockSpec((SSM_CHUNK, D_SSM), lambda c: (rev(c), 0)),
                  pl.BlockSpec((None, 8, N_LANES), lambda c: (rev(c), 0, 0)),
                  pl.BlockSpec((None, 8, N_LANES), lambda c: (rev(c), 0, 0)),
                  _whole(b_bd.shape), _whole(c_bd.shape), _whole(p_re.shape), _whole(p_im.shape),
                  _whole((1, D_SSM)), _whole(perm.shape), _whole(perm.shape), pl.BlockSpec(memory_space=pl.ANY)],
        out_specs=[pl.BlockSpec((SSM_CHUNK, D_SSM), lambda c: (rev(c), C_U // D_SSM)),
                   _whole(b_bd.shape), _whole(c_bd.shape), _whole((1, N_LANES)), _whole((1, N_LANES)),
                   _whole((1, D_SSM))],
        out_shape=[jax.ShapeDtypeStruct(dproj.shape, dproj.dtype),
                   jax.ShapeDtypeStruct(b_bd.shape, F32), jax.ShapeDtypeStruct(c_bd.shape, F32),
                   jax.ShapeDtypeStruct((1, N_LANES), F32), jax.ShapeDtypeStruct((1, N_LANES), F32),
                   jax.ShapeDtypeStruct((1, D_SSM), F32)],
        scratch_shapes=[big(), big(), big(), big()] + [small() for _ in range(8)]
        + [pltpu.VMEM((SSM_CHUNK, D_SSM), F32)],
        input_output_aliases={12: 0},
        compiler_params=_params(("arbitrary",)),
    )(proj, dg_in, y_ssm, car_re, car_im, b_bd, c_bd, p_re, p_im, d_skip, jnp.asarray(perm, BF16),
      jnp.asarray(perm.T, BF16), dproj)
    return outs


def _bucket_table():
    i = np.arange(BLOCK)[:, None]
    j = np.arange(2 * BLOCK)[None, :]
    dist = BLOCK + i - j
    ok = (dist >= 0) & (dist < WINDOW)
    max_exact = N_BUCKETS // 2
    d = np.maximum(dist, 1).astype(np.float32)
    large = max_exact + (np.log(d / max_exact) / math.log(MAX_DISTANCE / max_exact)
                         * (N_BUCKETS - max_exact)).astype(np.int32)
    large = np.minimum(large, N_BUCKETS - 1)
    bucket = np.where(dist < max_exact, dist, large)
    return np.where(ok, bucket, -1).astype(np.int32)


def _bias_build_body(table_ref, bucket_ref, o_ref):
    h = pl.program_id(0)
    bucket = bucket_ref[...]
    acc = jnp.full(bucket.shape, NEG_INF, F32)
    for b in range(N_BUCKETS):
        acc = jnp.where(bucket == b, table_ref[b, h], acc)
    o_ref[...] = acc


def _bias_build(rel_bias_table, bucket):
    return pl.pallas_call(
        functools.partial(_bias_build_body), name="bias_build", grid=(N_Q_HEADS,),
        in_specs=[pl.BlockSpec(memory_space=pltpu.SMEM), _whole(bucket.shape)],
        out_specs=pl.BlockSpec((None, BLOCK, 2 * BLOCK), lambda h: (h, 0, 0)),
        out_shape=jax.ShapeDtypeStruct((N_Q_HEADS, BLOCK, 2 * BLOCK), F32),
        compiler_params=_params(("arbitrary",)),
    )(rel_bias_table, bucket)


def _bias_grad_body(g_ref, bucket_ref, o_ref):
    bucket = bucket_ref[...]
    lane = lax.broadcasted_iota(jnp.int32, (N_BUCKETS, 128), 1)

    def head(h, out):
        g = g_ref[h]
        rows = [jnp.sum(jnp.where(bucket == b, g, 0.0), axis=0, keepdims=True) for b in range(N_BUCKETS)]
        colsum = jnp.sum(jnp.concatenate(rows, axis=0), axis=1, keepdims=True)
        return jnp.where(lane == h, colsum, out)

    o_ref[...] = lax.fori_loop(0, N_Q_HEADS, head, jnp.zeros((N_BUCKETS, 128), F32))


def _bias_grad(g_bias, bucket):
    out = pl.pallas_call(
        functools.partial(_bias_grad_body), name="bias_grad", grid=(1,),
        in_specs=[_whole(g_bias.shape), _whole(bucket.shape)],
        out_specs=_whole((N_BUCKETS, 128)),
        out_shape=jax.ShapeDtypeStruct((N_BUCKETS, 128), F32),
        compiler_params=_params(("arbitrary",)),
    )(g_bias, bucket)
    return out[:, :N_Q_HEADS]


def _head_logits(qh, kk, bias_h, first_block):
    s = lax.dot_general(qh, kk, _DIMS["nt"], preferred_element_type=F32) * ATTN_SCALE + bias_h
    col = lax.broadcasted_iota(jnp.int32, s.shape, 1)
    return jnp.where(jnp.logical_and(first_block, col < BLOCK), NEG_INF, s)


def _attn_fwd_body(sink_ref, q_ref, kp_ref, kc_ref, vp_ref, vc_ref, bias_ref, o_ref, lse_ref):
    n = pl.program_id(0)
    outs, lses = [], []
    for kv in range(N_KV_HEADS):
        cs = slice(kv * HEAD_DIM, (kv + 1) * HEAD_DIM)
        kk = jnp.concatenate([kp_ref[:, cs], kc_ref[:, cs]], axis=0).astype(BF16)
        vv = jnp.concatenate([vp_ref[:, cs], vc_ref[:, cs]], axis=0).astype(BF16)
        for g in range(Q_PER_KV):
            h = kv * Q_PER_KV + g
            qh = q_ref[:, h * HEAD_DIM:(h + 1) * HEAD_DIM].astype(BF16)
            s = _head_logits(qh, kk, bias_ref[h], n == 0)
            sink = sink_ref[0, h]
            m = jnp.maximum(jnp.max(s, axis=1, keepdims=True), sink)
            p = jnp.exp(s - m)
            den = jnp.sum(p, axis=1, keepdims=True) + jnp.exp(sink - m)
            p = p / den
            outs.append(jnp.dot(p.astype(BF16), vv, preferred_element_type=F32))
            lses.append(m + jnp.log(den))
    o_ref[...] = jnp.concatenate(outs, axis=1).astype(o_ref.dtype)
    lse_ref[...] = jnp.concatenate(lses, axis=1)


def _attn_specs(nb):
    prev = lambda n: jnp.maximum(jnp.minimum(n, nb - 1) - 1, 0)
    cur = lambda n: jnp.minimum(n, nb - 1)
    return [pl.BlockSpec((BLOCK, D_ATTN), lambda n: (cur(n), C_Q // D_ATTN)),
            pl.BlockSpec((BLOCK, D_KV), lambda n: (prev(n), C_K // D_KV)),
            pl.BlockSpec((BLOCK, D_KV), lambda n: (cur(n), C_K // D_KV)),
            pl.BlockSpec((BLOCK, D_KV), lambda n: (prev(n), C_V // D_KV)),
            pl.BlockSpec((BLOCK, D_KV), lambda n: (cur(n), C_V // D_KV))]


def _attn_forward(proj, sinks, bias):
    t = proj.shape[0]
    nb = t // BLOCK
    return pl.pallas_call(
        functools.partial(_attn_fwd_body), name="attn_fwd", grid=(nb,),
        in_specs=[pl.BlockSpec(memory_space=pltpu.SMEM)] + _attn_specs(nb) + [_whole(bias.shape)],
        out_specs=[pl.BlockSpec((BLOCK, D_ATTN), lambda n: (n, 0)),
                   pl.BlockSpec((BLOCK, N_Q_HEADS), lambda n: (n, 0))],
        out_shape=[jax.ShapeDtypeStruct((t, D_ATTN), BF16), jax.ShapeDtypeStruct((t, N_Q_HEADS), F32)],
        compiler_params=_params(("parallel",)),
    )(sinks, proj, proj, proj, proj, proj, bias)


def _attn_bwd_body(sink_ref, q_ref, kp_ref, kc_ref, vp_ref, vc_ref, bias_ref, do_ref, o_ref, lse_ref, dproj_in,
                   dq_ref, dkv_ref, gbias_ref, gsink_ref, carry_ref, *, nb):
    del dproj_in
    n = pl.program_id(0)

    @pl.when(n == 0)
    def _():
        gbias_ref[...] = jnp.zeros_like(gbias_ref)
        gsink_ref[...] = jnp.zeros_like(gsink_ref)
        carry_ref[...] = jnp.zeros_like(carry_ref)

    @pl.when(n < nb)
    def _():
        lane = lax.broadcasted_iota(jnp.int32, (1, 128), 1)
        dqs, dks, dvs = [], [], []
        gsink = jnp.zeros((1, 128), F32)
        for kv in range(N_KV_HEADS):
            cs = slice(kv * HEAD_DIM, (kv + 1) * HEAD_DIM)
            kk = jnp.concatenate([kp_ref[:, cs], kc_ref[:, cs]], axis=0).astype(BF16)
            vv = jnp.concatenate([vp_ref[:, cs], vc_ref[:, cs]], axis=0).astype(BF16)
            dk = jnp.zeros((2 * BLOCK, HEAD_DIM), F32)
            dv = jnp.zeros((2 * BLOCK, HEAD_DIM), F32)
            for g in range(Q_PER_KV):
                h = kv * Q_PER_KV + g
                hs = slice(h * HEAD_DIM, (h + 1) * HEAD_DIM)
                qh = q_ref[:, hs].astype(BF16)
                s = _head_logits(qh, kk, bias_ref[h], n == 0)
                lse = lse_ref[:, h:h + 1]
                p = jnp.exp(s - lse)
                do = do_ref[:, hs].astype(F32)
                delta = jnp.sum(do * o_ref[:, hs].astype(F32), axis=1, keepdims=True)
                dob = do.astype(BF16)
                dp = lax.dot_general(dob, vv, _DIMS["nt"], preferred_element_type=F32)
                dl = p * (dp - delta)
                gbias_ref[h] += dl
                psink = jnp.exp(sink_ref[0, h] - lse)
                gsink = gsink + jnp.where(lane == h, -jnp.sum(psink * delta), 0.0)
                dlb = dl.astype(BF16)
                dqs.append(jnp.dot(dlb, kk, preferred_element_type=F32) * ATTN_SCALE)
                dk = dk + lax.dot_general(dlb, qh, _DIMS["tn"], preferred_element_type=F32) * ATTN_SCALE
                dv = dv + lax.dot_general(p.astype(BF16), dob, _DIMS["tn"], preferred_element_type=F32)
            dks.append(dk)
            dvs.append(dv)
        dq_ref[...] = jnp.concatenate(dqs, axis=1).astype(dq_ref.dtype)
        gsink_ref[...] += gsink
        dkv = jnp.concatenate(dks + dvs, axis=1)
        dkv_ref[...] = (carry_ref[...] + dkv[:BLOCK]).astype(dkv_ref.dtype)
        carry_ref[...] = dkv[BLOCK:]

    @pl.when(n == nb)
    def _():
        dkv_ref[...] = carry_ref[...].astype(dkv_ref.dtype)


def _attn_backward(proj, sinks, bias, d_attn, attn, lse, dproj):
    t = proj.shape[0]
    nb = t // BLOCK
    cur = lambda n: jnp.minimum(n, nb - 1)
    return pl.pallas_call(
        functools.partial(_attn_bwd_body, nb=nb), name="attn_bwd", grid=(nb + 1,),
        in_specs=[pl.BlockSpec(memory_space=pltpu.SMEM)] + _attn_specs(nb) + [
            _whole(bias.shape),
            pl.BlockSpec((BLOCK, D_ATTN), lambda n: (cur(n), 0)),
            pl.BlockSpec((BLOCK, D_ATTN), lambda n: (cur(n), 0)),
            pl.BlockSpec((BLOCK, N_Q_HEADS), lambda n: (cur(n), 0)),
            pl.BlockSpec(memory_space=pl.ANY)],
        out_specs=[pl.BlockSpec((BLOCK, D_ATTN), lambda n: (cur(n), C_Q // D_ATTN)),
                   pl.BlockSpec((BLOCK, 2 * D_KV), lambda n: (jnp.maximum(n - 1, 0), 0)),
                   _whole(bias.shape), _whole((1, 128))],
        out_shape=[jax.ShapeDtypeStruct(dproj.shape, dproj.dtype), jax.ShapeDtypeStruct((t, 2 * D_KV), dproj.dtype),
                   jax.ShapeDtypeStruct(bias.shape, F32), jax.ShapeDtypeStruct((1, 128), F32)],
        scratch_shapes=[pltpu.VMEM((BLOCK, 2 * D_KV), F32)],
        input_output_aliases={10: 0},
        compiler_params=_params(("arbitrary",)),
    )(sinks, proj, proj, proj, proj, proj, bias, d_attn, attn, lse, dproj)


ROWS = 256


def _rowwise(body, name, t, ins, outs, aliases=None):
    rows = min(ROWS, t)

    def col_spec(w, c0):
        if c0 % w == 0:
            return pl.BlockSpec((rows, w), lambda i: (i, c0 // w))
        return pl.BlockSpec((pl.Element(rows), pl.Element(w)), lambda i: (i * rows, c0))

    in_specs, args = [], []
    for a, w, c0 in ins:
        args.append(a)
        if w is None:
            in_specs.append(pl.BlockSpec(memory_space=pl.ANY) if c0 == "any" else _whole(a.shape))
        else:
            in_specs.append(col_spec(w, c0))
    out_specs, out_shape = [], []
    for shape, dtype, w, c0 in outs:
        out_shape.append(jax.ShapeDtypeStruct(shape, dtype))
        out_specs.append(_whole(shape) if w is None else col_spec(w, c0))
    accum = any(o[2] is None for o in outs)
    return pl.pallas_call(
        body, name=name, grid=(t // rows,), in_specs=in_specs, out_specs=out_specs, out_shape=out_shape,
        input_output_aliases=aliases or {},
        compiler_params=_params(("arbitrary",) if accum else ("parallel",)),
    )(*args)


def _f32(ref, *idx):
    return (ref[idx] if idx else ref[...]).astype(F32)


def _ssm_gate_fwd_body(glu_ref, z_ref, h_ref):
    a, b = _f32(glu_ref, slice(None), slice(0, D_SSM)), _f32(glu_ref, slice(None), slice(D_SSM, None))
    h_ref[...] = ((a * _sigmoid(b)) * _silu_and_grad(_f32(z_ref))[0]).astype(h_ref.dtype)


def _ssm_gate_bwd_body(dh_ref, glu_ref, z_ref, dproj_in, dglu_ref, dz_ref):
    del dproj_in
    a, b = _f32(glu_ref, slice(None), slice(0, D_SSM)), _f32(glu_ref, slice(None), slice(D_SSM, None))
    sb = _sigmoid(b)
    silu, dsilu = _silu_and_grad(_f32(z_ref))
    dh = _f32(dh_ref)
    dg = dh * silu
    dz_ref[...] = (dh * (a * sb) * dsilu).astype(dz_ref.dtype)
    dglu_ref[:, :D_SSM] = (dg * sb).astype(dglu_ref.dtype)
    dglu_ref[:, D_SSM:] = (dg * a * sb * (1.0 - sb)).astype(dglu_ref.dtype)


def _attn_gate_fwd_body(attn_ref, z_ref, h_ref):
    h_ref[...] = (_f32(attn_ref) * _silu_and_grad(_f32(z_ref))[0]).astype(h_ref.dtype)


def _attn_gate_bwd_body(dh_ref, attn_ref, z_ref, dproj_in, dattn_ref, dz_ref):
    del dproj_in
    silu, dsilu = _silu_and_grad(_f32(z_ref))
    dh = _f32(dh_ref)
    dattn_ref[...] = (dh * silu).astype(dattn_ref.dtype)
    dz_ref[...] = (dh * _f32(attn_ref) * dsilu).astype(dz_ref.dtype)


def _merge_fwd_body(bs_ref, ba_ref, gl_ref, m_ref):
    gs = _sigmoid(_f32(gl_ref, slice(None), slice(0, D_MODEL)))
    ga = _sigmoid(_f32(gl_ref, slice(None), slice(D_MODEL, None)))
    m_ref[...] = (gs * _f32(bs_ref) + ga * _f32(ba_ref)).astype(m_ref.dtype)


def _merge_bwd_body(dm_ref, bs_ref, ba_ref, gl_ref, dbs_ref, dba_ref, dgl_ref):
    gs = _sigmoid(_f32(gl_ref, slice(None), slice(0, D_MODEL)))
    ga = _sigmoid(_f32(gl_ref, slice(None), slice(D_MODEL, None)))
    dm = _f32(dm_ref)
    dbs_ref[...] = (dm * gs).astype(dbs_ref.dtype)
    dba_ref[...] = (dm * ga).astype(dba_ref.dtype)
    dgl_ref[:, :D_MODEL] = (dm * _f32(bs_ref) * gs * (1.0 - gs)).astype(dgl_ref.dtype)
    dgl_ref[:, D_MODEL:] = (dm * _f32(ba_ref) * ga * (1.0 - ga)).astype(dgl_ref.dtype)


def _ln_loss_body(x_ref, o_ref, tgt_ref, gain_ref, bias_ref, dr_ref, drb_ref, loss_ref, dgain_ref, dbias_ref):
    @pl.when(pl.program_id(0) == 0)
    def _():
        loss_ref[...] = jnp.zeros_like(loss_ref)
        dgain_ref[...] = jnp.zeros_like(dgain_ref)
        dbias_ref[...] = jnp.zeros_like(dbias_ref)

    r = ALPHA * x_ref[...] + o_ref[...]
    mu = jnp.mean(r, axis=1, keepdims=True)
    rc = r - mu
    var = jnp.mean(rc * rc, axis=1, keepdims=True)
    rstd = lax.rsqrt(var + LN_EPS)
    xhat = rc * rstd
    gain = gain_ref[...]
    err = xhat * gain + bias_ref[...] - tgt_ref[...]
    loss_ref[...] += 0.5 * jnp.sum(jnp.mean(err * err, axis=1, keepdims=True), axis=0, keepdims=True)
    dy = err * (1.0 / D_MODEL)
    dgain_ref[...] += jnp.sum(dy * xhat, axis=0, keepdims=True)
    dbias_ref[...] += jnp.sum(dy, axis=0, keepdims=True)
    dxhat = dy * gain
    m1 = jnp.mean(dxhat, axis=1, keepdims=True)
    m2 = jnp.mean(dxhat * xhat, axis=1, keepdims=True)
    dr = rstd * (dxhat - m1 - xhat * m2)
    dr_ref[...] = dr
    drb_ref[...] = dr.astype(drb_ref.dtype)


def _place_body(piece_ref, dproj_in, o_ref):
    del dproj_in
    o_ref[...] = piece_ref[...]


def _adamw_update(w_ref, m_ref, v_ref, g, g_ref, d_ref, nm_ref, nv_ref):
    m = ADAM_B1 * m_ref[...] + (1.0 - ADAM_B1) * g
    v = ADAM_B2 * v_ref[...] + (1.0 - ADAM_B2) * (g * g)
    m_hat = m / (1.0 - ADAM_B1 ** ADAM_STEP)
    v_hat = v / (1.0 - ADAM_B2 ** ADAM_STEP)
    g_ref[...] = g
    d_ref[...] = -ADAM_LR * (m_hat / (jnp.sqrt(v_hat) + ADAM_EPS) + ADAM_WD * w_ref[...])
    nm_ref[...] = m
    nv_ref[...] = v


def _adamw_body(*refs, n_parts):
    w_ref, m_ref, v_ref = refs[:3]
    parts = refs[3:3 + n_parts]
    g = parts[0][...].astype(F32)
    for p in parts[1:]:
        g = g + p[...].astype(F32)
    _adamw_update(w_ref, m_ref, v_ref, g, *refs[3 + n_parts:])


def _adamw_shard_body(c_ref, w_ref, m_ref, v_ref, mine_ref, other_ref, g_ref, d_ref, nm_ref, nv_ref, *, nth):
    in_mine = pl.program_id(0) // nth == c_ref[0]
    g = jnp.where(in_mine, mine_ref[...], other_ref[...])
    _adamw_update(w_ref, m_ref, v_ref, g, g_ref, d_ref, nm_ref, nv_ref)


def _adamw_shard(w, m, v, halves, core, *, name, rows):
    shape = w.shape
    w2, m2, v2 = (a.reshape(-1, shape[-1]) for a in (w, m, v))
    r, c = w2.shape
    nth = r // 2 // rows
    assert 2 * nth * rows == r and halves[0].shape == (r // 2, c)
    spec = pl.BlockSpec((rows, c), lambda i, cr: (i, 0))
    half_spec = pl.BlockSpec((rows, c), lambda i, cr: (i % nth, 0))
    outs = pl.pallas_call(
        functools.partial(_adamw_shard_body, nth=nth), name=name,
        grid_spec=pltpu.PrefetchScalarGridSpec(
            num_scalar_prefetch=1, grid=(r // rows,),
            in_specs=[spec] * 3 + [half_spec] * 2, out_specs=[spec] * 4),
        out_shape=[jax.ShapeDtypeStruct((r, c), F32)] * 4,
        compiler_params=_params(("parallel",)),
    )(core, w2, m2, v2, *halves)
    return tuple(o.reshape(shape) for o in outs)


def _adamw(w, m, v, parts, *, name, rows=256):
    shape = w.shape
    w2, m2, v2 = (a.reshape(-1, shape[-1]) for a in (w, m, v))
    parts = [p.reshape(w2.shape) for p in parts]
    r, c = w2.shape
    rows = rows if r % rows == 0 else r
    spec = pl.BlockSpec((rows, c), lambda i: (i, 0))
    outs = pl.pallas_call(
        functools.partial(_adamw_body, n_parts=len(parts)), name=name, grid=(r // rows,),
        in_specs=[spec] * (3 + len(parts)), out_specs=[spec] * 4,
        out_shape=[jax.ShapeDtypeStruct((r, c), F32)] * 4,
        compiler_params=_params(("parallel",)),
    )(w2, m2, v2, *parts)
    return tuple(o.reshape(shape) for o in outs)


BIG = ("w_in", "w_glu", "w_branch_ssm", "w_branch_attn", "w_out")
SHARD_AXIS = dict(w_in=1, w_glu=1, w_branch_ssm=1, w_branch_attn=1, w_out=0)
HBM = pl.BlockSpec(memory_space=pl.ANY)


def _position():
    x, y, c = lax.axis_index("x"), lax.axis_index("y"), lax.axis_index("c")
    other_chips = [(1 - x, y), (x, 1 - y), (1 - x, 1 - y)]
    return x, y, c, other_chips


def _window(ref, axis, shard, n_shards, half=None):
    rows, cols = ref.shape[-2:]
    sel = [slice(None), slice(None)]
    size = ref.shape[-2 + axis] // n_shards
    sel[axis] = pl.ds(pl.multiple_of(shard * size, 128), size)
    if half is not None:
        hsize = ref.shape[-1 - axis] // 2
        sel[1 - axis] = pl.ds(pl.multiple_of(half * hsize, 128), hsize)
    return ref.at[tuple(sel)]


def _half(ref, axis, half):
    hsize = ref.shape[-1 - axis] // 2
    sel = [slice(None), slice(None)]
    sel[1 - axis] = pl.ds(pl.multiple_of(half * hsize, 128), hsize)
    return ref.at[tuple(sel)]


def _remote(src, dst, send_sem, recv_sem, device):
    return pltpu.make_async_remote_copy(src_ref=src, dst_ref=dst, send_sem=send_sem, recv_sem=recv_sem,
                                        device_id=device, device_id_type=MESH)


def _gather_body(*refs, axes):
    n = len(axes)
    shards, outs = refs[:n], refs[n:2 * n]
    send_sems, recv_sems, fsend_sems, frecv_sems, local_sems = refs[2 * n:]
    x, y, c, chips = _position()
    me = 2 * x + y
    local, sends, passes = [], [], []
    for w, ax in enumerate(axes):
        cp = pltpu.make_async_copy(shards[w], _window(outs[w], ax, me, N_CHIPS), local_sems.at[w])
        cp.start()
        local.append(cp)
    for w, ax in enumerate(axes):
        for r, (px, py) in enumerate(chips):
            cp = _remote(_half(shards[w], ax, c), _window(outs[w], ax, me, N_CHIPS, c),
                         send_sems.at[3 * w + r], recv_sems.at[3 * w + r], (px, py, c))
            cp.start()
            sends.append(cp)
    for w, ax in enumerate(axes):
        for r, (px, py) in enumerate(chips):
            landed = _window(outs[w], ax, 2 * px + py, N_CHIPS, c)
            _remote(landed, landed, send_sems.at[3 * w + r], recv_sems.at[3 * w + r], (px, py, c)).wait_recv()
            cp = _remote(landed, landed, fsend_sems.at[3 * w + r], frecv_sems.at[3 * w + r], (x, y, 1 - c))
            cp.start()
            passes.append(cp)
    for w, ax in enumerate(axes):
        for r, (px, py) in enumerate(chips):
            other = _window(outs[w], ax, 2 * px + py, N_CHIPS, 1 - c)
            _remote(other, other, fsend_sems.at[3 * w + r], frecv_sems.at[3 * w + r], (x, y, 1 - c)).wait_recv()
    for cp in sends + passes:
        cp.wait_send()
    for cp in local:
        cp.wait()


def _all_gather(shards):
    axes = tuple(SHARD_AXIS[k] for k in BIG)
    out_shape = []
    for k in BIG:
        s = list(shards[k].shape)
        s[SHARD_AXIS[k]] *= N_CHIPS
        out_shape.append(jax.ShapeDtypeStruct(tuple(s), shards[k].dtype))
    n = len(BIG)
    outs = pl.pallas_call(
        functools.partial(_gather_body, axes=axes), name="gather_weights",
        in_specs=[HBM] * n, out_specs=[HBM] * n, out_shape=out_shape,
        scratch_shapes=[pltpu.SemaphoreType.DMA((3 * n,))] * 4 + [pltpu.SemaphoreType.DMA((n,))],
    )(*[shards[k] for k in BIG])
    return dict(zip(BIG, outs))


def _half_tile(n, h, nt, axis):
    return h * nt + n if axis == 1 else 2 * n + h


def _sibling_stream(n, nt, stage, recv, send_sems, recv_sems, credit, produce, consume):
    x, y, c, _ = _position()
    sibling = (x, y, 1 - c)

    def copy(slot):
        return _remote(stage.at[slot], recv.at[slot], send_sems.at[slot], recv_sems.at[slot], sibling)

    @pl.when(n < nt)
    def _():
        slot = n % 2

        @pl.when(n >= 2)
        def _():
            copy(slot).wait_send()
            pl.semaphore_wait(credit, 1)

        stage[slot] = produce().astype(stage.dtype)
        copy(slot).start()

    @pl.when(n >= 1)
    def _():
        slot = (n - 1) % 2
        copy(slot).wait_recv()
        consume(recv[slot])

        @pl.when(n - 1 < nt - 2)
        def _():
            pl.semaphore_signal(credit, 1, device_id=sibling, device_id_type=MESH)

    @pl.when(n == nt)
    def _():
        for slot in range(min(2, nt)):
            copy(slot).wait_send()


def _pair_reduce_body(c_ref, mine_ref, theirs_ref, out_ref, stage, recv, send_sems, recv_sems, credit, *, nt):
    del c_ref

    def consume(got):
        out_ref[...] = (mine_ref[...] + got.astype(F32)).astype(out_ref.dtype)

    _sibling_stream(pl.program_id(0), nt, stage, recv, send_sems, recv_sems, credit,
                    lambda: theirs_ref[...], consume)


def _pair_reduce(grad, core, axis, *, name, rows):
    r, c = grad.shape
    nt = r // 2 // rows
    assert nt * rows * 2 == r and (axis == 1 or rows == r // (2 * N_CHIPS))
    tile = lambda n, h: _half_tile(n, h, nt, axis)
    return pl.pallas_call(
        functools.partial(_pair_reduce_body, nt=nt), name=name,
        grid_spec=pltpu.PrefetchScalarGridSpec(
            num_scalar_prefetch=1, grid=(nt + 1,),
            in_specs=[pl.BlockSpec((rows, c), lambda n, cr: (tile(jnp.maximum(n - 1, 0), cr[0]), 0)),
                      pl.BlockSpec((rows, c), lambda n, cr: (tile(jnp.minimum(n, nt - 1), 1 - cr[0]), 0))],
            out_specs=pl.BlockSpec((rows, c), lambda n, cr: (jnp.maximum(n - 1, 0), 0)),
            scratch_shapes=[pltpu.VMEM((2, rows, c), BF16), pltpu.VMEM((2, rows, c), BF16),
                            pltpu.SemaphoreType.DMA((2,)), pltpu.SemaphoreType.DMA((2,)),
                            pltpu.SemaphoreType.REGULAR]),
        out_shape=jax.ShapeDtypeStruct((r // 2, c), BF16),
        compiler_params=_params(("arbitrary",)),
    )(core, grad, grad)


def _chip_add_share_body(c_ref, s0, s1, s2, s3, mine_ref, other_ref, stage, recv, send_sems, recv_sems, credit, *, nt):
    del c_ref

    def produce():
        total = s3[...].astype(F32) + s0[...].astype(F32) + s1[...].astype(F32) + s2[...].astype(F32)
        mine_ref[...] = total
        return total

    def consume(got):
        other_ref[...] = got

    _sibling_stream(pl.program_id(0), nt, stage, recv, send_sems, recv_sems, credit, produce, consume)


def _chip_add_share(slots, core, *, name, rows):
    _, r, c = slots.shape
    nt = r // rows
    assert nt * rows == r
    send = lambda j: pl.BlockSpec((None, rows, c), lambda n, cr: (j, jnp.minimum(n, nt - 1), 0))
    return pl.pallas_call(
        functools.partial(_chip_add_share_body, nt=nt), name=name,
        grid_spec=pltpu.PrefetchScalarGridSpec(
            num_scalar_prefetch=1, grid=(nt + 1,),
            in_specs=[send(j) for j in range(4)],
            out_specs=[pl.BlockSpec((rows, c), lambda n, cr: (jnp.minimum(n, nt - 1), 0)),
                       pl.BlockSpec((rows, c), lambda n, cr: (jnp.maximum(n - 1, 0), 0))],
            scratch_shapes=[pltpu.VMEM((2, rows, c), F32), pltpu.VMEM((2, rows, c), F32),
                            pltpu.SemaphoreType.DMA((2,)), pltpu.SemaphoreType.DMA((2,)),
                            pltpu.SemaphoreType.REGULAR]),
        out_shape=[jax.ShapeDtypeStruct((r, c), F32)] * 2,
        compiler_params=_params(("arbitrary",)),
    )(core, slots, slots, slots, slots)


def _chip_exchange_body(*refs, axes):
    n = len(axes)
    sums, slots = refs[:n], refs[n:2 * n]
    send_sems, recv_sems, local_sems = refs[2 * n:]
    x, y, c, chips = _position()
    me = 2 * x + y
    copies, local = [], []
    for w, ax in enumerate(axes):
        for r, (px, py) in enumerate(chips):
            cp = _remote(_window(sums[w], ax, 2 * px + py, N_CHIPS), slots[w].at[r],
                         send_sems.at[3 * w + r], recv_sems.at[3 * w + r], (px, py, c))
            cp.start()
            copies.append(cp)
        lc = pltpu.make_async_copy(_window(sums[w], ax, me, N_CHIPS), slots[w].at[3], local_sems.at[w])
        lc.start()
        local.append(lc)
    for cp in copies:
        cp.wait_recv()
    for cp in copies:
        cp.wait_send()
    for lc in local:
        lc.wait()


def _comm_call(body, name, ins, out_shape, n_sem_arrays, n_sems):
    axes = tuple(SHARD_AXIS[k] for k in BIG)
    return pl.pallas_call(
        functools.partial(body, axes=axes), name=name,
        in_specs=[HBM] * len(ins), out_specs=[HBM] * len(out_shape), out_shape=out_shape,
        scratch_shapes=[pltpu.SemaphoreType.DMA((n_sems,))] * n_sem_arrays + [pltpu.SemaphoreType.DMA((len(BIG),))],
    )(*ins)


PAIR_ROWS = dict(w_in=64, w_glu=128, w_branch_ssm=128, w_branch_attn=128, w_out=256)
SHARE_ROWS = dict(w_in=128, w_glu=128, w_branch_ssm=128, w_branch_attn=128, w_out=64)


def _reduce_grads(grads):
    n = len(BIG)
    core = lax.axis_index("c").astype(jnp.int32).reshape(1)
    pair = [_pair_reduce(grads[k], core, SHARD_AXIS[k], name="pair_reduce_" + k, rows=PAIR_ROWS[k]) for k in BIG]
    slot_shapes = []
    for k, p in zip(BIG, pair):
        s = list(p.shape)
        s[SHARD_AXIS[k]] //= N_CHIPS
        slot_shapes.append(jax.ShapeDtypeStruct((4, *s), BF16))
    slots = _comm_call(_chip_exchange_body, "chip_exchange", pair, slot_shapes, 2, 3 * n)
    return {k: _chip_add_share(slots[w], core, name="chip_add_share_" + k, rows=SHARE_ROWS[k])
            for w, k in enumerate(BIG)}


SMALL = (("ssm_lambda_re", (1, 64, 64)), ("ssm_lambda_im", (1, 64, 64)), ("ssm_b_re", (1, 64, 64, 16)),
         ("ssm_b_im", (1, 64, 64, 16)), ("ssm_c_re", (1, 64, 16, 64)), ("ssm_c_im", (1, 64, 16, 64)),
         ("ssm_d", (1, 1024)), ("ssm_log_step", (1, 64)), ("attn_sinks", (1, 16)), ("rel_bias_table", (32, 16)),
         ("ln_gain", (1, 2048)), ("ln_bias", (1, 2048)))
SMALL_SIZE = sum(int(np.prod(s)) for _, s in SMALL)
PACK_ROWS = -(-(SMALL_SIZE + 1) // (8 * 128)) * 8


def _pack(values, extra=None):
    flat = [values[k].reshape(-1).astype(F32) for k, _ in SMALL]
    flat.append(jnp.zeros((1,), F32) if extra is None else extra.reshape(1))
    flat.append(jnp.zeros((PACK_ROWS * 128 - SMALL_SIZE - 1,), F32))
    return jnp.concatenate(flat).reshape(PACK_ROWS, 128)


def _unpack(packed):
    flat = packed.reshape(-1)
    out, off = {}, 0
    for k, s in SMALL:
        size = int(np.prod(s))
        out[k] = flat[off:off + size].reshape(s)
        off += size
    return out, flat[off]


def _small_allreduce_body(p_ref, o_ref, buf, send_sems, recv_sems):
    x, y, c, _ = _position()
    me = 4 * x + 2 * y + c
    buf[0] = p_ref[...]
    copies = []
    for r in range(1, 8):
        peer = tuple(1 - v if (r >> s) & 1 else v for v, s in ((x, 2), (y, 1), (c, 0)))
        cp = _remote(p_ref, buf.at[r], send_sems.at[r - 1], recv_sems.at[r - 1], peer)
        cp.start()
        copies.append(cp)
    for cp in copies:
        cp.wait_recv()
    for cp in copies:
        cp.wait_send()
    acc = buf[me]
    for s in range(1, 8):
        acc = acc + buf[jnp.bitwise_xor(me, s)]
    o_ref[...] = acc


def _small_allreduce(packed):
    vm = pl.BlockSpec(memory_space=pltpu.VMEM)
    return pl.pallas_call(
        functools.partial(_small_allreduce_body), name="small_allreduce",
        in_specs=[vm], out_specs=vm, out_shape=jax.ShapeDtypeStruct(packed.shape, F32),
        scratch_shapes=[pltpu.VMEM((8,) + packed.shape, F32), pltpu.SemaphoreType.DMA((7,)),
                        pltpu.SemaphoreType.DMA((7,))],
        compiler_params=pltpu.CompilerParams(vmem_limit_bytes=VMEM_LIMIT),
    )(packed)


def _local_step(x, target, w_in, w_glu, w_bs, w_ba, w_out, lam_re, lam_im, b_re, b_im, c_re, c_im, d_skip,
                log_step, sinks, rel_bias_table, ln_gain, ln_bias):
    t = x.shape[0]
    bucket = jnp.asarray(_bucket_table())
    p_re, p_im, b_bd, c_bd = _ssm_prepare(lam_re, lam_im, b_re, b_im, c_re, c_im, log_step)
    bias = _bias_build(rel_bias_table, bucket)

    xb, xbt = _cast_and_transpose(x, name="cast_x")
    act = dict(out_dtype=BF16)
    proj = _matmul(xb, w_in, "nn", name="proj", tm=2048, tn=512, **act)
    y_ssm, g_in, car_re, car_im = _ssm_forward(proj, p_re, p_im, b_bd, c_bd, d_skip)
    glu = _matmul(g_in, w_glu, "nn", name="glu", tm=1024, tn=2048, **act)
    (h_ssm,) = _rowwise(functools.partial(_ssm_gate_fwd_body), "ssm_gate_fwd", t,
                        [(glu, 2 * D_SSM, 0), (proj, D_SSM, C_ZS)], [((t, D_SSM), BF16, D_SSM, 0)])
    attn, lse = _attn_forward(proj, sinks, bias)
    (h_attn,) = _rowwise(functools.partial(_attn_gate_fwd_body), "attn_gate_fwd", t,
                         [(attn, D_ATTN, 0), (proj, D_ATTN, C_ZA)], [((t, D_ATTN), BF16, D_ATTN, 0)])
    bs = _matmul(h_ssm, w_bs, "nn", name="branch_ssm", tm=1024, tn=2048, **act)
    ba = _matmul(h_attn, w_ba, "nn", name="branch_attn", tm=1024, tn=2048, **act)
    gl_in = (proj, 2 * D_MODEL, C_GL)
    (merged,) = _rowwise(functools.partial(_merge_fwd_body), "merge_fwd", t,
                         [(bs, D_MODEL, 0), (ba, D_MODEL, 0), gl_in], [((t, D_MODEL), BF16, D_MODEL, 0)])
    out = _matmul(merged, w_out, "nn", name="out_proj", tm=1024, tn=1024)
    dr, drb, loss, g_gain, g_bias = _rowwise(
        functools.partial(_ln_loss_body), "ln_loss", t,
        [(x, D_MODEL, 0), (out, D_MODEL, 0), (target, D_MODEL, 0), (ln_gain, None, 0), (ln_bias, None, 0)],
        [((t, D_MODEL), F32, D_MODEL, 0), ((t, D_MODEL), BF16, D_MODEL, 0), ((1, 1), F32, None, 0),
         ((1, D_MODEL), F32, None, 0), ((1, D_MODEL), F32, None, 0)])

    g_w_out = _matmul(merged, drb, "tn", name="g_w_out", tm=1024, tn=512, tk=1024)
    d_merged = _matmul(drb, w_out, "nt", name="d_merged", tm=1024, tn=1024, **act)
    d_bs, d_ba, dproj = _rowwise(
        functools.partial(_merge_bwd_body), "merge_bwd", t,
        [(d_merged, D_MODEL, 0), (bs, D_MODEL, 0), (ba, D_MODEL, 0), gl_in],
        [((t, D_MODEL), BF16, D_MODEL, 0), ((t, D_MODEL), BF16, D_MODEL, 0),
         ((t, D_IN), BF16, 2 * D_MODEL, C_GL)])
    g_w_bs = _matmul(h_ssm, d_bs, "tn", name="g_w_branch_ssm", tm=1024, tn=512, tk=1024)
    d_h_ssm = _matmul(d_bs, w_bs, "nt", name="d_h_ssm", tm=1024, tn=1024, **act)
    g_w_ba = _matmul(h_attn, d_ba, "tn", name="g_w_branch_attn", tm=1024, tn=512, tk=1024)
    d_h_attn = _matmul(d_ba, w_ba, "nt", name="d_h_attn", tm=1024, tn=1024, **act)

    d_attn, dproj = _rowwise(
        functools.partial(_attn_gate_bwd_body), "attn_gate_bwd", t,
        [(d_h_attn, D_ATTN, 0), (attn, D_ATTN, 0), (proj, D_ATTN, C_ZA), (dproj, None, "any")],
        [((t, D_ATTN), BF16, D_ATTN, 0), ((t, D_IN), BF16, D_ATTN, C_ZA)], aliases={3: 1})
    dproj, d_kv, g_bias_full, g_sinks = _attn_backward(proj, sinks, bias, d_attn, attn, lse, dproj)
    (dproj,) = _rowwise(functools.partial(_place_body), "place_dkv", t,
                        [(d_kv, 2 * D_KV, 0), (dproj, None, "any")],
                        [((t, D_IN), BF16, 2 * D_KV, C_K)], aliases={1: 0})
    g_table = _bias_grad(g_bias_full, bucket)

    d_glu, dproj = _rowwise(
        functools.partial(_ssm_gate_bwd_body), "ssm_gate_bwd", t,
        [(d_h_ssm, D_SSM, 0), (glu, 2 * D_SSM, 0), (proj, D_SSM, C_ZS), (dproj, None, "any")],
        [((t, 2 * D_SSM), BF16, 2 * D_SSM, 0), ((t, D_IN), BF16, D_SSM, C_ZS)], aliases={3: 1})
    g_w_glu = _matmul(g_in, d_glu, "tn", name="g_w_glu", tm=1024, tn=512, tk=1024)
    d_g_in = _matmul(d_glu, w_glu, "nt", name="d_g_in", tm=1024, tn=1024, **act)
    dproj, g_bbd, g_cbd, g_lam_re, g_lam_im, g_d = _ssm_backward(
        proj, d_g_in, y_ssm, car_re, car_im, p_re, p_im, b_bd, c_bd, d_skip, dproj)
    g_lr, g_li, g_br, g_bi, g_cr, g_ci, g_ls = _ssm_param_grads(
        lam_re, lam_im, b_re, b_im, log_step, g_lam_re, g_lam_im, g_bbd, g_cbd)

    g_w_in = _matmul(xbt, dproj, "nn", name="g_w_in", tm=512, tn=512)
    grad_x = _matmul(dproj, w_in, "nt", name="grad_x", tm=512, tn=512, res=dr, res_scale=ALPHA)

    big = dict(w_in=g_w_in, w_glu=g_w_glu, w_branch_ssm=g_w_bs, w_branch_attn=g_w_ba, w_out=g_w_out)
    small = dict(ssm_lambda_re=g_lr, ssm_lambda_im=g_li, ssm_b_re=g_br, ssm_b_im=g_bi, ssm_c_re=g_cr,
                 ssm_c_im=g_ci, ssm_d=g_d, ssm_log_step=g_ls, attn_sinks=g_sinks[:, :N_Q_HEADS],
                 rel_bias_table=g_table, ln_gain=g_gain, ln_bias=g_bias)
    return loss, grad_x, big, small


WEIGHTS = ("w_in", "ssm_lambda_re", "ssm_lambda_im", "ssm_b_re", "ssm_b_im", "ssm_c_re", "ssm_c_im", "ssm_d",
           "ssm_log_step", "w_glu", "attn_sinks", "rel_bias_table", "w_branch_ssm", "w_branch_attn", "w_out",
           "ln_gain", "ln_bias")


def kernel(x, w_in, ssm_lambda_re, ssm_lambda_im, ssm_b_re, ssm_b_im, ssm_c_re, ssm_c_im, ssm_d, ssm_log_step, w_glu, attn_sinks, rel_bias_table, w_branch_ssm, w_branch_attn, w_out, ln_gain, ln_bias, loss_target, m_w_in, m_ssm_lambda_re, m_ssm_lambda_im, m_ssm_b_re, m_ssm_b_im, m_ssm_c_re, m_ssm_c_im, m_ssm_d, m_ssm_log_step, m_w_glu, m_attn_sinks, m_rel_bias_table, m_w_branch_ssm, m_w_branch_attn, m_w_out, m_ln_gain, m_ln_bias, v_w_in, v_ssm_lambda_re, v_ssm_lambda_im, v_ssm_b_re, v_ssm_b_im, v_ssm_c_re, v_ssm_c_im, v_ssm_d, v_ssm_log_step, v_w_glu, v_attn_sinks, v_rel_bias_table, v_w_branch_ssm, v_w_branch_attn, v_w_out, v_ln_gain, v_ln_bias):
    given = dict(locals())
    w = {k: given[k] for k in WEIGHTS}
    m = {k: given["m_" + k] for k in WEIGHTS}
    v = {k: given["v_" + k] for k in WEIGHTS}

    shards = {k: _cast(w[k][0], BF16, name="cast_" + k) for k in BIG}
    full = _all_gather(shards)
    loss, grad_x, g_big, g_small = _local_step(
        x[0], loss_target[0], full["w_in"], full["w_glu"], full["w_branch_ssm"], full["w_branch_attn"],
        full["w_out"], ssm_lambda_re[0], ssm_lambda_im[0], ssm_b_re[0], ssm_b_im[0], ssm_c_re[0], ssm_c_im[0],
        ssm_d, ssm_log_step, attn_sinks, rel_bias_table, ln_gain, ln_bias)

    g_shard = _reduce_grads(g_big)
    g_packed = _small_allreduce(_pack(g_small, loss))
    loss_sum = _unpack(g_packed)[1]

    grad, delta, new_m, new_v = {}, {}, {}, {}
    core = lax.axis_index("c").astype(jnp.int32).reshape(1)
    for k in BIG:
        grad[k], delta[k], new_m[k], new_v[k] = _adamw_shard(w[k], m[k], v[k], g_shard[k], core,
                                                             name="adamw_" + k, rows=SHARE_ROWS[k])
    gs, ds, ms, vs = _adamw(_pack(w), _pack(m), _pack(v), [g_packed], name="adamw_small")
    for dst, packed in ((grad, gs), (delta, ds), (new_m, ms), (new_v, vs)):
        dst.update(_unpack(packed)[0])

    return (loss_sum, grad_x[None], *[grad[k] for k in WEIGHTS], *[delta[k] for k in WEIGHTS],
            *[new_m[k] for k in WEIGHTS], *[new_v[k] for k in WEIGHTS])
```

```python
import functools
import math

import numpy as np
import jax
import jax.numpy as jnp
from jax import lax
from jax.experimental import pallas as pl
from jax.experimental.pallas import tpu as pltpu

F32 = jnp.float32
BF16 = jnp.bfloat16

D_MODEL = 2048
D_SSM = 1024
SSM_GROUP = 16
N_GROUPS = 64
SSM_STATE = 64
N_LANES = N_GROUPS * SSM_STATE
N_Q_HEADS = 16
N_KV_HEADS = 4
HEAD_DIM = 64
Q_PER_KV = 4
D_ATTN = 1024
D_KV = 256
WINDOW = 128
BLOCK = 128
N_BUCKETS = 32
MAX_DISTANCE = 128
D_IN = 8704
ALPHA = 2.0 ** 0.25
LN_EPS = 1e-5
NEG_INF = -1e30
ATTN_SCALE = HEAD_DIM ** -0.5

C_U, C_ZS, C_Q, C_K, C_V, C_ZA, C_GL = 0, 1024, 2048, 3072, 3328, 3584, 4608

ADAM_LR = 0.001
ADAM_B1 = 0.9
ADAM_B2 = 0.999
ADAM_EPS = 1e-08
ADAM_WD = 0.01
ADAM_STEP = 10

N_CHIPS = 4
MESH = pl.DeviceIdType.MESH

SSM_CHUNK = 256
SEG_LEN = SSM_CHUNK // 8
SLAB_LANES = 512
N_SLABS = N_LANES // SLAB_LANES
SLAB_CH = D_SSM // N_SLABS

VMEM_LIMIT = 60 * 1024 * 1024


def _params(sem=None, **kw):
    return pltpu.CompilerParams(dimension_semantics=sem, vmem_limit_bytes=VMEM_LIMIT, **kw)


_DIMS = {"nn": (((1,), (0,)), ((), ())), "nt": (((1,), (1,)), ((), ())), "tn": (((0,), (0,)), ((), ()))}


class _Exchange:
    def __init__(self, ins, out_shape, n_sems, copies):
        self.ins, self.out_shape, self.n_sems, self.copies = list(ins), list(out_shape), list(n_sems), copies

    def start(self, ins, outs, sems):
        remote, local = self.copies(ins, outs, sems)
        for cp in remote + local:
            cp.start()

    def finish(self, ins, outs, sems):
        remote, local = self.copies(ins, outs, sems)
        for cp in remote:
            cp.wait_recv()
        for cp in remote:
            cp.wait_send()
        for cp in local:
            cp.wait()


def _mm_body(*refs, dims, nk, res_scale, exchange, grid):
    n_ex_in = len(exchange.ins) if exchange else 0
    n_ex_out = len(exchange.out_shape) if exchange else 0
    n_in = 2 + (res_scale is not None) + n_ex_in
    a_ref, b_ref = refs[0], refs[1]
    r_ref = refs[2] if res_scale is not None else None
    ex_in = refs[n_in - n_ex_in:n_in]
    o_ref, ex_out = refs[n_in], refs[n_in + 1:n_in + 1 + n_ex_out]
    acc_ref, ex_sems = refs[n_in + 1 + n_ex_out], refs[n_in + 2 + n_ex_out:]
    i, j, k = pl.program_id(0), pl.program_id(1), pl.program_id(2)

    if exchange:
        @pl.when(jnp.logical_and(jnp.logical_and(i == 0, j == 0), k == 0))
        def _():
            exchange.start(ex_in, ex_out, ex_sems)

    part = lax.dot_general(a_ref[...].astype(BF16), b_ref[...].astype(BF16), _DIMS[dims],
                           preferred_element_type=F32)

    def finish(acc):
        if r_ref is not None:
            acc = acc + res_scale * r_ref[...]
        o_ref[...] = acc.astype(o_ref.dtype)

    if nk == 1:
        finish(part)
    else:
        @pl.when(k == 0)
        def _():
            acc_ref[...] = part

        @pl.when(k > 0)
        def _():
            acc_ref[...] += part

        @pl.when(k == nk - 1)
        def _():
            finish(acc_ref[...])

    if exchange:
        @pl.when(jnp.logical_and(jnp.logical_and(i == grid[0] - 1, j == grid[1] - 1), k == grid[2] - 1))
        def _():
            exchange.finish(ex_in, ex_out, ex_sems)


def _matmul(a, b, dims, *, name, out_dtype=F32, tm=512, tn=512, tk=None, res=None, res_scale=None, exchange=None):
    if dims == "nn":
        (m, kk), n = a.shape, b.shape[1]
    elif dims == "nt":
        (m, kk), n = a.shape, b.shape[0]
    else:
        (kk, m), n = a.shape, b.shape[1]
    tm, tn = min(tm, m), min(tn, n)
    tk = kk if tk is None else min(tk, kk)
    assert m % tm == 0 and n % tn == 0 and kk % tk == 0, (name, m, n, kk, tm, tn, tk)
    nk = kk // tk
    a_spec = {"nn": pl.BlockSpec((tm, tk), lambda i, j, k: (i, k)),
              "nt": pl.BlockSpec((tm, tk), lambda i, j, k: (i, k)),
              "tn": pl.BlockSpec((tk, tm), lambda i, j, k: (k, i))}[dims]
    b_spec = {"nn": pl.BlockSpec((tk, tn), lambda i, j, k: (k, j)),
              "nt": pl.BlockSpec((tn, tk), lambda i, j, k: (j, k)),
              "tn": pl.BlockSpec((tk, tn), lambda i, j, k: (k, j))}[dims]
    in_specs, args = [a_spec, b_spec], [a, b]
    if res is not None:
        in_specs.append(pl.BlockSpec((tm, tn), lambda i, j, k: (i, j)))
        args.append(res)
    grid = (m // tm, n // tn, nk)
    out_specs = [pl.BlockSpec((tm, tn), lambda i, j, k: (i, j))]
    out_shape = [jax.ShapeDtypeStruct((m, n), out_dtype)]
    scratch = [pltpu.VMEM((tm, tn), F32)]
    if exchange:
        in_specs += [pl.BlockSpec(memory_space=pl.ANY)] * len(exchange.ins)
        args += exchange.ins
        out_specs += [pl.BlockSpec(memory_space=pl.ANY)] * len(exchange.out_shape)
        out_shape += exchange.out_shape
        scratch += [pltpu.SemaphoreType.DMA((s,)) for s in exchange.n_sems]
    outs = pl.pallas_call(
        functools.partial(_mm_body, dims=dims, nk=nk, res_scale=res_scale if res is not None else None,
                          exchange=exchange, grid=grid),
        name=name, grid=grid, in_specs=in_specs, out_specs=out_specs, out_shape=out_shape, scratch_shapes=scratch,
        compiler_params=_params(("arbitrary",) * 3 if exchange else ("parallel", "parallel", "arbitrary")),
    )(*args)
    return (outs[0], outs[1:]) if exchange else outs[0]


def _sigmoid(v):
    return 1.0 / (1.0 + jnp.exp(-v))


def _silu_and_grad(z):
    s = _sigmoid(z)
    return z * s, s * (1.0 + z * (1.0 - s))


def _cast_body(x_ref, o_ref):
    o_ref[...] = x_ref[...].astype(o_ref.dtype)


def _cast_and_transpose_body(x_ref, o_ref, ot_ref):
    xb = x_ref[...].astype(BF16)
    o_ref[...] = xb
    ot_ref[...] = xb.T


def _cast_and_transpose(x, *, name, rows=512):
    m, n = x.shape
    rows = min(rows, m)
    return pl.pallas_call(
        functools.partial(_cast_and_transpose_body), name=name, grid=(m // rows,),
        in_specs=[pl.BlockSpec((rows, n), lambda i: (i, 0))],
        out_specs=[pl.BlockSpec((rows, n), lambda i: (i, 0)), pl.BlockSpec((n, rows), lambda i: (0, i))],
        out_shape=[jax.ShapeDtypeStruct((m, n), BF16), jax.ShapeDtypeStruct((n, m), BF16)],
        compiler_params=_params(("parallel",)),
    )(x)


def _cast(x, dtype, *, name, rows=512):
    m, n = x.shape
    rows = min(rows, m)
    return pl.pallas_call(
        functools.partial(_cast_body), name=name, grid=(m // rows,),
        in_specs=[pl.BlockSpec((rows, n), lambda i: (i, 0))],
        out_specs=pl.BlockSpec((rows, n), lambda i: (i, 0)),
        out_shape=jax.ShapeDtypeStruct((m, n), dtype),
        compiler_params=_params(("parallel",)),
    )(x)


def _lam_bar(lr, li, ls):
    step = jnp.exp(ls)
    er = jnp.exp(lr * step)
    return step, er * jnp.cos(li * step), er * jnp.sin(li * step)


def _ssm_pow_body(lr_ref, li_ref, ls_ref, pr_ref, pi_ref):
    _, ar, ai = _lam_bar(lr_ref[...], li_ref[...], ls_ref[...])
    cr, ci = ar, ai
    for i in range(SEG_LEN):
        pr_ref[pl.ds(i, 1), :] = cr
        pi_ref[pl.ds(i, 1), :] = ci
        cr, ci = cr * ar - ci * ai, cr * ai + ci * ar


def _ssm_bbar_body(lr_ref, li_ref, ls_ref, br_ref, bi_ref, or_ref, oi_ref):
    lr, li = lr_ref[...], li_ref[...]
    _, ar, ai = _lam_bar(lr, li, ls_ref[...])
    d = lr * lr + li * li
    ir, ii = lr / d, -li / d
    nr, ni = ar - 1.0, ai
    cr, ci = nr * ir - ni * ii, nr * ii + ni * ir
    br, bi = br_ref[...], bi_ref[...]
    or_ref[...] = cr * br - ci * bi
    oi_ref[...] = cr * bi + ci * br


def _ssm_param_bwd_body(lr_ref, li_ref, ls_ref, br_ref, bi_ref, glr_ref, gli_ref, gbr_ref, gbi_ref,
                        dlr_ref, dli_ref, dls_ref, dbr_ref, dbi_ref):
    lr, li = lr_ref[...], li_ref[...]
    step, ar, ai = _lam_bar(lr, li, ls_ref[...])
    d = lr * lr + li * li
    ir, ii = lr / d, -li / d
    nr, ni = ar - 1.0, ai
    cr, ci = nr * ir - ni * ii, nr * ii + ni * ir
    br, bi, gbr, gbi = br_ref[...], bi_ref[...], gbr_ref[...], gbi_ref[...]
    dbr_ref[...] = cr * gbr + ci * gbi
    dbi_ref[...] = cr * gbi - ci * gbr
    gcr = jnp.sum(br * gbr + bi * gbi, axis=1, keepdims=True)
    gci = jnp.sum(br * gbi - bi * gbr, axis=1, keepdims=True)
    gnr, gni = ir * gcr + ii * gci, ir * gci - ii * gcr
    gir, gii = nr * gcr + ni * gci, nr * gci - ni * gcr
    gtr, gti = glr_ref[...] + gnr, gli_ref[...] + gni
    i2r, i2i = ir * ir - ii * ii, 2.0 * ir * ii
    g1r, g1i = -(i2r * gir + i2i * gii), -(i2r * gii - i2i * gir)
    g2r, g2i = step * (ar * gtr + ai * gti), step * (ar * gti - ai * gtr)
    mr, mi = lr * ar - li * ai, lr * ai + li * ar
    dlr_ref[...] = g1r + g2r
    dli_ref[...] = g1i + g2i
    dls_ref[...] = (mr * gtr + mi * gti) * step


def _whole(shape):
    return pl.BlockSpec(shape, lambda *_: (0,) * len(shape))


def _ssm_prepare(lam_re, lam_im, b_re, b_im, c_re, c_im, log_step):
    row = lambda a: a.reshape(1, N_LANES)
    col = lambda a: a.reshape(N_LANES, 1)
    ls = jnp.repeat(log_step.reshape(N_GROUPS), SSM_STATE)
    p_re, p_im = pl.pallas_call(
        functools.partial(_ssm_pow_body), name="ssm_pow",
        in_specs=[_whole((1, N_LANES))] * 3, out_specs=[_whole((SEG_LEN, N_LANES))] * 2,
        out_shape=[jax.ShapeDtypeStruct((SEG_LEN, N_LANES), F32)] * 2, grid=(1,),
    )(row(lam_re), row(lam_im), row(ls))
    bb_re, bb_im = pl.pallas_call(
        functools.partial(_ssm_bbar_body), name="ssm_bbar",
        in_specs=[_whole((N_LANES, 1))] * 3 + [_whole((N_LANES, SSM_GROUP))] * 2,
        out_specs=[_whole((N_LANES, SSM_GROUP))] * 2,
        out_shape=[jax.ShapeDtypeStruct((N_LANES, SSM_GROUP), F32)] * 2, grid=(1,),
    )(col(lam_re), col(lam_im), col(ls), b_re.reshape(N_LANES, SSM_GROUP), b_im.reshape(N_LANES, SSM_GROUP))
    eye = jnp.eye(8, dtype=F32)

    def b_slabs(bb):
        t = bb.reshape(N_SLABS, 8, SSM_STATE, SSM_GROUP).transpose(0, 1, 3, 2)
        return (t[:, :, :, None, :] * eye[None, :, None, :, None]).reshape(N_SLABS, SLAB_CH, SLAB_LANES)

    def c_slabs(c):
        t = c.reshape(N_SLABS, 8, SSM_GROUP, SSM_STATE).transpose(0, 1, 3, 2)
        return (t[:, :, :, None, :] * eye[None, :, None, :, None]).reshape(N_SLABS, SLAB_LANES, SLAB_CH)

    b_bd = jnp.concatenate([b_slabs(bb_re), b_slabs(bb_im)], axis=2).astype(BF16)
    c_bd = jnp.concatenate([c_slabs(c_re.reshape(N_GROUPS, SSM_GROUP, SSM_STATE)),
                            -c_slabs(c_im.reshape(N_GROUPS, SSM_GROUP, SSM_STATE))], axis=1).astype(BF16)
    return p_re, p_im, b_bd, c_bd


def _diag_blocks_b(g):
    t = g.reshape(N_SLABS, 8, SSM_GROUP, 8, SSM_STATE)
    t = jnp.stack([t[:, i, :, i, :] for i in range(8)], axis=1)
    return t.transpose(0, 1, 3, 2).reshape(N_LANES, SSM_GROUP)


def _diag_blocks_c(g):
    t = g.reshape(N_SLABS, 8, SSM_STATE, 8, SSM_GROUP)
    t = jnp.stack([t[:, i, :, i, :] for i in range(8)], axis=1)
    return t.transpose(0, 1, 3, 2).reshape(N_GROUPS, SSM_GROUP, SSM_STATE)


def _ssm_param_grads(lam_re, lam_im, b_re, b_im, log_step, g_lam_re, g_lam_im, g_bbd, g_cbd):
    col = lambda a: a.reshape(N_LANES, 1)
    ls = jnp.repeat(log_step.reshape(N_GROUPS), SSM_STATE)
    gbr = _diag_blocks_b(g_bbd[:, :, :SLAB_LANES])
    gbi = _diag_blocks_b(g_bbd[:, :, SLAB_LANES:])
    outs = pl.pallas_call(
        functools.partial(_ssm_param_bwd_body), name="ssm_param_bwd", grid=(1,),
        in_specs=[_whole((N_LANES, 1))] * 3 + [_whole((N_LANES, SSM_GROUP))] * 2 + [_whole((N_LANES, 1))] * 2
        + [_whole((N_LANES, SSM_GROUP))] * 2,
        out_specs=[_whole((N_LANES, 1))] * 3 + [_whole((N_LANES, SSM_GROUP))] * 2,
        out_shape=[jax.ShapeDtypeStruct((N_LANES, 1), F32)] * 3 + [jax.ShapeDtypeStruct((N_LANES, SSM_GROUP), F32)] * 2,
    )(col(lam_re), col(lam_im), col(ls), b_re.reshape(N_LANES, SSM_GROUP), b_im.reshape(N_LANES, SSM_GROUP),
      col(g_lam_re), col(g_lam_im), gbr, gbi)
    dlr, dli, dls, dbr, dbi = outs
    d_c_re = _diag_blocks_c(g_cbd[:, :SLAB_LANES, :])
    d_c_im = -_diag_blocks_c(g_cbd[:, SLAB_LANES:, :])
    return (dlr.reshape(1, N_GROUPS, SSM_STATE), dli.reshape(1, N_GROUPS, SSM_STATE),
            dbr.reshape(1, N_GROUPS, SSM_STATE, SSM_GROUP), dbi.reshape(1, N_GROUPS, SSM_STATE, SSM_GROUP),
            d_c_re[None], d_c_im[None], dls.reshape(N_GROUPS, SSM_STATE).sum(axis=1).reshape(1, N_GROUPS))


def _bcast8(v):
    return jnp.broadcast_to(v, (8, v.shape[1]))


def _segment_permutation():
    p = np.zeros((SSM_CHUNK, SSM_CHUNK), np.float32)
    rows = np.arange(SSM_CHUNK)
    p[rows, (rows % 8) * SEG_LEN + rows // 8] = 1.0
    return p


def _permute_exact(perm, val, pieces):
    out, rest = None, val
    for n in range(pieces):
        part = rest.astype(BF16)
        moved = jnp.dot(perm, part, preferred_element_type=F32)
        out = moved if out is None else out + moved
        if n + 1 < pieces:
            rest = rest - part.astype(F32)
    return out


def _scan_buffer():
    return pltpu.VMEM((SSM_CHUNK, N_LANES), F32)


def _lanes(k):
    return pl.ds(k * SLAB_LANES, SLAB_LANES)


def _tile(i):
    return pl.ds(i * 8 if isinstance(i, int) else pl.multiple_of(i * 8, 8), 8)


def _seg_get(ref, k, i):
    return ref[_tile(i), _lanes(k)]


def _seg_put(ref, k, i, val):
    ref[_tile(i), _lanes(k)] = val


def _slab_get(ref, k):
    return ref[:, _lanes(k)]


def _slab_put(ref, k, val):
    ref[:, _lanes(k)] = val


def _scan_forward(s_re, s_im, p_re, p_im, car_re, car_im, sp_re=None, sp_im=None):
    for k in range(N_SLABS):
        ln = pl.ds(k * SLAB_LANES, SLAB_LANES)
        ar, ai = _bcast8(p_re[pl.ds(0, 1), ln]), _bcast8(p_im[pl.ds(0, 1), ln])

        def step(i, s, k=k, ar=ar, ai=ai):
            sr, si = s
            nr = ar * sr - ai * si + _seg_get(s_re, k, i)
            ni = ar * si + ai * sr + _seg_get(s_im, k, i)
            _seg_put(s_re, k, i, nr)
            _seg_put(s_im, k, i, ni)
            return nr, ni

        zero = jnp.zeros((8, SLAB_LANES), F32)
        er, ei = lax.fori_loop(0, SEG_LEN, step, (zero, zero), unroll=4)
        lr, li = p_re[pl.ds(SEG_LEN - 1, 1), ln], p_im[pl.ds(SEG_LEN - 1, 1), ln]
        cr, ci = car_re[pl.ds(0, 1), ln], car_im[pl.ds(0, 1), ln]
        rows_r, rows_i = [], []
        for r in range(8):
            rows_r.append(cr)
            rows_i.append(ci)
            cr, ci = er[r:r + 1] + lr * cr - li * ci, ei[r:r + 1] + lr * ci + li * cr
        pr8, pi8 = jnp.concatenate(rows_r, axis=0), jnp.concatenate(rows_i, axis=0)
        car_re[:, ln] = _bcast8(cr)
        car_im[:, ln] = _bcast8(ci)
        if sp_re is not None:
            sp_re[:, ln] = pr8
            sp_im[:, ln] = pi8

        def fix(i, _, k=k, ln=ln, pr8=pr8, pi8=pi8):
            qr, qi = _bcast8(p_re[pl.ds(i, 1), ln]), _bcast8(p_im[pl.ds(i, 1), ln])
            _seg_put(s_re, k, i, _seg_get(s_re, k, i) + qr * pr8 - qi * pi8)
            _seg_put(s_im, k, i, _seg_get(s_im, k, i) + qr * pi8 + qi * pr8)
            return 0

        lax.fori_loop(0, SEG_LEN, fix, 0, unroll=4)


def _scan_backward(g_re, g_im, s_re, s_im, sp_re, sp_im, p_re, p_im, car_re, car_im, acc_re, acc_im):
    for k in range(N_SLABS):
        ln = pl.ds(k * SLAB_LANES, SLAB_LANES)
        ar, ai = _bcast8(p_re[pl.ds(0, 1), ln]), -_bcast8(p_im[pl.ds(0, 1), ln])

        def step(j, s, k=k, ar=ar, ai=ai):
            i = SEG_LEN - 1 - j
            sr, si = s
            nr = ar * sr - ai * si + _seg_get(g_re, k, i)
            ni = ar * si + ai * sr + _seg_get(g_im, k, i)
            _seg_put(g_re, k, i, nr)
            _seg_put(g_im, k, i, ni)
            return nr, ni

        zero = jnp.zeros((8, SLAB_LANES), F32)
        er, ei = lax.fori_loop(0, SEG_LEN, step, (zero, zero), unroll=4)
        lr, li = p_re[pl.ds(SEG_LEN - 1, 1), ln], -p_im[pl.ds(SEG_LEN - 1, 1), ln]
        cr, ci = car_re[pl.ds(0, 1), ln], car_im[pl.ds(0, 1), ln]
        rows_r, rows_i = [None] * 8, [None] * 8
        for r in range(7, -1, -1):
            rows_r[r], rows_i[r] = cr, ci
            cr, ci = er[r:r + 1] + lr * cr - li * ci, ei[r:r + 1] + lr * ci + li * cr
        nr8, ni8 = jnp.concatenate(rows_r, axis=0), jnp.concatenate(rows_i, axis=0)
        car_re[:, ln] = _bcast8(cr)
        car_im[:, ln] = _bcast8(ci)

        def fix(i, acc, k=k, ln=ln, nr8=nr8, ni8=ni8):
            qr = _bcast8(p_re[pl.ds(SEG_LEN - 1 - i, 1), ln])
            qi = -_bcast8(p_im[pl.ds(SEG_LEN - 1 - i, 1), ln])
            gr = _seg_get(g_re, k, i) + qr * nr8 - qi * ni8
            gi = _seg_get(g_im, k, i) + qr * ni8 + qi * nr8
            _seg_put(g_re, k, i, gr)
            _seg_put(g_im, k, i, gi)
            return gr, gi

        def prod(xr, xi, gr, gi):
            return xr * gr + xi * gi, xr * gi - xi * gr

        gr, gi = fix(0, None)
        a_r, a_i = prod(sp_re[:, ln], sp_im[:, ln], gr, gi)

        def fix_acc(i, acc, k=k, fix=fix):
            gr, gi = fix(i, None)
            dr, di = prod(_seg_get(s_re, k, i - 1), _seg_get(s_im, k, i - 1), gr, gi)
            return acc[0] + dr, acc[1] + di

        acc = (a_r, a_i)
        for i in range(1, 4):
            acc = fix_acc(i, acc)
        a_r, a_i = lax.fori_loop(4, SEG_LEN, fix_acc, acc, unroll=4)
        acc_re[:, ln] += a_r
        acc_im[:, ln] += a_i


def _gelu_and_grad(y):
    cdf = 0.5 * (1.0 + lax.erf(y * (2.0 ** -0.5)))
    pdf = jnp.exp(-0.5 * y * y) * (1.0 / math.sqrt(2.0 * math.pi))
    return y * cdf, cdf + y * pdf


def _ssm_fwd_body(u_ref, bbd_ref, cbd_ref, pre_ref, pim_ref, d_ref, perm_ref, unperm_ref,
                  y_ref, gin_ref, cre_out, cim_out, s_re, s_im, car_re, car_im, yp):
    c = pl.program_id(0)

    @pl.when(c == 0)
    def _():
        car_re[...] = jnp.zeros_like(car_re)
        car_im[...] = jnp.zeros_like(car_im)

    cre_out[...] = car_re[...]
    cim_out[...] = car_im[...]
    u = u_ref[...].astype(F32)
    up = jnp.dot(perm_ref[...], u_ref[...].astype(BF16), preferred_element_type=F32).astype(BF16)
    for k in range(N_SLABS):
        bu = jnp.dot(up[:, k * SLAB_CH:(k + 1) * SLAB_CH], bbd_ref[k], preferred_element_type=F32)
        _slab_put(s_re, k, bu[:, :SLAB_LANES])
        _slab_put(s_im, k, bu[:, SLAB_LANES:])
    _scan_forward(s_re, s_im, pre_ref, pim_ref, car_re, car_im)
    for k in range(N_SLABS):
        yp[:, pl.ds(k * SLAB_CH, SLAB_CH)] = (
            jnp.dot(_slab_get(s_re, k).astype(BF16), cbd_ref[k, :SLAB_LANES, :], preferred_element_type=F32)
            + jnp.dot(_slab_get(s_im, k).astype(BF16), cbd_ref[k, SLAB_LANES:, :], preferred_element_type=F32))
    y = _permute_exact(unperm_ref[...], yp[...], 3) + d_ref[...] * u
    y_ref[...] = y.astype(y_ref.dtype)
    gin_ref[...] = _gelu_and_grad(y)[0].astype(BF16)


def _ssm_forward(proj, p_re, p_im, b_bd, c_bd, d_skip):
    t = proj.shape[0]
    nc = t // SSM_CHUNK
    perm = _segment_permutation()
    return pl.pallas_call(
        functools.partial(_ssm_fwd_body), name="ssm_fwd", grid=(nc,),
        in_specs=[pl.BlockSpec((SSM_CHUNK, D_SSM), lambda c: (c, C_U // D_SSM)),
                  _whole(b_bd.shape), _whole(c_bd.shape), _whole(p_re.shape), _whole(p_im.shape),
                  _whole((1, D_SSM)), _whole(perm.shape), _whole(perm.shape)],
        out_specs=[pl.BlockSpec((SSM_CHUNK, D_SSM), lambda c: (c, 0)),
                   pl.BlockSpec((SSM_CHUNK, D_SSM), lambda c: (c, 0)),
                   pl.BlockSpec((None, 8, N_LANES), lambda c: (c, 0, 0)),
                   pl.BlockSpec((None, 8, N_LANES), lambda c: (c, 0, 0))],
        out_shape=[jax.ShapeDtypeStruct((t, D_SSM), BF16), jax.ShapeDtypeStruct((t, D_SSM), BF16),
                   jax.ShapeDtypeStruct((nc, 8, N_LANES), F32), jax.ShapeDtypeStruct((nc, 8, N_LANES), F32)],
        scratch_shapes=[_scan_buffer(), _scan_buffer(),
                        pltpu.VMEM((8, N_LANES), F32), pltpu.VMEM((8, N_LANES), F32),
                        pltpu.VMEM((SSM_CHUNK, D_SSM), F32)],
        compiler_params=_params(("arbitrary",)),
    )(proj, b_bd, c_bd, p_re, p_im, d_skip, jnp.asarray(perm, BF16), jnp.asarray(perm.T, BF16))


def _ssm_bwd_body(u_ref, dgin_ref, y_ref, cre_in, cim_in, bbd_ref, cbd_ref, pre_ref, pim_ref, d_ref, perm_ref,
                  unperm_ref, dproj_in,
                  du_ref, gb_ref, gc_ref, glr_ref, gli_ref, gd_ref,
                  s_re, s_im, g_re, g_im, sp_re, sp_im, car_re, car_im, gcar_re, gcar_im, acc_re, acc_im, dup):
    del dproj_in
    c = pl.program_id(0)
    nc = pl.num_programs(0)

    @pl.when(c == 0)
    def _():
        gcar_re[...] = jnp.zeros_like(gcar_re)
        gcar_im[...] = jnp.zeros_like(gcar_im)
        acc_re[...] = jnp.zeros_like(acc_re)
        acc_im[...] = jnp.zeros_like(acc_im)
        gb_ref[...] = jnp.zeros_like(gb_ref)
        gc_ref[...] = jnp.zeros_like(gc_ref)
        gd_ref[...] = jnp.zeros_like(gd_ref)

    car_re[...] = cre_in[...]
    car_im[...] = cim_in[...]
    u = u_ref[...].astype(F32)
    dy = dgin_ref[...].astype(F32) * _gelu_and_grad(y_ref[...].astype(F32))[1]
    gd_ref[...] += jnp.sum(dy * u, axis=0, keepdims=True)
    up = jnp.dot(perm_ref[...], u.astype(BF16), preferred_element_type=F32).astype(BF16)
    dyp = jnp.dot(perm_ref[...], dy.astype(BF16), preferred_element_type=F32).astype(BF16)
    for k in range(N_SLABS):
        ch = slice(k * SLAB_CH, (k + 1) * SLAB_CH)
        bu = jnp.dot(up[:, ch], bbd_ref[k], preferred_element_type=F32)
        _slab_put(s_re, k, bu[:, :SLAB_LANES])
        _slab_put(s_im, k, bu[:, SLAB_LANES:])
        ds = lax.dot_general(dyp[:, ch], cbd_ref[k], _DIMS["nt"], preferred_element_type=F32)
        _slab_put(g_re, k, ds[:, :SLAB_LANES])
        _slab_put(g_im, k, ds[:, SLAB_LANES:])
    _scan_forward(s_re, s_im, pre_ref, pim_ref, car_re, car_im, sp_re, sp_im)
    _scan_backward(g_re, g_im, s_re, s_im, sp_re, sp_im, pre_ref, pim_ref, gcar_re, gcar_im, acc_re, acc_im)
    for k in range(N_SLABS):
        ch = slice(k * SLAB_CH, (k + 1) * SLAB_CH)
        uk, dyk = up[:, ch], dyp[:, ch]
        sr, si = _slab_get(s_re, k).astype(BF16), _slab_get(s_im, k).astype(BF16)
        gr, gi = _slab_get(g_re, k).astype(BF16), _slab_get(g_im, k).astype(BF16)
        gc_ref[k, :SLAB_LANES, :] += lax.dot_general(sr, dyk, _DIMS["tn"], preferred_element_type=F32)
        gc_ref[k, SLAB_LANES:, :] += lax.dot_general(si, dyk, _DIMS["tn"], preferred_element_type=F32)
        gb_ref[k, :, :SLAB_LANES] += lax.dot_general(uk, gr, _DIMS["tn"], preferred_element_type=F32)
        gb_ref[k, :, SLAB_LANES:] += lax.dot_general(uk, gi, _DIMS["tn"], preferred_element_type=F32)
        dup[:, pl.ds(k * SLAB_CH, SLAB_CH)] = (
            lax.dot_general(gr, bbd_ref[k, :, :SLAB_LANES], _DIMS["nt"], preferred_element_type=F32)
            + lax.dot_general(gi, bbd_ref[k, :, SLAB_LANES:], _DIMS["nt"], preferred_element_type=F32))
    du = _permute_exact(unperm_ref[...], dup[...], 2) + d_ref[...] * dy
    du_ref[...] = du.astype(du_ref.dtype)

    @pl.when(c == nc - 1)
    def _():
        glr_ref[...] = jnp.sum(acc_re[...], axis=0, keepdims=True)
        gli_ref[...] = jnp.sum(acc_im[...], axis=0, keepdims=True)


def _ssm_backward(proj, dg_in, y_ssm, car_re, car_im, p_re, p_im, b_bd, c_bd, d_skip, dproj):
    t = proj.shape[0]
    nc = t // SSM_CHUNK
    rev = lambda c: nc - 1 - c
    big = _scan_buffer
    small = lambda: pltpu.VMEM((8, N_LANES), F32)
    perm = _segment_permutation()
    outs = pl.pallas_call(
        functools.partial(_ssm_bwd_body), name="ssm_bwd", grid=(nc,),
        in_specs=[pl.BlockSpec((SSM_CHUNK, D_SSM), lambda c: (rev(c), C_U // D_SSM)),
                  pl.BlockSpec((SSM_CHUNK, D_SSM), lambda c: (rev(c), 0)),
                  pl.BlockSpec((SSM_CHUNK, D_SSM), lambda c: (rev(c), 0)),
                  pl.BlockSpec((None, 8, N_LANES), lambda c: (rev(c), 0, 0)),
                  pl.BlockSpec((None, 8, N_LANES), lambda c: (rev(c), 0, 0)),
                  _whole(b_bd.shape), _whole(c_bd.shape), _whole(p_re.shape), _whole(p_im.shape),
                  _whole((1, D_SSM)), _whole(perm.shape), _whole(perm.shape), pl.BlockSpec(memory_space=pl.ANY)],
        out_specs=[pl.BlockSpec((SSM_CHUNK, D_SSM), lambda c: (rev(c), C_U // D_SSM)),
                   _whole(b_bd.shape), _whole(c_bd.shape), _whole((1, N_LANES)), _whole((1, N_LANES)),
                   _whole((1, D_SSM))],
        out_shape=[jax.ShapeDtypeStruct(dproj.shape, dproj.dtype),
                   jax.ShapeDtypeStruct(b_bd.shape, F32), jax.ShapeDtypeStruct(c_bd.shape, F32),
                   jax.ShapeDtypeStruct((1, N_LANES), F32), jax.ShapeDtypeStruct((1, N_LANES), F32),
                   jax.ShapeDtypeStruct((1, D_SSM), F32)],
        scratch_shapes=[big(), big(), big(), big()] + [small() for _ in range(8)]
        + [pltpu.VMEM((SSM_CHUNK, D_SSM), F32)],
        input_output_aliases={12: 0},
        compiler_params=_params(("arbitrary",)),
    )(proj, dg_in, y_ssm, car_re, car_im, b_bd, c_bd, p_re, p_im, d_skip, jnp.asarray(perm, BF16),
      jnp.asarray(perm.T, BF16), dproj)
    return outs


def _bucket_table():
    i = np.arange(BLOCK)[:, None]
    j = np.arange(2 * BLOCK)[None, :]
    dist = BLOCK + i - j
    ok = (dist >= 0) & (dist < WINDOW)
    max_exact = N_BUCKETS // 2
    d = np.maximum(dist, 1).astype(np.float32)
    large = max_exact + (np.log(d / max_exact) / math.log(MAX_DISTANCE / max_exact)
                         * (N_BUCKETS - max_exact)).astype(np.int32)
    large = np.minimum(large, N_BUCKETS - 1)
    bucket = np.where(dist < max_exact, dist, large)
    return np.where(ok, bucket, -1).astype(np.int32)


def _bias_build_body(table_ref, bucket_ref, o_ref):
    h = pl.program_id(0)
    bucket = bucket_ref[...]
    acc = jnp.full(bucket.shape, NEG_INF, F32)
    for b in range(N_BUCKETS):
        acc = jnp.where(bucket == b, table_ref[b, h], acc)
    o_ref[...] = acc


def _bias_build(rel_bias_table, bucket):
    return pl.pallas_call(
        functools.partial(_bias_build_body), name="bias_build", grid=(N_Q_HEADS,),
        in_specs=[pl.BlockSpec(memory_space=pltpu.SMEM), _whole(bucket.shape)],
        out_specs=pl.BlockSpec((None, BLOCK, 2 * BLOCK), lambda h: (h, 0, 0)),
        out_shape=jax.ShapeDtypeStruct((N_Q_HEADS, BLOCK, 2 * BLOCK), F32),
        compiler_params=_params(("arbitrary",)),
    )(rel_bias_table, bucket)


def _bias_grad_body(g_ref, bucket_ref, o_ref):
    bucket = bucket_ref[...]
    lane = lax.broadcasted_iota(jnp.int32, (N_BUCKETS, 128), 1)

    def head(h, out):
        g = g_ref[h]
        rows = [jnp.sum(jnp.where(bucket == b, g, 0.0), axis=0, keepdims=True) for b in range(N_BUCKETS)]
        colsum = jnp.sum(jnp.concatenate(rows, axis=0), axis=1, keepdims=True)
        return jnp.where(lane == h, colsum, out)

    o_ref[...] = lax.fori_loop(0, N_Q_HEADS, head, jnp.zeros((N_BUCKETS, 128), F32))


def _bias_grad(g_bias, bucket):
    out = pl.pallas_call(
        functools.partial(_bias_grad_body), name="bias_grad", grid=(1,),
        in_specs=[_whole(g_bias.shape), _whole(bucket.shape)],
        out_specs=_whole((N_BUCKETS, 128)),
        out_shape=jax.ShapeDtypeStruct((N_BUCKETS, 128), F32),
        compiler_params=_params(("arbitrary",)),
    )(g_bias, bucket)
    return out[:, :N_Q_HEADS]


def _head_logits(qh, kk, bias_h, first_block):
    s = lax.dot_general(qh, kk, _DIMS["nt"], preferred_element_type=F32) * ATTN_SCALE + bias_h
    col = lax.broadcasted_iota(jnp.int32, s.shape, 1)
    return jnp.where(jnp.logical_and(first_block, col < BLOCK), NEG_INF, s)


def _attn_fwd_body(sink_ref, q_ref, kp_ref, kc_ref, vp_ref, vc_ref, bias_ref, o_ref, lse_ref):
    n = pl.program_id(0)
    outs, lses = [], []
    for kv in range(N_KV_HEADS):
        cs = slice(kv * HEAD_DIM, (kv + 1) * HEAD_DIM)
        kk = jnp.concatenate([kp_ref[:, cs], kc_ref[:, cs]], axis=0).astype(BF16)
        vv = jnp.concatenate([vp_ref[:, cs], vc_ref[:, cs]], axis=0).astype(BF16)
        for g in range(Q_PER_KV):
            h = kv * Q_PER_KV + g
            qh = q_ref[:, h * HEAD_DIM:(h + 1) * HEAD_DIM].astype(BF16)
            s = _head_logits(qh, kk, bias_ref[h], n == 0)
            sink = sink_ref[0, h]
            m = jnp.maximum(jnp.max(s, axis=1, keepdims=True), sink)
            p = jnp.exp(s - m)
            den = jnp.sum(p, axis=1, keepdims=True) + jnp.exp(sink - m)
            p = p / den
            outs.append(jnp.dot(p.astype(BF16), vv, preferred_element_type=F32))
            lses.append(m + jnp.log(den))
    o_ref[...] = jnp.concatenate(outs, axis=1).astype(o_ref.dtype)
    lse_ref[...] = jnp.concatenate(lses, axis=1)


def _attn_specs(nb):
    prev = lambda n: jnp.maximum(jnp.minimum(n, nb - 1) - 1, 0)
    cur = lambda n: jnp.minimum(n, nb - 1)
    return [pl.BlockSpec((BLOCK, D_ATTN), lambda n: (cur(n), C_Q // D_ATTN)),
            pl.BlockSpec((BLOCK, D_KV), lambda n: (prev(n), C_K // D_KV)),
            pl.BlockSpec((BLOCK, D_KV), lambda n: (cur(n), C_K // D_KV)),
            pl.BlockSpec((BLOCK, D_KV), lambda n: (prev(n), C_V // D_KV)),
            pl.BlockSpec((BLOCK, D_KV), lambda n: (cur(n), C_V // D_KV))]


def _attn_forward(proj, sinks, bias):
    t = proj.shape[0]
    nb = t // BLOCK
    return pl.pallas_call(
        functools.partial(_attn_fwd_body), name="attn_fwd", grid=(nb,),
        in_specs=[pl.BlockSpec(memory_space=pltpu.SMEM)] + _attn_specs(nb) + [_whole(bias.shape)],
        out_specs=[pl.BlockSpec((BLOCK, D_ATTN), lambda n: (n, 0)),
                   pl.BlockSpec((BLOCK, N_Q_HEADS), lambda n: (n, 0))],
        out_shape=[jax.ShapeDtypeStruct((t, D_ATTN), BF16), jax.ShapeDtypeStruct((t, N_Q_HEADS), F32)],
        compiler_params=_params(("parallel",)),
    )(sinks, proj, proj, proj, proj, proj, bias)


def _attn_bwd_body(sink_ref, q_ref, kp_ref, kc_ref, vp_ref, vc_ref, bias_ref, do_ref, o_ref, lse_ref, dproj_in,
                   dq_ref, dkv_ref, gbias_ref, gsink_ref, carry_ref, *, nb):
    del dproj_in
    n = pl.program_id(0)

    @pl.when(n == 0)
    def _():
        gbias_ref[...] = jnp.zeros_like(gbias_ref)
        gsink_ref[...] = jnp.zeros_like(gsink_ref)
        carry_ref[...] = jnp.zeros_like(carry_ref)

    @pl.when(n < nb)
    def _():
        lane = lax.broadcasted_iota(jnp.int32, (1, 128), 1)
        dqs, dks, dvs = [], [], []
        gsink = jnp.zeros((1, 128), F32)
        for kv in range(N_KV_HEADS):
            cs = slice(kv * HEAD_DIM, (kv + 1) * HEAD_DIM)
            kk = jnp.concatenate([kp_ref[:, cs], kc_ref[:, cs]], axis=0).astype(BF16)
            vv = jnp.concatenate([vp_ref[:, cs], vc_ref[:, cs]], axis=0).astype(BF16)
            dk = jnp.zeros((2 * BLOCK, HEAD_DIM), F32)
            dv = jnp.zeros((2 * BLOCK, HEAD_DIM), F32)
            for g in range(Q_PER_KV):
                h = kv * Q_PER_KV + g
                hs = slice(h * HEAD_DIM, (h + 1) * HEAD_DIM)
                qh = q_ref[:, hs].astype(BF16)
                s = _head_logits(qh, kk, bias_ref[h], n == 0)
                lse = lse_ref[:, h:h + 1]
                p = jnp.exp(s - lse)
                do = do_ref[:, hs].astype(F32)
                delta = jnp.sum(do * o_ref[:, hs].astype(F32), axis=1, keepdims=True)
                dob = do.astype(BF16)
                dp = lax.dot_general(dob, vv, _DIMS["nt"], preferred_element_type=F32)
                dl = p * (dp - delta)
                gbias_ref[h] += dl
                psink = jnp.exp(sink_ref[0, h] - lse)
                gsink = gsink + jnp.where(lane == h, -jnp.sum(psink * delta), 0.0)
                dlb = dl.astype(BF16)
                dqs.append(jnp.dot(dlb, kk, preferred_element_type=F32) * ATTN_SCALE)
                dk = dk + lax.dot_general(dlb, qh, _DIMS["tn"], preferred_element_type=F32) * ATTN_SCALE
                dv = dv + lax.dot_general(p.astype(BF16), dob, _DIMS["tn"], preferred_element_type=F32)
            dks.append(dk)
            dvs.append(dv)
        dq_ref[...] = jnp.concatenate(dqs, axis=1).astype(dq_ref.dtype)
        gsink_ref[...] += gsink
        dkv = jnp.concatenate(dks + dvs, axis=1)
        dkv_ref[...] = (carry_ref[...] + dkv[:BLOCK]).astype(dkv_ref.dtype)
        carry_ref[...] = dkv[BLOCK:]

    @pl.when(n == nb)
    def _():
        dkv_ref[...] = carry_ref[...].astype(dkv_ref.dtype)


def _attn_backward(proj, sinks, bias, d_attn, attn, lse, dproj):
    t = proj.shape[0]
    nb = t // BLOCK
    cur = lambda n: jnp.minimum(n, nb - 1)
    return pl.pallas_call(
        functools.partial(_attn_bwd_body, nb=nb), name="attn_bwd", grid=(nb + 1,),
        in_specs=[pl.BlockSpec(memory_space=pltpu.SMEM)] + _attn_specs(nb) + [
            _whole(bias.shape),
            pl.BlockSpec((BLOCK, D_ATTN), lambda n: (cur(n), 0)),
            pl.BlockSpec((BLOCK, D_ATTN), lambda n: (cur(n), 0)),
            pl.BlockSpec((BLOCK, N_Q_HEADS), lambda n: (cur(n), 0)),
            pl.BlockSpec(memory_space=pl.ANY)],
        out_specs=[pl.BlockSpec((BLOCK, D_ATTN), lambda n: (cur(n), C_Q // D_ATTN)),
                   pl.BlockSpec((BLOCK, 2 * D_KV), lambda n: (jnp.maximum(n - 1, 0), 0)),
                   _whole(bias.shape), _whole((1, 128))],
        out_shape=[jax.ShapeDtypeStruct(dproj.shape, dproj.dtype), jax.ShapeDtypeStruct((t, 2 * D_KV), dproj.dtype),
                   jax.ShapeDtypeStruct(bias.shape, F32), jax.ShapeDtypeStruct((1, 128), F32)],
        scratch_shapes=[pltpu.VMEM((BLOCK, 2 * D_KV), F32)],
        input_output_aliases={10: 0},
        compiler_params=_params(("arbitrary",)),
    )(sinks, proj, proj, proj, proj, proj, bias, d_attn, attn, lse, dproj)


ROWS = 256


def _rowwise(body, name, t, ins, outs, aliases=None):
    rows = min(ROWS, t)

    def col_spec(w, c0):
        if c0 % w == 0:
            return pl.BlockSpec((rows, w), lambda i: (i, c0 // w))
        return pl.BlockSpec((pl.Element(rows), pl.Element(w)), lambda i: (i * rows, c0))

    in_specs, args = [], []
    for a, w, c0 in ins:
        args.append(a)
        if w is None:
            in_specs.append(pl.BlockSpec(memory_space=pl.ANY) if c0 == "any" else _whole(a.shape))
        else:
            in_specs.append(col_spec(w, c0))
    out_specs, out_shape = [], []
    for shape, dtype, w, c0 in outs:
        out_shape.append(jax.ShapeDtypeStruct(shape, dtype))
        out_specs.append(_whole(shape) if w is None else col_spec(w, c0))
    accum = any(o[2] is None for o in outs)
    return pl.pallas_call(
        body, name=name, grid=(t // rows,), in_specs=in_specs, out_specs=out_specs, out_shape=out_shape,
        input_output_aliases=aliases or {},
        compiler_params=_params(("arbitrary",) if accum else ("parallel",)),
    )(*args)


def _f32(ref, *idx):
    return (ref[idx] if idx else ref[...]).astype(F32)


def _ssm_gate_fwd_body(glu_ref, z_ref, h_ref):
    a, b = _f32(glu_ref, slice(None), slice(0, D_SSM)), _f32(glu_ref, slice(None), slice(D_SSM, None))
    h_ref[...] = ((a * _sigmoid(b)) * _silu_and_grad(_f32(z_ref))[0]).astype(h_ref.dtype)


def _ssm_gate_bwd_body(dh_ref, glu_ref, z_ref, dproj_in, dglu_ref, dz_ref):
    del dproj_in
    a, b = _f32(glu_ref, slice(None), slice(0, D_SSM)), _f32(glu_ref, slice(None), slice(D_SSM, None))
    sb = _sigmoid(b)
    silu, dsilu = _silu_and_grad(_f32(z_ref))
    dh = _f32(dh_ref)
    dg = dh * silu
    dz_ref[...] = (dh * (a * sb) * dsilu).astype(dz_ref.dtype)
    dglu_ref[:, :D_SSM] = (dg * sb).astype(dglu_ref.dtype)
    dglu_ref[:, D_SSM:] = (dg * a * sb * (1.0 - sb)).astype(dglu_ref.dtype)


def _attn_gate_fwd_body(attn_ref, z_ref, h_ref):
    h_ref[...] = (_f32(attn_ref) * _silu_and_grad(_f32(z_ref))[0]).astype(h_ref.dtype)


def _attn_gate_bwd_body(dh_ref, attn_ref, z_ref, dproj_in, dattn_ref, dz_ref):
    del dproj_in
    silu, dsilu = _silu_and_grad(_f32(z_ref))
    dh = _f32(dh_ref)
    dattn_ref[...] = (dh * silu).astype(dattn_ref.dtype)
    dz_ref[...] = (dh * _f32(attn_ref) * dsilu).astype(dz_ref.dtype)


def _merge_fwd_body(bs_ref, ba_ref, gl_ref, m_ref):
    gs = _sigmoid(_f32(gl_ref, slice(None), slice(0, D_MODEL)))
    ga = _sigmoid(_f32(gl_ref, slice(None), slice(D_MODEL, None)))
    m_ref[...] = (gs * _f32(bs_ref) + ga * _f32(ba_ref)).astype(m_ref.dtype)


def _merge_bwd_body(dm_ref, bs_ref, ba_ref, gl_ref, dbs_ref, dba_ref, dgl_ref):
    gs = _sigmoid(_f32(gl_ref, slice(None), slice(0, D_MODEL)))
    ga = _sigmoid(_f32(gl_ref, slice(None), slice(D_MODEL, None)))
    dm = _f32(dm_ref)
    dbs_ref[...] = (dm * gs).astype(dbs_ref.dtype)
    dba_ref[...] = (dm * ga).astype(dba_ref.dtype)
    dgl_ref[:, :D_MODEL] = (dm * _f32(bs_ref) * gs * (1.0 - gs)).astype(dgl_ref.dtype)
    dgl_ref[:, D_MODEL:] = (dm * _f32(ba_ref) * ga * (1.0 - ga)).astype(dgl_ref.dtype)


def _ln_loss_body(x_ref, o_ref, tgt_ref, gain_ref, bias_ref, dr_ref, drb_ref, loss_ref, dgain_ref, dbias_ref):
    @pl.when(pl.program_id(0) == 0)
    def _():
        loss_ref[...] = jnp.zeros_like(loss_ref)
        dgain_ref[...] = jnp.zeros_like(dgain_ref)
        dbias_ref[...] = jnp.zeros_like(dbias_ref)

    r = ALPHA * x_ref[...] + o_ref[...]
    mu = jnp.mean(r, axis=1, keepdims=True)
    rc = r - mu
    var = jnp.mean(rc * rc, axis=1, keepdims=True)
    rstd = lax.rsqrt(var + LN_EPS)
    xhat = rc * rstd
    gain = gain_ref[...]
    err = xhat * gain + bias_ref[...] - tgt_ref[...]
    loss_ref[...] += 0.5 * jnp.sum(jnp.mean(err * err, axis=1, keepdims=True), axis=0, keepdims=True)
    dy = err * (1.0 / D_MODEL)
    dgain_ref[...] += jnp.sum(dy * xhat, axis=0, keepdims=True)
    dbias_ref[...] += jnp.sum(dy, axis=0, keepdims=True)
    dxhat = dy * gain
    m1 = jnp.mean(dxhat, axis=1, keepdims=True)
    m2 = jnp.mean(dxhat * xhat, axis=1, keepdims=True)
    dr = rstd * (dxhat - m1 - xhat * m2)
    dr_ref[...] = dr
    drb_ref[...] = dr.astype(drb_ref.dtype)


def _place_body(piece_ref, dproj_in, o_ref):
    del dproj_in
    o_ref[...] = piece_ref[...]


def _adamw_update(w_ref, m_ref, v_ref, g, g_ref, d_ref, nm_ref, nv_ref):
    m = ADAM_B1 * m_ref[...] + (1.0 - ADAM_B1) * g
    v = ADAM_B2 * v_ref[...] + (1.0 - ADAM_B2) * (g * g)
    m_hat = m / (1.0 - ADAM_B1 ** ADAM_STEP)
    v_hat = v / (1.0 - ADAM_B2 ** ADAM_STEP)
    g_ref[...] = g
    d_ref[...] = -ADAM_LR * (m_hat / (jnp.sqrt(v_hat) + ADAM_EPS) + ADAM_WD * w_ref[...])
    nm_ref[...] = m
    nv_ref[...] = v


def _adamw_body(*refs, n_parts):
    w_ref, m_ref, v_ref = refs[:3]
    parts = refs[3:3 + n_parts]
    g = parts[0][...].astype(F32)
    for p in parts[1:]:
        g = g + p[...].astype(F32)
    _adamw_update(w_ref, m_ref, v_ref, g, *refs[3 + n_parts:])


def _adamw_shard_body(c_ref, w_ref, m_ref, v_ref, mine_ref, other_ref, g_ref, d_ref, nm_ref, nv_ref, *, nth):
    in_mine = pl.program_id(0) // nth == c_ref[0]
    g = jnp.where(in_mine, mine_ref[...], other_ref[...])
    _adamw_update(w_ref, m_ref, v_ref, g, g_ref, d_ref, nm_ref, nv_ref)


def _adamw_shard(w, m, v, halves, core, *, name, rows):
    shape = w.shape
    w2, m2, v2 = (a.reshape(-1, shape[-1]) for a in (w, m, v))
    r, c = w2.shape
    nth = r // 2 // rows
    assert 2 * nth * rows == r and halves[0].shape == (r // 2, c)
    spec = pl.BlockSpec((rows, c), lambda i, cr: (i, 0))
    half_spec = pl.BlockSpec((rows, c), lambda i, cr: (i % nth, 0))
    outs = pl.pallas_call(
        functools.partial(_adamw_shard_body, nth=nth), name=name,
        grid_spec=pltpu.PrefetchScalarGridSpec(
            num_scalar_prefetch=1, grid=(r // rows,),
            in_specs=[spec] * 3 + [half_spec] * 2, out_specs=[spec] * 4),
        out_shape=[jax.ShapeDtypeStruct((r, c), F32)] * 4,
        compiler_params=_params(("parallel",)),
    )(core, w2, m2, v2, *halves)
    return tuple(o.reshape(shape) for o in outs)


def _adamw(w, m, v, parts, *, name, rows=256):
    shape = w.shape
    w2, m2, v2 = (a.reshape(-1, shape[-1]) for a in (w, m, v))
    parts = [p.reshape(w2.shape) for p in parts]
    r, c = w2.shape
    rows = rows if r % rows == 0 else r
    spec = pl.BlockSpec((rows, c), lambda i: (i, 0))
    outs = pl.pallas_call(
        functools.partial(_adamw_body, n_parts=len(parts)), name=name, grid=(r // rows,),
        in_specs=[spec] * (3 + len(parts)), out_specs=[spec] * 4,
        out_shape=[jax.ShapeDtypeStruct((r, c), F32)] * 4,
        compiler_params=_params(("parallel",)),
    )(w2, m2, v2, *parts)
    return tuple(o.reshape(shape) for o in outs)


BIG = ("w_in", "w_glu", "w_branch_ssm", "w_branch_attn", "w_out")
SHARD_AXIS = dict(w_in=1, w_glu=1, w_branch_ssm=1, w_branch_attn=1, w_out=0)
HBM = pl.BlockSpec(memory_space=pl.ANY)


def _position():
    x, y, c = lax.axis_index("x"), lax.axis_index("y"), lax.axis_index("c")
    other_chips = [(1 - x, y), (x, 1 - y), (1 - x, 1 - y)]
    return x, y, c, other_chips


def _window(ref, axis, shard, n_shards, half=None):
    rows, cols = ref.shape[-2:]
    sel = [slice(None), slice(None)]
    size = ref.shape[-2 + axis] // n_shards
    sel[axis] = pl.ds(pl.multiple_of(shard * size, 128), size)
    if half is not None:
        hsize = ref.shape[-1 - axis] // 2
        sel[1 - axis] = pl.ds(pl.multiple_of(half * hsize, 128), hsize)
    return ref.at[tuple(sel)]


def _half(ref, axis, half):
    hsize = ref.shape[-1 - axis] // 2
    sel = [slice(None), slice(None)]
    sel[1 - axis] = pl.ds(pl.multiple_of(half * hsize, 128), hsize)
    return ref.at[tuple(sel)]


def _remote(src, dst, send_sem, recv_sem, device):
    return pltpu.make_async_remote_copy(src_ref=src, dst_ref=dst, send_sem=send_sem, recv_sem=recv_sem,
                                        device_id=device, device_id_type=MESH)


def _gather_body(*refs, axes):
    n = len(axes)
    shards, outs = refs[:n], refs[n:2 * n]
    send_sems, recv_sems, fsend_sems, frecv_sems, local_sems = refs[2 * n:]
    x, y, c, chips = _position()
    me = 2 * x + y
    local, sends, passes = [], [], []
    for w, ax in enumerate(axes):
        cp = pltpu.make_async_copy(shards[w], _window(outs[w], ax, me, N_CHIPS), local_sems.at[w])
        cp.start()
        local.append(cp)
    for w, ax in enumerate(axes):
        for r, (px, py) in enumerate(chips):
            cp = _remote(_half(shards[w], ax, c), _window(outs[w], ax, me, N_CHIPS, c),
                         send_sems.at[3 * w + r], recv_sems.at[3 * w + r], (px, py, c))
            cp.start()
            sends.append(cp)
    for w, ax in enumerate(axes):
        for r, (px, py) in enumerate(chips):
            landed = _window(outs[w], ax, 2 * px + py, N_CHIPS, c)
            _remote(landed, landed, send_sems.at[3 * w + r], recv_sems.at[3 * w + r], (px, py, c)).wait_recv()
            cp = _remote(landed, landed, fsend_sems.at[3 * w + r], frecv_sems.at[3 * w + r], (x, y, 1 - c))
            cp.start()
            passes.append(cp)
    for w, ax in enumerate(axes):
        for r, (px, py) in enumerate(chips):
            other = _window(outs[w], ax, 2 * px + py, N_CHIPS, 1 - c)
            _remote(other, other, fsend_sems.at[3 * w + r], frecv_sems.at[3 * w + r], (x, y, 1 - c)).wait_recv()
    for cp in sends + passes:
        cp.wait_send()
    for cp in local:
        cp.wait()


def _full_shapes(shards, names):
    out = []
    for k in names:
        s = list(shards[k].shape)
        s[SHARD_AXIS[k]] *= N_CHIPS
        out.append(jax.ShapeDtypeStruct(tuple(s), shards[k].dtype))
    return out


def _all_gather(shards, names):
    n = len(names)
    return pl.pallas_call(
        functools.partial(_gather_body, axes=tuple(SHARD_AXIS[k] for k in names)), name="gather_weights",
        in_specs=[HBM] * n, out_specs=[HBM] * n, out_shape=_full_shapes(shards, names),
        scratch_shapes=[pltpu.SemaphoreType.DMA((3 * n,))] * 4 + [pltpu.SemaphoreType.DMA((n,))],
    )(*[shards[k] for k in names])


def _gather_exchange(shards, names):
    axes = tuple(SHARD_AXIS[k] for k in names)
    n = len(names)

    def copies(ins, outs, sems):
        send_sems, recv_sems, local_sems = sems
        x, y, c, chips = _position()
        me = 2 * x + y
        remote, local = [], []
        for w, ax in enumerate(axes):
            for r, (px, py) in enumerate(chips):
                remote.append(_remote(_half(ins[w], ax, c), _window(outs[w], ax, me, N_CHIPS, c),
                                      send_sems.at[3 * w + r], recv_sems.at[3 * w + r], (px, py, c)))
            local.append(pltpu.make_async_copy(ins[w], _window(outs[w], ax, me, N_CHIPS), local_sems.at[w]))
        return remote, local

    return _Exchange([shards[k] for k in names], _full_shapes(shards, names), (3 * n, 3 * n, n), copies)


def _pass_on_body(*refs, axes):
    n = len(axes)
    ins, outs = refs[:n], refs[n:2 * n]
    sbuf, rbuf = refs[2 * n:5 * n], refs[5 * n:8 * n]
    send_sems, recv_sems, load_sems, store_sems = refs[8 * n:]
    x, y, c, chips = _position()
    region = lambda ref, w, r, half: _window(ref, axes[w], 2 * chips[r][0] + chips[r][1], N_CHIPS, half)
    pairs = [(w, r) for w in range(n) for r in range(3)]
    loads = [pltpu.make_async_copy(region(ins[w], w, r, c), sbuf[3 * w + r], load_sems.at[3 * w + r]) for w, r in pairs]
    for cp in loads:
        cp.start()
    sends = []
    for i, cp in enumerate(loads):
        cp.wait()
        sends.append(_remote(sbuf[i], rbuf[i], send_sems.at[i], recv_sems.at[i], (x, y, 1 - c)))
        sends[-1].start()
    stores = []
    for i, (w, r) in enumerate(pairs):
        sends[i].wait_recv()
        stores.append(pltpu.make_async_copy(rbuf[i], region(outs[w], w, r, 1 - c), store_sems.at[i]))
        stores[-1].start()
    for cp in sends:
        cp.wait_send()
    for cp in stores:
        cp.wait()


def _pass_on(fulls, names):
    axes = tuple(SHARD_AXIS[k] for k in names)
    n = len(names)
    bufs = []
    for a, ax in zip(fulls, axes):
        s = list(a.shape)
        s[ax] //= N_CHIPS
        s[1 - ax] //= 2
        bufs += [pltpu.VMEM(tuple(s), a.dtype)] * 3
    return pl.pallas_call(
        functools.partial(_pass_on_body, axes=axes), name="pass_on_weights",
        in_specs=[HBM] * n, out_specs=[HBM] * n,
        out_shape=[jax.ShapeDtypeStruct(a.shape, a.dtype) for a in fulls],
        scratch_shapes=bufs + bufs + [pltpu.SemaphoreType.DMA((3 * n,))] * 4,
        input_output_aliases={i: i for i in range(n)},
        compiler_params=pltpu.CompilerParams(vmem_limit_bytes=VMEM_LIMIT),
    )(*fulls)


def _chip_exchange(pairs, names):
    axes = tuple(SHARD_AXIS[k] for k in names)
    n = len(names)
    out_shape = []
    for p, ax in zip(pairs, axes):
        s = list(p.shape)
        s[ax] //= N_CHIPS
        out_shape.append(jax.ShapeDtypeStruct((4, *s), p.dtype))

    def copies(ins, outs, sems):
        send_sems, recv_sems, local_sems = sems
        x, y, c, chips = _position()
        me = 2 * x + y
        remote, local = [], []
        for w, ax in enumerate(axes):
            for r, (px, py) in enumerate(chips):
                remote.append(_remote(_window(ins[w], ax, 2 * px + py, N_CHIPS), outs[w].at[r],
                                      send_sems.at[3 * w + r], recv_sems.at[3 * w + r], (px, py, c)))
            local.append(pltpu.make_async_copy(_window(ins[w], ax, me, N_CHIPS), outs[w].at[3], local_sems.at[w]))
        return remote, local

    return _Exchange(pairs, out_shape, (3 * n, 3 * n, n), copies)


def _half_tile(n, h, nt, axis):
    return h * nt + n if axis == 1 else 2 * n + h


def _sibling_stream(n, nt, stage, recv, send_sems, recv_sems, credit, produce, consume):
    x, y, c, _ = _position()
    sibling = (x, y, 1 - c)

    def copy(slot):
        return _remote(stage.at[slot], recv.at[slot], send_sems.at[slot], recv_sems.at[slot], sibling)

    @pl.when(n < nt)
    def _():
        slot = n % 2

        @pl.when(n >= 2)
        def _():
            copy(slot).wait_send()
            pl.semaphore_wait(credit, 1)

        stage[slot] = produce().astype(stage.dtype)
        copy(slot).start()

    @pl.when(n >= 1)
    def _():
        slot = (n - 1) % 2
        copy(slot).wait_recv()
        consume(recv[slot])

        @pl.when(n - 1 < nt - 2)
        def _():
            pl.semaphore_signal(credit, 1, device_id=sibling, device_id_type=MESH)

    @pl.when(n == nt)
    def _():
        for slot in range(min(2, nt)):
            copy(slot).wait_send()


def _pair_reduce_body(c_ref, mine_ref, theirs_ref, out_ref, stage, recv, send_sems, recv_sems, credit, *, nt):
    del c_ref

    def consume(got):
        out_ref[...] = (mine_ref[...] + got.astype(F32)).astype(out_ref.dtype)

    _sibling_stream(pl.program_id(0), nt, stage, recv, send_sems, recv_sems, credit,
                    lambda: theirs_ref[...], consume)


def _pair_reduce(grad, core, axis, *, name, rows):
    r, c = grad.shape
    nt = r // 2 // rows
    assert nt * rows * 2 == r and (axis == 1 or rows == r // (2 * N_CHIPS))
    tile = lambda n, h: _half_tile(n, h, nt, axis)
    return pl.pallas_call(
        functools.partial(_pair_reduce_body, nt=nt), name=name,
        grid_spec=pltpu.PrefetchScalarGridSpec(
            num_scalar_prefetch=1, grid=(nt + 1,),
            in_specs=[pl.BlockSpec((rows, c), lambda n, cr: (tile(jnp.maximum(n - 1, 0), cr[0]), 0)),
                      pl.BlockSpec((rows, c), lambda n, cr: (tile(jnp.minimum(n, nt - 1), 1 - cr[0]), 0))],
            out_specs=pl.BlockSpec((rows, c), lambda n, cr: (jnp.maximum(n - 1, 0), 0)),
            scratch_shapes=[pltpu.VMEM((2, rows, c), BF16), pltpu.VMEM((2, rows, c), BF16),
                            pltpu.SemaphoreType.DMA((2,)), pltpu.SemaphoreType.DMA((2,)),
                            pltpu.SemaphoreType.REGULAR]),
        out_shape=jax.ShapeDtypeStruct((r // 2, c), BF16),
        compiler_params=_params(("arbitrary",)),
    )(core, grad, grad)


def _chip_add_share_body(c_ref, s0, s1, s2, s3, mine_ref, other_ref, stage, recv, send_sems, recv_sems, credit, *, nt):
    del c_ref

    def produce():
        total = s3[...].astype(F32) + s0[...].astype(F32) + s1[...].astype(F32) + s2[...].astype(F32)
        mine_ref[...] = total
        return total

    def consume(got):
        other_ref[...] = got

    _sibling_stream(pl.program_id(0), nt, stage, recv, send_sems, recv_sems, credit, produce, consume)


def _chip_add_share(slots, core, *, name, rows):
    _, r, c = slots.shape
    nt = r // rows
    assert nt * rows == r
    send = lambda j: pl.BlockSpec((None, rows, c), lambda n, cr: (j, jnp.minimum(n, nt - 1), 0))
    return pl.pallas_call(
        functools.partial(_chip_add_share_body, nt=nt), name=name,
        grid_spec=pltpu.PrefetchScalarGridSpec(
            num_scalar_prefetch=1, grid=(nt + 1,),
            in_specs=[send(j) for j in range(4)],
            out_specs=[pl.BlockSpec((rows, c), lambda n, cr: (jnp.minimum(n, nt - 1), 0)),
                       pl.BlockSpec((rows, c), lambda n, cr: (jnp.maximum(n - 1, 0), 0))],
            scratch_shapes=[pltpu.VMEM((2, rows, c), F32), pltpu.VMEM((2, rows, c), F32),
                            pltpu.SemaphoreType.DMA((2,)), pltpu.SemaphoreType.DMA((2,)),
                            pltpu.SemaphoreType.REGULAR]),
        out_shape=[jax.ShapeDtypeStruct((r, c), F32)] * 2,
        compiler_params=_params(("arbitrary",)),
    )(core, slots, slots, slots, slots)


PAIR_ROWS = dict(w_in=64, w_glu=128, w_branch_ssm=128, w_branch_attn=128, w_out=256)
SHARE_ROWS = dict(w_in=128, w_glu=128, w_branch_ssm=128, w_branch_attn=128, w_out=64)


def _pair_sums(grads, names, core):
    return [_pair_reduce(grads[k], core, SHARD_AXIS[k], name="pair_reduce_" + k, rows=PAIR_ROWS[k]) for k in names]


def _shard_halves(slots, names, core):
    return {k: _chip_add_share(s, core, name="chip_add_share_" + k, rows=SHARE_ROWS[k]) for k, s in zip(names, slots)}


SMALL = (("ssm_lambda_re", (1, 64, 64)), ("ssm_lambda_im", (1, 64, 64)), ("ssm_b_re", (1, 64, 64, 16)),
         ("ssm_b_im", (1, 64, 64, 16)), ("ssm_c_re", (1, 64, 16, 64)), ("ssm_c_im", (1, 64, 16, 64)),
         ("ssm_d", (1, 1024)), ("ssm_log_step", (1, 64)), ("attn_sinks", (1, 16)), ("rel_bias_table", (32, 16)),
         ("ln_gain", (1, 2048)), ("ln_bias", (1, 2048)))
SMALL_SIZE = sum(int(np.prod(s)) for _, s in SMALL)
PACK_ROWS = -(-(SMALL_SIZE + 1) // (8 * 128)) * 8


def _pack(values, extra=None):
    flat = [values[k].reshape(-1).astype(F32) for k, _ in SMALL]
    flat.append(jnp.zeros((1,), F32) if extra is None else extra.reshape(1))
    flat.append(jnp.zeros((PACK_ROWS * 128 - SMALL_SIZE - 1,), F32))
    return jnp.concatenate(flat).reshape(PACK_ROWS, 128)


def _unpack(packed):
    flat = packed.reshape(-1)
    out, off = {}, 0
    for k, s in SMALL:
        size = int(np.prod(s))
        out[k] = flat[off:off + size].reshape(s)
        off += size
    return out, flat[off]


def _small_allreduce_body(p_ref, o_ref, buf, send_sems, recv_sems):
    x, y, c, _ = _position()
    me = 4 * x + 2 * y + c
    buf[0] = p_ref[...]
    copies = []
    for r in range(1, 8):
        peer = tuple(1 - v if (r >> s) & 1 else v for v, s in ((x, 2), (y, 1), (c, 0)))
        cp = _remote(p_ref, buf.at[r], send_sems.at[r - 1], recv_sems.at[r - 1], peer)
        cp.start()
        copies.append(cp)
    for cp in copies:
        cp.wait_recv()
    for cp in copies:
        cp.wait_send()
    acc = buf[me]
    for s in range(1, 8):
        acc = acc + buf[jnp.bitwise_xor(me, s)]
    o_ref[...] = acc


def _small_allreduce(packed):
    vm = pl.BlockSpec(memory_space=pltpu.VMEM)
    return pl.pallas_call(
        functools.partial(_small_allreduce_body), name="small_allreduce",
        in_specs=[vm], out_specs=vm, out_shape=jax.ShapeDtypeStruct(packed.shape, F32),
        scratch_shapes=[pltpu.VMEM((8,) + packed.shape, F32), pltpu.SemaphoreType.DMA((7,)),
                        pltpu.SemaphoreType.DMA((7,))],
        compiler_params=pltpu.CompilerParams(vmem_limit_bytes=VMEM_LIMIT),
    )(packed)


LATE = BIG[1:]


def _local_step(x, target, w_in, late_shards, core, lam_re, lam_im, b_re, b_im, c_re, c_im, d_skip,
                log_step, sinks, rel_bias_table, ln_gain, ln_bias):
    t = x.shape[0]
    bucket = jnp.asarray(_bucket_table())
    p_re, p_im, b_bd, c_bd = _ssm_prepare(lam_re, lam_im, b_re, b_im, c_re, c_im, log_step)
    bias = _bias_build(rel_bias_table, bucket)

    xb, xbt = _cast_and_transpose(x, name="cast_x")
    act = dict(out_dtype=BF16)
    proj, landed = _matmul(xb, w_in, "nn", name="proj", tm=2048, tn=512,
                           exchange=_gather_exchange(late_shards, LATE), **act)
    w_glu, w_bs, w_ba, w_out = _pass_on(list(landed), LATE)
    y_ssm, g_in, car_re, car_im = _ssm_forward(proj, p_re, p_im, b_bd, c_bd, d_skip)
    glu = _matmul(g_in, w_glu, "nn", name="glu", tm=1024, tn=2048, **act)
    (h_ssm,) = _rowwise(functools.partial(_ssm_gate_fwd_body), "ssm_gate_fwd", t,
                        [(glu, 2 * D_SSM, 0), (proj, D_SSM, C_ZS)], [((t, D_SSM), BF16, D_SSM, 0)])
    attn, lse = _attn_forward(proj, sinks, bias)
    (h_attn,) = _rowwise(functools.partial(_attn_gate_fwd_body), "attn_gate_fwd", t,
                         [(attn, D_ATTN, 0), (proj, D_ATTN, C_ZA)], [((t, D_ATTN), BF16, D_ATTN, 0)])
    bs = _matmul(h_ssm, w_bs, "nn", name="branch_ssm", tm=1024, tn=2048, **act)
    ba = _matmul(h_attn, w_ba, "nn", name="branch_attn", tm=1024, tn=2048, **act)
    gl_in = (proj, 2 * D_MODEL, C_GL)
    (merged,) = _rowwise(functools.partial(_merge_fwd_body), "merge_fwd", t,
                         [(bs, D_MODEL, 0), (ba, D_MODEL, 0), gl_in], [((t, D_MODEL), BF16, D_MODEL, 0)])
    out = _matmul(merged, w_out, "nn", name="out_proj", tm=1024, tn=1024)
    dr, drb, loss, g_gain, g_bias = _rowwise(
        functools.partial(_ln_loss_body), "ln_loss", t,
        [(x, D_MODEL, 0), (out, D_MODEL, 0), (target, D_MODEL, 0), (ln_gain, None, 0), (ln_bias, None, 0)],
        [((t, D_MODEL), F32, D_MODEL, 0), ((t, D_MODEL), BF16, D_MODEL, 0), ((1, 1), F32, None, 0),
         ((1, D_MODEL), F32, None, 0), ((1, D_MODEL), F32, None, 0)])

    g_w_out = _matmul(merged, drb, "tn", name="g_w_out", tm=1024, tn=512, tk=1024)
    d_merged = _matmul(drb, w_out, "nt", name="d_merged", tm=1024, tn=1024, **act)
    d_bs, d_ba, dproj = _rowwise(
        functools.partial(_merge_bwd_body), "merge_bwd", t,
        [(d_merged, D_MODEL, 0), (bs, D_MODEL, 0), (ba, D_MODEL, 0), gl_in],
        [((t, D_MODEL), BF16, D_MODEL, 0), ((t, D_MODEL), BF16, D_MODEL, 0),
         ((t, D_IN), BF16, 2 * D_MODEL, C_GL)])
    g_w_bs = _matmul(h_ssm, d_bs, "tn", name="g_w_branch_ssm", tm=1024, tn=512, tk=1024)
    d_h_ssm = _matmul(d_bs, w_bs, "nt", name="d_h_ssm", tm=1024, tn=1024, **act)
    g_w_ba = _matmul(h_attn, d_ba, "tn", name="g_w_branch_attn", tm=1024, tn=512, tk=1024)
    d_h_attn = _matmul(d_ba, w_ba, "nt", name="d_h_attn", tm=1024, tn=1024, **act)

    d_attn, dproj = _rowwise(
        functools.partial(_attn_gate_bwd_body), "attn_gate_bwd", t,
        [(d_h_attn, D_ATTN, 0), (attn, D_ATTN, 0), (proj, D_ATTN, C_ZA), (dproj, None, "any")],
        [((t, D_ATTN), BF16, D_ATTN, 0), ((t, D_IN), BF16, D_ATTN, C_ZA)], aliases={3: 1})
    dproj, d_kv, g_bias_full, g_sinks = _attn_backward(proj, sinks, bias, d_attn, attn, lse, dproj)
    (dproj,) = _rowwise(functools.partial(_place_body), "place_dkv", t,
                        [(d_kv, 2 * D_KV, 0), (dproj, None, "any")],
                        [((t, D_IN), BF16, 2 * D_KV, C_K)], aliases={1: 0})
    g_table = _bias_grad(g_bias_full, bucket)

    d_glu, dproj = _rowwise(
        functools.partial(_ssm_gate_bwd_body), "ssm_gate_bwd", t,
        [(d_h_ssm, D_SSM, 0), (glu, 2 * D_SSM, 0), (proj, D_SSM, C_ZS), (dproj, None, "any")],
        [((t, 2 * D_SSM), BF16, 2 * D_SSM, 0), ((t, D_IN), BF16, D_SSM, C_ZS)], aliases={3: 1})
    g_w_glu = _matmul(g_in, d_glu, "tn", name="g_w_glu", tm=1024, tn=512, tk=1024)
    d_g_in = _matmul(d_glu, w_glu, "nt", name="d_g_in", tm=1024, tn=1024, **act)
    dproj, g_bbd, g_cbd, g_lam_re, g_lam_im, g_d = _ssm_backward(
        proj, d_g_in, y_ssm, car_re, car_im, p_re, p_im, b_bd, c_bd, d_skip, dproj)
    g_lr, g_li, g_br, g_bi, g_cr, g_ci, g_ls = _ssm_param_grads(
        lam_re, lam_im, b_re, b_im, log_step, g_lam_re, g_lam_im, g_bbd, g_cbd)

    late = dict(w_glu=g_w_glu, w_branch_ssm=g_w_bs, w_branch_attn=g_w_ba, w_out=g_w_out)
    g_w_in, late_slots = _matmul(xbt, dproj, "nn", name="g_w_in", tm=512, tn=512,
                                 exchange=_chip_exchange(_pair_sums(late, LATE, core), LATE))
    grad_x, in_slots = _matmul(dproj, w_in, "nt", name="grad_x", tm=512, tn=512, res=dr, res_scale=ALPHA,
                               exchange=_chip_exchange(_pair_sums(dict(w_in=g_w_in), BIG[:1], core), BIG[:1]))
    big = {**_shard_halves(in_slots, BIG[:1], core), **_shard_halves(late_slots, LATE, core)}
    small = dict(ssm_lambda_re=g_lr, ssm_lambda_im=g_li, ssm_b_re=g_br, ssm_b_im=g_bi, ssm_c_re=g_cr,
                 ssm_c_im=g_ci, ssm_d=g_d, ssm_log_step=g_ls, attn_sinks=g_sinks[:, :N_Q_HEADS],
                 rel_bias_table=g_table, ln_gain=g_gain, ln_bias=g_bias)
    return loss, grad_x, big, small


WEIGHTS = ("w_in", "ssm_lambda_re", "ssm_lambda_im", "ssm_b_re", "ssm_b_im", "ssm_c_re", "ssm_c_im", "ssm_d",
           "ssm_log_step", "w_glu", "attn_sinks", "rel_bias_table", "w_branch_ssm", "w_branch_attn", "w_out",
           "ln_gain", "ln_bias")


def kernel(x, w_in, ssm_lambda_re, ssm_lambda_im, ssm_b_re, ssm_b_im, ssm_c_re, ssm_c_im, ssm_d, ssm_log_step, w_glu, attn_sinks, rel_bias_table, w_branch_ssm, w_branch_attn, w_out, ln_gain, ln_bias, loss_target, m_w_in, m_ssm_lambda_re, m_ssm_lambda_im, m_ssm_b_re, m_ssm_b_im, m_ssm_c_re, m_ssm_c_im, m_ssm_d, m_ssm_log_step, m_w_glu, m_attn_sinks, m_rel_bias_table, m_w_branch_ssm, m_w_branch_attn, m_w_out, m_ln_gain, m_ln_bias, v_w_in, v_ssm_lambda_re, v_ssm_lambda_im, v_ssm_b_re, v_ssm_b_im, v_ssm_c_re, v_ssm_c_im, v_ssm_d, v_ssm_log_step, v_w_glu, v_attn_sinks, v_rel_bias_table, v_w_branch_ssm, v_w_branch_attn, v_w_out, v_ln_gain, v_ln_bias):
    given = dict(locals())
    w = {k: given[k] for k in WEIGHTS}
    m = {k: given["m_" + k] for k in WEIGHTS}
    v = {k: given["v_" + k] for k in WEIGHTS}

    core = lax.axis_index("c").astype(jnp.int32).reshape(1)
    shards = {k: _cast(w[k][0], BF16, name="cast_" + k) for k in BIG}
    (w_in_full,) = _all_gather(shards, BIG[:1])
    loss, grad_x, g_shard, g_small = _local_step(
        x[0], loss_target[0], w_in_full, shards, core, ssm_lambda_re[0], ssm_lambda_im[0], ssm_b_re[0],
        ssm_b_im[0], ssm_c_re[0], ssm_c_im[0], ssm_d, ssm_log_step, attn_sinks, rel_bias_table, ln_gain, ln_bias)

    g_packed = _small_allreduce(_pack(g_small, loss))
    loss_sum = _unpack(g_packed)[1]

    grad, delta, new_m, new_v = {}, {}, {}, {}
    for k in BIG:
        grad[k], delta[k], new_m[k], new_v[k] = _adamw_shard(w[k], m[k], v[k], g_shard[k], core,
                                                             name="adamw_" + k, rows=SHARE_ROWS[k])
    gs, ds, ms, vs = _adamw(_pack(w), _pack(m), _pack(v), [g_packed], name="adamw_small")
    for dst, packed in ((grad, gs), (delta, ds), (new_m, ms), (new_v, vs)):
        dst.update(_unpack(packed)[0])

    return (loss_sum, grad_x[None], *[grad[k] for k in WEIGHTS], *[delta[k] for k in WEIGHTS],
            *[new_m[k] for k in WEIGHTS], *[new_v[k] for k in WEIGHTS])
```

```python
import functools
import math

import numpy as np
import jax
import jax.numpy as jnp
from jax import lax
from jax.experimental import pallas as pl
from jax.experimental.pallas import tpu as pltpu

F32 = jnp.float32
BF16 = jnp.bfloat16

D_MODEL = 2048
D_SSM = 1024
SSM_GROUP = 16
N_GROUPS = 64
SSM_STATE = 64
N_LANES = N_GROUPS * SSM_STATE
N_Q_HEADS = 16
N_KV_HEADS = 4
HEAD_DIM = 64
Q_PER_KV = 4
D_ATTN = 1024
D_KV = 256
WINDOW = 128
BLOCK = 128
N_BUCKETS = 32
MAX_DISTANCE = 128
D_IN = 8704
ALPHA = 2.0 ** 0.25
LN_EPS = 1e-5
NEG_INF = -1e30
ATTN_SCALE = HEAD_DIM ** -0.5

C_U, C_ZS, C_Q, C_K, C_V, C_ZA, C_GL = 0, 1024, 2048, 3072, 3328, 3584, 4608

ADAM_LR = 0.001
ADAM_B1 = 0.9
ADAM_B2 = 0.999
ADAM_EPS = 1e-08
ADAM_WD = 0.01
ADAM_STEP = 10

N_CHIPS = 4
MESH = pl.DeviceIdType.MESH

SSM_CHUNK = 256
SEG_LEN = SSM_CHUNK // 8
SLAB_LANES = 512
N_SLABS = N_LANES // SLAB_LANES
SLAB_CH = D_SSM // N_SLABS

VMEM_LIMIT = 60 * 1024 * 1024


def _params(sem=None, **kw):
    return pltpu.CompilerParams(dimension_semantics=sem, vmem_limit_bytes=VMEM_LIMIT, **kw)


_DIMS = {"nn": (((1,), (0,)), ((), ())), "nt": (((1,), (1,)), ((), ())), "tn": (((0,), (0,)), ((), ()))}


class _Exchange:
    def __init__(self, ins, out_shape, n_sems, copies):
        self.ins, self.out_shape, self.n_sems, self.copies = list(ins), list(out_shape), list(n_sems), copies

    def start(self, ins, outs, sems):
        remote, local = self.copies(ins, outs, sems)
        for cp in remote + local:
            cp.start()

    def finish(self, ins, outs, sems):
        remote, local = self.copies(ins, outs, sems)
        for cp in remote:
            cp.wait_recv()
        for cp in remote:
            cp.wait_send()
        for cp in local:
            cp.wait()


def _mm_body(*refs, dims, nk, res_scale, exchange, grid):
    n_ex_in = len(exchange.ins) if exchange else 0
    n_ex_out = len(exchange.out_shape) if exchange else 0
    n_in = 2 + (res_scale is not None) + n_ex_in
    a_ref, b_ref = refs[0], refs[1]
    r_ref = refs[2] if res_scale is not None else None
    ex_in = refs[n_in - n_ex_in:n_in]
    o_ref, ex_out = refs[n_in], refs[n_in + 1:n_in + 1 + n_ex_out]
    acc_ref, ex_sems = refs[n_in + 1 + n_ex_out], refs[n_in + 2 + n_ex_out:]
    i, j, k = pl.program_id(0), pl.program_id(1), pl.program_id(2)

    if exchange:
        @pl.when(jnp.logical_and(jnp.logical_and(i == 0, j == 0), k == 0))
        def _():
            exchange.start(ex_in, ex_out, ex_sems)

    part = lax.dot_general(a_ref[...].astype(BF16), b_ref[...].astype(BF16), _DIMS[dims],
                           preferred_element_type=F32)

    def finish(acc):
        if r_ref is not None:
            acc = acc + res_scale * r_ref[...]
        o_ref[...] = acc.astype(o_ref.dtype)

    if nk == 1:
        finish(part)
    else:
        @pl.when(k == 0)
        def _():
            acc_ref[...] = part

        @pl.when(k > 0)
        def _():
            acc_ref[...] += part

        @pl.when(k == nk - 1)
        def _():
            finish(acc_ref[...])

    if exchange:
        @pl.when(jnp.logical_and(jnp.logical_and(i == grid[0] - 1, j == grid[1] - 1), k == grid[2] - 1))
        def _():
            exchange.finish(ex_in, ex_out, ex_sems)


def _matmul(a, b, dims, *, name, out_dtype=F32, tm=512, tn=512, tk=None, res=None, res_scale=None, exchange=None):
    if dims == "nn":
        (m, kk), n = a.shape, b.shape[1]
    elif dims == "nt":
        (m, kk), n = a.shape, b.shape[0]
    else:
        (kk, m), n = a.shape, b.shape[1]
    tm, tn = min(tm, m), min(tn, n)
    tk = kk if tk is None else min(tk, kk)
    assert m % tm == 0 and n % tn == 0 and kk % tk == 0, (name, m, n, kk, tm, tn, tk)
    nk = kk // tk
    a_spec = {"nn": pl.BlockSpec((tm, tk), lambda i, j, k: (i, k)),
              "nt": pl.BlockSpec((tm, tk), lambda i, j, k: (i, k)),
              "tn": pl.BlockSpec((tk, tm), lambda i, j, k: (k, i))}[dims]
    b_spec = {"nn": pl.BlockSpec((tk, tn), lambda i, j, k: (k, j)),
              "nt": pl.BlockSpec((tn, tk), lambda i, j, k: (j, k)),
              "tn": pl.BlockSpec((tk, tn), lambda i, j, k: (k, j))}[dims]
    in_specs, args = [a_spec, b_spec], [a, b]
    if res is not None:
        in_specs.append(pl.BlockSpec((tm, tn), lambda i, j, k: (i, j)))
        args.append(res)
    grid = (m // tm, n // tn, nk)
    out_specs = [pl.BlockSpec((tm, tn), lambda i, j, k: (i, j))]
    out_shape = [jax.ShapeDtypeStruct((m, n), out_dtype)]
    scratch = [pltpu.VMEM((tm, tn), F32)]
    if exchange:
        in_specs += [pl.BlockSpec(memory_space=pl.ANY)] * len(exchange.ins)
        args += exchange.ins
        out_specs += [pl.BlockSpec(memory_space=pl.ANY)] * len(exchange.out_shape)
        out_shape += exchange.out_shape
        scratch += [pltpu.SemaphoreType.DMA((s,)) for s in exchange.n_sems]
    outs = pl.pallas_call(
        functools.partial(_mm_body, dims=dims, nk=nk, res_scale=res_scale if res is not None else None,
                          exchange=exchange, grid=grid),
        name=name, grid=grid, in_specs=in_specs, out_specs=out_specs, out_shape=out_shape, scratch_shapes=scratch,
        compiler_params=_params(("arbitrary",) * 3 if exchange else ("parallel", "parallel", "arbitrary")),
    )(*args)
    return (outs[0], outs[1:]) if exchange else outs[0]


def _sigmoid(v):
    return 1.0 / (1.0 + jnp.exp(-v))


def _silu_and_grad(z):
    s = _sigmoid(z)
    return z * s, s * (1.0 + z * (1.0 - s))


def _cast_body(x_ref, o_ref):
    o_ref[...] = x_ref[...].astype(o_ref.dtype)


def _cast_and_transpose_body(x_ref, o_ref, ot_ref):
    xb = x_ref[...].astype(BF16)
    o_ref[...] = xb
    ot_ref[...] = xb.T


def _cast_and_transpose(x, *, name, rows=512):
    m, n = x.shape
    rows = min(rows, m)
    return pl.pallas_call(
        functools.partial(_cast_and_transpose_body), name=name, grid=(m // rows,),
        in_specs=[pl.BlockSpec((rows, n), lambda i: (i, 0))],
        out_specs=[pl.BlockSpec((rows, n), lambda i: (i, 0)), pl.BlockSpec((n, rows), lambda i: (0, i))],
        out_shape=[jax.ShapeDtypeStruct((m, n), BF16), jax.ShapeDtypeStruct((n, m), BF16)],
        compiler_params=_params(("parallel",)),
    )(x)


def _cast(x, dtype, *, name, rows=512):
    m, n = x.shape
    rows = min(rows, m)
    return pl.pallas_call(
        functools.partial(_cast_body), name=name, grid=(m // rows,),
        in_specs=[pl.BlockSpec((rows, n), lambda i: (i, 0))],
        out_specs=pl.BlockSpec((rows, n), lambda i: (i, 0)),
        out_shape=jax.ShapeDtypeStruct((m, n), dtype),
        compiler_params=_params(("parallel",)),
    )(x)


def _lam_bar(lr, li, ls):
    step = jnp.exp(ls)
    er = jnp.exp(lr * step)
    return step, er * jnp.cos(li * step), er * jnp.sin(li * step)


def _ssm_pow_body(lr_ref, li_ref, ls_ref, pr_ref, pi_ref):
    _, ar, ai = _lam_bar(lr_ref[...], li_ref[...], ls_ref[...])
    cr, ci = ar, ai
    for i in range(SEG_LEN):
        pr_ref[pl.ds(i, 1), :] = cr
        pi_ref[pl.ds(i, 1), :] = ci
        cr, ci = cr * ar - ci * ai, cr * ai + ci * ar


def _ssm_bbar_body(lr_ref, li_ref, ls_ref, br_ref, bi_ref, or_ref, oi_ref):
    lr, li = lr_ref[...], li_ref[...]
    _, ar, ai = _lam_bar(lr, li, ls_ref[...])
    d = lr * lr + li * li
    ir, ii = lr / d, -li / d
    nr, ni = ar - 1.0, ai
    cr, ci = nr * ir - ni * ii, nr * ii + ni * ir
    br, bi = br_ref[...], bi_ref[...]
    or_ref[...] = cr * br - ci * bi
    oi_ref[...] = cr * bi + ci * br


def _ssm_param_bwd_body(lr_ref, li_ref, ls_ref, br_ref, bi_ref, glr_ref, gli_ref, gbr_ref, gbi_ref,
                        dlr_ref, dli_ref, dls_ref, dbr_ref, dbi_ref):
    lr, li = lr_ref[...], li_ref[...]
    step, ar, ai = _lam_bar(lr, li, ls_ref[...])
    d = lr * lr + li * li
    ir, ii = lr / d, -li / d
    nr, ni = ar - 1.0, ai
    cr, ci = nr * ir - ni * ii, nr * ii + ni * ir
    br, bi, gbr, gbi = br_ref[...], bi_ref[...], gbr_ref[...], gbi_ref[...]
    dbr_ref[...] = cr * gbr + ci * gbi
    dbi_ref[...] = cr * gbi - ci * gbr
    gcr = jnp.sum(br * gbr + bi * gbi, axis=1, keepdims=True)
    gci = jnp.sum(br * gbi - bi * gbr, axis=1, keepdims=True)
    gnr, gni = ir * gcr + ii * gci, ir * gci - ii * gcr
    gir, gii = nr * gcr + ni * gci, nr * gci - ni * gcr
    gtr, gti = glr_ref[...] + gnr, gli_ref[...] + gni
    i2r, i2i = ir * ir - ii * ii, 2.0 * ir * ii
    g1r, g1i = -(i2r * gir + i2i * gii), -(i2r * gii - i2i * gir)
    g2r, g2i = step * (ar * gtr + ai * gti), step * (ar * gti - ai * gtr)
    mr, mi = lr * ar - li * ai, lr * ai + li * ar
    dlr_ref[...] = g1r + g2r
    dli_ref[...] = g1i + g2i
    dls_ref[...] = (mr * gtr + mi * gti) * step


def _whole(shape):
    return pl.BlockSpec(shape, lambda *_: (0,) * len(shape))


def _ssm_prepare(lam_re, lam_im, b_re, b_im, c_re, c_im, log_step):
    row = lambda a: a.reshape(1, N_LANES)
    col = lambda a: a.reshape(N_LANES, 1)
    ls = jnp.repeat(log_step.reshape(N_GROUPS), SSM_STATE)
    p_re, p_im = pl.pallas_call(
        functools.partial(_ssm_pow_body), name="ssm_pow",
        in_specs=[_whole((1, N_LANES))] * 3, out_specs=[_whole((SEG_LEN, N_LANES))] * 2,
        out_shape=[jax.ShapeDtypeStruct((SEG_LEN, N_LANES), F32)] * 2, grid=(1,),
    )(row(lam_re), row(lam_im), row(ls))
    bb_re, bb_im = pl.pallas_call(
        functools.partial(_ssm_bbar_body), name="ssm_bbar",
        in_specs=[_whole((N_LANES, 1))] * 3 + [_whole((N_LANES, SSM_GROUP))] * 2,
        out_specs=[_whole((N_LANES, SSM_GROUP))] * 2,
        out_shape=[jax.ShapeDtypeStruct((N_LANES, SSM_GROUP), F32)] * 2, grid=(1,),
    )(col(lam_re), col(lam_im), col(ls), b_re.reshape(N_LANES, SSM_GROUP), b_im.reshape(N_LANES, SSM_GROUP))
    eye = jnp.eye(8, dtype=F32)

    def b_slabs(bb):
        t = bb.reshape(N_SLABS, 8, SSM_STATE, SSM_GROUP).transpose(0, 1, 3, 2)
        return (t[:, :, :, None, :] * eye[None, :, None, :, None]).reshape(N_SLABS, SLAB_CH, SLAB_LANES)

    def c_slabs(c):
        t = c.reshape(N_SLABS, 8, SSM_GROUP, SSM_STATE).transpose(0, 1, 3, 2)
        return (t[:, :, :, None, :] * eye[None, :, None, :, None]).reshape(N_SLABS, SLAB_LANES, SLAB_CH)

    b_bd = jnp.concatenate([b_slabs(bb_re), b_slabs(bb_im)], axis=2).astype(BF16)
    c_bd = jnp.concatenate([c_slabs(c_re.reshape(N_GROUPS, SSM_GROUP, SSM_STATE)),
                            -c_slabs(c_im.reshape(N_GROUPS, SSM_GROUP, SSM_STATE))], axis=1).astype(BF16)
    return p_re, p_im, b_bd, c_bd


def _diag_blocks_b(g):
    t = g.reshape(N_SLABS, 8, SSM_GROUP, 8, SSM_STATE)
    t = jnp.stack([t[:, i, :, i, :] for i in range(8)], axis=1)
    return t.transpose(0, 1, 3, 2).reshape(N_LANES, SSM_GROUP)


def _diag_blocks_c(g):
    t = g.reshape(N_SLABS, 8, SSM_STATE, 8, SSM_GROUP)
    t = jnp.stack([t[:, i, :, i, :] for i in range(8)], axis=1)
    return t.transpose(0, 1, 3, 2).reshape(N_GROUPS, SSM_GROUP, SSM_STATE)


def _ssm_param_grads(lam_re, lam_im, b_re, b_im, log_step, g_lam_re, g_lam_im, g_bbd, g_cbd):
    col = lambda a: a.reshape(N_LANES, 1)
    ls = jnp.repeat(log_step.reshape(N_GROUPS), SSM_STATE)
    gbr = _diag_blocks_b(g_bbd[:, :, :SLAB_LANES])
    gbi = _diag_blocks_b(g_bbd[:, :, SLAB_LANES:])
    outs = pl.pallas_call(
        functools.partial(_ssm_param_bwd_body), name="ssm_param_bwd", grid=(1,),
        in_specs=[_whole((N_LANES, 1))] * 3 + [_whole((N_LANES, SSM_GROUP))] * 2 + [_whole((N_LANES, 1))] * 2
        + [_whole((N_LANES, SSM_GROUP))] * 2,
        out_specs=[_whole((N_LANES, 1))] * 3 + [_whole((N_LANES, SSM_GROUP))] * 2,
        out_shape=[jax.ShapeDtypeStruct((N_LANES, 1), F32)] * 3 + [jax.ShapeDtypeStruct((N_LANES, SSM_GROUP), F32)] * 2,
    )(col(lam_re), col(lam_im), col(ls), b_re.reshape(N_LANES, SSM_GROUP), b_im.reshape(N_LANES, SSM_GROUP),
      col(g_lam_re), col(g_lam_im), gbr, gbi)
    dlr, dli, dls, dbr, dbi = outs
    d_c_re = _diag_blocks_c(g_cbd[:, :SLAB_LANES, :])
    d_c_im = -_diag_blocks_c(g_cbd[:, SLAB_LANES:, :])
    return (dlr.reshape(1, N_GROUPS, SSM_STATE), dli.reshape(1, N_GROUPS, SSM_STATE),
            dbr.reshape(1, N_GROUPS, SSM_STATE, SSM_GROUP), dbi.reshape(1, N_GROUPS, SSM_STATE, SSM_GROUP),
            d_c_re[None], d_c_im[None], dls.reshape(N_GROUPS, SSM_STATE).sum(axis=1).reshape(1, N_GROUPS))


def _bcast8(v):
    return jnp.broadcast_to(v, (8, v.shape[1]))


def _segment_permutation():
    p = np.zeros((SSM_CHUNK, SSM_CHUNK), np.float32)
    rows = np.arange(SSM_CHUNK)
    p[rows, (rows % 8) * SEG_LEN + rows // 8] = 1.0
    return p


def _permute_exact(perm, val, pieces):
    out, rest = None, val
    for n in range(pieces):
        part = rest.astype(BF16)
        moved = jnp.dot(perm, part, preferred_element_type=F32)
        out = moved if out is None else out + moved
        if n + 1 < pieces:
            rest = rest - part.astype(F32)
    return out


def _scan_buffer():
    return pltpu.VMEM((SSM_CHUNK, N_LANES), F32)


def _lanes(k):
    return pl.ds(k * SLAB_LANES, SLAB_LANES)


def _tile(i):
    return pl.ds(i * 8 if isinstance(i, int) else pl.multiple_of(i * 8, 8), 8)


def _seg_get(ref, k, i):
    return ref[_tile(i), _lanes(k)]


def _seg_put(ref, k, i, val):
    ref[_tile(i), _lanes(k)] = val


def _slab_get(ref, k):
    return ref[:, _lanes(k)]


def _slab_put(ref, k, val):
    ref[:, _lanes(k)] = val


def _scan_forward(s_re, s_im, p_re, p_im, car_re, car_im, sp_re=None, sp_im=None):
    for k in range(N_SLABS):
        ln = pl.ds(k * SLAB_LANES, SLAB_LANES)
        ar, ai = _bcast8(p_re[pl.ds(0, 1), ln]), _bcast8(p_im[pl.ds(0, 1), ln])

        def step(i, s, k=k, ar=ar, ai=ai):
            sr, si = s
            nr = ar * sr - ai * si + _seg_get(s_re, k, i)
            ni = ar * si + ai * sr + _seg_get(s_im, k, i)
            _seg_put(s_re, k, i, nr)
            _seg_put(s_im, k, i, ni)
            return nr, ni

        zero = jnp.zeros((8, SLAB_LANES), F32)
        er, ei = lax.fori_loop(0, SEG_LEN, step, (zero, zero), unroll=4)
        lr, li = p_re[pl.ds(SEG_LEN - 1, 1), ln], p_im[pl.ds(SEG_LEN - 1, 1), ln]
        cr, ci = car_re[pl.ds(0, 1), ln], car_im[pl.ds(0, 1), ln]
        rows_r, rows_i = [], []
        for r in range(8):
            rows_r.append(cr)
            rows_i.append(ci)
            cr, ci = er[r:r + 1] + lr * cr - li * ci, ei[r:r + 1] + lr * ci + li * cr
        pr8, pi8 = jnp.concatenate(rows_r, axis=0), jnp.concatenate(rows_i, axis=0)
        car_re[:, ln] = _bcast8(cr)
        car_im[:, ln] = _bcast8(ci)
        if sp_re is not None:
            sp_re[:, ln] = pr8
            sp_im[:, ln] = pi8

        def fix(i, _, k=k, ln=ln, pr8=pr8, pi8=pi8):
            qr, qi = _bcast8(p_re[pl.ds(i, 1), ln]), _bcast8(p_im[pl.ds(i, 1), ln])
            _seg_put(s_re, k, i, _seg_get(s_re, k, i) + qr * pr8 - qi * pi8)
            _seg_put(s_im, k, i, _seg_get(s_im, k, i) + qr * pi8 + qi * pr8)
            return 0

        lax.fori_loop(0, SEG_LEN, fix, 0, unroll=4)


def _scan_backward(g_re, g_im, s_re, s_im, sp_re, sp_im, p_re, p_im, car_re, car_im, acc_re, acc_im):
    for k in range(N_SLABS):
        ln = pl.ds(k * SLAB_LANES, SLAB_LANES)
        ar, ai = _bcast8(p_re[pl.ds(0, 1), ln]), -_bcast8(p_im[pl.ds(0, 1), ln])

        def step(j, s, k=k, ar=ar, ai=ai):
            i = SEG_LEN - 1 - j
            sr, si = s
            nr = ar * sr - ai * si + _seg_get(g_re, k, i)
            ni = ar * si + ai * sr + _seg_get(g_im, k, i)
            _seg_put(g_re, k, i, nr)
            _seg_put(g_im, k, i, ni)
            return nr, ni

        zero = jnp.zeros((8, SLAB_LANES), F32)
        er, ei = lax.fori_loop(0, SEG_LEN, step, (zero, zero), unroll=4)
        lr, li = p_re[pl.ds(SEG_LEN - 1, 1), ln], -p_im[pl.ds(SEG_LEN - 1, 1), ln]
        cr, ci = car_re[pl.ds(0, 1), ln], car_im[pl.ds(0, 1), ln]
        rows_r, rows_i = [None] * 8, [None] * 8
        for r in range(7, -1, -1):
            rows_r[r], rows_i[r] = cr, ci
            cr, ci = er[r:r + 1] + lr * cr - li * ci, ei[r:r + 1] + lr * ci + li * cr
        nr8, ni8 = jnp.concatenate(rows_r, axis=0), jnp.concatenate(rows_i, axis=0)
        car_re[:, ln] = _bcast8(cr)
        car_im[:, ln] = _bcast8(ci)

        def fix(i, acc, k=k, ln=ln, nr8=nr8, ni8=ni8):
            qr = _bcast8(p_re[pl.ds(SEG_LEN - 1 - i, 1), ln])
            qi = -_bcast8(p_im[pl.ds(SEG_LEN - 1 - i, 1), ln])
            gr = _seg_get(g_re, k, i) + qr * nr8 - qi * ni8
            gi = _seg_get(g_im, k, i) + qr * ni8 + qi * nr8
            _seg_put(g_re, k, i, gr)
            _seg_put(g_im, k, i, gi)
            return gr, gi

        def prod(xr, xi, gr, gi):
            return xr * gr + xi * gi, xr * gi - xi * gr

        gr, gi = fix(0, None)
        a_r, a_i = prod(sp_re[:, ln], sp_im[:, ln], gr, gi)

        def fix_acc(i, acc, k=k, fix=fix):
            gr, gi = fix(i, None)
            dr, di = prod(_seg_get(s_re, k, i - 1), _seg_get(s_im, k, i - 1), gr, gi)
            return acc[0] + dr, acc[1] + di

        acc = (a_r, a_i)
        for i in range(1, 4):
            acc = fix_acc(i, acc)
        a_r, a_i = lax.fori_loop(4, SEG_LEN, fix_acc, acc, unroll=4)
        acc_re[:, ln] += a_r
        acc_im[:, ln] += a_i


def _gelu_and_grad(y):
    cdf = 0.5 * (1.0 + lax.erf(y * (2.0 ** -0.5)))
    pdf = jnp.exp(-0.5 * y * y) * (1.0 / math.sqrt(2.0 * math.pi))
    return y * cdf, cdf + y * pdf


def _ssm_fwd_body(u_ref, bbd_ref, cbd_ref, pre_ref, pim_ref, d_ref, perm_ref, unperm_ref,
                  y_ref, gin_ref, cre_out, cim_out, s_re, s_im, car_re, car_im, yp):
    c = pl.program_id(0)

    @pl.when(c == 0)
    def _():
        car_re[...] = jnp.zeros_like(car_re)
        car_im[...] = jnp.zeros_like(car_im)

    cre_out[...] = car_re[...]
    cim_out[...] = car_im[...]
    u = u_ref[...].astype(F32)
    up = jnp.dot(perm_ref[...], u_ref[...].astype(BF16), preferred_element_type=F32).astype(BF16)
    for k in range(N_SLABS):
        bu = jnp.dot(up[:, k * SLAB_CH:(k + 1) * SLAB_CH], bbd_ref[k], preferred_element_type=F32)
        _slab_put(s_re, k, bu[:, :SLAB_LANES])
        _slab_put(s_im, k, bu[:, SLAB_LANES:])
    _scan_forward(s_re, s_im, pre_ref, pim_ref, car_re, car_im)
    for k in range(N_SLABS):
        yp[:, pl.ds(k * SLAB_CH, SLAB_CH)] = (
            jnp.dot(_slab_get(s_re, k).astype(BF16), cbd_ref[k, :SLAB_LANES, :], preferred_element_type=F32)
            + jnp.dot(_slab_get(s_im, k).astype(BF16), cbd_ref[k, SLAB_LANES:, :], preferred_element_type=F32))
    y = _permute_exact(unperm_ref[...], yp[...], 2) + d_ref[...] * u
    y_ref[...] = y.astype(y_ref.dtype)
    gin_ref[...] = _gelu_and_grad(y)[0].astype(BF16)


def _ssm_forward(proj, p_re, p_im, b_bd, c_bd, d_skip):
    t = proj.shape[0]
    nc = t // SSM_CHUNK
    perm = _segment_permutation()
    return pl.pallas_call(
        functools.partial(_ssm_fwd_body), name="ssm_fwd", grid=(nc,),
        in_specs=[pl.BlockSpec((SSM_CHUNK, D_SSM), lambda c: (c, C_U // D_SSM)),
                  _whole(b_bd.shape), _whole(c_bd.shape), _whole(p_re.shape), _whole(p_im.shape),
                  _whole((1, D_SSM)), _whole(perm.shape), _whole(perm.shape)],
        out_specs=[pl.BlockSpec((SSM_CHUNK, D_SSM), lambda c: (c, 0)),
                   pl.BlockSpec((SSM_CHUNK, D_SSM), lambda c: (c, 0)),
                   pl.BlockSpec((None, 8, N_LANES), lambda c: (c, 0, 0)),
                   pl.BlockSpec((None, 8, N_LANES), lambda c: (c, 0, 0))],
        out_shape=[jax.ShapeDtypeStruct((t, D_SSM), BF16), jax.ShapeDtypeStruct((t, D_SSM), BF16),
                   jax.ShapeDtypeStruct((nc, 8, N_LANES), F32), jax.ShapeDtypeStruct((nc, 8, N_LANES), F32)],
        scratch_shapes=[_scan_buffer(), _scan_buffer(),
                        pltpu.VMEM((8, N_LANES), F32), pltpu.VMEM((8, N_LANES), F32),
                        pltpu.VMEM((SSM_CHUNK, D_SSM), F32)],
        compiler_params=_params(("arbitrary",)),
    )(proj, b_bd, c_bd, p_re, p_im, d_skip, jnp.asarray(perm, BF16), jnp.asarray(perm.T, BF16))


def _ssm_bwd_body(u_ref, dgin_ref, y_ref, cre_in, cim_in, bbd_ref, cbd_ref, pre_ref, pim_ref, d_ref, perm_ref,
                  unperm_ref, dproj_in,
                  du_ref, gb_ref, gc_ref, glr_ref, gli_ref, gd_ref,
                  s_re, s_im, g_re, g_im, sp_re, sp_im, car_re, car_im, gcar_re, gcar_im, acc_re, acc_im, dup):
    del dproj_in
    c = pl.program_id(0)
    nc = pl.num_programs(0)

    @pl.when(c == 0)
    def _():
        gcar_re[...] = jnp.zeros_like(gcar_re)
        gcar_im[...] = jnp.zeros_like(gcar_im)
        acc_re[...] = jnp.zeros_like(acc_re)
        acc_im[...] = jnp.zeros_like(acc_im)
        gb_ref[...] = jnp.zeros_like(gb_ref)
        gc_ref[...] = jnp.zeros_like(gc_ref)
        gd_ref[...] = jnp.zeros_like(gd_ref)

    car_re[...] = cre_in[...]
    car_im[...] = cim_in[...]
    u = u_ref[...].astype(F32)
    dy = dgin_ref[...].astype(F32) * _gelu_and_grad(y_ref[...].astype(F32))[1]
    gd_ref[...] += jnp.sum(dy * u, axis=0, keepdims=True)
    up = jnp.dot(perm_ref[...], u.astype(BF16), preferred_element_type=F32).astype(BF16)
    dyp = jnp.dot(perm_ref[...], dy.astype(BF16), preferred_element_type=F32).astype(BF16)
    for k in range(N_SLABS):
        ch = slice(k * SLAB_CH, (k + 1) * SLAB_CH)
        bu = jnp.dot(up[:, ch], bbd_ref[k], preferred_element_type=F32)
        _slab_put(s_re, k, bu[:, :SLAB_LANES])
        _slab_put(s_im, k, bu[:, SLAB_LANES:])
        ds = lax.dot_general(dyp[:, ch], cbd_ref[k], _DIMS["nt"], preferred_element_type=F32)
        _slab_put(g_re, k, ds[:, :SLAB_LANES])
        _slab_put(g_im, k, ds[:, SLAB_LANES:])
    _scan_forward(s_re, s_im, pre_ref, pim_ref, car_re, car_im, sp_re, sp_im)
    _scan_backward(g_re, g_im, s_re, s_im, sp_re, sp_im, pre_ref, pim_ref, gcar_re, gcar_im, acc_re, acc_im)
    for k in range(N_SLABS):
        ch = slice(k * SLAB_CH, (k + 1) * SLAB_CH)
        uk, dyk = up[:, ch], dyp[:, ch]
        sr, si = _slab_get(s_re, k).astype(BF16), _slab_get(s_im, k).astype(BF16)
        gr, gi = _slab_get(g_re, k).astype(BF16), _slab_get(g_im, k).astype(BF16)
        gc_ref[k, :SLAB_LANES, :] += lax.dot_general(sr, dyk, _DIMS["tn"], preferred_element_type=F32)
        gc_ref[k, SLAB_LANES:, :] += lax.dot_general(si, dyk, _DIMS["tn"], preferred_element_type=F32)
        gb_ref[k, :, :SLAB_LANES] += lax.dot_general(uk, gr, _DIMS["tn"], preferred_element_type=F32)
        gb_ref[k, :, SLAB_LANES:] += lax.dot_general(uk, gi, _DIMS["tn"], preferred_element_type=F32)
        dup[:, pl.ds(k * SLAB_CH, SLAB_CH)] = (
            lax.dot_general(gr, bbd_ref[k, :, :SLAB_LANES], _DIMS["nt"], preferred_element_type=F32)
            + lax.dot_general(gi, bbd_ref[k, :, SLAB_LANES:], _DIMS["nt"], preferred_element_type=F32))
    du = _permute_exact(unperm_ref[...], dup[...], 2) + d_ref[...] * dy
    du_ref[...] = du.astype(du_ref.dtype)

    @pl.when(c == nc - 1)
    def _():
        glr_ref[...] = jnp.sum(acc_re[...], axis=0, keepdims=True)
        gli_ref[...] = jnp.sum(acc_im[...], axis=0, keepdims=True)


def _ssm_backward(proj, dg_in, y_ssm, car_re, car_im, p_re, p_im, b_bd, c_bd, d_skip, dproj):
    t = proj.shape[0]
    nc = t // SSM_CHUNK
    rev = lambda c: nc - 1 - c
    big = _scan_buffer
    small = lambda: pltpu.VMEM((8, N_LANES), F32)
    perm = _segment_permutation()
    outs = pl.pallas_call(
        functools.partial(_ssm_bwd_body), name="ssm_bwd", grid=(nc,),
        in_specs=[pl.BlockSpec((SSM_CHUNK, D_SSM), lambda c: (rev(c), C_U // D_SSM)),
                  pl.BlockSpec((SSM_CHUNK, D_SSM), lambda c: (rev(c), 0)),
                  pl.BlockSpec((SSM_CHUNK, D_SSM), lambda c: (rev(c), 0)),
                  pl.BlockSpec((None, 8, N_LANES), lambda c: (rev(c), 0, 0)),
                  pl.BlockSpec((None, 8, N_LANES), lambda c: (rev(c), 0, 0)),
                  _whole(b_bd.shape), _whole(c_bd.shape), _whole(p_re.shape), _whole(p_im.shape),
                  _whole((1, D_SSM)), _whole(perm.shape), _whole(perm.shape), pl.BlockSpec(memory_space=pl.ANY)],
        out_specs=[pl.BlockSpec((SSM_CHUNK, D_SSM), lambda c: (rev(c), C_U // D_SSM)),
                   _whole(b_bd.shape), _whole(c_bd.shape), _whole((1, N_LANES)), _whole((1, N_LANES)),
                   _whole((1, D_SSM))],
        out_shape=[jax.ShapeDtypeStruct(dproj.shape, dproj.dtype),
                   jax.ShapeDtypeStruct(b_bd.shape, F32), jax.ShapeDtypeStruct(c_bd.shape, F32),
                   jax.ShapeDtypeStruct((1, N_LANES), F32), jax.ShapeDtypeStruct((1, N_LANES), F32),
                   jax.ShapeDtypeStruct((1, D_SSM), F32)],
        scratch_shapes=[big(), big(), big(), big()] + [small() for _ in range(8)]
        + [pltpu.VMEM((SSM_CHUNK, D_SSM), F32)],
        input_output_aliases={12: 0},
        compiler_params=_params(("arbitrary",)),
    )(proj, dg_in, y_ssm, car_re, car_im, b_bd, c_bd, p_re, p_im, d_skip, jnp.asarray(perm, BF16),
      jnp.asarray(perm.T, BF16), dproj)
    return outs


def _bucket_table():
    i = np.arange(BLOCK)[:, None]
    j = np.arange(2 * BLOCK)[None, :]
    dist = BLOCK + i - j
    ok = (dist >= 0) & (dist < WINDOW)
    max_exact = N_BUCKETS // 2
    d = np.maximum(dist, 1).astype(np.float32)
    large = max_exact + (np.log(d / max_exact) / math.log(MAX_DISTANCE / max_exact)
                         * (N_BUCKETS - max_exact)).astype(np.int32)
    large = np.minimum(large, N_BUCKETS - 1)
    bucket = np.where(dist < max_exact, dist, large)
    return np.where(ok, bucket, -1).astype(np.int32)


def _bias_build_body(table_ref, bucket_ref, o_ref):
    h = pl.program_id(0)
    bucket = bucket_ref[...]
    acc = jnp.full(bucket.shape, NEG_INF, F32)
    for b in range(N_BUCKETS):
        acc = jnp.where(bucket == b, table_ref[b, h], acc)
    o_ref[...] = acc


def _bias_build(rel_bias_table, bucket):
    return pl.pallas_call(
        functools.partial(_bias_build_body), name="bias_build", grid=(N_Q_HEADS,),
        in_specs=[pl.BlockSpec(memory_space=pltpu.SMEM), _whole(bucket.shape)],
        out_specs=pl.BlockSpec((None, BLOCK, 2 * BLOCK), lambda h: (h, 0, 0)),
        out_shape=jax.ShapeDtypeStruct((N_Q_HEADS, BLOCK, 2 * BLOCK), F32),
        compiler_params=_params(("arbitrary",)),
    )(rel_bias_table, bucket)


def _bias_grad_body(g_ref, bucket_ref, o_ref):
    bucket = bucket_ref[...]
    lane = lax.broadcasted_iota(jnp.int32, (N_BUCKETS, 128), 1)

    def head(h, out):
        g = g_ref[h]
        rows = [jnp.sum(jnp.where(bucket == b, g, 0.0), axis=0, keepdims=True) for b in range(N_BUCKETS)]
        colsum = jnp.sum(jnp.concatenate(rows, axis=0), axis=1, keepdims=True)
        return jnp.where(lane == h, colsum, out)

    o_ref[...] = lax.fori_loop(0, N_Q_HEADS, head, jnp.zeros((N_BUCKETS, 128), F32))


def _bias_grad(g_bias, bucket):
    out = pl.pallas_call(
        functools.partial(_bias_grad_body), name="bias_grad", grid=(1,),
        in_specs=[_whole(g_bias.shape), _whole(bucket.shape)],
        out_specs=_whole((N_BUCKETS, 128)),
        out_shape=jax.ShapeDtypeStruct((N_BUCKETS, 128), F32),
        compiler_params=_params(("arbitrary",)),
    )(g_bias, bucket)
    return out[:, :N_Q_HEADS]


def _head_logits(qh, kk, bias_h, first_block):
    s = lax.dot_general(qh, kk, _DIMS["nt"], preferred_element_type=F32) * ATTN_SCALE + bias_h
    col = lax.broadcasted_iota(jnp.int32, s.shape, 1)
    return jnp.where(jnp.logical_and(first_block, col < BLOCK), NEG_INF, s)


def _attn_fwd_body(sink_ref, q_ref, kp_ref, kc_ref, vp_ref, vc_ref, bias_ref, o_ref, lse_ref):
    n = pl.program_id(0)
    outs, lses = [], []
    for kv in range(N_KV_HEADS):
        cs = slice(kv * HEAD_DIM, (kv + 1) * HEAD_DIM)
        kk = jnp.concatenate([kp_ref[:, cs], kc_ref[:, cs]], axis=0).astype(BF16)
        vv = jnp.concatenate([vp_ref[:, cs], vc_ref[:, cs]], axis=0).astype(BF16)
        for g in range(Q_PER_KV):
            h = kv * Q_PER_KV + g
            qh = q_ref[:, h * HEAD_DIM:(h + 1) * HEAD_DIM].astype(BF16)
            s = _head_logits(qh, kk, bias_ref[h], n == 0)
            sink = sink_ref[0, h]
            m = jnp.maximum(jnp.max(s, axis=1, keepdims=True), sink)
            p = jnp.exp(s - m)
            den = jnp.sum(p, axis=1, keepdims=True) + jnp.exp(sink - m)
            p = p / den
            outs.append(jnp.dot(p.astype(BF16), vv, preferred_element_type=F32))
            lses.append(m + jnp.log(den))
    o_ref[...] = jnp.concatenate(outs, axis=1).astype(o_ref.dtype)
    lse_ref[...] = jnp.concatenate(lses, axis=1)


def _attn_specs(nb):
    prev = lambda n: jnp.maximum(jnp.minimum(n, nb - 1) - 1, 0)
    cur = lambda n: jnp.minimum(n, nb - 1)
    return [pl.BlockSpec((BLOCK, D_ATTN), lambda n: (cur(n), C_Q // D_ATTN)),
            pl.BlockSpec((BLOCK, D_KV), lambda n: (prev(n), C_K // D_KV)),
            pl.BlockSpec((BLOCK, D_KV), lambda n: (cur(n), C_K // D_KV)),
            pl.BlockSpec((BLOCK, D_KV), lambda n: (prev(n), C_V // D_KV)),
            pl.BlockSpec((BLOCK, D_KV), lambda n: (cur(n), C_V // D_KV))]


def _attn_forward(proj, sinks, bias):
    t = proj.shape[0]
    nb = t // BLOCK
    return pl.pallas_call(
        functools.partial(_attn_fwd_body), name="attn_fwd", grid=(nb,),
        in_specs=[pl.BlockSpec(memory_space=pltpu.SMEM)] + _attn_specs(nb) + [_whole(bias.shape)],
        out_specs=[pl.BlockSpec((BLOCK, D_ATTN), lambda n: (n, 0)),
                   pl.BlockSpec((BLOCK, N_Q_HEADS), lambda n: (n, 0))],
        out_shape=[jax.ShapeDtypeStruct((t, D_ATTN), BF16), jax.ShapeDtypeStruct((t, N_Q_HEADS), F32)],
        compiler_params=_params(("parallel",)),
    )(sinks, proj, proj, proj, proj, proj, bias)


def _attn_bwd_body(sink_ref, q_ref, kp_ref, kc_ref, vp_ref, vc_ref, bias_ref, do_ref, o_ref, lse_ref, dproj_in,
                   dq_ref, dkv_ref, gbias_ref, gsink_ref, carry_ref, *, nb):
    del dproj_in
    n = pl.program_id(0)

    @pl.when(n == 0)
    def _():
        gbias_ref[...] = jnp.zeros_like(gbias_ref)
        gsink_ref[...] = jnp.zeros_like(gsink_ref)
        carry_ref[...] = jnp.zeros_like(carry_ref)

    @pl.when(n < nb)
    def _():
        lane = lax.broadcasted_iota(jnp.int32, (1, 128), 1)
        dqs, dks, dvs = [], [], []
        gsink = jnp.zeros((1, 128), F32)
        for kv in range(N_KV_HEADS):
            cs = slice(kv * HEAD_DIM, (kv + 1) * HEAD_DIM)
            kk = jnp.concatenate([kp_ref[:, cs], kc_ref[:, cs]], axis=0).astype(BF16)
            vv = jnp.concatenate([vp_ref[:, cs], vc_ref[:, cs]], axis=0).astype(BF16)
            dk = jnp.zeros((2 * BLOCK, HEAD_DIM), F32)
            dv = jnp.zeros((2 * BLOCK, HEAD_DIM), F32)
            for g in range(Q_PER_KV):
                h = kv * Q_PER_KV + g
                hs = slice(h * HEAD_DIM, (h + 1) * HEAD_DIM)
                qh = q_ref[:, hs].astype(BF16)
                s = _head_logits(qh, kk, bias_ref[h], n == 0)
                lse = lse_ref[:, h:h + 1]
                p = jnp.exp(s - lse)
                do = do_ref[:, hs].astype(F32)
                delta = jnp.sum(do * o_ref[:, hs].astype(F32), axis=1, keepdims=True)
                dob = do.astype(BF16)
                dp = lax.dot_general(dob, vv, _DIMS["nt"], preferred_element_type=F32)
                dl = p * (dp - delta)
                gbias_ref[h] += dl
                psink = jnp.exp(sink_ref[0, h] - lse)
                gsink = gsink + jnp.where(lane == h, -jnp.sum(psink * delta), 0.0)
                dlb = dl.astype(BF16)
                dqs.append(jnp.dot(dlb, kk, preferred_element_type=F32) * ATTN_SCALE)
                dk = dk + lax.dot_general(dlb, qh, _DIMS["tn"], preferred_element_type=F32) * ATTN_SCALE
                dv = dv + lax.dot_general(p.astype(BF16), dob, _DIMS["tn"], preferred_element_type=F32)
            dks.append(dk)
            dvs.append(dv)
        dq_ref[...] = jnp.concatenate(dqs, axis=1).astype(dq_ref.dtype)
        gsink_ref[...] += gsink
        dkv = jnp.concatenate(dks + dvs, axis=1)
        dkv_ref[...] = (carry_ref[...] + dkv[:BLOCK]).astype(dkv_ref.dtype)
        carry_ref[...] = dkv[BLOCK:]

    @pl.when(n == nb)
    def _():
        dkv_ref[...] = carry_ref[...].astype(dkv_ref.dtype)


def _attn_backward(proj, sinks, bias, d_attn, attn, lse, dproj):
    t = proj.shape[0]
    nb = t // BLOCK
    cur = lambda n: jnp.minimum(n, nb - 1)
    return pl.pallas_call(
        functools.partial(_attn_bwd_body, nb=nb), name="attn_bwd", grid=(nb + 1,),
        in_specs=[pl.BlockSpec(memory_space=pltpu.SMEM)] + _attn_specs(nb) + [
            _whole(bias.shape),
            pl.BlockSpec((BLOCK, D_ATTN), lambda n: (cur(n), 0)),
            pl.BlockSpec((BLOCK, D_ATTN), lambda n: (cur(n), 0)),
            pl.BlockSpec((BLOCK, N_Q_HEADS), lambda n: (cur(n), 0)),
            pl.BlockSpec(memory_space=pl.ANY)],
        out_specs=[pl.BlockSpec((BLOCK, D_ATTN), lambda n: (cur(n), C_Q // D_ATTN)),
                   pl.BlockSpec((BLOCK, 2 * D_KV), lambda n: (jnp.maximum(n - 1, 0), 0)),
                   _whole(bias.shape), _whole((1, 128))],
        out_shape=[jax.ShapeDtypeStruct(dproj.shape, dproj.dtype), jax.ShapeDtypeStruct((t, 2 * D_KV), dproj.dtype),
                   jax.ShapeDtypeStruct(bias.shape, F32), jax.ShapeDtypeStruct((1, 128), F32)],
        scratch_shapes=[pltpu.VMEM((BLOCK, 2 * D_KV), F32)],
        input_output_aliases={10: 0},
        compiler_params=_params(("arbitrary",)),
    )(sinks, proj, proj, proj, proj, proj, bias, d_attn, attn, lse, dproj)


ROWS = 256


def _rowwise(body, name, t, ins, outs, aliases=None):
    rows = min(ROWS, t)

    def col_spec(w, c0):
        if c0 % w == 0:
            return pl.BlockSpec((rows, w), lambda i: (i, c0 // w))
        return pl.BlockSpec((pl.Element(rows), pl.Element(w)), lambda i: (i * rows, c0))

    in_specs, args = [], []
    for a, w, c0 in ins:
        args.append(a)
        if w is None:
            in_specs.append(pl.BlockSpec(memory_space=pl.ANY) if c0 == "any" else _whole(a.shape))
        else:
            in_specs.append(col_spec(w, c0))
    out_specs, out_shape = [], []
    for shape, dtype, w, c0 in outs:
        out_shape.append(jax.ShapeDtypeStruct(shape, dtype))
        out_specs.append(_whole(shape) if w is None else col_spec(w, c0))
    accum = any(o[2] is None for o in outs)
    return pl.pallas_call(
        body, name=name, grid=(t // rows,), in_specs=in_specs, out_specs=out_specs, out_shape=out_shape,
        input_output_aliases=aliases or {},
        compiler_params=_params(("arbitrary",) if accum else ("parallel",)),
    )(*args)


def _f32(ref, *idx):
    return (ref[idx] if idx else ref[...]).astype(F32)


def _ssm_gate_fwd_body(glu_ref, z_ref, h_ref):
    a, b = _f32(glu_ref, slice(None), slice(0, D_SSM)), _f32(glu_ref, slice(None), slice(D_SSM, None))
    h_ref[...] = ((a * _sigmoid(b)) * _silu_and_grad(_f32(z_ref))[0]).astype(h_ref.dtype)


def _ssm_gate_bwd_body(dh_ref, glu_ref, z_ref, dproj_in, dglu_ref, dz_ref):
    del dproj_in
    a, b = _f32(glu_ref, slice(None), slice(0, D_SSM)), _f32(glu_ref, slice(None), slice(D_SSM, None))
    sb = _sigmoid(b)
    silu, dsilu = _silu_and_grad(_f32(z_ref))
    dh = _f32(dh_ref)
    dg = dh * silu
    dz_ref[...] = (dh * (a * sb) * dsilu).astype(dz_ref.dtype)
    dglu_ref[:, :D_SSM] = (dg * sb).astype(dglu_ref.dtype)
    dglu_ref[:, D_SSM:] = (dg * a * sb * (1.0 - sb)).astype(dglu_ref.dtype)


def _attn_gate_fwd_body(attn_ref, z_ref, h_ref):
    h_ref[...] = (_f32(attn_ref) * _silu_and_grad(_f32(z_ref))[0]).astype(h_ref.dtype)


def _attn_gate_bwd_body(dh_ref, attn_ref, z_ref, dproj_in, dattn_ref, dz_ref):
    del dproj_in
    silu, dsilu = _silu_and_grad(_f32(z_ref))
    dh = _f32(dh_ref)
    dattn_ref[...] = (dh * silu).astype(dattn_ref.dtype)
    dz_ref[...] = (dh * _f32(attn_ref) * dsilu).astype(dz_ref.dtype)


def _merge_fwd_body(bs_ref, ba_ref, gl_ref, m_ref):
    gs = _sigmoid(_f32(gl_ref, slice(None), slice(0, D_MODEL)))
    ga = _sigmoid(_f32(gl_ref, slice(None), slice(D_MODEL, None)))
    m_ref[...] = (gs * _f32(bs_ref) + ga * _f32(ba_ref)).astype(m_ref.dtype)


def _merge_bwd_body(dm_ref, bs_ref, ba_ref, gl_ref, dbs_ref, dba_ref, dgl_ref):
    gs = _sigmoid(_f32(gl_ref, slice(None), slice(0, D_MODEL)))
    ga = _sigmoid(_f32(gl_ref, slice(None), slice(D_MODEL, None)))
    dm = _f32(dm_ref)
    dbs_ref[...] = (dm * gs).astype(dbs_ref.dtype)
    dba_ref[...] = (dm * ga).astype(dba_ref.dtype)
    dgl_ref[:, :D_MODEL] = (dm * _f32(bs_ref) * gs * (1.0 - gs)).astype(dgl_ref.dtype)
    dgl_ref[:, D_MODEL:] = (dm * _f32(ba_ref) * ga * (1.0 - ga)).astype(dgl_ref.dtype)


def _ln_loss_body(x_ref, o_ref, tgt_ref, gain_ref, bias_ref, dr_ref, drb_ref, loss_ref, dgain_ref, dbias_ref):
    @pl.when(pl.program_id(0) == 0)
    def _():
        loss_ref[...] = jnp.zeros_like(loss_ref)
        dgain_ref[...] = jnp.zeros_like(dgain_ref)
        dbias_ref[...] = jnp.zeros_like(dbias_ref)

    r = ALPHA * x_ref[...] + o_ref[...]
    mu = jnp.mean(r, axis=1, keepdims=True)
    rc = r - mu
    var = jnp.mean(rc * rc, axis=1, keepdims=True)
    rstd = lax.rsqrt(var + LN_EPS)
    xhat = rc * rstd
    gain = gain_ref[...]
    err = xhat * gain + bias_ref[...] - tgt_ref[...]
    loss_ref[...] += 0.5 * jnp.sum(jnp.mean(err * err, axis=1, keepdims=True), axis=0, keepdims=True)
    dy = err * (1.0 / D_MODEL)
    dgain_ref[...] += jnp.sum(dy * xhat, axis=0, keepdims=True)
    dbias_ref[...] += jnp.sum(dy, axis=0, keepdims=True)
    dxhat = dy * gain
    m1 = jnp.mean(dxhat, axis=1, keepdims=True)
    m2 = jnp.mean(dxhat * xhat, axis=1, keepdims=True)
    dr = rstd * (dxhat - m1 - xhat * m2)
    dr_ref[...] = dr
    drb_ref[...] = dr.astype(drb_ref.dtype)


def _place_body(piece_ref, dproj_in, o_ref):
    del dproj_in
    o_ref[...] = piece_ref[...]


def _adamw_update(w_ref, m_ref, v_ref, g, g_ref, d_ref, nm_ref, nv_ref):
    m = ADAM_B1 * m_ref[...] + (1.0 - ADAM_B1) * g
    v = ADAM_B2 * v_ref[...] + (1.0 - ADAM_B2) * (g * g)
    m_hat = m / (1.0 - ADAM_B1 ** ADAM_STEP)
    v_hat = v / (1.0 - ADAM_B2 ** ADAM_STEP)
    g_ref[...] = g
    d_ref[...] = -ADAM_LR * (m_hat / (jnp.sqrt(v_hat) + ADAM_EPS) + ADAM_WD * w_ref[...])
    nm_ref[...] = m
    nv_ref[...] = v


def _adamw_body(*refs, n_parts):
    w_ref, m_ref, v_ref = refs[:3]
    parts = refs[3:3 + n_parts]
    g = parts[0][...].astype(F32)
    for p in parts[1:]:
        g = g + p[...].astype(F32)
    _adamw_update(w_ref, m_ref, v_ref, g, *refs[3 + n_parts:])


def _adamw_shard_body(c_ref, w_ref, m_ref, v_ref, mine_ref, other_ref, g_ref, d_ref, nm_ref, nv_ref, *, nth):
    in_mine = pl.program_id(0) // nth == c_ref[0]
    g = jnp.where(in_mine, mine_ref[...], other_ref[...])
    _adamw_update(w_ref, m_ref, v_ref, g, g_ref, d_ref, nm_ref, nv_ref)


def _adamw_shard(w, m, v, halves, core, *, name, rows):
    shape = w.shape
    w2, m2, v2 = (a.reshape(-1, shape[-1]) for a in (w, m, v))
    r, c = w2.shape
    nth = r // 2 // rows
    assert 2 * nth * rows == r and halves[0].shape == (r // 2, c)
    spec = pl.BlockSpec((rows, c), lambda i, cr: (i, 0))
    half_spec = pl.BlockSpec((rows, c), lambda i, cr: (i % nth, 0))
    outs = pl.pallas_call(
        functools.partial(_adamw_shard_body, nth=nth), name=name,
        grid_spec=pltpu.PrefetchScalarGridSpec(
            num_scalar_prefetch=1, grid=(r // rows,),
            in_specs=[spec] * 3 + [half_spec] * 2, out_specs=[spec] * 4),
        out_shape=[jax.ShapeDtypeStruct((r, c), F32)] * 4,
        compiler_params=_params(("parallel",)),
    )(core, w2, m2, v2, *halves)
    return tuple(o.reshape(shape) for o in outs)


def _adamw(w, m, v, parts, *, name, rows=256):
    shape = w.shape
    w2, m2, v2 = (a.reshape(-1, shape[-1]) for a in (w, m, v))
    parts = [p.reshape(w2.shape) for p in parts]
    r, c = w2.shape
    rows = rows if r % rows == 0 else r
    spec = pl.BlockSpec((rows, c), lambda i: (i, 0))
    outs = pl.pallas_call(
        functools.partial(_adamw_body, n_parts=len(parts)), name=name, grid=(r // rows,),
        in_specs=[spec] * (3 + len(parts)), out_specs=[spec] * 4,
        out_shape=[jax.ShapeDtypeStruct((r, c), F32)] * 4,
        compiler_params=_params(("parallel",)),
    )(w2, m2, v2, *parts)
    return tuple(o.reshape(shape) for o in outs)


BIG = ("w_in", "w_glu", "w_branch_ssm", "w_branch_attn", "w_out")
SHARD_AXIS = dict(w_in=1, w_glu=1, w_branch_ssm=1, w_branch_attn=1, w_out=0)
HBM = pl.BlockSpec(memory_space=pl.ANY)


def _position():
    x, y, c = lax.axis_index("x"), lax.axis_index("y"), lax.axis_index("c")
    other_chips = [(1 - x, y), (x, 1 - y), (1 - x, 1 - y)]
    return x, y, c, other_chips


def _window(ref, axis, shard, n_shards, half=None):
    rows, cols = ref.shape[-2:]
    sel = [slice(None), slice(None)]
    size = ref.shape[-2 + axis] // n_shards
    sel[axis] = pl.ds(pl.multiple_of(shard * size, 128), size)
    if half is not None:
        hsize = ref.shape[-1 - axis] // 2
        sel[1 - axis] = pl.ds(pl.multiple_of(half * hsize, 128), hsize)
    return ref.at[tuple(sel)]


def _half(ref, axis, half):
    hsize = ref.shape[-1 - axis] // 2
    sel = [slice(None), slice(None)]
    sel[1 - axis] = pl.ds(pl.multiple_of(half * hsize, 128), hsize)
    return ref.at[tuple(sel)]


def _remote(src, dst, send_sem, recv_sem, device):
    return pltpu.make_async_remote_copy(src_ref=src, dst_ref=dst, send_sem=send_sem, recv_sem=recv_sem,
                                        device_id=device, device_id_type=MESH)


def _gather_body(*refs, axes):
    n = len(axes)
    shards, outs = refs[:n], refs[n:2 * n]
    send_sems, recv_sems, fsend_sems, frecv_sems, local_sems = refs[2 * n:]
    x, y, c, chips = _position()
    me = 2 * x + y
    local, sends, passes = [], [], []
    for w, ax in enumerate(axes):
        cp = pltpu.make_async_copy(shards[w], _window(outs[w], ax, me, N_CHIPS), local_sems.at[w])
        cp.start()
        local.append(cp)
    for w, ax in enumerate(axes):
        for r, (px, py) in enumerate(chips):
            cp = _remote(_half(shards[w], ax, c), _window(outs[w], ax, me, N_CHIPS, c),
                         send_sems.at[3 * w + r], recv_sems.at[3 * w + r], (px, py, c))
            cp.start()
            sends.append(cp)
    for w, ax in enumerate(axes):
        for r, (px, py) in enumerate(chips):
            landed = _window(outs[w], ax, 2 * px + py, N_CHIPS, c)
            _remote(landed, landed, send_sems.at[3 * w + r], recv_sems.at[3 * w + r], (px, py, c)).wait_recv()
            cp = _remote(landed, landed, fsend_sems.at[3 * w + r], frecv_sems.at[3 * w + r], (x, y, 1 - c))
            cp.start()
            passes.append(cp)
    for w, ax in enumerate(axes):
        for r, (px, py) in enumerate(chips):
            other = _window(outs[w], ax, 2 * px + py, N_CHIPS, 1 - c)
            _remote(other, other, fsend_sems.at[3 * w + r], frecv_sems.at[3 * w + r], (x, y, 1 - c)).wait_recv()
    for cp in sends + passes:
        cp.wait_send()
    for cp in local:
        cp.wait()


def _full_shapes(shards, names):
    out = []
    for k in names:
        s = list(shards[k].shape)
        s[SHARD_AXIS[k]] *= N_CHIPS
        out.append(jax.ShapeDtypeStruct(tuple(s), shards[k].dtype))
    return out


def _all_gather(shards, names):
    n = len(names)
    return pl.pallas_call(
        functools.partial(_gather_body, axes=tuple(SHARD_AXIS[k] for k in names)), name="gather_weights",
        in_specs=[HBM] * n, out_specs=[HBM] * n, out_shape=_full_shapes(shards, names),
        scratch_shapes=[pltpu.SemaphoreType.DMA((3 * n,))] * 4 + [pltpu.SemaphoreType.DMA((n,))],
    )(*[shards[k] for k in names])


def _gather_exchange(shards, names):
    axes = tuple(SHARD_AXIS[k] for k in names)
    n = len(names)

    def copies(ins, outs, sems):
        send_sems, recv_sems, local_sems = sems
        x, y, c, chips = _position()
        me = 2 * x + y
        remote, local = [], []
        for w, ax in enumerate(axes):
            for r, (px, py) in enumerate(chips):
                remote.append(_remote(_half(ins[w], ax, c), _window(outs[w], ax, me, N_CHIPS, c),
                                      send_sems.at[3 * w + r], recv_sems.at[3 * w + r], (px, py, c)))
            local.append(pltpu.make_async_copy(ins[w], _window(outs[w], ax, me, N_CHIPS), local_sems.at[w]))
        return remote, local

    return _Exchange([shards[k] for k in names], _full_shapes(shards, names), (3 * n, 3 * n, n), copies)


def _pass_on_body(*refs, axes):
    n = len(axes)
    ins, outs = refs[:n], refs[n:2 * n]
    sbuf, rbuf = refs[2 * n:5 * n], refs[5 * n:8 * n]
    send_sems, recv_sems, load_sems, store_sems = refs[8 * n:]
    x, y, c, chips = _position()
    region = lambda ref, w, r, half: _window(ref, axes[w], 2 * chips[r][0] + chips[r][1], N_CHIPS, half)
    pairs = [(w, r) for w in range(n) for r in range(3)]
    loads = [pltpu.make_async_copy(region(ins[w], w, r, c), sbuf[3 * w + r], load_sems.at[3 * w + r]) for w, r in pairs]
    for cp in loads:
        cp.start()
    sends = []
    for i, cp in enumerate(loads):
        cp.wait()
        sends.append(_remote(sbuf[i], rbuf[i], send_sems.at[i], recv_sems.at[i], (x, y, 1 - c)))
        sends[-1].start()
    stores = []
    for i, (w, r) in enumerate(pairs):
        sends[i].wait_recv()
        stores.append(pltpu.make_async_copy(rbuf[i], region(outs[w], w, r, 1 - c), store_sems.at[i]))
        stores[-1].start()
    for cp in sends:
        cp.wait_send()
    for cp in stores:
        cp.wait()


def _pass_on(fulls, names):
    axes = tuple(SHARD_AXIS[k] for k in names)
    n = len(names)
    bufs = []
    for a, ax in zip(fulls, axes):
        s = list(a.shape)
        s[ax] //= N_CHIPS
        s[1 - ax] //= 2
        bufs += [pltpu.VMEM(tuple(s), a.dtype)] * 3
    return pl.pallas_call(
        functools.partial(_pass_on_body, axes=axes), name="pass_on_weights",
        in_specs=[HBM] * n, out_specs=[HBM] * n,
        out_shape=[jax.ShapeDtypeStruct(a.shape, a.dtype) for a in fulls],
        scratch_shapes=bufs + bufs + [pltpu.SemaphoreType.DMA((3 * n,))] * 4,
        input_output_aliases={i: i for i in range(n)},
        compiler_params=pltpu.CompilerParams(vmem_limit_bytes=VMEM_LIMIT),
    )(*fulls)


def _chip_exchange(pairs, names):
    axes = tuple(SHARD_AXIS[k] for k in names)
    n = len(names)
    out_shape = []
    for p, ax in zip(pairs, axes):
        s = list(p.shape)
        s[ax] //= N_CHIPS
        out_shape.append(jax.ShapeDtypeStruct((4, *s), p.dtype))

    def copies(ins, outs, sems):
        send_sems, recv_sems, local_sems = sems
        x, y, c, chips = _position()
        me = 2 * x + y
        remote, local = [], []
        for w, ax in enumerate(axes):
            for r, (px, py) in enumerate(chips):
                remote.append(_remote(_window(ins[w], ax, 2 * px + py, N_CHIPS), outs[w].at[r],
                                      send_sems.at[3 * w + r], recv_sems.at[3 * w + r], (px, py, c)))
            local.append(pltpu.make_async_copy(_window(ins[w], ax, me, N_CHIPS), outs[w].at[3], local_sems.at[w]))
        return remote, local

    return _Exchange(pairs, out_shape, (3 * n, 3 * n, n), copies)


def _half_tile(n, h, nt, axis):
    return h * nt + n if axis == 1 else 2 * n + h


def _sibling_stream(n, nt, stage, recv, send_sems, recv_sems, credit, produce, consume):
    x, y, c, _ = _position()
    sibling = (x, y, 1 - c)

    def copy(slot):
        return _remote(stage.at[slot], recv.at[slot], send_sems.at[slot], recv_sems.at[slot], sibling)

    @pl.when(n < nt)
    def _():
        slot = n % 2

        @pl.when(n >= 2)
        def _():
            copy(slot).wait_send()
            pl.semaphore_wait(credit, 1)

        stage[slot] = produce().astype(stage.dtype)
        copy(slot).start()

    @pl.when(n >= 1)
    def _():
        slot = (n - 1) % 2
        copy(slot).wait_recv()
        consume(recv[slot])

        @pl.when(n - 1 < nt - 2)
        def _():
            pl.semaphore_signal(credit, 1, device_id=sibling, device_id_type=MESH)

    @pl.when(n == nt)
    def _():
        for slot in range(min(2, nt)):
            copy(slot).wait_send()


def _pair_reduce_body(c_ref, mine_ref, theirs_ref, out_ref, stage, recv, send_sems, recv_sems, credit, *, nt):
    del c_ref

    def consume(got):
        out_ref[...] = (mine_ref[...] + got.astype(F32)).astype(out_ref.dtype)

    _sibling_stream(pl.program_id(0), nt, stage, recv, send_sems, recv_sems, credit,
                    lambda: theirs_ref[...], consume)


def _pair_reduce(grad, core, axis, *, name, rows):
    r, c = grad.shape
    nt = r // 2 // rows
    assert nt * rows * 2 == r and (axis == 1 or rows == r // (2 * N_CHIPS))
    tile = lambda n, h: _half_tile(n, h, nt, axis)
    return pl.pallas_call(
        functools.partial(_pair_reduce_body, nt=nt), name=name,
        grid_spec=pltpu.PrefetchScalarGridSpec(
            num_scalar_prefetch=1, grid=(nt + 1,),
            in_specs=[pl.BlockSpec((rows, c), lambda n, cr: (tile(jnp.maximum(n - 1, 0), cr[0]), 0)),
                      pl.BlockSpec((rows, c), lambda n, cr: (tile(jnp.minimum(n, nt - 1), 1 - cr[0]), 0))],
            out_specs=pl.BlockSpec((rows, c), lambda n, cr: (jnp.maximum(n - 1, 0), 0)),
            scratch_shapes=[pltpu.VMEM((2, rows, c), BF16), pltpu.VMEM((2, rows, c), BF16),
                            pltpu.SemaphoreType.DMA((2,)), pltpu.SemaphoreType.DMA((2,)),
                            pltpu.SemaphoreType.REGULAR]),
        out_shape=jax.ShapeDtypeStruct((r // 2, c), BF16),
        compiler_params=_params(("arbitrary",)),
    )(core, grad, grad)


def _chip_add_share_body(c_ref, s0, s1, s2, s3, mine_ref, other_ref, stage, recv, send_sems, recv_sems, credit, *, nt):
    del c_ref

    def produce():
        total = s3[...].astype(F32) + s0[...].astype(F32) + s1[...].astype(F32) + s2[...].astype(F32)
        mine_ref[...] = total
        return total

    def consume(got):
        other_ref[...] = got

    _sibling_stream(pl.program_id(0), nt, stage, recv, send_sems, recv_sems, credit, produce, consume)


def _chip_add_share(slots, core, *, name, rows):
    _, r, c = slots.shape
    nt = r // rows
    assert nt * rows == r
    send = lambda j: pl.BlockSpec((None, rows, c), lambda n, cr: (j, jnp.minimum(n, nt - 1), 0))
    return pl.pallas_call(
        functools.partial(_chip_add_share_body, nt=nt), name=name,
        grid_spec=pltpu.PrefetchScalarGridSpec(
            num_scalar_prefetch=1, grid=(nt + 1,),
            in_specs=[send(j) for j in range(4)],
            out_specs=[pl.BlockSpec((rows, c), lambda n, cr: (jnp.minimum(n, nt - 1), 0)),
                       pl.BlockSpec((rows, c), lambda n, cr: (jnp.maximum(n - 1, 0), 0))],
            scratch_shapes=[pltpu.VMEM((2, rows, c), F32), pltpu.VMEM((2, rows, c), F32),
                            pltpu.SemaphoreType.DMA((2,)), pltpu.SemaphoreType.DMA((2,)),
                            pltpu.SemaphoreType.REGULAR]),
        out_shape=[jax.ShapeDtypeStruct((r, c), F32)] * 2,
        compiler_params=_params(("arbitrary",)),
    )(core, slots, slots, slots, slots)


PAIR_ROWS = dict(w_in=64, w_glu=128, w_branch_ssm=128, w_branch_attn=128, w_out=256)
SHARE_ROWS = dict(w_in=128, w_glu=128, w_branch_ssm=128, w_branch_attn=128, w_out=64)


def _pair_sums(grads, names, core):
    return [_pair_reduce(grads[k], core, SHARD_AXIS[k], name="pair_reduce_" + k, rows=PAIR_ROWS[k]) for k in names]


def _shard_halves(slots, names, core):
    return {k: _chip_add_share(s, core, name="chip_add_share_" + k, rows=SHARE_ROWS[k]) for k, s in zip(names, slots)}


SMALL = (("ssm_lambda_re", (1, 64, 64)), ("ssm_lambda_im", (1, 64, 64)), ("ssm_b_re", (1, 64, 64, 16)),
         ("ssm_b_im", (1, 64, 64, 16)), ("ssm_c_re", (1, 64, 16, 64)), ("ssm_c_im", (1, 64, 16, 64)),
         ("ssm_d", (1, 1024)), ("ssm_log_step", (1, 64)), ("attn_sinks", (1, 16)), ("rel_bias_table", (32, 16)),
         ("ln_gain", (1, 2048)), ("ln_bias", (1, 2048)))
SMALL_SIZE = sum(int(np.prod(s)) for _, s in SMALL)
PACK_ROWS = -(-(SMALL_SIZE + 1) // (8 * 128)) * 8


def _pack(values, extra=None):
    flat = [values[k].reshape(-1).astype(F32) for k, _ in SMALL]
    flat.append(jnp.zeros((1,), F32) if extra is None else extra.reshape(1))
    flat.append(jnp.zeros((PACK_ROWS * 128 - SMALL_SIZE - 1,), F32))
    return jnp.concatenate(flat).reshape(PACK_ROWS, 128)


def _unpack(packed):
    flat = packed.reshape(-1)
    out, off = {}, 0
    for k, s in SMALL:
        size = int(np.prod(s))
        out[k] = flat[off:off + size].reshape(s)
        off += size
    return out, flat[off]


def _small_exchange(packed):
    def copies(ins, outs, sems):
        send_sems, recv_sems, local_sems = sems
        x, y, c, _ = _position()
        remote = []
        for r in range(1, 8):
            peer = tuple(1 - v if (r >> s) & 1 else v for v, s in ((x, 2), (y, 1), (c, 0)))
            remote.append(_remote(ins[0], outs[0].at[r], send_sems.at[r - 1], recv_sems.at[r - 1], peer))
        return remote, [pltpu.make_async_copy(ins[0], outs[0].at[0], local_sems.at[0])]

    return _Exchange([packed], [jax.ShapeDtypeStruct((8, *packed.shape), F32)], (7, 7, 1), copies)


def _merge_exchanges(a, b):
    na, ma, sa = len(a.ins), len(a.out_shape), len(a.n_sems)

    def copies(ins, outs, sems):
        ra, la = a.copies(ins[:na], outs[:ma], sems[:sa])
        rb, lb = b.copies(ins[na:], outs[ma:], sems[sa:])
        return ra + rb, la + lb

    return _Exchange(a.ins + b.ins, a.out_shape + b.out_shape, a.n_sems + b.n_sems, copies)


def _small_sum_body(slots_ref, o_ref):
    x, y, c, _ = _position()
    me = 4 * x + 2 * y + c
    acc = slots_ref[me]
    for s in range(1, 8):
        acc = acc + slots_ref[jnp.bitwise_xor(me, s)]
    o_ref[...] = acc


def _small_sum(slots):
    vm = pl.BlockSpec(memory_space=pltpu.VMEM)
    return pl.pallas_call(
        functools.partial(_small_sum_body), name="small_sum",
        in_specs=[vm], out_specs=vm, out_shape=jax.ShapeDtypeStruct(slots.shape[1:], F32),
        compiler_params=pltpu.CompilerParams(vmem_limit_bytes=VMEM_LIMIT),
    )(slots)


LATE = BIG[1:]


def _local_step(x, target, w_in, late_shards, core, lam_re, lam_im, b_re, b_im, c_re, c_im, d_skip,
                log_step, sinks, rel_bias_table, ln_gain, ln_bias):
    t = x.shape[0]
    bucket = jnp.asarray(_bucket_table())
    p_re, p_im, b_bd, c_bd = _ssm_prepare(lam_re, lam_im, b_re, b_im, c_re, c_im, log_step)
    bias = _bias_build(rel_bias_table, bucket)

    xb, xbt = _cast_and_transpose(x, name="cast_x")
    act = dict(out_dtype=BF16)
    proj, landed = _matmul(xb, w_in, "nn", name="proj", tm=2048, tn=512,
                           exchange=_gather_exchange(late_shards, LATE), **act)
    w_glu, w_bs, w_ba, w_out = _pass_on(list(landed), LATE)
    y_ssm, g_in, car_re, car_im = _ssm_forward(proj, p_re, p_im, b_bd, c_bd, d_skip)
    glu = _matmul(g_in, w_glu, "nn", name="glu", tm=1024, tn=2048, **act)
    (h_ssm,) = _rowwise(functools.partial(_ssm_gate_fwd_body), "ssm_gate_fwd", t,
                        [(glu, 2 * D_SSM, 0), (proj, D_SSM, C_ZS)], [((t, D_SSM), BF16, D_SSM, 0)])
    attn, lse = _attn_forward(proj, sinks, bias)
    (h_attn,) = _rowwise(functools.partial(_attn_gate_fwd_body), "attn_gate_fwd", t,
                         [(attn, D_ATTN, 0), (proj, D_ATTN, C_ZA)], [((t, D_ATTN), BF16, D_ATTN, 0)])
    bs = _matmul(h_ssm, w_bs, "nn", name="branch_ssm", tm=1024, tn=2048, **act)
    ba = _matmul(h_attn, w_ba, "nn", name="branch_attn", tm=1024, tn=2048, **act)
    gl_in = (proj, 2 * D_MODEL, C_GL)
    (merged,) = _rowwise(functools.partial(_merge_fwd_body), "merge_fwd", t,
                         [(bs, D_MODEL, 0), (ba, D_MODEL, 0), gl_in], [((t, D_MODEL), BF16, D_MODEL, 0)])
    out = _matmul(merged, w_out, "nn", name="out_proj", tm=1024, tn=1024)
    dr, drb, loss, g_gain, g_bias = _rowwise(
        functools.partial(_ln_loss_body), "ln_loss", t,
        [(x, D_MODEL, 0), (out, D_MODEL, 0), (target, D_MODEL, 0), (ln_gain, None, 0), (ln_bias, None, 0)],
        [((t, D_MODEL), F32, D_MODEL, 0), ((t, D_MODEL), BF16, D_MODEL, 0), ((1, 1), F32, None, 0),
         ((1, D_MODEL), F32, None, 0), ((1, D_MODEL), F32, None, 0)])

    g_w_out = _matmul(merged, drb, "tn", name="g_w_out", tm=1024, tn=512, tk=1024)
    d_merged = _matmul(drb, w_out, "nt", name="d_merged", tm=1024, tn=1024, **act)
    d_bs, d_ba, dproj = _rowwise(
        functools.partial(_merge_bwd_body), "merge_bwd", t,
        [(d_merged, D_MODEL, 0), (bs, D_MODEL, 0), (ba, D_MODEL, 0), gl_in],
        [((t, D_MODEL), BF16, D_MODEL, 0), ((t, D_MODEL), BF16, D_MODEL, 0),
         ((t, D_IN), BF16, 2 * D_MODEL, C_GL)])
    g_w_bs = _matmul(h_ssm, d_bs, "tn", name="g_w_branch_ssm", tm=1024, tn=512, tk=1024)
    d_h_ssm = _matmul(d_bs, w_bs, "nt", name="d_h_ssm", tm=1024, tn=1024, **act)
    g_w_ba = _matmul(h_attn, d_ba, "tn", name="g_w_branch_attn", tm=1024, tn=512, tk=1024)
    d_h_attn = _matmul(d_ba, w_ba, "nt", name="d_h_attn", tm=1024, tn=1024, **act)

    d_attn, dproj = _rowwise(
        functools.partial(_attn_gate_bwd_body), "attn_gate_bwd", t,
        [(d_h_attn, D_ATTN, 0), (attn, D_ATTN, 0), (proj, D_ATTN, C_ZA), (dproj, None, "any")],
        [((t, D_ATTN), BF16, D_ATTN, 0), ((t, D_IN), BF16, D_ATTN, C_ZA)], aliases={3: 1})
    dproj, d_kv, g_bias_full, g_sinks = _attn_backward(proj, sinks, bias, d_attn, attn, lse, dproj)
    (dproj,) = _rowwise(functools.partial(_place_body), "place_dkv", t,
                        [(d_kv, 2 * D_KV, 0), (dproj, None, "any")],
                        [((t, D_IN), BF16, 2 * D_KV, C_K)], aliases={1: 0})
    g_table = _bias_grad(g_bias_full, bucket)

    d_glu, dproj = _rowwise(
        functools.partial(_ssm_gate_bwd_body), "ssm_gate_bwd", t,
        [(d_h_ssm, D_SSM, 0), (glu, 2 * D_SSM, 0), (proj, D_SSM, C_ZS), (dproj, None, "any")],
        [((t, 2 * D_SSM), BF16, 2 * D_SSM, 0), ((t, D_IN), BF16, D_SSM, C_ZS)], aliases={3: 1})
    g_w_glu = _matmul(g_in, d_glu, "tn", name="g_w_glu", tm=1024, tn=512, tk=1024)
    d_g_in = _matmul(d_glu, w_glu, "nt", name="d_g_in", tm=1024, tn=1024, **act)
    dproj, g_bbd, g_cbd, g_lam_re, g_lam_im, g_d = _ssm_backward(
        proj, d_g_in, y_ssm, car_re, car_im, p_re, p_im, b_bd, c_bd, d_skip, dproj)
    g_lr, g_li, g_br, g_bi, g_cr, g_ci, g_ls = _ssm_param_grads(
        lam_re, lam_im, b_re, b_im, log_step, g_lam_re, g_lam_im, g_bbd, g_cbd)

    late = dict(w_glu=g_w_glu, w_branch_ssm=g_w_bs, w_branch_attn=g_w_ba, w_out=g_w_out)
    g_w_in, late_slots = _matmul(xbt, dproj, "nn", name="g_w_in", tm=512, tn=512,
                                 exchange=_chip_exchange(_pair_sums(late, LATE, core), LATE))
    small = dict(ssm_lambda_re=g_lr, ssm_lambda_im=g_li, ssm_b_re=g_br, ssm_b_im=g_bi, ssm_c_re=g_cr,
                 ssm_c_im=g_ci, ssm_d=g_d, ssm_log_step=g_ls, attn_sinks=g_sinks[:, :N_Q_HEADS],
                 rel_bias_table=g_table, ln_gain=g_gain, ln_bias=g_bias)
    last = _merge_exchanges(_chip_exchange(_pair_sums(dict(w_in=g_w_in), BIG[:1], core), BIG[:1]),
                            _small_exchange(_pack(small, loss)))
    grad_x, (in_slots, small_slots) = _matmul(dproj, w_in, "nt", name="grad_x", tm=512, tn=512, res=dr,
                                              res_scale=ALPHA, exchange=last)
    big = {**_shard_halves([in_slots], BIG[:1], core), **_shard_halves(late_slots, LATE, core)}
    return grad_x, big, _small_sum(small_slots)


WEIGHTS = ("w_in", "ssm_lambda_re", "ssm_lambda_im", "ssm_b_re", "ssm_b_im", "ssm_c_re", "ssm_c_im", "ssm_d",
           "ssm_log_step", "w_glu", "attn_sinks", "rel_bias_table", "w_branch_ssm", "w_branch_attn", "w_out",
           "ln_gain", "ln_bias")


def kernel(x, w_in, ssm_lambda_re, ssm_lambda_im, ssm_b_re, ssm_b_im, ssm_c_re, ssm_c_im, ssm_d, ssm_log_step, w_glu, attn_sinks, rel_bias_table, w_branch_ssm, w_branch_attn, w_out, ln_gain, ln_bias, loss_target, m_w_in, m_ssm_lambda_re, m_ssm_lambda_im, m_ssm_b_re, m_ssm_b_im, m_ssm_c_re, m_ssm_c_im, m_ssm_d, m_ssm_log_step, m_w_glu, m_attn_sinks, m_rel_bias_table, m_w_branch_ssm, m_w_branch_attn, m_w_out, m_ln_gain, m_ln_bias, v_w_in, v_ssm_lambda_re, v_ssm_lambda_im, v_ssm_b_re, v_ssm_b_im, v_ssm_c_re, v_ssm_c_im, v_ssm_d, v_ssm_log_step, v_w_glu, v_attn_sinks, v_rel_bias_table, v_w_branch_ssm, v_w_branch_attn, v_w_out, v_ln_gain, v_ln_bias):
    given = dict(locals())
    w = {k: given[k] for k in WEIGHTS}
    m = {k: given["m_" + k] for k in WEIGHTS}
    v = {k: given["v_" + k] for k in WEIGHTS}

    core = lax.axis_index("c").astype(jnp.int32).reshape(1)
    shards = {k: _cast(w[k][0], BF16, name="cast_" + k) for k in BIG}
    (w_in_full,) = _all_gather(shards, BIG[:1])
    grad_x, g_shard, g_packed = _local_step(
        x[0], loss_target[0], w_in_full, shards, core, ssm_lambda_re[0], ssm_lambda_im[0], ssm_b_re[0],
        ssm_b_im[0], ssm_c_re[0], ssm_c_im[0], ssm_d, ssm_log_step, attn_sinks, rel_bias_table, ln_gain, ln_bias)
    loss_sum = _unpack(g_packed)[1]

    grad, delta, new_m, new_v = {}, {}, {}, {}
    for k in BIG:
        grad[k], delta[k], new_m[k], new_v[k] = _adamw_shard(w[k], m[k], v[k], g_shard[k], core,
                                                             name="adamw_" + k, rows=SHARE_ROWS[k])
    gs, ds, ms, vs = _adamw(_pack(w), _pack(m), _pack(v), [g_packed], name="adamw_small")
    for dst, packed in ((grad, gs), (delta, ds), (new_m, ms), (new_v, vs)):
        dst.update(_unpack(packed)[0])

    return (loss_sum, grad_x[None], *[grad[k] for k in WEIGHTS], *[delta[k] for k in WEIGHTS],
            *[new_m[k] for k in WEIGHTS], *[new_v[k] for k in WEIGHTS])
```

```python
import functools
import math

import numpy as np
import jax
import jax.numpy as jnp
from jax import lax
from jax.experimental import pallas as pl
from jax.experimental.pallas import tpu as pltpu

F32 = jnp.float32
BF16 = jnp.bfloat16

D_MODEL = 2048
D_SSM = 1024
SSM_GROUP = 16
N_GROUPS = 64
SSM_STATE = 64
N_LANES = N_GROUPS * SSM_STATE
N_Q_HEADS = 16
N_KV_HEADS = 4
HEAD_DIM = 64
Q_PER_KV = 4
D_ATTN = 1024
D_KV = 256
WINDOW = 128
BLOCK = 128
N_BUCKETS = 32
MAX_DISTANCE = 128
D_IN = 8704
ALPHA = 2.0 ** 0.25
LN_EPS = 1e-5
NEG_INF = -1e30
ATTN_SCALE = HEAD_DIM ** -0.5

C_U, C_ZS, C_Q, C_K, C_V, C_ZA, C_GL = 0, 1024, 2048, 3072, 3328, 3584, 4608

ADAM_LR = 0.001
ADAM_B1 = 0.9
ADAM_B2 = 0.999
ADAM_EPS = 1e-08
ADAM_WD = 0.01
ADAM_STEP = 10

N_CHIPS = 4
MESH = pl.DeviceIdType.MESH

SSM_CHUNK = 256
SEG_LEN = SSM_CHUNK // 8
SLAB_LANES = 512
N_SLABS = N_LANES // SLAB_LANES
SLAB_CH = D_SSM // N_SLABS

VMEM_LIMIT = 60 * 1024 * 1024


def _params(sem=None, **kw):
    return pltpu.CompilerParams(dimension_semantics=sem, vmem_limit_bytes=VMEM_LIMIT, **kw)


_DIMS = {"nn": (((1,), (0,)), ((), ())), "nt": (((1,), (1,)), ((), ())), "tn": (((0,), (0,)), ((), ()))}


class _Exchange:
    def __init__(self, ins, out_shape, n_sems, copies):
        self.ins, self.out_shape, self.n_sems, self.copies = list(ins), list(out_shape), list(n_sems), copies

    def start(self, ins, outs, sems):
        remote, local = self.copies(ins, outs, sems)
        for cp in remote + local:
            cp.start()

    def finish(self, ins, outs, sems):
        remote, local = self.copies(ins, outs, sems)
        for cp in remote:
            cp.wait_recv()
        for cp in remote:
            cp.wait_send()
        for cp in local:
            cp.wait()


def _mm_body(*refs, dims, nk, res_scale, exchange, grid):
    n_ex_in = len(exchange.ins) if exchange else 0
    n_ex_out = len(exchange.out_shape) if exchange else 0
    n_in = 2 + (res_scale is not None) + n_ex_in
    a_ref, b_ref = refs[0], refs[1]
    r_ref = refs[2] if res_scale is not None else None
    ex_in = refs[n_in - n_ex_in:n_in]
    o_ref, ex_out = refs[n_in], refs[n_in + 1:n_in + 1 + n_ex_out]
    acc_ref, ex_sems = refs[n_in + 1 + n_ex_out], refs[n_in + 2 + n_ex_out:]
    i, j, k = pl.program_id(0), pl.program_id(1), pl.program_id(2)

    if exchange:
        @pl.when(jnp.logical_and(jnp.logical_and(i == 0, j == 0), k == 0))
        def _():
            exchange.start(ex_in, ex_out, ex_sems)

    part = lax.dot_general(a_ref[...].astype(BF16), b_ref[...].astype(BF16), _DIMS[dims],
                           preferred_element_type=F32)

    def finish(acc):
        if r_ref is not None:
            acc = acc + res_scale * r_ref[...]
        o_ref[...] = acc.astype(o_ref.dtype)

    if nk == 1:
        finish(part)
    else:
        @pl.when(k == 0)
        def _():
            acc_ref[...] = part

        @pl.when(k > 0)
        def _():
            acc_ref[...] += part

        @pl.when(k == nk - 1)
        def _():
            finish(acc_ref[...])

    if exchange:
        @pl.when(jnp.logical_and(jnp.logical_and(i == grid[0] - 1, j == grid[1] - 1), k == grid[2] - 1))
        def _():
            exchange.finish(ex_in, ex_out, ex_sems)


def _matmul(a, b, dims, *, name, out_dtype=F32, tm=512, tn=512, tk=None, res=None, res_scale=None, exchange=None):
    if dims == "nn":
        (m, kk), n = a.shape, b.shape[1]
    elif dims == "nt":
        (m, kk), n = a.shape, b.shape[0]
    else:
        (kk, m), n = a.shape, b.shape[1]
    tm, tn = min(tm, m), min(tn, n)
    tk = kk if tk is None else min(tk, kk)
    assert m % tm == 0 and n % tn == 0 and kk % tk == 0, (name, m, n, kk, tm, tn, tk)
    nk = kk // tk
    a_spec = {"nn": pl.BlockSpec((tm, tk), lambda i, j, k: (i, k)),
              "nt": pl.BlockSpec((tm, tk), lambda i, j, k: (i, k)),
              "tn": pl.BlockSpec((tk, tm), lambda i, j, k: (k, i))}[dims]
    b_spec = {"nn": pl.BlockSpec((tk, tn), lambda i, j, k: (k, j)),
              "nt": pl.BlockSpec((tn, tk), lambda i, j, k: (j, k)),
              "tn": pl.BlockSpec((tk, tn), lambda i, j, k: (k, j))}[dims]
    in_specs, args = [a_spec, b_spec], [a, b]
    if res is not None:
        in_specs.append(pl.BlockSpec((tm, tn), lambda i, j, k: (i, j)))
        args.append(res)
    grid = (m // tm, n // tn, nk)
    out_specs = [pl.BlockSpec((tm, tn), lambda i, j, k: (i, j))]
    out_shape = [jax.ShapeDtypeStruct((m, n), out_dtype)]
    scratch = [pltpu.VMEM((tm, tn), F32)]
    if exchange:
        in_specs += [pl.BlockSpec(memory_space=pl.ANY)] * len(exchange.ins)
        args += exchange.ins
        out_specs += [pl.BlockSpec(memory_space=pl.ANY)] * len(exchange.out_shape)
        out_shape += exchange.out_shape
        scratch += [pltpu.SemaphoreType.DMA((s,)) for s in exchange.n_sems]
    outs = pl.pallas_call(
        functools.partial(_mm_body, dims=dims, nk=nk, res_scale=res_scale if res is not None else None,
                          exchange=exchange, grid=grid),
        name=name, grid=grid, in_specs=in_specs, out_specs=out_specs, out_shape=out_shape, scratch_shapes=scratch,
        compiler_params=_params(("arbitrary",) * 3 if exchange else ("parallel", "parallel", "arbitrary")),
    )(*args)
    return (outs[0], outs[1:]) if exchange else outs[0]


def _sigmoid(v):
    return 1.0 / (1.0 + jnp.exp(-v))


def _silu_and_grad(z):
    s = _sigmoid(z)
    return z * s, s * (1.0 + z * (1.0 - s))


def _cast_body(x_ref, o_ref):
    o_ref[...] = x_ref[...].astype(o_ref.dtype)


def _cast_and_transpose_body(x_ref, o_ref, ot_ref):
    xb = x_ref[...].astype(BF16)
    o_ref[...] = xb
    ot_ref[...] = xb.T


def _cast_and_transpose(x, *, name, rows=512):
    m, n = x.shape
    rows = min(rows, m)
    return pl.pallas_call(
        functools.partial(_cast_and_transpose_body), name=name, grid=(m // rows,),
        in_specs=[pl.BlockSpec((rows, n), lambda i: (i, 0))],
        out_specs=[pl.BlockSpec((rows, n), lambda i: (i, 0)), pl.BlockSpec((n, rows), lambda i: (0, i))],
        out_shape=[jax.ShapeDtypeStruct((m, n), BF16), jax.ShapeDtypeStruct((n, m), BF16)],
        compiler_params=_params(("parallel",)),
    )(x)


def _cast(x, dtype, *, name, rows=512):
    m, n = x.shape
    rows = min(rows, m)
    return pl.pallas_call(
        functools.partial(_cast_body), name=name, grid=(m // rows,),
        in_specs=[pl.BlockSpec((rows, n), lambda i: (i, 0))],
        out_specs=pl.BlockSpec((rows, n), lambda i: (i, 0)),
        out_shape=jax.ShapeDtypeStruct((m, n), dtype),
        compiler_params=_params(("parallel",)),
    )(x)


def _lam_bar(lr, li, ls):
    step = jnp.exp(ls)
    er = jnp.exp(lr * step)
    return step, er * jnp.cos(li * step), er * jnp.sin(li * step)


def _ssm_pow_body(lr_ref, li_ref, ls_ref, pr_ref, pi_ref):
    _, ar, ai = _lam_bar(lr_ref[...], li_ref[...], ls_ref[...])
    cr, ci = ar, ai
    for i in range(SEG_LEN):
        pr_ref[pl.ds(i, 1), :] = cr
        pi_ref[pl.ds(i, 1), :] = ci
        cr, ci = cr * ar - ci * ai, cr * ai + ci * ar


def _ssm_bbar_body(lr_ref, li_ref, ls_ref, br_ref, bi_ref, or_ref, oi_ref):
    lr, li = lr_ref[...], li_ref[...]
    _, ar, ai = _lam_bar(lr, li, ls_ref[...])
    d = lr * lr + li * li
    ir, ii = lr / d, -li / d
    nr, ni = ar - 1.0, ai
    cr, ci = nr * ir - ni * ii, nr * ii + ni * ir
    br, bi = br_ref[...], bi_ref[...]
    or_ref[...] = cr * br - ci * bi
    oi_ref[...] = cr * bi + ci * br


def _ssm_param_bwd_body(lr_ref, li_ref, ls_ref, br_ref, bi_ref, glr_ref, gli_ref, gbr_ref, gbi_ref,
                        dlr_ref, dli_ref, dls_ref, dbr_ref, dbi_ref):
    lr, li = lr_ref[...], li_ref[...]
    step, ar, ai = _lam_bar(lr, li, ls_ref[...])
    d = lr * lr + li * li
    ir, ii = lr / d, -li / d
    nr, ni = ar - 1.0, ai
    cr, ci = nr * ir - ni * ii, nr * ii + ni * ir
    br, bi, gbr, gbi = br_ref[...], bi_ref[...], gbr_ref[...], gbi_ref[...]
    dbr_ref[...] = cr * gbr + ci * gbi
    dbi_ref[...] = cr * gbi - ci * gbr
    gcr = jnp.sum(br * gbr + bi * gbi, axis=1, keepdims=True)
    gci = jnp.sum(br * gbi - bi * gbr, axis=1, keepdims=True)
    gnr, gni = ir * gcr + ii * gci, ir * gci - ii * gcr
    gir, gii = nr * gcr + ni * gci, nr * gci - ni * gcr
    gtr, gti = glr_ref[...] + gnr, gli_ref[...] + gni
    i2r, i2i = ir * ir - ii * ii, 2.0 * ir * ii
    g1r, g1i = -(i2r * gir + i2i * gii), -(i2r * gii - i2i * gir)
    g2r, g2i = step * (ar * gtr + ai * gti), step * (ar * gti - ai * gtr)
    mr, mi = lr * ar - li * ai, lr * ai + li * ar
    dlr_ref[...] = g1r + g2r
    dli_ref[...] = g1i + g2i
    dls_ref[...] = (mr * gtr + mi * gti) * step


def _whole(shape):
    return pl.BlockSpec(shape, lambda *_: (0,) * len(shape))


def _ssm_prepare(lam_re, lam_im, b_re, b_im, c_re, c_im, log_step):
    row = lambda a: a.reshape(1, N_LANES)
    col = lambda a: a.reshape(N_LANES, 1)
    ls = jnp.repeat(log_step.reshape(N_GROUPS), SSM_STATE)
    p_re, p_im = pl.pallas_call(
        functools.partial(_ssm_pow_body), name="ssm_pow",
        in_specs=[_whole((1, N_LANES))] * 3, out_specs=[_whole((SEG_LEN, N_LANES))] * 2,
        out_shape=[jax.ShapeDtypeStruct((SEG_LEN, N_LANES), F32)] * 2, grid=(1,),
    )(row(lam_re), row(lam_im), row(ls))
    bb_re, bb_im = pl.pallas_call(
        functools.partial(_ssm_bbar_body), name="ssm_bbar",
        in_specs=[_whole((N_LANES, 1))] * 3 + [_whole((N_LANES, SSM_GROUP))] * 2,
        out_specs=[_whole((N_LANES, SSM_GROUP))] * 2,
        out_shape=[jax.ShapeDtypeStruct((N_LANES, SSM_GROUP), F32)] * 2, grid=(1,),
    )(col(lam_re), col(lam_im), col(ls), b_re.reshape(N_LANES, SSM_GROUP), b_im.reshape(N_LANES, SSM_GROUP))
    eye = jnp.eye(8, dtype=F32)

    def b_slabs(bb):
        t = bb.reshape(N_SLABS, 8, SSM_STATE, SSM_GROUP).transpose(0, 1, 3, 2)
        return (t[:, :, :, None, :] * eye[None, :, None, :, None]).reshape(N_SLABS, SLAB_CH, SLAB_LANES)

    def c_slabs(c):
        t = c.reshape(N_SLABS, 8, SSM_GROUP, SSM_STATE).transpose(0, 1, 3, 2)
        return (t[:, :, :, None, :] * eye[None, :, None, :, None]).reshape(N_SLABS, SLAB_LANES, SLAB_CH)

    b_bd = jnp.concatenate([b_slabs(bb_re), b_slabs(bb_im)], axis=2).astype(BF16)
    c_bd = jnp.concatenate([c_slabs(c_re.reshape(N_GROUPS, SSM_GROUP, SSM_STATE)),
                            -c_slabs(c_im.reshape(N_GROUPS, SSM_GROUP, SSM_STATE))], axis=1).astype(BF16)
    return p_re, p_im, b_bd, c_bd


def _diag_blocks_b(g):
    t = g.reshape(N_SLABS, 8, SSM_GROUP, 8, SSM_STATE)
    t = jnp.stack([t[:, i, :, i, :] for i in range(8)], axis=1)
    return t.transpose(0, 1, 3, 2).reshape(N_LANES, SSM_GROUP)


def _diag_blocks_c(g):
    t = g.reshape(N_SLABS, 8, SSM_STATE, 8, SSM_GROUP)
    t = jnp.stack([t[:, i, :, i, :] for i in range(8)], axis=1)
    return t.transpose(0, 1, 3, 2).reshape(N_GROUPS, SSM_GROUP, SSM_STATE)


def _ssm_param_grads(lam_re, lam_im, b_re, b_im, log_step, g_lam_re, g_lam_im, g_bbd, g_cbd):
    col = lambda a: a.reshape(N_LANES, 1)
    ls = jnp.repeat(log_step.reshape(N_GROUPS), SSM_STATE)
    gbr = _diag_blocks_b(g_bbd[:, :, :SLAB_LANES])
    gbi = _diag_blocks_b(g_bbd[:, :, SLAB_LANES:])
    outs = pl.pallas_call(
        functools.partial(_ssm_param_bwd_body), name="ssm_param_bwd", grid=(1,),
        in_specs=[_whole((N_LANES, 1))] * 3 + [_whole((N_LANES, SSM_GROUP))] * 2 + [_whole((N_LANES, 1))] * 2
        + [_whole((N_LANES, SSM_GROUP))] * 2,
        out_specs=[_whole((N_LANES, 1))] * 3 + [_whole((N_LANES, SSM_GROUP))] * 2,
        out_shape=[jax.ShapeDtypeStruct((N_LANES, 1), F32)] * 3 + [jax.ShapeDtypeStruct((N_LANES, SSM_GROUP), F32)] * 2,
    )(col(lam_re), col(lam_im), col(ls), b_re.reshape(N_LANES, SSM_GROUP), b_im.reshape(N_LANES, SSM_GROUP),
      col(g_lam_re), col(g_lam_im), gbr, gbi)
    dlr, dli, dls, dbr, dbi = outs
    d_c_re = _diag_blocks_c(g_cbd[:, :SLAB_LANES, :])
    d_c_im = -_diag_blocks_c(g_cbd[:, SLAB_LANES:, :])
    return (dlr.reshape(1, N_GROUPS, SSM_STATE), dli.reshape(1, N_GROUPS, SSM_STATE),
            dbr.reshape(1, N_GROUPS, SSM_STATE, SSM_GROUP), dbi.reshape(1, N_GROUPS, SSM_STATE, SSM_GROUP),
            d_c_re[None], d_c_im[None], dls.reshape(N_GROUPS, SSM_STATE).sum(axis=1).reshape(1, N_GROUPS))


def _bcast8(v):
    return jnp.broadcast_to(v, (8, v.shape[1]))


def _segment_permutation():
    p = np.zeros((SSM_CHUNK, SSM_CHUNK), np.float32)
    rows = np.arange(SSM_CHUNK)
    p[rows, (rows % 8) * SEG_LEN + rows // 8] = 1.0
    return p


def _permute_exact(perm, val, pieces):
    out, rest = None, val
    for n in range(pieces):
        part = rest.astype(BF16)
        moved = jnp.dot(perm, part, preferred_element_type=F32)
        out = moved if out is None else out + moved
        if n + 1 < pieces:
            rest = rest - part.astype(F32)
    return out


def _scan_buffer():
    return pltpu.VMEM((SSM_CHUNK, N_LANES), F32)


def _lanes(k):
    return pl.ds(k * SLAB_LANES, SLAB_LANES)


def _tile(i):
    return pl.ds(i * 8 if isinstance(i, int) else pl.multiple_of(i * 8, 8), 8)


def _seg_get(ref, k, i):
    return ref[_tile(i), _lanes(k)]


def _seg_put(ref, k, i, val):
    ref[_tile(i), _lanes(k)] = val


def _slab_get(ref, k):
    return ref[:, _lanes(k)]


def _slab_put(ref, k, val):
    ref[:, _lanes(k)] = val


def _scan_forward(s_re, s_im, p_re, p_im, car_re, car_im, sp_re=None, sp_im=None):
    for k in range(N_SLABS):
        ln = pl.ds(k * SLAB_LANES, SLAB_LANES)
        ar, ai = _bcast8(p_re[pl.ds(0, 1), ln]), _bcast8(p_im[pl.ds(0, 1), ln])

        def step(i, s, k=k, ar=ar, ai=ai):
            sr, si = s
            nr = ar * sr - ai * si + _seg_get(s_re, k, i)
            ni = ar * si + ai * sr + _seg_get(s_im, k, i)
            _seg_put(s_re, k, i, nr)
            _seg_put(s_im, k, i, ni)
            return nr, ni

        zero = jnp.zeros((8, SLAB_LANES), F32)
        er, ei = lax.fori_loop(0, SEG_LEN, step, (zero, zero), unroll=4)
        lr, li = p_re[pl.ds(SEG_LEN - 1, 1), ln], p_im[pl.ds(SEG_LEN - 1, 1), ln]
        cr, ci = car_re[pl.ds(0, 1), ln], car_im[pl.ds(0, 1), ln]
        rows_r, rows_i = [], []
        for r in range(8):
            rows_r.append(cr)
            rows_i.append(ci)
            cr, ci = er[r:r + 1] + lr * cr - li * ci, ei[r:r + 1] + lr * ci + li * cr
        pr8, pi8 = jnp.concatenate(rows_r, axis=0), jnp.concatenate(rows_i, axis=0)
        car_re[:, ln] = _bcast8(cr)
        car_im[:, ln] = _bcast8(ci)
        if sp_re is not None:
            sp_re[:, ln] = pr8
            sp_im[:, ln] = pi8

        def fix(i, _, k=k, ln=ln, pr8=pr8, pi8=pi8):
            qr, qi = _bcast8(p_re[pl.ds(i, 1), ln]), _bcast8(p_im[pl.ds(i, 1), ln])
            _seg_put(s_re, k, i, _seg_get(s_re, k, i) + qr * pr8 - qi * pi8)
            _seg_put(s_im, k, i, _seg_get(s_im, k, i) + qr * pi8 + qi * pr8)
            return 0

        lax.fori_loop(0, SEG_LEN, fix, 0, unroll=4)


def _scan_backward(g_re, g_im, s_re, s_im, sp_re, sp_im, p_re, p_im, car_re, car_im, acc_re, acc_im):
    for k in range(N_SLABS):
        ln = pl.ds(k * SLAB_LANES, SLAB_LANES)
        ar, ai = _bcast8(p_re[pl.ds(0, 1), ln]), -_bcast8(p_im[pl.ds(0, 1), ln])

        def step(j, s, k=k, ar=ar, ai=ai):
            i = SEG_LEN - 1 - j
            sr, si = s
            nr = ar * sr - ai * si + _seg_get(g_re, k, i)
            ni = ar * si + ai * sr + _seg_get(g_im, k, i)
            _seg_put(g_re, k, i, nr)
            _seg_put(g_im, k, i, ni)
            return nr, ni

        zero = jnp.zeros((8, SLAB_LANES), F32)
        er, ei = lax.fori_loop(0, SEG_LEN, step, (zero, zero), unroll=4)
        lr, li = p_re[pl.ds(SEG_LEN - 1, 1), ln], -p_im[pl.ds(SEG_LEN - 1, 1), ln]
        cr, ci = car_re[pl.ds(0, 1), ln], car_im[pl.ds(0, 1), ln]
        rows_r, rows_i = [None] * 8, [None] * 8
        for r in range(7, -1, -1):
            rows_r[r], rows_i[r] = cr, ci
            cr, ci = er[r:r + 1] + lr * cr - li * ci, ei[r:r + 1] + lr * ci + li * cr
        nr8, ni8 = jnp.concatenate(rows_r, axis=0), jnp.concatenate(rows_i, axis=0)
        car_re[:, ln] = _bcast8(cr)
        car_im[:, ln] = _bcast8(ci)

        def fix(i, acc, k=k, ln=ln, nr8=nr8, ni8=ni8):
            qr = _bcast8(p_re[pl.ds(SEG_LEN - 1 - i, 1), ln])
            qi = -_bcast8(p_im[pl.ds(SEG_LEN - 1 - i, 1), ln])
            gr = _seg_get(g_re, k, i) + qr * nr8 - qi * ni8
            gi = _seg_get(g_im, k, i) + qr * ni8 + qi * nr8
            _seg_put(g_re, k, i, gr)
            _seg_put(g_im, k, i, gi)
            return gr, gi

        def prod(xr, xi, gr, gi):
            return xr * gr + xi * gi, xr * gi - xi * gr

        gr, gi = fix(0, None)
        a_r, a_i = prod(sp_re[:, ln], sp_im[:, ln], gr, gi)

        def fix_acc(i, acc, k=k, fix=fix):
            gr, gi = fix(i, None)
            dr, di = prod(_seg_get(s_re, k, i - 1), _seg_get(s_im, k, i - 1), gr, gi)
            return acc[0] + dr, acc[1] + di

        acc = (a_r, a_i)
        for i in range(1, 4):
            acc = fix_acc(i, acc)
        a_r, a_i = lax.fori_loop(4, SEG_LEN, fix_acc, acc, unroll=4)
        acc_re[:, ln] += a_r
        acc_im[:, ln] += a_i


def _gelu_and_grad(y):
    cdf = 0.5 * (1.0 + lax.erf(y * (2.0 ** -0.5)))
    pdf = jnp.exp(-0.5 * y * y) * (1.0 / math.sqrt(2.0 * math.pi))
    return y * cdf, cdf + y * pdf


def _ssm_fwd_body(u_ref, bbd_ref, cbd_ref, pre_ref, pim_ref, d_ref, perm_ref, unperm_ref,
                  y_ref, gin_ref, gint_ref, cre_out, cim_out, s_re, s_im, car_re, car_im, yp):
    c = pl.program_id(0)

    @pl.when(c == 0)
    def _():
        car_re[...] = jnp.zeros_like(car_re)
        car_im[...] = jnp.zeros_like(car_im)

    cre_out[...] = car_re[...]
    cim_out[...] = car_im[...]
    u = u_ref[...].astype(F32)
    up = jnp.dot(perm_ref[...], u_ref[...].astype(BF16), preferred_element_type=F32).astype(BF16)
    for k in range(N_SLABS):
        bu = jnp.dot(up[:, k * SLAB_CH:(k + 1) * SLAB_CH], bbd_ref[k], preferred_element_type=F32)
        _slab_put(s_re, k, bu[:, :SLAB_LANES])
        _slab_put(s_im, k, bu[:, SLAB_LANES:])
    _scan_forward(s_re, s_im, pre_ref, pim_ref, car_re, car_im)
    for k in range(N_SLABS):
        yp[:, pl.ds(k * SLAB_CH, SLAB_CH)] = (
            jnp.dot(_slab_get(s_re, k).astype(BF16), cbd_ref[k, :SLAB_LANES, :], preferred_element_type=F32)
            + jnp.dot(_slab_get(s_im, k).astype(BF16), cbd_ref[k, SLAB_LANES:, :], preferred_element_type=F32))
    y = _permute_exact(unperm_ref[...], yp[...], 2) + d_ref[...] * u
    y_ref[...] = y.astype(y_ref.dtype)
    _store_with_transpose(_gelu_and_grad(y)[0], gin_ref, gint_ref)


def _ssm_forward(proj, p_re, p_im, b_bd, c_bd, d_skip):
    t = proj.shape[0]
    nc = t // SSM_CHUNK
    perm = _segment_permutation()
    return pl.pallas_call(
        functools.partial(_ssm_fwd_body), name="ssm_fwd", grid=(nc,),
        in_specs=[pl.BlockSpec((SSM_CHUNK, D_SSM), lambda c: (c, C_U // D_SSM)),
                  _whole(b_bd.shape), _whole(c_bd.shape), _whole(p_re.shape), _whole(p_im.shape),
                  _whole((1, D_SSM)), _whole(perm.shape), _whole(perm.shape)],
        out_specs=[pl.BlockSpec((SSM_CHUNK, D_SSM), lambda c: (c, 0)),
                   pl.BlockSpec((SSM_CHUNK, D_SSM), lambda c: (c, 0)),
                   pl.BlockSpec((D_SSM, SSM_CHUNK), lambda c: (0, c)),
                   pl.BlockSpec((None, 8, N_LANES), lambda c: (c, 0, 0)),
                   pl.BlockSpec((None, 8, N_LANES), lambda c: (c, 0, 0))],
        out_shape=[jax.ShapeDtypeStruct((t, D_SSM), BF16), jax.ShapeDtypeStruct((t, D_SSM), BF16),
                   jax.ShapeDtypeStruct((D_SSM, t), BF16),
                   jax.ShapeDtypeStruct((nc, 8, N_LANES), F32), jax.ShapeDtypeStruct((nc, 8, N_LANES), F32)],
        scratch_shapes=[_scan_buffer(), _scan_buffer(),
                        pltpu.VMEM((8, N_LANES), F32), pltpu.VMEM((8, N_LANES), F32),
                        pltpu.VMEM((SSM_CHUNK, D_SSM), F32)],
        compiler_params=_params(("arbitrary",)),
    )(proj, b_bd, c_bd, p_re, p_im, d_skip, jnp.asarray(perm, BF16), jnp.asarray(perm.T, BF16))


def _ssm_bwd_body(u_ref, dgin_ref, y_ref, cre_in, cim_in, bbd_ref, cbd_ref, pre_ref, pim_ref, d_ref, perm_ref,
                  unperm_ref, dproj_in,
                  du_ref, gb_ref, gc_ref, glr_ref, gli_ref, gd_ref,
                  s_re, s_im, g_re, g_im, sp_re, sp_im, car_re, car_im, gcar_re, gcar_im, acc_re, acc_im, dup):
    del dproj_in
    c = pl.program_id(0)
    nc = pl.num_programs(0)

    @pl.when(c == 0)
    def _():
        gcar_re[...] = jnp.zeros_like(gcar_re)
        gcar_im[...] = jnp.zeros_like(gcar_im)
        acc_re[...] = jnp.zeros_like(acc_re)
        acc_im[...] = jnp.zeros_like(acc_im)
        gb_ref[...] = jnp.zeros_like(gb_ref)
        gc_ref[...] = jnp.zeros_like(gc_ref)
        gd_ref[...] = jnp.zeros_like(gd_ref)

    car_re[...] = cre_in[...]
    car_im[...] = cim_in[...]
    u = u_ref[...].astype(F32)
    dy = dgin_ref[...].astype(F32) * _gelu_and_grad(y_ref[...].astype(F32))[1]
    gd_ref[...] += jnp.sum(dy * u, axis=0, keepdims=True)
    up = jnp.dot(perm_ref[...], u.astype(BF16), preferred_element_type=F32).astype(BF16)
    dyp = jnp.dot(perm_ref[...], dy.astype(BF16), preferred_element_type=F32).astype(BF16)
    for k in range(N_SLABS):
        ch = slice(k * SLAB_CH, (k + 1) * SLAB_CH)
        bu = jnp.dot(up[:, ch], bbd_ref[k], preferred_element_type=F32)
        _slab_put(s_re, k, bu[:, :SLAB_LANES])
        _slab_put(s_im, k, bu[:, SLAB_LANES:])
        ds = lax.dot_general(dyp[:, ch], cbd_ref[k], _DIMS["nt"], preferred_element_type=F32)
        _slab_put(g_re, k, ds[:, :SLAB_LANES])
        _slab_put(g_im, k, ds[:, SLAB_LANES:])
    _scan_forward(s_re, s_im, pre_ref, pim_ref, car_re, car_im, sp_re, sp_im)
    _scan_backward(g_re, g_im, s_re, s_im, sp_re, sp_im, pre_ref, pim_ref, gcar_re, gcar_im, acc_re, acc_im)
    for k in range(N_SLABS):
        ch = slice(k * SLAB_CH, (k + 1) * SLAB_CH)
        uk, dyk = up[:, ch], dyp[:, ch]
        sr, si = _slab_get(s_re, k).astype(BF16), _slab_get(s_im, k).astype(BF16)
        gr, gi = _slab_get(g_re, k).astype(BF16), _slab_get(g_im, k).astype(BF16)
        gc_ref[k, :SLAB_LANES, :] += lax.dot_general(sr, dyk, _DIMS["tn"], preferred_element_type=F32)
        gc_ref[k, SLAB_LANES:, :] += lax.dot_general(si, dyk, _DIMS["tn"], preferred_element_type=F32)
        gb_ref[k, :, :SLAB_LANES] += lax.dot_general(uk, gr, _DIMS["tn"], preferred_element_type=F32)
        gb_ref[k, :, SLAB_LANES:] += lax.dot_general(uk, gi, _DIMS["tn"], preferred_element_type=F32)
        dup[:, pl.ds(k * SLAB_CH, SLAB_CH)] = (
            lax.dot_general(gr, bbd_ref[k, :, :SLAB_LANES], _DIMS["nt"], preferred_element_type=F32)
            + lax.dot_general(gi, bbd_ref[k, :, SLAB_LANES:], _DIMS["nt"], preferred_element_type=F32))
    du = _permute_exact(unperm_ref[...], dup[...], 2) + d_ref[...] * dy
    du_ref[...] = du.astype(du_ref.dtype)

    @pl.when(c == nc - 1)
    def _():
        glr_ref[...] = jnp.sum(acc_re[...], axis=0, keepdims=True)
        gli_ref[...] = jnp.sum(acc_im[...], axis=0, keepdims=True)


def _ssm_backward(proj, dg_in, y_ssm, car_re, car_im, p_re, p_im, b_bd, c_bd, d_skip, dproj):
    t = proj.shape[0]
    nc = t // SSM_CHUNK
    rev = lambda c: nc - 1 - c
    big = _scan_buffer
    small = lambda: pltpu.VMEM((8, N_LANES), F32)
    perm = _segment_permutation()
    outs = pl.pallas_call(
        functools.partial(_ssm_bwd_body), name="ssm_bwd", grid=(nc,),
        in_specs=[pl.BlockSpec((SSM_CHUNK, D_SSM), lambda c: (rev(c), C_U // D_SSM)),
                  pl.BlockSpec((SSM_CHUNK, D_SSM), lambda c: (rev(c), 0)),
                  pl.BlockSpec((SSM_CHUNK, D_SSM), lambda c: (rev(c), 0)),
                  pl.BlockSpec((None, 8, N_LANES), lambda c: (rev(c), 0, 0)),
                  pl.BlockSpec((None, 8, N_LANES), lambda c: (rev(c), 0, 0)),
                  _whole(b_bd.shape), _whole(c_bd.shape), _whole(p_re.shape), _whole(p_im.shape),
                  _whole((1, D_SSM)), _whole(perm.shape), _whole(perm.shape), pl.BlockSpec(memory_space=pl.ANY)],
        out_specs=[pl.BlockSpec((SSM_CHUNK, D_SSM), lambda c: (rev(c), C_U // D_SSM)),
                   _whole(b_bd.shape), _whole(c_bd.shape), _whole((1, N_LANES)), _whole((1, N_LANES)),
                   _whole((1, D_SSM))],
        out_shape=[jax.ShapeDtypeStruct(dproj.shape, dproj.dtype),
                   jax.ShapeDtypeStruct(b_bd.shape, F32), jax.ShapeDtypeStruct(c_bd.shape, F32),
                   jax.ShapeDtypeStruct((1, N_LANES), F32), jax.ShapeDtypeStruct((1, N_LANES), F32),
                   jax.ShapeDtypeStruct((1, D_SSM), F32)],
        scratch_shapes=[big(), big(), big(), big()] + [small() for _ in range(8)]
        + [pltpu.VMEM((SSM_CHUNK, D_SSM), F32)],
        input_output_aliases={12: 0},
        compiler_params=_params(("arbitrary",)),
    )(proj, dg_in, y_ssm, car_re, car_im, b_bd, c_bd, p_re, p_im, d_skip, jnp.asarray(perm, BF16),
      jnp.asarray(perm.T, BF16), dproj)
    return outs


def _bucket_table():
    i = np.arange(BLOCK)[:, None]
    j = np.arange(2 * BLOCK)[None, :]
    dist = BLOCK + i - j
    ok = (dist >= 0) & (dist < WINDOW)
    max_exact = N_BUCKETS // 2
    d = np.maximum(dist, 1).astype(np.float32)
    large = max_exact + (np.log(d / max_exact) / math.log(MAX_DISTANCE / max_exact)
                         * (N_BUCKETS - max_exact)).astype(np.int32)
    large = np.minimum(large, N_BUCKETS - 1)
    bucket = np.where(dist < max_exact, dist, large)
    return np.where(ok, bucket, -1).astype(np.int32)


def _bias_build_body(table_ref, bucket_ref, o_ref):
    h = pl.program_id(0)
    bucket = bucket_ref[...]
    acc = jnp.full(bucket.shape, NEG_INF, F32)
    for b in range(N_BUCKETS):
        acc = jnp.where(bucket == b, table_ref[b, h], acc)
    o_ref[...] = acc


def _bias_build(rel_bias_table, bucket):
    return pl.pallas_call(
        functools.partial(_bias_build_body), name="bias_build", grid=(N_Q_HEADS,),
        in_specs=[pl.BlockSpec(memory_space=pltpu.SMEM), _whole(bucket.shape)],
        out_specs=pl.BlockSpec((None, BLOCK, 2 * BLOCK), lambda h: (h, 0, 0)),
        out_shape=jax.ShapeDtypeStruct((N_Q_HEADS, BLOCK, 2 * BLOCK), F32),
        compiler_params=_params(("arbitrary",)),
    )(rel_bias_table, bucket)


def _bias_grad_body(g_ref, bucket_ref, o_ref):
    bucket = bucket_ref[...]
    lane = lax.broadcasted_iota(jnp.int32, (N_BUCKETS, 128), 1)

    def head(h, out):
        g = g_ref[h]
        rows = [jnp.sum(jnp.where(bucket == b, g, 0.0), axis=0, keepdims=True) for b in range(N_BUCKETS)]
        colsum = jnp.sum(jnp.concatenate(rows, axis=0), axis=1, keepdims=True)
        return jnp.where(lane == h, colsum, out)

    o_ref[...] = lax.fori_loop(0, N_Q_HEADS, head, jnp.zeros((N_BUCKETS, 128), F32))


def _bias_grad(g_bias, bucket):
    out = pl.pallas_call(
        functools.partial(_bias_grad_body), name="bias_grad", grid=(1,),
        in_specs=[_whole(g_bias.shape), _whole(bucket.shape)],
        out_specs=_whole((N_BUCKETS, 128)),
        out_shape=jax.ShapeDtypeStruct((N_BUCKETS, 128), F32),
        compiler_params=_params(("arbitrary",)),
    )(g_bias, bucket)
    return out[:, :N_Q_HEADS]


def _head_logits(qh, kk, bias_h, first_block):
    s = lax.dot_general(qh, kk, _DIMS["nt"], preferred_element_type=F32) * ATTN_SCALE + bias_h
    col = lax.broadcasted_iota(jnp.int32, s.shape, 1)
    return jnp.where(jnp.logical_and(first_block, col < BLOCK), NEG_INF, s)


def _attn_fwd_body(sink_ref, q_ref, kp_ref, kc_ref, vp_ref, vc_ref, bias_ref, o_ref, lse_ref):
    n = pl.program_id(0)
    outs, lses = [], []
    for kv in range(N_KV_HEADS):
        cs = slice(kv * HEAD_DIM, (kv + 1) * HEAD_DIM)
        kk = jnp.concatenate([kp_ref[:, cs], kc_ref[:, cs]], axis=0).astype(BF16)
        vv = jnp.concatenate([vp_ref[:, cs], vc_ref[:, cs]], axis=0).astype(BF16)
        for g in range(Q_PER_KV):
            h = kv * Q_PER_KV + g
            qh = q_ref[:, h * HEAD_DIM:(h + 1) * HEAD_DIM].astype(BF16)
            s = _head_logits(qh, kk, bias_ref[h], n == 0)
            sink = sink_ref[0, h]
            m = jnp.maximum(jnp.max(s, axis=1, keepdims=True), sink)
            p = jnp.exp(s - m)
            den = jnp.sum(p, axis=1, keepdims=True) + jnp.exp(sink - m)
            p = p / den
            outs.append(jnp.dot(p.astype(BF16), vv, preferred_element_type=F32))
            lses.append(m + jnp.log(den))
    o_ref[...] = jnp.concatenate(outs, axis=1).astype(o_ref.dtype)
    lse_ref[...] = jnp.concatenate(lses, axis=1)


def _attn_specs(nb):
    prev = lambda n: jnp.maximum(jnp.minimum(n, nb - 1) - 1, 0)
    cur = lambda n: jnp.minimum(n, nb - 1)
    return [pl.BlockSpec((BLOCK, D_ATTN), lambda n: (cur(n), C_Q // D_ATTN)),
            pl.BlockSpec((BLOCK, D_KV), lambda n: (prev(n), C_K // D_KV)),
            pl.BlockSpec((BLOCK, D_KV), lambda n: (cur(n), C_K // D_KV)),
            pl.BlockSpec((BLOCK, D_KV), lambda n: (prev(n), C_V // D_KV)),
            pl.BlockSpec((BLOCK, D_KV), lambda n: (cur(n), C_V // D_KV))]


def _attn_forward(proj, sinks, bias):
    t = proj.shape[0]
    nb = t // BLOCK
    return pl.pallas_call(
        functools.partial(_attn_fwd_body), name="attn_fwd", grid=(nb,),
        in_specs=[pl.BlockSpec(memory_space=pltpu.SMEM)] + _attn_specs(nb) + [_whole(bias.shape)],
        out_specs=[pl.BlockSpec((BLOCK, D_ATTN), lambda n: (n, 0)),
                   pl.BlockSpec((BLOCK, N_Q_HEADS), lambda n: (n, 0))],
        out_shape=[jax.ShapeDtypeStruct((t, D_ATTN), BF16), jax.ShapeDtypeStruct((t, N_Q_HEADS), F32)],
        compiler_params=_params(("parallel",)),
    )(sinks, proj, proj, proj, proj, proj, bias)


def _attn_bwd_body(sink_ref, q_ref, kp_ref, kc_ref, vp_ref, vc_ref, bias_ref, do_ref, o_ref, lse_ref, dproj_in,
                   dq_ref, dkv_ref, gbias_ref, gsink_ref, carry_ref, *, nb):
    del dproj_in
    n = pl.program_id(0)

    @pl.when(n == 0)
    def _():
        gbias_ref[...] = jnp.zeros_like(gbias_ref)
        gsink_ref[...] = jnp.zeros_like(gsink_ref)
        carry_ref[...] = jnp.zeros_like(carry_ref)

    @pl.when(n < nb)
    def _():
        lane = lax.broadcasted_iota(jnp.int32, (1, 128), 1)
        dqs, dks, dvs = [], [], []
        gsink = jnp.zeros((1, 128), F32)
        for kv in range(N_KV_HEADS):
            cs = slice(kv * HEAD_DIM, (kv + 1) * HEAD_DIM)
            kk = jnp.concatenate([kp_ref[:, cs], kc_ref[:, cs]], axis=0).astype(BF16)
            vv = jnp.concatenate([vp_ref[:, cs], vc_ref[:, cs]], axis=0).astype(BF16)
            dk = jnp.zeros((2 * BLOCK, HEAD_DIM), F32)
            dv = jnp.zeros((2 * BLOCK, HEAD_DIM), F32)
            for g in range(Q_PER_KV):
                h = kv * Q_PER_KV + g
                hs = slice(h * HEAD_DIM, (h + 1) * HEAD_DIM)
                qh = q_ref[:, hs].astype(BF16)
                s = _head_logits(qh, kk, bias_ref[h], n == 0)
                lse = lse_ref[:, h:h + 1]
                p = jnp.exp(s - lse)
                do = do_ref[:, hs].astype(F32)
                delta = jnp.sum(do * o_ref[:, hs].astype(F32), axis=1, keepdims=True)
                dob = do.astype(BF16)
                dp = lax.dot_general(dob, vv, _DIMS["nt"], preferred_element_type=F32)
                dl = p * (dp - delta)
                gbias_ref[h] += dl
                psink = jnp.exp(sink_ref[0, h] - lse)
                gsink = gsink + jnp.where(lane == h, -jnp.sum(psink * delta), 0.0)
                dlb = dl.astype(BF16)
                dqs.append(jnp.dot(dlb, kk, preferred_element_type=F32) * ATTN_SCALE)
                dk = dk + lax.dot_general(dlb, qh, _DIMS["tn"], preferred_element_type=F32) * ATTN_SCALE
                dv = dv + lax.dot_general(p.astype(BF16), dob, _DIMS["tn"], preferred_element_type=F32)
            dks.append(dk)
            dvs.append(dv)
        dq_ref[...] = jnp.concatenate(dqs, axis=1).astype(dq_ref.dtype)
        gsink_ref[...] += gsink
        dkv = jnp.concatenate(dks + dvs, axis=1)
        dkv_ref[...] = (carry_ref[...] + dkv[:BLOCK]).astype(dkv_ref.dtype)
        carry_ref[...] = dkv[BLOCK:]

    @pl.when(n == nb)
    def _():
        dkv_ref[...] = carry_ref[...].astype(dkv_ref.dtype)


def _attn_backward(proj, sinks, bias, d_attn, attn, lse, dproj):
    t = proj.shape[0]
    nb = t // BLOCK
    cur = lambda n: jnp.minimum(n, nb - 1)
    return pl.pallas_call(
        functools.partial(_attn_bwd_body, nb=nb), name="attn_bwd", grid=(nb + 1,),
        in_specs=[pl.BlockSpec(memory_space=pltpu.SMEM)] + _attn_specs(nb) + [
            _whole(bias.shape),
            pl.BlockSpec((BLOCK, D_ATTN), lambda n: (cur(n), 0)),
            pl.BlockSpec((BLOCK, D_ATTN), lambda n: (cur(n), 0)),
            pl.BlockSpec((BLOCK, N_Q_HEADS), lambda n: (cur(n), 0)),
            pl.BlockSpec(memory_space=pl.ANY)],
        out_specs=[pl.BlockSpec((BLOCK, D_ATTN), lambda n: (cur(n), C_Q // D_ATTN)),
                   pl.BlockSpec((BLOCK, 2 * D_KV), lambda n: (jnp.maximum(n - 1, 0), 0)),
                   _whole(bias.shape), _whole((1, 128))],
        out_shape=[jax.ShapeDtypeStruct(dproj.shape, dproj.dtype), jax.ShapeDtypeStruct((t, 2 * D_KV), dproj.dtype),
                   jax.ShapeDtypeStruct(bias.shape, F32), jax.ShapeDtypeStruct((1, 128), F32)],
        scratch_shapes=[pltpu.VMEM((BLOCK, 2 * D_KV), F32)],
        input_output_aliases={10: 0},
        compiler_params=_params(("arbitrary",)),
    )(sinks, proj, proj, proj, proj, proj, bias, d_attn, attn, lse, dproj)


ROWS = 256


def _rowwise(body, name, t, ins, outs, aliases=None):
    rows = min(ROWS, t)

    def col_spec(w, c0):
        if c0 == "T":
            return pl.BlockSpec((w, rows), lambda i: (0, i))
        if c0 % w == 0:
            return pl.BlockSpec((rows, w), lambda i: (i, c0 // w))
        return pl.BlockSpec((pl.Element(rows), pl.Element(w)), lambda i: (i * rows, c0))

    in_specs, args = [], []
    for a, w, c0 in ins:
        args.append(a)
        if w is None:
            in_specs.append(pl.BlockSpec(memory_space=pl.ANY) if c0 == "any" else _whole(a.shape))
        else:
            in_specs.append(col_spec(w, c0))
    out_specs, out_shape = [], []
    for shape, dtype, w, c0 in outs:
        out_shape.append(jax.ShapeDtypeStruct(shape, dtype))
        out_specs.append(_whole(shape) if w is None else col_spec(w, c0))
    accum = any(o[2] is None for o in outs)
    return pl.pallas_call(
        body, name=name, grid=(t // rows,), in_specs=in_specs, out_specs=out_specs, out_shape=out_shape,
        input_output_aliases=aliases or {},
        compiler_params=_params(("arbitrary",) if accum else ("parallel",)),
    )(*args)


def _f32(ref, *idx):
    return (ref[idx] if idx else ref[...]).astype(F32)


def _store_with_transpose(val, ref, t_ref):
    val = val.astype(ref.dtype)
    ref[...] = val
    t_ref[...] = val.T


def _ssm_gate_fwd_body(glu_ref, z_ref, h_ref, ht_ref):
    a, b = _f32(glu_ref, slice(None), slice(0, D_SSM)), _f32(glu_ref, slice(None), slice(D_SSM, None))
    _store_with_transpose((a * _sigmoid(b)) * _silu_and_grad(_f32(z_ref))[0], h_ref, ht_ref)


def _ssm_gate_bwd_body(dh_ref, glu_ref, z_ref, dproj_in, dglu_ref, dz_ref):
    del dproj_in
    a, b = _f32(glu_ref, slice(None), slice(0, D_SSM)), _f32(glu_ref, slice(None), slice(D_SSM, None))
    sb = _sigmoid(b)
    silu, dsilu = _silu_and_grad(_f32(z_ref))
    dh = _f32(dh_ref)
    dg = dh * silu
    dz_ref[...] = (dh * (a * sb) * dsilu).astype(dz_ref.dtype)
    dglu_ref[:, :D_SSM] = (dg * sb).astype(dglu_ref.dtype)
    dglu_ref[:, D_SSM:] = (dg * a * sb * (1.0 - sb)).astype(dglu_ref.dtype)


def _attn_gate_fwd_body(attn_ref, z_ref, h_ref, ht_ref):
    _store_with_transpose(_f32(attn_ref) * _silu_and_grad(_f32(z_ref))[0], h_ref, ht_ref)


def _attn_gate_bwd_body(dh_ref, attn_ref, z_ref, dproj_in, dattn_ref, dz_ref):
    del dproj_in
    silu, dsilu = _silu_and_grad(_f32(z_ref))
    dh = _f32(dh_ref)
    dattn_ref[...] = (dh * silu).astype(dattn_ref.dtype)
    dz_ref[...] = (dh * _f32(attn_ref) * dsilu).astype(dz_ref.dtype)


def _merge_fwd_body(bs_ref, ba_ref, gl_ref, m_ref, mt_ref):
    gs = _sigmoid(_f32(gl_ref, slice(None), slice(0, D_MODEL)))
    ga = _sigmoid(_f32(gl_ref, slice(None), slice(D_MODEL, None)))
    _store_with_transpose(gs * _f32(bs_ref) + ga * _f32(ba_ref), m_ref, mt_ref)


def _merge_bwd_body(dm_ref, bs_ref, ba_ref, gl_ref, dbs_ref, dba_ref, dgl_ref):
    gs = _sigmoid(_f32(gl_ref, slice(None), slice(0, D_MODEL)))
    ga = _sigmoid(_f32(gl_ref, slice(None), slice(D_MODEL, None)))
    dm = _f32(dm_ref)
    dbs_ref[...] = (dm * gs).astype(dbs_ref.dtype)
    dba_ref[...] = (dm * ga).astype(dba_ref.dtype)
    dgl_ref[:, :D_MODEL] = (dm * _f32(bs_ref) * gs * (1.0 - gs)).astype(dgl_ref.dtype)
    dgl_ref[:, D_MODEL:] = (dm * _f32(ba_ref) * ga * (1.0 - ga)).astype(dgl_ref.dtype)


def _ln_loss_body(x_ref, o_ref, tgt_ref, gain_ref, bias_ref, dr_ref, drb_ref, loss_ref, dgain_ref, dbias_ref):
    @pl.when(pl.program_id(0) == 0)
    def _():
        loss_ref[...] = jnp.zeros_like(loss_ref)
        dgain_ref[...] = jnp.zeros_like(dgain_ref)
        dbias_ref[...] = jnp.zeros_like(dbias_ref)

    r = ALPHA * x_ref[...] + o_ref[...]
    mu = jnp.mean(r, axis=1, keepdims=True)
    rc = r - mu
    var = jnp.mean(rc * rc, axis=1, keepdims=True)
    rstd = lax.rsqrt(var + LN_EPS)
    xhat = rc * rstd
    gain = gain_ref[...]
    err = xhat * gain + bias_ref[...] - tgt_ref[...]
    loss_ref[...] += 0.5 * jnp.sum(jnp.mean(err * err, axis=1, keepdims=True), axis=0, keepdims=True)
    dy = err * (1.0 / D_MODEL)
    dgain_ref[...] += jnp.sum(dy * xhat, axis=0, keepdims=True)
    dbias_ref[...] += jnp.sum(dy, axis=0, keepdims=True)
    dxhat = dy * gain
    m1 = jnp.mean(dxhat, axis=1, keepdims=True)
    m2 = jnp.mean(dxhat * xhat, axis=1, keepdims=True)
    dr = rstd * (dxhat - m1 - xhat * m2)
    dr_ref[...] = dr
    drb_ref[...] = dr.astype(drb_ref.dtype)


def _place_body(piece_ref, dproj_in, o_ref):
    del dproj_in
    o_ref[...] = piece_ref[...]


def _adamw_update(w_ref, m_ref, v_ref, g, g_ref, d_ref, nm_ref, nv_ref):
    m = ADAM_B1 * m_ref[...] + (1.0 - ADAM_B1) * g
    v = ADAM_B2 * v_ref[...] + (1.0 - ADAM_B2) * (g * g)
    m_hat = m / (1.0 - ADAM_B1 ** ADAM_STEP)
    v_hat = v / (1.0 - ADAM_B2 ** ADAM_STEP)
    g_ref[...] = g
    d_ref[...] = -ADAM_LR * (m_hat / (jnp.sqrt(v_hat) + ADAM_EPS) + ADAM_WD * w_ref[...])
    nm_ref[...] = m
    nv_ref[...] = v


def _adamw_body(*refs, n_parts):
    w_ref, m_ref, v_ref = refs[:3]
    parts = refs[3:3 + n_parts]
    g = parts[0][...].astype(F32)
    for p in parts[1:]:
        g = g + p[...].astype(F32)
    _adamw_update(w_ref, m_ref, v_ref, g, *refs[3 + n_parts:])


def _adamw_shard_body(c_ref, w_ref, m_ref, v_ref, mine_ref, other_ref, g_ref, d_ref, nm_ref, nv_ref, *, nth):
    in_mine = pl.program_id(0) // nth == c_ref[0]
    g = jnp.where(in_mine, mine_ref[...], other_ref[...])
    _adamw_update(w_ref, m_ref, v_ref, g, g_ref, d_ref, nm_ref, nv_ref)


def _adamw_shard(w, m, v, halves, core, *, name, rows):
    shape = w.shape
    w2, m2, v2 = (a.reshape(-1, shape[-1]) for a in (w, m, v))
    r, c = w2.shape
    nth = r // 2 // rows
    assert 2 * nth * rows == r and halves[0].shape == (r // 2, c)
    spec = pl.BlockSpec((rows, c), lambda i, cr: (i, 0))
    half_spec = pl.BlockSpec((rows, c), lambda i, cr: (i % nth, 0))
    outs = pl.pallas_call(
        functools.partial(_adamw_shard_body, nth=nth), name=name,
        grid_spec=pltpu.PrefetchScalarGridSpec(
            num_scalar_prefetch=1, grid=(r // rows,),
            in_specs=[spec] * 3 + [half_spec] * 2, out_specs=[spec] * 4),
        out_shape=[jax.ShapeDtypeStruct((r, c), F32)] * 4,
        compiler_params=_params(("parallel",)),
    )(core, w2, m2, v2, *halves)
    return tuple(o.reshape(shape) for o in outs)


def _adamw(w, m, v, parts, *, name, rows=256):
    shape = w.shape
    w2, m2, v2 = (a.reshape(-1, shape[-1]) for a in (w, m, v))
    parts = [p.reshape(w2.shape) for p in parts]
    r, c = w2.shape
    rows = rows if r % rows == 0 else r
    spec = pl.BlockSpec((rows, c), lambda i: (i, 0))
    outs = pl.pallas_call(
        functools.partial(_adamw_body, n_parts=len(parts)), name=name, grid=(r // rows,),
        in_specs=[spec] * (3 + len(parts)), out_specs=[spec] * 4,
        out_shape=[jax.ShapeDtypeStruct((r, c), F32)] * 4,
        compiler_params=_params(("parallel",)),
    )(w2, m2, v2, *parts)
    return tuple(o.reshape(shape) for o in outs)


BIG = ("w_in", "w_glu", "w_branch_ssm", "w_branch_attn", "w_out")
SHARD_AXIS = dict(w_in=1, w_glu=1, w_branch_ssm=1, w_branch_attn=1, w_out=0)
HBM = pl.BlockSpec(memory_space=pl.ANY)


def _position():
    x, y, c = lax.axis_index("x"), lax.axis_index("y"), lax.axis_index("c")
    other_chips = [(1 - x, y), (x, 1 - y), (1 - x, 1 - y)]
    return x, y, c, other_chips


def _window(ref, axis, shard, n_shards, half=None):
    rows, cols = ref.shape[-2:]
    sel = [slice(None), slice(None)]
    size = ref.shape[-2 + axis] // n_shards
    sel[axis] = pl.ds(pl.multiple_of(shard * size, 128), size)
    if half is not None:
        hsize = ref.shape[-1 - axis] // 2
        sel[1 - axis] = pl.ds(pl.multiple_of(half * hsize, 128), hsize)
    return ref.at[tuple(sel)]


def _half(ref, axis, half):
    hsize = ref.shape[-1 - axis] // 2
    sel = [slice(None), slice(None)]
    sel[1 - axis] = pl.ds(pl.multiple_of(half * hsize, 128), hsize)
    return ref.at[tuple(sel)]


def _remote(src, dst, send_sem, recv_sem, device):
    return pltpu.make_async_remote_copy(src_ref=src, dst_ref=dst, send_sem=send_sem, recv_sem=recv_sem,
                                        device_id=device, device_id_type=MESH)


def _full_shapes(shards, names):
    out = []
    for k in names:
        s = list(shards[k].shape)
        s[SHARD_AXIS[k]] *= N_CHIPS
        out.append(jax.ShapeDtypeStruct(tuple(s), shards[k].dtype))
    return out


def _quarter(ref, axis, shard, half, quarter):
    sel = [slice(None), slice(None)]
    size = ref.shape[-2 + axis] // N_CHIPS
    sel[axis] = pl.ds(pl.multiple_of(shard * size, 128), size)
    qsize = ref.shape[-1 - axis] // 4
    sel[1 - axis] = pl.ds(pl.multiple_of((2 * half + quarter) * qsize, 128), qsize)
    return ref.at[tuple(sel)]


def _balanced_gather_body(shard, out, send_sems, recv_sems, fsend_sems, frecv_sems, local_sem, *, axis):
    x, y, c, _ = _position()
    me = 2 * x + y
    xn, yn, dg = (1 - x, y), (x, 1 - y), (1 - x, 1 - y)
    chip = lambda p: 2 * p[0] + p[1]
    own = pltpu.make_async_copy(shard, _window(out, axis, me, N_CHIPS), local_sem.at[0])
    own.start()
    direct = [_remote(_half(shard, axis, c), _window(out, axis, me, N_CHIPS, c), send_sems.at[i], recv_sems.at[i],
                      (*p, c)) for i, p in enumerate((xn, yn))]
    for cp in direct:
        cp.start()
    passed = []
    for i, (src, dst) in enumerate(((yn, xn), (xn, yn))):
        landed = _window(out, axis, chip(src), N_CHIPS, c)
        _remote(landed, landed, send_sems.at[1 - i], recv_sems.at[1 - i], (*src, c)).wait_recv()
        piece = _quarter(out, axis, chip(src), c, i)
        passed.append(_remote(piece, piece, fsend_sems.at[i], frecv_sems.at[i], (*dst, c)))
        passed[-1].start()
    for i, src in enumerate((xn, yn)):
        piece = _quarter(out, axis, chip(dg), c, i)
        _remote(piece, piece, fsend_sems.at[i], frecv_sems.at[i], (*src, c)).wait_recv()
    for cp in direct + passed:
        cp.wait_send()
    own.wait()


def _balanced_gather(shard, name):
    axis = SHARD_AXIS[name]
    return pl.pallas_call(
        functools.partial(_balanced_gather_body, axis=axis), name="gather_" + name,
        in_specs=[HBM], out_specs=HBM, out_shape=_full_shapes({name: shard}, (name,))[0],
        scratch_shapes=[pltpu.SemaphoreType.DMA((2,))] * 4 + [pltpu.SemaphoreType.DMA((1,))],
    )(shard)


def _gather_exchange(shards, names):
    axes = tuple(SHARD_AXIS[k] for k in names)
    n = len(names)

    def copies(ins, outs, sems):
        send_sems, recv_sems, local_sems = sems
        x, y, c, chips = _position()
        me = 2 * x + y
        remote, local = [], []
        for w, ax in enumerate(axes):
            for r, (px, py) in enumerate(chips):
                remote.append(_remote(_half(ins[w], ax, c), _window(outs[w], ax, me, N_CHIPS, c),
                                      send_sems.at[3 * w + r], recv_sems.at[3 * w + r], (px, py, c)))
            local.append(pltpu.make_async_copy(ins[w], _window(outs[w], ax, me, N_CHIPS), local_sems.at[w]))
        return remote, local

    return _Exchange([shards[k] for k in names], _full_shapes(shards, names), (3 * n, 3 * n, n), copies)


def _pass_on_body(*refs, axes):
    n = len(axes)
    ins, outs = refs[:n], refs[n:2 * n]
    sbuf, rbuf = refs[2 * n:5 * n], refs[5 * n:8 * n]
    send_sems, recv_sems, load_sems, store_sems = refs[8 * n:]
    x, y, c, chips = _position()
    region = lambda ref, w, r, half: _window(ref, axes[w], 2 * chips[r][0] + chips[r][1], N_CHIPS, half)
    pairs = [(w, r) for w in range(n) for r in range(3)]
    loads = [pltpu.make_async_copy(region(ins[w], w, r, c), sbuf[3 * w + r], load_sems.at[3 * w + r]) for w, r in pairs]
    for cp in loads:
        cp.start()
    sends = []
    for i, cp in enumerate(loads):
        cp.wait()
        sends.append(_remote(sbuf[i], rbuf[i], send_sems.at[i], recv_sems.at[i], (x, y, 1 - c)))
        sends[-1].start()
    stores = []
    for i, (w, r) in enumerate(pairs):
        sends[i].wait_recv()
        stores.append(pltpu.make_async_copy(rbuf[i], region(outs[w], w, r, 1 - c), store_sems.at[i]))
        stores[-1].start()
    for cp in sends:
        cp.wait_send()
    for cp in stores:
        cp.wait()


def _pass_on(fulls, names):
    axes = tuple(SHARD_AXIS[k] for k in names)
    n = len(names)
    bufs = []
    for a, ax in zip(fulls, axes):
        s = list(a.shape)
        s[ax] //= N_CHIPS
        s[1 - ax] //= 2
        bufs += [pltpu.VMEM(tuple(s), a.dtype)] * 3
    return pl.pallas_call(
        functools.partial(_pass_on_body, axes=axes), name="pass_on_" + names[0],
        in_specs=[HBM] * n, out_specs=[HBM] * n,
        out_shape=[jax.ShapeDtypeStruct(a.shape, a.dtype) for a in fulls],
        scratch_shapes=bufs + bufs + [pltpu.SemaphoreType.DMA((3 * n,))] * 4,
        input_output_aliases={i: i for i in range(n)},
        compiler_params=pltpu.CompilerParams(vmem_limit_bytes=VMEM_LIMIT),
    )(*fulls)


def _chip_exchange(pairs, names):
    axes = tuple(SHARD_AXIS[k] for k in names)
    n = len(names)
    out_shape = []
    for p, ax in zip(pairs, axes):
        s = list(p.shape)
        s[ax] //= N_CHIPS
        out_shape.append(jax.ShapeDtypeStruct((4, *s), p.dtype))

    def copies(ins, outs, sems):
        send_sems, recv_sems, local_sems = sems
        x, y, c, chips = _position()
        me = 2 * x + y
        remote, local = [], []
        for w, ax in enumerate(axes):
            for r, (px, py) in enumerate(chips):
                remote.append(_remote(_window(ins[w], ax, 2 * px + py, N_CHIPS), outs[w].at[r],
                                      send_sems.at[3 * w + r], recv_sems.at[3 * w + r], (px, py, c)))
            local.append(pltpu.make_async_copy(_window(ins[w], ax, me, N_CHIPS), outs[w].at[3], local_sems.at[w]))
        return remote, local

    return _Exchange(pairs, out_shape, (3 * n, 3 * n, n), copies)


def _half_tile(n, h, nt, axis):
    return h * nt + n if axis == 1 else 2 * n + h


def _sibling_stream(n, nt, stage, recv, send_sems, recv_sems, credit, produce, consume):
    x, y, c, _ = _position()
    sibling = (x, y, 1 - c)

    def copy(slot):
        return _remote(stage.at[slot], recv.at[slot], send_sems.at[slot], recv_sems.at[slot], sibling)

    @pl.when(n < nt)
    def _():
        slot = n % 2

        @pl.when(n >= 2)
        def _():
            copy(slot).wait_send()
            pl.semaphore_wait(credit, 1)

        stage[slot] = produce().astype(stage.dtype)
        copy(slot).start()

    @pl.when(n >= 1)
    def _():
        slot = (n - 1) % 2
        copy(slot).wait_recv()
        consume(recv[slot])

        @pl.when(n - 1 < nt - 2)
        def _():
            pl.semaphore_signal(credit, 1, device_id=sibling, device_id_type=MESH)

    @pl.when(n == nt)
    def _():
        for slot in range(min(2, nt)):
            copy(slot).wait_send()


def _pair_reduce_body(c_ref, mine_ref, theirs_ref, out_ref, stage, recv, send_sems, recv_sems, credit, *, nt):
    del c_ref

    def consume(got):
        out_ref[...] = (mine_ref[...] + got.astype(F32)).astype(out_ref.dtype)

    _sibling_stream(pl.program_id(0), nt, stage, recv, send_sems, recv_sems, credit,
                    lambda: theirs_ref[...], consume)


def _pair_reduce(grad, core, axis, *, name, rows):
    r, c = grad.shape
    nt = r // 2 // rows
    assert nt * rows * 2 == r and (axis == 1 or rows == r // (2 * N_CHIPS))
    tile = lambda n, h: _half_tile(n, h, nt, axis)
    return pl.pallas_call(
        functools.partial(_pair_reduce_body, nt=nt), name=name,
        grid_spec=pltpu.PrefetchScalarGridSpec(
            num_scalar_prefetch=1, grid=(nt + 1,),
            in_specs=[pl.BlockSpec((rows, c), lambda n, cr: (tile(jnp.maximum(n - 1, 0), cr[0]), 0)),
                      pl.BlockSpec((rows, c), lambda n, cr: (tile(jnp.minimum(n, nt - 1), 1 - cr[0]), 0))],
            out_specs=pl.BlockSpec((rows, c), lambda n, cr: (jnp.maximum(n - 1, 0), 0)),
            scratch_shapes=[pltpu.VMEM((2, rows, c), BF16), pltpu.VMEM((2, rows, c), BF16),
                            pltpu.SemaphoreType.DMA((2,)), pltpu.SemaphoreType.DMA((2,)),
                            pltpu.SemaphoreType.REGULAR]),
        out_shape=jax.ShapeDtypeStruct((r // 2, c), BF16),
        compiler_params=_params(("arbitrary",)),
    )(core, grad, grad)


def _chip_add_share_body(c_ref, s0, s1, s2, s3, mine_ref, other_ref, stage, recv, send_sems, recv_sems, credit, *, nt):
    del c_ref

    def produce():
        total = s3[...].astype(F32) + s0[...].astype(F32) + s1[...].astype(F32) + s2[...].astype(F32)
        mine_ref[...] = total
        return total

    def consume(got):
        other_ref[...] = got

    _sibling_stream(pl.program_id(0), nt, stage, recv, send_sems, recv_sems, credit, produce, consume)


def _chip_add_share(slots, core, *, name, rows):
    _, r, c = slots.shape
    nt = r // rows
    assert nt * rows == r
    send = lambda j: pl.BlockSpec((None, rows, c), lambda n, cr: (j, jnp.minimum(n, nt - 1), 0))
    return pl.pallas_call(
        functools.partial(_chip_add_share_body, nt=nt), name=name,
        grid_spec=pltpu.PrefetchScalarGridSpec(
            num_scalar_prefetch=1, grid=(nt + 1,),
            in_specs=[send(j) for j in range(4)],
            out_specs=[pl.BlockSpec((rows, c), lambda n, cr: (jnp.minimum(n, nt - 1), 0)),
                       pl.BlockSpec((rows, c), lambda n, cr: (jnp.maximum(n - 1, 0), 0))],
            scratch_shapes=[pltpu.VMEM((2, rows, c), F32), pltpu.VMEM((2, rows, c), F32),
                            pltpu.SemaphoreType.DMA((2,)), pltpu.SemaphoreType.DMA((2,)),
                            pltpu.SemaphoreType.REGULAR]),
        out_shape=[jax.ShapeDtypeStruct((r, c), F32)] * 2,
        compiler_params=_params(("arbitrary",)),
    )(core, slots, slots, slots, slots)


PAIR_ROWS = dict(w_in=64, w_glu=128, w_branch_ssm=128, w_branch_attn=128, w_out=256)
SHARE_ROWS = dict(w_in=128, w_glu=128, w_branch_ssm=128, w_branch_attn=128, w_out=64)


def _pair_sums(grads, names, core):
    return [_pair_reduce(grads[k], core, SHARD_AXIS[k], name="pair_reduce_" + k, rows=PAIR_ROWS[k]) for k in names]


def _shard_halves(slots, names, core):
    return {k: _chip_add_share(s, core, name="chip_add_share_" + k, rows=SHARE_ROWS[k]) for k, s in zip(names, slots)}


SMALL = (("ssm_lambda_re", (1, 64, 64)), ("ssm_lambda_im", (1, 64, 64)), ("ssm_b_re", (1, 64, 64, 16)),
         ("ssm_b_im", (1, 64, 64, 16)), ("ssm_c_re", (1, 64, 16, 64)), ("ssm_c_im", (1, 64, 16, 64)),
         ("ssm_d", (1, 1024)), ("ssm_log_step", (1, 64)), ("attn_sinks", (1, 16)), ("rel_bias_table", (32, 16)),
         ("ln_gain", (1, 2048)), ("ln_bias", (1, 2048)))
SMALL_SIZE = sum(int(np.prod(s)) for _, s in SMALL)
PACK_ROWS = -(-(SMALL_SIZE + 1) // (8 * 128)) * 8


def _pack(values, extra=None):
    flat = [values[k].reshape(-1).astype(F32) for k, _ in SMALL]
    flat.append(jnp.zeros((1,), F32) if extra is None else extra.reshape(1))
    flat.append(jnp.zeros((PACK_ROWS * 128 - SMALL_SIZE - 1,), F32))
    return jnp.concatenate(flat).reshape(PACK_ROWS, 128)


def _unpack(packed):
    flat = packed.reshape(-1)
    out, off = {}, 0
    for k, s in SMALL:
        size = int(np.prod(s))
        out[k] = flat[off:off + size].reshape(s)
        off += size
    return out, flat[off]


def _small_exchange(packed):
    def copies(ins, outs, sems):
        send_sems, recv_sems, local_sems = sems
        x, y, c, _ = _position()
        remote = []
        for r in range(1, 8):
            peer = tuple(1 - v if (r >> s) & 1 else v for v, s in ((x, 2), (y, 1), (c, 0)))
            remote.append(_remote(ins[0], outs[0].at[r], send_sems.at[r - 1], recv_sems.at[r - 1], peer))
        return remote, [pltpu.make_async_copy(ins[0], outs[0].at[0], local_sems.at[0])]

    return _Exchange([packed], [jax.ShapeDtypeStruct((8, *packed.shape), F32)], (7, 7, 1), copies)


def _merge_exchanges(a, b):
    na, ma, sa = len(a.ins), len(a.out_shape), len(a.n_sems)

    def copies(ins, outs, sems):
        ra, la = a.copies(ins[:na], outs[:ma], sems[:sa])
        rb, lb = b.copies(ins[na:], outs[ma:], sems[sa:])
        return ra + rb, la + lb

    return _Exchange(a.ins + b.ins, a.out_shape + b.out_shape, a.n_sems + b.n_sems, copies)


def _small_sum_body(slots_ref, o_ref):
    x, y, c, _ = _position()
    me = 4 * x + 2 * y + c
    acc = slots_ref[me]
    for s in range(1, 8):
        acc = acc + slots_ref[jnp.bitwise_xor(me, s)]
    o_ref[...] = acc


def _small_sum(slots):
    vm = pl.BlockSpec(memory_space=pltpu.VMEM)
    return pl.pallas_call(
        functools.partial(_small_sum_body), name="small_sum",
        in_specs=[vm], out_specs=vm, out_shape=jax.ShapeDtypeStruct(slots.shape[1:], F32),
        compiler_params=pltpu.CompilerParams(vmem_limit_bytes=VMEM_LIMIT),
    )(slots)


LATE = BIG[1:]


def _local_step(x, target, w_in, late_shards, core, lam_re, lam_im, b_re, b_im, c_re, c_im, d_skip,
                log_step, sinks, rel_bias_table, ln_gain, ln_bias):
    t = x.shape[0]
    bucket = jnp.asarray(_bucket_table())
    p_re, p_im, b_bd, c_bd = _ssm_prepare(lam_re, lam_im, b_re, b_im, c_re, c_im, log_step)
    bias = _bias_build(rel_bias_table, bucket)

    xb, xbt = _cast_and_transpose(x, name="cast_x")
    act = dict(out_dtype=BF16)
    proj, landed = _matmul(xb, w_in, "nn", name="proj", tm=2048, tn=512,
                           exchange=_gather_exchange(late_shards, LATE), **act)
    w_glu, w_bs, w_ba, w_out = _pass_on(list(landed), LATE)
    y_ssm, g_in, g_in_t, car_re, car_im = _ssm_forward(proj, p_re, p_im, b_bd, c_bd, d_skip)
    glu = _matmul(g_in, w_glu, "nn", name="glu", tm=1024, tn=2048, **act)
    h_ssm, h_ssm_t = _rowwise(functools.partial(_ssm_gate_fwd_body), "ssm_gate_fwd", t,
                              [(glu, 2 * D_SSM, 0), (proj, D_SSM, C_ZS)],
                              [((t, D_SSM), BF16, D_SSM, 0), ((D_SSM, t), BF16, D_SSM, "T")])
    attn, lse = _attn_forward(proj, sinks, bias)
    h_attn, h_attn_t = _rowwise(functools.partial(_attn_gate_fwd_body), "attn_gate_fwd", t,
                                [(attn, D_ATTN, 0), (proj, D_ATTN, C_ZA)],
                                [((t, D_ATTN), BF16, D_ATTN, 0), ((D_ATTN, t), BF16, D_ATTN, "T")])
    bs = _matmul(h_ssm, w_bs, "nn", name="branch_ssm", tm=1024, tn=2048, **act)
    ba = _matmul(h_attn, w_ba, "nn", name="branch_attn", tm=1024, tn=2048, **act)
    gl_in = (proj, 2 * D_MODEL, C_GL)
    merged, merged_t = _rowwise(functools.partial(_merge_fwd_body), "merge_fwd", t,
                                [(bs, D_MODEL, 0), (ba, D_MODEL, 0), gl_in],
                                [((t, D_MODEL), BF16, D_MODEL, 0), ((D_MODEL, t), BF16, D_MODEL, "T")])
    out = _matmul(merged, w_out, "nn", name="out_proj", tm=1024, tn=1024)
    dr, drb, loss, g_gain, g_bias = _rowwise(
        functools.partial(_ln_loss_body), "ln_loss", t,
        [(x, D_MODEL, 0), (out, D_MODEL, 0), (target, D_MODEL, 0), (ln_gain, None, 0), (ln_bias, None, 0)],
        [((t, D_MODEL), F32, D_MODEL, 0), ((t, D_MODEL), BF16, D_MODEL, 0), ((1, 1), F32, None, 0),
         ((1, D_MODEL), F32, None, 0), ((1, D_MODEL), F32, None, 0)])

    g_w_out = _matmul(merged_t, drb, "nn", name="g_w_out", tm=512, tn=512)
    d_merged = _matmul(drb, w_out, "nt", name="d_merged", tm=1024, tn=1024, **act)
    d_bs, d_ba, dproj = _rowwise(
        functools.partial(_merge_bwd_body), "merge_bwd", t,
        [(d_merged, D_MODEL, 0), (bs, D_MODEL, 0), (ba, D_MODEL, 0), gl_in],
        [((t, D_MODEL), BF16, D_MODEL, 0), ((t, D_MODEL), BF16, D_MODEL, 0),
         ((t, D_IN), BF16, 2 * D_MODEL, C_GL)])
    g_w_bs = _matmul(h_ssm_t, d_bs, "nn", name="g_w_branch_ssm", tm=512, tn=512)
    d_h_ssm = _matmul(d_bs, w_bs, "nt", name="d_h_ssm", tm=1024, tn=1024, **act)
    g_w_ba = _matmul(h_attn_t, d_ba, "nn", name="g_w_branch_attn", tm=512, tn=512)
    d_h_attn = _matmul(d_ba, w_ba, "nt", name="d_h_attn", tm=1024, tn=1024, **act)

    d_attn, dproj = _rowwise(
        functools.partial(_attn_gate_bwd_body), "attn_gate_bwd", t,
        [(d_h_attn, D_ATTN, 0), (attn, D_ATTN, 0), (proj, D_ATTN, C_ZA), (dproj, None, "any")],
        [((t, D_ATTN), BF16, D_ATTN, 0), ((t, D_IN), BF16, D_ATTN, C_ZA)], aliases={3: 1})
    dproj, d_kv, g_bias_full, g_sinks = _attn_backward(proj, sinks, bias, d_attn, attn, lse, dproj)
    (dproj,) = _rowwise(functools.partial(_place_body), "place_dkv", t,
                        [(d_kv, 2 * D_KV, 0), (dproj, None, "any")],
                        [((t, D_IN), BF16, 2 * D_KV, C_K)], aliases={1: 0})
    g_table = _bias_grad(g_bias_full, bucket)

    d_glu, dproj = _rowwise(
        functools.partial(_ssm_gate_bwd_body), "ssm_gate_bwd", t,
        [(d_h_ssm, D_SSM, 0), (glu, 2 * D_SSM, 0), (proj, D_SSM, C_ZS), (dproj, None, "any")],
        [((t, 2 * D_SSM), BF16, 2 * D_SSM, 0), ((t, D_IN), BF16, D_SSM, C_ZS)], aliases={3: 1})
    g_w_glu = _matmul(g_in_t, d_glu, "nn", name="g_w_glu", tm=512, tn=512)
    d_g_in = _matmul(d_glu, w_glu, "nt", name="d_g_in", tm=1024, tn=1024, **act)
    dproj, g_bbd, g_cbd, g_lam_re, g_lam_im, g_d = _ssm_backward(
        proj, d_g_in, y_ssm, car_re, car_im, p_re, p_im, b_bd, c_bd, d_skip, dproj)
    g_lr, g_li, g_br, g_bi, g_cr, g_ci, g_ls = _ssm_param_grads(
        lam_re, lam_im, b_re, b_im, log_step, g_lam_re, g_lam_im, g_bbd, g_cbd)

    late = dict(w_glu=g_w_glu, w_branch_ssm=g_w_bs, w_branch_attn=g_w_ba, w_out=g_w_out)
    g_w_in, late_slots = _matmul(xbt, dproj, "nn", name="g_w_in", tm=512, tn=512,
                                 exchange=_chip_exchange(_pair_sums(late, LATE, core), LATE))
    small = dict(ssm_lambda_re=g_lr, ssm_lambda_im=g_li, ssm_b_re=g_br, ssm_b_im=g_bi, ssm_c_re=g_cr,
                 ssm_c_im=g_ci, ssm_d=g_d, ssm_log_step=g_ls, attn_sinks=g_sinks[:, :N_Q_HEADS],
                 rel_bias_table=g_table, ln_gain=g_gain, ln_bias=g_bias)
    last = _merge_exchanges(_chip_exchange(_pair_sums(dict(w_in=g_w_in), BIG[:1], core), BIG[:1]),
                            _small_exchange(_pack(small, loss)))
    grad_x, (in_slots, small_slots) = _matmul(dproj, w_in, "nt", name="grad_x", tm=512, tn=512, res=dr,
                                              res_scale=ALPHA, exchange=last)
    big = {**_shard_halves([in_slots], BIG[:1], core), **_shard_halves(late_slots, LATE, core)}
    return grad_x, big, _small_sum(small_slots)


WEIGHTS = ("w_in", "ssm_lambda_re", "ssm_lambda_im", "ssm_b_re", "ssm_b_im", "ssm_c_re", "ssm_c_im", "ssm_d",
           "ssm_log_step", "w_glu", "attn_sinks", "rel_bias_table", "w_branch_ssm", "w_branch_attn", "w_out",
           "ln_gain", "ln_bias")


def kernel(x, w_in, ssm_lambda_re, ssm_lambda_im, ssm_b_re, ssm_b_im, ssm_c_re, ssm_c_im, ssm_d, ssm_log_step, w_glu, attn_sinks, rel_bias_table, w_branch_ssm, w_branch_attn, w_out, ln_gain, ln_bias, loss_target, m_w_in, m_ssm_lambda_re, m_ssm_lambda_im, m_ssm_b_re, m_ssm_b_im, m_ssm_c_re, m_ssm_c_im, m_ssm_d, m_ssm_log_step, m_w_glu, m_attn_sinks, m_rel_bias_table, m_w_branch_ssm, m_w_branch_attn, m_w_out, m_ln_gain, m_ln_bias, v_w_in, v_ssm_lambda_re, v_ssm_lambda_im, v_ssm_b_re, v_ssm_b_im, v_ssm_c_re, v_ssm_c_im, v_ssm_d, v_ssm_log_step, v_w_glu, v_attn_sinks, v_rel_bias_table, v_w_branch_ssm, v_w_branch_attn, v_w_out, v_ln_gain, v_ln_bias):
    given = dict(locals())
    w = {k: given[k] for k in WEIGHTS}
    m = {k: given["m_" + k] for k in WEIGHTS}
    v = {k: given["v_" + k] for k in WEIGHTS}

    core = lax.axis_index("c").astype(jnp.int32).reshape(1)
    shards = {k: _cast(w[k][0], BF16, name="cast_" + k) for k in BIG}
    (w_in_full,) = _pass_on([_balanced_gather(shards["w_in"], "w_in")], BIG[:1])
    grad_x, g_shard, g_packed = _local_step(
        x[0], loss_target[0], w_in_full, shards, core, ssm_lambda_re[0], ssm_lambda_im[0], ssm_b_re[0],
        ssm_b_im[0], ssm_c_re[0], ssm_c_im[0], ssm_d, ssm_log_step, attn_sinks, rel_bias_table, ln_gain, ln_bias)
    loss_sum = _unpack(g_packed)[1]

    grad, delta, new_m, new_v = {}, {}, {}, {}
    for k in BIG:
        grad[k], delta[k], new_m[k], new_v[k] = _adamw_shard(w[k], m[k], v[k], g_shard[k], core,
                                                             name="adamw_" + k, rows=SHARE_ROWS[k])
    gs, ds, ms, vs = _adamw(_pack(w), _pack(m), _pack(v), [g_packed], name="adamw_small")
    for dst, packed in ((grad, gs), (delta, ds), (new_m, ms), (new_v, vs)):
        dst.update(_unpack(packed)[0])

    return (loss_sum, grad_x[None], *[grad[k] for k in WEIGHTS], *[delta[k] for k in WEIGHTS],
            *[new_m[k] for k in WEIGHTS], *[new_v[k] for k in WEIGHTS])
```

```python
import functools
import math

import numpy as np
import jax
import jax.numpy as jnp
from jax import lax
from jax.experimental import pallas as pl
from jax.experimental.pallas import tpu as pltpu

F32 = jnp.float32
BF16 = jnp.bfloat16

D_MODEL = 2048
D_SSM = 1024
SSM_GROUP = 16
N_GROUPS = 64
SSM_STATE = 64
N_LANES = N_GROUPS * SSM_STATE
N_Q_HEADS = 16
N_KV_HEADS = 4
HEAD_DIM = 64
Q_PER_KV = 4
D_ATTN = 1024
D_KV = 256
WINDOW = 128
BLOCK = 128
N_BUCKETS = 32
MAX_DISTANCE = 128
D_IN = 8704
ALPHA = 2.0 ** 0.25
LN_EPS = 1e-5
NEG_INF = -1e30
ATTN_SCALE = HEAD_DIM ** -0.5

C_U, C_ZS, C_Q, C_K, C_V, C_ZA, C_GL = 0, 1024, 2048, 3072, 3328, 3584, 4608

ADAM_LR = 0.001
ADAM_B1 = 0.9
ADAM_B2 = 0.999
ADAM_EPS = 1e-08
ADAM_WD = 0.01
ADAM_STEP = 10

N_CHIPS = 4
MESH = pl.DeviceIdType.MESH

SSM_CHUNK = 256
SEG_LEN = SSM_CHUNK // 8
SLAB_LANES = 512
N_SLABS = N_LANES // SLAB_LANES
SLAB_CH = D_SSM // N_SLABS

VMEM_LIMIT = 60 * 1024 * 1024


def _params(sem=None, **kw):
    return pltpu.CompilerParams(dimension_semantics=sem, vmem_limit_bytes=VMEM_LIMIT, **kw)


_DIMS = {"nn": (((1,), (0,)), ((), ())), "nt": (((1,), (1,)), ((), ())), "tn": (((0,), (0,)), ((), ()))}


class _Exchange:
    def __init__(self, ins, out_shape, n_sems, copies):
        self.ins, self.out_shape, self.n_sems, self.copies = list(ins), list(out_shape), list(n_sems), copies

    def start(self, ins, outs, sems):
        remote, local = self.copies(ins, outs, sems)
        for cp in remote + local:
            cp.start()

    def finish(self, ins, outs, sems):
        remote, local = self.copies(ins, outs, sems)
        for cp in remote:
            cp.wait_recv()
        for cp in remote:
            cp.wait_send()
        for cp in local:
            cp.wait()


def _mm_body(*refs, dims, nk, res_scale, exchange, grid):
    n_ex_in = len(exchange.ins) if exchange else 0
    n_ex_out = len(exchange.out_shape) if exchange else 0
    n_in = 2 + (res_scale is not None) + n_ex_in
    a_ref, b_ref = refs[0], refs[1]
    r_ref = refs[2] if res_scale is not None else None
    ex_in = refs[n_in - n_ex_in:n_in]
    o_ref, ex_out = refs[n_in], refs[n_in + 1:n_in + 1 + n_ex_out]
    acc_ref, ex_sems = refs[n_in + 1 + n_ex_out], refs[n_in + 2 + n_ex_out:]
    i, j, k = pl.program_id(0), pl.program_id(1), pl.program_id(2)

    if exchange:
        @pl.when(jnp.logical_and(jnp.logical_and(i == 0, j == 0), k == 0))
        def _():
            exchange.start(ex_in, ex_out, ex_sems)

    part = lax.dot_general(a_ref[...].astype(BF16), b_ref[...].astype(BF16), _DIMS[dims],
                           preferred_element_type=F32)

    def finish(acc):
        if r_ref is not None:
            acc = acc + res_scale * r_ref[...].astype(F32)
        o_ref[...] = acc.astype(o_ref.dtype)

    if nk == 1:
        finish(part)
    else:
        @pl.when(k == 0)
        def _():
            acc_ref[...] = part

        @pl.when(k > 0)
        def _():
            acc_ref[...] += part

        @pl.when(k == nk - 1)
        def _():
            finish(acc_ref[...])

    if exchange:
        @pl.when(jnp.logical_and(jnp.logical_and(i == grid[0] - 1, j == grid[1] - 1), k == grid[2] - 1))
        def _():
            exchange.finish(ex_in, ex_out, ex_sems)


def _matmul(a, b, dims, *, name, out_dtype=F32, tm=512, tn=512, tk=None, res=None, res_scale=None, exchange=None):
    if dims == "nn":
        (m, kk), n = a.shape, b.shape[1]
    elif dims == "nt":
        (m, kk), n = a.shape, b.shape[0]
    else:
        (kk, m), n = a.shape, b.shape[1]
    tm, tn = min(tm, m), min(tn, n)
    tk = kk if tk is None else min(tk, kk)
    assert m % tm == 0 and n % tn == 0 and kk % tk == 0, (name, m, n, kk, tm, tn, tk)
    nk = kk // tk
    a_spec = {"nn": pl.BlockSpec((tm, tk), lambda i, j, k: (i, k)),
              "nt": pl.BlockSpec((tm, tk), lambda i, j, k: (i, k)),
              "tn": pl.BlockSpec((tk, tm), lambda i, j, k: (k, i))}[dims]
    b_spec = {"nn": pl.BlockSpec((tk, tn), lambda i, j, k: (k, j)),
              "nt": pl.BlockSpec((tn, tk), lambda i, j, k: (j, k)),
              "tn": pl.BlockSpec((tk, tn), lambda i, j, k: (k, j))}[dims]
    in_specs, args = [a_spec, b_spec], [a, b]
    if res is not None:
        in_specs.append(pl.BlockSpec((tm, tn), lambda i, j, k: (i, j)))
        args.append(res)
    grid = (m // tm, n // tn, nk)
    out_specs = [pl.BlockSpec((tm, tn), lambda i, j, k: (i, j))]
    out_shape = [jax.ShapeDtypeStruct((m, n), out_dtype)]
    scratch = [pltpu.VMEM((tm, tn), F32)]
    if exchange:
        in_specs += [pl.BlockSpec(memory_space=pl.ANY)] * len(exchange.ins)
        args += exchange.ins
        out_specs += [pl.BlockSpec(memory_space=pl.ANY)] * len(exchange.out_shape)
        out_shape += exchange.out_shape
        scratch += [pltpu.SemaphoreType.DMA((s,)) for s in exchange.n_sems]
    outs = pl.pallas_call(
        functools.partial(_mm_body, dims=dims, nk=nk, res_scale=res_scale if res is not None else None,
                          exchange=exchange, grid=grid),
        name=name, grid=grid, in_specs=in_specs, out_specs=out_specs, out_shape=out_shape, scratch_shapes=scratch,
        compiler_params=_params(("arbitrary",) * 3 if exchange else ("parallel", "parallel", "arbitrary")),
    )(*args)
    return (outs[0], outs[1:]) if exchange else outs[0]


def _proj_shards_body(chip_ref, *refs, n_ex_in, has_base, exchange, grid):
    del chip_ref
    refs = list(refs)
    if has_base:
        del refs[2 + n_ex_in]
    _mm_body(*refs, dims="nn", nk=1, res_scale=None, exchange=exchange, grid=grid)


def _proj_shards(xb, w, chip, offsets, *, name, base=None, exchange=None, tm=1024):
    t, d = xb.shape
    cs = D_IN // N_CHIPS
    own = w.shape[1] == cs
    tm = min(tm, t)
    col = lambda j, cr: (cr[0] + offsets[0] + j) % N_CHIPS
    grid = (len(offsets), t // tm, 1)
    in_specs = [pl.BlockSpec((tm, d), lambda j, i, k, cr: (i, 0)),
                pl.BlockSpec((d, cs), lambda j, i, k, cr: (0, 0 if own else col(j, cr)))]
    args = [xb, w]
    out_specs = [pl.BlockSpec((tm, cs), lambda j, i, k, cr: (i, col(j, cr)))]
    out_shape = [jax.ShapeDtypeStruct((t, D_IN), BF16)]
    scratch = [pltpu.VMEM((8, 128), F32)]
    n_ex_in = 0
    if exchange:
        n_ex_in = len(exchange.ins)
        in_specs += [pl.BlockSpec(memory_space=pl.ANY)] * n_ex_in
        args += exchange.ins
        out_specs += [pl.BlockSpec(memory_space=pl.ANY)] * len(exchange.out_shape)
        out_shape += exchange.out_shape
        scratch += [pltpu.SemaphoreType.DMA((s,)) for s in exchange.n_sems]
    aliases = {}
    if base is not None:
        in_specs.append(pl.BlockSpec(memory_space=pl.ANY))
        args.append(base)
        aliases = {len(args): 0}
    outs = pl.pallas_call(
        functools.partial(_proj_shards_body, n_ex_in=n_ex_in, has_base=base is not None, exchange=exchange, grid=grid),
        name=name,
        grid_spec=pltpu.PrefetchScalarGridSpec(num_scalar_prefetch=1, grid=grid, in_specs=in_specs,
                                               out_specs=out_specs, scratch_shapes=scratch),
        out_shape=out_shape, input_output_aliases=aliases,
        compiler_params=_params(("arbitrary",) * 3),
    )(chip, *args)
    return (outs[0], outs[1:]) if exchange else outs[0]


def _sigmoid(v):
    return 1.0 / (1.0 + jnp.exp(-v))


def _silu_and_grad(z):
    s = _sigmoid(z)
    return z * s, s * (1.0 + z * (1.0 - s))


def _cast_body(x_ref, o_ref):
    o_ref[...] = x_ref[...].astype(o_ref.dtype)


def _cast_and_transpose_body(x_ref, o_ref, ot_ref):
    xb = x_ref[...].astype(BF16)
    o_ref[...] = xb
    ot_ref[...] = xb.T


def _cast_and_transpose(x, *, name, rows=512):
    m, n = x.shape
    rows = min(rows, m)
    return pl.pallas_call(
        functools.partial(_cast_and_transpose_body), name=name, grid=(m // rows,),
        in_specs=[pl.BlockSpec((rows, n), lambda i: (i, 0))],
        out_specs=[pl.BlockSpec((rows, n), lambda i: (i, 0)), pl.BlockSpec((n, rows), lambda i: (0, i))],
        out_shape=[jax.ShapeDtypeStruct((m, n), BF16), jax.ShapeDtypeStruct((n, m), BF16)],
        compiler_params=_params(("parallel",)),
    )(x)


def _cast(x, dtype, *, name, rows=512):
    m, n = x.shape
    rows = min(rows, m)
    return pl.pallas_call(
        functools.partial(_cast_body), name=name, grid=(m // rows,),
        in_specs=[pl.BlockSpec((rows, n), lambda i: (i, 0))],
        out_specs=pl.BlockSpec((rows, n), lambda i: (i, 0)),
        out_shape=jax.ShapeDtypeStruct((m, n), dtype),
        compiler_params=_params(("parallel",)),
    )(x)


def _lam_bar(lr, li, ls):
    step = jnp.exp(ls)
    er = jnp.exp(lr * step)
    return step, er * jnp.cos(li * step), er * jnp.sin(li * step)


def _ssm_pow_body(lr_ref, li_ref, ls_ref, pr_ref, pi_ref):
    _, ar, ai = _lam_bar(lr_ref[...], li_ref[...], ls_ref[...])
    cr, ci = ar, ai
    for i in range(SEG_LEN):
        pr_ref[pl.ds(i, 1), :] = cr
        pi_ref[pl.ds(i, 1), :] = ci
        cr, ci = cr * ar - ci * ai, cr * ai + ci * ar


def _ssm_bbar_body(lr_ref, li_ref, ls_ref, br_ref, bi_ref, or_ref, oi_ref):
    lr, li = lr_ref[...], li_ref[...]
    _, ar, ai = _lam_bar(lr, li, ls_ref[...])
    d = lr * lr + li * li
    ir, ii = lr / d, -li / d
    nr, ni = ar - 1.0, ai
    cr, ci = nr * ir - ni * ii, nr * ii + ni * ir
    br, bi = br_ref[...], bi_ref[...]
    or_ref[...] = cr * br - ci * bi
    oi_ref[...] = cr * bi + ci * br


def _ssm_param_bwd_body(lr_ref, li_ref, ls_ref, br_ref, bi_ref, glr_ref, gli_ref, gbr_ref, gbi_ref,
                        dlr_ref, dli_ref, dls_ref, dbr_ref, dbi_ref):
    lr, li = lr_ref[...], li_ref[...]
    step, ar, ai = _lam_bar(lr, li, ls_ref[...])
    d = lr * lr + li * li
    ir, ii = lr / d, -li / d
    nr, ni = ar - 1.0, ai
    cr, ci = nr * ir - ni * ii, nr * ii + ni * ir
    br, bi, gbr, gbi = br_ref[...], bi_ref[...], gbr_ref[...], gbi_ref[...]
    dbr_ref[...] = cr * gbr + ci * gbi
    dbi_ref[...] = cr * gbi - ci * gbr
    gcr = jnp.sum(br * gbr + bi * gbi, axis=1, keepdims=True)
    gci = jnp.sum(br * gbi - bi * gbr, axis=1, keepdims=True)
    gnr, gni = ir * gcr + ii * gci, ir * gci - ii * gcr
    gir, gii = nr * gcr + ni * gci, nr * gci - ni * gcr
    gtr, gti = glr_ref[...] + gnr, gli_ref[...] + gni
    i2r, i2i = ir * ir - ii * ii, 2.0 * ir * ii
    g1r, g1i = -(i2r * gir + i2i * gii), -(i2r * gii - i2i * gir)
    g2r, g2i = step * (ar * gtr + ai * gti), step * (ar * gti - ai * gtr)
    mr, mi = lr * ar - li * ai, lr * ai + li * ar
    dlr_ref[...] = g1r + g2r
    dli_ref[...] = g1i + g2i
    dls_ref[...] = (mr * gtr + mi * gti) * step


def _whole(shape):
    return pl.BlockSpec(shape, lambda *_: (0,) * len(shape))


def _ssm_prepare(lam_re, lam_im, b_re, b_im, c_re, c_im, log_step):
    row = lambda a: a.reshape(1, N_LANES)
    col = lambda a: a.reshape(N_LANES, 1)
    ls = jnp.repeat(log_step.reshape(N_GROUPS), SSM_STATE)
    p_re, p_im = pl.pallas_call(
        functools.partial(_ssm_pow_body), name="ssm_pow",
        in_specs=[_whole((1, N_LANES))] * 3, out_specs=[_whole((SEG_LEN, N_LANES))] * 2,
        out_shape=[jax.ShapeDtypeStruct((SEG_LEN, N_LANES), F32)] * 2, grid=(1,),
    )(row(lam_re), row(lam_im), row(ls))
    bb_re, bb_im = pl.pallas_call(
        functools.partial(_ssm_bbar_body), name="ssm_bbar",
        in_specs=[_whole((N_LANES, 1))] * 3 + [_whole((N_LANES, SSM_GROUP))] * 2,
        out_specs=[_whole((N_LANES, SSM_GROUP))] * 2,
        out_shape=[jax.ShapeDtypeStruct((N_LANES, SSM_GROUP), F32)] * 2, grid=(1,),
    )(col(lam_re), col(lam_im), col(ls), b_re.reshape(N_LANES, SSM_GROUP), b_im.reshape(N_LANES, SSM_GROUP))
    eye = jnp.eye(8, dtype=F32)

    def b_slabs(bb):
        t = bb.reshape(N_SLABS, 8, SSM_STATE, SSM_GROUP).transpose(0, 1, 3, 2)
        return (t[:, :, :, None, :] * eye[None, :, None, :, None]).reshape(N_SLABS, SLAB_CH, SLAB_LANES)

    def c_slabs(c):
        t = c.reshape(N_SLABS, 8, SSM_GROUP, SSM_STATE).transpose(0, 1, 3, 2)
        return (t[:, :, :, None, :] * eye[None, :, None, :, None]).reshape(N_SLABS, SLAB_LANES, SLAB_CH)

    b_bd = jnp.concatenate([b_slabs(bb_re), b_slabs(bb_im)], axis=2).astype(BF16)
    c_bd = jnp.concatenate([c_slabs(c_re.reshape(N_GROUPS, SSM_GROUP, SSM_STATE)),
                            -c_slabs(c_im.reshape(N_GROUPS, SSM_GROUP, SSM_STATE))], axis=1).astype(BF16)
    return p_re, p_im, b_bd, c_bd


def _diag_blocks_b(g):
    t = g.reshape(N_SLABS, 8, SSM_GROUP, 8, SSM_STATE)
    t = jnp.stack([t[:, i, :, i, :] for i in range(8)], axis=1)
    return t.transpose(0, 1, 3, 2).reshape(N_LANES, SSM_GROUP)


def _diag_blocks_c(g):
    t = g.reshape(N_SLABS, 8, SSM_STATE, 8, SSM_GROUP)
    t = jnp.stack([t[:, i, :, i, :] for i in range(8)], axis=1)
    return t.transpose(0, 1, 3, 2).reshape(N_GROUPS, SSM_GROUP, SSM_STATE)


def _ssm_param_grads(lam_re, lam_im, b_re, b_im, log_step, g_lam_re, g_lam_im, g_bbd, g_cbd):
    col = lambda a: a.reshape(N_LANES, 1)
    ls = jnp.repeat(log_step.reshape(N_GROUPS), SSM_STATE)
    gbr = _diag_blocks_b(g_bbd[:, :, :SLAB_LANES])
    gbi = _diag_blocks_b(g_bbd[:, :, SLAB_LANES:])
    outs = pl.pallas_call(
        functools.partial(_ssm_param_bwd_body), name="ssm_param_bwd", grid=(1,),
        in_specs=[_whole((N_LANES, 1))] * 3 + [_whole((N_LANES, SSM_GROUP))] * 2 + [_whole((N_LANES, 1))] * 2
        + [_whole((N_LANES, SSM_GROUP))] * 2,
        out_specs=[_whole((N_LANES, 1))] * 3 + [_whole((N_LANES, SSM_GROUP))] * 2,
        out_shape=[jax.ShapeDtypeStruct((N_LANES, 1), F32)] * 3 + [jax.ShapeDtypeStruct((N_LANES, SSM_GROUP), F32)] * 2,
    )(col(lam_re), col(lam_im), col(ls), b_re.reshape(N_LANES, SSM_GROUP), b_im.reshape(N_LANES, SSM_GROUP),
      col(g_lam_re), col(g_lam_im), gbr, gbi)
    dlr, dli, dls, dbr, dbi = outs
    d_c_re = _diag_blocks_c(g_cbd[:, :SLAB_LANES, :])
    d_c_im = -_diag_blocks_c(g_cbd[:, SLAB_LANES:, :])
    return (dlr.reshape(1, N_GROUPS, SSM_STATE), dli.reshape(1, N_GROUPS, SSM_STATE),
            dbr.reshape(1, N_GROUPS, SSM_STATE, SSM_GROUP), dbi.reshape(1, N_GROUPS, SSM_STATE, SSM_GROUP),
            d_c_re[None], d_c_im[None], dls.reshape(N_GROUPS, SSM_STATE).sum(axis=1).reshape(1, N_GROUPS))


def _bcast8(v):
    return jnp.broadcast_to(v, (8, v.shape[1]))


def _segment_permutation():
    p = np.zeros((SSM_CHUNK, SSM_CHUNK), np.float32)
    rows = np.arange(SSM_CHUNK)
    p[rows, (rows % 8) * SEG_LEN + rows // 8] = 1.0
    return p


def _permute_exact(perm, val, pieces):
    out, rest = None, val
    for n in range(pieces):
        part = rest.astype(BF16)
        moved = jnp.dot(perm, part, preferred_element_type=F32)
        out = moved if out is None else out + moved
        if n + 1 < pieces:
            rest = rest - part.astype(F32)
    return out


def _scan_buffer():
    return pltpu.VMEM((SSM_CHUNK, N_LANES), F32)


def _lanes(k):
    return pl.ds(k * SLAB_LANES, SLAB_LANES)


def _tile(i):
    return pl.ds(i * 8 if isinstance(i, int) else pl.multiple_of(i * 8, 8), 8)


SCAN_LANES = 1024


def _scan_lanes(k):
    return pl.ds(k * SCAN_LANES, SCAN_LANES)


def _seg_get(ref, k, i):
    return ref[_tile(i), _scan_lanes(k)]


def _seg_put(ref, k, i, val):
    ref[_tile(i), _scan_lanes(k)] = val


def _slab_get(ref, k):
    return ref[:, _lanes(k)]


def _slab_put(ref, k, val):
    ref[:, _lanes(k)] = val


def _scan_forward(s_re, s_im, p_re, p_im, car_re, car_im, sp_re=None, sp_im=None):
    for k in range(N_LANES // SCAN_LANES):
        ln = _scan_lanes(k)
        ar, ai =_bcast8(p_re[pl.ds(0, 1), ln]), _bcast8(p_im[pl.ds(0, 1), ln])

        def step(i, s, k=k, ar=ar, ai=ai):
            sr, si = s
            nr = ar * sr - ai * si + _seg_get(s_re, k, i)
            ni = ar * si + ai * sr + _seg_get(s_im, k, i)
            _seg_put(s_re, k, i, nr)
            _seg_put(s_im, k, i, ni)
            return nr, ni

        zero = jnp.zeros((8, SCAN_LANES), F32)
        er, ei = lax.fori_loop(0, SEG_LEN, step, (zero, zero), unroll=4)
        lr, li = p_re[pl.ds(SEG_LEN - 1, 1), ln], p_im[pl.ds(SEG_LEN - 1, 1), ln]
        cr, ci = car_re[pl.ds(0, 1), ln], car_im[pl.ds(0, 1), ln]
        rows_r, rows_i = [], []
        for r in range(8):
            rows_r.append(cr)
            rows_i.append(ci)
            cr, ci = er[r:r + 1] + lr * cr - li * ci, ei[r:r + 1] + lr * ci + li * cr
        pr8, pi8 = jnp.concatenate(rows_r, axis=0), jnp.concatenate(rows_i, axis=0)
        car_re[:, ln] = _bcast8(cr)
        car_im[:, ln] = _bcast8(ci)
        if sp_re is not None:
            sp_re[:, ln] = pr8
            sp_im[:, ln] = pi8

        def fix(i, _, k=k, ln=ln, pr8=pr8, pi8=pi8):
            qr, qi = _bcast8(p_re[pl.ds(i, 1), ln]), _bcast8(p_im[pl.ds(i, 1), ln])
            _seg_put(s_re, k, i, _seg_get(s_re, k, i) + qr * pr8 - qi * pi8)
            _seg_put(s_im, k, i, _seg_get(s_im, k, i) + qr * pi8 + qi * pr8)
            return 0

        lax.fori_loop(0, SEG_LEN, fix, 0, unroll=4)


def _scan_backward(g_re, g_im, s_re, s_im, sp_re, sp_im, p_re, p_im, car_re, car_im, acc_re, acc_im):
    for k in range(N_LANES // SCAN_LANES):
        ln = _scan_lanes(k)
        ar, ai =_bcast8(p_re[pl.ds(0, 1), ln]), -_bcast8(p_im[pl.ds(0, 1), ln])

        def step(j, s, k=k, ar=ar, ai=ai):
            i = SEG_LEN - 1 - j
            sr, si = s
            nr = ar * sr - ai * si + _seg_get(g_re, k, i)
            ni = ar * si + ai * sr + _seg_get(g_im, k, i)
            _seg_put(g_re, k, i, nr)
            _seg_put(g_im, k, i, ni)
            return nr, ni

        zero = jnp.zeros((8, SCAN_LANES), F32)
        er, ei = lax.fori_loop(0, SEG_LEN, step, (zero, zero), unroll=4)
        lr, li = p_re[pl.ds(SEG_LEN - 1, 1), ln], -p_im[pl.ds(SEG_LEN - 1, 1), ln]
        cr, ci = car_re[pl.ds(0, 1), ln], car_im[pl.ds(0, 1), ln]
        rows_r, rows_i = [None] * 8, [None] * 8
        for r in range(7, -1, -1):
            rows_r[r], rows_i[r] = cr, ci
            cr, ci = er[r:r + 1] + lr * cr - li * ci, ei[r:r + 1] + lr * ci + li * cr
        nr8, ni8 = jnp.concatenate(rows_r, axis=0), jnp.concatenate(rows_i, axis=0)
        car_re[:, ln] = _bcast8(cr)
        car_im[:, ln] = _bcast8(ci)

        def fix(i, acc, k=k, ln=ln, nr8=nr8, ni8=ni8):
            qr = _bcast8(p_re[pl.ds(SEG_LEN - 1 - i, 1), ln])
            qi = -_bcast8(p_im[pl.ds(SEG_LEN - 1 - i, 1), ln])
            gr = _seg_get(g_re, k, i) + qr * nr8 - qi * ni8
            gi = _seg_get(g_im, k, i) + qr * ni8 + qi * nr8
            _seg_put(g_re, k, i, gr)
            _seg_put(g_im, k, i, gi)
            return gr, gi

        def prod(xr, xi, gr, gi):
            return xr * gr + xi * gi, xr * gi - xi * gr

        gr, gi = fix(0, None)
        a_r, a_i = prod(sp_re[:, ln], sp_im[:, ln], gr, gi)

        def fix_acc(i, acc, k=k, fix=fix):
            gr, gi = fix(i, None)
            dr, di = prod(_seg_get(s_re, k, i - 1), _seg_get(s_im, k, i - 1), gr, gi)
            return acc[0] + dr, acc[1] + di

        acc = (a_r, a_i)
        for i in range(1, 4):
            acc = fix_acc(i, acc)
        a_r, a_i = lax.fori_loop(4, SEG_LEN, fix_acc, acc, unroll=4)
        acc_re[:, ln] += a_r
        acc_im[:, ln] += a_i


def _gelu_and_grad(y):
    cdf = 0.5 * (1.0 + lax.erf(y * (2.0 ** -0.5)))
    pdf = jnp.exp(-0.5 * y * y) * (1.0 / math.sqrt(2.0 * math.pi))
    return y * cdf, cdf + y * pdf


def _ssm_fwd_body(u_ref, bbd_ref, cbd_ref, pre_ref, pim_ref, d_ref, perm_ref, unperm_ref,
                  y_ref, gin_ref, gint_ref, cre_out, cim_out, s_re, s_im, car_re, car_im, yp):
    c = pl.program_id(0)

    @pl.when(c == 0)
    def _():
        car_re[...] = jnp.zeros_like(car_re)
        car_im[...] = jnp.zeros_like(car_im)

    cre_out[...] = car_re[...]
    cim_out[...] = car_im[...]
    u = u_ref[...].astype(F32)
    up = jnp.dot(perm_ref[...], u_ref[...].astype(BF16), preferred_element_type=F32).astype(BF16)
    for k in range(N_SLABS):
        bu = jnp.dot(up[:, k * SLAB_CH:(k + 1) * SLAB_CH], bbd_ref[k], preferred_element_type=F32)
        _slab_put(s_re, k, bu[:, :SLAB_LANES])
        _slab_put(s_im, k, bu[:, SLAB_LANES:])
    _scan_forward(s_re, s_im, pre_ref, pim_ref, car_re, car_im)
    for k in range(N_SLABS):
        yp[:, pl.ds(k * SLAB_CH, SLAB_CH)] = (
            jnp.dot(_slab_get(s_re, k).astype(BF16), cbd_ref[k, :SLAB_LANES, :], preferred_element_type=F32)
            + jnp.dot(_slab_get(s_im, k).astype(BF16), cbd_ref[k, SLAB_LANES:, :], preferred_element_type=F32))
    y = _permute_exact(unperm_ref[...], yp[...], 2) + d_ref[...] * u
    y_ref[...] = y.astype(y_ref.dtype)
    _store_with_transpose(_gelu_and_grad(y)[0], gin_ref, gint_ref)


def _ssm_forward(proj, p_re, p_im, b_bd, c_bd, d_skip):
    t = proj.shape[0]
    nc = t // SSM_CHUNK
    perm = _segment_permutation()
    return pl.pallas_call(
        functools.partial(_ssm_fwd_body), name="ssm_fwd", grid=(nc,),
        in_specs=[pl.BlockSpec((SSM_CHUNK, D_SSM), lambda c: (c, C_U // D_SSM)),
                  _whole(b_bd.shape), _whole(c_bd.shape), _whole(p_re.shape), _whole(p_im.shape),
                  _whole((1, D_SSM)), _whole(perm.shape), _whole(perm.shape)],
        out_specs=[pl.BlockSpec((SSM_CHUNK, D_SSM), lambda c: (c, 0)),
                   pl.BlockSpec((SSM_CHUNK, D_SSM), lambda c: (c, 0)),
                   pl.BlockSpec((D_SSM, SSM_CHUNK), lambda c: (0, c)),
                   pl.BlockSpec((None, 8, N_LANES), lambda c: (c, 0, 0)),
                   pl.BlockSpec((None, 8, N_LANES), lambda c: (c, 0, 0))],
        out_shape=[jax.ShapeDtypeStruct((t, D_SSM), BF16), jax.ShapeDtypeStruct((t, D_SSM), BF16),
                   jax.ShapeDtypeStruct((D_SSM, t), BF16),
                   jax.ShapeDtypeStruct((nc, 8, N_LANES), F32), jax.ShapeDtypeStruct((nc, 8, N_LANES), F32)],
        scratch_shapes=[_scan_buffer(), _scan_buffer(),
                        pltpu.VMEM((8, N_LANES), F32), pltpu.VMEM((8, N_LANES), F32),
                        pltpu.VMEM((SSM_CHUNK, D_SSM), F32)],
        compiler_params=_params(("arbitrary",)),
    )(proj, b_bd, c_bd, p_re, p_im, d_skip, jnp.asarray(perm, BF16), jnp.asarray(perm.T, BF16))


def _ssm_bwd_body(u_ref, dgin_ref, y_ref, cre_in, cim_in, bbd_ref, cbd_ref, pre_ref, pim_ref, d_ref, perm_ref,
                  unperm_ref, dproj_in,
                  du_ref, gb_ref, gc_ref, glr_ref, gli_ref, gd_ref,
                  s_re, s_im, g_re, g_im, sp_re, sp_im, car_re, car_im, gcar_re, gcar_im, acc_re, acc_im, dup):
    del dproj_in
    c = pl.program_id(0)
    nc = pl.num_programs(0)

    @pl.when(c == 0)
    def _():
        gcar_re[...] = jnp.zeros_like(gcar_re)
        gcar_im[...] = jnp.zeros_like(gcar_im)
        acc_re[...] = jnp.zeros_like(acc_re)
        acc_im[...] = jnp.zeros_like(acc_im)
        gb_ref[...] = jnp.zeros_like(gb_ref)
        gc_ref[...] = jnp.zeros_like(gc_ref)
        gd_ref[...] = jnp.zeros_like(gd_ref)

    car_re[...] = cre_in[...]
    car_im[...] = cim_in[...]
    u = u_ref[...].astype(F32)
    dy = dgin_ref[...].astype(F32) * _gelu_and_grad(y_ref[...].astype(F32))[1]
    gd_ref[...] += jnp.sum(dy * u, axis=0, keepdims=True)
    up = jnp.dot(perm_ref[...], u.astype(BF16), preferred_element_type=F32).astype(BF16)
    dyp = jnp.dot(perm_ref[...], dy.astype(BF16), preferred_element_type=F32).astype(BF16)
    for k in range(N_SLABS):
        ch = slice(k * SLAB_CH, (k + 1) * SLAB_CH)
        bu = jnp.dot(up[:, ch], bbd_ref[k], preferred_element_type=F32)
        _slab_put(s_re, k, bu[:, :SLAB_LANES])
        _slab_put(s_im, k, bu[:, SLAB_LANES:])
        ds = lax.dot_general(dyp[:, ch], cbd_ref[k], _DIMS["nt"], preferred_element_type=F32)
        _slab_put(g_re, k, ds[:, :SLAB_LANES])
        _slab_put(g_im, k, ds[:, SLAB_LANES:])
    _scan_forward(s_re, s_im, pre_ref, pim_ref, car_re, car_im, sp_re, sp_im)
    _scan_backward(g_re, g_im, s_re, s_im, sp_re, sp_im, pre_ref, pim_ref, gcar_re, gcar_im, acc_re, acc_im)
    for k in range(N_SLABS):
        ch = slice(k * SLAB_CH, (k + 1) * SLAB_CH)
        uk, dyk = up[:, ch], dyp[:, ch]
        sr, si = _slab_get(s_re, k).astype(BF16), _slab_get(s_im, k).astype(BF16)
        gr, gi = _slab_get(g_re, k).astype(BF16), _slab_get(g_im, k).astype(BF16)
        gc_ref[k, :SLAB_LANES, :] += lax.dot_general(sr, dyk, _DIMS["tn"], preferred_element_type=F32)
        gc_ref[k, SLAB_LANES:, :] += lax.dot_general(si, dyk, _DIMS["tn"], preferred_element_type=F32)
        gb_ref[k, :, :SLAB_LANES] += lax.dot_general(uk, gr, _DIMS["tn"], preferred_element_type=F32)
        gb_ref[k, :, SLAB_LANES:] += lax.dot_general(uk, gi, _DIMS["tn"], preferred_element_type=F32)
        dup[:, pl.ds(k * SLAB_CH, SLAB_CH)] = (
            lax.dot_general(gr, bbd_ref[k, :, :SLAB_LANES], _DIMS["nt"], preferred_element_type=F32)
            + lax.dot_general(gi, bbd_ref[k, :, SLAB_LANES:], _DIMS["nt"], preferred_element_type=F32))
    du = _permute_exact(unperm_ref[...], dup[...], 2) + d_ref[...] * dy
    du_ref[...] = du.astype(du_ref.dtype)

    @pl.when(c == nc - 1)
    def _():
        glr_ref[...] = jnp.sum(acc_re[...], axis=0, keepdims=True)
        gli_ref[...] = jnp.sum(acc_im[...], axis=0, keepdims=True)


def _ssm_backward(proj, dg_in, y_ssm, car_re, car_im, p_re, p_im, b_bd, c_bd, d_skip, dproj):
    t = proj.shape[0]
    nc = t // SSM_CHUNK
    rev = lambda c: nc - 1 - c
    big = _scan_buffer
    small = lambda: pltpu.VMEM((8, N_LANES), F32)
    perm = _segment_permutation()
    outs = pl.pallas_call(
        functools.partial(_ssm_bwd_body), name="ssm_bwd", grid=(nc,),
        in_specs=[pl.BlockSpec((SSM_CHUNK, D_SSM), lambda c: (rev(c), C_U // D_SSM)),
                  pl.BlockSpec((SSM_CHUNK, D_SSM), lambda c: (rev(c), 0)),
                  pl.BlockSpec((SSM_CHUNK, D_SSM), lambda c: (rev(c), 0)),
                  pl.BlockSpec((None, 8, N_LANES), lambda c: (rev(c), 0, 0)),
                  pl.BlockSpec((None, 8, N_LANES), lambda c: (rev(c), 0, 0)),
                  _whole(b_bd.shape), _whole(c_bd.shape), _whole(p_re.shape), _whole(p_im.shape),
                  _whole((1, D_SSM)), _whole(perm.shape), _whole(perm.shape), pl.BlockSpec(memory_space=pl.ANY)],
        out_specs=[pl.BlockSpec((SSM_CHUNK, D_SSM), lambda c: (rev(c), C_U // D_SSM)),
                   _whole(b_bd.shape), _whole(c_bd.shape), _whole((1, N_LANES)), _whole((1, N_LANES)),
                   _whole((1, D_SSM))],
        out_shape=[jax.ShapeDtypeStruct(dproj.shape, dproj.dtype),
                   jax.ShapeDtypeStruct(b_bd.shape, F32), jax.ShapeDtypeStruct(c_bd.shape, F32),
                   jax.ShapeDtypeStruct((1, N_LANES), F32), jax.ShapeDtypeStruct((1, N_LANES), F32),
                   jax.ShapeDtypeStruct((1, D_SSM), F32)],
        scratch_shapes=[big(), big(), big(), big()] + [small() for _ in range(8)]
        + [pltpu.VMEM((SSM_CHUNK, D_SSM), F32)],
        input_output_aliases={12: 0},
        compiler_params=_params(("arbitrary",)),
    )(proj, dg_in, y_ssm, car_re, car_im, b_bd, c_bd, p_re, p_im, d_skip, jnp.asarray(perm, BF16),
      jnp.asarray(perm.T, BF16), dproj)
    return outs


def _bucket_table():
    i = np.arange(BLOCK)[:, None]
    j = np.arange(2 * BLOCK)[None, :]
    dist = BLOCK + i - j
    ok = (dist >= 0) & (dist < WINDOW)
    max_exact = N_BUCKETS // 2
    d = np.maximum(dist, 1).astype(np.float32)
    large = max_exact + (np.log(d / max_exact) / math.log(MAX_DISTANCE / max_exact)
                         * (N_BUCKETS - max_exact)).astype(np.int32)
    large = np.minimum(large, N_BUCKETS - 1)
    bucket = np.where(dist < max_exact, dist, large)
    return np.where(ok, bucket, -1).astype(np.int32)


def _bias_build_body(table_ref, bucket_ref, o_ref):
    h = pl.program_id(0)
    bucket = bucket_ref[...]
    acc = jnp.full(bucket.shape, NEG_INF, F32)
    for b in range(N_BUCKETS):
        acc = jnp.where(bucket == b, table_ref[b, h], acc)
    o_ref[...] = acc


def _bias_build(rel_bias_table, bucket):
    return pl.pallas_call(
        functools.partial(_bias_build_body), name="bias_build", grid=(N_Q_HEADS,),
        in_specs=[pl.BlockSpec(memory_space=pltpu.SMEM), _whole(bucket.shape)],
        out_specs=pl.BlockSpec((None, BLOCK, 2 * BLOCK), lambda h: (h, 0, 0)),
        out_shape=jax.ShapeDtypeStruct((N_Q_HEADS, BLOCK, 2 * BLOCK), F32),
        compiler_params=_params(("arbitrary",)),
    )(rel_bias_table, bucket)


def _bias_grad_body(g_ref, bucket_ref, o_ref):
    bucket = bucket_ref[...]
    lane = lax.broadcasted_iota(jnp.int32, (N_BUCKETS, 128), 1)

    def head(h, out):
        g = g_ref[h]
        rows = [jnp.sum(jnp.where(bucket == b, g, 0.0), axis=0, keepdims=True) for b in range(N_BUCKETS)]
        colsum = jnp.sum(jnp.concatenate(rows, axis=0), axis=1, keepdims=True)
        return jnp.where(lane == h, colsum, out)

    o_ref[...] = lax.fori_loop(0, N_Q_HEADS, head, jnp.zeros((N_BUCKETS, 128), F32))


def _bias_grad(g_bias, bucket):
    out = pl.pallas_call(
        functools.partial(_bias_grad_body), name="bias_grad", grid=(1,),
        in_specs=[_whole(g_bias.shape), _whole(bucket.shape)],
        out_specs=_whole((N_BUCKETS, 128)),
        out_shape=jax.ShapeDtypeStruct((N_BUCKETS, 128), F32),
        compiler_params=_params(("arbitrary",)),
    )(g_bias, bucket)
    return out[:, :N_Q_HEADS]


def _head_logits(qh, kk, bias_h, first_block):
    s = lax.dot_general(qh, kk, _DIMS["nt"], preferred_element_type=F32) * ATTN_SCALE + bias_h
    col = lax.broadcasted_iota(jnp.int32, s.shape, 1)
    return jnp.where(jnp.logical_and(first_block, col < BLOCK), NEG_INF, s)


def _attn_fwd_body(sink_ref, q_ref, kp_ref, kc_ref, vp_ref, vc_ref, bias_ref, o_ref, lse_ref):
    n = pl.program_id(0)
    outs, lses = [], []
    for kv in range(N_KV_HEADS):
        cs = slice(kv * HEAD_DIM, (kv + 1) * HEAD_DIM)
        kk = jnp.concatenate([kp_ref[:, cs], kc_ref[:, cs]], axis=0).astype(BF16)
        vv = jnp.concatenate([vp_ref[:, cs], vc_ref[:, cs]], axis=0).astype(BF16)
        for g in range(Q_PER_KV):
            h = kv * Q_PER_KV + g
            qh = q_ref[:, h * HEAD_DIM:(h + 1) * HEAD_DIM].astype(BF16)
            s = _head_logits(qh, kk, bias_ref[h], n == 0)
            sink = sink_ref[0, h]
            m = jnp.maximum(jnp.max(s, axis=1, keepdims=True), sink)
            p = jnp.exp(s - m)
            den = jnp.sum(p, axis=1, keepdims=True) + jnp.exp(sink - m)
            p = p / den
            outs.append(jnp.dot(p.astype(BF16), vv, preferred_element_type=F32))
            lses.append(m + jnp.log(den))
    o_ref[...] = jnp.concatenate(outs, axis=1).astype(o_ref.dtype)
    lse_ref[...] = jnp.concatenate(lses, axis=1)


def _attn_specs(nb):
    prev = lambda n: jnp.maximum(jnp.minimum(n, nb - 1) - 1, 0)
    cur = lambda n: jnp.minimum(n, nb - 1)
    return [pl.BlockSpec((BLOCK, D_ATTN), lambda n: (cur(n), C_Q // D_ATTN)),
            pl.BlockSpec((BLOCK, D_KV), lambda n: (prev(n), C_K // D_KV)),
            pl.BlockSpec((BLOCK, D_KV), lambda n: (cur(n), C_K // D_KV)),
            pl.BlockSpec((BLOCK, D_KV), lambda n: (prev(n), C_V // D_KV)),
            pl.BlockSpec((BLOCK, D_KV), lambda n: (cur(n), C_V // D_KV))]


def _attn_forward(proj, sinks, bias):
    t = proj.shape[0]
    nb = t // BLOCK
    return pl.pallas_call(
        functools.partial(_attn_fwd_body), name="attn_fwd", grid=(nb,),
        in_specs=[pl.BlockSpec(memory_space=pltpu.SMEM)] + _attn_specs(nb) + [_whole(bias.shape)],
        out_specs=[pl.BlockSpec((BLOCK, D_ATTN), lambda n: (n, 0)),
                   pl.BlockSpec((BLOCK, N_Q_HEADS), lambda n: (n, 0))],
        out_shape=[jax.ShapeDtypeStruct((t, D_ATTN), BF16), jax.ShapeDtypeStruct((t, N_Q_HEADS), F32)],
        compiler_params=_params(("parallel",)),
    )(sinks, proj, proj, proj, proj, proj, bias)


def _attn_bwd_body(sink_ref, q_ref, kp_ref, kc_ref, vp_ref, vc_ref, bias_ref, do_ref, o_ref, lse_ref, dproj_in,
                   dq_ref, dkv_ref, gbias_ref, gsink_ref, carry_ref, *, nb):
    del dproj_in
    n = pl.program_id(0)

    @pl.when(n == 0)
    def _():
        gbias_ref[...] = jnp.zeros_like(gbias_ref)
        gsink_ref[...] = jnp.zeros_like(gsink_ref)
        carry_ref[...] = jnp.zeros_like(carry_ref)

    @pl.when(n < nb)
    def _():
        lane = lax.broadcasted_iota(jnp.int32, (1, 128), 1)
        dqs, dks, dvs = [], [], []
        gsink = jnp.zeros((1, 128), F32)
        for kv in range(N_KV_HEADS):
            cs = slice(kv * HEAD_DIM, (kv + 1) * HEAD_DIM)
            kk = jnp.concatenate([kp_ref[:, cs], kc_ref[:, cs]], axis=0).astype(BF16)
            vv = jnp.concatenate([vp_ref[:, cs], vc_ref[:, cs]], axis=0).astype(BF16)
            dk = jnp.zeros((2 * BLOCK, HEAD_DIM), F32)
            dv = jnp.zeros((2 * BLOCK, HEAD_DIM), F32)
            for g in range(Q_PER_KV):
                h = kv * Q_PER_KV + g
                hs = slice(h * HEAD_DIM, (h + 1) * HEAD_DIM)
                qh = q_ref[:, hs].astype(BF16)
                s = _head_logits(qh, kk, bias_ref[h], n == 0)
                lse = lse_ref[:, h:h + 1]
                p = jnp.exp(s - lse)
                do = do_ref[:, hs].astype(F32)
                delta = jnp.sum(do * o_ref[:, hs].astype(F32), axis=1, keepdims=True)
                dob = do.astype(BF16)
                dp = lax.dot_general(dob, vv, _DIMS["nt"], preferred_element_type=F32)
                dl = p * (dp - delta)
                gbias_ref[h] += dl
                psink = jnp.exp(sink_ref[0, h] - lse)
                gsink = gsink + jnp.where(lane == h, -jnp.sum(psink * delta), 0.0)
                dlb = dl.astype(BF16)
                dqs.append(jnp.dot(dlb, kk, preferred_element_type=F32) * ATTN_SCALE)
                dk = dk + lax.dot_general(dlb, qh, _DIMS["tn"], preferred_element_type=F32) * ATTN_SCALE
                dv = dv + lax.dot_general(p.astype(BF16), dob, _DIMS["tn"], preferred_element_type=F32)
            dks.append(dk)
            dvs.append(dv)
        dq_ref[...] = jnp.concatenate(dqs, axis=1).astype(dq_ref.dtype)
        gsink_ref[...] += gsink
        dkv = jnp.concatenate(dks + dvs, axis=1)
        dkv_ref[...] = (carry_ref[...] + dkv[:BLOCK]).astype(dkv_ref.dtype)
        carry_ref[...] = dkv[BLOCK:]

    @pl.when(n == nb)
    def _():
        dkv_ref[...] = carry_ref[...].astype(dkv_ref.dtype)


def _attn_backward(proj, sinks, bias, d_attn, attn, lse, dproj):
    t = proj.shape[0]
    nb = t // BLOCK
    cur = lambda n: jnp.minimum(n, nb - 1)
    return pl.pallas_call(
        functools.partial(_attn_bwd_body, nb=nb), name="attn_bwd", grid=(nb + 1,),
        in_specs=[pl.BlockSpec(memory_space=pltpu.SMEM)] + _attn_specs(nb) + [
            _whole(bias.shape),
            pl.BlockSpec((BLOCK, D_ATTN), lambda n: (cur(n), 0)),
            pl.BlockSpec((BLOCK, D_ATTN), lambda n: (cur(n), 0)),
            pl.BlockSpec((BLOCK, N_Q_HEADS), lambda n: (cur(n), 0)),
            pl.BlockSpec(memory_space=pl.ANY)],
        out_specs=[pl.BlockSpec((BLOCK, D_ATTN), lambda n: (cur(n), C_Q // D_ATTN)),
                   pl.BlockSpec((BLOCK, 2 * D_KV), lambda n: (jnp.maximum(n - 1, 0), 0)),
                   _whole(bias.shape), _whole((1, 128))],
        out_shape=[jax.ShapeDtypeStruct(dproj.shape, dproj.dtype), jax.ShapeDtypeStruct((t, 2 * D_KV), dproj.dtype),
                   jax.ShapeDtypeStruct(bias.shape, F32), jax.ShapeDtypeStruct((1, 128), F32)],
        scratch_shapes=[pltpu.VMEM((BLOCK, 2 * D_KV), F32)],
        input_output_aliases={10: 0},
        compiler_params=_params(("arbitrary",)),
    )(sinks, proj, proj, proj, proj, proj, bias, d_attn, attn, lse, dproj)


ROWS = 256


def _rowwise(body, name, t, ins, outs, aliases=None):
    rows = min(ROWS, t)

    def col_spec(w, c0):
        if c0 == "T":
            return pl.BlockSpec((w, rows), lambda i: (0, i))
        if c0 % w == 0:
            return pl.BlockSpec((rows, w), lambda i: (i, c0 // w))
        return pl.BlockSpec((pl.Element(rows), pl.Element(w)), lambda i: (i * rows, c0))

    in_specs, args = [], []
    for a, w, c0 in ins:
        args.append(a)
        if w is None:
            in_specs.append(pl.BlockSpec(memory_space=pl.ANY) if c0 == "any" else _whole(a.shape))
        else:
            in_specs.append(col_spec(w, c0))
    out_specs, out_shape = [], []
    for shape, dtype, w, c0 in outs:
        out_shape.append(jax.ShapeDtypeStruct(shape, dtype))
        out_specs.append(_whole(shape) if w is None else col_spec(w, c0))
    accum = any(o[2] is None for o in outs)
    return pl.pallas_call(
        body, name=name, grid=(t // rows,), in_specs=in_specs, out_specs=out_specs, out_shape=out_shape,
        input_output_aliases=aliases or {},
        compiler_params=_params(("arbitrary",) if accum else ("parallel",)),
    )(*args)


def _f32(ref, *idx):
    return (ref[idx] if idx else ref[...]).astype(F32)


def _store_with_transpose(val, ref, t_ref):
    val = val.astype(ref.dtype)
    ref[...] = val
    t_ref[...] = val.T


def _ssm_gate_fwd_body(glu_ref, z_ref, h_ref, ht_ref):
    a, b = _f32(glu_ref, slice(None), slice(0, D_SSM)), _f32(glu_ref, slice(None), slice(D_SSM, None))
    _store_with_transpose((a * _sigmoid(b)) * _silu_and_grad(_f32(z_ref))[0], h_ref, ht_ref)


def _ssm_gate_bwd_body(dh_ref, glu_ref, z_ref, dproj_in, dglu_ref, dz_ref):
    del dproj_in
    a, b = _f32(glu_ref, slice(None), slice(0, D_SSM)), _f32(glu_ref, slice(None), slice(D_SSM, None))
    sb = _sigmoid(b)
    silu, dsilu = _silu_and_grad(_f32(z_ref))
    dh = _f32(dh_ref)
    dg = dh * silu
    dz_ref[...] = (dh * (a * sb) * dsilu).astype(dz_ref.dtype)
    dglu_ref[:, :D_SSM] = (dg * sb).astype(dglu_ref.dtype)
    dglu_ref[:, D_SSM:] = (dg * a * sb * (1.0 - sb)).astype(dglu_ref.dtype)


def _attn_gate_fwd_body(attn_ref, z_ref, h_ref, ht_ref):
    _store_with_transpose(_f32(attn_ref) * _silu_and_grad(_f32(z_ref))[0], h_ref, ht_ref)


def _attn_gate_bwd_body(dh_ref, attn_ref, z_ref, dproj_in, dattn_ref, dz_ref):
    del dproj_in
    silu, dsilu = _silu_and_grad(_f32(z_ref))
    dh = _f32(dh_ref)
    dattn_ref[...] = (dh * silu).astype(dattn_ref.dtype)
    dz_ref[...] = (dh * _f32(attn_ref) * dsilu).astype(dz_ref.dtype)


def _merge_fwd_body(bs_ref, ba_ref, gl_ref, m_ref, mt_ref):
    gs = _sigmoid(_f32(gl_ref, slice(None), slice(0, D_MODEL)))
    ga = _sigmoid(_f32(gl_ref, slice(None), slice(D_MODEL, None)))
    _store_with_transpose(gs * _f32(bs_ref) + ga * _f32(ba_ref), m_ref, mt_ref)


def _merge_bwd_body(dm_ref, bs_ref, ba_ref, gl_ref, dbs_ref, dba_ref, dgl_ref):
    gs = _sigmoid(_f32(gl_ref, slice(None), slice(0, D_MODEL)))
    ga = _sigmoid(_f32(gl_ref, slice(None), slice(D_MODEL, None)))
    dm = _f32(dm_ref)
    dbs_ref[...] = (dm * gs).astype(dbs_ref.dtype)
    dba_ref[...] = (dm * ga).astype(dba_ref.dtype)
    dgl_ref[:, :D_MODEL] = (dm * _f32(bs_ref) * gs * (1.0 - gs)).astype(dgl_ref.dtype)
    dgl_ref[:, D_MODEL:] = (dm * _f32(ba_ref) * ga * (1.0 - ga)).astype(dgl_ref.dtype)


def _ln_loss_body(x_ref, o_ref, tgt_ref, gain_ref, bias_ref, dr_ref, loss_ref, dgain_ref, dbias_ref):
    @pl.when(pl.program_id(0) == 0)
    def _():
        loss_ref[...] = jnp.zeros_like(loss_ref)
        dgain_ref[...] = jnp.zeros_like(dgain_ref)
        dbias_ref[...] = jnp.zeros_like(dbias_ref)

    r = ALPHA * x_ref[...] + o_ref[...].astype(F32)
    mu = jnp.mean(r, axis=1, keepdims=True)
    rc = r - mu
    var = jnp.mean(rc * rc, axis=1, keepdims=True)
    rstd = lax.rsqrt(var + LN_EPS)
    xhat = rc * rstd
    gain = gain_ref[...]
    err = xhat * gain + bias_ref[...] - tgt_ref[...]
    loss_ref[...] += 0.5 * jnp.sum(jnp.mean(err * err, axis=1, keepdims=True), axis=0, keepdims=True)
    dy = err * (1.0 / D_MODEL)
    dgain_ref[...] += jnp.sum(dy * xhat, axis=0, keepdims=True)
    dbias_ref[...] += jnp.sum(dy, axis=0, keepdims=True)
    dxhat = dy * gain
    m1 = jnp.mean(dxhat, axis=1, keepdims=True)
    m2 = jnp.mean(dxhat * xhat, axis=1, keepdims=True)
    dr_ref[...] = (rstd * (dxhat - m1 - xhat * m2)).astype(dr_ref.dtype)


def _place_body(piece_ref, dproj_in, o_ref):
    del dproj_in
    o_ref[...] = piece_ref[...]


def _adamw_update(w_ref, m_ref, v_ref, g, g_ref, d_ref, nm_ref, nv_ref):
    m = ADAM_B1 * m_ref[...] + (1.0 - ADAM_B1) * g
    v = ADAM_B2 * v_ref[...] + (1.0 - ADAM_B2) * (g * g)
    m_hat = m / (1.0 - ADAM_B1 ** ADAM_STEP)
    v_hat = v / (1.0 - ADAM_B2 ** ADAM_STEP)
    g_ref[...] = g
    d_ref[...] = -ADAM_LR * (m_hat / (jnp.sqrt(v_hat) + ADAM_EPS) + ADAM_WD * w_ref[...])
    nm_ref[...] = m
    nv_ref[...] = v


def _adamw_body(*refs, n_parts):
    w_ref, m_ref, v_ref = refs[:3]
    parts = refs[3:3 + n_parts]
    g = parts[0][...].astype(F32)
    for p in parts[1:]:
        g = g + p[...].astype(F32)
    _adamw_update(w_ref, m_ref, v_ref, g, *refs[3 + n_parts:])


def _adamw_shard_body(c_ref, w_ref, m_ref, v_ref, mine_ref, other_ref, g_ref, d_ref, nm_ref, nv_ref, *, nth):
    in_mine = pl.program_id(0) // nth == c_ref[0]
    g = jnp.where(in_mine, mine_ref[...], other_ref[...])
    _adamw_update(w_ref, m_ref, v_ref, g, g_ref, d_ref, nm_ref, nv_ref)


def _adamw_shard(w, m, v, halves, core, *, name, rows):
    shape = w.shape
    w2, m2, v2 = (a.reshape(-1, shape[-1]) for a in (w, m, v))
    r, c = w2.shape
    nth = r // 2 // rows
    assert 2 * nth * rows == r and halves[0].shape == (r // 2, c)
    spec = pl.BlockSpec((rows, c), lambda i, cr: (i, 0))
    half_spec = pl.BlockSpec((rows, c), lambda i, cr: (i % nth, 0))
    outs = pl.pallas_call(
        functools.partial(_adamw_shard_body, nth=nth), name=name,
        grid_spec=pltpu.PrefetchScalarGridSpec(
            num_scalar_prefetch=1, grid=(r // rows,),
            in_specs=[spec] * 3 + [half_spec] * 2, out_specs=[spec] * 4),
        out_shape=[jax.ShapeDtypeStruct((r, c), F32)] * 4,
        compiler_params=_params(("parallel",)),
    )(core, w2, m2, v2, *halves)
    return tuple(o.reshape(shape) for o in outs)


def _adamw(w, m, v, parts, *, name, rows=256):
    shape = w.shape
    w2, m2, v2 = (a.reshape(-1, shape[-1]) for a in (w, m, v))
    parts = [p.reshape(w2.shape) for p in parts]
    r, c = w2.shape
    rows = rows if r % rows == 0 else r
    spec = pl.BlockSpec((rows, c), lambda i: (i, 0))
    outs = pl.pallas_call(
        functools.partial(_adamw_body, n_parts=len(parts)), name=name, grid=(r // rows,),
        in_specs=[spec] * (3 + len(parts)), out_specs=[spec] * 4,
        out_shape=[jax.ShapeDtypeStruct((r, c), F32)] * 4,
        compiler_params=_params(("parallel",)),
    )(w2, m2, v2, *parts)
    return tuple(o.reshape(shape) for o in outs)


BIG = ("w_in", "w_glu", "w_branch_ssm", "w_branch_attn", "w_out")
SHARD_AXIS = dict(w_in=1, w_glu=1, w_branch_ssm=1, w_branch_attn=1, w_out=0)
HBM = pl.BlockSpec(memory_space=pl.ANY)


def _position():
    x, y, c = lax.axis_index("x"), lax.axis_index("y"), lax.axis_index("c")
    other_chips = [(1 - x, y), (x, 1 - y), (1 - x, 1 - y)]
    return x, y, c, other_chips


def _window(ref, axis, shard, n_shards, half=None):
    rows, cols = ref.shape[-2:]
    sel = [slice(None), slice(None)]
    size = ref.shape[-2 + axis] // n_shards
    sel[axis] = pl.ds(pl.multiple_of(shard * size, 128), size)
    if half is not None:
        hsize = ref.shape[-1 - axis] // 2
        sel[1 - axis] = pl.ds(pl.multiple_of(half * hsize, 128), hsize)
    return ref.at[tuple(sel)]


def _half(ref, axis, half):
    hsize = ref.shape[-1 - axis] // 2
    sel = [slice(None), slice(None)]
    sel[1 - axis] = pl.ds(pl.multiple_of(half * hsize, 128), hsize)
    return ref.at[tuple(sel)]


def _remote(src, dst, send_sem, recv_sem, device):
    return pltpu.make_async_remote_copy(src_ref=src, dst_ref=dst, send_sem=send_sem, recv_sem=recv_sem,
                                        device_id=device, device_id_type=MESH)


def _full_shapes(shards, names):
    out = []
    for k in names:
        s = list(shards[k].shape)
        s[SHARD_AXIS[k]] *= N_CHIPS
        out.append(jax.ShapeDtypeStruct(tuple(s), shards[k].dtype))
    return out


def _gather_exchange(shards, names):
    axes = tuple(SHARD_AXIS[k] for k in names)
    n = len(names)

    def copies(ins, outs, sems):
        send_sems, recv_sems, local_sems = sems
        x, y, c, chips = _position()
        me = 2 * x + y
        remote, local = [], []
        for w, ax in enumerate(axes):
            for r, (px, py) in enumerate(chips):
                remote.append(_remote(_half(ins[w], ax, c), _window(outs[w], ax, me, N_CHIPS, c),
                                      send_sems.at[3 * w + r], recv_sems.at[3 * w + r], (px, py, c)))
            local.append(pltpu.make_async_copy(ins[w], _window(outs[w], ax, me, N_CHIPS), local_sems.at[w]))
        return remote, local

    return _Exchange([shards[k] for k in names], _full_shapes(shards, names), (3 * n, 3 * n, n), copies)


def _pass_on_body(*refs, axes):
    n = len(axes)
    ins, outs = refs[:n], refs[n:2 * n]
    sbuf, rbuf = refs[2 * n:5 * n], refs[5 * n:8 * n]
    send_sems, recv_sems, load_sems, store_sems = refs[8 * n:]
    x, y, c, chips = _position()
    region = lambda ref, w, r, half: _window(ref, axes[w], 2 * chips[r][0] + chips[r][1], N_CHIPS, half)
    pairs = [(w, r) for w in range(n) for r in range(3)]
    loads = [pltpu.make_async_copy(region(ins[w], w, r, c), sbuf[3 * w + r], load_sems.at[3 * w + r]) for w, r in pairs]
    for cp in loads:
        cp.start()
    sends = []
    for i, cp in enumerate(loads):
        cp.wait()
        sends.append(_remote(sbuf[i], rbuf[i], send_sems.at[i], recv_sems.at[i], (x, y, 1 - c)))
        sends[-1].start()
    stores = []
    for i, (w, r) in enumerate(pairs):
        sends[i].wait_recv()
        stores.append(pltpu.make_async_copy(rbuf[i], region(outs[w], w, r, 1 - c), store_sems.at[i]))
        stores[-1].start()
    for cp in sends:
        cp.wait_send()
    for cp in stores:
        cp.wait()


def _pass_on(fulls, names):
    axes = tuple(SHARD_AXIS[k] for k in names)
    n = len(names)
    bufs = []
    for a, ax in zip(fulls, axes):
        s = list(a.shape)
        s[ax] //= N_CHIPS
        s[1 - ax] //= 2
        bufs += [pltpu.VMEM(tuple(s), a.dtype)] * 3
    return pl.pallas_call(
        functools.partial(_pass_on_body, axes=axes), name="pass_on_" + names[0],
        in_specs=[HBM] * n, out_specs=[HBM] * n,
        out_shape=[jax.ShapeDtypeStruct(a.shape, a.dtype) for a in fulls],
        scratch_shapes=bufs + bufs + [pltpu.SemaphoreType.DMA((3 * n,))] * 4,
        input_output_aliases={i: i for i in range(n)},
        compiler_params=pltpu.CompilerParams(vmem_limit_bytes=VMEM_LIMIT),
    )(*fulls)


def _chip_exchange(pairs, names):
    axes = tuple(SHARD_AXIS[k] for k in names)
    n = len(names)
    out_shape = []
    for p, ax in zip(pairs, axes):
        s = list(p.shape)
        s[ax] //= N_CHIPS
        out_shape.append(jax.ShapeDtypeStruct((4, *s), p.dtype))

    def copies(ins, outs, sems):
        send_sems, recv_sems, local_sems = sems
        x, y, c, chips = _position()
        me = 2 * x + y
        remote, local = [], []
        for w, ax in enumerate(axes):
            for r, (px, py) in enumerate(chips):
                remote.append(_remote(_window(ins[w], ax, 2 * px + py, N_CHIPS), outs[w].at[r],
                                      send_sems.at[3 * w + r], recv_sems.at[3 * w + r], (px, py, c)))
            local.append(pltpu.make_async_copy(_window(ins[w], ax, me, N_CHIPS), outs[w].at[3], local_sems.at[w]))
        return remote, local

    return _Exchange(pairs, out_shape, (3 * n, 3 * n, n), copies)


def _half_tile(n, h, nt, axis):
    return h * nt + n if axis == 1 else 2 * n + h


def _sibling_stream(n, nt, stage, recv, send_sems, recv_sems, credit, produce, consume):
    x, y, c, _ = _position()
    sibling = (x, y, 1 - c)

    def copy(slot):
        return _remote(stage.at[slot], recv.at[slot], send_sems.at[slot], recv_sems.at[slot], sibling)

    @pl.when(n < nt)
    def _():
        slot = n % 2

        @pl.when(n >= 2)
        def _():
            copy(slot).wait_send()
            pl.semaphore_wait(credit, 1)

        stage[slot] = produce().astype(stage.dtype)
        copy(slot).start()

    @pl.when(n >= 1)
    def _():
        slot = (n - 1) % 2
        copy(slot).wait_recv()
        consume(recv[slot])

        @pl.when(n - 1 < nt - 2)
        def _():
            pl.semaphore_signal(credit, 1, device_id=sibling, device_id_type=MESH)

    @pl.when(n == nt)
    def _():
        for slot in range(min(2, nt)):
            copy(slot).wait_send()


def _pair_reduce_body(c_ref, mine_ref, theirs_ref, out_ref, stage, recv, send_sems, recv_sems, credit, *, nt):
    del c_ref

    def consume(got):
        out_ref[...] = (mine_ref[...] + got.astype(F32)).astype(out_ref.dtype)

    _sibling_stream(pl.program_id(0), nt, stage, recv, send_sems, recv_sems, credit,
                    lambda: theirs_ref[...], consume)


def _pair_reduce(grad, core, axis, *, name, rows):
    r, c = grad.shape
    nt = r // 2 // rows
    assert nt * rows * 2 == r and (axis == 1 or rows == r // (2 * N_CHIPS))
    tile = lambda n, h: _half_tile(n, h, nt, axis)
    return pl.pallas_call(
        functools.partial(_pair_reduce_body, nt=nt), name=name,
        grid_spec=pltpu.PrefetchScalarGridSpec(
            num_scalar_prefetch=1, grid=(nt + 1,),
            in_specs=[pl.BlockSpec((rows, c), lambda n, cr: (tile(jnp.maximum(n - 1, 0), cr[0]), 0)),
                      pl.BlockSpec((rows, c), lambda n, cr: (tile(jnp.minimum(n, nt - 1), 1 - cr[0]), 0))],
            out_specs=pl.BlockSpec((rows, c), lambda n, cr: (jnp.maximum(n - 1, 0), 0)),
            scratch_shapes=[pltpu.VMEM((2, rows, c), BF16), pltpu.VMEM((2, rows, c), BF16),
                            pltpu.SemaphoreType.DMA((2,)), pltpu.SemaphoreType.DMA((2,)),
                            pltpu.SemaphoreType.REGULAR]),
        out_shape=jax.ShapeDtypeStruct((r // 2, c), BF16),
        compiler_params=_params(("arbitrary",)),
    )(core, grad, grad)


def _chip_add_share_body(c_ref, s0, s1, s2, s3, mine_ref, other_ref, stage, recv, send_sems, recv_sems, credit, *, nt):
    del c_ref

    def produce():
        total = s3[...].astype(F32) + s0[...].astype(F32) + s1[...].astype(F32) + s2[...].astype(F32)
        mine_ref[...] = total
        return total

    def consume(got):
        other_ref[...] = got

    _sibling_stream(pl.program_id(0), nt, stage, recv, send_sems, recv_sems, credit, produce, consume)


def _chip_add_share(slots, core, *, name, rows):
    _, r, c = slots.shape
    nt = r // rows
    assert nt * rows == r
    send = lambda j: pl.BlockSpec((None, rows, c), lambda n, cr: (j, jnp.minimum(n, nt - 1), 0))
    return pl.pallas_call(
        functools.partial(_chip_add_share_body, nt=nt), name=name,
        grid_spec=pltpu.PrefetchScalarGridSpec(
            num_scalar_prefetch=1, grid=(nt + 1,),
            in_specs=[send(j) for j in range(4)],
            out_specs=[pl.BlockSpec((rows, c), lambda n, cr: (jnp.minimum(n, nt - 1), 0)),
                       pl.BlockSpec((rows, c), lambda n, cr: (jnp.maximum(n - 1, 0), 0))],
            scratch_shapes=[pltpu.VMEM((2, rows, c), F32), pltpu.VMEM((2, rows, c), F32),
                            pltpu.SemaphoreType.DMA((2,)), pltpu.SemaphoreType.DMA((2,)),
                            pltpu.SemaphoreType.REGULAR]),
        out_shape=[jax.ShapeDtypeStruct((r, c), F32)] * 2,
        compiler_params=_params(("arbitrary",)),
    )(core, slots, slots, slots, slots)


PAIR_ROWS = dict(w_in=64, w_glu=128, w_branch_ssm=128, w_branch_attn=128, w_out=256)
SHARE_ROWS = dict(w_in=128, w_glu=128, w_branch_ssm=128, w_branch_attn=128, w_out=64)


def _pair_sums(grads, names, core):
    return [_pair_reduce(grads[k], core, SHARD_AXIS[k], name="pair_reduce_" + k, rows=PAIR_ROWS[k]) for k in names]


def _shard_halves(slots, names, core):
    return {k: _chip_add_share(s, core, name="chip_add_share_" + k, rows=SHARE_ROWS[k]) for k, s in zip(names, slots)}


SMALL = (("ssm_lambda_re", (1, 64, 64)), ("ssm_lambda_im", (1, 64, 64)), ("ssm_b_re", (1, 64, 64, 16)),
         ("ssm_b_im", (1, 64, 64, 16)), ("ssm_c_re", (1, 64, 16, 64)), ("ssm_c_im", (1, 64, 16, 64)),
         ("ssm_d", (1, 1024)), ("ssm_log_step", (1, 64)), ("attn_sinks", (1, 16)), ("rel_bias_table", (32, 16)),
         ("ln_gain", (1, 2048)), ("ln_bias", (1, 2048)))
SMALL_SIZE = sum(int(np.prod(s)) for _, s in SMALL)
PACK_ROWS = -(-(SMALL_SIZE + 1) // (8 * 128)) * 8


def _pack(values, extra=None):
    flat = [values[k].reshape(-1).astype(F32) for k, _ in SMALL]
    flat.append(jnp.zeros((1,), F32) if extra is None else extra.reshape(1))
    flat.append(jnp.zeros((PACK_ROWS * 128 - SMALL_SIZE - 1,), F32))
    return jnp.concatenate(flat).reshape(PACK_ROWS, 128)


def _unpack(packed):
    flat = packed.reshape(-1)
    out, off = {}, 0
    for k, s in SMALL:
        size = int(np.prod(s))
        out[k] = flat[off:off + size].reshape(s)
        off += size
    return out, flat[off]


def _small_exchange(packed):
    def copies(ins, outs, sems):
        send_sems, recv_sems, local_sems = sems
        x, y, c, _ = _position()
        remote = []
        for r in range(1, 8):
            peer = tuple(1 - v if (r >> s) & 1 else v for v, s in ((x, 2), (y, 1), (c, 0)))
            remote.append(_remote(ins[0], outs[0].at[r], send_sems.at[r - 1], recv_sems.at[r - 1], peer))
        return remote, [pltpu.make_async_copy(ins[0], outs[0].at[0], local_sems.at[0])]

    return _Exchange([packed], [jax.ShapeDtypeStruct((8, *packed.shape), F32)], (7, 7, 1), copies)


def _merge_exchanges(a, b):
    na, ma, sa = len(a.ins), len(a.out_shape), len(a.n_sems)

    def copies(ins, outs, sems):
        ra, la = a.copies(ins[:na], outs[:ma], sems[:sa])
        rb, lb = b.copies(ins[na:], outs[ma:], sems[sa:])
        return ra + rb, la + lb

    return _Exchange(a.ins + b.ins, a.out_shape + b.out_shape, a.n_sems + b.n_sems, copies)


def _small_sum_body(slots_ref, o_ref):
    x, y, c, _ = _position()
    me = 4 * x + 2 * y + c
    acc = slots_ref[me]
    for s in range(1, 8):
        acc = acc + slots_ref[jnp.bitwise_xor(me, s)]
    o_ref[...] = acc


def _small_sum(slots):
    vm = pl.BlockSpec(memory_space=pltpu.VMEM)
    return pl.pallas_call(
        functools.partial(_small_sum_body), name="small_sum",
        in_specs=[vm], out_specs=vm, out_shape=jax.ShapeDtypeStruct(slots.shape[1:], F32),
        compiler_params=pltpu.CompilerParams(vmem_limit_bytes=VMEM_LIMIT),
    )(slots)


LATE = BIG[1:]


def _local_step(x, target, shards, core, chip, lam_re, lam_im, b_re, b_im, c_re, c_im, d_skip,
                log_step, sinks, rel_bias_table, ln_gain, ln_bias):
    t = x.shape[0]
    bucket = jnp.asarray(_bucket_table())
    p_re, p_im, b_bd, c_bd = _ssm_prepare(lam_re, lam_im, b_re, b_im, c_re, c_im, log_step)
    bias = _bias_build(rel_bias_table, bucket)

    xb, xbt = _cast_and_transpose(x, name="cast_x")
    act = dict(out_dtype=BF16)
    proj, landed = _proj_shards(xb, shards["w_in"], chip, (0,), name="proj_own",
                                exchange=_gather_exchange(shards, BIG[:1]))
    (w_in,) = _pass_on(list(landed), BIG[:1])
    proj, landed = _proj_shards(xb, w_in, chip, (1, 2, 3), name="proj", base=proj,
                                exchange=_gather_exchange(shards, LATE))
    w_glu, w_bs, w_ba, w_out = _pass_on(list(landed), LATE)
    y_ssm, g_in, g_in_t, car_re, car_im = _ssm_forward(proj, p_re, p_im, b_bd, c_bd, d_skip)
    glu = _matmul(g_in, w_glu, "nn", name="glu", tm=1024, tn=2048, **act)
    h_ssm, h_ssm_t = _rowwise(functools.partial(_ssm_gate_fwd_body), "ssm_gate_fwd", t,
                              [(glu, 2 * D_SSM, 0), (proj, D_SSM, C_ZS)],
                              [((t, D_SSM), BF16, D_SSM, 0), ((D_SSM, t), BF16, D_SSM, "T")])
    attn, lse = _attn_forward(proj, sinks, bias)
    h_attn, h_attn_t = _rowwise(functools.partial(_attn_gate_fwd_body), "attn_gate_fwd", t,
                                [(attn, D_ATTN, 0), (proj, D_ATTN, C_ZA)],
                                [((t, D_ATTN), BF16, D_ATTN, 0), ((D_ATTN, t), BF16, D_ATTN, "T")])
    bs = _matmul(h_ssm, w_bs, "nn", name="branch_ssm", tm=1024, tn=2048, **act)
    ba = _matmul(h_attn, w_ba, "nn", name="branch_attn", tm=1024, tn=2048, **act)
    gl_in = (proj, 2 * D_MODEL, C_GL)
    merged, merged_t = _rowwise(functools.partial(_merge_fwd_body), "merge_fwd", t,
                                [(bs, D_MODEL, 0), (ba, D_MODEL, 0), gl_in],
                                [((t, D_MODEL), BF16, D_MODEL, 0), ((D_MODEL, t), BF16, D_MODEL, "T")])
    out = _matmul(merged, w_out, "nn", name="out_proj", tm=1024, tn=1024, **act)
    drb, loss, g_gain, g_bias = _rowwise(
        functools.partial(_ln_loss_body), "ln_loss", t,
        [(x, D_MODEL, 0), (out, D_MODEL, 0), (target, D_MODEL, 0), (ln_gain, None, 0), (ln_bias, None, 0)],
        [((t, D_MODEL), BF16, D_MODEL, 0), ((1, 1), F32, None, 0),
         ((1, D_MODEL), F32, None, 0), ((1, D_MODEL), F32, None, 0)])

    g_w_out = _matmul(merged_t, drb, "nn", name="g_w_out", tm=512, tn=512)
    d_merged = _matmul(drb, w_out, "nt", name="d_merged", tm=1024, tn=1024, **act)
    d_bs, d_ba, dproj = _rowwise(
        functools.partial(_merge_bwd_body), "merge_bwd", t,
        [(d_merged, D_MODEL, 0), (bs, D_MODEL, 0), (ba, D_MODEL, 0), gl_in],
        [((t, D_MODEL), BF16, D_MODEL, 0), ((t, D_MODEL), BF16, D_MODEL, 0),
         ((t, D_IN), BF16, 2 * D_MODEL, C_GL)])
    g_w_bs = _matmul(h_ssm_t, d_bs, "nn", name="g_w_branch_ssm", tm=512, tn=512)
    d_h_ssm = _matmul(d_bs, w_bs, "nt", name="d_h_ssm", tm=1024, tn=1024, **act)
    g_w_ba = _matmul(h_attn_t, d_ba, "nn", name="g_w_branch_attn", tm=512, tn=512)
    d_h_attn = _matmul(d_ba, w_ba, "nt", name="d_h_attn", tm=1024, tn=1024, **act)

    d_attn, dproj = _rowwise(
        functools.partial(_attn_gate_bwd_body), "attn_gate_bwd", t,
        [(d_h_attn, D_ATTN, 0), (attn, D_ATTN, 0), (proj, D_ATTN, C_ZA), (dproj, None, "any")],
        [((t, D_ATTN), BF16, D_ATTN, 0), ((t, D_IN), BF16, D_ATTN, C_ZA)], aliases={3: 1})
    dproj, d_kv, g_bias_full, g_sinks = _attn_backward(proj, sinks, bias, d_attn, attn, lse, dproj)
    (dproj,) = _rowwise(functools.partial(_place_body), "place_dkv", t,
                        [(d_kv, 2 * D_KV, 0), (dproj, None, "any")],
                        [((t, D_IN), BF16, 2 * D_KV, C_K)], aliases={1: 0})
    g_table = _bias_grad(g_bias_full, bucket)

    d_glu, dproj = _rowwise(
        functools.partial(_ssm_gate_bwd_body), "ssm_gate_bwd", t,
        [(d_h_ssm, D_SSM, 0), (glu, 2 * D_SSM, 0), (proj, D_SSM, C_ZS), (dproj, None, "any")],
        [((t, 2 * D_SSM), BF16, 2 * D_SSM, 0), ((t, D_IN), BF16, D_SSM, C_ZS)], aliases={3: 1})
    g_w_glu = _matmul(g_in_t, d_glu, "nn", name="g_w_glu", tm=512, tn=512)
    d_g_in = _matmul(d_glu, w_glu, "nt", name="d_g_in", tm=1024, tn=1024, **act)
    dproj, g_bbd, g_cbd, g_lam_re, g_lam_im, g_d = _ssm_backward(
        proj, d_g_in, y_ssm, car_re, car_im, p_re, p_im, b_bd, c_bd, d_skip, dproj)
    g_lr, g_li, g_br, g_bi, g_cr, g_ci, g_ls = _ssm_param_grads(
        lam_re, lam_im, b_re, b_im, log_step, g_lam_re, g_lam_im, g_bbd, g_cbd)

    late = dict(w_glu=g_w_glu, w_branch_ssm=g_w_bs, w_branch_attn=g_w_ba, w_out=g_w_out)
    g_w_in, late_slots = _matmul(xbt, dproj, "nn", name="g_w_in", tm=512, tn=512,
                                 exchange=_chip_exchange(_pair_sums(late, LATE, core), LATE))
    small = dict(ssm_lambda_re=g_lr, ssm_lambda_im=g_li, ssm_b_re=g_br, ssm_b_im=g_bi, ssm_c_re=g_cr,
                 ssm_c_im=g_ci, ssm_d=g_d, ssm_log_step=g_ls, attn_sinks=g_sinks[:, :N_Q_HEADS],
                 rel_bias_table=g_table, ln_gain=g_gain, ln_bias=g_bias)
    last = _merge_exchanges(_chip_exchange(_pair_sums(dict(w_in=g_w_in), BIG[:1], core), BIG[:1]),
                            _small_exchange(_pack(small, loss)))
    grad_x, (in_slots, small_slots) = _matmul(dproj, w_in, "nt", name="grad_x", tm=512, tn=512, res=drb,
                                              res_scale=ALPHA, exchange=last)
    big = {**_shard_halves([in_slots], BIG[:1], core), **_shard_halves(late_slots, LATE, core)}
    return grad_x, big, _small_sum(small_slots)


WEIGHTS = ("w_in", "ssm_lambda_re", "ssm_lambda_im", "ssm_b_re", "ssm_b_im", "ssm_c_re", "ssm_c_im", "ssm_d",
           "ssm_log_step", "w_glu", "attn_sinks", "rel_bias_table", "w_branch_ssm", "w_branch_attn", "w_out",
           "ln_gain", "ln_bias")


def kernel(x, w_in, ssm_lambda_re, ssm_lambda_im, ssm_b_re, ssm_b_im, ssm_c_re, ssm_c_im, ssm_d, ssm_log_step, w_glu, attn_sinks, rel_bias_table, w_branch_ssm, w_branch_attn, w_out, ln_gain, ln_bias, loss_target, m_w_in, m_ssm_lambda_re, m_ssm_lambda_im, m_ssm_b_re, m_ssm_b_im, m_ssm_c_re, m_ssm_c_im, m_ssm_d, m_ssm_log_step, m_w_glu, m_attn_sinks, m_rel_bias_table, m_w_branch_ssm, m_w_branch_attn, m_w_out, m_ln_gain, m_ln_bias, v_w_in, v_ssm_lambda_re, v_ssm_lambda_im, v_ssm_b_re, v_ssm_b_im, v_ssm_c_re, v_ssm_c_im, v_ssm_d, v_ssm_log_step, v_w_glu, v_attn_sinks, v_rel_bias_table, v_w_branch_ssm, v_w_branch_attn, v_w_out, v_ln_gain, v_ln_bias):
    given = dict(locals())
    w = {k: given[k] for k in WEIGHTS}
    m = {k: given["m_" + k] for k in WEIGHTS}
    v = {k: given["v_" + k] for k in WEIGHTS}

    core = lax.axis_index("c").astype(jnp.int32).reshape(1)
    shards = {k: _cast(w[k][0], BF16, name="cast_" + k) for k in BIG}
    chip = (2 * lax.axis_index("x") + lax.axis_index("y")).astype(jnp.int32).reshape(1)
    grad_x, g_shard, g_packed = _local_step(
        x[0], loss_target[0], shards, core, chip, ssm_lambda_re[0], ssm_lambda_im[0], ssm_b_re[0],
        ssm_b_im[0], ssm_c_re[0], ssm_c_im[0], ssm_d, ssm_log_step, attn_sinks, rel_bias_table, ln_gain, ln_bias)
    loss_sum = _unpack(g_packed)[1]

    grad, delta, new_m, new_v = {}, {}, {}, {}
    for k in BIG:
        grad[k], delta[k], new_m[k], new_v[k] = _adamw_shard(w[k], m[k], v[k], g_shard[k], core,
                                                             name="adamw_" + k, rows=SHARE_ROWS[k])
    gs, ds, ms, vs = _adamw(_pack(w), _pack(m), _pack(v), [g_packed], name="adamw_small")
    for dst, packed in ((grad, gs), (delta, ds), (new_m, ms), (new_v, vs)):
        dst.update(_unpack(packed)[0])

    return (loss_sum, grad_x[None], *[grad[k] for k in WEIGHTS], *[delta[k] for k in WEIGHTS],
            *[new_m[k] for k in WEIGHTS], *[new_v[k] for k in WEIGHTS])
```

```python
import functools
import math

import numpy as np
import jax
import jax.numpy as jnp
from jax import lax
from jax.experimental import pallas as pl
from jax.experimental.pallas import tpu as pltpu

F32 = jnp.float32
BF16 = jnp.bfloat16

D_MODEL = 2048
D_SSM = 1024
SSM_GROUP = 16
N_GROUPS = 64
SSM_STATE = 64
N_LANES = N_GROUPS * SSM_STATE
N_Q_HEADS = 16
N_KV_HEADS = 4
HEAD_DIM = 64
Q_PER_KV = 4
D_ATTN = 1024
D_KV = 256
WINDOW = 128
BLOCK = 128
N_BUCKETS = 32
MAX_DISTANCE = 128
D_IN = 8704
ALPHA = 2.0 ** 0.25
LN_EPS = 1e-5
NEG_INF = -1e30
ATTN_SCALE = HEAD_DIM ** -0.5

C_U, C_ZS, C_Q, C_K, C_V, C_ZA, C_GL = 0, 1024, 2048, 3072, 3328, 3584, 4608

ADAM_LR = 0.001
ADAM_B1 = 0.9
ADAM_B2 = 0.999
ADAM_EPS = 1e-08
ADAM_WD = 0.01
ADAM_STEP = 10

N_CHIPS = 4
MESH = pl.DeviceIdType.MESH

SSM_CHUNK = 256
SEG_LEN = SSM_CHUNK // 8
SLAB_LANES = 512
N_SLABS = N_LANES // SLAB_LANES
SLAB_CH = D_SSM // N_SLABS

VMEM_LIMIT = 60 * 1024 * 1024


def _params(sem=None, **kw):
    return pltpu.CompilerParams(dimension_semantics=sem, vmem_limit_bytes=VMEM_LIMIT, **kw)


_DIMS = {"nn": (((1,), (0,)), ((), ())), "nt": (((1,), (1,)), ((), ())), "tn": (((0,), (0,)), ((), ()))}


class _Exchange:
    def __init__(self, ins, out_shape, n_sems, copies):
        self.ins, self.out_shape, self.n_sems, self.copies = list(ins), list(out_shape), list(n_sems), copies

    def start(self, ins, outs, sems):
        remote, local = self.copies(ins, outs, sems)
        for cp in remote + local:
            cp.start()

    def finish(self, ins, outs, sems):
        remote, local = self.copies(ins, outs, sems)
        for cp in remote:
            cp.wait_recv()
        for cp in remote:
            cp.wait_send()
        for cp in local:
            cp.wait()


def _mm_body(*refs, dims, nk, res_scale, exchange, grid):
    n_ex_in = len(exchange.ins) if exchange else 0
    n_ex_out = len(exchange.out_shape) if exchange else 0
    n_in = 2 + (res_scale is not None) + n_ex_in
    a_ref, b_ref = refs[0], refs[1]
    r_ref = refs[2] if res_scale is not None else None
    ex_in = refs[n_in - n_ex_in:n_in]
    o_ref, ex_out = refs[n_in], refs[n_in + 1:n_in + 1 + n_ex_out]
    acc_ref, ex_sems = refs[n_in + 1 + n_ex_out], refs[n_in + 2 + n_ex_out:]
    i, j, k = pl.program_id(0), pl.program_id(1), pl.program_id(2)

    if exchange:
        @pl.when(jnp.logical_and(jnp.logical_and(i == 0, j == 0), k == 0))
        def _():
            exchange.start(ex_in, ex_out, ex_sems)

    part = lax.dot_general(a_ref[...].astype(BF16), b_ref[...].astype(BF16), _DIMS[dims],
                           preferred_element_type=F32)

    def finish(acc):
        if r_ref is not None:
            acc = acc + res_scale * r_ref[...].astype(F32)
        o_ref[...] = acc.astype(o_ref.dtype)

    if nk == 1:
        finish(part)
    else:
        @pl.when(k == 0)
        def _():
            acc_ref[...] = part

        @pl.when(k > 0)
        def _():
            acc_ref[...] += part

        @pl.when(k == nk - 1)
        def _():
            finish(acc_ref[...])

    if exchange:
        @pl.when(jnp.logical_and(jnp.logical_and(i == grid[0] - 1, j == grid[1] - 1), k == grid[2] - 1))
        def _():
            exchange.finish(ex_in, ex_out, ex_sems)


def _matmul(a, b, dims, *, name, out_dtype=F32, tm=512, tn=512, tk=None, res=None, res_scale=None, exchange=None):
    if dims == "nn":
        (m, kk), n = a.shape, b.shape[1]
    elif dims == "nt":
        (m, kk), n = a.shape, b.shape[0]
    else:
        (kk, m), n = a.shape, b.shape[1]
    tm, tn = min(tm, m), min(tn, n)
    tk = kk if tk is None else min(tk, kk)
    assert m % tm == 0 and n % tn == 0 and kk % tk == 0, (name, m, n, kk, tm, tn, tk)
    nk = kk // tk
    a_spec = {"nn": pl.BlockSpec((tm, tk), lambda i, j, k: (i, k)),
              "nt": pl.BlockSpec((tm, tk), lambda i, j, k: (i, k)),
              "tn": pl.BlockSpec((tk, tm), lambda i, j, k: (k, i))}[dims]
    b_spec = {"nn": pl.BlockSpec((tk, tn), lambda i, j, k: (k, j)),
              "nt": pl.BlockSpec((tn, tk), lambda i, j, k: (j, k)),
              "tn": pl.BlockSpec((tk, tn), lambda i, j, k: (k, j))}[dims]
    in_specs, args = [a_spec, b_spec], [a, b]
    if res is not None:
        in_specs.append(pl.BlockSpec((tm, tn), lambda i, j, k: (i, j)))
        args.append(res)
    grid = (m // tm, n // tn, nk)
    out_specs = [pl.BlockSpec((tm, tn), lambda i, j, k: (i, j))]
    out_shape = [jax.ShapeDtypeStruct((m, n), out_dtype)]
    scratch = [pltpu.VMEM((tm, tn), F32)]
    if exchange:
        in_specs += [pl.BlockSpec(memory_space=pl.ANY)] * len(exchange.ins)
        args += exchange.ins
        out_specs += [pl.BlockSpec(memory_space=pl.ANY)] * len(exchange.out_shape)
        out_shape += exchange.out_shape
        scratch += [pltpu.SemaphoreType.DMA((s,)) for s in exchange.n_sems]
    outs = pl.pallas_call(
        functools.partial(_mm_body, dims=dims, nk=nk, res_scale=res_scale if res is not None else None,
                          exchange=exchange, grid=grid),
        name=name, grid=grid, in_specs=in_specs, out_specs=out_specs, out_shape=out_shape, scratch_shapes=scratch,
        compiler_params=_params(("arbitrary",) * 3 if exchange else ("parallel", "parallel", "arbitrary")),
    )(*args)
    return (outs[0], outs[1:]) if exchange else outs[0]


def _proj_shards_body(chip_ref, *refs, n_ex_in, has_base, exchange, grid):
    del chip_ref
    refs = list(refs)
    if has_base:
        del refs[2 + n_ex_in]
    _mm_body(*refs, dims="nn", nk=1, res_scale=None, exchange=exchange, grid=grid)


def _proj_shards(xb, w, chip, offsets, *, name, base=None, exchange=None, tm=1024):
    t, d = xb.shape
    cs = D_IN // N_CHIPS
    own = w.shape[1] == cs
    tm = min(tm, t)
    col = lambda j, cr: (cr[0] + offsets[0] + j) % N_CHIPS
    grid = (len(offsets), t // tm, 1)
    in_specs = [pl.BlockSpec((tm, d), lambda j, i, k, cr: (i, 0)),
                pl.BlockSpec((d, cs), lambda j, i, k, cr: (0, 0 if own else col(j, cr)))]
    args = [xb, w]
    out_specs = [pl.BlockSpec((tm, cs), lambda j, i, k, cr: (i, col(j, cr)))]
    out_shape = [jax.ShapeDtypeStruct((t, D_IN), BF16)]
    scratch = [pltpu.VMEM((8, 128), F32)]
    n_ex_in = 0
    if exchange:
        n_ex_in = len(exchange.ins)
        in_specs += [pl.BlockSpec(memory_space=pl.ANY)] * n_ex_in
        args += exchange.ins
        out_specs += [pl.BlockSpec(memory_space=pl.ANY)] * len(exchange.out_shape)
        out_shape += exchange.out_shape
        scratch += [pltpu.SemaphoreType.DMA((s,)) for s in exchange.n_sems]
    aliases = {}
    if base is not None:
        in_specs.append(pl.BlockSpec(memory_space=pl.ANY))
        args.append(base)
        aliases = {len(args): 0}
    outs = pl.pallas_call(
        functools.partial(_proj_shards_body, n_ex_in=n_ex_in, has_base=base is not None, exchange=exchange, grid=grid),
        name=name,
        grid_spec=pltpu.PrefetchScalarGridSpec(num_scalar_prefetch=1, grid=grid, in_specs=in_specs,
                                               out_specs=out_specs, scratch_shapes=scratch),
        out_shape=out_shape, input_output_aliases=aliases,
        compiler_params=_params(("arbitrary",) * 3),
    )(chip, *args)
    return (outs[0], outs[1:]) if exchange else outs[0]


def _sigmoid(v):
    return 1.0 / (1.0 + jnp.exp(-v))


def _silu_and_grad(z):
    s = _sigmoid(z)
    return z * s, s * (1.0 + z * (1.0 - s))


def _cast_body(x_ref, o_ref):
    o_ref[...] = x_ref[...].astype(o_ref.dtype)


def _cast_and_transpose_body(x_ref, after_ref, o_ref, ot_ref):
    del after_ref
    xb = x_ref[...].astype(BF16)
    o_ref[...] = xb
    ot_ref[...] = xb.T


def _cast_and_transpose(x, after, *, name, rows=512):
    m, n = x.shape
    rows = min(rows, m)
    return pl.pallas_call(
        functools.partial(_cast_and_transpose_body), name=name, grid=(m // rows,),
        in_specs=[pl.BlockSpec((rows, n), lambda i: (i, 0)), pl.BlockSpec(memory_space=pl.ANY)],
        out_specs=[pl.BlockSpec((rows, n), lambda i: (i, 0)), pl.BlockSpec((n, rows), lambda i: (0, i))],
        out_shape=[jax.ShapeDtypeStruct((m, n), BF16), jax.ShapeDtypeStruct((n, m), BF16)],
        compiler_params=_params(("parallel",)),
    )(x, after)


def _cast(x, dtype, *, name, rows=512):
    m, n = x.shape
    rows = min(rows, m)
    return pl.pallas_call(
        functools.partial(_cast_body), name=name, grid=(m // rows,),
        in_specs=[pl.BlockSpec((rows, n), lambda i: (i, 0))],
        out_specs=pl.BlockSpec((rows, n), lambda i: (i, 0)),
        out_shape=jax.ShapeDtypeStruct((m, n), dtype),
        compiler_params=_params(("parallel",)),
    )(x)


def _lam_bar(lr, li, ls):
    step = jnp.exp(ls)
    er = jnp.exp(lr * step)
    return step, er * jnp.cos(li * step), er * jnp.sin(li * step)


def _ssm_pow_body(lr_ref, li_ref, ls_ref, pr_ref, pi_ref):
    _, ar, ai = _lam_bar(lr_ref[...], li_ref[...], ls_ref[...])
    cr, ci = ar, ai
    for i in range(SEG_LEN):
        pr_ref[pl.ds(i, 1), :] = cr
        pi_ref[pl.ds(i, 1), :] = ci
        cr, ci = cr * ar - ci * ai, cr * ai + ci * ar


def _ssm_bbar_body(lr_ref, li_ref, ls_ref, br_ref, bi_ref, or_ref, oi_ref):
    lr, li = lr_ref[...], li_ref[...]
    _, ar, ai = _lam_bar(lr, li, ls_ref[...])
    d = lr * lr + li * li
    ir, ii = lr / d, -li / d
    nr, ni = ar - 1.0, ai
    cr, ci = nr * ir - ni * ii, nr * ii + ni * ir
    br, bi = br_ref[...], bi_ref[...]
    or_ref[...] = cr * br - ci * bi
    oi_ref[...] = cr * bi + ci * br


def _ssm_param_bwd_body(lr_ref, li_ref, ls_ref, br_ref, bi_ref, glr_ref, gli_ref, gbr_ref, gbi_ref,
                        dlr_ref, dli_ref, dls_ref, dbr_ref, dbi_ref):
    lr, li = lr_ref[...], li_ref[...]
    step, ar, ai = _lam_bar(lr, li, ls_ref[...])
    d = lr * lr + li * li
    ir, ii = lr / d, -li / d
    nr, ni = ar - 1.0, ai
    cr, ci = nr * ir - ni * ii, nr * ii + ni * ir
    br, bi, gbr, gbi = br_ref[...], bi_ref[...], gbr_ref[...], gbi_ref[...]
    dbr_ref[...] = cr * gbr + ci * gbi
    dbi_ref[...] = cr * gbi - ci * gbr
    gcr = jnp.sum(br * gbr + bi * gbi, axis=1, keepdims=True)
    gci = jnp.sum(br * gbi - bi * gbr, axis=1, keepdims=True)
    gnr, gni = ir * gcr + ii * gci, ir * gci - ii * gcr
    gir, gii = nr * gcr + ni * gci, nr * gci - ni * gcr
    gtr, gti = glr_ref[...] + gnr, gli_ref[...] + gni
    i2r, i2i = ir * ir - ii * ii, 2.0 * ir * ii
    g1r, g1i = -(i2r * gir + i2i * gii), -(i2r * gii - i2i * gir)
    g2r, g2i = step * (ar * gtr + ai * gti), step * (ar * gti - ai * gtr)
    mr, mi = lr * ar - li * ai, lr * ai + li * ar
    dlr_ref[...] = g1r + g2r
    dli_ref[...] = g1i + g2i
    dls_ref[...] = (mr * gtr + mi * gti) * step


def _whole(shape):
    return pl.BlockSpec(shape, lambda *_: (0,) * len(shape))


def _ssm_prepare(lam_re, lam_im, b_re, b_im, c_re, c_im, log_step):
    row = lambda a: a.reshape(1, N_LANES)
    col = lambda a: a.reshape(N_LANES, 1)
    ls = jnp.repeat(log_step.reshape(N_GROUPS), SSM_STATE)
    p_re, p_im = pl.pallas_call(
        functools.partial(_ssm_pow_body), name="ssm_pow",
        in_specs=[_whole((1, N_LANES))] * 3, out_specs=[_whole((SEG_LEN, N_LANES))] * 2,
        out_shape=[jax.ShapeDtypeStruct((SEG_LEN, N_LANES), F32)] * 2, grid=(1,),
    )(row(lam_re), row(lam_im), row(ls))
    bb_re, bb_im = pl.pallas_call(
        functools.partial(_ssm_bbar_body), name="ssm_bbar",
        in_specs=[_whole((N_LANES, 1))] * 3 + [_whole((N_LANES, SSM_GROUP))] * 2,
        out_specs=[_whole((N_LANES, SSM_GROUP))] * 2,
        out_shape=[jax.ShapeDtypeStruct((N_LANES, SSM_GROUP), F32)] * 2, grid=(1,),
    )(col(lam_re), col(lam_im), col(ls), b_re.reshape(N_LANES, SSM_GROUP), b_im.reshape(N_LANES, SSM_GROUP))
    eye = jnp.eye(8, dtype=F32)

    def b_slabs(bb):
        t = bb.reshape(N_SLABS, 8, SSM_STATE, SSM_GROUP).transpose(0, 1, 3, 2)
        return (t[:, :, :, None, :] * eye[None, :, None, :, None]).reshape(N_SLABS, SLAB_CH, SLAB_LANES)

    def c_slabs(c):
        t = c.reshape(N_SLABS, 8, SSM_GROUP, SSM_STATE).transpose(0, 1, 3, 2)
        return (t[:, :, :, None, :] * eye[None, :, None, :, None]).reshape(N_SLABS, SLAB_LANES, SLAB_CH)

    b_bd = jnp.concatenate([b_slabs(bb_re), b_slabs(bb_im)], axis=2).astype(BF16)
    c_bd = jnp.concatenate([c_slabs(c_re.reshape(N_GROUPS, SSM_GROUP, SSM_STATE)),
                            -c_slabs(c_im.reshape(N_GROUPS, SSM_GROUP, SSM_STATE))], axis=1).astype(BF16)
    return p_re, p_im, b_bd, c_bd


def _diag_blocks_b(g):
    t = g.reshape(N_SLABS, 8, SSM_GROUP, 8, SSM_STATE)
    t = jnp.stack([t[:, i, :, i, :] for i in range(8)], axis=1)
    return t.transpose(0, 1, 3, 2).reshape(N_LANES, SSM_GROUP)


def _diag_blocks_c(g):
    t = g.reshape(N_SLABS, 8, SSM_STATE, 8, SSM_GROUP)
    t = jnp.stack([t[:, i, :, i, :] for i in range(8)], axis=1)
    return t.transpose(0, 1, 3, 2).reshape(N_GROUPS, SSM_GROUP, SSM_STATE)


def _ssm_param_grads(lam_re, lam_im, b_re, b_im, log_step, g_lam_re, g_lam_im, g_bbd, g_cbd):
    col = lambda a: a.reshape(N_LANES, 1)
    ls = jnp.repeat(log_step.reshape(N_GROUPS), SSM_STATE)
    gbr = _diag_blocks_b(g_bbd[:, :, :SLAB_LANES])
    gbi = _diag_blocks_b(g_bbd[:, :, SLAB_LANES:])
    outs = pl.pallas_call(
        functools.partial(_ssm_param_bwd_body), name="ssm_param_bwd", grid=(1,),
        in_specs=[_whole((N_LANES, 1))] * 3 + [_whole((N_LANES, SSM_GROUP))] * 2 + [_whole((N_LANES, 1))] * 2
        + [_whole((N_LANES, SSM_GROUP))] * 2,
        out_specs=[_whole((N_LANES, 1))] * 3 + [_whole((N_LANES, SSM_GROUP))] * 2,
        out_shape=[jax.ShapeDtypeStruct((N_LANES, 1), F32)] * 3 + [jax.ShapeDtypeStruct((N_LANES, SSM_GROUP), F32)] * 2,
    )(col(lam_re), col(lam_im), col(ls), b_re.reshape(N_LANES, SSM_GROUP), b_im.reshape(N_LANES, SSM_GROUP),
      col(g_lam_re), col(g_lam_im), gbr, gbi)
    dlr, dli, dls, dbr, dbi = outs
    d_c_re = _diag_blocks_c(g_cbd[:, :SLAB_LANES, :])
    d_c_im = -_diag_blocks_c(g_cbd[:, SLAB_LANES:, :])
    return (dlr.reshape(1, N_GROUPS, SSM_STATE), dli.reshape(1, N_GROUPS, SSM_STATE),
            dbr.reshape(1, N_GROUPS, SSM_STATE, SSM_GROUP), dbi.reshape(1, N_GROUPS, SSM_STATE, SSM_GROUP),
            d_c_re[None], d_c_im[None], dls.reshape(N_GROUPS, SSM_STATE).sum(axis=1).reshape(1, N_GROUPS))


def _bcast8(v):
    return jnp.broadcast_to(v, (8, v.shape[1]))


def _segment_permutation():
    p = np.zeros((SSM_CHUNK, SSM_CHUNK), np.float32)
    rows = np.arange(SSM_CHUNK)
    p[rows, (rows % 8) * SEG_LEN + rows // 8] = 1.0
    return p


def _permute_exact(perm, val, pieces):
    out, rest = None, val
    for n in range(pieces):
        part = rest.astype(BF16)
        moved = jnp.dot(perm, part, preferred_element_type=F32)
        out = moved if out is None else out + moved
        if n + 1 < pieces:
            rest = rest - part.astype(F32)
    return out


def _scan_buffer():
    return pltpu.VMEM((SSM_CHUNK, N_LANES), F32)


def _lanes(k):
    return pl.ds(k * SLAB_LANES, SLAB_LANES)


def _tile(i):
    return pl.ds(i * 8 if isinstance(i, int) else pl.multiple_of(i * 8, 8), 8)


SCAN_LANES = 1024


def _scan_lanes(k):
    return pl.ds(k * SCAN_LANES, SCAN_LANES)


def _seg_get(ref, k, i):
    return ref[_tile(i), _scan_lanes(k)]


def _seg_put(ref, k, i, val):
    ref[_tile(i), _scan_lanes(k)] = val


def _slab_get(ref, k):
    return ref[:, _lanes(k)]


def _slab_put(ref, k, val):
    ref[:, _lanes(k)] = val


def _scan_forward(s_re, s_im, p_re, p_im, car_re, car_im, sp_re=None, sp_im=None):
    for k in range(N_LANES // SCAN_LANES):
        ln = _scan_lanes(k)
        ar, ai =_bcast8(p_re[pl.ds(0, 1), ln]), _bcast8(p_im[pl.ds(0, 1), ln])

        def step(i, s, k=k, ar=ar, ai=ai):
            sr, si = s
            nr = ar * sr - ai * si + _seg_get(s_re, k, i)
            ni = ar * si + ai * sr + _seg_get(s_im, k, i)
            _seg_put(s_re, k, i, nr)
            _seg_put(s_im, k, i, ni)
            return nr, ni

        zero = jnp.zeros((8, SCAN_LANES), F32)
        er, ei = lax.fori_loop(0, SEG_LEN, step, (zero, zero), unroll=4)
        lr, li = p_re[pl.ds(SEG_LEN - 1, 1), ln], p_im[pl.ds(SEG_LEN - 1, 1), ln]
        cr, ci = car_re[pl.ds(0, 1), ln], car_im[pl.ds(0, 1), ln]
        rows_r, rows_i = [], []
        for r in range(8):
            rows_r.append(cr)
            rows_i.append(ci)
            cr, ci = er[r:r + 1] + lr * cr - li * ci, ei[r:r + 1] + lr * ci + li * cr
        pr8, pi8 = jnp.concatenate(rows_r, axis=0), jnp.concatenate(rows_i, axis=0)
        car_re[:, ln] = _bcast8(cr)
        car_im[:, ln] = _bcast8(ci)
        if sp_re is not None:
            sp_re[:, ln] = pr8
            sp_im[:, ln] = pi8

        def fix(i, _, k=k, ln=ln, pr8=pr8, pi8=pi8):
            qr, qi = _bcast8(p_re[pl.ds(i, 1), ln]), _bcast8(p_im[pl.ds(i, 1), ln])
            _seg_put(s_re, k, i, _seg_get(s_re, k, i) + qr * pr8 - qi * pi8)
            _seg_put(s_im, k, i, _seg_get(s_im, k, i) + qr * pi8 + qi * pr8)
            return 0

        lax.fori_loop(0, SEG_LEN, fix, 0, unroll=4)


def _scan_backward(g_re, g_im, s_re, s_im, sp_re, sp_im, p_re, p_im, car_re, car_im, acc_re, acc_im):
    for k in range(N_LANES // SCAN_LANES):
        ln = _scan_lanes(k)
        ar, ai =_bcast8(p_re[pl.ds(0, 1), ln]), -_bcast8(p_im[pl.ds(0, 1), ln])

        def step(j, s, k=k, ar=ar, ai=ai):
            i = SEG_LEN - 1 - j
            sr, si = s
            nr = ar * sr - ai * si + _seg_get(g_re, k, i)
            ni = ar * si + ai * sr + _seg_get(g_im, k, i)
            _seg_put(g_re, k, i, nr)
            _seg_put(g_im, k, i, ni)
            return nr, ni

        zero = jnp.zeros((8, SCAN_LANES), F32)
        er, ei = lax.fori_loop(0, SEG_LEN, step, (zero, zero), unroll=4)
        lr, li = p_re[pl.ds(SEG_LEN - 1, 1), ln], -p_im[pl.ds(SEG_LEN - 1, 1), ln]
        cr, ci = car_re[pl.ds(0, 1), ln], car_im[pl.ds(0, 1), ln]
        rows_r, rows_i = [None] * 8, [None] * 8
        for r in range(7, -1, -1):
            rows_r[r], rows_i[r] = cr, ci
            cr, ci = er[r:r + 1] + lr * cr - li * ci, ei[r:r + 1] + lr * ci + li * cr
        nr8, ni8 = jnp.concatenate(rows_r, axis=0), jnp.concatenate(rows_i, axis=0)
        car_re[:, ln] = _bcast8(cr)
        car_im[:, ln] = _bcast8(ci)

        def fix(i, acc, k=k, ln=ln, nr8=nr8, ni8=ni8):
            qr = _bcast8(p_re[pl.ds(SEG_LEN - 1 - i, 1), ln])
            qi = -_bcast8(p_im[pl.ds(SEG_LEN - 1 - i, 1), ln])
            gr = _seg_get(g_re, k, i) + qr * nr8 - qi * ni8
            gi = _seg_get(g_im, k, i) + qr * ni8 + qi * nr8
            _seg_put(g_re, k, i, gr)
            _seg_put(g_im, k, i, gi)
            return gr, gi

        def prod(xr, xi, gr, gi):
            return xr * gr + xi * gi, xr * gi - xi * gr

        gr, gi = fix(0, None)
        a_r, a_i = prod(sp_re[:, ln], sp_im[:, ln], gr, gi)

        def fix_acc(i, acc, k=k, fix=fix):
            gr, gi = fix(i, None)
            dr, di = prod(_seg_get(s_re, k, i - 1), _seg_get(s_im, k, i - 1), gr, gi)
            return acc[0] + dr, acc[1] + di

        acc = (a_r, a_i)
        for i in range(1, 4):
            acc = fix_acc(i, acc)
        a_r, a_i = lax.fori_loop(4, SEG_LEN, fix_acc, acc, unroll=4)
        acc_re[:, ln] += a_r
        acc_im[:, ln] += a_i


def _gelu_and_grad(y):
    cdf = 0.5 * (1.0 + lax.erf(y * (2.0 ** -0.5)))
    pdf = jnp.exp(-0.5 * y * y) * (1.0 / math.sqrt(2.0 * math.pi))
    return y * cdf, cdf + y * pdf


def _ssm_fwd_body(u_ref, bbd_ref, cbd_ref, pre_ref, pim_ref, d_ref, perm_ref, unperm_ref,
                  y_ref, gin_ref, gint_ref, cre_out, cim_out, s_re, s_im, car_re, car_im, yp):
    c = pl.program_id(0)

    @pl.when(c == 0)
    def _():
        car_re[...] = jnp.zeros_like(car_re)
        car_im[...] = jnp.zeros_like(car_im)

    cre_out[...] = car_re[...]
    cim_out[...] = car_im[...]
    u = u_ref[...].astype(F32)
    up = jnp.dot(perm_ref[...], u_ref[...].astype(BF16), preferred_element_type=F32).astype(BF16)
    for k in range(N_SLABS):
        bu = jnp.dot(up[:, k * SLAB_CH:(k + 1) * SLAB_CH], bbd_ref[k], preferred_element_type=F32)
        _slab_put(s_re, k, bu[:, :SLAB_LANES])
        _slab_put(s_im, k, bu[:, SLAB_LANES:])
    _scan_forward(s_re, s_im, pre_ref, pim_ref, car_re, car_im)
    for k in range(N_SLABS):
        yp[:, pl.ds(k * SLAB_CH, SLAB_CH)] = (
            jnp.dot(_slab_get(s_re, k).astype(BF16), cbd_ref[k, :SLAB_LANES, :], preferred_element_type=F32)
            + jnp.dot(_slab_get(s_im, k).astype(BF16), cbd_ref[k, SLAB_LANES:, :], preferred_element_type=F32))
    y = _permute_exact(unperm_ref[...], yp[...], 2) + d_ref[...] * u
    y_ref[...] = y.astype(y_ref.dtype)
    _store_with_transpose(_gelu_and_grad(y)[0], gin_ref, gint_ref)


def _ssm_forward(proj, p_re, p_im, b_bd, c_bd, d_skip):
    t = proj.shape[0]
    nc = t // SSM_CHUNK
    perm = _segment_permutation()
    return pl.pallas_call(
        functools.partial(_ssm_fwd_body), name="ssm_fwd", grid=(nc,),
        in_specs=[pl.BlockSpec((SSM_CHUNK, D_SSM), lambda c: (c, C_U // D_SSM)),
                  _whole(b_bd.shape), _whole(c_bd.shape), _whole(p_re.shape), _whole(p_im.shape),
                  _whole((1, D_SSM)), _whole(perm.shape), _whole(perm.shape)],
        out_specs=[pl.BlockSpec((SSM_CHUNK, D_SSM), lambda c: (c, 0)),
                   pl.BlockSpec((SSM_CHUNK, D_SSM), lambda c: (c, 0)),
                   pl.BlockSpec((D_SSM, SSM_CHUNK), lambda c: (0, c)),
                   pl.BlockSpec((None, 8, N_LANES), lambda c: (c, 0, 0)),
                   pl.BlockSpec((None, 8, N_LANES), lambda c: (c, 0, 0))],
        out_shape=[jax.ShapeDtypeStruct((t, D_SSM), BF16), jax.ShapeDtypeStruct((t, D_SSM), BF16),
                   jax.ShapeDtypeStruct((D_SSM, t), BF16),
                   jax.ShapeDtypeStruct((nc, 8, N_LANES), F32), jax.ShapeDtypeStruct((nc, 8, N_LANES), F32)],
        scratch_shapes=[_scan_buffer(), _scan_buffer(),
                        pltpu.VMEM((8, N_LANES), F32), pltpu.VMEM((8, N_LANES), F32),
                        pltpu.VMEM((SSM_CHUNK, D_SSM), F32)],
        compiler_params=_params(("arbitrary",)),
    )(proj, b_bd, c_bd, p_re, p_im, d_skip, jnp.asarray(perm, BF16), jnp.asarray(perm.T, BF16))


def _ssm_bwd_body(u_ref, dgin_ref, y_ref, cre_in, cim_in, bbd_ref, cbd_ref, pre_ref, pim_ref, d_ref, perm_ref,
                  unperm_ref, dproj_in,
                  du_ref, gb_ref, gc_ref, glr_ref, gli_ref, gd_ref,
                  s_re, s_im, g_re, g_im, sp_re, sp_im, car_re, car_im, gcar_re, gcar_im, acc_re, acc_im, dup):
    del dproj_in
    c = pl.program_id(0)
    nc = pl.num_programs(0)

    @pl.when(c == 0)
    def _():
        gcar_re[...] = jnp.zeros_like(gcar_re)
        gcar_im[...] = jnp.zeros_like(gcar_im)
        acc_re[...] = jnp.zeros_like(acc_re)
        acc_im[...] = jnp.zeros_like(acc_im)
        gb_ref[...] = jnp.zeros_like(gb_ref)
        gc_ref[...] = jnp.zeros_like(gc_ref)
        gd_ref[...] = jnp.zeros_like(gd_ref)

    car_re[...] = cre_in[...]
    car_im[...] = cim_in[...]
    u = u_ref[...].astype(F32)
    dy = dgin_ref[...].astype(F32) * _gelu_and_grad(y_ref[...].astype(F32))[1]
    gd_ref[...] += jnp.sum(dy * u, axis=0, keepdims=True)
    up = jnp.dot(perm_ref[...], u.astype(BF16), preferred_element_type=F32).astype(BF16)
    dyp = jnp.dot(perm_ref[...], dy.astype(BF16), preferred_element_type=F32).astype(BF16)
    for k in range(N_SLABS):
        ch = slice(k * SLAB_CH, (k + 1) * SLAB_CH)
        bu = jnp.dot(up[:, ch], bbd_ref[k], preferred_element_type=F32)
        _slab_put(s_re, k, bu[:, :SLAB_LANES])
        _slab_put(s_im, k, bu[:, SLAB_LANES:])
        ds = lax.dot_general(dyp[:, ch], cbd_ref[k], _DIMS["nt"], preferred_element_type=F32)
        _slab_put(g_re, k, ds[:, :SLAB_LANES])
        _slab_put(g_im, k, ds[:, SLAB_LANES:])
    _scan_forward(s_re, s_im, pre_ref, pim_ref, car_re, car_im, sp_re, sp_im)
    _scan_backward(g_re, g_im, s_re, s_im, sp_re, sp_im, pre_ref, pim_ref, gcar_re, gcar_im, acc_re, acc_im)
    for k in range(N_SLABS):
        ch = slice(k * SLAB_CH, (k + 1) * SLAB_CH)
        uk, dyk = up[:, ch], dyp[:, ch]
        sr, si = _slab_get(s_re, k).astype(BF16), _slab_get(s_im, k).astype(BF16)
        gr, gi = _slab_get(g_re, k).astype(BF16), _slab_get(g_im, k).astype(BF16)
        gc_ref[k, :SLAB_LANES, :] += lax.dot_general(sr, dyk, _DIMS["tn"], preferred_element_type=F32)
        gc_ref[k, SLAB_LANES:, :] += lax.dot_general(si, dyk, _DIMS["tn"], preferred_element_type=F32)
        gb_ref[k, :, :SLAB_LANES] += lax.dot_general(uk, gr, _DIMS["tn"], preferred_element_type=F32)
        gb_ref[k, :, SLAB_LANES:] += lax.dot_general(uk, gi, _DIMS["tn"], preferred_element_type=F32)
        dup[:, pl.ds(k * SLAB_CH, SLAB_CH)] = (
            lax.dot_general(gr, bbd_ref[k, :, :SLAB_LANES], _DIMS["nt"], preferred_element_type=F32)
            + lax.dot_general(gi, bbd_ref[k, :, SLAB_LANES:], _DIMS["nt"], preferred_element_type=F32))
    du = _permute_exact(unperm_ref[...], dup[...], 2) + d_ref[...] * dy
    du_ref[...] = du.astype(du_ref.dtype)

    @pl.when(c == nc - 1)
    def _():
        glr_ref[...] = jnp.sum(acc_re[...], axis=0, keepdims=True)
        gli_ref[...] = jnp.sum(acc_im[...], axis=0, keepdims=True)


def _ssm_backward(proj, dg_in, y_ssm, car_re, car_im, p_re, p_im, b_bd, c_bd, d_skip, dproj):
    t = proj.shape[0]
    nc = t // SSM_CHUNK
    rev = lambda c: nc - 1 - c
    big = _scan_buffer
    small = lambda: pltpu.VMEM((8, N_LANES), F32)
    perm = _segment_permutation()
    outs = pl.pallas_call(
        functools.partial(_ssm_bwd_body), name="ssm_bwd", grid=(nc,),
        in_specs=[pl.BlockSpec((SSM_CHUNK, D_SSM), lambda c: (rev(c), C_U // D_SSM)),
                  pl.BlockSpec((SSM_CHUNK, D_SSM), lambda c: (rev(c), 0)),
                  pl.BlockSpec((SSM_CHUNK, D_SSM), lambda c: (rev(c), 0)),
                  pl.BlockSpec((None, 8, N_LANES), lambda c: (rev(c), 0, 0)),
                  pl.BlockSpec((None, 8, N_LANES), lambda c: (rev(c), 0, 0)),
                  _whole(b_bd.shape), _whole(c_bd.shape), _whole(p_re.shape), _whole(p_im.shape),
                  _whole((1, D_SSM)), _whole(perm.shape), _whole(perm.shape), pl.BlockSpec(memory_space=pl.ANY)],
        out_specs=[pl.BlockSpec((SSM_CHUNK, D_SSM), lambda c: (rev(c), C_U // D_SSM)),
                   _whole(b_bd.shape), _whole(c_bd.shape), _whole((1, N_LANES)), _whole((1, N_LANES)),
                   _whole((1, D_SSM))],
        out_shape=[jax.ShapeDtypeStruct(dproj.shape, dproj.dtype),
                   jax.ShapeDtypeStruct(b_bd.shape, F32), jax.ShapeDtypeStruct(c_bd.shape, F32),
                   jax.ShapeDtypeStruct((1, N_LANES), F32), jax.ShapeDtypeStruct((1, N_LANES), F32),
                   jax.ShapeDtypeStruct((1, D_SSM), F32)],
        scratch_shapes=[big(), big(), big(), big()] + [small() for _ in range(8)]
        + [pltpu.VMEM((SSM_CHUNK, D_SSM), F32)],
        input_output_aliases={12: 0},
        compiler_params=_params(("arbitrary",)),
    )(proj, dg_in, y_ssm, car_re, car_im, b_bd, c_bd, p_re, p_im, d_skip, jnp.asarray(perm, BF16),
      jnp.asarray(perm.T, BF16), dproj)
    return outs


def _bucket_table():
    i = np.arange(BLOCK)[:, None]
    j = np.arange(2 * BLOCK)[None, :]
    dist = BLOCK + i - j
    ok = (dist >= 0) & (dist < WINDOW)
    max_exact = N_BUCKETS // 2
    d = np.maximum(dist, 1).astype(np.float32)
    large = max_exact + (np.log(d / max_exact) / math.log(MAX_DISTANCE / max_exact)
                         * (N_BUCKETS - max_exact)).astype(np.int32)
    large = np.minimum(large, N_BUCKETS - 1)
    bucket = np.where(dist < max_exact, dist, large)
    return np.where(ok, bucket, -1).astype(np.int32)


def _bias_build_body(table_ref, bucket_ref, o_ref):
    h = pl.program_id(0)
    bucket = bucket_ref[...]
    acc = jnp.full(bucket.shape, NEG_INF, F32)
    for b in range(N_BUCKETS):
        acc = jnp.where(bucket == b, table_ref[b, h], acc)
    o_ref[...] = acc


def _bias_build(rel_bias_table, bucket):
    return pl.pallas_call(
        functools.partial(_bias_build_body), name="bias_build", grid=(N_Q_HEADS,),
        in_specs=[pl.BlockSpec(memory_space=pltpu.SMEM), _whole(bucket.shape)],
        out_specs=pl.BlockSpec((None, BLOCK, 2 * BLOCK), lambda h: (h, 0, 0)),
        out_shape=jax.ShapeDtypeStruct((N_Q_HEADS, BLOCK, 2 * BLOCK), F32),
        compiler_params=_params(("arbitrary",)),
    )(rel_bias_table, bucket)


def _bias_grad_body(g_ref, bucket_ref, o_ref):
    bucket = bucket_ref[...]
    lane = lax.broadcasted_iota(jnp.int32, (N_BUCKETS, 128), 1)

    def head(h, out):
        g = g_ref[h]
        rows = [jnp.sum(jnp.where(bucket == b, g, 0.0), axis=0, keepdims=True) for b in range(N_BUCKETS)]
        colsum = jnp.sum(jnp.concatenate(rows, axis=0), axis=1, keepdims=True)
        return jnp.where(lane == h, colsum, out)

    o_ref[...] = lax.fori_loop(0, N_Q_HEADS, head, jnp.zeros((N_BUCKETS, 128), F32))


def _bias_grad(g_bias, bucket):
    out = pl.pallas_call(
        functools.partial(_bias_grad_body), name="bias_grad", grid=(1,),
        in_specs=[_whole(g_bias.shape), _whole(bucket.shape)],
        out_specs=_whole((N_BUCKETS, 128)),
        out_shape=jax.ShapeDtypeStruct((N_BUCKETS, 128), F32),
        compiler_params=_params(("arbitrary",)),
    )(g_bias, bucket)
    return out[:, :N_Q_HEADS]


def _head_logits(qh, kk, bias_h, first_block):
    s = lax.dot_general(qh, kk, _DIMS["nt"], preferred_element_type=F32) * ATTN_SCALE + bias_h
    col = lax.broadcasted_iota(jnp.int32, s.shape, 1)
    return jnp.where(jnp.logical_and(first_block, col < BLOCK), NEG_INF, s)


def _attn_fwd_body(sink_ref, q_ref, kp_ref, kc_ref, vp_ref, vc_ref, bias_ref, o_ref, lse_ref):
    n = pl.program_id(0)
    outs, lses = [], []
    for kv in range(N_KV_HEADS):
        cs = slice(kv * HEAD_DIM, (kv + 1) * HEAD_DIM)
        kk = jnp.concatenate([kp_ref[:, cs], kc_ref[:, cs]], axis=0).astype(BF16)
        vv = jnp.concatenate([vp_ref[:, cs], vc_ref[:, cs]], axis=0).astype(BF16)
        for g in range(Q_PER_KV):
            h = kv * Q_PER_KV + g
            qh = q_ref[:, h * HEAD_DIM:(h + 1) * HEAD_DIM].astype(BF16)
            s = _head_logits(qh, kk, bias_ref[h], n == 0)
            sink = sink_ref[0, h]
            m = jnp.maximum(jnp.max(s, axis=1, keepdims=True), sink)
            p = jnp.exp(s - m)
            den = jnp.sum(p, axis=1, keepdims=True) + jnp.exp(sink - m)
            p = p / den
            outs.append(jnp.dot(p.astype(BF16), vv, preferred_element_type=F32))
            lses.append(m + jnp.log(den))
    o_ref[...] = jnp.concatenate(outs, axis=1).astype(o_ref.dtype)
    lse_ref[...] = jnp.concatenate(lses, axis=1)


def _attn_specs(nb):
    prev = lambda n: jnp.maximum(jnp.minimum(n, nb - 1) - 1, 0)
    cur = lambda n: jnp.minimum(n, nb - 1)
    return [pl.BlockSpec((BLOCK, D_ATTN), lambda n: (cur(n), C_Q // D_ATTN)),
            pl.BlockSpec((BLOCK, D_KV), lambda n: (prev(n), C_K // D_KV)),
            pl.BlockSpec((BLOCK, D_KV), lambda n: (cur(n), C_K // D_KV)),
            pl.BlockSpec((BLOCK, D_KV), lambda n: (prev(n), C_V // D_KV)),
            pl.BlockSpec((BLOCK, D_KV), lambda n: (cur(n), C_V // D_KV))]


def _attn_forward(proj, sinks, bias):
    t = proj.shape[0]
    nb = t // BLOCK
    return pl.pallas_call(
        functools.partial(_attn_fwd_body), name="attn_fwd", grid=(nb,),
        in_specs=[pl.BlockSpec(memory_space=pltpu.SMEM)] + _attn_specs(nb) + [_whole(bias.shape)],
        out_specs=[pl.BlockSpec((BLOCK, D_ATTN), lambda n: (n, 0)),
                   pl.BlockSpec((BLOCK, N_Q_HEADS), lambda n: (n, 0))],
        out_shape=[jax.ShapeDtypeStruct((t, D_ATTN), BF16), jax.ShapeDtypeStruct((t, N_Q_HEADS), F32)],
        compiler_params=_params(("parallel",)),
    )(sinks, proj, proj, proj, proj, proj, bias)


def _attn_bwd_body(sink_ref, q_ref, kp_ref, kc_ref, vp_ref, vc_ref, bias_ref, do_ref, o_ref, lse_ref, dproj_in,
                   dq_ref, dkv_ref, gbias_ref, gsink_ref, carry_ref, *, nb):
    del dproj_in
    n = pl.program_id(0)

    @pl.when(n == 0)
    def _():
        gbias_ref[...] = jnp.zeros_like(gbias_ref)
        gsink_ref[...] = jnp.zeros_like(gsink_ref)
        carry_ref[...] = jnp.zeros_like(carry_ref)

    @pl.when(n < nb)
    def _():
        lane = lax.broadcasted_iota(jnp.int32, (1, 128), 1)
        dqs, dks, dvs = [], [], []
        gsink = jnp.zeros((1, 128), F32)
        for kv in range(N_KV_HEADS):
            cs = slice(kv * HEAD_DIM, (kv + 1) * HEAD_DIM)
            kk = jnp.concatenate([kp_ref[:, cs], kc_ref[:, cs]], axis=0).astype(BF16)
            vv = jnp.concatenate([vp_ref[:, cs], vc_ref[:, cs]], axis=0).astype(BF16)
            dk = jnp.zeros((2 * BLOCK, HEAD_DIM), F32)
            dv = jnp.zeros((2 * BLOCK, HEAD_DIM), F32)
            for g in range(Q_PER_KV):
                h = kv * Q_PER_KV + g
                hs = slice(h * HEAD_DIM, (h + 1) * HEAD_DIM)
                qh = q_ref[:, hs].astype(BF16)
                s = _head_logits(qh, kk, bias_ref[h], n == 0)
                lse = lse_ref[:, h:h + 1]
                p = jnp.exp(s - lse)
                do = do_ref[:, hs].astype(F32)
                delta = jnp.sum(do * o_ref[:, hs].astype(F32), axis=1, keepdims=True)
                dob = do.astype(BF16)
                dp = lax.dot_general(dob, vv, _DIMS["nt"], preferred_element_type=F32)
                dl = p * (dp - delta)
                gbias_ref[h] += dl
                psink = jnp.exp(sink_ref[0, h] - lse)
                gsink = gsink + jnp.where(lane == h, -jnp.sum(psink * delta), 0.0)
                dlb = dl.astype(BF16)
                dqs.append(jnp.dot(dlb, kk, preferred_element_type=F32) * ATTN_SCALE)
                dk = dk + lax.dot_general(dlb, qh, _DIMS["tn"], preferred_element_type=F32) * ATTN_SCALE
                dv = dv + lax.dot_general(p.astype(BF16), dob, _DIMS["tn"], preferred_element_type=F32)
            dks.append(dk)
            dvs.append(dv)
        dq_ref[...] = jnp.concatenate(dqs, axis=1).astype(dq_ref.dtype)
        gsink_ref[...] += gsink
        dkv = jnp.concatenate(dks + dvs, axis=1)
        dkv_ref[...] = (carry_ref[...] + dkv[:BLOCK]).astype(dkv_ref.dtype)
        carry_ref[...] = dkv[BLOCK:]

    @pl.when(n == nb)
    def _():
        dkv_ref[...] = carry_ref[...].astype(dkv_ref.dtype)


def _attn_backward(proj, sinks, bias, d_attn, attn, lse, dproj):
    t = proj.shape[0]
    nb = t // BLOCK
    cur = lambda n: jnp.minimum(n, nb - 1)
    return pl.pallas_call(
        functools.partial(_attn_bwd_body, nb=nb), name="attn_bwd", grid=(nb + 1,),
        in_specs=[pl.BlockSpec(memory_space=pltpu.SMEM)] + _attn_specs(nb) + [
            _whole(bias.shape),
            pl.BlockSpec((BLOCK, D_ATTN), lambda n: (cur(n), 0)),
            pl.BlockSpec((BLOCK, D_ATTN), lambda n: (cur(n), 0)),
            pl.BlockSpec((BLOCK, N_Q_HEADS), lambda n: (cur(n), 0)),
            pl.BlockSpec(memory_space=pl.ANY)],
        out_specs=[pl.BlockSpec((BLOCK, D_ATTN), lambda n: (cur(n), C_Q // D_ATTN)),
                   pl.BlockSpec((BLOCK, 2 * D_KV), lambda n: (jnp.maximum(n - 1, 0), 0)),
                   _whole(bias.shape), _whole((1, 128))],
        out_shape=[jax.ShapeDtypeStruct(dproj.shape, dproj.dtype), jax.ShapeDtypeStruct((t, 2 * D_KV), dproj.dtype),
                   jax.ShapeDtypeStruct(bias.shape, F32), jax.ShapeDtypeStruct((1, 128), F32)],
        scratch_shapes=[pltpu.VMEM((BLOCK, 2 * D_KV), F32)],
        input_output_aliases={10: 0},
        compiler_params=_params(("arbitrary",)),
    )(sinks, proj, proj, proj, proj, proj, bias, d_attn, attn, lse, dproj)


ROWS = 256


def _rowwise(body, name, t, ins, outs, aliases=None):
    rows = min(ROWS, t)

    def col_spec(w, c0):
        if c0 == "T":
            return pl.BlockSpec((w, rows), lambda i: (0, i))
        if c0 % w == 0:
            return pl.BlockSpec((rows, w), lambda i: (i, c0 // w))
        return pl.BlockSpec((pl.Element(rows), pl.Element(w)), lambda i: (i * rows, c0))

    in_specs, args = [], []
    for a, w, c0 in ins:
        args.append(a)
        if w is None:
            in_specs.append(pl.BlockSpec(memory_space=pl.ANY) if c0 == "any" else _whole(a.shape))
        else:
            in_specs.append(col_spec(w, c0))
    out_specs, out_shape = [], []
    for shape, dtype, w, c0 in outs:
        out_shape.append(jax.ShapeDtypeStruct(shape, dtype))
        out_specs.append(_whole(shape) if w is None else col_spec(w, c0))
    accum = any(o[2] is None for o in outs)
    return pl.pallas_call(
        body, name=name, grid=(t // rows,), in_specs=in_specs, out_specs=out_specs, out_shape=out_shape,
        input_output_aliases=aliases or {},
        compiler_params=_params(("arbitrary",) if accum else ("parallel",)),
    )(*args)


def _f32(ref, *idx):
    return (ref[idx] if idx else ref[...]).astype(F32)


def _store_with_transpose(val, ref, t_ref):
    val = val.astype(ref.dtype)
    ref[...] = val
    t_ref[...] = val.T


def _ssm_gate_fwd_body(glu_ref, z_ref, h_ref, ht_ref):
    a, b = _f32(glu_ref, slice(None), slice(0, D_SSM)), _f32(glu_ref, slice(None), slice(D_SSM, None))
    _store_with_transpose((a * _sigmoid(b)) * _silu_and_grad(_f32(z_ref))[0], h_ref, ht_ref)


def _ssm_gate_bwd_body(dh_ref, glu_ref, z_ref, dproj_in, dglu_ref, dz_ref):
    del dproj_in
    a, b = _f32(glu_ref, slice(None), slice(0, D_SSM)), _f32(glu_ref, slice(None), slice(D_SSM, None))
    sb = _sigmoid(b)
    silu, dsilu = _silu_and_grad(_f32(z_ref))
    dh = _f32(dh_ref)
    dg = dh * silu
    dz_ref[...] = (dh * (a * sb) * dsilu).astype(dz_ref.dtype)
    dglu_ref[:, :D_SSM] = (dg * sb).astype(dglu_ref.dtype)
    dglu_ref[:, D_SSM:] = (dg * a * sb * (1.0 - sb)).astype(dglu_ref.dtype)


def _attn_gate_fwd_body(attn_ref, z_ref, h_ref, ht_ref):
    _store_with_transpose(_f32(attn_ref) * _silu_and_grad(_f32(z_ref))[0], h_ref, ht_ref)


def _attn_gate_bwd_body(dh_ref, attn_ref, z_ref, dproj_in, dattn_ref, dz_ref):
    del dproj_in
    silu, dsilu = _silu_and_grad(_f32(z_ref))
    dh = _f32(dh_ref)
    dattn_ref[...] = (dh * silu).astype(dattn_ref.dtype)
    dz_ref[...] = (dh * _f32(attn_ref) * dsilu).astype(dz_ref.dtype)


def _merge_fwd_body(bs_ref, ba_ref, gl_ref, m_ref, mt_ref):
    gs = _sigmoid(_f32(gl_ref, slice(None), slice(0, D_MODEL)))
    ga = _sigmoid(_f32(gl_ref, slice(None), slice(D_MODEL, None)))
    _store_with_transpose(gs * _f32(bs_ref) + ga * _f32(ba_ref), m_ref, mt_ref)


def _merge_bwd_body(dm_ref, bs_ref, ba_ref, gl_ref, dbs_ref, dba_ref, dgl_ref):
    gs = _sigmoid(_f32(gl_ref, slice(None), slice(0, D_MODEL)))
    ga = _sigmoid(_f32(gl_ref, slice(None), slice(D_MODEL, None)))
    dm = _f32(dm_ref)
    dbs_ref[...] = (dm * gs).astype(dbs_ref.dtype)
    dba_ref[...] = (dm * ga).astype(dba_ref.dtype)
    dgl_ref[:, :D_MODEL] = (dm * _f32(bs_ref) * gs * (1.0 - gs)).astype(dgl_ref.dtype)
    dgl_ref[:, D_MODEL:] = (dm * _f32(ba_ref) * ga * (1.0 - ga)).astype(dgl_ref.dtype)


def _ln_loss_body(x_ref, o_ref, tgt_ref, gain_ref, bias_ref, dr_ref, loss_ref, dgain_ref, dbias_ref):
    @pl.when(pl.program_id(0) == 0)
    def _():
        loss_ref[...] = jnp.zeros_like(loss_ref)
        dgain_ref[...] = jnp.zeros_like(dgain_ref)
        dbias_ref[...] = jnp.zeros_like(dbias_ref)

    r = ALPHA * x_ref[...] + o_ref[...].astype(F32)
    mu = jnp.mean(r, axis=1, keepdims=True)
    rc = r - mu
    var = jnp.mean(rc * rc, axis=1, keepdims=True)
    rstd = lax.rsqrt(var + LN_EPS)
    xhat = rc * rstd
    gain = gain_ref[...]
    err = xhat * gain + bias_ref[...] - tgt_ref[...]
    loss_ref[...] += 0.5 * jnp.sum(jnp.mean(err * err, axis=1, keepdims=True), axis=0, keepdims=True)
    dy = err * (1.0 / D_MODEL)
    dgain_ref[...] += jnp.sum(dy * xhat, axis=0, keepdims=True)
    dbias_ref[...] += jnp.sum(dy, axis=0, keepdims=True)
    dxhat = dy * gain
    m1 = jnp.mean(dxhat, axis=1, keepdims=True)
    m2 = jnp.mean(dxhat * xhat, axis=1, keepdims=True)
    dr_ref[...] = (rstd * (dxhat - m1 - xhat * m2)).astype(dr_ref.dtype)


def _place_body(piece_ref, dproj_in, o_ref):
    del dproj_in
    o_ref[...] = piece_ref[...]


def _adamw_update(w_ref, m_ref, v_ref, g, g_ref, d_ref, nm_ref, nv_ref):
    m = ADAM_B1 * m_ref[...] + (1.0 - ADAM_B1) * g
    v = ADAM_B2 * v_ref[...] + (1.0 - ADAM_B2) * (g * g)
    m_hat = m / (1.0 - ADAM_B1 ** ADAM_STEP)
    v_hat = v / (1.0 - ADAM_B2 ** ADAM_STEP)
    g_ref[...] = g
    d_ref[...] = -ADAM_LR * (m_hat / (jnp.sqrt(v_hat) + ADAM_EPS) + ADAM_WD * w_ref[...])
    nm_ref[...] = m
    nv_ref[...] = v


def _adamw_body(*refs, n_parts):
    w_ref, m_ref, v_ref = refs[:3]
    parts = refs[3:3 + n_parts]
    g = parts[0][...].astype(F32)
    for p in parts[1:]:
        g = g + p[...].astype(F32)
    _adamw_update(w_ref, m_ref, v_ref, g, *refs[3 + n_parts:])


def _adamw_shard_body(c_ref, w_ref, m_ref, v_ref, mine_ref, other_ref, g_ref, d_ref, nm_ref, nv_ref, *, nth):
    in_mine = pl.program_id(0) // nth == c_ref[0]
    g = jnp.where(in_mine, mine_ref[...], other_ref[...])
    _adamw_update(w_ref, m_ref, v_ref, g, g_ref, d_ref, nm_ref, nv_ref)


def _adamw_shard(w, m, v, halves, core, *, name, rows):
    shape = w.shape
    w2, m2, v2 = (a.reshape(-1, shape[-1]) for a in (w, m, v))
    r, c = w2.shape
    nth = r // 2 // rows
    assert 2 * nth * rows == r and halves[0].shape == (r // 2, c)
    spec = pl.BlockSpec((rows, c), lambda i, cr: (i, 0))
    half_spec = pl.BlockSpec((rows, c), lambda i, cr: (i % nth, 0))
    outs = pl.pallas_call(
        functools.partial(_adamw_shard_body, nth=nth), name=name,
        grid_spec=pltpu.PrefetchScalarGridSpec(
            num_scalar_prefetch=1, grid=(r // rows,),
            in_specs=[spec] * 3 + [half_spec] * 2, out_specs=[spec] * 4),
        out_shape=[jax.ShapeDtypeStruct((r, c), F32)] * 4,
        compiler_params=_params(("parallel",)),
    )(core, w2, m2, v2, *halves)
    return tuple(o.reshape(shape) for o in outs)


def _adamw(w, m, v, parts, *, name, rows=256):
    shape = w.shape
    w2, m2, v2 = (a.reshape(-1, shape[-1]) for a in (w, m, v))
    parts = [p.reshape(w2.shape) for p in parts]
    r, c = w2.shape
    rows = rows if r % rows == 0 else r
    spec = pl.BlockSpec((rows, c), lambda i: (i, 0))
    outs = pl.pallas_call(
        functools.partial(_adamw_body, n_parts=len(parts)), name=name, grid=(r // rows,),
        in_specs=[spec] * (3 + len(parts)), out_specs=[spec] * 4,
        out_shape=[jax.ShapeDtypeStruct((r, c), F32)] * 4,
        compiler_params=_params(("parallel",)),
    )(w2, m2, v2, *parts)
    return tuple(o.reshape(shape) for o in outs)


BIG = ("w_in", "w_glu", "w_branch_ssm", "w_branch_attn", "w_out")
SHARD_AXIS = dict(w_in=1, w_glu=1, w_branch_ssm=1, w_branch_attn=1, w_out=0)
HBM = pl.BlockSpec(memory_space=pl.ANY)


def _position():
    x, y, c = lax.axis_index("x"), lax.axis_index("y"), lax.axis_index("c")
    other_chips = [(1 - x, y), (x, 1 - y), (1 - x, 1 - y)]
    return x, y, c, other_chips


def _window(ref, axis, shard, n_shards, half=None):
    rows, cols = ref.shape[-2:]
    sel = [slice(None), slice(None)]
    size = ref.shape[-2 + axis] // n_shards
    sel[axis] = pl.ds(pl.multiple_of(shard * size, 128), size)
    if half is not None:
        hsize = ref.shape[-1 - axis] // 2
        sel[1 - axis] = pl.ds(pl.multiple_of(half * hsize, 128), hsize)
    return ref.at[tuple(sel)]


def _half(ref, axis, half):
    hsize = ref.shape[-1 - axis] // 2
    sel = [slice(None), slice(None)]
    sel[1 - axis] = pl.ds(pl.multiple_of(half * hsize, 128), hsize)
    return ref.at[tuple(sel)]


def _remote(src, dst, send_sem, recv_sem, device):
    return pltpu.make_async_remote_copy(src_ref=src, dst_ref=dst, send_sem=send_sem, recv_sem=recv_sem,
                                        device_id=device, device_id_type=MESH)


def _full_shapes(shards, names):
    out = []
    for k in names:
        s = list(shards[k].shape)
        s[SHARD_AXIS[k]] *= N_CHIPS
        out.append(jax.ShapeDtypeStruct(tuple(s), shards[k].dtype))
    return out


def _gather_exchange(shards, names):
    axes = tuple(SHARD_AXIS[k] for k in names)
    n = len(names)

    def copies(ins, outs, sems):
        send_sems, recv_sems, local_sems = sems
        x, y, c, chips = _position()
        me = 2 * x + y
        remote, local = [], []
        for w, ax in enumerate(axes):
            for r, (px, py) in enumerate(chips):
                remote.append(_remote(_half(ins[w], ax, c), _window(outs[w], ax, me, N_CHIPS, c),
                                      send_sems.at[3 * w + r], recv_sems.at[3 * w + r], (px, py, c)))
            local.append(pltpu.make_async_copy(ins[w], _window(outs[w], ax, me, N_CHIPS), local_sems.at[w]))
        return remote, local

    return _Exchange([shards[k] for k in names], _full_shapes(shards, names), (3 * n, 3 * n, n), copies)


SEM = pl.BlockSpec(memory_space=pltpu.SEMAPHORE)


def _gather_start_body(shard_ref, land_ref, send_sems, recv_sems, shard_thru, land_thru, token, *, axis):
    del shard_thru, land_thru
    x, y, c, chips = _position()
    me = 2 * x + y
    for r, (px, py) in enumerate(chips):
        _remote(_half(shard_ref, axis, c), _window(land_ref, axis, me, N_CHIPS, c),
                send_sems.at[r], recv_sems.at[r], (px, py, c)).start()
    token[...] = jnp.zeros_like(token)


def _gather_start(shard, name):
    axis = SHARD_AXIS[name]
    full = _full_shapes({name: shard}, (name,))[0]
    return pl.pallas_call(
        functools.partial(_gather_start_body, axis=axis), name="gather_start_" + name,
        out_shape=(pltpu.SemaphoreType.DMA((3,)), pltpu.SemaphoreType.DMA((3,)), pltpu.HBM(shard.shape, shard.dtype),
                   pltpu.HBM(full.shape, full.dtype), jax.ShapeDtypeStruct((8, 128), F32)),
        in_specs=(HBM, HBM), out_specs=(SEM, SEM, HBM, HBM, pl.BlockSpec(memory_space=pltpu.VMEM)),
        input_output_aliases={0: 2, 1: 3},
        compiler_params=pltpu.CompilerParams(has_side_effects=pltpu.SideEffectType.DATAFLOW_SIDE_EFFECTING),
    )(pltpu.with_memory_space_constraint(shard, pltpu.HBM),
      pltpu.with_memory_space_constraint(lax.empty(full.shape, full.dtype), pltpu.HBM))


def _gather_wait_body(shard_ref, land_ref, send_sems, recv_sems, after_ref, shard_dead, landed, *, axis):
    del after_ref, shard_dead, landed
    x, y, c, chips = _position()
    for r, (px, py) in enumerate(chips):
        cp = _remote(_half(shard_ref, axis, c), _window(land_ref, axis, 2 * px + py, N_CHIPS, c),
                     send_sems.at[r], recv_sems.at[r], (px, py, c))
        cp.wait_send()
        cp.wait_recv()


def _gather_wait(send_sems, recv_sems, shard_thru, land_thru, after, name):
    return pl.pallas_call(
        functools.partial(_gather_wait_body, axis=SHARD_AXIS[name]), name="gather_wait_" + name,
        out_shape=(pltpu.HBM(shard_thru.shape, shard_thru.dtype), pltpu.HBM(land_thru.shape, land_thru.dtype)),
        in_specs=(HBM, HBM, SEM, SEM, HBM), out_specs=(HBM, HBM), input_output_aliases={0: 0, 1: 1},
        compiler_params=pltpu.CompilerParams(has_side_effects=pltpu.SideEffectType.DATAFLOW_SIDE_EFFECTING),
    )(shard_thru, land_thru, send_sems, recv_sems, after)[1]


def _pass_on_body(*refs, axes, n_own):
    n = len(axes)
    ins, own = refs[:n], refs[n:n + n_own]
    refs = refs[:n] + refs[n + n_own:]
    outs = refs[n:2 * n]
    sbuf, rbuf = refs[2 * n:5 * n], refs[5 * n:8 * n]
    send_sems, recv_sems, load_sems, store_sems, own_sems = refs[8 * n:]
    x, y, c, chips = _position()
    placed = [pltpu.make_async_copy(own[w], _window(outs[w], axes[w], 2 * x + y, N_CHIPS), own_sems.at[w])
              for w in range(n_own)]
    for cp in placed:
        cp.start()
    region = lambda ref, w, r, half: _window(ref, axes[w], 2 * chips[r][0] + chips[r][1], N_CHIPS, half)
    pairs = [(w, r) for w in range(n) for r in range(3)]
    loads = [pltpu.make_async_copy(region(ins[w], w, r, c), sbuf[3 * w + r], load_sems.at[3 * w + r]) for w, r in pairs]
    for cp in loads:
        cp.start()
    sends = []
    for i, cp in enumerate(loads):
        cp.wait()
        sends.append(_remote(sbuf[i], rbuf[i], send_sems.at[i], recv_sems.at[i], (x, y, 1 - c)))
        sends[-1].start()
    stores = []
    for i, (w, r) in enumerate(pairs):
        sends[i].wait_recv()
        stores.append(pltpu.make_async_copy(rbuf[i], region(outs[w], w, r, 1 - c), store_sems.at[i]))
        stores[-1].start()
    for cp in sends:
        cp.wait_send()
    for cp in stores + placed:
        cp.wait()


def _pass_on(fulls, names, own=()):
    axes = tuple(SHARD_AXIS[k] for k in names)
    n = len(names)
    assert len(own) in (0, n)
    bufs = []
    for a, ax in zip(fulls, axes):
        s = list(a.shape)
        s[ax] //= N_CHIPS
        s[1 - ax] //= 2
        bufs += [pltpu.VMEM(tuple(s), a.dtype)] * 3
    return pl.pallas_call(
        functools.partial(_pass_on_body, axes=axes, n_own=len(own)), name="pass_on_" + names[0],
        in_specs=[HBM] * (n + len(own)), out_specs=[HBM] * n,
        out_shape=[jax.ShapeDtypeStruct(a.shape, a.dtype) for a in fulls],
        scratch_shapes=bufs + bufs + [pltpu.SemaphoreType.DMA((3 * n,))] * 4 + [pltpu.SemaphoreType.DMA((n,))],
        input_output_aliases={i: i for i in range(n)},
        compiler_params=pltpu.CompilerParams(vmem_limit_bytes=VMEM_LIMIT),
    )(*fulls, *own)


def _chip_exchange(pairs, names):
    axes = tuple(SHARD_AXIS[k] for k in names)
    n = len(names)
    out_shape = []
    for p, ax in zip(pairs, axes):
        s = list(p.shape)
        s[ax] //= N_CHIPS
        out_shape.append(jax.ShapeDtypeStruct((4, *s), p.dtype))

    def copies(ins, outs, sems):
        send_sems, recv_sems, local_sems = sems
        x, y, c, chips = _position()
        me = 2 * x + y
        remote, local = [], []
        for w, ax in enumerate(axes):
            for r, (px, py) in enumerate(chips):
                remote.append(_remote(_window(ins[w], ax, 2 * px + py, N_CHIPS), outs[w].at[r],
                                      send_sems.at[3 * w + r], recv_sems.at[3 * w + r], (px, py, c)))
            local.append(pltpu.make_async_copy(_window(ins[w], ax, me, N_CHIPS), outs[w].at[3], local_sems.at[w]))
        return remote, local

    return _Exchange(pairs, out_shape, (3 * n, 3 * n, n), copies)


def _half_tile(n, h, nt, axis):
    return h * nt + n if axis == 1 else 2 * n + h


def _sibling_stream(n, nt, stage, recv, send_sems, recv_sems, credit, produce, consume):
    x, y, c, _ = _position()
    sibling = (x, y, 1 - c)

    def copy(slot):
        return _remote(stage.at[slot], recv.at[slot], send_sems.at[slot], recv_sems.at[slot], sibling)

    @pl.when(n < nt)
    def _():
        slot = n % 2

        @pl.when(n >= 2)
        def _():
            copy(slot).wait_send()
            pl.semaphore_wait(credit, 1)

        stage[slot] = produce().astype(stage.dtype)
        copy(slot).start()

    @pl.when(n >= 1)
    def _():
        slot = (n - 1) % 2
        copy(slot).wait_recv()
        consume(recv[slot])

        @pl.when(n - 1 < nt - 2)
        def _():
            pl.semaphore_signal(credit, 1, device_id=sibling, device_id_type=MESH)

    @pl.when(n == nt)
    def _():
        for slot in range(min(2, nt)):
            copy(slot).wait_send()


def _pair_reduce_body(c_ref, mine_ref, theirs_ref, out_ref, stage, recv, send_sems, recv_sems, credit, *, nt):
    del c_ref

    def consume(got):
        out_ref[...] = (mine_ref[...] + got.astype(F32)).astype(out_ref.dtype)

    _sibling_stream(pl.program_id(0), nt, stage, recv, send_sems, recv_sems, credit,
                    lambda: theirs_ref[...], consume)


def _pair_reduce(grad, core, axis, *, name, rows):
    r, c = grad.shape
    nt = r // 2 // rows
    assert nt * rows * 2 == r and (axis == 1 or rows == r // (2 * N_CHIPS))
    tile = lambda n, h: _half_tile(n, h, nt, axis)
    return pl.pallas_call(
        functools.partial(_pair_reduce_body, nt=nt), name=name,
        grid_spec=pltpu.PrefetchScalarGridSpec(
            num_scalar_prefetch=1, grid=(nt + 1,),
            in_specs=[pl.BlockSpec((rows, c), lambda n, cr: (tile(jnp.maximum(n - 1, 0), cr[0]), 0)),
                      pl.BlockSpec((rows, c), lambda n, cr: (tile(jnp.minimum(n, nt - 1), 1 - cr[0]), 0))],
            out_specs=pl.BlockSpec((rows, c), lambda n, cr: (jnp.maximum(n - 1, 0), 0)),
            scratch_shapes=[pltpu.VMEM((2, rows, c), BF16), pltpu.VMEM((2, rows, c), BF16),
                            pltpu.SemaphoreType.DMA((2,)), pltpu.SemaphoreType.DMA((2,)),
                            pltpu.SemaphoreType.REGULAR]),
        out_shape=jax.ShapeDtypeStruct((r // 2, c), BF16),
        compiler_params=_params(("arbitrary",)),
    )(core, grad, grad)


def _chip_add_share_body(c_ref, s0, s1, s2, s3, mine_ref, other_ref, stage, recv, send_sems, recv_sems, credit, *, nt):
    del c_ref

    def produce():
        total = s3[...].astype(F32) + s0[...].astype(F32) + s1[...].astype(F32) + s2[...].astype(F32)
        mine_ref[...] = total
        return total

    def consume(got):
        other_ref[...] = got

    _sibling_stream(pl.program_id(0), nt, stage, recv, send_sems, recv_sems, credit, produce, consume)


def _chip_add_share(slots, core, *, name, rows):
    _, r, c = slots.shape
    nt = r // rows
    assert nt * rows == r
    send = lambda j: pl.BlockSpec((None, rows, c), lambda n, cr: (j, jnp.minimum(n, nt - 1), 0))
    return pl.pallas_call(
        functools.partial(_chip_add_share_body, nt=nt), name=name,
        grid_spec=pltpu.PrefetchScalarGridSpec(
            num_scalar_prefetch=1, grid=(nt + 1,),
            in_specs=[send(j) for j in range(4)],
            out_specs=[pl.BlockSpec((rows, c), lambda n, cr: (jnp.minimum(n, nt - 1), 0)),
                       pl.BlockSpec((rows, c), lambda n, cr: (jnp.maximum(n - 1, 0), 0))],
            scratch_shapes=[pltpu.VMEM((2, rows, c), F32), pltpu.VMEM((2, rows, c), F32),
                            pltpu.SemaphoreType.DMA((2,)), pltpu.SemaphoreType.DMA((2,)),
                            pltpu.SemaphoreType.REGULAR]),
        out_shape=[jax.ShapeDtypeStruct((r, c), F32)] * 2,
        compiler_params=_params(("arbitrary",)),
    )(core, slots, slots, slots, slots)


PAIR_ROWS = dict(w_in=64, w_glu=128, w_branch_ssm=128, w_branch_attn=128, w_out=256)
SHARE_ROWS = dict(w_in=128, w_glu=128, w_branch_ssm=128, w_branch_attn=128, w_out=64)


def _pair_sums(grads, names, core):
    return [_pair_reduce(grads[k], core, SHARD_AXIS[k], name="pair_reduce_" + k, rows=PAIR_ROWS[k]) for k in names]


def _shard_halves(slots, names, core):
    return {k: _chip_add_share(s, core, name="chip_add_share_" + k, rows=SHARE_ROWS[k]) for k, s in zip(names, slots)}


SMALL = (("ssm_lambda_re", (1, 64, 64)), ("ssm_lambda_im", (1, 64, 64)), ("ssm_b_re", (1, 64, 64, 16)),
         ("ssm_b_im", (1, 64, 64, 16)), ("ssm_c_re", (1, 64, 16, 64)), ("ssm_c_im", (1, 64, 16, 64)),
         ("ssm_d", (1, 1024)), ("ssm_log_step", (1, 64)), ("attn_sinks", (1, 16)), ("rel_bias_table", (32, 16)),
         ("ln_gain", (1, 2048)), ("ln_bias", (1, 2048)))
SMALL_SIZE = sum(int(np.prod(s)) for _, s in SMALL)
PACK_ROWS = -(-(SMALL_SIZE + 1) // (8 * 128)) * 8


def _pack(values, extra=None):
    flat = [values[k].reshape(-1).astype(F32) for k, _ in SMALL]
    flat.append(jnp.zeros((1,), F32) if extra is None else extra.reshape(1))
    flat.append(jnp.zeros((PACK_ROWS * 128 - SMALL_SIZE - 1,), F32))
    return jnp.concatenate(flat).reshape(PACK_ROWS, 128)


def _unpack(packed):
    flat = packed.reshape(-1)
    out, off = {}, 0
    for k, s in SMALL:
        size = int(np.prod(s))
        out[k] = flat[off:off + size].reshape(s)
        off += size
    return out, flat[off]


def _small_exchange(packed):
    def copies(ins, outs, sems):
        send_sems, recv_sems, local_sems = sems
        x, y, c, _ = _position()
        remote = []
        for r in range(1, 8):
            peer = tuple(1 - v if (r >> s) & 1 else v for v, s in ((x, 2), (y, 1), (c, 0)))
            remote.append(_remote(ins[0], outs[0].at[r], send_sems.at[r - 1], recv_sems.at[r - 1], peer))
        return remote, [pltpu.make_async_copy(ins[0], outs[0].at[0], local_sems.at[0])]

    return _Exchange([packed], [jax.ShapeDtypeStruct((8, *packed.shape), F32)], (7, 7, 1), copies)


def _merge_exchanges(a, b):
    na, ma, sa = len(a.ins), len(a.out_shape), len(a.n_sems)

    def copies(ins, outs, sems):
        ra, la = a.copies(ins[:na], outs[:ma], sems[:sa])
        rb, lb = b.copies(ins[na:], outs[ma:], sems[sa:])
        return ra + rb, la + lb

    return _Exchange(a.ins + b.ins, a.out_shape + b.out_shape, a.n_sems + b.n_sems, copies)


def _small_sum_body(slots_ref, o_ref):
    x, y, c, _ = _position()
    me = 4 * x + 2 * y + c
    acc = slots_ref[me]
    for s in range(1, 8):
        acc = acc + slots_ref[jnp.bitwise_xor(me, s)]
    o_ref[...] = acc


def _small_sum(slots):
    vm = pl.BlockSpec(memory_space=pltpu.VMEM)
    return pl.pallas_call(
        functools.partial(_small_sum_body), name="small_sum",
        in_specs=[vm], out_specs=vm, out_shape=jax.ShapeDtypeStruct(slots.shape[1:], F32),
        compiler_params=pltpu.CompilerParams(vmem_limit_bytes=VMEM_LIMIT),
    )(slots)


LATE = BIG[1:]


def _local_step(x, target, shards, core, chip, lam_re, lam_im, b_re, b_im, c_re, c_im, d_skip,
                log_step, sinks, rel_bias_table, ln_gain, ln_bias):
    t = x.shape[0]
    send_sems, recv_sems, shard_thru, land_thru, token = _gather_start(shards["w_in"], "w_in")
    xb, xbt = _cast_and_transpose(x, token, name="cast_x")
    bucket = jnp.asarray(_bucket_table())
    p_re, p_im, b_bd, c_bd = _ssm_prepare(lam_re, lam_im, b_re, b_im, c_re, c_im, log_step)
    bias = _bias_build(rel_bias_table, bucket)

    act = dict(out_dtype=BF16)
    proj = _proj_shards(xb, shards["w_in"], chip, (0,), name="proj_own")
    landed = _gather_wait(send_sems, recv_sems, shard_thru, land_thru, proj, "w_in")
    (w_in,) = _pass_on([landed], BIG[:1], own=[shards["w_in"]])
    proj, landed = _proj_shards(xb, w_in, chip, (1, 2, 3), name="proj", base=proj,
                                exchange=_gather_exchange(shards, LATE))
    w_glu, w_bs, w_ba, w_out = _pass_on(list(landed), LATE)
    y_ssm, g_in, g_in_t, car_re, car_im = _ssm_forward(proj, p_re, p_im, b_bd, c_bd, d_skip)
    glu = _matmul(g_in, w_glu, "nn", name="glu", tm=1024, tn=2048, **act)
    h_ssm, h_ssm_t = _rowwise(functools.partial(_ssm_gate_fwd_body), "ssm_gate_fwd", t,
                              [(glu, 2 * D_SSM, 0), (proj, D_SSM, C_ZS)],
                              [((t, D_SSM), BF16, D_SSM, 0), ((D_SSM, t), BF16, D_SSM, "T")])
    attn, lse = _attn_forward(proj, sinks, bias)
    h_attn, h_attn_t = _rowwise(functools.partial(_attn_gate_fwd_body), "attn_gate_fwd", t,
                                [(attn, D_ATTN, 0), (proj, D_ATTN, C_ZA)],
                                [((t, D_ATTN), BF16, D_ATTN, 0), ((D_ATTN, t), BF16, D_ATTN, "T")])
    bs = _matmul(h_ssm, w_bs, "nn", name="branch_ssm", tm=1024, tn=2048, **act)
    ba = _matmul(h_attn, w_ba, "nn", name="branch_attn", tm=1024, tn=2048, **act)
    gl_in = (proj, 2 * D_MODEL, C_GL)
    merged, merged_t = _rowwise(functools.partial(_merge_fwd_body), "merge_fwd", t,
                                [(bs, D_MODEL, 0), (ba, D_MODEL, 0), gl_in],
                                [((t, D_MODEL), BF16, D_MODEL, 0), ((D_MODEL, t), BF16, D_MODEL, "T")])
    out = _matmul(merged, w_out, "nn", name="out_proj", tm=1024, tn=1024, **act)
    drb, loss, g_gain, g_bias = _rowwise(
        functools.partial(_ln_loss_body), "ln_loss", t,
        [(x, D_MODEL, 0), (out, D_MODEL, 0), (target, D_MODEL, 0), (ln_gain, None, 0), (ln_bias, None, 0)],
        [((t, D_MODEL), BF16, D_MODEL, 0), ((1, 1), F32, None, 0),
         ((1, D_MODEL), F32, None, 0), ((1, D_MODEL), F32, None, 0)])

    g_w_out = _matmul(merged_t, drb, "nn", name="g_w_out", tm=512, tn=512)
    d_merged = _matmul(drb, w_out, "nt", name="d_merged", tm=1024, tn=1024, **act)
    d_bs, d_ba, dproj = _rowwise(
        functools.partial(_merge_bwd_body), "merge_bwd", t,
        [(d_merged, D_MODEL, 0), (bs, D_MODEL, 0), (ba, D_MODEL, 0), gl_in],
        [((t, D_MODEL), BF16, D_MODEL, 0), ((t, D_MODEL), BF16, D_MODEL, 0),
         ((t, D_IN), BF16, 2 * D_MODEL, C_GL)])
    g_w_bs = _matmul(h_ssm_t, d_bs, "nn", name="g_w_branch_ssm", tm=512, tn=512)
    d_h_ssm = _matmul(d_bs, w_bs, "nt", name="d_h_ssm", tm=1024, tn=1024, **act)
    g_w_ba = _matmul(h_attn_t, d_ba, "nn", name="g_w_branch_attn", tm=512, tn=512)
    d_h_attn = _matmul(d_ba, w_ba, "nt", name="d_h_attn", tm=1024, tn=1024, **act)

    d_attn, dproj = _rowwise(
        functools.partial(_attn_gate_bwd_body), "attn_gate_bwd", t,
        [(d_h_attn, D_ATTN, 0), (attn, D_ATTN, 0), (proj, D_ATTN, C_ZA), (dproj, None, "any")],
        [((t, D_ATTN), BF16, D_ATTN, 0), ((t, D_IN), BF16, D_ATTN, C_ZA)], aliases={3: 1})
    dproj, d_kv, g_bias_full, g_sinks = _attn_backward(proj, sinks, bias, d_attn, attn, lse, dproj)
    (dproj,) = _rowwise(functools.partial(_place_body), "place_dkv", t,
                        [(d_kv, 2 * D_KV, 0), (dproj, None, "any")],
                        [((t, D_IN), BF16, 2 * D_KV, C_K)], aliases={1: 0})
    g_table = _bias_grad(g_bias_full, bucket)

    d_glu, dproj = _rowwise(
        functools.partial(_ssm_gate_bwd_body), "ssm_gate_bwd", t,
        [(d_h_ssm, D_SSM, 0), (glu, 2 * D_SSM, 0), (proj, D_SSM, C_ZS), (dproj, None, "any")],
        [((t, 2 * D_SSM), BF16, 2 * D_SSM, 0), ((t, D_IN), BF16, D_SSM, C_ZS)], aliases={3: 1})
    g_w_glu = _matmul(g_in_t, d_glu, "nn", name="g_w_glu", tm=512, tn=512)
    d_g_in = _matmul(d_glu, w_glu, "nt", name="d_g_in", tm=1024, tn=1024, **act)
    dproj, g_bbd, g_cbd, g_lam_re, g_lam_im, g_d = _ssm_backward(
        proj, d_g_in, y_ssm, car_re, car_im, p_re, p_im, b_bd, c_bd, d_skip, dproj)
    g_lr, g_li, g_br, g_bi, g_cr, g_ci, g_ls = _ssm_param_grads(
        lam_re, lam_im, b_re, b_im, log_step, g_lam_re, g_lam_im, g_bbd, g_cbd)

    late = dict(w_glu=g_w_glu, w_branch_ssm=g_w_bs, w_branch_attn=g_w_ba, w_out=g_w_out)
    g_w_in, late_slots = _matmul(xbt, dproj, "nn", name="g_w_in", tm=512, tn=512,
                                 exchange=_chip_exchange(_pair_sums(late, LATE, core), LATE))
    small = dict(ssm_lambda_re=g_lr, ssm_lambda_im=g_li, ssm_b_re=g_br, ssm_b_im=g_bi, ssm_c_re=g_cr,
                 ssm_c_im=g_ci, ssm_d=g_d, ssm_log_step=g_ls, attn_sinks=g_sinks[:, :N_Q_HEADS],
                 rel_bias_table=g_table, ln_gain=g_gain, ln_bias=g_bias)
    last = _merge_exchanges(_chip_exchange(_pair_sums(dict(w_in=g_w_in), BIG[:1], core), BIG[:1]),
                            _small_exchange(_pack(small, loss)))
    grad_x, (in_slots, small_slots) = _matmul(dproj, w_in, "nt", name="grad_x", tm=512, tn=512, res=drb,
                                              res_scale=ALPHA, exchange=last)
    big = {**_shard_halves([in_slots], BIG[:1], core), **_shard_halves(late_slots, LATE, core)}
    return grad_x, big, _small_sum(small_slots)


WEIGHTS = ("w_in", "ssm_lambda_re", "ssm_lambda_im", "ssm_b_re", "ssm_b_im", "ssm_c_re", "ssm_c_im", "ssm_d",
           "ssm_log_step", "w_glu", "attn_sinks", "rel_bias_table", "w_branch_ssm", "w_branch_attn", "w_out",
           "ln_gain", "ln_bias")


def kernel(x, w_in, ssm_lambda_re, ssm_lambda_im, ssm_b_re, ssm_b_im, ssm_c_re, ssm_c_im, ssm_d, ssm_log_step, w_glu, attn_sinks, rel_bias_table, w_branch_ssm, w_branch_attn, w_out, ln_gain, ln_bias, loss_target, m_w_in, m_ssm_lambda_re, m_ssm_lambda_im, m_ssm_b_re, m_ssm_b_im, m_ssm_c_re, m_ssm_c_im, m_ssm_d, m_ssm_log_step, m_w_glu, m_attn_sinks, m_rel_bias_table, m_w_branch_ssm, m_w_branch_attn, m_w_out, m_ln_gain, m_ln_bias, v_w_in, v_ssm_lambda_re, v_ssm_lambda_im, v_ssm_b_re, v_ssm_b_im, v_ssm_c_re, v_ssm_c_im, v_ssm_d, v_ssm_log_step, v_w_glu, v_attn_sinks, v_rel_bias_table, v_w_branch_ssm, v_w_branch_attn, v_w_out, v_ln_gain, v_ln_bias):
    given = dict(locals())
    w = {k: given[k] for k in WEIGHTS}
    m = {k: given["m_" + k] for k in WEIGHTS}
    v = {k: given["v_" + k] for k in WEIGHTS}

    core = lax.axis_index("c").astype(jnp.int32).reshape(1)
    shards = {k: _cast(w[k][0], BF16, name="cast_" + k) for k in BIG}
    chip = (2 * lax.axis_index("x") + lax.axis_index("y")).astype(jnp.int32).reshape(1)
    grad_x, g_shard, g_packed = _local_step(
        x[0], loss_target[0], shards, core, chip, ssm_lambda_re[0], ssm_lambda_im[0], ssm_b_re[0],
        ssm_b_im[0], ssm_c_re[0], ssm_c_im[0], ssm_d, ssm_log_step, attn_sinks, rel_bias_table, ln_gain, ln_bias)
    loss_sum = _unpack(g_packed)[1]

    grad, delta, new_m, new_v = {}, {}, {}, {}
    for k in BIG:
        grad[k], delta[k], new_m[k], new_v[k] = _adamw_shard(w[k], m[k], v[k], g_shard[k], core,
                                                             name="adamw_" + k, rows=SHARE_ROWS[k])
    gs, ds, ms, vs = _adamw(_pack(w), _pack(m), _pack(v), [g_packed], name="adamw_small")
    for dst, packed in ((grad, gs), (delta, ds), (new_m, ms), (new_v, vs)):
        dst.update(_unpack(packed)[0])

    return (loss_sum, grad_x[None], *[grad[k] for k in WEIGHTS], *[delta[k] for k in WEIGHTS],
            *[new_m[k] for k in WEIGHTS], *[new_v[k] for k in WEIGHTS])
```

```python
import functools
import math

import numpy as np
import jax
import jax.numpy as jnp
from jax import lax
from jax.experimental import pallas as pl
from jax.experimental.pallas import tpu as pltpu

F32 = jnp.float32
BF16 = jnp.bfloat16

D_MODEL = 2048
D_SSM = 1024
SSM_GROUP = 16
N_GROUPS = 64
SSM_STATE = 64
N_LANES = N_GROUPS * SSM_STATE
N_Q_HEADS = 16
N_KV_HEADS = 4
HEAD_DIM = 64
Q_PER_KV = 4
D_ATTN = 1024
D_KV = 256
WINDOW = 128
BLOCK = 128
N_BUCKETS = 32
MAX_DISTANCE = 128
D_IN = 8704
ALPHA = 2.0 ** 0.25
LN_EPS = 1e-5
NEG_INF = -1e30
ATTN_SCALE = HEAD_DIM ** -0.5

C_U, C_ZS, C_Q, C_K, C_V, C_ZA, C_GL = 0, 1024, 2048, 3072, 3328, 3584, 4608

ADAM_LR = 0.001
ADAM_B1 = 0.9
ADAM_B2 = 0.999
ADAM_EPS = 1e-08
ADAM_WD = 0.01
ADAM_STEP = 10

N_CHIPS = 4
MESH = pl.DeviceIdType.MESH

SSM_CHUNK = 256
SEG_LEN = SSM_CHUNK // 8
SLAB_LANES = 512
N_SLABS = N_LANES // SLAB_LANES
SLAB_CH = D_SSM // N_SLABS

VMEM_LIMIT = 60 * 1024 * 1024


def _params(sem=None, **kw):
    return pltpu.CompilerParams(dimension_semantics=sem, vmem_limit_bytes=VMEM_LIMIT, **kw)


_DIMS = {"nn": (((1,), (0,)), ((), ())), "nt": (((1,), (1,)), ((), ())), "tn": (((0,), (0,)), ((), ()))}


class _Exchange:
    def __init__(self, ins, out_shape, n_sems, copies):
        self.ins, self.out_shape, self.n_sems, self.copies = list(ins), list(out_shape), list(n_sems), copies

    def start(self, ins, outs, sems):
        remote, local = self.copies(ins, outs, sems)
        for cp in remote + local:
            cp.start()

    def finish(self, ins, outs, sems):
        remote, local = self.copies(ins, outs, sems)
        for cp in remote:
            cp.wait_recv()
        for cp in remote:
            cp.wait_send()
        for cp in local:
            cp.wait()


def _mm_body(*refs, dims, nk, res_scale, exchange, grid):
    n_ex_in = len(exchange.ins) if exchange else 0
    n_ex_out = len(exchange.out_shape) if exchange else 0
    n_in = 2 + (res_scale is not None) + n_ex_in
    a_ref, b_ref = refs[0], refs[1]
    r_ref = refs[2] if res_scale is not None else None
    ex_in = refs[n_in - n_ex_in:n_in]
    o_ref, ex_out = refs[n_in], refs[n_in + 1:n_in + 1 + n_ex_out]
    acc_ref, ex_sems = refs[n_in + 1 + n_ex_out], refs[n_in + 2 + n_ex_out:]
    i, j, k = pl.program_id(0), pl.program_id(1), pl.program_id(2)

    if exchange:
        @pl.when(jnp.logical_and(jnp.logical_and(i == 0, j == 0), k == 0))
        def _():
            exchange.start(ex_in, ex_out, ex_sems)

    part = lax.dot_general(a_ref[...].astype(BF16), b_ref[...].astype(BF16), _DIMS[dims],
                           preferred_element_type=F32)

    def finish(acc):
        if r_ref is not None:
            acc = acc + res_scale * r_ref[...].astype(F32)
        o_ref[...] = acc.astype(o_ref.dtype)

    if nk == 1:
        finish(part)
    else:
        @pl.when(k == 0)
        def _():
            acc_ref[...] = part

        @pl.when(k > 0)
        def _():
            acc_ref[...] += part

        @pl.when(k == nk - 1)
        def _():
            finish(acc_ref[...])

    if exchange:
        @pl.when(jnp.logical_and(jnp.logical_and(i == grid[0] - 1, j == grid[1] - 1), k == grid[2] - 1))
        def _():
            exchange.finish(ex_in, ex_out, ex_sems)


def _matmul(a, b, dims, *, name, out_dtype=F32, tm=512, tn=512, tk=None, res=None, res_scale=None, exchange=None):
    if dims == "nn":
        (m, kk), n = a.shape, b.shape[1]
    elif dims == "nt":
        (m, kk), n = a.shape, b.shape[0]
    else:
        (kk, m), n = a.shape, b.shape[1]
    tm, tn = min(tm, m), min(tn, n)
    tk = kk if tk is None else min(tk, kk)
    assert m % tm == 0 and n % tn == 0 and kk % tk == 0, (name, m, n, kk, tm, tn, tk)
    nk = kk // tk
    a_spec = {"nn": pl.BlockSpec((tm, tk), lambda i, j, k: (i, k)),
              "nt": pl.BlockSpec((tm, tk), lambda i, j, k: (i, k)),
              "tn": pl.BlockSpec((tk, tm), lambda i, j, k: (k, i))}[dims]
    b_spec = {"nn": pl.BlockSpec((tk, tn), lambda i, j, k: (k, j)),
              "nt": pl.BlockSpec((tn, tk), lambda i, j, k: (j, k)),
              "tn": pl.BlockSpec((tk, tn), lambda i, j, k: (k, j))}[dims]
    in_specs, args = [a_spec, b_spec], [a, b]
    if res is not None:
        in_specs.append(pl.BlockSpec((tm, tn), lambda i, j, k: (i, j)))
        args.append(res)
    grid = (m // tm, n // tn, nk)
    out_specs = [pl.BlockSpec((tm, tn), lambda i, j, k: (i, j))]
    out_shape = [jax.ShapeDtypeStruct((m, n), out_dtype)]
    scratch = [pltpu.VMEM((tm, tn), F32)]
    if exchange:
        in_specs += [pl.BlockSpec(memory_space=pl.ANY)] * len(exchange.ins)
        args += exchange.ins
        out_specs += [pl.BlockSpec(memory_space=pl.ANY)] * len(exchange.out_shape)
        out_shape += exchange.out_shape
        scratch += [pltpu.SemaphoreType.DMA((s,)) for s in exchange.n_sems]
    outs = pl.pallas_call(
        functools.partial(_mm_body, dims=dims, nk=nk, res_scale=res_scale if res is not None else None,
                          exchange=exchange, grid=grid),
        name=name, grid=grid, in_specs=in_specs, out_specs=out_specs, out_shape=out_shape, scratch_shapes=scratch,
        compiler_params=_params(("arbitrary",) * 3 if exchange else ("parallel", "parallel", "arbitrary")),
    )(*args)
    return (outs[0], outs[1:]) if exchange else outs[0]


def _proj_shards_body(chip_ref, *refs, n_ex_in, has_base, exchange, grid):
    del chip_ref
    refs = list(refs)
    if has_base:
        del refs[2 + n_ex_in]
    _mm_body(*refs, dims="nn", nk=1, res_scale=None, exchange=exchange, grid=grid)


def _proj_shards(xb, w, chip, offsets, *, name, base=None, exchange=None, tm=1024):
    t, d = xb.shape
    cs = D_IN // N_CHIPS
    own = w.shape[1] == cs
    tm = min(tm, t)
    col = lambda j, cr: (cr[0] + offsets[0] + j) % N_CHIPS
    grid = (len(offsets), t // tm, 1)
    in_specs = [pl.BlockSpec((tm, d), lambda j, i, k, cr: (i, 0)),
                pl.BlockSpec((d, cs), lambda j, i, k, cr: (0, 0 if own else col(j, cr)))]
    args = [xb, w]
    out_specs = [pl.BlockSpec((tm, cs), lambda j, i, k, cr: (i, col(j, cr)))]
    out_shape = [jax.ShapeDtypeStruct((t, D_IN), BF16)]
    scratch = [pltpu.VMEM((8, 128), F32)]
    n_ex_in = 0
    if exchange:
        n_ex_in = len(exchange.ins)
        in_specs += [pl.BlockSpec(memory_space=pl.ANY)] * n_ex_in
        args += exchange.ins
        out_specs += [pl.BlockSpec(memory_space=pl.ANY)] * len(exchange.out_shape)
        out_shape += exchange.out_shape
        scratch += [pltpu.SemaphoreType.DMA((s,)) for s in exchange.n_sems]
    aliases = {}
    if base is not None:
        in_specs.append(pl.BlockSpec(memory_space=pl.ANY))
        args.append(base)
        aliases = {len(args): 0}
    outs = pl.pallas_call(
        functools.partial(_proj_shards_body, n_ex_in=n_ex_in, has_base=base is not None, exchange=exchange, grid=grid),
        name=name,
        grid_spec=pltpu.PrefetchScalarGridSpec(num_scalar_prefetch=1, grid=grid, in_specs=in_specs,
                                               out_specs=out_specs, scratch_shapes=scratch),
        out_shape=out_shape, input_output_aliases=aliases,
        compiler_params=_params(("arbitrary",) * 3),
    )(chip, *args)
    return (outs[0], outs[1:]) if exchange else outs[0]


def _sigmoid(v):
    return 1.0 / (1.0 + jnp.exp(-v))


def _silu_and_grad(z):
    s = _sigmoid(z)
    return z * s, s * (1.0 + z * (1.0 - s))


def _cast_body(x_ref, o_ref):
    o_ref[...] = x_ref[...].astype(o_ref.dtype)


def _cast_and_transpose_body(x_ref, after_ref, o_ref, ot_ref):
    del after_ref
    xb = x_ref[...].astype(BF16)
    o_ref[...] = xb
    ot_ref[...] = xb.T


def _cast_and_transpose(x, after, *, name, rows=512):
    m, n = x.shape
    rows = min(rows, m)
    return pl.pallas_call(
        functools.partial(_cast_and_transpose_body), name=name, grid=(m // rows,),
        in_specs=[pl.BlockSpec((rows, n), lambda i: (i, 0)), pl.BlockSpec(memory_space=pl.ANY)],
        out_specs=[pl.BlockSpec((rows, n), lambda i: (i, 0)), pl.BlockSpec((n, rows), lambda i: (0, i))],
        out_shape=[jax.ShapeDtypeStruct((m, n), BF16), jax.ShapeDtypeStruct((n, m), BF16)],
        compiler_params=_params(("parallel",)),
    )(x, after)


def _cast(x, dtype, *, name, rows=512):
    m, n = x.shape
    rows = min(rows, m)
    return pl.pallas_call(
        functools.partial(_cast_body), name=name, grid=(m // rows,),
        in_specs=[pl.BlockSpec((rows, n), lambda i: (i, 0))],
        out_specs=pl.BlockSpec((rows, n), lambda i: (i, 0)),
        out_shape=jax.ShapeDtypeStruct((m, n), dtype),
        compiler_params=_params(("parallel",)),
    )(x)


def _lam_bar(lr, li, ls):
    step = jnp.exp(ls)
    er = jnp.exp(lr * step)
    return step, er * jnp.cos(li * step), er * jnp.sin(li * step)


def _ssm_pow_body(lr_ref, li_ref, ls_ref, pr_ref, pi_ref):
    _, ar, ai = _lam_bar(lr_ref[...], li_ref[...], ls_ref[...])
    cr, ci = ar, ai
    for i in range(SEG_LEN):
        pr_ref[pl.ds(i, 1), :] = cr
        pi_ref[pl.ds(i, 1), :] = ci
        cr, ci = cr * ar - ci * ai, cr * ai + ci * ar


def _ssm_bbar_body(lr_ref, li_ref, ls_ref, br_ref, bi_ref, or_ref, oi_ref):
    lr, li = lr_ref[...], li_ref[...]
    _, ar, ai = _lam_bar(lr, li, ls_ref[...])
    d = lr * lr + li * li
    ir, ii = lr / d, -li / d
    nr, ni = ar - 1.0, ai
    cr, ci = nr * ir - ni * ii, nr * ii + ni * ir
    br, bi = br_ref[...], bi_ref[...]
    or_ref[...] = cr * br - ci * bi
    oi_ref[...] = cr * bi + ci * br


def _ssm_param_bwd_body(lr_ref, li_ref, ls_ref, br_ref, bi_ref, glr_ref, gli_ref, gbr_ref, gbi_ref,
                        dlr_ref, dli_ref, dls_ref, dbr_ref, dbi_ref):
    lr, li = lr_ref[...], li_ref[...]
    step, ar, ai = _lam_bar(lr, li, ls_ref[...])
    d = lr * lr + li * li
    ir, ii = lr / d, -li / d
    nr, ni = ar - 1.0, ai
    cr, ci = nr * ir - ni * ii, nr * ii + ni * ir
    br, bi, gbr, gbi = br_ref[...], bi_ref[...], gbr_ref[...], gbi_ref[...]
    dbr_ref[...] = cr * gbr + ci * gbi
    dbi_ref[...] = cr * gbi - ci * gbr
    gcr = jnp.sum(br * gbr + bi * gbi, axis=1, keepdims=True)
    gci = jnp.sum(br * gbi - bi * gbr, axis=1, keepdims=True)
    gnr, gni = ir * gcr + ii * gci, ir * gci - ii * gcr
    gir, gii = nr * gcr + ni * gci, nr * gci - ni * gcr
    gtr, gti = glr_ref[...] + gnr, gli_ref[...] + gni
    i2r, i2i = ir * ir - ii * ii, 2.0 * ir * ii
    g1r, g1i = -(i2r * gir + i2i * gii), -(i2r * gii - i2i * gir)
    g2r, g2i = step * (ar * gtr + ai * gti), step * (ar * gti - ai * gtr)
    mr, mi = lr * ar - li * ai, lr * ai + li * ar
    dlr_ref[...] = g1r + g2r
    dli_ref[...] = g1i + g2i
    dls_ref[...] = (mr * gtr + mi * gti) * step


def _whole(shape):
    return pl.BlockSpec(shape, lambda *_: (0,) * len(shape))


def _ssm_prepare(lam_re, lam_im, b_re, b_im, c_re, c_im, log_step):
    row = lambda a: a.reshape(1, N_LANES)
    col = lambda a: a.reshape(N_LANES, 1)
    ls = jnp.repeat(log_step.reshape(N_GROUPS), SSM_STATE)
    p_re, p_im = pl.pallas_call(
        functools.partial(_ssm_pow_body), name="ssm_pow",
        in_specs=[_whole((1, N_LANES))] * 3, out_specs=[_whole((SEG_LEN, N_LANES))] * 2,
        out_shape=[jax.ShapeDtypeStruct((SEG_LEN, N_LANES), F32)] * 2, grid=(1,),
    )(row(lam_re), row(lam_im), row(ls))
    bb_re, bb_im = pl.pallas_call(
        functools.partial(_ssm_bbar_body), name="ssm_bbar",
        in_specs=[_whole((N_LANES, 1))] * 3 + [_whole((N_LANES, SSM_GROUP))] * 2,
        out_specs=[_whole((N_LANES, SSM_GROUP))] * 2,
        out_shape=[jax.ShapeDtypeStruct((N_LANES, SSM_GROUP), F32)] * 2, grid=(1,),
    )(col(lam_re), col(lam_im), col(ls), b_re.reshape(N_LANES, SSM_GROUP), b_im.reshape(N_LANES, SSM_GROUP))
    eye = jnp.eye(8, dtype=F32)

    def b_slabs(bb):
        t = bb.reshape(N_SLABS, 8, SSM_STATE, SSM_GROUP).transpose(0, 1, 3, 2)
        return (t[:, :, :, None, :] * eye[None, :, None, :, None]).reshape(N_SLABS, SLAB_CH, SLAB_LANES)

    def c_slabs(c):
        t = c.reshape(N_SLABS, 8, SSM_GROUP, SSM_STATE).transpose(0, 1, 3, 2)
        return (t[:, :, :, None, :] * eye[None, :, None, :, None]).reshape(N_SLABS, SLAB_LANES, SLAB_CH)

    b_bd = jnp.concatenate([b_slabs(bb_re), b_slabs(bb_im)], axis=2).astype(BF16)
    c_bd = jnp.concatenate([c_slabs(c_re.reshape(N_GROUPS, SSM_GROUP, SSM_STATE)),
                            -c_slabs(c_im.reshape(N_GROUPS, SSM_GROUP, SSM_STATE))], axis=1).astype(BF16)
    return p_re, p_im, b_bd, c_bd


def _diag_blocks_b(g):
    t = g.reshape(N_SLABS, 8, SSM_GROUP, 8, SSM_STATE)
    t = jnp.stack([t[:, i, :, i, :] for i in range(8)], axis=1)
    return t.transpose(0, 1, 3, 2).reshape(N_LANES, SSM_GROUP)


def _diag_blocks_c(g):
    t = g.reshape(N_SLABS, 8, SSM_STATE, 8, SSM_GROUP)
    t = jnp.stack([t[:, i, :, i, :] for i in range(8)], axis=1)
    return t.transpose(0, 1, 3, 2).reshape(N_GROUPS, SSM_GROUP, SSM_STATE)


def _ssm_param_grads(lam_re, lam_im, b_re, b_im, log_step, g_lam_re, g_lam_im, g_bbd, g_cbd):
    col = lambda a: a.reshape(N_LANES, 1)
    ls = jnp.repeat(log_step.reshape(N_GROUPS), SSM_STATE)
    gbr = _diag_blocks_b(g_bbd[:, :, :SLAB_LANES])
    gbi = _diag_blocks_b(g_bbd[:, :, SLAB_LANES:])
    outs = pl.pallas_call(
        functools.partial(_ssm_param_bwd_body), name="ssm_param_bwd", grid=(1,),
        in_specs=[_whole((N_LANES, 1))] * 3 + [_whole((N_LANES, SSM_GROUP))] * 2 + [_whole((N_LANES, 1))] * 2
        + [_whole((N_LANES, SSM_GROUP))] * 2,
        out_specs=[_whole((N_LANES, 1))] * 3 + [_whole((N_LANES, SSM_GROUP))] * 2,
        out_shape=[jax.ShapeDtypeStruct((N_LANES, 1), F32)] * 3 + [jax.ShapeDtypeStruct((N_LANES, SSM_GROUP), F32)] * 2,
    )(col(lam_re), col(lam_im), col(ls), b_re.reshape(N_LANES, SSM_GROUP), b_im.reshape(N_LANES, SSM_GROUP),
      col(g_lam_re), col(g_lam_im), gbr, gbi)
    dlr, dli, dls, dbr, dbi = outs
    d_c_re = _diag_blocks_c(g_cbd[:, :SLAB_LANES, :])
    d_c_im = -_diag_blocks_c(g_cbd[:, SLAB_LANES:, :])
    return (dlr.reshape(1, N_GROUPS, SSM_STATE), dli.reshape(1, N_GROUPS, SSM_STATE),
            dbr.reshape(1, N_GROUPS, SSM_STATE, SSM_GROUP), dbi.reshape(1, N_GROUPS, SSM_STATE, SSM_GROUP),
            d_c_re[None], d_c_im[None], dls.reshape(N_GROUPS, SSM_STATE).sum(axis=1).reshape(1, N_GROUPS))


def _bcast8(v):
    return jnp.broadcast_to(v, (8, v.shape[1]))


def _segment_permutation():
    p = np.zeros((SSM_CHUNK, SSM_CHUNK), np.float32)
    rows = np.arange(SSM_CHUNK)
    p[rows, (rows % 8) * SEG_LEN + rows // 8] = 1.0
    return p


def _permute_exact(perm, val, pieces):
    out, rest = None, val
    for n in range(pieces):
        part = rest.astype(BF16)
        moved = jnp.dot(perm, part, preferred_element_type=F32)
        out = moved if out is None else out + moved
        if n + 1 < pieces:
            rest = rest - part.astype(F32)
    return out


def _scan_buffer():
    return pltpu.VMEM((SSM_CHUNK, N_LANES), F32)


def _lanes(k):
    return pl.ds(k * SLAB_LANES, SLAB_LANES)


def _tile(i):
    return pl.ds(i * 8 if isinstance(i, int) else pl.multiple_of(i * 8, 8), 8)


SCAN_LANES = 1024


def _scan_lanes(k):
    return pl.ds(k * SCAN_LANES, SCAN_LANES)


def _seg_get(ref, k, i):
    return ref[_tile(i), _scan_lanes(k)]


def _seg_put(ref, k, i, val):
    ref[_tile(i), _scan_lanes(k)] = val


def _slab_get(ref, k):
    return ref[:, _lanes(k)]


def _slab_put(ref, k, val):
    ref[:, _lanes(k)] = val


def _scan_forward(s_re, s_im, p_re, p_im, car_re, car_im, sp_re=None, sp_im=None):
    for k in range(N_LANES // SCAN_LANES):
        ln = _scan_lanes(k)
        ar, ai =_bcast8(p_re[pl.ds(0, 1), ln]), _bcast8(p_im[pl.ds(0, 1), ln])

        def step(i, s, k=k, ar=ar, ai=ai):
            sr, si = s
            nr = ar * sr - ai * si + _seg_get(s_re, k, i)
            ni = ar * si + ai * sr + _seg_get(s_im, k, i)
            _seg_put(s_re, k, i, nr)
            _seg_put(s_im, k, i, ni)
            return nr, ni

        zero = jnp.zeros((8, SCAN_LANES), F32)
        er, ei = lax.fori_loop(0, SEG_LEN, step, (zero, zero), unroll=4)
        lr, li = p_re[pl.ds(SEG_LEN - 1, 1), ln], p_im[pl.ds(SEG_LEN - 1, 1), ln]
        cr, ci = car_re[pl.ds(0, 1), ln], car_im[pl.ds(0, 1), ln]
        rows_r, rows_i = [], []
        for r in range(8):
            rows_r.append(cr)
            rows_i.append(ci)
            cr, ci = er[r:r + 1] + lr * cr - li * ci, ei[r:r + 1] + lr * ci + li * cr
        pr8, pi8 = jnp.concatenate(rows_r, axis=0), jnp.concatenate(rows_i, axis=0)
        car_re[:, ln] = _bcast8(cr)
        car_im[:, ln] = _bcast8(ci)
        if sp_re is not None:
            sp_re[:, ln] = pr8
            sp_im[:, ln] = pi8

        def fix(i, _, k=k, ln=ln, pr8=pr8, pi8=pi8):
            qr, qi = _bcast8(p_re[pl.ds(i, 1), ln]), _bcast8(p_im[pl.ds(i, 1), ln])
            _seg_put(s_re, k, i, _seg_get(s_re, k, i) + qr * pr8 - qi * pi8)
            _seg_put(s_im, k, i, _seg_get(s_im, k, i) + qr * pi8 + qi * pr8)
            return 0

        lax.fori_loop(0, SEG_LEN, fix, 0, unroll=4)


def _scan_backward(g_re, g_im, s_re, s_im, sp_re, sp_im, p_re, p_im, car_re, car_im, acc_re, acc_im):
    for k in range(N_LANES // SCAN_LANES):
        ln = _scan_lanes(k)
        ar, ai =_bcast8(p_re[pl.ds(0, 1), ln]), -_bcast8(p_im[pl.ds(0, 1), ln])

        def step(j, s, k=k, ar=ar, ai=ai):
            i = SEG_LEN - 1 - j
            sr, si = s
            nr = ar * sr - ai * si + _seg_get(g_re, k, i)
            ni = ar * si + ai * sr + _seg_get(g_im, k, i)
            _seg_put(g_re, k, i, nr)
            _seg_put(g_im, k, i, ni)
            return nr, ni

        zero = jnp.zeros((8, SCAN_LANES), F32)
        er, ei = lax.fori_loop(0, SEG_LEN, step, (zero, zero), unroll=4)
        lr, li = p_re[pl.ds(SEG_LEN - 1, 1), ln], -p_im[pl.ds(SEG_LEN - 1, 1), ln]
        cr, ci = car_re[pl.ds(0, 1), ln], car_im[pl.ds(0, 1), ln]
        rows_r, rows_i = [None] * 8, [None] * 8
        for r in range(7, -1, -1):
            rows_r[r], rows_i[r] = cr, ci
            cr, ci = er[r:r + 1] + lr * cr - li * ci, ei[r:r + 1] + lr * ci + li * cr
        nr8, ni8 = jnp.concatenate(rows_r, axis=0), jnp.concatenate(rows_i, axis=0)
        car_re[:, ln] = _bcast8(cr)
        car_im[:, ln] = _bcast8(ci)

        def fix(i, acc, k=k, ln=ln, nr8=nr8, ni8=ni8):
            qr = _bcast8(p_re[pl.ds(SEG_LEN - 1 - i, 1), ln])
            qi = -_bcast8(p_im[pl.ds(SEG_LEN - 1 - i, 1), ln])
            gr = _seg_get(g_re, k, i) + qr * nr8 - qi * ni8
            gi = _seg_get(g_im, k, i) + qr * ni8 + qi * nr8
            _seg_put(g_re, k, i, gr)
            _seg_put(g_im, k, i, gi)
            return gr, gi

        def prod(xr, xi, gr, gi):
            return xr * gr + xi * gi, xr * gi - xi * gr

        gr, gi = fix(0, None)
        a_r, a_i = prod(sp_re[:, ln], sp_im[:, ln], gr, gi)

        def fix_acc(i, acc, k=k, fix=fix):
            gr, gi = fix(i, None)
            dr, di = prod(_seg_get(s_re, k, i - 1), _seg_get(s_im, k, i - 1), gr, gi)
            return acc[0] + dr, acc[1] + di

        acc = (a_r, a_i)
        for i in range(1, 4):
            acc = fix_acc(i, acc)
        a_r, a_i = lax.fori_loop(4, SEG_LEN, fix_acc, acc, unroll=4)
        acc_re[:, ln] += a_r
        acc_im[:, ln] += a_i


def _gelu_and_grad(y):
    cdf = 0.5 * (1.0 + lax.erf(y * (2.0 ** -0.5)))
    pdf = jnp.exp(-0.5 * y * y) * (1.0 / math.sqrt(2.0 * math.pi))
    return y * cdf, cdf + y * pdf


def _ssm_fwd_body(u_ref, bbd_ref, cbd_ref, pre_ref, pim_ref, d_ref, perm_ref, unperm_ref,
                  y_ref, gin_ref, gint_ref, cre_out, cim_out, s_re, s_im, car_re, car_im, yp):
    c = pl.program_id(0)

    @pl.when(c == 0)
    def _():
        car_re[...] = jnp.zeros_like(car_re)
        car_im[...] = jnp.zeros_like(car_im)

    cre_out[...] = car_re[...]
    cim_out[...] = car_im[...]
    u = u_ref[...].astype(F32)
    up = jnp.dot(perm_ref[...], u_ref[...].astype(BF16), preferred_element_type=F32).astype(BF16)
    for k in range(N_SLABS):
        bu = jnp.dot(up[:, k * SLAB_CH:(k + 1) * SLAB_CH], bbd_ref[k], preferred_element_type=F32)
        _slab_put(s_re, k, bu[:, :SLAB_LANES])
        _slab_put(s_im, k, bu[:, SLAB_LANES:])
    _scan_forward(s_re, s_im, pre_ref, pim_ref, car_re, car_im)
    for k in range(N_SLABS):
        yp[:, pl.ds(k * SLAB_CH, SLAB_CH)] = (
            jnp.dot(_slab_get(s_re, k).astype(BF16), cbd_ref[k, :SLAB_LANES, :], preferred_element_type=F32)
            + jnp.dot(_slab_get(s_im, k).astype(BF16), cbd_ref[k, SLAB_LANES:, :], preferred_element_type=F32))
    y = _permute_exact(unperm_ref[...], yp[...], 2) + d_ref[...] * u
    y_ref[...] = y.astype(y_ref.dtype)
    _store_with_transpose(_gelu_and_grad(y)[0], gin_ref, gint_ref)


def _ssm_forward(proj, p_re, p_im, b_bd, c_bd, d_skip):
    t = proj.shape[0]
    nc = t // SSM_CHUNK
    perm = _segment_permutation()
    return pl.pallas_call(
        functools.partial(_ssm_fwd_body), name="ssm_fwd", grid=(nc,),
        in_specs=[pl.BlockSpec((SSM_CHUNK, D_SSM), lambda c: (c, C_U // D_SSM)),
                  _whole(b_bd.shape), _whole(c_bd.shape), _whole(p_re.shape), _whole(p_im.shape),
                  _whole((1, D_SSM)), _whole(perm.shape), _whole(perm.shape)],
        out_specs=[pl.BlockSpec((SSM_CHUNK, D_SSM), lambda c: (c, 0)),
                   pl.BlockSpec((SSM_CHUNK, D_SSM), lambda c: (c, 0)),
                   pl.BlockSpec((D_SSM, SSM_CHUNK), lambda c: (0, c)),
                   pl.BlockSpec((None, 8, N_LANES), lambda c: (c, 0, 0)),
                   pl.BlockSpec((None, 8, N_LANES), lambda c: (c, 0, 0))],
        out_shape=[jax.ShapeDtypeStruct((t, D_SSM), BF16), jax.ShapeDtypeStruct((t, D_SSM), BF16),
                   jax.ShapeDtypeStruct((D_SSM, t), BF16),
                   jax.ShapeDtypeStruct((nc, 8, N_LANES), F32), jax.ShapeDtypeStruct((nc, 8, N_LANES), F32)],
        scratch_shapes=[_scan_buffer(), _scan_buffer(),
                        pltpu.VMEM((8, N_LANES), F32), pltpu.VMEM((8, N_LANES), F32),
                        pltpu.VMEM((SSM_CHUNK, D_SSM), F32)],
        compiler_params=_params(("arbitrary",)),
    )(proj, b_bd, c_bd, p_re, p_im, d_skip, jnp.asarray(perm, BF16), jnp.asarray(perm.T, BF16))


def _ssm_bwd_body(u_ref, dgin_ref, y_ref, cre_in, cim_in, bbd_ref, cbd_ref, pre_ref, pim_ref, d_ref, perm_ref,
                  unperm_ref, dproj_in,
                  du_ref, gb_ref, gc_ref, glr_ref, gli_ref, gd_ref,
                  s_re, s_im, g_re, g_im, sp_re, sp_im, car_re, car_im, gcar_re, gcar_im, acc_re, acc_im, dup):
    del dproj_in
    c = pl.program_id(0)
    nc = pl.num_programs(0)

    @pl.when(c == 0)
    def _():
        gcar_re[...] = jnp.zeros_like(gcar_re)
        gcar_im[...] = jnp.zeros_like(gcar_im)
        acc_re[...] = jnp.zeros_like(acc_re)
        acc_im[...] = jnp.zeros_like(acc_im)
        gb_ref[...] = jnp.zeros_like(gb_ref)
        gc_ref[...] = jnp.zeros_like(gc_ref)
        gd_ref[...] = jnp.zeros_like(gd_ref)

    car_re[...] = cre_in[...]
    car_im[...] = cim_in[...]
    u = u_ref[...].astype(F32)
    dy = dgin_ref[...].astype(F32) * _gelu_and_grad(y_ref[...].astype(F32))[1]
    gd_ref[...] += jnp.sum(dy * u, axis=0, keepdims=True)
    up = jnp.dot(perm_ref[...], u.astype(BF16), preferred_element_type=F32).astype(BF16)
    dyp = jnp.dot(perm_ref[...], dy.astype(BF16), preferred_element_type=F32).astype(BF16)
    for k in range(N_SLABS):
        ch = slice(k * SLAB_CH, (k + 1) * SLAB_CH)
        bu = jnp.dot(up[:, ch], bbd_ref[k], preferred_element_type=F32)
        _slab_put(s_re, k, bu[:, :SLAB_LANES])
        _slab_put(s_im, k, bu[:, SLAB_LANES:])
        ds = lax.dot_general(dyp[:, ch], cbd_ref[k], _DIMS["nt"], preferred_element_type=F32)
        _slab_put(g_re, k, ds[:, :SLAB_LANES])
        _slab_put(g_im, k, ds[:, SLAB_LANES:])
    _scan_forward(s_re, s_im, pre_ref, pim_ref, car_re, car_im, sp_re, sp_im)
    _scan_backward(g_re, g_im, s_re, s_im, sp_re, sp_im, pre_ref, pim_ref, gcar_re, gcar_im, acc_re, acc_im)
    for k in range(N_SLABS):
        ch = slice(k * SLAB_CH, (k + 1) * SLAB_CH)
        uk, dyk = up[:, ch], dyp[:, ch]
        sr, si = _slab_get(s_re, k).astype(BF16), _slab_get(s_im, k).astype(BF16)
        gr, gi = _slab_get(g_re, k).astype(BF16), _slab_get(g_im, k).astype(BF16)
        gc_ref[k, :SLAB_LANES, :] += lax.dot_general(sr, dyk, _DIMS["tn"], preferred_element_type=F32)
        gc_ref[k, SLAB_LANES:, :] += lax.dot_general(si, dyk, _DIMS["tn"], preferred_element_type=F32)
        gb_ref[k, :, :SLAB_LANES] += lax.dot_general(uk, gr, _DIMS["tn"], preferred_element_type=F32)
        gb_ref[k, :, SLAB_LANES:] += lax.dot_general(uk, gi, _DIMS["tn"], preferred_element_type=F32)
        dup[:, pl.ds(k * SLAB_CH, SLAB_CH)] = (
            lax.dot_general(gr, bbd_ref[k, :, :SLAB_LANES], _DIMS["nt"], preferred_element_type=F32)
            + lax.dot_general(gi, bbd_ref[k, :, SLAB_LANES:], _DIMS["nt"], preferred_element_type=F32))
    du = _permute_exact(unperm_ref[...], dup[...], 2) + d_ref[...] * dy
    du_ref[...] = du.astype(du_ref.dtype)

    @pl.when(c == nc - 1)
    def _():
        glr_ref[...] = jnp.sum(acc_re[...], axis=0, keepdims=True)
        gli_ref[...] = jnp.sum(acc_im[...], axis=0, keepdims=True)


def _ssm_backward(proj, dg_in, y_ssm, car_re, car_im, p_re, p_im, b_bd, c_bd, d_skip, dproj):
    t = proj.shape[0]
    nc = t // SSM_CHUNK
    rev = lambda c: nc - 1 - c
    big = _scan_buffer
    small = lambda: pltpu.VMEM((8, N_LANES), F32)
    perm = _segment_permutation()
    outs = pl.pallas_call(
        functools.partial(_ssm_bwd_body), name="ssm_bwd", grid=(nc,),
        in_specs=[pl.BlockSpec((SSM_CHUNK, D_SSM), lambda c: (rev(c), C_U // D_SSM)),
                  pl.BlockSpec((SSM_CHUNK, D_SSM), lambda c: (rev(c), 0)),
                  pl.BlockSpec((SSM_CHUNK, D_SSM), lambda c: (rev(c), 0)),
                  pl.BlockSpec((None, 8, N_LANES), lambda c: (rev(c), 0, 0)),
                  pl.BlockSpec((None, 8, N_LANES), lambda c: (rev(c), 0, 0)),
                  _whole(b_bd.shape), _whole(c_bd.shape), _whole(p_re.shape), _whole(p_im.shape),
                  _whole((1, D_SSM)), _whole(perm.shape), _whole(perm.shape), pl.BlockSpec(memory_space=pl.ANY)],
        out_specs=[pl.BlockSpec((SSM_CHUNK, D_SSM), lambda c: (rev(c), C_U // D_SSM)),
                   _whole(b_bd.shape), _whole(c_bd.shape), _whole((1, N_LANES)), _whole((1, N_LANES)),
                   _whole((1, D_SSM))],
        out_shape=[jax.ShapeDtypeStruct(dproj.shape, dproj.dtype),
                   jax.ShapeDtypeStruct(b_bd.shape, F32), jax.ShapeDtypeStruct(c_bd.shape, F32),
                   jax.ShapeDtypeStruct((1, N_LANES), F32), jax.ShapeDtypeStruct((1, N_LANES), F32),
                   jax.ShapeDtypeStruct((1, D_SSM), F32)],
        scratch_shapes=[big(), big(), big(), big()] + [small() for _ in range(8)]
        + [pltpu.VMEM((SSM_CHUNK, D_SSM), F32)],
        input_output_aliases={12: 0},
        compiler_params=_params(("arbitrary",)),
    )(proj, dg_in, y_ssm, car_re, car_im, b_bd, c_bd, p_re, p_im, d_skip, jnp.asarray(perm, BF16),
      jnp.asarray(perm.T, BF16), dproj)
    return outs


def _bucket_table():
    i = np.arange(BLOCK)[:, None]
    j = np.arange(2 * BLOCK)[None, :]
    dist = BLOCK + i - j
    ok = (dist >= 0) & (dist < WINDOW)
    max_exact = N_BUCKETS // 2
    d = np.maximum(dist, 1).astype(np.float32)
    large = max_exact + (np.log(d / max_exact) / math.log(MAX_DISTANCE / max_exact)
                         * (N_BUCKETS - max_exact)).astype(np.int32)
    large = np.minimum(large, N_BUCKETS - 1)
    bucket = np.where(dist < max_exact, dist, large)
    return np.where(ok, bucket, -1).astype(np.int32)


def _bias_build_body(table_ref, bucket_ref, o_ref):
    h = pl.program_id(0)
    bucket = bucket_ref[...]
    acc = jnp.full(bucket.shape, NEG_INF, F32)
    for b in range(N_BUCKETS):
        acc = jnp.where(bucket == b, table_ref[b, h], acc)
    o_ref[...] = acc


def _bias_build(rel_bias_table, bucket):
    return pl.pallas_call(
        functools.partial(_bias_build_body), name="bias_build", grid=(N_Q_HEADS,),
        in_specs=[pl.BlockSpec(memory_space=pltpu.SMEM), _whole(bucket.shape)],
        out_specs=pl.BlockSpec((None, BLOCK, 2 * BLOCK), lambda h: (h, 0, 0)),
        out_shape=jax.ShapeDtypeStruct((N_Q_HEADS, BLOCK, 2 * BLOCK), F32),
        compiler_params=_params(("arbitrary",)),
    )(rel_bias_table, bucket)


def _bias_grad_body(g_ref, bucket_ref, o_ref):
    bucket = bucket_ref[...]
    lane = lax.broadcasted_iota(jnp.int32, (N_BUCKETS, 128), 1)

    def head(h, out):
        g = g_ref[h]
        rows = [jnp.sum(jnp.where(bucket == b, g, 0.0), axis=0, keepdims=True) for b in range(N_BUCKETS)]
        colsum = jnp.sum(jnp.concatenate(rows, axis=0), axis=1, keepdims=True)
        return jnp.where(lane == h, colsum, out)

    o_ref[...] = lax.fori_loop(0, N_Q_HEADS, head, jnp.zeros((N_BUCKETS, 128), F32))


def _bias_grad(g_bias, bucket):
    out = pl.pallas_call(
        functools.partial(_bias_grad_body), name="bias_grad", grid=(1,),
        in_specs=[_whole(g_bias.shape), _whole(bucket.shape)],
        out_specs=_whole((N_BUCKETS, 128)),
        out_shape=jax.ShapeDtypeStruct((N_BUCKETS, 128), F32),
        compiler_params=_params(("arbitrary",)),
    )(g_bias, bucket)
    return out[:, :N_Q_HEADS]


def _head_logits(qh, kk, bias_h, first_block):
    s = lax.dot_general(qh, kk, _DIMS["nt"], preferred_element_type=F32) * ATTN_SCALE + bias_h
    col = lax.broadcasted_iota(jnp.int32, s.shape, 1)
    return jnp.where(jnp.logical_and(first_block, col < BLOCK), NEG_INF, s)


def _attn_fwd_body(sink_ref, q_ref, kp_ref, kc_ref, vp_ref, vc_ref, bias_ref, o_ref, lse_ref):
    n = pl.program_id(0)
    outs, lses = [], []
    for kv in range(N_KV_HEADS):
        cs = slice(kv * HEAD_DIM, (kv + 1) * HEAD_DIM)
        kk = jnp.concatenate([kp_ref[:, cs], kc_ref[:, cs]], axis=0).astype(BF16)
        vv = jnp.concatenate([vp_ref[:, cs], vc_ref[:, cs]], axis=0).astype(BF16)
        for g in range(Q_PER_KV):
            h = kv * Q_PER_KV + g
            qh = q_ref[:, h * HEAD_DIM:(h + 1) * HEAD_DIM].astype(BF16)
            s = _head_logits(qh, kk, bias_ref[h], n == 0)
            sink = sink_ref[0, h]
            m = jnp.maximum(jnp.max(s, axis=1, keepdims=True), sink)
            p = jnp.exp(s - m)
            den = jnp.sum(p, axis=1, keepdims=True) + jnp.exp(sink - m)
            p = p / den
            outs.append(jnp.dot(p.astype(BF16), vv, preferred_element_type=F32))
            lses.append(m + jnp.log(den))
    o_ref[...] = jnp.concatenate(outs, axis=1).astype(o_ref.dtype)
    lse_ref[...] = jnp.concatenate(lses, axis=1)


def _attn_specs(nb):
    prev = lambda n: jnp.maximum(jnp.minimum(n, nb - 1) - 1, 0)
    cur = lambda n: jnp.minimum(n, nb - 1)
    return [pl.BlockSpec((BLOCK, D_ATTN), lambda n: (cur(n), C_Q // D_ATTN)),
            pl.BlockSpec((BLOCK, D_KV), lambda n: (prev(n), C_K // D_KV)),
            pl.BlockSpec((BLOCK, D_KV), lambda n: (cur(n), C_K // D_KV)),
            pl.BlockSpec((BLOCK, D_KV), lambda n: (prev(n), C_V // D_KV)),
            pl.BlockSpec((BLOCK, D_KV), lambda n: (cur(n), C_V // D_KV))]


def _attn_forward(proj, sinks, bias):
    t = proj.shape[0]
    nb = t // BLOCK
    return pl.pallas_call(
        functools.partial(_attn_fwd_body), name="attn_fwd", grid=(nb,),
        in_specs=[pl.BlockSpec(memory_space=pltpu.SMEM)] + _attn_specs(nb) + [_whole(bias.shape)],
        out_specs=[pl.BlockSpec((BLOCK, D_ATTN), lambda n: (n, 0)),
                   pl.BlockSpec((BLOCK, N_Q_HEADS), lambda n: (n, 0))],
        out_shape=[jax.ShapeDtypeStruct((t, D_ATTN), BF16), jax.ShapeDtypeStruct((t, N_Q_HEADS), F32)],
        compiler_params=_params(("parallel",)),
    )(sinks, proj, proj, proj, proj, proj, bias)


def _attn_bwd_body(sink_ref, q_ref, kp_ref, kc_ref, vp_ref, vc_ref, bias_ref, do_ref, o_ref, lse_ref, dproj_in,
                   dq_ref, dkv_ref, gbias_ref, gsink_ref, carry_ref, *, nb):
    del dproj_in
    n = pl.program_id(0)

    @pl.when(n == 0)
    def _():
        gbias_ref[...] = jnp.zeros_like(gbias_ref)
        gsink_ref[...] = jnp.zeros_like(gsink_ref)
        carry_ref[...] = jnp.zeros_like(carry_ref)

    @pl.when(n < nb)
    def _():
        lane = lax.broadcasted_iota(jnp.int32, (1, 128), 1)
        dqs, dks, dvs = [], [], []
        gsink = jnp.zeros((1, 128), F32)
        for kv in range(N_KV_HEADS):
            cs = slice(kv * HEAD_DIM, (kv + 1) * HEAD_DIM)
            kk = jnp.concatenate([kp_ref[:, cs], kc_ref[:, cs]], axis=0).astype(BF16)
            vv = jnp.concatenate([vp_ref[:, cs], vc_ref[:, cs]], axis=0).astype(BF16)
            dk = jnp.zeros((2 * BLOCK, HEAD_DIM), F32)
            dv = jnp.zeros((2 * BLOCK, HEAD_DIM), F32)
            for g in range(Q_PER_KV):
                h = kv * Q_PER_KV + g
                hs = slice(h * HEAD_DIM, (h + 1) * HEAD_DIM)
                qh = q_ref[:, hs].astype(BF16)
                s = _head_logits(qh, kk, bias_ref[h], n == 0)
                lse = lse_ref[:, h:h + 1]
                p = jnp.exp(s - lse)
                do = do_ref[:, hs].astype(F32)
                delta = jnp.sum(do * o_ref[:, hs].astype(F32), axis=1, keepdims=True)
                dob = do.astype(BF16)
                dp = lax.dot_general(dob, vv, _DIMS["nt"], preferred_element_type=F32)
                dl = p * (dp - delta)
                gbias_ref[h] += dl
                psink = jnp.exp(sink_ref[0, h] - lse)
                gsink = gsink + jnp.where(lane == h, -jnp.sum(psink * delta), 0.0)
                dlb = dl.astype(BF16)
                dqs.append(jnp.dot(dlb, kk, preferred_element_type=F32) * ATTN_SCALE)
                dk = dk + lax.dot_general(dlb, qh, _DIMS["tn"], preferred_element_type=F32) * ATTN_SCALE
                dv = dv + lax.dot_general(p.astype(BF16), dob, _DIMS["tn"], preferred_element_type=F32)
            dks.append(dk)
            dvs.append(dv)
        dq_ref[...] = jnp.concatenate(dqs, axis=1).astype(dq_ref.dtype)
        gsink_ref[...] += gsink
        dkv = jnp.concatenate(dks + dvs, axis=1)
        dkv_ref[...] = (carry_ref[...] + dkv[:BLOCK]).astype(dkv_ref.dtype)
        carry_ref[...] = dkv[BLOCK:]

    @pl.when(n == nb)
    def _():
        dkv_ref[...] = carry_ref[...].astype(dkv_ref.dtype)


def _attn_backward(proj, sinks, bias, d_attn, attn, lse, dproj):
    t = proj.shape[0]
    nb = t // BLOCK
    cur = lambda n: jnp.minimum(n, nb - 1)
    return pl.pallas_call(
        functools.partial(_attn_bwd_body, nb=nb), name="attn_bwd", grid=(nb + 1,),
        in_specs=[pl.BlockSpec(memory_space=pltpu.SMEM)] + _attn_specs(nb) + [
            _whole(bias.shape),
            pl.BlockSpec((BLOCK, D_ATTN), lambda n: (cur(n), 0)),
            pl.BlockSpec((BLOCK, D_ATTN), lambda n: (cur(n), 0)),
            pl.BlockSpec((BLOCK, N_Q_HEADS), lambda n: (cur(n), 0)),
            pl.BlockSpec(memory_space=pl.ANY)],
        out_specs=[pl.BlockSpec((BLOCK, D_ATTN), lambda n: (cur(n), C_Q // D_ATTN)),
                   pl.BlockSpec((BLOCK, 2 * D_KV), lambda n: (jnp.maximum(n - 1, 0), 0)),
                   _whole(bias.shape), _whole((1, 128))],
        out_shape=[jax.ShapeDtypeStruct(dproj.shape, dproj.dtype), jax.ShapeDtypeStruct((t, 2 * D_KV), dproj.dtype),
                   jax.ShapeDtypeStruct(bias.shape, F32), jax.ShapeDtypeStruct((1, 128), F32)],
        scratch_shapes=[pltpu.VMEM((BLOCK, 2 * D_KV), F32)],
        input_output_aliases={10: 0},
        compiler_params=_params(("arbitrary",)),
    )(sinks, proj, proj, proj, proj, proj, bias, d_attn, attn, lse, dproj)


ROWS = 256


def _rowwise(body, name, t, ins, outs, aliases=None):
    rows = min(ROWS, t)

    def col_spec(w, c0):
        if c0 == "T":
            return pl.BlockSpec((w, rows), lambda i: (0, i))
        if c0 % w == 0:
            return pl.BlockSpec((rows, w), lambda i: (i, c0 // w))
        return pl.BlockSpec((pl.Element(rows), pl.Element(w)), lambda i: (i * rows, c0))

    in_specs, args = [], []
    for a, w, c0 in ins:
        args.append(a)
        if w is None:
            in_specs.append(pl.BlockSpec(memory_space=pl.ANY) if c0 == "any" else _whole(a.shape))
        else:
            in_specs.append(col_spec(w, c0))
    out_specs, out_shape = [], []
    for shape, dtype, w, c0 in outs:
        out_shape.append(jax.ShapeDtypeStruct(shape, dtype))
        out_specs.append(_whole(shape) if w is None else col_spec(w, c0))
    accum = any(o[2] is None for o in outs)
    return pl.pallas_call(
        body, name=name, grid=(t // rows,), in_specs=in_specs, out_specs=out_specs, out_shape=out_shape,
        input_output_aliases=aliases or {},
        compiler_params=_params(("arbitrary",) if accum else ("parallel",)),
    )(*args)


def _f32(ref, *idx):
    return (ref[idx] if idx else ref[...]).astype(F32)


def _store_with_transpose(val, ref, t_ref):
    val = val.astype(ref.dtype)
    ref[...] = val
    t_ref[...] = val.T


def _ssm_gate_fwd_body(glu_ref, z_ref, h_ref, ht_ref):
    a, b = _f32(glu_ref, slice(None), slice(0, D_SSM)), _f32(glu_ref, slice(None), slice(D_SSM, None))
    _store_with_transpose((a * _sigmoid(b)) * _silu_and_grad(_f32(z_ref))[0], h_ref, ht_ref)


def _ssm_gate_bwd_body(dh_ref, glu_ref, z_ref, dproj_in, dglu_ref, dz_ref):
    del dproj_in
    a, b = _f32(glu_ref, slice(None), slice(0, D_SSM)), _f32(glu_ref, slice(None), slice(D_SSM, None))
    sb = _sigmoid(b)
    silu, dsilu = _silu_and_grad(_f32(z_ref))
    dh = _f32(dh_ref)
    dg = dh * silu
    dz_ref[...] = (dh * (a * sb) * dsilu).astype(dz_ref.dtype)
    dglu_ref[:, :D_SSM] = (dg * sb).astype(dglu_ref.dtype)
    dglu_ref[:, D_SSM:] = (dg * a * sb * (1.0 - sb)).astype(dglu_ref.dtype)


def _attn_gate_fwd_body(attn_ref, z_ref, h_ref, ht_ref):
    _store_with_transpose(_f32(attn_ref) * _silu_and_grad(_f32(z_ref))[0], h_ref, ht_ref)


def _attn_gate_bwd_body(dh_ref, attn_ref, z_ref, dproj_in, dattn_ref, dz_ref):
    del dproj_in
    silu, dsilu = _silu_and_grad(_f32(z_ref))
    dh = _f32(dh_ref)
    dattn_ref[...] = (dh * silu).astype(dattn_ref.dtype)
    dz_ref[...] = (dh * _f32(attn_ref) * dsilu).astype(dz_ref.dtype)


def _merge_fwd_body(bs_ref, ba_ref, gl_ref, m_ref, mt_ref):
    gs = _sigmoid(_f32(gl_ref, slice(None), slice(0, D_MODEL)))
    ga = _sigmoid(_f32(gl_ref, slice(None), slice(D_MODEL, None)))
    _store_with_transpose(gs * _f32(bs_ref) + ga * _f32(ba_ref), m_ref, mt_ref)


def _merge_bwd_body(dm_ref, bs_ref, ba_ref, gl_ref, dbs_ref, dba_ref, dgl_ref):
    gs = _sigmoid(_f32(gl_ref, slice(None), slice(0, D_MODEL)))
    ga = _sigmoid(_f32(gl_ref, slice(None), slice(D_MODEL, None)))
    dm = _f32(dm_ref)
    dbs_ref[...] = (dm * gs).astype(dbs_ref.dtype)
    dba_ref[...] = (dm * ga).astype(dba_ref.dtype)
    dgl_ref[:, :D_MODEL] = (dm * _f32(bs_ref) * gs * (1.0 - gs)).astype(dgl_ref.dtype)
    dgl_ref[:, D_MODEL:] = (dm * _f32(ba_ref) * ga * (1.0 - ga)).astype(dgl_ref.dtype)


def _ln_loss_body(x_ref, o_ref, tgt_ref, gain_ref, bias_ref, dr_ref, loss_ref, dgain_ref, dbias_ref):
    @pl.when(pl.program_id(0) == 0)
    def _():
        loss_ref[...] = jnp.zeros_like(loss_ref)
        dgain_ref[...] = jnp.zeros_like(dgain_ref)
        dbias_ref[...] = jnp.zeros_like(dbias_ref)

    r = ALPHA * x_ref[...] + o_ref[...].astype(F32)
    mu = jnp.mean(r, axis=1, keepdims=True)
    rc = r - mu
    var = jnp.mean(rc * rc, axis=1, keepdims=True)
    rstd = lax.rsqrt(var + LN_EPS)
    xhat = rc * rstd
    gain = gain_ref[...]
    err = xhat * gain + bias_ref[...] - tgt_ref[...]
    loss_ref[...] += 0.5 * jnp.sum(jnp.mean(err * err, axis=1, keepdims=True), axis=0, keepdims=True)
    dy = err * (1.0 / D_MODEL)
    dgain_ref[...] += jnp.sum(dy * xhat, axis=0, keepdims=True)
    dbias_ref[...] += jnp.sum(dy, axis=0, keepdims=True)
    dxhat = dy * gain
    m1 = jnp.mean(dxhat, axis=1, keepdims=True)
    m2 = jnp.mean(dxhat * xhat, axis=1, keepdims=True)
    dr_ref[...] = (rstd * (dxhat - m1 - xhat * m2)).astype(dr_ref.dtype)


def _place_body(piece_ref, dproj_in, o_ref):
    del dproj_in
    o_ref[...] = piece_ref[...]


def _adamw_update(w_ref, m_ref, v_ref, g, g_ref, d_ref, nm_ref, nv_ref):
    m = ADAM_B1 * m_ref[...] + (1.0 - ADAM_B1) * g
    v = ADAM_B2 * v_ref[...] + (1.0 - ADAM_B2) * (g * g)
    m_hat = m / (1.0 - ADAM_B1 ** ADAM_STEP)
    v_hat = v / (1.0 - ADAM_B2 ** ADAM_STEP)
    g_ref[...] = g
    d_ref[...] = -ADAM_LR * (m_hat / (jnp.sqrt(v_hat) + ADAM_EPS) + ADAM_WD * w_ref[...])
    nm_ref[...] = m
    nv_ref[...] = v


def _adamw_body(*refs, n_parts):
    w_ref, m_ref, v_ref = refs[:3]
    parts = refs[3:3 + n_parts]
    g = parts[0][...].astype(F32)
    for p in parts[1:]:
        g = g + p[...].astype(F32)
    _adamw_update(w_ref, m_ref, v_ref, g, *refs[3 + n_parts:])


def _adamw_shard_body(c_ref, w_ref, m_ref, v_ref, mine_ref, other_ref, g_ref, d_ref, nm_ref, nv_ref, *, nth):
    in_mine = pl.program_id(0) // nth == c_ref[0]
    g = jnp.where(in_mine, mine_ref[...], other_ref[...])
    _adamw_update(w_ref, m_ref, v_ref, g, g_ref, d_ref, nm_ref, nv_ref)


def _adamw_shard(w, m, v, halves, core, *, name, rows):
    shape = w.shape
    w2, m2, v2 = (a.reshape(-1, shape[-1]) for a in (w, m, v))
    r, c = w2.shape
    nth = r // 2 // rows
    assert 2 * nth * rows == r and halves[0].shape == (r // 2, c)
    spec = pl.BlockSpec((rows, c), lambda i, cr: (i, 0))
    half_spec = pl.BlockSpec((rows, c), lambda i, cr: (i % nth, 0))
    outs = pl.pallas_call(
        functools.partial(_adamw_shard_body, nth=nth), name=name,
        grid_spec=pltpu.PrefetchScalarGridSpec(
            num_scalar_prefetch=1, grid=(r // rows,),
            in_specs=[spec] * 3 + [half_spec] * 2, out_specs=[spec] * 4),
        out_shape=[jax.ShapeDtypeStruct((r, c), F32)] * 4,
        compiler_params=_params(("parallel",)),
    )(core, w2, m2, v2, *halves)
    return tuple(o.reshape(shape) for o in outs)


def _adamw(w, m, v, parts, *, name, rows=256):
    shape = w.shape
    w2, m2, v2 = (a.reshape(-1, shape[-1]) for a in (w, m, v))
    parts = [p.reshape(w2.shape) for p in parts]
    r, c = w2.shape
    rows = rows if r % rows == 0 else r
    spec = pl.BlockSpec((rows, c), lambda i: (i, 0))
    outs = pl.pallas_call(
        functools.partial(_adamw_body, n_parts=len(parts)), name=name, grid=(r // rows,),
        in_specs=[spec] * (3 + len(parts)), out_specs=[spec] * 4,
        out_shape=[jax.ShapeDtypeStruct((r, c), F32)] * 4,
        compiler_params=_params(("parallel",)),
    )(w2, m2, v2, *parts)
    return tuple(o.reshape(shape) for o in outs)


BIG = ("w_in", "w_glu", "w_branch_ssm", "w_branch_attn", "w_out")
SHARD_AXIS = dict(w_in=1, w_glu=1, w_branch_ssm=1, w_branch_attn=1, w_out=0)
HBM = pl.BlockSpec(memory_space=pl.ANY)


def _position():
    x, y, c = lax.axis_index("x"), lax.axis_index("y"), lax.axis_index("c")
    other_chips = [(1 - x, y), (x, 1 - y), (1 - x, 1 - y)]
    return x, y, c, other_chips


def _window(ref, axis, shard, n_shards, half=None):
    rows, cols = ref.shape[-2:]
    sel = [slice(None), slice(None)]
    size = ref.shape[-2 + axis] // n_shards
    sel[axis] = pl.ds(pl.multiple_of(shard * size, 128), size)
    if half is not None:
        hsize = ref.shape[-1 - axis] // 2
        sel[1 - axis] = pl.ds(pl.multiple_of(half * hsize, 128), hsize)
    return ref.at[tuple(sel)]


def _half(ref, axis, half):
    hsize = ref.shape[-1 - axis] // 2
    sel = [slice(None), slice(None)]
    sel[1 - axis] = pl.ds(pl.multiple_of(half * hsize, 128), hsize)
    return ref.at[tuple(sel)]


def _remote(src, dst, send_sem, recv_sem, device):
    return pltpu.make_async_remote_copy(src_ref=src, dst_ref=dst, send_sem=send_sem, recv_sem=recv_sem,
                                        device_id=device, device_id_type=MESH)


def _full_shapes(shards, names):
    out = []
    for k in names:
        s = list(shards[k].shape)
        s[SHARD_AXIS[k]] *= N_CHIPS
        out.append(jax.ShapeDtypeStruct(tuple(s), shards[k].dtype))
    return out


def _gather_exchange(shards, names):
    axes = tuple(SHARD_AXIS[k] for k in names)
    n = len(names)

    def copies(ins, outs, sems):
        send_sems, recv_sems, local_sems = sems
        x, y, c, chips = _position()
        me = 2 * x + y
        remote, local = [], []
        for w, ax in enumerate(axes):
            for r, (px, py) in enumerate(chips):
                remote.append(_remote(_half(ins[w], ax, c), _window(outs[w], ax, me, N_CHIPS, c),
                                      send_sems.at[3 * w + r], recv_sems.at[3 * w + r], (px, py, c)))
            local.append(pltpu.make_async_copy(ins[w], _window(outs[w], ax, me, N_CHIPS), local_sems.at[w]))
        return remote, local

    return _Exchange([shards[k] for k in names], _full_shapes(shards, names), (3 * n, 3 * n, n), copies)


SEM = pl.BlockSpec(memory_space=pltpu.SEMAPHORE)


def _gather_start_body(shard_ref, land_ref, send_sems, recv_sems, shard_thru, land_thru, token, *, axis):
    del shard_thru, land_thru
    x, y, c, chips = _position()
    me = 2 * x + y
    for r, (px, py) in enumerate(chips):
        _remote(_half(shard_ref, axis, c), _window(land_ref, axis, me, N_CHIPS, c),
                send_sems.at[r], recv_sems.at[r], (px, py, c)).start()
    token[...] = jnp.zeros_like(token)


def _gather_start(shard, name):
    axis = SHARD_AXIS[name]
    full = _full_shapes({name: shard}, (name,))[0]
    return pl.pallas_call(
        functools.partial(_gather_start_body, axis=axis), name="gather_start_" + name,
        out_shape=(pltpu.SemaphoreType.DMA((3,)), pltpu.SemaphoreType.DMA((3,)), pltpu.HBM(shard.shape, shard.dtype),
                   pltpu.HBM(full.shape, full.dtype), jax.ShapeDtypeStruct((8, 128), F32)),
        in_specs=(HBM, HBM), out_specs=(SEM, SEM, HBM, HBM, pl.BlockSpec(memory_space=pltpu.VMEM)),
        input_output_aliases={0: 2, 1: 3},
        compiler_params=pltpu.CompilerParams(has_side_effects=pltpu.SideEffectType.DATAFLOW_SIDE_EFFECTING),
    )(pltpu.with_memory_space_constraint(shard, pltpu.HBM),
      pltpu.with_memory_space_constraint(lax.empty(full.shape, full.dtype), pltpu.HBM))


def _gather_wait_body(shard_ref, land_ref, send_sems, recv_sems, after_ref, shard_dead, landed, *, axis):
    del after_ref, shard_dead, landed
    x, y, c, chips = _position()
    for r, (px, py) in enumerate(chips):
        cp = _remote(_half(shard_ref, axis, c), _window(land_ref, axis, 2 * px + py, N_CHIPS, c),
                     send_sems.at[r], recv_sems.at[r], (px, py, c))
        cp.wait_send()
        cp.wait_recv()


def _gather_wait(send_sems, recv_sems, shard_thru, land_thru, after, name):
    return pl.pallas_call(
        functools.partial(_gather_wait_body, axis=SHARD_AXIS[name]), name="gather_wait_" + name,
        out_shape=(pltpu.HBM(shard_thru.shape, shard_thru.dtype), pltpu.HBM(land_thru.shape, land_thru.dtype)),
        in_specs=(HBM, HBM, SEM, SEM, HBM), out_specs=(HBM, HBM), input_output_aliases={0: 0, 1: 1},
        compiler_params=pltpu.CompilerParams(has_side_effects=pltpu.SideEffectType.DATAFLOW_SIDE_EFFECTING),
    )(shard_thru, land_thru, send_sems, recv_sems, after)[1]


def _pass_on_body(*refs, axes, n_own):
    n = len(axes)
    ins, own = refs[:n], refs[n:n + n_own]
    refs = refs[:n] + refs[n + n_own:]
    outs = refs[n:2 * n]
    sbuf, rbuf = refs[2 * n:5 * n], refs[5 * n:8 * n]
    obuf = refs[8 * n:8 * n + n_own]
    send_sems, recv_sems, load_sems, store_sems, own_sems, placed_sems = refs[8 * n + n_own:]
    x, y, c, chips = _position()
    fetched = [pltpu.make_async_copy(own[w], obuf[w], own_sems.at[w]) for w in range(n_own)]
    for cp in fetched:
        cp.start()
    region = lambda ref, w, r, half: _window(ref, axes[w], 2 * chips[r][0] + chips[r][1], N_CHIPS, half)
    pairs = [(w, r) for w in range(n) for r in range(3)]
    loads = [pltpu.make_async_copy(region(ins[w], w, r, c), sbuf[3 * w + r], load_sems.at[3 * w + r]) for w, r in pairs]
    for cp in loads:
        cp.start()
    placed = []
    for w, cp in enumerate(fetched):
        cp.wait()
        placed.append(pltpu.make_async_copy(obuf[w], _window(outs[w], axes[w], 2 * x + y, N_CHIPS), placed_sems.at[w]))
        placed[-1].start()
    sends = []
    for i, cp in enumerate(loads):
        cp.wait()
        sends.append(_remote(sbuf[i], rbuf[i], send_sems.at[i], recv_sems.at[i], (x, y, 1 - c)))
        sends[-1].start()
    stores = []
    for i, (w, r) in enumerate(pairs):
        sends[i].wait_recv()
        stores.append(pltpu.make_async_copy(rbuf[i], region(outs[w], w, r, 1 - c), store_sems.at[i]))
        stores[-1].start()
    for cp in sends:
        cp.wait_send()
    for cp in stores + placed:
        cp.wait()


def _pass_on(fulls, names, own=()):
    axes = tuple(SHARD_AXIS[k] for k in names)
    n = len(names)
    assert len(own) in (0, n)
    bufs = []
    for a, ax in zip(fulls, axes):
        s = list(a.shape)
        s[ax] //= N_CHIPS
        s[1 - ax] //= 2
        bufs += [pltpu.VMEM(tuple(s), a.dtype)] * 3
    return pl.pallas_call(
        functools.partial(_pass_on_body, axes=axes, n_own=len(own)), name="pass_on_" + names[0],
        in_specs=[HBM] * (n + len(own)), out_specs=[HBM] * n,
        out_shape=[jax.ShapeDtypeStruct(a.shape, a.dtype) for a in fulls],
        scratch_shapes=bufs + bufs + [pltpu.VMEM(o.shape, o.dtype) for o in own]
        + [pltpu.SemaphoreType.DMA((3 * n,))] * 4 + [pltpu.SemaphoreType.DMA((n,))] * 2,
        input_output_aliases={i: i for i in range(n)},
        compiler_params=pltpu.CompilerParams(vmem_limit_bytes=VMEM_LIMIT),
    )(*fulls, *own)


def _chip_exchange(pairs, names):
    axes = tuple(SHARD_AXIS[k] for k in names)
    n = len(names)
    out_shape = []
    for p, ax in zip(pairs, axes):
        s = list(p.shape)
        s[ax] //= N_CHIPS
        out_shape.append(jax.ShapeDtypeStruct((4, *s), p.dtype))

    def copies(ins, outs, sems):
        send_sems, recv_sems, local_sems = sems
        x, y, c, chips = _position()
        me = 2 * x + y
        remote, local = [], []
        for w, ax in enumerate(axes):
            for r, (px, py) in enumerate(chips):
                remote.append(_remote(_window(ins[w], ax, 2 * px + py, N_CHIPS), outs[w].at[r],
                                      send_sems.at[3 * w + r], recv_sems.at[3 * w + r], (px, py, c)))
            local.append(pltpu.make_async_copy(_window(ins[w], ax, me, N_CHIPS), outs[w].at[3], local_sems.at[w]))
        return remote, local

    return _Exchange(pairs, out_shape, (3 * n, 3 * n, n), copies)


def _half_tile(n, h, nt, axis):
    return h * nt + n if axis == 1 else 2 * n + h


def _sibling_stream(n, nt, stage, recv, send_sems, recv_sems, credit, produce, consume):
    x, y, c, _ = _position()
    sibling = (x, y, 1 - c)

    def copy(slot):
        return _remote(stage.at[slot], recv.at[slot], send_sems.at[slot], recv_sems.at[slot], sibling)

    @pl.when(n < nt)
    def _():
        slot = n % 2

        @pl.when(n >= 2)
        def _():
            copy(slot).wait_send()
            pl.semaphore_wait(credit, 1)

        stage[slot] = produce().astype(stage.dtype)
        copy(slot).start()

    @pl.when(n >= 1)
    def _():
        slot = (n - 1) % 2
        copy(slot).wait_recv()
        consume(recv[slot])

        @pl.when(n - 1 < nt - 2)
        def _():
            pl.semaphore_signal(credit, 1, device_id=sibling, device_id_type=MESH)

    @pl.when(n == nt)
    def _():
        for slot in range(min(2, nt)):
            copy(slot).wait_send()


def _pair_reduce_body(c_ref, mine_ref, theirs_ref, out_ref, stage, recv, send_sems, recv_sems, credit, *, nt):
    del c_ref

    def consume(got):
        out_ref[...] = (mine_ref[...] + got.astype(F32)).astype(out_ref.dtype)

    _sibling_stream(pl.program_id(0), nt, stage, recv, send_sems, recv_sems, credit,
                    lambda: theirs_ref[...], consume)


def _pair_reduce(grad, core, axis, *, name, rows):
    r, c = grad.shape
    nt = r // 2 // rows
    assert nt * rows * 2 == r and (axis == 1 or rows == r // (2 * N_CHIPS))
    tile = lambda n, h: _half_tile(n, h, nt, axis)
    return pl.pallas_call(
        functools.partial(_pair_reduce_body, nt=nt), name=name,
        grid_spec=pltpu.PrefetchScalarGridSpec(
            num_scalar_prefetch=1, grid=(nt + 1,),
            in_specs=[pl.BlockSpec((rows, c), lambda n, cr: (tile(jnp.maximum(n - 1, 0), cr[0]), 0)),
                      pl.BlockSpec((rows, c), lambda n, cr: (tile(jnp.minimum(n, nt - 1), 1 - cr[0]), 0))],
            out_specs=pl.BlockSpec((rows, c), lambda n, cr: (jnp.maximum(n - 1, 0), 0)),
            scratch_shapes=[pltpu.VMEM((2, rows, c), BF16), pltpu.VMEM((2, rows, c), BF16),
                            pltpu.SemaphoreType.DMA((2,)), pltpu.SemaphoreType.DMA((2,)),
                            pltpu.SemaphoreType.REGULAR]),
        out_shape=jax.ShapeDtypeStruct((r // 2, c), BF16),
        compiler_params=_params(("arbitrary",)),
    )(core, grad, grad)


def _chip_add_share_body(c_ref, s0, s1, s2, s3, mine_ref, other_ref, stage, recv, send_sems, recv_sems, credit, *, nt):
    del c_ref

    def produce():
        total = s3[...].astype(F32) + s0[...].astype(F32) + s1[...].astype(F32) + s2[...].astype(F32)
        mine_ref[...] = total
        return total

    def consume(got):
        other_ref[...] = got

    _sibling_stream(pl.program_id(0), nt, stage, recv, send_sems, recv_sems, credit, produce, consume)


def _chip_add_share(slots, core, *, name, rows):
    _, r, c = slots.shape
    nt = r // rows
    assert nt * rows == r
    send = lambda j: pl.BlockSpec((None, rows, c), lambda n, cr: (j, jnp.minimum(n, nt - 1), 0))
    return pl.pallas_call(
        functools.partial(_chip_add_share_body, nt=nt), name=name,
        grid_spec=pltpu.PrefetchScalarGridSpec(
            num_scalar_prefetch=1, grid=(nt + 1,),
            in_specs=[send(j) for j in range(4)],
            out_specs=[pl.BlockSpec((rows, c), lambda n, cr: (jnp.minimum(n, nt - 1), 0)),
                       pl.BlockSpec((rows, c), lambda n, cr: (jnp.maximum(n - 1, 0), 0))],
            scratch_shapes=[pltpu.VMEM((2, rows, c), F32), pltpu.VMEM((2, rows, c), F32),
                            pltpu.SemaphoreType.DMA((2,)), pltpu.SemaphoreType.DMA((2,)),
                            pltpu.SemaphoreType.REGULAR]),
        out_shape=[jax.ShapeDtypeStruct((r, c), F32)] * 2,
        compiler_params=_params(("arbitrary",)),
    )(core, slots, slots, slots, slots)


PAIR_ROWS = dict(w_in=64, w_glu=128, w_branch_ssm=128, w_branch_attn=128, w_out=256)
SHARE_ROWS = dict(w_in=128, w_glu=128, w_branch_ssm=128, w_branch_attn=128, w_out=64)


def _pair_sums(grads, names, core):
    return [_pair_reduce(grads[k], core, SHARD_AXIS[k], name="pair_reduce_" + k, rows=PAIR_ROWS[k]) for k in names]


def _shard_halves(slots, names, core):
    return {k: _chip_add_share(s, core, name="chip_add_share_" + k, rows=SHARE_ROWS[k]) for k, s in zip(names, slots)}


SMALL = (("ssm_lambda_re", (1, 64, 64)), ("ssm_lambda_im", (1, 64, 64)), ("ssm_b_re", (1, 64, 64, 16)),
         ("ssm_b_im", (1, 64, 64, 16)), ("ssm_c_re", (1, 64, 16, 64)), ("ssm_c_im", (1, 64, 16, 64)),
         ("ssm_d", (1, 1024)), ("ssm_log_step", (1, 64)), ("attn_sinks", (1, 16)), ("rel_bias_table", (32, 16)),
         ("ln_gain", (1, 2048)), ("ln_bias", (1, 2048)))
SMALL_SIZE = sum(int(np.prod(s)) for _, s in SMALL)
PACK_ROWS = -(-(SMALL_SIZE + 1) // (8 * 128)) * 8


def _pack(values, extra=None):
    flat = [values[k].reshape(-1).astype(F32) for k, _ in SMALL]
    flat.append(jnp.zeros((1,), F32) if extra is None else extra.reshape(1))
    flat.append(jnp.zeros((PACK_ROWS * 128 - SMALL_SIZE - 1,), F32))
    return jnp.concatenate(flat).reshape(PACK_ROWS, 128)


def _unpack(packed):
    flat = packed.reshape(-1)
    out, off = {}, 0
    for k, s in SMALL:
        size = int(np.prod(s))
        out[k] = flat[off:off + size].reshape(s)
        off += size
    return out, flat[off]


def _small_exchange(packed):
    def copies(ins, outs, sems):
        send_sems, recv_sems, local_sems = sems
        x, y, c, _ = _position()
        remote = []
        for r in range(1, 8):
            peer = tuple(1 - v if (r >> s) & 1 else v for v, s in ((x, 2), (y, 1), (c, 0)))
            remote.append(_remote(ins[0], outs[0].at[r], send_sems.at[r - 1], recv_sems.at[r - 1], peer))
        return remote, [pltpu.make_async_copy(ins[0], outs[0].at[0], local_sems.at[0])]

    return _Exchange([packed], [jax.ShapeDtypeStruct((8, *packed.shape), F32)], (7, 7, 1), copies)


def _merge_exchanges(a, b):
    na, ma, sa = len(a.ins), len(a.out_shape), len(a.n_sems)

    def copies(ins, outs, sems):
        ra, la = a.copies(ins[:na], outs[:ma], sems[:sa])
        rb, lb = b.copies(ins[na:], outs[ma:], sems[sa:])
        return ra + rb, la + lb

    return _Exchange(a.ins + b.ins, a.out_shape + b.out_shape, a.n_sems + b.n_sems, copies)


def _small_sum_body(slots_ref, o_ref):
    x, y, c, _ = _position()
    me = 4 * x + 2 * y + c
    acc = slots_ref[me]
    for s in range(1, 8):
        acc = acc + slots_ref[jnp.bitwise_xor(me, s)]
    o_ref[...] = acc


def _small_sum(slots):
    vm = pl.BlockSpec(memory_space=pltpu.VMEM)
    return pl.pallas_call(
        functools.partial(_small_sum_body), name="small_sum",
        in_specs=[vm], out_specs=vm, out_shape=jax.ShapeDtypeStruct(slots.shape[1:], F32),
        compiler_params=pltpu.CompilerParams(vmem_limit_bytes=VMEM_LIMIT),
    )(slots)


LATE = BIG[1:]


def _local_step(x, target, shards, core, chip, lam_re, lam_im, b_re, b_im, c_re, c_im, d_skip,
                log_step, sinks, rel_bias_table, ln_gain, ln_bias):
    t = x.shape[0]
    send_sems, recv_sems, shard_thru, land_thru, token = _gather_start(shards["w_in"], "w_in")
    xb, xbt = _cast_and_transpose(x, token, name="cast_x")
    bucket = jnp.asarray(_bucket_table())
    p_re, p_im, b_bd, c_bd = _ssm_prepare(lam_re, lam_im, b_re, b_im, c_re, c_im, log_step)
    bias = _bias_build(rel_bias_table, bucket)

    act = dict(out_dtype=BF16)
    proj = _proj_shards(xb, shards["w_in"], chip, (0,), name="proj_own")
    landed = _gather_wait(send_sems, recv_sems, shard_thru, land_thru, proj, "w_in")
    (w_in,) = _pass_on([landed], BIG[:1], own=[shards["w_in"]])
    proj, landed = _proj_shards(xb, w_in, chip, (1, 2, 3), name="proj", base=proj,
                                exchange=_gather_exchange(shards, LATE))
    w_glu, w_bs, w_ba, w_out = _pass_on(list(landed), LATE)
    y_ssm, g_in, g_in_t, car_re, car_im = _ssm_forward(proj, p_re, p_im, b_bd, c_bd, d_skip)
    glu = _matmul(g_in, w_glu, "nn", name="glu", tm=1024, tn=2048, **act)
    h_ssm, h_ssm_t = _rowwise(functools.partial(_ssm_gate_fwd_body), "ssm_gate_fwd", t,
                              [(glu, 2 * D_SSM, 0), (proj, D_SSM, C_ZS)],
                              [((t, D_SSM), BF16, D_SSM, 0), ((D_SSM, t), BF16, D_SSM, "T")])
    attn, lse = _attn_forward(proj, sinks, bias)
    h_attn, h_attn_t = _rowwise(functools.partial(_attn_gate_fwd_body), "attn_gate_fwd", t,
                                [(attn, D_ATTN, 0), (proj, D_ATTN, C_ZA)],
                                [((t, D_ATTN), BF16, D_ATTN, 0), ((D_ATTN, t), BF16, D_ATTN, "T")])
    bs = _matmul(h_ssm, w_bs, "nn", name="branch_ssm", tm=1024, tn=2048, **act)
    ba = _matmul(h_attn, w_ba, "nn", name="branch_attn", tm=1024, tn=2048, **act)
    gl_in = (proj, 2 * D_MODEL, C_GL)
    merged, merged_t = _rowwise(functools.partial(_merge_fwd_body), "merge_fwd", t,
                                [(bs, D_MODEL, 0), (ba, D_MODEL, 0), gl_in],
                                [((t, D_MODEL), BF16, D_MODEL, 0), ((D_MODEL, t), BF16, D_MODEL, "T")])
    out = _matmul(merged, w_out, "nn", name="out_proj", tm=1024, tn=1024, **act)
    drb, loss, g_gain, g_bias = _rowwise(
        functools.partial(_ln_loss_body), "ln_loss", t,
        [(x, D_MODEL, 0), (out, D_MODEL, 0), (target, D_MODEL, 0), (ln_gain, None, 0), (ln_bias, None, 0)],
        [((t, D_MODEL), BF16, D_MODEL, 0), ((1, 1), F32, None, 0),
         ((1, D_MODEL), F32, None, 0), ((1, D_MODEL), F32, None, 0)])

    g_w_out = _matmul(merged_t, drb, "nn", name="g_w_out", tm=512, tn=512)
    d_merged = _matmul(drb, w_out, "nt", name="d_merged", tm=1024, tn=1024, **act)
    d_bs, d_ba, dproj = _rowwise(
        functools.partial(_merge_bwd_body), "merge_bwd", t,
        [(d_merged, D_MODEL, 0), (bs, D_MODEL, 0), (ba, D_MODEL, 0), gl_in],
        [((t, D_MODEL), BF16, D_MODEL, 0), ((t, D_MODEL), BF16, D_MODEL, 0),
         ((t, D_IN), BF16, 2 * D_MODEL, C_GL)])
    g_w_bs = _matmul(h_ssm_t, d_bs, "nn", name="g_w_branch_ssm", tm=512, tn=512)
    d_h_ssm = _matmul(d_bs, w_bs, "nt", name="d_h_ssm", tm=1024, tn=1024, **act)
    g_w_ba = _matmul(h_attn_t, d_ba, "nn", name="g_w_branch_attn", tm=512, tn=512)
    d_h_attn = _matmul(d_ba, w_ba, "nt", name="d_h_attn", tm=1024, tn=1024, **act)

    d_attn, dproj = _rowwise(
        functools.partial(_attn_gate_bwd_body), "attn_gate_bwd", t,
        [(d_h_attn, D_ATTN, 0), (attn, D_ATTN, 0), (proj, D_ATTN, C_ZA), (dproj, None, "any")],
        [((t, D_ATTN), BF16, D_ATTN, 0), ((t, D_IN), BF16, D_ATTN, C_ZA)], aliases={3: 1})
    dproj, d_kv, g_bias_full, g_sinks = _attn_backward(proj, sinks, bias, d_attn, attn, lse, dproj)
    (dproj,) = _rowwise(functools.partial(_place_body), "place_dkv", t,
                        [(d_kv, 2 * D_KV, 0), (dproj, None, "any")],
                        [((t, D_IN), BF16, 2 * D_KV, C_K)], aliases={1: 0})
    g_table = _bias_grad(g_bias_full, bucket)

    d_glu, dproj = _rowwise(
        functools.partial(_ssm_gate_bwd_body), "ssm_gate_bwd", t,
        [(d_h_ssm, D_SSM, 0), (glu, 2 * D_SSM, 0), (proj, D_SSM, C_ZS), (dproj, None, "any")],
        [((t, 2 * D_SSM), BF16, 2 * D_SSM, 0), ((t, D_IN), BF16, D_SSM, C_ZS)], aliases={3: 1})
    g_w_glu = _matmul(g_in_t, d_glu, "nn", name="g_w_glu", tm=512, tn=512)
    d_g_in = _matmul(d_glu, w_glu, "nt", name="d_g_in", tm=1024, tn=1024, **act)
    dproj, g_bbd, g_cbd, g_lam_re, g_lam_im, g_d = _ssm_backward(
        proj, d_g_in, y_ssm, car_re, car_im, p_re, p_im, b_bd, c_bd, d_skip, dproj)
    g_lr, g_li, g_br, g_bi, g_cr, g_ci, g_ls = _ssm_param_grads(
        lam_re, lam_im, b_re, b_im, log_step, g_lam_re, g_lam_im, g_bbd, g_cbd)

    late = dict(w_glu=g_w_glu, w_branch_ssm=g_w_bs, w_branch_attn=g_w_ba, w_out=g_w_out)
    g_w_in, late_slots = _matmul(xbt, dproj, "nn", name="g_w_in", tm=512, tn=512,
                                 exchange=_chip_exchange(_pair_sums(late, LATE, core), LATE))
    small = dict(ssm_lambda_re=g_lr, ssm_lambda_im=g_li, ssm_b_re=g_br, ssm_b_im=g_bi, ssm_c_re=g_cr,
                 ssm_c_im=g_ci, ssm_d=g_d, ssm_log_step=g_ls, attn_sinks=g_sinks[:, :N_Q_HEADS],
                 rel_bias_table=g_table, ln_gain=g_gain, ln_bias=g_bias)
    last = _merge_exchanges(_chip_exchange(_pair_sums(dict(w_in=g_w_in), BIG[:1], core), BIG[:1]),
                            _small_exchange(_pack(small, loss)))
    grad_x, (in_slots, small_slots) = _matmul(dproj, w_in, "nt", name="grad_x", tm=512, tn=512, res=drb,
                                              res_scale=ALPHA, exchange=last)
    big = {**_shard_halves([in_slots], BIG[:1], core), **_shard_halves(late_slots, LATE, core)}
    return grad_x, big, _small_sum(small_slots)


WEIGHTS = ("w_in", "ssm_lambda_re", "ssm_lambda_im", "ssm_b_re", "ssm_b_im", "ssm_c_re", "ssm_c_im", "ssm_d",
           "ssm_log_step", "w_glu", "attn_sinks", "rel_bias_table", "w_branch_ssm", "w_branch_attn", "w_out",
           "ln_gain", "ln_bias")


def kernel(x, w_in, ssm_lambda_re, ssm_lambda_im, ssm_b_re, ssm_b_im, ssm_c_re, ssm_c_im, ssm_d, ssm_log_step, w_glu, attn_sinks, rel_bias_table, w_branch_ssm, w_branch_attn, w_out, ln_gain, ln_bias, loss_target, m_w_in, m_ssm_lambda_re, m_ssm_lambda_im, m_ssm_b_re, m_ssm_b_im, m_ssm_c_re, m_ssm_c_im, m_ssm_d, m_ssm_log_step, m_w_glu, m_attn_sinks, m_rel_bias_table, m_w_branch_ssm, m_w_branch_attn, m_w_out, m_ln_gain, m_ln_bias, v_w_in, v_ssm_lambda_re, v_ssm_lambda_im, v_ssm_b_re, v_ssm_b_im, v_ssm_c_re, v_ssm_c_im, v_ssm_d, v_ssm_log_step, v_w_glu, v_attn_sinks, v_rel_bias_table, v_w_branch_ssm, v_w_branch_attn, v_w_out, v_ln_gain, v_ln_bias):
    given = dict(locals())
    w = {k: given[k] for k in WEIGHTS}
    m = {k: given["m_" + k] for k in WEIGHTS}
    v = {k: given["v_" + k] for k in WEIGHTS}

    core = lax.axis_index("c").astype(jnp.int32).reshape(1)
    shards = {k: _cast(w[k][0], BF16, name="cast_" + k) for k in BIG}
    chip = (2 * lax.axis_index("x") + lax.axis_index("y")).astype(jnp.int32).reshape(1)
    grad_x, g_shard, g_packed = _local_step(
        x[0], loss_target[0], shards, core, chip, ssm_lambda_re[0], ssm_lambda_im[0], ssm_b_re[0],
        ssm_b_im[0], ssm_c_re[0], ssm_c_im[0], ssm_d, ssm_log_step, attn_sinks, rel_bias_table, ln_gain, ln_bias)
    loss_sum = _unpack(g_packed)[1]

    grad, delta, new_m, new_v = {}, {}, {}, {}
    for k in BIG:
        grad[k], delta[k], new_m[k], new_v[k] = _adamw_shard(w[k], m[k], v[k], g_shard[k], core,
                                                             name="adamw_" + k, rows=SHARE_ROWS[k])
    gs, ds, ms, vs = _adamw(_pack(w), _pack(m), _pack(v), [g_packed], name="adamw_small")
    for dst, packed in ((grad, gs), (delta, ds), (new_m, ms), (new_v, vs)):
        dst.update(_unpack(packed)[0])

    return (loss_sum, grad_x[None], *[grad[k] for k in WEIGHTS], *[delta[k] for k in WEIGHTS],
            *[new_m[k] for k in WEIGHTS], *[new_v[k] for k in WEIGHTS])
```

```python
import functools
import math

import numpy as np
import jax
import jax.numpy as jnp
from jax import lax
from jax.experimental import pallas as pl
from jax.experimental.pallas import tpu as pltpu

F32 = jnp.float32
BF16 = jnp.bfloat16

D_MODEL = 2048
D_SSM = 1024
SSM_GROUP = 16
N_GROUPS = 64
SSM_STATE = 64
N_LANES = N_GROUPS * SSM_STATE
N_Q_HEADS = 16
N_KV_HEADS = 4
HEAD_DIM = 64
Q_PER_KV = 4
D_ATTN = 1024
D_KV = 256
WINDOW = 128
BLOCK = 128
N_BUCKETS = 32
MAX_DISTANCE = 128
D_IN = 8704
ALPHA = 2.0 ** 0.25
LN_EPS = 1e-5
NEG_INF = -1e30
ATTN_SCALE = HEAD_DIM ** -0.5

C_U, C_ZS, C_Q, C_K, C_V, C_ZA, C_GL = 0, 1024, 2048, 3072, 3328, 3584, 4608

ADAM_LR = 0.001
ADAM_B1 = 0.9
ADAM_B2 = 0.999
ADAM_EPS = 1e-08
ADAM_WD = 0.01
ADAM_STEP = 10

N_CHIPS = 4
MESH = pl.DeviceIdType.MESH

SSM_CHUNK = 256
SEG_LEN = SSM_CHUNK // 8
SLAB_LANES = 512
N_SLABS = N_LANES // SLAB_LANES
SLAB_CH = D_SSM // N_SLABS

VMEM_LIMIT = 60 * 1024 * 1024


def _params(sem=None, **kw):
    return pltpu.CompilerParams(dimension_semantics=sem, vmem_limit_bytes=VMEM_LIMIT, **kw)


_DIMS = {"nn": (((1,), (0,)), ((), ())), "nt": (((1,), (1,)), ((), ())), "tn": (((0,), (0,)), ((), ()))}


class _Exchange:
    def __init__(self, ins, out_shape, n_sems, copies):
        self.ins, self.out_shape, self.n_sems, self.copies = list(ins), list(out_shape), list(n_sems), copies

    def start(self, ins, outs, sems):
        remote, local = self.copies(ins, outs, sems)
        for cp in remote + local:
            cp.start()

    def finish(self, ins, outs, sems):
        remote, local = self.copies(ins, outs, sems)
        for cp in remote:
            cp.wait_recv()
        for cp in remote:
            cp.wait_send()
        for cp in local:
            cp.wait()


def _mm_body(*refs, dims, nk, res_scale, exchange, grid):
    n_ex_in = len(exchange.ins) if exchange else 0
    n_ex_out = len(exchange.out_shape) if exchange else 0
    n_in = 2 + (res_scale is not None) + n_ex_in
    a_ref, b_ref = refs[0], refs[1]
    r_ref = refs[2] if res_scale is not None else None
    ex_in = refs[n_in - n_ex_in:n_in]
    o_ref, ex_out = refs[n_in], refs[n_in + 1:n_in + 1 + n_ex_out]
    acc_ref, ex_sems = refs[n_in + 1 + n_ex_out], refs[n_in + 2 + n_ex_out:]
    i, j, k = pl.program_id(0), pl.program_id(1), pl.program_id(2)

    if exchange:
        @pl.when(jnp.logical_and(jnp.logical_and(i == 0, j == 0), k == 0))
        def _():
            exchange.start(ex_in, ex_out, ex_sems)

    part = lax.dot_general(a_ref[...].astype(BF16), b_ref[...].astype(BF16), _DIMS[dims],
                           preferred_element_type=F32)

    def finish(acc):
        if r_ref is not None:
            acc = acc + res_scale * r_ref[...].astype(F32)
        o_ref[...] = acc.astype(o_ref.dtype)

    if nk == 1:
        finish(part)
    else:
        @pl.when(k == 0)
        def _():
            acc_ref[...] = part

        @pl.when(k > 0)
        def _():
            acc_ref[...] += part

        @pl.when(k == nk - 1)
        def _():
            finish(acc_ref[...])

    if exchange:
        @pl.when(jnp.logical_and(jnp.logical_and(i == grid[0] - 1, j == grid[1] - 1), k == grid[2] - 1))
        def _():
            exchange.finish(ex_in, ex_out, ex_sems)


def _matmul(a, b, dims, *, name, out_dtype=F32, tm=512, tn=512, tk=None, res=None, res_scale=None, exchange=None):
    if dims == "nn":
        (m, kk), n = a.shape, b.shape[1]
    elif dims == "nt":
        (m, kk), n = a.shape, b.shape[0]
    else:
        (kk, m), n = a.shape, b.shape[1]
    tm, tn = min(tm, m), min(tn, n)
    tk = kk if tk is None else min(tk, kk)
    assert m % tm == 0 and n % tn == 0 and kk % tk == 0, (name, m, n, kk, tm, tn, tk)
    nk = kk // tk
    a_spec = {"nn": pl.BlockSpec((tm, tk), lambda i, j, k: (i, k)),
              "nt": pl.BlockSpec((tm, tk), lambda i, j, k: (i, k)),
              "tn": pl.BlockSpec((tk, tm), lambda i, j, k: (k, i))}[dims]
    b_spec = {"nn": pl.BlockSpec((tk, tn), lambda i, j, k: (k, j)),
              "nt": pl.BlockSpec((tn, tk), lambda i, j, k: (j, k)),
              "tn": pl.BlockSpec((tk, tn), lambda i, j, k: (k, j))}[dims]
    in_specs, args = [a_spec, b_spec], [a, b]
    if res is not None:
        in_specs.append(pl.BlockSpec((tm, tn), lambda i, j, k: (i, j)))
        args.append(res)
    grid = (m // tm, n // tn, nk)
    out_specs = [pl.BlockSpec((tm, tn), lambda i, j, k: (i, j))]
    out_shape = [jax.ShapeDtypeStruct((m, n), out_dtype)]
    scratch = [pltpu.VMEM((tm, tn), F32)]
    if exchange:
        in_specs += [pl.BlockSpec(memory_space=pl.ANY)] * len(exchange.ins)
        args += exchange.ins
        out_specs += [pl.BlockSpec(memory_space=pl.ANY)] * len(exchange.out_shape)
        out_shape += exchange.out_shape
        scratch += [pltpu.SemaphoreType.DMA((s,)) for s in exchange.n_sems]
    outs = pl.pallas_call(
        functools.partial(_mm_body, dims=dims, nk=nk, res_scale=res_scale if res is not None else None,
                          exchange=exchange, grid=grid),
        name=name, grid=grid, in_specs=in_specs, out_specs=out_specs, out_shape=out_shape, scratch_shapes=scratch,
        compiler_params=_params(("arbitrary",) * 3 if exchange else ("parallel", "parallel", "arbitrary")),
    )(*args)
    return (outs[0], outs[1:]) if exchange else outs[0]


def _proj_shards_body(chip_ref, *refs, n_ex_in, has_base, exchange, grid):
    del chip_ref
    refs = list(refs)
    if has_base:
        del refs[2 + n_ex_in]
    _mm_body(*refs, dims="nn", nk=1, res_scale=None, exchange=exchange, grid=grid)


def _proj_shards(xb, w, chip, offsets, *, name, base=None, exchange=None, tm=1024):
    t, d = xb.shape
    cs = D_IN // N_CHIPS
    own = w.shape[1] == cs
    tm = min(tm, t)
    col = lambda j, cr: (cr[0] + offsets[0] + j) % N_CHIPS
    grid = (len(offsets), t // tm, 1)
    in_specs = [pl.BlockSpec((tm, d), lambda j, i, k, cr: (i, 0)),
                pl.BlockSpec((d, cs), lambda j, i, k, cr: (0, 0 if own else col(j, cr)))]
    args = [xb, w]
    out_specs = [pl.BlockSpec((tm, cs), lambda j, i, k, cr: (i, col(j, cr)))]
    out_shape = [jax.ShapeDtypeStruct((t, D_IN), BF16)]
    scratch = [pltpu.VMEM((8, 128), F32)]
    n_ex_in = 0
    if exchange:
        n_ex_in = len(exchange.ins)
        in_specs += [pl.BlockSpec(memory_space=pl.ANY)] * n_ex_in
        args += exchange.ins
        out_specs += [pl.BlockSpec(memory_space=pl.ANY)] * len(exchange.out_shape)
        out_shape += exchange.out_shape
        scratch += [pltpu.SemaphoreType.DMA((s,)) for s in exchange.n_sems]
    aliases = {}
    if base is not None:
        in_specs.append(pl.BlockSpec(memory_space=pl.ANY))
        args.append(base)
        aliases = {len(args): 0}
    outs = pl.pallas_call(
        functools.partial(_proj_shards_body, n_ex_in=n_ex_in, has_base=base is not None, exchange=exchange, grid=grid),
        name=name,
        grid_spec=pltpu.PrefetchScalarGridSpec(num_scalar_prefetch=1, grid=grid, in_specs=in_specs,
                                               out_specs=out_specs, scratch_shapes=scratch),
        out_shape=out_shape, input_output_aliases=aliases,
        compiler_params=_params(("arbitrary",) * 3),
    )(chip, *args)
    return (outs[0], outs[1:]) if exchange else outs[0]


def _sigmoid(v):
    return 1.0 / (1.0 + jnp.exp(-v))


def _silu_and_grad(z):
    s = _sigmoid(z)
    return z * s, s * (1.0 + z * (1.0 - s))


def _cast_body(x_ref, o_ref):
    o_ref[...] = x_ref[...].astype(o_ref.dtype)


def _cast_and_transpose_body(x_ref, after_ref, o_ref, ot_ref):
    del after_ref
    xb = x_ref[...].astype(BF16)
    o_ref[...] = xb
    ot_ref[...] = xb.T


def _cast_and_transpose(x, after, *, name, rows=512):
    m, n = x.shape
    rows = min(rows, m)
    return pl.pallas_call(
        functools.partial(_cast_and_transpose_body), name=name, grid=(m // rows,),
        in_specs=[pl.BlockSpec((rows, n), lambda i: (i, 0)), pl.BlockSpec(memory_space=pl.ANY)],
        out_specs=[pl.BlockSpec((rows, n), lambda i: (i, 0)), pl.BlockSpec((n, rows), lambda i: (0, i))],
        out_shape=[jax.ShapeDtypeStruct((m, n), BF16), jax.ShapeDtypeStruct((n, m), BF16)],
        compiler_params=_params(("parallel",)),
    )(x, after)


def _cast(x, dtype, *, name, rows=512):
    m, n = x.shape
    rows = min(rows, m)
    return pl.pallas_call(
        functools.partial(_cast_body), name=name, grid=(m // rows,),
        in_specs=[pl.BlockSpec((rows, n), lambda i: (i, 0))],
        out_specs=pl.BlockSpec((rows, n), lambda i: (i, 0)),
        out_shape=jax.ShapeDtypeStruct((m, n), dtype),
        compiler_params=_params(("parallel",)),
    )(x)


def _lam_bar(lr, li, ls):
    step = jnp.exp(ls)
    er = jnp.exp(lr * step)
    return step, er * jnp.cos(li * step), er * jnp.sin(li * step)


def _ssm_pow_body(lr_ref, li_ref, ls_ref, pr_ref, pi_ref):
    _, ar, ai = _lam_bar(lr_ref[...], li_ref[...], ls_ref[...])
    cr, ci = ar, ai
    for i in range(SEG_LEN):
        pr_ref[pl.ds(i, 1), :] = cr
        pi_ref[pl.ds(i, 1), :] = ci
        cr, ci = cr * ar - ci * ai, cr * ai + ci * ar


def _ssm_bbar_body(lr_ref, li_ref, ls_ref, br_ref, bi_ref, or_ref, oi_ref):
    lr, li = lr_ref[...], li_ref[...]
    _, ar, ai = _lam_bar(lr, li, ls_ref[...])
    d = lr * lr + li * li
    ir, ii = lr / d, -li / d
    nr, ni = ar - 1.0, ai
    cr, ci = nr * ir - ni * ii, nr * ii + ni * ir
    br, bi = br_ref[...], bi_ref[...]
    or_ref[...] = cr * br - ci * bi
    oi_ref[...] = cr * bi + ci * br


def _ssm_param_bwd_body(lr_ref, li_ref, ls_ref, br_ref, bi_ref, glr_ref, gli_ref, gbr_ref, gbi_ref,
                        dlr_ref, dli_ref, dls_ref, dbr_ref, dbi_ref):
    lr, li = lr_ref[...], li_ref[...]
    step, ar, ai = _lam_bar(lr, li, ls_ref[...])
    d = lr * lr + li * li
    ir, ii = lr / d, -li / d
    nr, ni = ar - 1.0, ai
    cr, ci = nr * ir - ni * ii, nr * ii + ni * ir
    br, bi, gbr, gbi = br_ref[...], bi_ref[...], gbr_ref[...], gbi_ref[...]
    dbr_ref[...] = cr * gbr + ci * gbi
    dbi_ref[...] = cr * gbi - ci * gbr
    gcr = jnp.sum(br * gbr + bi * gbi, axis=1, keepdims=True)
    gci = jnp.sum(br * gbi - bi * gbr, axis=1, keepdims=True)
    gnr, gni = ir * gcr + ii * gci, ir * gci - ii * gcr
    gir, gii = nr * gcr + ni * gci, nr * gci - ni * gcr
    gtr, gti = glr_ref[...] + gnr, gli_ref[...] + gni
    i2r, i2i = ir * ir - ii * ii, 2.0 * ir * ii
    g1r, g1i = -(i2r * gir + i2i * gii), -(i2r * gii - i2i * gir)
    g2r, g2i = step * (ar * gtr + ai * gti), step * (ar * gti - ai * gtr)
    mr, mi = lr * ar - li * ai, lr * ai + li * ar
    dlr_ref[...] = g1r + g2r
    dli_ref[...] = g1i + g2i
    dls_ref[...] = (mr * gtr + mi * gti) * step


def _whole(shape):
    return pl.BlockSpec(shape, lambda *_: (0,) * len(shape))


def _ssm_prepare(lam_re, lam_im, b_re, b_im, c_re, c_im, log_step):
    row = lambda a: a.reshape(1, N_LANES)
    col = lambda a: a.reshape(N_LANES, 1)
    ls = jnp.repeat(log_step.reshape(N_GROUPS), SSM_STATE)
    p_re, p_im = pl.pallas_call(
        functools.partial(_ssm_pow_body), name="ssm_pow",
        in_specs=[_whole((1, N_LANES))] * 3, out_specs=[_whole((SEG_LEN, N_LANES))] * 2,
        out_shape=[jax.ShapeDtypeStruct((SEG_LEN, N_LANES), F32)] * 2, grid=(1,),
    )(row(lam_re), row(lam_im), row(ls))
    bb_re, bb_im = pl.pallas_call(
        functools.partial(_ssm_bbar_body), name="ssm_bbar",
        in_specs=[_whole((N_LANES, 1))] * 3 + [_whole((N_LANES, SSM_GROUP))] * 2,
        out_specs=[_whole((N_LANES, SSM_GROUP))] * 2,
        out_shape=[jax.ShapeDtypeStruct((N_LANES, SSM_GROUP), F32)] * 2, grid=(1,),
    )(col(lam_re), col(lam_im), col(ls), b_re.reshape(N_LANES, SSM_GROUP), b_im.reshape(N_LANES, SSM_GROUP))
    eye = jnp.eye(8, dtype=F32)

    def b_slabs(bb):
        t = bb.reshape(N_SLABS, 8, SSM_STATE, SSM_GROUP).transpose(0, 1, 3, 2)
        return (t[:, :, :, None, :] * eye[None, :, None, :, None]).reshape(N_SLABS, SLAB_CH, SLAB_LANES)

    def c_slabs(c):
        t = c.reshape(N_SLABS, 8, SSM_GROUP, SSM_STATE).transpose(0, 1, 3, 2)
        return (t[:, :, :, None, :] * eye[None, :, None, :, None]).reshape(N_SLABS, SLAB_LANES, SLAB_CH)

    b_bd = jnp.concatenate([b_slabs(bb_re), b_slabs(bb_im)], axis=2).astype(BF16)
    c_bd = jnp.concatenate([c_slabs(c_re.reshape(N_GROUPS, SSM_GROUP, SSM_STATE)),
                            -c_slabs(c_im.reshape(N_GROUPS, SSM_GROUP, SSM_STATE))], axis=1).astype(BF16)
    return p_re, p_im, b_bd, c_bd


def _diag_blocks_b(g):
    t = g.reshape(N_SLABS, 8, SSM_GROUP, 8, SSM_STATE)
    t = jnp.stack([t[:, i, :, i, :] for i in range(8)], axis=1)
    return t.transpose(0, 1, 3, 2).reshape(N_LANES, SSM_GROUP)


def _diag_blocks_c(g):
    t = g.reshape(N_SLABS, 8, SSM_STATE, 8, SSM_GROUP)
    t = jnp.stack([t[:, i, :, i, :] for i in range(8)], axis=1)
    return t.transpose(0, 1, 3, 2).reshape(N_GROUPS, SSM_GROUP, SSM_STATE)


def _ssm_param_grads(lam_re, lam_im, b_re, b_im, log_step, g_lam_re, g_lam_im, g_bbd, g_cbd):
    col = lambda a: a.reshape(N_LANES, 1)
    ls = jnp.repeat(log_step.reshape(N_GROUPS), SSM_STATE)
    gbr = _diag_blocks_b(g_bbd[:, :, :SLAB_LANES])
    gbi = _diag_blocks_b(g_bbd[:, :, SLAB_LANES:])
    outs = pl.pallas_call(
        functools.partial(_ssm_param_bwd_body), name="ssm_param_bwd", grid=(1,),
        in_specs=[_whole((N_LANES, 1))] * 3 + [_whole((N_LANES, SSM_GROUP))] * 2 + [_whole((N_LANES, 1))] * 2
        + [_whole((N_LANES, SSM_GROUP))] * 2,
        out_specs=[_whole((N_LANES, 1))] * 3 + [_whole((N_LANES, SSM_GROUP))] * 2,
        out_shape=[jax.ShapeDtypeStruct((N_LANES, 1), F32)] * 3 + [jax.ShapeDtypeStruct((N_LANES, SSM_GROUP), F32)] * 2,
    )(col(lam_re), col(lam_im), col(ls), b_re.reshape(N_LANES, SSM_GROUP), b_im.reshape(N_LANES, SSM_GROUP),
      col(g_lam_re), col(g_lam_im), gbr, gbi)
    dlr, dli, dls, dbr, dbi = outs
    d_c_re = _diag_blocks_c(g_cbd[:, :SLAB_LANES, :])
    d_c_im = -_diag_blocks_c(g_cbd[:, SLAB_LANES:, :])
    return (dlr.reshape(1, N_GROUPS, SSM_STATE), dli.reshape(1, N_GROUPS, SSM_STATE),
            dbr.reshape(1, N_GROUPS, SSM_STATE, SSM_GROUP), dbi.reshape(1, N_GROUPS, SSM_STATE, SSM_GROUP),
            d_c_re[None], d_c_im[None], dls.reshape(N_GROUPS, SSM_STATE).sum(axis=1).reshape(1, N_GROUPS))


def _bcast8(v):
    return jnp.broadcast_to(v, (8, v.shape[1]))


def _segment_permutation():
    p = np.zeros((SSM_CHUNK, SSM_CHUNK), np.float32)
    rows = np.arange(SSM_CHUNK)
    p[rows, (rows % 8) * SEG_LEN + rows // 8] = 1.0
    return p


def _permute_exact(perm, val, pieces):
    out, rest = None, val
    for n in range(pieces):
        part = rest.astype(BF16)
        moved = jnp.dot(perm, part, preferred_element_type=F32)
        out = moved if out is None else out + moved
        if n + 1 < pieces:
            rest = rest - part.astype(F32)
    return out


def _scan_buffer():
    return pltpu.VMEM((SSM_CHUNK, N_LANES), F32)


def _lanes(k):
    return pl.ds(k * SLAB_LANES, SLAB_LANES)


def _tile(i):
    return pl.ds(i * 8 if isinstance(i, int) else pl.multiple_of(i * 8, 8), 8)


SCAN_LANES = 1024


def _scan_lanes(k):
    return pl.ds(k * SCAN_LANES, SCAN_LANES)


def _seg_get(ref, k, i):
    return ref[_tile(i), _scan_lanes(k)]


def _seg_put(ref, k, i, val):
    ref[_tile(i), _scan_lanes(k)] = val


def _slab_get(ref, k):
    return ref[:, _lanes(k)]


def _slab_put(ref, k, val):
    ref[:, _lanes(k)] = val


def _scan_forward(s_re, s_im, p_re, p_im, car_re, car_im, sp_re=None, sp_im=None):
    for k in range(N_LANES // SCAN_LANES):
        ln = _scan_lanes(k)
        ar, ai =_bcast8(p_re[pl.ds(0, 1), ln]), _bcast8(p_im[pl.ds(0, 1), ln])

        def step(i, s, k=k, ar=ar, ai=ai):
            sr, si = s
            nr = ar * sr - ai * si + _seg_get(s_re, k, i)
            ni = ar * si + ai * sr + _seg_get(s_im, k, i)
            _seg_put(s_re, k, i, nr)
            _seg_put(s_im, k, i, ni)
            return nr, ni

        zero = jnp.zeros((8, SCAN_LANES), F32)
        er, ei = lax.fori_loop(0, SEG_LEN, step, (zero, zero), unroll=4)
        lr, li = p_re[pl.ds(SEG_LEN - 1, 1), ln], p_im[pl.ds(SEG_LEN - 1, 1), ln]
        cr, ci = car_re[pl.ds(0, 1), ln], car_im[pl.ds(0, 1), ln]
        rows_r, rows_i = [], []
        for r in range(8):
            rows_r.append(cr)
            rows_i.append(ci)
            cr, ci = er[r:r + 1] + lr * cr - li * ci, ei[r:r + 1] + lr * ci + li * cr
        pr8, pi8 = jnp.concatenate(rows_r, axis=0), jnp.concatenate(rows_i, axis=0)
        car_re[:, ln] = _bcast8(cr)
        car_im[:, ln] = _bcast8(ci)
        if sp_re is not None:
            sp_re[:, ln] = pr8
            sp_im[:, ln] = pi8

        def fix(i, _, k=k, ln=ln, pr8=pr8, pi8=pi8):
            qr, qi = _bcast8(p_re[pl.ds(i, 1), ln]), _bcast8(p_im[pl.ds(i, 1), ln])
            _seg_put(s_re, k, i, _seg_get(s_re, k, i) + qr * pr8 - qi * pi8)
            _seg_put(s_im, k, i, _seg_get(s_im, k, i) + qr * pi8 + qi * pr8)
            return 0

        lax.fori_loop(0, SEG_LEN, fix, 0, unroll=4)


def _scan_backward(g_re, g_im, s_re, s_im, sp_re, sp_im, p_re, p_im, car_re, car_im, acc_re, acc_im):
    for k in range(N_LANES // SCAN_LANES):
        ln = _scan_lanes(k)
        ar, ai =_bcast8(p_re[pl.ds(0, 1), ln]), -_bcast8(p_im[pl.ds(0, 1), ln])

        def step(j, s, k=k, ar=ar, ai=ai):
            i = SEG_LEN - 1 - j
            sr, si = s
            nr = ar * sr - ai * si + _seg_get(g_re, k, i)
            ni = ar * si + ai * sr + _seg_get(g_im, k, i)
            _seg_put(g_re, k, i, nr)
            _seg_put(g_im, k, i, ni)
            return nr, ni

        zero = jnp.zeros((8, SCAN_LANES), F32)
        er, ei = lax.fori_loop(0, SEG_LEN, step, (zero, zero), unroll=4)
        lr, li = p_re[pl.ds(SEG_LEN - 1, 1), ln], -p_im[pl.ds(SEG_LEN - 1, 1), ln]
        cr, ci = car_re[pl.ds(0, 1), ln], car_im[pl.ds(0, 1), ln]
        rows_r, rows_i = [None] * 8, [None] * 8
        for r in range(7, -1, -1):
            rows_r[r], rows_i[r] = cr, ci
            cr, ci = er[r:r + 1] + lr * cr - li * ci, ei[r:r + 1] + lr * ci + li * cr
        nr8, ni8 = jnp.concatenate(rows_r, axis=0), jnp.concatenate(rows_i, axis=0)
        car_re[:, ln] = _bcast8(cr)
        car_im[:, ln] = _bcast8(ci)

        def fix(i, acc, k=k, ln=ln, nr8=nr8, ni8=ni8):
            qr = _bcast8(p_re[pl.ds(SEG_LEN - 1 - i, 1), ln])
            qi = -_bcast8(p_im[pl.ds(SEG_LEN - 1 - i, 1), ln])
            gr = _seg_get(g_re, k, i) + qr * nr8 - qi * ni8
            gi = _seg_get(g_im, k, i) + qr * ni8 + qi * nr8
            _seg_put(g_re, k, i, gr)
            _seg_put(g_im, k, i, gi)
            return gr, gi

        def prod(xr, xi, gr, gi):
            return xr * gr + xi * gi, xr * gi - xi * gr

        gr, gi = fix(0, None)
        a_r, a_i = prod(sp_re[:, ln], sp_im[:, ln], gr, gi)

        def fix_acc(i, acc, k=k, fix=fix):
            gr, gi = fix(i, None)
            dr, di = prod(_seg_get(s_re, k, i - 1), _seg_get(s_im, k, i - 1), gr, gi)
            return acc[0] + dr, acc[1] + di

        acc = (a_r, a_i)
        for i in range(1, 4):
            acc = fix_acc(i, acc)
        a_r, a_i = lax.fori_loop(4, SEG_LEN, fix_acc, acc, unroll=4)
        acc_re[:, ln] += a_r
        acc_im[:, ln] += a_i


def _gelu_and_grad(y):
    cdf = 0.5 * (1.0 + lax.erf(y * (2.0 ** -0.5)))
    pdf = jnp.exp(-0.5 * y * y) * (1.0 / math.sqrt(2.0 * math.pi))
    return y * cdf, cdf + y * pdf


def _ssm_fwd_body(u_ref, bbd_ref, cbd_ref, pre_ref, pim_ref, d_ref, perm_ref, unperm_ref,
                  y_ref, gin_ref, gint_ref, cre_out, cim_out, s_re, s_im, car_re, car_im, yp):
    c = pl.program_id(0)

    @pl.when(c == 0)
    def _():
        car_re[...] = jnp.zeros_like(car_re)
        car_im[...] = jnp.zeros_like(car_im)

    cre_out[...] = car_re[...]
    cim_out[...] = car_im[...]
    u = u_ref[...].astype(F32)
    up = jnp.dot(perm_ref[...], u_ref[...].astype(BF16), preferred_element_type=F32).astype(BF16)
    for k in range(N_SLABS):
        bu = jnp.dot(up[:, k * SLAB_CH:(k + 1) * SLAB_CH], bbd_ref[k], preferred_element_type=F32)
        _slab_put(s_re, k, bu[:, :SLAB_LANES])
        _slab_put(s_im, k, bu[:, SLAB_LANES:])
    _scan_forward(s_re, s_im, pre_ref, pim_ref, car_re, car_im)
    for k in range(N_SLABS):
        yp[:, pl.ds(k * SLAB_CH, SLAB_CH)] = (
            jnp.dot(_slab_get(s_re, k).astype(BF16), cbd_ref[k, :SLAB_LANES, :], preferred_element_type=F32)
            + jnp.dot(_slab_get(s_im, k).astype(BF16), cbd_ref[k, SLAB_LANES:, :], preferred_element_type=F32))
    y = _permute_exact(unperm_ref[...], yp[...], 1) + d_ref[...] * u
    y_ref[...] = y.astype(y_ref.dtype)
    _store_with_transpose(_gelu_and_grad(y)[0], gin_ref, gint_ref)


def _ssm_forward(proj, p_re, p_im, b_bd, c_bd, d_skip):
    t = proj.shape[0]
    nc = t // SSM_CHUNK
    perm = _segment_permutation()
    return pl.pallas_call(
        functools.partial(_ssm_fwd_body), name="ssm_fwd", grid=(nc,),
        in_specs=[pl.BlockSpec((SSM_CHUNK, D_SSM), lambda c: (c, C_U // D_SSM)),
                  _whole(b_bd.shape), _whole(c_bd.shape), _whole(p_re.shape), _whole(p_im.shape),
                  _whole((1, D_SSM)), _whole(perm.shape), _whole(perm.shape)],
        out_specs=[pl.BlockSpec((SSM_CHUNK, D_SSM), lambda c: (c, 0)),
                   pl.BlockSpec((SSM_CHUNK, D_SSM), lambda c: (c, 0)),
                   pl.BlockSpec((D_SSM, SSM_CHUNK), lambda c: (0, c)),
                   pl.BlockSpec((None, 8, N_LANES), lambda c: (c, 0, 0)),
                   pl.BlockSpec((None, 8, N_LANES), lambda c: (c, 0, 0))],
        out_shape=[jax.ShapeDtypeStruct((t, D_SSM), BF16), jax.ShapeDtypeStruct((t, D_SSM), BF16),
                   jax.ShapeDtypeStruct((D_SSM, t), BF16),
                   jax.ShapeDtypeStruct((nc, 8, N_LANES), F32), jax.ShapeDtypeStruct((nc, 8, N_LANES), F32)],
        scratch_shapes=[_scan_buffer(), _scan_buffer(),
                        pltpu.VMEM((8, N_LANES), F32), pltpu.VMEM((8, N_LANES), F32),
                        pltpu.VMEM((SSM_CHUNK, D_SSM), F32)],
        compiler_params=_params(("arbitrary",)),
    )(proj, b_bd, c_bd, p_re, p_im, d_skip, jnp.asarray(perm, BF16), jnp.asarray(perm.T, BF16))


def _ssm_bwd_body(u_ref, dgin_ref, y_ref, cre_in, cim_in, bbd_ref, cbd_ref, pre_ref, pim_ref, d_ref, perm_ref,
                  unperm_ref, dproj_in,
                  du_ref, gb_ref, gc_ref, glr_ref, gli_ref, gd_ref,
                  s_re, s_im, g_re, g_im, sp_re, sp_im, car_re, car_im, gcar_re, gcar_im, acc_re, acc_im, dup):
    del dproj_in
    c = pl.program_id(0)
    nc = pl.num_programs(0)

    @pl.when(c == 0)
    def _():
        gcar_re[...] = jnp.zeros_like(gcar_re)
        gcar_im[...] = jnp.zeros_like(gcar_im)
        acc_re[...] = jnp.zeros_like(acc_re)
        acc_im[...] = jnp.zeros_like(acc_im)
        gb_ref[...] = jnp.zeros_like(gb_ref)
        gc_ref[...] = jnp.zeros_like(gc_ref)
        gd_ref[...] = jnp.zeros_like(gd_ref)

    car_re[...] = cre_in[...]
    car_im[...] = cim_in[...]
    u = u_ref[...].astype(F32)
    dy = dgin_ref[...].astype(F32) * _gelu_and_grad(y_ref[...].astype(F32))[1]
    gd_ref[...] += jnp.sum(dy * u, axis=0, keepdims=True)
    up = jnp.dot(perm_ref[...], u.astype(BF16), preferred_element_type=F32).astype(BF16)
    dyp = jnp.dot(perm_ref[...], dy.astype(BF16), preferred_element_type=F32).astype(BF16)
    for k in range(N_SLABS):
        ch = slice(k * SLAB_CH, (k + 1) * SLAB_CH)
        bu = jnp.dot(up[:, ch], bbd_ref[k], preferred_element_type=F32)
        _slab_put(s_re, k, bu[:, :SLAB_LANES])
        _slab_put(s_im, k, bu[:, SLAB_LANES:])
        ds = lax.dot_general(dyp[:, ch], cbd_ref[k], _DIMS["nt"], preferred_element_type=F32)
        _slab_put(g_re, k, ds[:, :SLAB_LANES])
        _slab_put(g_im, k, ds[:, SLAB_LANES:])
    _scan_forward(s_re, s_im, pre_ref, pim_ref, car_re, car_im, sp_re, sp_im)
    _scan_backward(g_re, g_im, s_re, s_im, sp_re, sp_im, pre_ref, pim_ref, gcar_re, gcar_im, acc_re, acc_im)
    for k in range(N_SLABS):
        ch = slice(k * SLAB_CH, (k + 1) * SLAB_CH)
        uk, dyk = up[:, ch], dyp[:, ch]
        sr, si = _slab_get(s_re, k).astype(BF16), _slab_get(s_im, k).astype(BF16)
        gr, gi = _slab_get(g_re, k).astype(BF16), _slab_get(g_im, k).astype(BF16)
        gc_ref[k, :SLAB_LANES, :] += lax.dot_general(sr, dyk, _DIMS["tn"], preferred_element_type=F32)
        gc_ref[k, SLAB_LANES:, :] += lax.dot_general(si, dyk, _DIMS["tn"], preferred_element_type=F32)
        gb_ref[k, :, :SLAB_LANES] += lax.dot_general(uk, gr, _DIMS["tn"], preferred_element_type=F32)
        gb_ref[k, :, SLAB_LANES:] += lax.dot_general(uk, gi, _DIMS["tn"], preferred_element_type=F32)
        dup[:, pl.ds(k * SLAB_CH, SLAB_CH)] = (
            lax.dot_general(gr, bbd_ref[k, :, :SLAB_LANES], _DIMS["nt"], preferred_element_type=F32)
            + lax.dot_general(gi, bbd_ref[k, :, SLAB_LANES:], _DIMS["nt"], preferred_element_type=F32))
    du = _permute_exact(unperm_ref[...], dup[...], 1) + d_ref[...] * dy
    du_ref[...] = du.astype(du_ref.dtype)

    @pl.when(c == nc - 1)
    def _():
        glr_ref[...] = jnp.sum(acc_re[...], axis=0, keepdims=True)
        gli_ref[...] = jnp.sum(acc_im[...], axis=0, keepdims=True)


def _ssm_backward(proj, dg_in, y_ssm, car_re, car_im, p_re, p_im, b_bd, c_bd, d_skip, dproj):
    t = proj.shape[0]
    nc = t // SSM_CHUNK
    rev = lambda c: nc - 1 - c
    big = _scan_buffer
    small = lambda: pltpu.VMEM((8, N_LANES), F32)
    perm = _segment_permutation()
    outs = pl.pallas_call(
        functools.partial(_ssm_bwd_body), name="ssm_bwd", grid=(nc,),
        in_specs=[pl.BlockSpec((SSM_CHUNK, D_SSM), lambda c: (rev(c), C_U // D_SSM)),
                  pl.BlockSpec((SSM_CHUNK, D_SSM), lambda c: (rev(c), 0)),
                  pl.BlockSpec((SSM_CHUNK, D_SSM), lambda c: (rev(c), 0)),
                  pl.BlockSpec((None, 8, N_LANES), lambda c: (rev(c), 0, 0)),
                  pl.BlockSpec((None, 8, N_LANES), lambda c: (rev(c), 0, 0)),
                  _whole(b_bd.shape), _whole(c_bd.shape), _whole(p_re.shape), _whole(p_im.shape),
                  _whole((1, D_SSM)), _whole(perm.shape), _whole(perm.shape), pl.BlockSpec(memory_space=pl.ANY)],
        out_specs=[pl.BlockSpec((SSM_CHUNK, D_SSM), lambda c: (rev(c), C_U // D_SSM)),
                   _whole(b_bd.shape), _whole(c_bd.shape), _whole((1, N_LANES)), _whole((1, N_LANES)),
                   _whole((1, D_SSM))],
        out_shape=[jax.ShapeDtypeStruct(dproj.shape, dproj.dtype),
                   jax.ShapeDtypeStruct(b_bd.shape, F32), jax.ShapeDtypeStruct(c_bd.shape, F32),
                   jax.ShapeDtypeStruct((1, N_LANES), F32), jax.ShapeDtypeStruct((1, N_LANES), F32),
                   jax.ShapeDtypeStruct((1, D_SSM), F32)],
        scratch_shapes=[big(), big(), big(), big()] + [small() for _ in range(8)]
        + [pltpu.VMEM((SSM_CHUNK, D_SSM), F32)],
        input_output_aliases={12: 0},
        compiler_params=_params(("arbitrary",)),
    )(proj, dg_in, y_ssm, car_re, car_im, b_bd, c_bd, p_re, p_im, d_skip, jnp.asarray(perm, BF16),
      jnp.asarray(perm.T, BF16), dproj)
    return outs


def _bucket_table():
    i = np.arange(BLOCK)[:, None]
    j = np.arange(2 * BLOCK)[None, :]
    dist = BLOCK + i - j
    ok = (dist >= 0) & (dist < WINDOW)
    max_exact = N_BUCKETS // 2
    d = np.maximum(dist, 1).astype(np.float32)
    large = max_exact + (np.log(d / max_exact) / math.log(MAX_DISTANCE / max_exact)
                         * (N_BUCKETS - max_exact)).astype(np.int32)
    large = np.minimum(large, N_BUCKETS - 1)
    bucket = np.where(dist < max_exact, dist, large)
    return np.where(ok, bucket, -1).astype(np.int32)


def _bias_build_body(table_ref, bucket_ref, o_ref):
    h = pl.program_id(0)
    bucket = bucket_ref[...]
    acc = jnp.full(bucket.shape, NEG_INF, F32)
    for b in range(N_BUCKETS):
        acc = jnp.where(bucket == b, table_ref[b, h], acc)
    o_ref[...] = acc


def _bias_build(rel_bias_table, bucket):
    return pl.pallas_call(
        functools.partial(_bias_build_body), name="bias_build", grid=(N_Q_HEADS,),
        in_specs=[pl.BlockSpec(memory_space=pltpu.SMEM), _whole(bucket.shape)],
        out_specs=pl.BlockSpec((None, BLOCK, 2 * BLOCK), lambda h: (h, 0, 0)),
        out_shape=jax.ShapeDtypeStruct((N_Q_HEADS, BLOCK, 2 * BLOCK), F32),
        compiler_params=_params(("arbitrary",)),
    )(rel_bias_table, bucket)


def _bias_grad_body(g_ref, bucket_ref, o_ref):
    bucket = bucket_ref[...]
    lane = lax.broadcasted_iota(jnp.int32, (N_BUCKETS, 128), 1)

    def head(h, out):
        g = g_ref[h]
        rows = [jnp.sum(jnp.where(bucket == b, g, 0.0), axis=0, keepdims=True) for b in range(N_BUCKETS)]
        colsum = jnp.sum(jnp.concatenate(rows, axis=0), axis=1, keepdims=True)
        return jnp.where(lane == h, colsum, out)

    o_ref[...] = lax.fori_loop(0, N_Q_HEADS, head, jnp.zeros((N_BUCKETS, 128), F32))


def _bias_grad(g_bias, bucket):
    out = pl.pallas_call(
        functools.partial(_bias_grad_body), name="bias_grad", grid=(1,),
        in_specs=[_whole(g_bias.shape), _whole(bucket.shape)],
        out_specs=_whole((N_BUCKETS, 128)),
        out_shape=jax.ShapeDtypeStruct((N_BUCKETS, 128), F32),
        compiler_params=_params(("arbitrary",)),
    )(g_bias, bucket)
    return out[:, :N_Q_HEADS]


def _head_logits(qh, kk, bias_h, first_block):
    s = lax.dot_general(qh, kk, _DIMS["nt"], preferred_element_type=F32) * ATTN_SCALE + bias_h
    col = lax.broadcasted_iota(jnp.int32, s.shape, 1)
    return jnp.where(jnp.logical_and(first_block, col < BLOCK), NEG_INF, s)


def _attn_fwd_body(sink_ref, q_ref, kp_ref, kc_ref, vp_ref, vc_ref, bias_ref, o_ref, lse_ref):
    n = pl.program_id(0)
    outs, lses = [], []
    for kv in range(N_KV_HEADS):
        cs = slice(kv * HEAD_DIM, (kv + 1) * HEAD_DIM)
        kk = jnp.concatenate([kp_ref[:, cs], kc_ref[:, cs]], axis=0).astype(BF16)
        vv = jnp.concatenate([vp_ref[:, cs], vc_ref[:, cs]], axis=0).astype(BF16)
        for g in range(Q_PER_KV):
            h = kv * Q_PER_KV + g
            qh = q_ref[:, h * HEAD_DIM:(h + 1) * HEAD_DIM].astype(BF16)
            s = _head_logits(qh, kk, bias_ref[h], n == 0)
            sink = sink_ref[0, h]
            m = jnp.maximum(jnp.max(s, axis=1, keepdims=True), sink)
            p = jnp.exp(s - m)
            den = jnp.sum(p, axis=1, keepdims=True) + jnp.exp(sink - m)
            p = p / den
            outs.append(jnp.dot(p.astype(BF16), vv, preferred_element_type=F32))
            lses.append(m + jnp.log(den))
    o_ref[...] = jnp.concatenate(outs, axis=1).astype(o_ref.dtype)
    lse_ref[...] = jnp.concatenate(lses, axis=1)


def _attn_specs(nb):
    prev = lambda n: jnp.maximum(jnp.minimum(n, nb - 1) - 1, 0)
    cur = lambda n: jnp.minimum(n, nb - 1)
    return [pl.BlockSpec((BLOCK, D_ATTN), lambda n: (cur(n), C_Q // D_ATTN)),
            pl.BlockSpec((BLOCK, D_KV), lambda n: (prev(n), C_K // D_KV)),
            pl.BlockSpec((BLOCK, D_KV), lambda n: (cur(n), C_K // D_KV)),
            pl.BlockSpec((BLOCK, D_KV), lambda n: (prev(n), C_V // D_KV)),
            pl.BlockSpec((BLOCK, D_KV), lambda n: (cur(n), C_V // D_KV))]


def _attn_forward(proj, sinks, bias):
    t = proj.shape[0]
    nb = t // BLOCK
    return pl.pallas_call(
        functools.partial(_attn_fwd_body), name="attn_fwd", grid=(nb,),
        in_specs=[pl.BlockSpec(memory_space=pltpu.SMEM)] + _attn_specs(nb) + [_whole(bias.shape)],
        out_specs=[pl.BlockSpec((BLOCK, D_ATTN), lambda n: (n, 0)),
                   pl.BlockSpec((BLOCK, N_Q_HEADS), lambda n: (n, 0))],
        out_shape=[jax.ShapeDtypeStruct((t, D_ATTN), BF16), jax.ShapeDtypeStruct((t, N_Q_HEADS), F32)],
        compiler_params=_params(("parallel",)),
    )(sinks, proj, proj, proj, proj, proj, bias)


def _attn_bwd_body(sink_ref, q_ref, kp_ref, kc_ref, vp_ref, vc_ref, bias_ref, do_ref, o_ref, lse_ref, dproj_in,
                   dq_ref, dkv_ref, gbias_ref, gsink_ref, carry_ref, *, nb):
    del dproj_in
    n = pl.program_id(0)

    @pl.when(n == 0)
    def _():
        gbias_ref[...] = jnp.zeros_like(gbias_ref)
        gsink_ref[...] = jnp.zeros_like(gsink_ref)
        carry_ref[...] = jnp.zeros_like(carry_ref)

    @pl.when(n < nb)
    def _():
        lane = lax.broadcasted_iota(jnp.int32, (1, 128), 1)
        dqs, dks, dvs = [], [], []
        gsink = jnp.zeros((1, 128), F32)
        for kv in range(N_KV_HEADS):
            cs = slice(kv * HEAD_DIM, (kv + 1) * HEAD_DIM)
            kk = jnp.concatenate([kp_ref[:, cs], kc_ref[:, cs]], axis=0).astype(BF16)
            vv = jnp.concatenate([vp_ref[:, cs], vc_ref[:, cs]], axis=0).astype(BF16)
            dk = jnp.zeros((2 * BLOCK, HEAD_DIM), F32)
            dv = jnp.zeros((2 * BLOCK, HEAD_DIM), F32)
            for g in range(Q_PER_KV):
                h = kv * Q_PER_KV + g
                hs = slice(h * HEAD_DIM, (h + 1) * HEAD_DIM)
                qh = q_ref[:, hs].astype(BF16)
                s = _head_logits(qh, kk, bias_ref[h], n == 0)
                lse = lse_ref[:, h:h + 1]
                p = jnp.exp(s - lse)
                do = do_ref[:, hs].astype(F32)
                delta = jnp.sum(do * o_ref[:, hs].astype(F32), axis=1, keepdims=True)
                dob = do.astype(BF16)
                dp = lax.dot_general(dob, vv, _DIMS["nt"], preferred_element_type=F32)
                dl = p * (dp - delta)
                gbias_ref[h] += dl
                psink = jnp.exp(sink_ref[0, h] - lse)
                gsink = gsink + jnp.where(lane == h, -jnp.sum(psink * delta), 0.0)
                dlb = dl.astype(BF16)
                dqs.append(jnp.dot(dlb, kk, preferred_element_type=F32) * ATTN_SCALE)
                dk = dk + lax.dot_general(dlb, qh, _DIMS["tn"], preferred_element_type=F32) * ATTN_SCALE
                dv = dv + lax.dot_general(p.astype(BF16), dob, _DIMS["tn"], preferred_element_type=F32)
            dks.append(dk)
            dvs.append(dv)
        dq_ref[...] = jnp.concatenate(dqs, axis=1).astype(dq_ref.dtype)
        gsink_ref[...] += gsink
        dkv = jnp.concatenate(dks + dvs, axis=1)
        dkv_ref[...] = (carry_ref[...] + dkv[:BLOCK]).astype(dkv_ref.dtype)
        carry_ref[...] = dkv[BLOCK:]

    @pl.when(n == nb)
    def _():
        dkv_ref[...] = carry_ref[...].astype(dkv_ref.dtype)


def _attn_backward(proj, sinks, bias, d_attn, attn, lse, dproj):
    t = proj.shape[0]
    nb = t // BLOCK
    cur = lambda n: jnp.minimum(n, nb - 1)
    return pl.pallas_call(
        functools.partial(_attn_bwd_body, nb=nb), name="attn_bwd", grid=(nb + 1,),
        in_specs=[pl.BlockSpec(memory_space=pltpu.SMEM)] + _attn_specs(nb) + [
            _whole(bias.shape),
            pl.BlockSpec((BLOCK, D_ATTN), lambda n: (cur(n), 0)),
            pl.BlockSpec((BLOCK, D_ATTN), lambda n: (cur(n), 0)),
            pl.BlockSpec((BLOCK, N_Q_HEADS), lambda n: (cur(n), 0)),
            pl.BlockSpec(memory_space=pl.ANY)],
        out_specs=[pl.BlockSpec((BLOCK, D_ATTN), lambda n: (cur(n), C_Q // D_ATTN)),
                   pl.BlockSpec((BLOCK, 2 * D_KV), lambda n: (jnp.maximum(n - 1, 0), 0)),
                   _whole(bias.shape), _whole((1, 128))],
        out_shape=[jax.ShapeDtypeStruct(dproj.shape, dproj.dtype), jax.ShapeDtypeStruct((t, 2 * D_KV), dproj.dtype),
                   jax.ShapeDtypeStruct(bias.shape, F32), jax.ShapeDtypeStruct((1, 128), F32)],
        scratch_shapes=[pltpu.VMEM((BLOCK, 2 * D_KV), F32)],
        input_output_aliases={10: 0},
        compiler_params=_params(("arbitrary",)),
    )(sinks, proj, proj, proj, proj, proj, bias, d_attn, attn, lse, dproj)


ROWS = 512


def _rowwise(body, name, t, ins, outs, aliases=None):
    rows = min(ROWS, t)

    def col_spec(w, c0):
        if c0 == "T":
            return pl.BlockSpec((w, rows), lambda i: (0, i))
        if c0 % w == 0:
            return pl.BlockSpec((rows, w), lambda i: (i, c0 // w))
        return pl.BlockSpec((pl.Element(rows), pl.Element(w)), lambda i: (i * rows, c0))

    in_specs, args = [], []
    for a, w, c0 in ins:
        args.append(a)
        if w is None:
            in_specs.append(pl.BlockSpec(memory_space=pl.ANY) if c0 == "any" else _whole(a.shape))
        else:
            in_specs.append(col_spec(w, c0))
    out_specs, out_shape = [], []
    for shape, dtype, w, c0 in outs:
        out_shape.append(jax.ShapeDtypeStruct(shape, dtype))
        out_specs.append(_whole(shape) if w is None else col_spec(w, c0))
    accum = any(o[2] is None for o in outs)
    return pl.pallas_call(
        body, name=name, grid=(t // rows,), in_specs=in_specs, out_specs=out_specs, out_shape=out_shape,
        input_output_aliases=aliases or {},
        compiler_params=_params(("arbitrary",) if accum else ("parallel",)),
    )(*args)


def _f32(ref, *idx):
    return (ref[idx] if idx else ref[...]).astype(F32)


def _store_with_transpose(val, ref, t_ref):
    val = val.astype(ref.dtype)
    ref[...] = val
    t_ref[...] = val.T


def _ssm_gate_fwd_body(glu_ref, z_ref, h_ref, ht_ref):
    a, b = _f32(glu_ref, slice(None), slice(0, D_SSM)), _f32(glu_ref, slice(None), slice(D_SSM, None))
    _store_with_transpose((a * _sigmoid(b)) * _silu_and_grad(_f32(z_ref))[0], h_ref, ht_ref)


def _ssm_gate_bwd_body(dh_ref, glu_ref, z_ref, dproj_in, dglu_ref, dz_ref):
    del dproj_in
    a, b = _f32(glu_ref, slice(None), slice(0, D_SSM)), _f32(glu_ref, slice(None), slice(D_SSM, None))
    sb = _sigmoid(b)
    silu, dsilu = _silu_and_grad(_f32(z_ref))
    dh = _f32(dh_ref)
    dg = dh * silu
    dz_ref[...] = (dh * (a * sb) * dsilu).astype(dz_ref.dtype)
    dglu_ref[:, :D_SSM] = (dg * sb).astype(dglu_ref.dtype)
    dglu_ref[:, D_SSM:] = (dg * a * sb * (1.0 - sb)).astype(dglu_ref.dtype)


def _attn_gate_fwd_body(attn_ref, z_ref, h_ref, ht_ref):
    _store_with_transpose(_f32(attn_ref) * _silu_and_grad(_f32(z_ref))[0], h_ref, ht_ref)


def _attn_gate_bwd_body(dh_ref, attn_ref, z_ref, dproj_in, dattn_ref, dz_ref):
    del dproj_in
    silu, dsilu = _silu_and_grad(_f32(z_ref))
    dh = _f32(dh_ref)
    dattn_ref[...] = (dh * silu).astype(dattn_ref.dtype)
    dz_ref[...] = (dh * _f32(attn_ref) * dsilu).astype(dz_ref.dtype)


def _merge_fwd_body(bs_ref, ba_ref, gl_ref, m_ref, mt_ref):
    gs = _sigmoid(_f32(gl_ref, slice(None), slice(0, D_MODEL)))
    ga = _sigmoid(_f32(gl_ref, slice(None), slice(D_MODEL, None)))
    _store_with_transpose(gs * _f32(bs_ref) + ga * _f32(ba_ref), m_ref, mt_ref)


def _merge_bwd_body(dm_ref, bs_ref, ba_ref, gl_ref, dbs_ref, dba_ref, dgl_ref):
    gs = _sigmoid(_f32(gl_ref, slice(None), slice(0, D_MODEL)))
    ga = _sigmoid(_f32(gl_ref, slice(None), slice(D_MODEL, None)))
    dm = _f32(dm_ref)
    dbs_ref[...] = (dm * gs).astype(dbs_ref.dtype)
    dba_ref[...] = (dm * ga).astype(dba_ref.dtype)
    dgl_ref[:, :D_MODEL] = (dm * _f32(bs_ref) * gs * (1.0 - gs)).astype(dgl_ref.dtype)
    dgl_ref[:, D_MODEL:] = (dm * _f32(ba_ref) * ga * (1.0 - ga)).astype(dgl_ref.dtype)


def _ln_loss_body(x_ref, o_ref, tgt_ref, gain_ref, bias_ref, dr_ref, loss_ref, dgain_ref, dbias_ref):
    @pl.when(pl.program_id(0) == 0)
    def _():
        loss_ref[...] = jnp.zeros_like(loss_ref)
        dgain_ref[...] = jnp.zeros_like(dgain_ref)
        dbias_ref[...] = jnp.zeros_like(dbias_ref)

    r = ALPHA * x_ref[...] + o_ref[...].astype(F32)
    mu = jnp.mean(r, axis=1, keepdims=True)
    rc = r - mu
    var = jnp.mean(rc * rc, axis=1, keepdims=True)
    rstd = lax.rsqrt(var + LN_EPS)
    xhat = rc * rstd
    gain = gain_ref[...]
    err = xhat * gain + bias_ref[...] - tgt_ref[...]
    loss_ref[...] += 0.5 * jnp.sum(jnp.mean(err * err, axis=1, keepdims=True), axis=0, keepdims=True)
    dy = err * (1.0 / D_MODEL)
    dgain_ref[...] += jnp.sum(dy * xhat, axis=0, keepdims=True)
    dbias_ref[...] += jnp.sum(dy, axis=0, keepdims=True)
    dxhat = dy * gain
    m1 = jnp.mean(dxhat, axis=1, keepdims=True)
    m2 = jnp.mean(dxhat * xhat, axis=1, keepdims=True)
    dr_ref[...] = (rstd * (dxhat - m1 - xhat * m2)).astype(dr_ref.dtype)


def _place_body(piece_ref, dproj_in, o_ref):
    del dproj_in
    o_ref[...] = piece_ref[...]


def _adamw_update(w_ref, m_ref, v_ref, g, g_ref, d_ref, nm_ref, nv_ref):
    m = ADAM_B1 * m_ref[...] + (1.0 - ADAM_B1) * g
    v = ADAM_B2 * v_ref[...] + (1.0 - ADAM_B2) * (g * g)
    m_hat = m / (1.0 - ADAM_B1 ** ADAM_STEP)
    v_hat = v / (1.0 - ADAM_B2 ** ADAM_STEP)
    g_ref[...] = g
    d_ref[...] = -ADAM_LR * (m_hat / (jnp.sqrt(v_hat) + ADAM_EPS) + ADAM_WD * w_ref[...])
    nm_ref[...] = m
    nv_ref[...] = v


def _adamw_body(*refs, n_parts):
    w_ref, m_ref, v_ref = refs[:3]
    parts = refs[3:3 + n_parts]
    g = parts[0][...].astype(F32)
    for p in parts[1:]:
        g = g + p[...].astype(F32)
    _adamw_update(w_ref, m_ref, v_ref, g, *refs[3 + n_parts:])


def _adamw_shard_body(c_ref, w_ref, m_ref, v_ref, mine_ref, other_ref, g_ref, d_ref, nm_ref, nv_ref, *, nth):
    in_mine = pl.program_id(0) // nth == c_ref[0]
    g = jnp.where(in_mine, mine_ref[...], other_ref[...])
    _adamw_update(w_ref, m_ref, v_ref, g, g_ref, d_ref, nm_ref, nv_ref)


def _adamw_shard(w, m, v, halves, core, *, name, rows):
    shape = w.shape
    w2, m2, v2 = (a.reshape(-1, shape[-1]) for a in (w, m, v))
    r, c = w2.shape
    nth = r // 2 // rows
    assert 2 * nth * rows == r and halves[0].shape == (r // 2, c)
    spec = pl.BlockSpec((rows, c), lambda i, cr: (i, 0))
    mine_spec = pl.BlockSpec((rows, c), lambda i, cr: (jnp.where(i // nth == cr[0], i % nth, 0), 0))
    other_spec = pl.BlockSpec((rows, c), lambda i, cr: (jnp.where(i // nth == cr[0], 0, i % nth), 0))
    outs = pl.pallas_call(
        functools.partial(_adamw_shard_body, nth=nth), name=name,
        grid_spec=pltpu.PrefetchScalarGridSpec(
            num_scalar_prefetch=1, grid=(r // rows,),
            in_specs=[spec] * 3 + [mine_spec, other_spec], out_specs=[spec] * 4),
        out_shape=[jax.ShapeDtypeStruct((r, c), F32)] * 4,
        compiler_params=_params(("parallel",)),
    )(core, w2, m2, v2, *halves)
    return tuple(o.reshape(shape) for o in outs)


def _adamw(w, m, v, parts, *, name, rows=256):
    shape = w.shape
    w2, m2, v2 = (a.reshape(-1, shape[-1]) for a in (w, m, v))
    parts = [p.reshape(w2.shape) for p in parts]
    r, c = w2.shape
    rows = rows if r % rows == 0 else r
    spec = pl.BlockSpec((rows, c), lambda i: (i, 0))
    outs = pl.pallas_call(
        functools.partial(_adamw_body, n_parts=len(parts)), name=name, grid=(r // rows,),
        in_specs=[spec] * (3 + len(parts)), out_specs=[spec] * 4,
        out_shape=[jax.ShapeDtypeStruct((r, c), F32)] * 4,
        compiler_params=_params(("parallel",)),
    )(w2, m2, v2, *parts)
    return tuple(o.reshape(shape) for o in outs)


BIG = ("w_in", "w_glu", "w_branch_ssm", "w_branch_attn", "w_out")
SHARD_AXIS = dict(w_in=1, w_glu=1, w_branch_ssm=1, w_branch_attn=1, w_out=0)
HBM = pl.BlockSpec(memory_space=pl.ANY)


def _position():
    x, y, c = lax.axis_index("x"), lax.axis_index("y"), lax.axis_index("c")
    other_chips = [(1 - x, y), (x, 1 - y), (1 - x, 1 - y)]
    return x, y, c, other_chips


def _window(ref, axis, shard, n_shards, half=None):
    rows, cols = ref.shape[-2:]
    sel = [slice(None), slice(None)]
    size = ref.shape[-2 + axis] // n_shards
    sel[axis] = pl.ds(pl.multiple_of(shard * size, 128), size)
    if half is not None:
        hsize = ref.shape[-1 - axis] // 2
        sel[1 - axis] = pl.ds(pl.multiple_of(half * hsize, 128), hsize)
    return ref.at[tuple(sel)]


def _half(ref, axis, half):
    hsize = ref.shape[-1 - axis] // 2
    sel = [slice(None), slice(None)]
    sel[1 - axis] = pl.ds(pl.multiple_of(half * hsize, 128), hsize)
    return ref.at[tuple(sel)]


def _remote(src, dst, send_sem, recv_sem, device):
    return pltpu.make_async_remote_copy(src_ref=src, dst_ref=dst, send_sem=send_sem, recv_sem=recv_sem,
                                        device_id=device, device_id_type=MESH)


def _full_shapes(shards, names):
    out = []
    for k in names:
        s = list(shards[k].shape)
        s[SHARD_AXIS[k]] *= N_CHIPS
        out.append(jax.ShapeDtypeStruct(tuple(s), shards[k].dtype))
    return out


def _gather_exchange(shards, names):
    axes = tuple(SHARD_AXIS[k] for k in names)
    n = len(names)

    def copies(ins, outs, sems):
        send_sems, recv_sems, local_sems = sems
        x, y, c, chips = _position()
        me = 2 * x + y
        remote, local = [], []
        for w, ax in enumerate(axes):
            for r, (px, py) in enumerate(chips):
                remote.append(_remote(_half(ins[w], ax, c), _window(outs[w], ax, me, N_CHIPS, c),
                                      send_sems.at[3 * w + r], recv_sems.at[3 * w + r], (px, py, c)))
            local.append(pltpu.make_async_copy(ins[w], _window(outs[w], ax, me, N_CHIPS), local_sems.at[w]))
        return remote, local

    return _Exchange([shards[k] for k in names], _full_shapes(shards, names), (3 * n, 3 * n, n), copies)


SEM = pl.BlockSpec(memory_space=pltpu.SEMAPHORE)


def _gather_start_body(shard_ref, land_ref, send_sems, recv_sems, shard_thru, land_thru, token, *, axis):
    del shard_thru, land_thru
    x, y, c, chips = _position()
    me = 2 * x + y
    for r, (px, py) in enumerate(chips):
        _remote(_half(shard_ref, axis, c), _window(land_ref, axis, me, N_CHIPS, c),
                send_sems.at[r], recv_sems.at[r], (px, py, c)).start()
    token[...] = jnp.zeros_like(token)


def _gather_start(shard, name):
    axis = SHARD_AXIS[name]
    full = _full_shapes({name: shard}, (name,))[0]
    return pl.pallas_call(
        functools.partial(_gather_start_body, axis=axis), name="gather_start_" + name,
        out_shape=(pltpu.SemaphoreType.DMA((3,)), pltpu.SemaphoreType.DMA((3,)), pltpu.HBM(shard.shape, shard.dtype),
                   pltpu.HBM(full.shape, full.dtype), jax.ShapeDtypeStruct((8, 128), F32)),
        in_specs=(HBM, HBM), out_specs=(SEM, SEM, HBM, HBM, pl.BlockSpec(memory_space=pltpu.VMEM)),
        input_output_aliases={0: 2, 1: 3},
        compiler_params=pltpu.CompilerParams(has_side_effects=pltpu.SideEffectType.DATAFLOW_SIDE_EFFECTING),
    )(pltpu.with_memory_space_constraint(shard, pltpu.HBM),
      pltpu.with_memory_space_constraint(lax.empty(full.shape, full.dtype), pltpu.HBM))


def _gather_wait_body(shard_ref, land_ref, send_sems, recv_sems, after_ref, shard_dead, landed, *, axis):
    del after_ref, shard_dead, landed
    x, y, c, chips = _position()
    for r, (px, py) in enumerate(chips):
        cp = _remote(_half(shard_ref, axis, c), _window(land_ref, axis, 2 * px + py, N_CHIPS, c),
                     send_sems.at[r], recv_sems.at[r], (px, py, c))
        cp.wait_send()
        cp.wait_recv()


def _gather_wait(send_sems, recv_sems, shard_thru, land_thru, after, name):
    return pl.pallas_call(
        functools.partial(_gather_wait_body, axis=SHARD_AXIS[name]), name="gather_wait_" + name,
        out_shape=(pltpu.HBM(shard_thru.shape, shard_thru.dtype), pltpu.HBM(land_thru.shape, land_thru.dtype)),
        in_specs=(HBM, HBM, SEM, SEM, HBM), out_specs=(HBM, HBM), input_output_aliases={0: 0, 1: 1},
        compiler_params=pltpu.CompilerParams(has_side_effects=pltpu.SideEffectType.DATAFLOW_SIDE_EFFECTING),
    )(shard_thru, land_thru, send_sems, recv_sems, after)[1]


def _pass_on_body(*refs, axes, n_own):
    n = len(axes)
    ins, own = refs[:n], refs[n:n + n_own]
    refs = refs[:n] + refs[n + n_own:]
    outs = refs[n:2 * n]
    sbuf, rbuf = refs[2 * n:5 * n], refs[5 * n:8 * n]
    obuf = refs[8 * n:8 * n + n_own]
    send_sems, recv_sems, load_sems, store_sems, own_sems, placed_sems = refs[8 * n + n_own:]
    x, y, c, chips = _position()
    fetched = [pltpu.make_async_copy(own[w], obuf[w], own_sems.at[w]) for w in range(n_own)]
    for cp in fetched:
        cp.start()
    region = lambda ref, w, r, half: _window(ref, axes[w], 2 * chips[r][0] + chips[r][1], N_CHIPS, half)
    pairs = [(w, r) for w in range(n) for r in range(3)]
    loads = [pltpu.make_async_copy(region(ins[w], w, r, c), sbuf[3 * w + r], load_sems.at[3 * w + r]) for w, r in pairs]
    for cp in loads:
        cp.start()
    placed = []
    for w, cp in enumerate(fetched):
        cp.wait()
        placed.append(pltpu.make_async_copy(obuf[w], _window(outs[w], axes[w], 2 * x + y, N_CHIPS), placed_sems.at[w]))
        placed[-1].start()
    sends = []
    for i, cp in enumerate(loads):
        cp.wait()
        sends.append(_remote(sbuf[i], rbuf[i], send_sems.at[i], recv_sems.at[i], (x, y, 1 - c)))
        sends[-1].start()
    stores = []
    for i, (w, r) in enumerate(pairs):
        sends[i].wait_recv()
        stores.append(pltpu.make_async_copy(rbuf[i], region(outs[w], w, r, 1 - c), store_sems.at[i]))
        stores[-1].start()
    for cp in sends:
        cp.wait_send()
    for cp in stores + placed:
        cp.wait()


def _pass_on(fulls, names, own=()):
    axes = tuple(SHARD_AXIS[k] for k in names)
    n = len(names)
    assert len(own) in (0, n)
    bufs = []
    for a, ax in zip(fulls, axes):
        s = list(a.shape)
        s[ax] //= N_CHIPS
        s[1 - ax] //= 2
        bufs += [pltpu.VMEM(tuple(s), a.dtype)] * 3
    return pl.pallas_call(
        functools.partial(_pass_on_body, axes=axes, n_own=len(own)), name="pass_on_" + names[0],
        in_specs=[HBM] * (n + len(own)), out_specs=[HBM] * n,
        out_shape=[jax.ShapeDtypeStruct(a.shape, a.dtype) for a in fulls],
        scratch_shapes=bufs + bufs + [pltpu.VMEM(o.shape, o.dtype) for o in own]
        + [pltpu.SemaphoreType.DMA((3 * n,))] * 4 + [pltpu.SemaphoreType.DMA((n,))] * 2,
        input_output_aliases={i: i for i in range(n)},
        compiler_params=pltpu.CompilerParams(vmem_limit_bytes=VMEM_LIMIT),
    )(*fulls, *own)


def _chip_exchange(pairs, names):
    axes = tuple(SHARD_AXIS[k] for k in names)
    n = len(names)
    out_shape = []
    for p, ax in zip(pairs, axes):
        s = list(p.shape)
        s[ax] //= N_CHIPS
        out_shape.append(jax.ShapeDtypeStruct((4, *s), p.dtype))

    def copies(ins, outs, sems):
        send_sems, recv_sems, local_sems = sems
        x, y, c, chips = _position()
        me = 2 * x + y
        remote, local = [], []
        for w, ax in enumerate(axes):
            for r, (px, py) in enumerate(chips):
                remote.append(_remote(_window(ins[w], ax, 2 * px + py, N_CHIPS), outs[w].at[r],
                                      send_sems.at[3 * w + r], recv_sems.at[3 * w + r], (px, py, c)))
            local.append(pltpu.make_async_copy(_window(ins[w], ax, me, N_CHIPS), outs[w].at[3], local_sems.at[w]))
        return remote, local

    return _Exchange(pairs, out_shape, (3 * n, 3 * n, n), copies)


def _half_tile(n, h, nt, axis):
    return h * nt + n if axis == 1 else 2 * n + h


def _sibling_stream(n, nt, stage, recv, send_sems, recv_sems, credit, produce, consume):
    x, y, c, _ = _position()
    sibling = (x, y, 1 - c)

    def copy(slot):
        return _remote(stage.at[slot], recv.at[slot], send_sems.at[slot], recv_sems.at[slot], sibling)

    @pl.when(n < nt)
    def _():
        slot = n % 2

        @pl.when(n >= 2)
        def _():
            copy(slot).wait_send()
            pl.semaphore_wait(credit, 1)

        stage[slot] = produce().astype(stage.dtype)
        copy(slot).start()

    @pl.when(n >= 1)
    def _():
        slot = (n - 1) % 2
        copy(slot).wait_recv()
        consume(recv[slot])

        @pl.when(n - 1 < nt - 2)
        def _():
            pl.semaphore_signal(credit, 1, device_id=sibling, device_id_type=MESH)

    @pl.when(n == nt)
    def _():
        for slot in range(min(2, nt)):
            copy(slot).wait_send()


def _pair_reduce_body(c_ref, mine_ref, theirs_ref, out_ref, stage, recv, send_sems, recv_sems, credit, *, nt):
    del c_ref

    def consume(got):
        out_ref[...] = (mine_ref[...] + got.astype(F32)).astype(out_ref.dtype)

    _sibling_stream(pl.program_id(0), nt, stage, recv, send_sems, recv_sems, credit,
                    lambda: theirs_ref[...], consume)


def _pair_reduce(grad, core, axis, *, name, rows):
    r, c = grad.shape
    nt = r // 2 // rows
    assert nt * rows * 2 == r and (axis == 1 or rows == r // (2 * N_CHIPS))
    tile = lambda n, h: _half_tile(n, h, nt, axis)
    return pl.pallas_call(
        functools.partial(_pair_reduce_body, nt=nt), name=name,
        grid_spec=pltpu.PrefetchScalarGridSpec(
            num_scalar_prefetch=1, grid=(nt + 1,),
            in_specs=[pl.BlockSpec((rows, c), lambda n, cr: (tile(jnp.maximum(n - 1, 0), cr[0]), 0)),
                      pl.BlockSpec((rows, c), lambda n, cr: (tile(jnp.minimum(n, nt - 1), 1 - cr[0]), 0))],
            out_specs=pl.BlockSpec((rows, c), lambda n, cr: (jnp.maximum(n - 1, 0), 0)),
            scratch_shapes=[pltpu.VMEM((2, rows, c), BF16), pltpu.VMEM((2, rows, c), BF16),
                            pltpu.SemaphoreType.DMA((2,)), pltpu.SemaphoreType.DMA((2,)),
                            pltpu.SemaphoreType.REGULAR]),
        out_shape=jax.ShapeDtypeStruct((r // 2, c), BF16),
        compiler_params=_params(("arbitrary",)),
    )(core, grad, grad)


def _chip_add_share_body(c_ref, s0, s1, s2, s3, mine_ref, other_ref, stage, recv, send_sems, recv_sems, credit, *, nt):
    del c_ref

    def produce():
        total = s3[...].astype(F32) + s0[...].astype(F32) + s1[...].astype(F32) + s2[...].astype(F32)
        mine_ref[...] = total
        return total

    def consume(got):
        other_ref[...] = got

    _sibling_stream(pl.program_id(0), nt, stage, recv, send_sems, recv_sems, credit, produce, consume)


def _chip_add_share(slots, core, *, name, rows):
    _, r, c = slots.shape
    nt = r // rows
    assert nt * rows == r
    send = lambda j: pl.BlockSpec((None, rows, c), lambda n, cr: (j, jnp.minimum(n, nt - 1), 0))
    return pl.pallas_call(
        functools.partial(_chip_add_share_body, nt=nt), name=name,
        grid_spec=pltpu.PrefetchScalarGridSpec(
            num_scalar_prefetch=1, grid=(nt + 1,),
            in_specs=[send(j) for j in range(4)],
            out_specs=[pl.BlockSpec((rows, c), lambda n, cr: (jnp.minimum(n, nt - 1), 0)),
                       pl.BlockSpec((rows, c), lambda n, cr: (jnp.maximum(n - 1, 0), 0))],
            scratch_shapes=[pltpu.VMEM((2, rows, c), F32), pltpu.VMEM((2, rows, c), F32),
                            pltpu.SemaphoreType.DMA((2,)), pltpu.SemaphoreType.DMA((2,)),
                            pltpu.SemaphoreType.REGULAR]),
        out_shape=[jax.ShapeDtypeStruct((r, c), F32)] * 2,
        compiler_params=_params(("arbitrary",)),
    )(core, slots, slots, slots, slots)


PAIR_ROWS = dict(w_in=64, w_glu=128, w_branch_ssm=128, w_branch_attn=128, w_out=256)
SHARE_ROWS = dict(w_in=128, w_glu=128, w_branch_ssm=128, w_branch_attn=128, w_out=64)


def _pair_sums(grads, names, core):
    return [_pair_reduce(grads[k], core, SHARD_AXIS[k], name="pair_reduce_" + k, rows=PAIR_ROWS[k]) for k in names]


def _shard_halves(slots, names, core):
    return {k: _chip_add_share(s, core, name="chip_add_share_" + k, rows=SHARE_ROWS[k]) for k, s in zip(names, slots)}


SMALL = (("ssm_lambda_re", (1, 64, 64)), ("ssm_lambda_im", (1, 64, 64)), ("ssm_b_re", (1, 64, 64, 16)),
         ("ssm_b_im", (1, 64, 64, 16)), ("ssm_c_re", (1, 64, 16, 64)), ("ssm_c_im", (1, 64, 16, 64)),
         ("ssm_d", (1, 1024)), ("ssm_log_step", (1, 64)), ("attn_sinks", (1, 16)), ("rel_bias_table", (32, 16)),
         ("ln_gain", (1, 2048)), ("ln_bias", (1, 2048)))
SMALL_SIZE = sum(int(np.prod(s)) for _, s in SMALL)
PACK_ROWS = -(-(SMALL_SIZE + 1) // (8 * 128)) * 8


def _pack(values, extra=None):
    flat = [values[k].reshape(-1).astype(F32) for k, _ in SMALL]
    flat.append(jnp.zeros((1,), F32) if extra is None else extra.reshape(1))
    flat.append(jnp.zeros((PACK_ROWS * 128 - SMALL_SIZE - 1,), F32))
    return jnp.concatenate(flat).reshape(PACK_ROWS, 128)


def _unpack(packed):
    flat = packed.reshape(-1)
    out, off = {}, 0
    for k, s in SMALL:
        size = int(np.prod(s))
        out[k] = flat[off:off + size].reshape(s)
        off += size
    return out, flat[off]


def _small_exchange(packed):
    def copies(ins, outs, sems):
        send_sems, recv_sems, local_sems = sems
        x, y, c, _ = _position()
        remote = []
        for r in range(1, 8):
            peer = tuple(1 - v if (r >> s) & 1 else v for v, s in ((x, 2), (y, 1), (c, 0)))
            remote.append(_remote(ins[0], outs[0].at[r], send_sems.at[r - 1], recv_sems.at[r - 1], peer))
        return remote, [pltpu.make_async_copy(ins[0], outs[0].at[0], local_sems.at[0])]

    return _Exchange([packed], [jax.ShapeDtypeStruct((8, *packed.shape), F32)], (7, 7, 1), copies)


def _merge_exchanges(a, b):
    na, ma, sa = len(a.ins), len(a.out_shape), len(a.n_sems)

    def copies(ins, outs, sems):
        ra, la = a.copies(ins[:na], outs[:ma], sems[:sa])
        rb, lb = b.copies(ins[na:], outs[ma:], sems[sa:])
        return ra + rb, la + lb

    return _Exchange(a.ins + b.ins, a.out_shape + b.out_shape, a.n_sems + b.n_sems, copies)


def _small_sum_body(slots_ref, o_ref):
    x, y, c, _ = _position()
    me = 4 * x + 2 * y + c
    acc = slots_ref[me]
    for s in range(1, 8):
        acc = acc + slots_ref[jnp.bitwise_xor(me, s)]
    o_ref[...] = acc


def _small_sum(slots):
    vm = pl.BlockSpec(memory_space=pltpu.VMEM)
    return pl.pallas_call(
        functools.partial(_small_sum_body), name="small_sum",
        in_specs=[vm], out_specs=vm, out_shape=jax.ShapeDtypeStruct(slots.shape[1:], F32),
        compiler_params=pltpu.CompilerParams(vmem_limit_bytes=VMEM_LIMIT),
    )(slots)


LATE = BIG[1:]


def _local_step(x, target, shards, core, chip, lam_re, lam_im, b_re, b_im, c_re, c_im, d_skip,
                log_step, sinks, rel_bias_table, ln_gain, ln_bias):
    t = x.shape[0]
    send_sems, recv_sems, shard_thru, land_thru, token = _gather_start(shards["w_in"], "w_in")
    xb, xbt = _cast_and_transpose(x, token, name="cast_x")
    bucket = jnp.asarray(_bucket_table())
    p_re, p_im, b_bd, c_bd = _ssm_prepare(lam_re, lam_im, b_re, b_im, c_re, c_im, log_step)
    bias = _bias_build(rel_bias_table, bucket)

    act = dict(out_dtype=BF16)
    proj = _proj_shards(xb, shards["w_in"], chip, (0,), name="proj_own")
    landed = _gather_wait(send_sems, recv_sems, shard_thru, land_thru, proj, "w_in")
    (w_in,) = _pass_on([landed], BIG[:1], own=[shards["w_in"]])
    proj, landed = _proj_shards(xb, w_in, chip, (1, 2, 3), name="proj", base=proj,
                                exchange=_gather_exchange(shards, LATE))
    w_glu, w_bs, w_ba, w_out = _pass_on(list(landed), LATE)
    y_ssm, g_in, g_in_t, car_re, car_im = _ssm_forward(proj, p_re, p_im, b_bd, c_bd, d_skip)
    glu = _matmul(g_in, w_glu, "nn", name="glu", tm=1024, tn=2048, **act)
    h_ssm, h_ssm_t = _rowwise(functools.partial(_ssm_gate_fwd_body), "ssm_gate_fwd", t,
                              [(glu, 2 * D_SSM, 0), (proj, D_SSM, C_ZS)],
                              [((t, D_SSM), BF16, D_SSM, 0), ((D_SSM, t), BF16, D_SSM, "T")])
    attn, lse = _attn_forward(proj, sinks, bias)
    h_attn, h_attn_t = _rowwise(functools.partial(_attn_gate_fwd_body), "attn_gate_fwd", t,
                                [(attn, D_ATTN, 0), (proj, D_ATTN, C_ZA)],
                                [((t, D_ATTN), BF16, D_ATTN, 0), ((D_ATTN, t), BF16, D_ATTN, "T")])
    bs = _matmul(h_ssm, w_bs, "nn", name="branch_ssm", tm=1024, tn=2048, **act)
    ba = _matmul(h_attn, w_ba, "nn", name="branch_attn", tm=1024, tn=2048, **act)
    gl_in = (proj, 2 * D_MODEL, C_GL)
    merged, merged_t = _rowwise(functools.partial(_merge_fwd_body), "merge_fwd", t,
                                [(bs, D_MODEL, 0), (ba, D_MODEL, 0), gl_in],
                                [((t, D_MODEL), BF16, D_MODEL, 0), ((D_MODEL, t), BF16, D_MODEL, "T")])
    out = _matmul(merged, w_out, "nn", name="out_proj", tm=1024, tn=1024, **act)
    drb, loss, g_gain, g_bias = _rowwise(
        functools.partial(_ln_loss_body), "ln_loss", t,
        [(x, D_MODEL, 0), (out, D_MODEL, 0), (target, D_MODEL, 0), (ln_gain, None, 0), (ln_bias, None, 0)],
        [((t, D_MODEL), BF16, D_MODEL, 0), ((1, 1), F32, None, 0),
         ((1, D_MODEL), F32, None, 0), ((1, D_MODEL), F32, None, 0)])

    g_w_out = _matmul(merged_t, drb, "nn", name="g_w_out", tm=512, tn=512)
    d_merged = _matmul(drb, w_out, "nt", name="d_merged", tm=1024, tn=1024, **act)
    d_bs, d_ba, dproj = _rowwise(
        functools.partial(_merge_bwd_body), "merge_bwd", t,
        [(d_merged, D_MODEL, 0), (bs, D_MODEL, 0), (ba, D_MODEL, 0), gl_in],
        [((t, D_MODEL), BF16, D_MODEL, 0), ((t, D_MODEL), BF16, D_MODEL, 0),
         ((t, D_IN), BF16, 2 * D_MODEL, C_GL)])
    g_w_bs = _matmul(h_ssm_t, d_bs, "nn", name="g_w_branch_ssm", tm=512, tn=512)
    d_h_ssm = _matmul(d_bs, w_bs, "nt", name="d_h_ssm", tm=1024, tn=1024, **act)
    g_w_ba = _matmul(h_attn_t, d_ba, "nn", name="g_w_branch_attn", tm=512, tn=512)
    d_h_attn = _matmul(d_ba, w_ba, "nt", name="d_h_attn", tm=1024, tn=1024, **act)

    d_attn, dproj = _rowwise(
        functools.partial(_attn_gate_bwd_body), "attn_gate_bwd", t,
        [(d_h_attn, D_ATTN, 0), (attn, D_ATTN, 0), (proj, D_ATTN, C_ZA), (dproj, None, "any")],
        [((t, D_ATTN), BF16, D_ATTN, 0), ((t, D_IN), BF16, D_ATTN, C_ZA)], aliases={3: 1})
    dproj, d_kv, g_bias_full, g_sinks = _attn_backward(proj, sinks, bias, d_attn, attn, lse, dproj)
    (dproj,) = _rowwise(functools.partial(_place_body), "place_dkv", t,
                        [(d_kv, 2 * D_KV, 0), (dproj, None, "any")],
                        [((t, D_IN), BF16, 2 * D_KV, C_K)], aliases={1: 0})
    g_table = _bias_grad(g_bias_full, bucket)

    d_glu, dproj = _rowwise(
        functools.partial(_ssm_gate_bwd_body), "ssm_gate_bwd", t,
        [(d_h_ssm, D_SSM, 0), (glu, 2 * D_SSM, 0), (proj, D_SSM, C_ZS), (dproj, None, "any")],
        [((t, 2 * D_SSM), BF16, 2 * D_SSM, 0), ((t, D_IN), BF16, D_SSM, C_ZS)], aliases={3: 1})
    g_w_glu = _matmul(g_in_t, d_glu, "nn", name="g_w_glu", tm=512, tn=512)
    d_g_in = _matmul(d_glu, w_glu, "nt", name="d_g_in", tm=1024, tn=1024, **act)
    dproj, g_bbd, g_cbd, g_lam_re, g_lam_im, g_d = _ssm_backward(
        proj, d_g_in, y_ssm, car_re, car_im, p_re, p_im, b_bd, c_bd, d_skip, dproj)
    g_lr, g_li, g_br, g_bi, g_cr, g_ci, g_ls = _ssm_param_grads(
        lam_re, lam_im, b_re, b_im, log_step, g_lam_re, g_lam_im, g_bbd, g_cbd)

    late = dict(w_glu=g_w_glu, w_branch_ssm=g_w_bs, w_branch_attn=g_w_ba, w_out=g_w_out)
    g_w_in, late_slots = _matmul(xbt, dproj, "nn", name="g_w_in", tm=512, tn=512,
                                 exchange=_chip_exchange(_pair_sums(late, LATE, core), LATE))
    small = dict(ssm_lambda_re=g_lr, ssm_lambda_im=g_li, ssm_b_re=g_br, ssm_b_im=g_bi, ssm_c_re=g_cr,
                 ssm_c_im=g_ci, ssm_d=g_d, ssm_log_step=g_ls, attn_sinks=g_sinks[:, :N_Q_HEADS],
                 rel_bias_table=g_table, ln_gain=g_gain, ln_bias=g_bias)
    last = _merge_exchanges(_chip_exchange(_pair_sums(dict(w_in=g_w_in), BIG[:1], core), BIG[:1]),
                            _small_exchange(_pack(small, loss)))
    grad_x, (in_slots, small_slots) = _matmul(dproj, w_in, "nt", name="grad_x", tm=512, tn=512, res=drb,
                                              res_scale=ALPHA, exchange=last)
    big = {**_shard_halves([in_slots], BIG[:1], core), **_shard_halves(late_slots, LATE, core)}
    return grad_x, big, _small_sum(small_slots)


WEIGHTS = ("w_in", "ssm_lambda_re", "ssm_lambda_im", "ssm_b_re", "ssm_b_im", "ssm_c_re", "ssm_c_im", "ssm_d",
           "ssm_log_step", "w_glu", "attn_sinks", "rel_bias_table", "w_branch_ssm", "w_branch_attn", "w_out",
           "ln_gain", "ln_bias")


def kernel(x, w_in, ssm_lambda_re, ssm_lambda_im, ssm_b_re, ssm_b_im, ssm_c_re, ssm_c_im, ssm_d, ssm_log_step, w_glu, attn_sinks, rel_bias_table, w_branch_ssm, w_branch_attn, w_out, ln_gain, ln_bias, loss_target, m_w_in, m_ssm_lambda_re, m_ssm_lambda_im, m_ssm_b_re, m_ssm_b_im, m_ssm_c_re, m_ssm_c_im, m_ssm_d, m_ssm_log_step, m_w_glu, m_attn_sinks, m_rel_bias_table, m_w_branch_ssm, m_w_branch_attn, m_w_out, m_ln_gain, m_ln_bias, v_w_in, v_ssm_lambda_re, v_ssm_lambda_im, v_ssm_b_re, v_ssm_b_im, v_ssm_c_re, v_ssm_c_im, v_ssm_d, v_ssm_log_step, v_w_glu, v_attn_sinks, v_rel_bias_table, v_w_branch_ssm, v_w_branch_attn, v_w_out, v_ln_gain, v_ln_bias):
    given = dict(locals())
    w = {k: given[k] for k in WEIGHTS}
    m = {k: given["m_" + k] for k in WEIGHTS}
    v = {k: given["v_" + k] for k in WEIGHTS}

    core = lax.axis_index("c").astype(jnp.int32).reshape(1)
    shards = {k: _cast(w[k][0], BF16, name="cast_" + k) for k in BIG}
    chip = (2 * lax.axis_index("x") + lax.axis_index("y")).astype(jnp.int32).reshape(1)
    grad_x, g_shard, g_packed = _local_step(
        x[0], loss_target[0], shards, core, chip, ssm_lambda_re[0], ssm_lambda_im[0], ssm_b_re[0],
        ssm_b_im[0], ssm_c_re[0], ssm_c_im[0], ssm_d, ssm_log_step, attn_sinks, rel_bias_table, ln_gain, ln_bias)
    loss_sum = _unpack(g_packed)[1]

    grad, delta, new_m, new_v = {}, {}, {}, {}
    for k in BIG:
        grad[k], delta[k], new_m[k], new_v[k] = _adamw_shard(w[k], m[k], v[k], g_shard[k], core,
                                                             name="adamw_" + k, rows=SHARE_ROWS[k])
    gs, ds, ms, vs = _adamw(_pack(w), _pack(m), _pack(v), [g_packed], name="adamw_small")
    for dst, packed in ((grad, gs), (delta, ds), (new_m, ms), (new_v, vs)):
        dst.update(_unpack(packed)[0])

    return (loss_sum, grad_x[None], *[grad[k] for k in WEIGHTS], *[delta[k] for k in WEIGHTS],
            *[new_m[k] for k in WEIGHTS], *[new_v[k] for k in WEIGHTS])
```

```python
import functools
import math

import numpy as np
import jax
import jax.numpy as jnp
from jax import lax
from jax.experimental import pallas as pl
from jax.experimental.pallas import tpu as pltpu

F32 = jnp.float32
BF16 = jnp.bfloat16

D_MODEL = 2048
D_SSM = 1024
SSM_GROUP = 16
N_GROUPS = 64
SSM_STATE = 64
N_LANES = N_GROUPS * SSM_STATE
N_Q_HEADS = 16
N_KV_HEADS = 4
HEAD_DIM = 64
Q_PER_KV = 4
D_ATTN = 1024
D_KV = 256
WINDOW = 128
BLOCK = 128
N_BUCKETS = 32
MAX_DISTANCE = 128
D_IN = 8704
ALPHA = 2.0 ** 0.25
LN_EPS = 1e-5
NEG_INF = -1e30
ATTN_SCALE = HEAD_DIM ** -0.5

C_U, C_ZS, C_Q, C_K, C_V, C_ZA, C_GL = 0, 1024, 2048, 3072, 3328, 3584, 4608

ADAM_LR = 0.001
ADAM_B1 = 0.9
ADAM_B2 = 0.999
ADAM_EPS = 1e-08
ADAM_WD = 0.01
ADAM_STEP = 10

N_CHIPS = 4
MESH = pl.DeviceIdType.MESH

SSM_CHUNK = 256
SEG_LEN = SSM_CHUNK // 8
SLAB_LANES = 512
N_SLABS = N_LANES // SLAB_LANES
SLAB_CH = D_SSM // N_SLABS

VMEM_LIMIT = 60 * 1024 * 1024


def _params(sem=None, **kw):
    return pltpu.CompilerParams(dimension_semantics=sem, vmem_limit_bytes=VMEM_LIMIT, **kw)


_DIMS = {"nn": (((1,), (0,)), ((), ())), "nt": (((1,), (1,)), ((), ())), "tn": (((0,), (0,)), ((), ()))}


class _Exchange:
    def __init__(self, ins, out_shape, n_sems, copies):
        self.ins, self.out_shape, self.n_sems, self.copies = list(ins), list(out_shape), list(n_sems), copies

    def start(self, ins, outs, sems):
        remote, local = self.copies(ins, outs, sems)
        for cp in remote + local:
            cp.start()

    def finish(self, ins, outs, sems):
        remote, local = self.copies(ins, outs, sems)
        for cp in remote:
            cp.wait_recv()
        for cp in remote:
            cp.wait_send()
        for cp in local:
            cp.wait()


def _mm_body(*refs, dims, nk, res_scale, exchange, grid):
    n_ex_in = len(exchange.ins) if exchange else 0
    n_ex_out = len(exchange.out_shape) if exchange else 0
    n_in = 2 + (res_scale is not None) + n_ex_in
    a_ref, b_ref = refs[0], refs[1]
    r_ref = refs[2] if res_scale is not None else None
    ex_in = refs[n_in - n_ex_in:n_in]
    o_ref, ex_out = refs[n_in], refs[n_in + 1:n_in + 1 + n_ex_out]
    acc_ref, ex_sems = refs[n_in + 1 + n_ex_out], refs[n_in + 2 + n_ex_out:]
    i, j, k = pl.program_id(0), pl.program_id(1), pl.program_id(2)

    if exchange:
        @pl.when(jnp.logical_and(jnp.logical_and(i == 0, j == 0), k == 0))
        def _():
            exchange.start(ex_in, ex_out, ex_sems)

    part = lax.dot_general(a_ref[...].astype(BF16), b_ref[...].astype(BF16), _DIMS[dims],
                           preferred_element_type=F32)

    def finish(acc):
        if r_ref is not None:
            acc = acc + res_scale * r_ref[...].astype(F32)
        o_ref[...] = acc.astype(o_ref.dtype)

    if nk == 1:
        finish(part)
    else:
        @pl.when(k == 0)
        def _():
            acc_ref[...] = part

        @pl.when(k > 0)
        def _():
            acc_ref[...] += part

        @pl.when(k == nk - 1)
        def _():
            finish(acc_ref[...])

    if exchange:
        @pl.when(jnp.logical_and(jnp.logical_and(i == grid[0] - 1, j == grid[1] - 1), k == grid[2] - 1))
        def _():
            exchange.finish(ex_in, ex_out, ex_sems)


def _matmul(a, b, dims, *, name, out_dtype=F32, tm=512, tn=512, tk=None, res=None, res_scale=None, exchange=None):
    if dims == "nn":
        (m, kk), n = a.shape, b.shape[1]
    elif dims == "nt":
        (m, kk), n = a.shape, b.shape[0]
    else:
        (kk, m), n = a.shape, b.shape[1]
    tm, tn = min(tm, m), min(tn, n)
    tk = kk if tk is None else min(tk, kk)
    assert m % tm == 0 and n % tn == 0 and kk % tk == 0, (name, m, n, kk, tm, tn, tk)
    nk = kk // tk
    a_spec = {"nn": pl.BlockSpec((tm, tk), lambda i, j, k: (i, k)),
              "nt": pl.BlockSpec((tm, tk), lambda i, j, k: (i, k)),
              "tn": pl.BlockSpec((tk, tm), lambda i, j, k: (k, i))}[dims]
    b_spec = {"nn": pl.BlockSpec((tk, tn), lambda i, j, k: (k, j)),
              "nt": pl.BlockSpec((tn, tk), lambda i, j, k: (j, k)),
              "tn": pl.BlockSpec((tk, tn), lambda i, j, k: (k, j))}[dims]
    in_specs, args = [a_spec, b_spec], [a, b]
    if res is not None:
        in_specs.append(pl.BlockSpec((tm, tn), lambda i, j, k: (i, j)))
        args.append(res)
    grid = (m // tm, n // tn, nk)
    out_specs = [pl.BlockSpec((tm, tn), lambda i, j, k: (i, j))]
    out_shape = [jax.ShapeDtypeStruct((m, n), out_dtype)]
    scratch = [pltpu.VMEM((tm, tn), F32)]
    if exchange:
        in_specs += [pl.BlockSpec(memory_space=pl.ANY)] * len(exchange.ins)
        args += exchange.ins
        out_specs += [pl.BlockSpec(memory_space=pl.ANY)] * len(exchange.out_shape)
        out_shape += exchange.out_shape
        scratch += [pltpu.SemaphoreType.DMA((s,)) for s in exchange.n_sems]
    outs = pl.pallas_call(
        functools.partial(_mm_body, dims=dims, nk=nk, res_scale=res_scale if res is not None else None,
                          exchange=exchange, grid=grid),
        name=name, grid=grid, in_specs=in_specs, out_specs=out_specs, out_shape=out_shape, scratch_shapes=scratch,
        compiler_params=_params(("arbitrary",) * 3 if exchange else ("parallel", "parallel", "arbitrary")),
    )(*args)
    return (outs[0], outs[1:]) if exchange else outs[0]


def _proj_shards_body(chip_ref, *refs, n_ex_in, has_base, exchange, grid):
    del chip_ref
    refs = list(refs)
    if has_base:
        del refs[2 + n_ex_in]
    _mm_body(*refs, dims="nn", nk=1, res_scale=None, exchange=exchange, grid=grid)


def _proj_shards(xb, w, chip, offsets, *, name, base=None, exchange=None, tm=1024):
    t, d = xb.shape
    cs = D_IN // N_CHIPS
    own = w.shape[1] == cs
    tm = min(tm, t)
    col = lambda j, cr: (cr[0] + offsets[0] + j) % N_CHIPS
    grid = (len(offsets), t // tm, 1)
    in_specs = [pl.BlockSpec((tm, d), lambda j, i, k, cr: (i, 0)),
                pl.BlockSpec((d, cs), lambda j, i, k, cr: (0, 0 if own else col(j, cr)))]
    args = [xb, w]
    out_specs = [pl.BlockSpec((tm, cs), lambda j, i, k, cr: (i, col(j, cr)))]
    out_shape = [jax.ShapeDtypeStruct((t, D_IN), BF16)]
    scratch = [pltpu.VMEM((8, 128), F32)]
    n_ex_in = 0
    if exchange:
        n_ex_in = len(exchange.ins)
        in_specs += [pl.BlockSpec(memory_space=pl.ANY)] * n_ex_in
        args += exchange.ins
        out_specs += [pl.BlockSpec(memory_space=pl.ANY)] * len(exchange.out_shape)
        out_shape += exchange.out_shape
        scratch += [pltpu.SemaphoreType.DMA((s,)) for s in exchange.n_sems]
    aliases = {}
    if base is not None:
        in_specs.append(pl.BlockSpec(memory_space=pl.ANY))
        args.append(base)
        aliases = {len(args): 0}
    outs = pl.pallas_call(
        functools.partial(_proj_shards_body, n_ex_in=n_ex_in, has_base=base is not None, exchange=exchange, grid=grid),
        name=name,
        grid_spec=pltpu.PrefetchScalarGridSpec(num_scalar_prefetch=1, grid=grid, in_specs=in_specs,
                                               out_specs=out_specs, scratch_shapes=scratch),
        out_shape=out_shape, input_output_aliases=aliases,
        compiler_params=_params(("arbitrary",) * 3),
    )(chip, *args)
    return (outs[0], outs[1:]) if exchange else outs[0]


def _sigmoid(v):
    return 1.0 / (1.0 + jnp.exp(-v))


def _silu_and_grad(z):
    s = _sigmoid(z)
    return z * s, s * (1.0 + z * (1.0 - s))


def _cast_body(x_ref, o_ref):
    o_ref[...] = x_ref[...].astype(o_ref.dtype)


def _cast_and_transpose_body(x_ref, after_ref, o_ref, ot_ref):
    del after_ref
    xb = x_ref[...].astype(BF16)
    o_ref[...] = xb
    ot_ref[...] = xb.T


def _cast_and_transpose(x, after, *, name, rows=512):
    m, n = x.shape
    rows = min(rows, m)
    return pl.pallas_call(
        functools.partial(_cast_and_transpose_body), name=name, grid=(m // rows,),
        in_specs=[pl.BlockSpec((rows, n), lambda i: (i, 0)), pl.BlockSpec(memory_space=pl.ANY)],
        out_specs=[pl.BlockSpec((rows, n), lambda i: (i, 0)), pl.BlockSpec((n, rows), lambda i: (0, i))],
        out_shape=[jax.ShapeDtypeStruct((m, n), BF16), jax.ShapeDtypeStruct((n, m), BF16)],
        compiler_params=_params(("parallel",)),
    )(x, after)


def _cast(x, dtype, *, name, rows=512):
    m, n = x.shape
    rows = min(rows, m)
    return pl.pallas_call(
        functools.partial(_cast_body), name=name, grid=(m // rows,),
        in_specs=[pl.BlockSpec((rows, n), lambda i: (i, 0))],
        out_specs=pl.BlockSpec((rows, n), lambda i: (i, 0)),
        out_shape=jax.ShapeDtypeStruct((m, n), dtype),
        compiler_params=_params(("parallel",)),
    )(x)


def _lam_bar(lr, li, ls):
    step = jnp.exp(ls)
    er = jnp.exp(lr * step)
    return step, er * jnp.cos(li * step), er * jnp.sin(li * step)


def _ssm_pow_body(lr_ref, li_ref, ls_ref, pr_ref, pi_ref):
    _, ar, ai = _lam_bar(lr_ref[...], li_ref[...], ls_ref[...])
    cr, ci = ar, ai
    for i in range(SEG_LEN):
        pr_ref[pl.ds(i, 1), :] = cr
        pi_ref[pl.ds(i, 1), :] = ci
        cr, ci = cr * ar - ci * ai, cr * ai + ci * ar


def _ssm_bbar_body(lr_ref, li_ref, ls_ref, br_ref, bi_ref, or_ref, oi_ref):
    lr, li = lr_ref[...], li_ref[...]
    _, ar, ai = _lam_bar(lr, li, ls_ref[...])
    d = lr * lr + li * li
    ir, ii = lr / d, -li / d
    nr, ni = ar - 1.0, ai
    cr, ci = nr * ir - ni * ii, nr * ii + ni * ir
    br, bi = br_ref[...], bi_ref[...]
    or_ref[...] = cr * br - ci * bi
    oi_ref[...] = cr * bi + ci * br


def _ssm_param_bwd_body(lr_ref, li_ref, ls_ref, br_ref, bi_ref, glr_ref, gli_ref, gbr_ref, gbi_ref,
                        dlr_ref, dli_ref, dls_ref, dbr_ref, dbi_ref):
    lr, li = lr_ref[...], li_ref[...]
    step, ar, ai = _lam_bar(lr, li, ls_ref[...])
    d = lr * lr + li * li
    ir, ii = lr / d, -li / d
    nr, ni = ar - 1.0, ai
    cr, ci = nr * ir - ni * ii, nr * ii + ni * ir
    br, bi, gbr, gbi = br_ref[...], bi_ref[...], gbr_ref[...], gbi_ref[...]
    dbr_ref[...] = cr * gbr + ci * gbi
    dbi_ref[...] = cr * gbi - ci * gbr
    gcr = jnp.sum(br * gbr + bi * gbi, axis=1, keepdims=True)
    gci = jnp.sum(br * gbi - bi * gbr, axis=1, keepdims=True)
    gnr, gni = ir * gcr + ii * gci, ir * gci - ii * gcr
    gir, gii = nr * gcr + ni * gci, nr * gci - ni * gcr
    gtr, gti = glr_ref[...] + gnr, gli_ref[...] + gni
    i2r, i2i = ir * ir - ii * ii, 2.0 * ir * ii
    g1r, g1i = -(i2r * gir + i2i * gii), -(i2r * gii - i2i * gir)
    g2r, g2i = step * (ar * gtr + ai * gti), step * (ar * gti - ai * gtr)
    mr, mi = lr * ar - li * ai, lr * ai + li * ar
    dlr_ref[...] = g1r + g2r
    dli_ref[...] = g1i + g2i
    dls_ref[...] = (mr * gtr + mi * gti) * step


def _whole(shape):
    return pl.BlockSpec(shape, lambda *_: (0,) * len(shape))


def _ssm_prepare(lam_re, lam_im, b_re, b_im, c_re, c_im, log_step):
    row = lambda a: a.reshape(1, N_LANES)
    col = lambda a: a.reshape(N_LANES, 1)
    ls = jnp.repeat(log_step.reshape(N_GROUPS), SSM_STATE)
    p_re, p_im = pl.pallas_call(
        functools.partial(_ssm_pow_body), name="ssm_pow",
        in_specs=[_whole((1, N_LANES))] * 3, out_specs=[_whole((SEG_LEN, N_LANES))] * 2,
        out_shape=[jax.ShapeDtypeStruct((SEG_LEN, N_LANES), F32)] * 2, grid=(1,),
    )(row(lam_re), row(lam_im), row(ls))
    bb_re, bb_im = pl.pallas_call(
        functools.partial(_ssm_bbar_body), name="ssm_bbar",
        in_specs=[_whole((N_LANES, 1))] * 3 + [_whole((N_LANES, SSM_GROUP))] * 2,
        out_specs=[_whole((N_LANES, SSM_GROUP))] * 2,
        out_shape=[jax.ShapeDtypeStruct((N_LANES, SSM_GROUP), F32)] * 2, grid=(1,),
    )(col(lam_re), col(lam_im), col(ls), b_re.reshape(N_LANES, SSM_GROUP), b_im.reshape(N_LANES, SSM_GROUP))
    eye = jnp.eye(8, dtype=F32)

    def b_slabs(bb):
        t = bb.reshape(N_SLABS, 8, SSM_STATE, SSM_GROUP).transpose(0, 1, 3, 2)
        return (t[:, :, :, None, :] * eye[None, :, None, :, None]).reshape(N_SLABS, SLAB_CH, SLAB_LANES)

    def c_slabs(c):
        t = c.reshape(N_SLABS, 8, SSM_GROUP, SSM_STATE).transpose(0, 1, 3, 2)
        return (t[:, :, :, None, :] * eye[None, :, None, :, None]).reshape(N_SLABS, SLAB_LANES, SLAB_CH)

    b_bd = jnp.concatenate([b_slabs(bb_re), b_slabs(bb_im)], axis=2).astype(BF16)
    c_bd = jnp.concatenate([c_slabs(c_re.reshape(N_GROUPS, SSM_GROUP, SSM_STATE)),
                            -c_slabs(c_im.reshape(N_GROUPS, SSM_GROUP, SSM_STATE))], axis=1).astype(BF16)
    return p_re, p_im, b_bd, c_bd


def _diag_blocks_b(g):
    t = g.reshape(N_SLABS, 8, SSM_GROUP, 8, SSM_STATE)
    t = jnp.stack([t[:, i, :, i, :] for i in range(8)], axis=1)
    return t.transpose(0, 1, 3, 2).reshape(N_LANES, SSM_GROUP)


def _diag_blocks_c(g):
    t = g.reshape(N_SLABS, 8, SSM_STATE, 8, SSM_GROUP)
    t = jnp.stack([t[:, i, :, i, :] for i in range(8)], axis=1)
    return t.transpose(0, 1, 3, 2).reshape(N_GROUPS, SSM_GROUP, SSM_STATE)


def _ssm_param_grads(lam_re, lam_im, b_re, b_im, log_step, g_lam_re, g_lam_im, g_bbd, g_cbd):
    col = lambda a: a.reshape(N_LANES, 1)
    ls = jnp.repeat(log_step.reshape(N_GROUPS), SSM_STATE)
    gbr = _diag_blocks_b(g_bbd[:, :, :SLAB_LANES])
    gbi = _diag_blocks_b(g_bbd[:, :, SLAB_LANES:])
    outs = pl.pallas_call(
        functools.partial(_ssm_param_bwd_body), name="ssm_param_bwd", grid=(1,),
        in_specs=[_whole((N_LANES, 1))] * 3 + [_whole((N_LANES, SSM_GROUP))] * 2 + [_whole((N_LANES, 1))] * 2
        + [_whole((N_LANES, SSM_GROUP))] * 2,
        out_specs=[_whole((N_LANES, 1))] * 3 + [_whole((N_LANES, SSM_GROUP))] * 2,
        out_shape=[jax.ShapeDtypeStruct((N_LANES, 1), F32)] * 3 + [jax.ShapeDtypeStruct((N_LANES, SSM_GROUP), F32)] * 2,
    )(col(lam_re), col(lam_im), col(ls), b_re.reshape(N_LANES, SSM_GROUP), b_im.reshape(N_LANES, SSM_GROUP),
      col(g_lam_re), col(g_lam_im), gbr, gbi)
    dlr, dli, dls, dbr, dbi = outs
    d_c_re = _diag_blocks_c(g_cbd[:, :SLAB_LANES, :])
    d_c_im = -_diag_blocks_c(g_cbd[:, SLAB_LANES:, :])
    return (dlr.reshape(1, N_GROUPS, SSM_STATE), dli.reshape(1, N_GROUPS, SSM_STATE),
            dbr.reshape(1, N_GROUPS, SSM_STATE, SSM_GROUP), dbi.reshape(1, N_GROUPS, SSM_STATE, SSM_GROUP),
            d_c_re[None], d_c_im[None], dls.reshape(N_GROUPS, SSM_STATE).sum(axis=1).reshape(1, N_GROUPS))


def _bcast8(v):
    return jnp.broadcast_to(v, (8, v.shape[1]))


def _segment_permutation():
    p = np.zeros((SSM_CHUNK, SSM_CHUNK), np.float32)
    rows = np.arange(SSM_CHUNK)
    p[rows, (rows % 8) * SEG_LEN + rows // 8] = 1.0
    return p


def _permute_exact(perm, val, pieces):
    out, rest = None, val
    for n in range(pieces):
        part = rest.astype(BF16)
        moved = jnp.dot(perm, part, preferred_element_type=F32)
        out = moved if out is None else out + moved
        if n + 1 < pieces:
            rest = rest - part.astype(F32)
    return out


def _scan_buffer():
    return pltpu.VMEM((SSM_CHUNK, N_LANES), F32)


def _lanes(k):
    return pl.ds(k * SLAB_LANES, SLAB_LANES)


def _tile(i):
    return pl.ds(i * 8 if isinstance(i, int) else pl.multiple_of(i * 8, 8), 8)


SCAN_LANES = 1024


def _scan_lanes(k):
    return pl.ds(k * SCAN_LANES, SCAN_LANES)


def _seg_get(ref, k, i):
    return ref[_tile(i), _scan_lanes(k)]


def _seg_put(ref, k, i, val):
    ref[_tile(i), _scan_lanes(k)] = val


def _slab_get(ref, k):
    return ref[:, _lanes(k)]


def _slab_put(ref, k, val):
    ref[:, _lanes(k)] = val


def _scan_forward(s_re, s_im, p_re, p_im, car_re, car_im, sp_re=None, sp_im=None):
    for k in range(N_LANES // SCAN_LANES):
        ln = _scan_lanes(k)
        ar, ai =_bcast8(p_re[pl.ds(0, 1), ln]), _bcast8(p_im[pl.ds(0, 1), ln])

        def step(i, s, k=k, ar=ar, ai=ai):
            sr, si = s
            nr = ar * sr - ai * si + _seg_get(s_re, k, i)
            ni = ar * si + ai * sr + _seg_get(s_im, k, i)
            _seg_put(s_re, k, i, nr)
            _seg_put(s_im, k, i, ni)
            return nr, ni

        zero = jnp.zeros((8, SCAN_LANES), F32)
        er, ei = lax.fori_loop(0, SEG_LEN, step, (zero, zero), unroll=4)
        lr, li = p_re[pl.ds(SEG_LEN - 1, 1), ln], p_im[pl.ds(SEG_LEN - 1, 1), ln]
        cr, ci = car_re[pl.ds(0, 1), ln], car_im[pl.ds(0, 1), ln]
        rows_r, rows_i = [], []
        for r in range(8):
            rows_r.append(cr)
            rows_i.append(ci)
            cr, ci = er[r:r + 1] + lr * cr - li * ci, ei[r:r + 1] + lr * ci + li * cr
        pr8, pi8 = jnp.concatenate(rows_r, axis=0), jnp.concatenate(rows_i, axis=0)
        car_re[:, ln] = _bcast8(cr)
        car_im[:, ln] = _bcast8(ci)
        if sp_re is not None:
            sp_re[:, ln] = pr8
            sp_im[:, ln] = pi8

        def fix(i, _, k=k, ln=ln, pr8=pr8, pi8=pi8):
            qr, qi = _bcast8(p_re[pl.ds(i, 1), ln]), _bcast8(p_im[pl.ds(i, 1), ln])
            _seg_put(s_re, k, i, _seg_get(s_re, k, i) + qr * pr8 - qi * pi8)
            _seg_put(s_im, k, i, _seg_get(s_im, k, i) + qr * pi8 + qi * pr8)
            return 0

        lax.fori_loop(0, SEG_LEN, fix, 0, unroll=4)


def _scan_backward(g_re, g_im, s_re, s_im, sp_re, sp_im, p_re, p_im, car_re, car_im, acc_re, acc_im):
    for k in range(N_LANES // SCAN_LANES):
        ln = _scan_lanes(k)
        ar, ai =_bcast8(p_re[pl.ds(0, 1), ln]), -_bcast8(p_im[pl.ds(0, 1), ln])

        def step(j, s, k=k, ar=ar, ai=ai):
            i = SEG_LEN - 1 - j
            sr, si = s
            nr = ar * sr - ai * si + _seg_get(g_re, k, i)
            ni = ar * si + ai * sr + _seg_get(g_im, k, i)
            _seg_put(g_re, k, i, nr)
            _seg_put(g_im, k, i, ni)
            return nr, ni

        zero = jnp.zeros((8, SCAN_LANES), F32)
        er, ei = lax.fori_loop(0, SEG_LEN, step, (zero, zero), unroll=4)
        lr, li = p_re[pl.ds(SEG_LEN - 1, 1), ln], -p_im[pl.ds(SEG_LEN - 1, 1), ln]
        cr, ci = car_re[pl.ds(0, 1), ln], car_im[pl.ds(0, 1), ln]
        rows_r, rows_i = [None] * 8, [None] * 8
        for r in range(7, -1, -1):
            rows_r[r], rows_i[r] = cr, ci
            cr, ci = er[r:r + 1] + lr * cr - li * ci, ei[r:r + 1] + lr * ci + li * cr
        nr8, ni8 = jnp.concatenate(rows_r, axis=0), jnp.concatenate(rows_i, axis=0)
        car_re[:, ln] = _bcast8(cr)
        car_im[:, ln] = _bcast8(ci)

        def fix(i, acc, k=k, ln=ln, nr8=nr8, ni8=ni8):
            qr = _bcast8(p_re[pl.ds(SEG_LEN - 1 - i, 1), ln])
            qi = -_bcast8(p_im[pl.ds(SEG_LEN - 1 - i, 1), ln])
            gr = _seg_get(g_re, k, i) + qr * nr8 - qi * ni8
            gi = _seg_get(g_im, k, i) + qr * ni8 + qi * nr8
            _seg_put(g_re, k, i, gr)
            _seg_put(g_im, k, i, gi)
            return gr, gi

        def prod(xr, xi, gr, gi):
            return xr * gr + xi * gi, xr * gi - xi * gr

        gr, gi = fix(0, None)
        a_r, a_i = prod(sp_re[:, ln], sp_im[:, ln], gr, gi)

        def fix_acc(i, acc, k=k, fix=fix):
            gr, gi = fix(i, None)
            dr, di = prod(_seg_get(s_re, k, i - 1), _seg_get(s_im, k, i - 1), gr, gi)
            return acc[0] + dr, acc[1] + di

        acc = (a_r, a_i)
        for i in range(1, 4):
            acc = fix_acc(i, acc)
        a_r, a_i = lax.fori_loop(4, SEG_LEN, fix_acc, acc, unroll=4)
        acc_re[:, ln] += a_r
        acc_im[:, ln] += a_i


def _gelu_and_grad(y):
    cdf = 0.5 * (1.0 + lax.erf(y * (2.0 ** -0.5)))
    pdf = jnp.exp(-0.5 * y * y) * (1.0 / math.sqrt(2.0 * math.pi))
    return y * cdf, cdf + y * pdf


def _ssm_fwd_body(u_ref, bbd_ref, cbd_ref, pre_ref, pim_ref, d_ref, perm_ref, unperm_ref,
                  y_ref, gin_ref, gint_ref, cre_out, cim_out, s_re, s_im, car_re, car_im, yp):
    c = pl.program_id(0)

    @pl.when(c == 0)
    def _():
        car_re[...] = jnp.zeros_like(car_re)
        car_im[...] = jnp.zeros_like(car_im)

    cre_out[...] = car_re[...]
    cim_out[...] = car_im[...]
    u = u_ref[...].astype(F32)
    up = jnp.dot(perm_ref[...], u_ref[...].astype(BF16), preferred_element_type=F32).astype(BF16)
    for k in range(N_SLABS):
        bu = jnp.dot(up[:, k * SLAB_CH:(k + 1) * SLAB_CH], bbd_ref[k], preferred_element_type=F32)
        _slab_put(s_re, k, bu[:, :SLAB_LANES])
        _slab_put(s_im, k, bu[:, SLAB_LANES:])
    _scan_forward(s_re, s_im, pre_ref, pim_ref, car_re, car_im)
    for k in range(N_SLABS):
        yp[:, pl.ds(k * SLAB_CH, SLAB_CH)] = (
            jnp.dot(_slab_get(s_re, k).astype(BF16), cbd_ref[k, :SLAB_LANES, :], preferred_element_type=F32)
            + jnp.dot(_slab_get(s_im, k).astype(BF16), cbd_ref[k, SLAB_LANES:, :], preferred_element_type=F32))
    y = _permute_exact(unperm_ref[...], yp[...], 1) + d_ref[...] * u
    y_ref[...] = y.astype(y_ref.dtype)
    _store_with_transpose(_gelu_and_grad(y)[0], gin_ref, gint_ref)


def _ssm_forward(proj, p_re, p_im, b_bd, c_bd, d_skip):
    t = proj.shape[0]
    nc = t // SSM_CHUNK
    perm = _segment_permutation()
    return pl.pallas_call(
        functools.partial(_ssm_fwd_body), name="ssm_fwd", grid=(nc,),
        in_specs=[pl.BlockSpec((SSM_CHUNK, D_SSM), lambda c: (c, C_U // D_SSM)),
                  _whole(b_bd.shape), _whole(c_bd.shape), _whole(p_re.shape), _whole(p_im.shape),
                  _whole((1, D_SSM)), _whole(perm.shape), _whole(perm.shape)],
        out_specs=[pl.BlockSpec((SSM_CHUNK, D_SSM), lambda c: (c, 0)),
                   pl.BlockSpec((SSM_CHUNK, D_SSM), lambda c: (c, 0)),
                   pl.BlockSpec((D_SSM, SSM_CHUNK), lambda c: (0, c)),
                   pl.BlockSpec((None, 8, N_LANES), lambda c: (c, 0, 0)),
                   pl.BlockSpec((None, 8, N_LANES), lambda c: (c, 0, 0))],
        out_shape=[jax.ShapeDtypeStruct((t, D_SSM), BF16), jax.ShapeDtypeStruct((t, D_SSM), BF16),
                   jax.ShapeDtypeStruct((D_SSM, t), BF16),
                   jax.ShapeDtypeStruct((nc, 8, N_LANES), F32), jax.ShapeDtypeStruct((nc, 8, N_LANES), F32)],
        scratch_shapes=[_scan_buffer(), _scan_buffer(),
                        pltpu.VMEM((8, N_LANES), F32), pltpu.VMEM((8, N_LANES), F32),
                        pltpu.VMEM((SSM_CHUNK, D_SSM), F32)],
        compiler_params=_params(("arbitrary",)),
    )(proj, b_bd, c_bd, p_re, p_im, d_skip, jnp.asarray(perm, BF16), jnp.asarray(perm.T, BF16))


def _ssm_bwd_body(u_ref, dgin_ref, y_ref, cre_in, cim_in, bbd_ref, cbd_ref, pre_ref, pim_ref, d_ref, perm_ref,
                  unperm_ref, dproj_in,
                  du_ref, gb_ref, gc_ref, glr_ref, gli_ref, gd_ref,
                  s_re, s_im, g_re, g_im, sp_re, sp_im, car_re, car_im, gcar_re, gcar_im, acc_re, acc_im, dup):
    del dproj_in
    c = pl.program_id(0)
    nc = pl.num_programs(0)

    @pl.when(c == 0)
    def _():
        gcar_re[...] = jnp.zeros_like(gcar_re)
        gcar_im[...] = jnp.zeros_like(gcar_im)
        acc_re[...] = jnp.zeros_like(acc_re)
        acc_im[...] = jnp.zeros_like(acc_im)
        gb_ref[...] = jnp.zeros_like(gb_ref)
        gc_ref[...] = jnp.zeros_like(gc_ref)
        gd_ref[...] = jnp.zeros_like(gd_ref)

    car_re[...] = cre_in[...]
    car_im[...] = cim_in[...]
    u = u_ref[...].astype(F32)
    dy = dgin_ref[...].astype(F32) * _gelu_and_grad(y_ref[...].astype(F32))[1]
    gd_ref[...] += jnp.sum(dy * u, axis=0, keepdims=True)
    up = jnp.dot(perm_ref[...], u.astype(BF16), preferred_element_type=F32).astype(BF16)
    dyp = jnp.dot(perm_ref[...], dy.astype(BF16), preferred_element_type=F32).astype(BF16)
    for k in range(N_SLABS):
        ch = slice(k * SLAB_CH, (k + 1) * SLAB_CH)
        bu = jnp.dot(up[:, ch], bbd_ref[k], preferred_element_type=F32)
        _slab_put(s_re, k, bu[:, :SLAB_LANES])
        _slab_put(s_im, k, bu[:, SLAB_LANES:])
        ds = lax.dot_general(dyp[:, ch], cbd_ref[k], _DIMS["nt"], preferred_element_type=F32)
        _slab_put(g_re, k, ds[:, :SLAB_LANES])
        _slab_put(g_im, k, ds[:, SLAB_LANES:])
    _scan_forward(s_re, s_im, pre_ref, pim_ref, car_re, car_im, sp_re, sp_im)
    _scan_backward(g_re, g_im, s_re, s_im, sp_re, sp_im, pre_ref, pim_ref, gcar_re, gcar_im, acc_re, acc_im)
    for k in range(N_SLABS):
        ch = slice(k * SLAB_CH, (k + 1) * SLAB_CH)
        uk, dyk = up[:, ch], dyp[:, ch]
        sr, si = _slab_get(s_re, k).astype(BF16), _slab_get(s_im, k).astype(BF16)
        gr, gi = _slab_get(g_re, k).astype(BF16), _slab_get(g_im, k).astype(BF16)
        gc_ref[k, :SLAB_LANES, :] += lax.dot_general(sr, dyk, _DIMS["tn"], preferred_element_type=F32)
        gc_ref[k, SLAB_LANES:, :] += lax.dot_general(si, dyk, _DIMS["tn"], preferred_element_type=F32)
        gb_ref[k, :, :SLAB_LANES] += lax.dot_general(uk, gr, _DIMS["tn"], preferred_element_type=F32)
        gb_ref[k, :, SLAB_LANES:] += lax.dot_general(uk, gi, _DIMS["tn"], preferred_element_type=F32)
        dup[:, pl.ds(k * SLAB_CH, SLAB_CH)] = (
            lax.dot_general(gr, bbd_ref[k, :, :SLAB_LANES], _DIMS["nt"], preferred_element_type=F32)
            + lax.dot_general(gi, bbd_ref[k, :, SLAB_LANES:], _DIMS["nt"], preferred_element_type=F32))
    du = _permute_exact(unperm_ref[...], dup[...], 1) + d_ref[...] * dy
    du_ref[...] = du.astype(du_ref.dtype)

    @pl.when(c == nc - 1)
    def _():
        glr_ref[...] = jnp.sum(acc_re[...], axis=0, keepdims=True)
        gli_ref[...] = jnp.sum(acc_im[...], axis=0, keepdims=True)


def _ssm_backward(proj, dg_in, y_ssm, car_re, car_im, p_re, p_im, b_bd, c_bd, d_skip, dproj):
    t = proj.shape[0]
    nc = t // SSM_CHUNK
    rev = lambda c: nc - 1 - c
    big = _scan_buffer
    small = lambda: pltpu.VMEM((8, N_LANES), F32)
    perm = _segment_permutation()
    outs = pl.pallas_call(
        functools.partial(_ssm_bwd_body), name="ssm_bwd", grid=(nc,),
        in_specs=[pl.BlockSpec((SSM_CHUNK, D_SSM), lambda c: (rev(c), C_U // D_SSM)),
                  pl.BlockSpec((SSM_CHUNK, D_SSM), lambda c: (rev(c), 0)),
                  pl.BlockSpec((SSM_CHUNK, D_SSM), lambda c: (rev(c), 0)),
                  pl.BlockSpec((None, 8, N_LANES), lambda c: (rev(c), 0, 0)),
                  pl.BlockSpec((None, 8, N_LANES), lambda c: (rev(c), 0, 0)),
                  _whole(b_bd.shape), _whole(c_bd.shape), _whole(p_re.shape), _whole(p_im.shape),
                  _whole((1, D_SSM)), _whole(perm.shape), _whole(perm.shape), pl.BlockSpec(memory_space=pl.ANY)],
        out_specs=[pl.BlockSpec((SSM_CHUNK, D_SSM), lambda c: (rev(c), C_U // D_SSM)),
                   _whole(b_bd.shape), _whole(c_bd.shape), _whole((1, N_LANES)), _whole((1, N_LANES)),
                   _whole((1, D_SSM))],
        out_shape=[jax.ShapeDtypeStruct(dproj.shape, dproj.dtype),
                   jax.ShapeDtypeStruct(b_bd.shape, F32), jax.ShapeDtypeStruct(c_bd.shape, F32),
                   jax.ShapeDtypeStruct((1, N_LANES), F32), jax.ShapeDtypeStruct((1, N_LANES), F32),
                   jax.ShapeDtypeStruct((1, D_SSM), F32)],
        scratch_shapes=[big(), big(), big(), big()] + [small() for _ in range(8)]
        + [pltpu.VMEM((SSM_CHUNK, D_SSM), F32)],
        input_output_aliases={12: 0},
        compiler_params=_params(("arbitrary",)),
    )(proj, dg_in, y_ssm, car_re, car_im, b_bd, c_bd, p_re, p_im, d_skip, jnp.asarray(perm, BF16),
      jnp.asarray(perm.T, BF16), dproj)
    return outs


def _bucket_table():
    i = np.arange(BLOCK)[:, None]
    j = np.arange(2 * BLOCK)[None, :]
    dist = BLOCK + i - j
    ok = (dist >= 0) & (dist < WINDOW)
    max_exact = N_BUCKETS // 2
    d = np.maximum(dist, 1).astype(np.float32)
    large = max_exact + (np.log(d / max_exact) / math.log(MAX_DISTANCE / max_exact)
                         * (N_BUCKETS - max_exact)).astype(np.int32)
    large = np.minimum(large, N_BUCKETS - 1)
    bucket = np.where(dist < max_exact, dist, large)
    return np.where(ok, bucket, -1).astype(np.int32)


def _bias_build_body(table_ref, bucket_ref, o_ref):
    h = pl.program_id(0)
    bucket = bucket_ref[...]
    acc = jnp.full(bucket.shape, NEG_INF, F32)
    for b in range(N_BUCKETS):
        acc = jnp.where(bucket == b, table_ref[b, h], acc)
    o_ref[...] = acc


def _bias_build(rel_bias_table, bucket):
    return pl.pallas_call(
        functools.partial(_bias_build_body), name="bias_build", grid=(N_Q_HEADS,),
        in_specs=[pl.BlockSpec(memory_space=pltpu.SMEM), _whole(bucket.shape)],
        out_specs=pl.BlockSpec((None, BLOCK, 2 * BLOCK), lambda h: (h, 0, 0)),
        out_shape=jax.ShapeDtypeStruct((N_Q_HEADS, BLOCK, 2 * BLOCK), F32),
        compiler_params=_params(("arbitrary",)),
    )(rel_bias_table, bucket)


def _bias_grad_body(g_ref, bucket_ref, o_ref):
    bucket = bucket_ref[...]
    lane = lax.broadcasted_iota(jnp.int32, (N_BUCKETS, 128), 1)

    def head(h, out):
        g = g_ref[h]
        rows = [jnp.sum(jnp.where(bucket == b, g, 0.0), axis=0, keepdims=True) for b in range(N_BUCKETS)]
        colsum = jnp.sum(jnp.concatenate(rows, axis=0), axis=1, keepdims=True)
        return jnp.where(lane == h, colsum, out)

    o_ref[...] = lax.fori_loop(0, N_Q_HEADS, head, jnp.zeros((N_BUCKETS, 128), F32))


def _bias_grad(g_bias, bucket):
    out = pl.pallas_call(
        functools.partial(_bias_grad_body), name="bias_grad", grid=(1,),
        in_specs=[_whole(g_bias.shape), _whole(bucket.shape)],
        out_specs=_whole((N_BUCKETS, 128)),
        out_shape=jax.ShapeDtypeStruct((N_BUCKETS, 128), F32),
        compiler_params=_params(("arbitrary",)),
    )(g_bias, bucket)
    return out[:, :N_Q_HEADS]


def _head_logits(qh, kk, bias_h, first_block):
    s = lax.dot_general(qh, kk, _DIMS["nt"], preferred_element_type=F32) * ATTN_SCALE + bias_h
    col = lax.broadcasted_iota(jnp.int32, s.shape, 1)
    return jnp.where(jnp.logical_and(first_block, col < BLOCK), NEG_INF, s)


def _attn_fwd_body(sink_ref, q_ref, kp_ref, kc_ref, vp_ref, vc_ref, bias_ref, o_ref, lse_ref):
    n = pl.program_id(0)
    outs, lses = [], []
    for kv in range(N_KV_HEADS):
        cs = slice(kv * HEAD_DIM, (kv + 1) * HEAD_DIM)
        kk = jnp.concatenate([kp_ref[:, cs], kc_ref[:, cs]], axis=0).astype(BF16)
        vv = jnp.concatenate([vp_ref[:, cs], vc_ref[:, cs]], axis=0).astype(BF16)
        for g in range(Q_PER_KV):
            h = kv * Q_PER_KV + g
            qh = q_ref[:, h * HEAD_DIM:(h + 1) * HEAD_DIM].astype(BF16)
            s = _head_logits(qh, kk, bias_ref[h], n == 0)
            sink = sink_ref[0, h]
            m = jnp.maximum(jnp.max(s, axis=1, keepdims=True), sink)
            p = jnp.exp(s - m)
            den = jnp.sum(p, axis=1, keepdims=True) + jnp.exp(sink - m)
            p = p / den
            outs.append(jnp.dot(p.astype(BF16), vv, preferred_element_type=F32))
            lses.append(m + jnp.log(den))
    o_ref[...] = jnp.concatenate(outs, axis=1).astype(o_ref.dtype)
    lse_ref[...] = jnp.concatenate(lses, axis=1)


def _attn_specs(nb):
    prev = lambda n: jnp.maximum(jnp.minimum(n, nb - 1) - 1, 0)
    cur = lambda n: jnp.minimum(n, nb - 1)
    return [pl.BlockSpec((BLOCK, D_ATTN), lambda n: (cur(n), C_Q // D_ATTN)),
            pl.BlockSpec((BLOCK, D_KV), lambda n: (prev(n), C_K // D_KV)),
            pl.BlockSpec((BLOCK, D_KV), lambda n: (cur(n), C_K // D_KV)),
            pl.BlockSpec((BLOCK, D_KV), lambda n: (prev(n), C_V // D_KV)),
            pl.BlockSpec((BLOCK, D_KV), lambda n: (cur(n), C_V // D_KV))]


def _attn_forward(proj, sinks, bias):
    t = proj.shape[0]
    nb = t // BLOCK
    return pl.pallas_call(
        functools.partial(_attn_fwd_body), name="attn_fwd", grid=(nb,),
        in_specs=[pl.BlockSpec(memory_space=pltpu.SMEM)] + _attn_specs(nb) + [_whole(bias.shape)],
        out_specs=[pl.BlockSpec((BLOCK, D_ATTN), lambda n: (n, 0)),
                   pl.BlockSpec((BLOCK, N_Q_HEADS), lambda n: (n, 0))],
        out_shape=[jax.ShapeDtypeStruct((t, D_ATTN), BF16), jax.ShapeDtypeStruct((t, N_Q_HEADS), F32)],
        compiler_params=_params(("parallel",)),
    )(sinks, proj, proj, proj, proj, proj, bias)


def _attn_bwd_body(sink_ref, q_ref, kp_ref, kc_ref, vp_ref, vc_ref, bias_ref, do_ref, o_ref, lse_ref, dproj_in,
                   dq_ref, dkv_ref, gbias_ref, gsink_ref, carry_ref, *, nb):
    del dproj_in
    n = pl.program_id(0)

    @pl.when(n == 0)
    def _():
        gbias_ref[...] = jnp.zeros_like(gbias_ref)
        gsink_ref[...] = jnp.zeros_like(gsink_ref)
        carry_ref[...] = jnp.zeros_like(carry_ref)

    @pl.when(n < nb)
    def _():
        lane = lax.broadcasted_iota(jnp.int32, (1, 128), 1)
        dqs, dks, dvs = [], [], []
        gsink = jnp.zeros((1, 128), F32)
        for kv in range(N_KV_HEADS):
            cs = slice(kv * HEAD_DIM, (kv + 1) * HEAD_DIM)
            kk = jnp.concatenate([kp_ref[:, cs], kc_ref[:, cs]], axis=0).astype(BF16)
            vv = jnp.concatenate([vp_ref[:, cs], vc_ref[:, cs]], axis=0).astype(BF16)
            dk = jnp.zeros((2 * BLOCK, HEAD_DIM), F32)
            dv = jnp.zeros((2 * BLOCK, HEAD_DIM), F32)
            for g in range(Q_PER_KV):
                h = kv * Q_PER_KV + g
                hs = slice(h * HEAD_DIM, (h + 1) * HEAD_DIM)
                qh = q_ref[:, hs].astype(BF16)
                s = _head_logits(qh, kk, bias_ref[h], n == 0)
                lse = lse_ref[:, h:h + 1]
                p = jnp.exp(s - lse)
                do = do_ref[:, hs].astype(F32)
                delta = jnp.sum(do * o_ref[:, hs].astype(F32), axis=1, keepdims=True)
                dob = do.astype(BF16)
                dp = lax.dot_general(dob, vv, _DIMS["nt"], preferred_element_type=F32)
                dl = p * (dp - delta)
                gbias_ref[h] += dl
                psink = jnp.exp(sink_ref[0, h] - lse)
                gsink = gsink + jnp.where(lane == h, -jnp.sum(psink * delta), 0.0)
                dlb = dl.astype(BF16)
                dqs.append(jnp.dot(dlb, kk, preferred_element_type=F32) * ATTN_SCALE)
                dk = dk + lax.dot_general(dlb, qh, _DIMS["tn"], preferred_element_type=F32) * ATTN_SCALE
                dv = dv + lax.dot_general(p.astype(BF16), dob, _DIMS["tn"], preferred_element_type=F32)
            dks.append(dk)
            dvs.append(dv)
        dq_ref[...] = jnp.concatenate(dqs, axis=1).astype(dq_ref.dtype)
        gsink_ref[...] += gsink
        dkv = jnp.concatenate(dks + dvs, axis=1)
        dkv_ref[...] = (carry_ref[...] + dkv[:BLOCK]).astype(dkv_ref.dtype)
        carry_ref[...] = dkv[BLOCK:]

    @pl.when(n == nb)
    def _():
        dkv_ref[...] = carry_ref[...].astype(dkv_ref.dtype)


def _attn_backward(proj, sinks, bias, d_attn, attn, lse, dproj):
    t = proj.shape[0]
    nb = t // BLOCK
    cur = lambda n: jnp.minimum(n, nb - 1)
    return pl.pallas_call(
        functools.partial(_attn_bwd_body, nb=nb), name="attn_bwd", grid=(nb + 1,),
        in_specs=[pl.BlockSpec(memory_space=pltpu.SMEM)] + _attn_specs(nb) + [
            _whole(bias.shape),
            pl.BlockSpec((BLOCK, D_ATTN), lambda n: (cur(n), 0)),
            pl.BlockSpec((BLOCK, D_ATTN), lambda n: (cur(n), 0)),
            pl.BlockSpec((BLOCK, N_Q_HEADS), lambda n: (cur(n), 0)),
            pl.BlockSpec(memory_space=pl.ANY)],
        out_specs=[pl.BlockSpec((BLOCK, D_ATTN), lambda n: (cur(n), C_Q // D_ATTN)),
                   pl.BlockSpec((BLOCK, 2 * D_KV), lambda n: (jnp.maximum(n - 1, 0), 0)),
                   _whole(bias.shape), _whole((1, 128))],
        out_shape=[jax.ShapeDtypeStruct(dproj.shape, dproj.dtype), jax.ShapeDtypeStruct((t, 2 * D_KV), dproj.dtype),
                   jax.ShapeDtypeStruct(bias.shape, F32), jax.ShapeDtypeStruct((1, 128), F32)],
        scratch_shapes=[pltpu.VMEM((BLOCK, 2 * D_KV), F32)],
        input_output_aliases={10: 0},
        compiler_params=_params(("arbitrary",)),
    )(sinks, proj, proj, proj, proj, proj, bias, d_attn, attn, lse, dproj)


ROWS = 512


def _rowwise(body, name, t, ins, outs, aliases=None):
    rows = min(ROWS, t)

    def col_spec(w, c0):
        if c0 == "T":
            return pl.BlockSpec((w, rows), lambda i: (0, i))
        if c0 % w == 0:
            return pl.BlockSpec((rows, w), lambda i: (i, c0 // w))
        return pl.BlockSpec((pl.Element(rows), pl.Element(w)), lambda i: (i * rows, c0))

    in_specs, args = [], []
    for a, w, c0 in ins:
        args.append(a)
        if w is None:
            in_specs.append(pl.BlockSpec(memory_space=pl.ANY) if c0 == "any" else _whole(a.shape))
        else:
            in_specs.append(col_spec(w, c0))
    out_specs, out_shape = [], []
    for shape, dtype, w, c0 in outs:
        out_shape.append(jax.ShapeDtypeStruct(shape, dtype))
        out_specs.append(_whole(shape) if w is None else col_spec(w, c0))
    accum = any(o[2] is None for o in outs)
    return pl.pallas_call(
        body, name=name, grid=(t // rows,), in_specs=in_specs, out_specs=out_specs, out_shape=out_shape,
        input_output_aliases=aliases or {},
        compiler_params=_params(("arbitrary",) if accum else ("parallel",)),
    )(*args)


def _f32(ref, *idx):
    return (ref[idx] if idx else ref[...]).astype(F32)


def _store_with_transpose(val, ref, t_ref):
    val = val.astype(ref.dtype)
    ref[...] = val
    t_ref[...] = val.T


def _ssm_gate_fwd_body(glu_ref, z_ref, h_ref, ht_ref):
    a, b = _f32(glu_ref, slice(None), slice(0, D_SSM)), _f32(glu_ref, slice(None), slice(D_SSM, None))
    _store_with_transpose((a * _sigmoid(b)) * _silu_and_grad(_f32(z_ref))[0], h_ref, ht_ref)


def _ssm_gate_bwd_body(dh_ref, glu_ref, z_ref, dproj_in, dglu_ref, dz_ref):
    del dproj_in
    a, b = _f32(glu_ref, slice(None), slice(0, D_SSM)), _f32(glu_ref, slice(None), slice(D_SSM, None))
    sb = _sigmoid(b)
    silu, dsilu = _silu_and_grad(_f32(z_ref))
    dh = _f32(dh_ref)
    dg = dh * silu
    dz_ref[...] = (dh * (a * sb) * dsilu).astype(dz_ref.dtype)
    dglu_ref[:, :D_SSM] = (dg * sb).astype(dglu_ref.dtype)
    dglu_ref[:, D_SSM:] = (dg * a * sb * (1.0 - sb)).astype(dglu_ref.dtype)


def _attn_gate_fwd_body(attn_ref, z_ref, h_ref, ht_ref):
    _store_with_transpose(_f32(attn_ref) * _silu_and_grad(_f32(z_ref))[0], h_ref, ht_ref)


def _attn_gate_bwd_body(dh_ref, attn_ref, z_ref, dproj_in, dattn_ref, dz_ref):
    del dproj_in
    silu, dsilu = _silu_and_grad(_f32(z_ref))
    dh = _f32(dh_ref)
    dattn_ref[...] = (dh * silu).astype(dattn_ref.dtype)
    dz_ref[...] = (dh * _f32(attn_ref) * dsilu).astype(dz_ref.dtype)


def _merge_fwd_body(bs_ref, ba_ref, gl_ref, m_ref, mt_ref):
    gs = _sigmoid(_f32(gl_ref, slice(None), slice(0, D_MODEL)))
    ga = _sigmoid(_f32(gl_ref, slice(None), slice(D_MODEL, None)))
    _store_with_transpose(gs * _f32(bs_ref) + ga * _f32(ba_ref), m_ref, mt_ref)


def _merge_bwd_body(dm_ref, bs_ref, ba_ref, gl_ref, dbs_ref, dba_ref, dgl_ref):
    gs = _sigmoid(_f32(gl_ref, slice(None), slice(0, D_MODEL)))
    ga = _sigmoid(_f32(gl_ref, slice(None), slice(D_MODEL, None)))
    dm = _f32(dm_ref)
    dbs_ref[...] = (dm * gs).astype(dbs_ref.dtype)
    dba_ref[...] = (dm * ga).astype(dba_ref.dtype)
    dgl_ref[:, :D_MODEL] = (dm * _f32(bs_ref) * gs * (1.0 - gs)).astype(dgl_ref.dtype)
    dgl_ref[:, D_MODEL:] = (dm * _f32(ba_ref) * ga * (1.0 - ga)).astype(dgl_ref.dtype)


def _ln_loss_body(x_ref, o_ref, tgt_ref, gain_ref, bias_ref, dr_ref, loss_ref, dgain_ref, dbias_ref):
    @pl.when(pl.program_id(0) == 0)
    def _():
        loss_ref[...] = jnp.zeros_like(loss_ref)
        dgain_ref[...] = jnp.zeros_like(dgain_ref)
        dbias_ref[...] = jnp.zeros_like(dbias_ref)

    r = ALPHA * x_ref[...] + o_ref[...].astype(F32)
    mu = jnp.mean(r, axis=1, keepdims=True)
    rc = r - mu
    var = jnp.mean(rc * rc, axis=1, keepdims=True)
    rstd = lax.rsqrt(var + LN_EPS)
    xhat = rc * rstd
    gain = gain_ref[...]
    err = xhat * gain + bias_ref[...] - tgt_ref[...]
    loss_ref[...] += 0.5 * jnp.sum(jnp.mean(err * err, axis=1, keepdims=True), axis=0, keepdims=True)
    dy = err * (1.0 / D_MODEL)
    dgain_ref[...] += jnp.sum(dy * xhat, axis=0, keepdims=True)
    dbias_ref[...] += jnp.sum(dy, axis=0, keepdims=True)
    dxhat = dy * gain
    m1 = jnp.mean(dxhat, axis=1, keepdims=True)
    m2 = jnp.mean(dxhat * xhat, axis=1, keepdims=True)
    dr_ref[...] = (rstd * (dxhat - m1 - xhat * m2)).astype(dr_ref.dtype)


def _place_body(piece_ref, dproj_in, o_ref):
    del dproj_in
    o_ref[...] = piece_ref[...]


def _adamw_update(w_ref, m_ref, v_ref, g, g_ref, d_ref, nm_ref, nv_ref):
    m = ADAM_B1 * m_ref[...] + (1.0 - ADAM_B1) * g
    v = ADAM_B2 * v_ref[...] + (1.0 - ADAM_B2) * (g * g)
    m_hat = m / (1.0 - ADAM_B1 ** ADAM_STEP)
    v_hat = v / (1.0 - ADAM_B2 ** ADAM_STEP)
    g_ref[...] = g
    d_ref[...] = -ADAM_LR * (m_hat / (jnp.sqrt(v_hat) + ADAM_EPS) + ADAM_WD * w_ref[...])
    nm_ref[...] = m
    nv_ref[...] = v


def _adamw_body(*refs, n_parts):
    w_ref, m_ref, v_ref = refs[:3]
    parts = refs[3:3 + n_parts]
    g = parts[0][...].astype(F32)
    for p in parts[1:]:
        g = g + p[...].astype(F32)
    _adamw_update(w_ref, m_ref, v_ref, g, *refs[3 + n_parts:])


def _adamw_shard_body(c_ref, w_ref, m_ref, v_ref, mine_ref, other_ref, g_ref, d_ref, nm_ref, nv_ref, *, nth):
    in_mine = pl.program_id(0) // nth == c_ref[0]
    g = jnp.where(in_mine, mine_ref[...], other_ref[...])
    _adamw_update(w_ref, m_ref, v_ref, g, g_ref, d_ref, nm_ref, nv_ref)


def _adamw_shard(w, m, v, halves, core, *, name, rows):
    shape = w.shape
    w2, m2, v2 = (a.reshape(-1, shape[-1]) for a in (w, m, v))
    r, c = w2.shape
    nth = r // 2 // rows
    assert 2 * nth * rows == r and halves[0].shape == (r // 2, c)
    spec = pl.BlockSpec((rows, c), lambda i, cr: (i, 0))
    mine_spec = pl.BlockSpec((rows, c), lambda i, cr: (jnp.where(i // nth == cr[0], i % nth, 0), 0))
    other_spec = pl.BlockSpec((rows, c), lambda i, cr: (jnp.where(i // nth == cr[0], 0, i % nth), 0))
    outs = pl.pallas_call(
        functools.partial(_adamw_shard_body, nth=nth), name=name,
        grid_spec=pltpu.PrefetchScalarGridSpec(
            num_scalar_prefetch=1, grid=(r // rows,),
            in_specs=[spec] * 3 + [mine_spec, other_spec], out_specs=[spec] * 4),
        out_shape=[jax.ShapeDtypeStruct((r, c), F32)] * 4,
        compiler_params=_params(("parallel",)),
    )(core, w2, m2, v2, *halves)
    return tuple(o.reshape(shape) for o in outs)


def _adamw_packed(wmv, g):
    r, c = g.shape
    spec = pl.BlockSpec((r, c), lambda i: (0, 0))
    return pl.pallas_call(
        functools.partial(_adamw_body, n_parts=1), name="adamw_small", grid=(1,),
        in_specs=[pl.BlockSpec((r, c), lambda i, k=k: (k, 0)) for k in range(3)] + [spec], out_specs=[spec] * 4,
        out_shape=[jax.ShapeDtypeStruct((r, c), F32)] * 4,
        compiler_params=_params(("arbitrary",)),
    )(wmv, wmv, wmv, g)


BIG = ("w_in", "w_glu", "w_branch_ssm", "w_branch_attn", "w_out")
SHARD_AXIS = dict(w_in=1, w_glu=1, w_branch_ssm=1, w_branch_attn=1, w_out=0)
HBM = pl.BlockSpec(memory_space=pl.ANY)


def _position():
    x, y, c = lax.axis_index("x"), lax.axis_index("y"), lax.axis_index("c")
    other_chips = [(1 - x, y), (x, 1 - y), (1 - x, 1 - y)]
    return x, y, c, other_chips


def _window(ref, axis, shard, n_shards, half=None):
    rows, cols = ref.shape[-2:]
    sel = [slice(None), slice(None)]
    size = ref.shape[-2 + axis] // n_shards
    sel[axis] = pl.ds(pl.multiple_of(shard * size, 128), size)
    if half is not None:
        hsize = ref.shape[-1 - axis] // 2
        sel[1 - axis] = pl.ds(pl.multiple_of(half * hsize, 128), hsize)
    return ref.at[tuple(sel)]


def _half(ref, axis, half):
    hsize = ref.shape[-1 - axis] // 2
    sel = [slice(None), slice(None)]
    sel[1 - axis] = pl.ds(pl.multiple_of(half * hsize, 128), hsize)
    return ref.at[tuple(sel)]


def _remote(src, dst, send_sem, recv_sem, device):
    return pltpu.make_async_remote_copy(src_ref=src, dst_ref=dst, send_sem=send_sem, recv_sem=recv_sem,
                                        device_id=device, device_id_type=MESH)


def _full_shapes(shards, names):
    out = []
    for k in names:
        s = list(shards[k].shape)
        s[SHARD_AXIS[k]] *= N_CHIPS
        out.append(jax.ShapeDtypeStruct(tuple(s), shards[k].dtype))
    return out


def _gather_exchange(shards, names):
    axes = tuple(SHARD_AXIS[k] for k in names)
    n = len(names)

    def copies(ins, outs, sems):
        send_sems, recv_sems, local_sems = sems
        x, y, c, chips = _position()
        me = 2 * x + y
        remote, local = [], []
        for w, ax in enumerate(axes):
            for r, (px, py) in enumerate(chips):
                remote.append(_remote(_half(ins[w], ax, c), _window(outs[w], ax, me, N_CHIPS, c),
                                      send_sems.at[3 * w + r], recv_sems.at[3 * w + r], (px, py, c)))
            local.append(pltpu.make_async_copy(ins[w], _window(outs[w], ax, me, N_CHIPS), local_sems.at[w]))
        return remote, local

    return _Exchange([shards[k] for k in names], _full_shapes(shards, names), (3 * n, 3 * n, n), copies)


SEM = pl.BlockSpec(memory_space=pltpu.SEMAPHORE)


def _gather_start_body(shard_ref, land_ref, send_sems, recv_sems, shard_thru, land_thru, token, *, axis):
    del shard_thru, land_thru
    x, y, c, chips = _position()
    me = 2 * x + y
    for r, (px, py) in enumerate(chips):
        _remote(_half(shard_ref, axis, c), _window(land_ref, axis, me, N_CHIPS, c),
                send_sems.at[r], recv_sems.at[r], (px, py, c)).start()
    token[...] = jnp.zeros_like(token)


def _gather_start(shard, name):
    axis = SHARD_AXIS[name]
    full = _full_shapes({name: shard}, (name,))[0]
    return pl.pallas_call(
        functools.partial(_gather_start_body, axis=axis), name="gather_start_" + name,
        out_shape=(pltpu.SemaphoreType.DMA((3,)), pltpu.SemaphoreType.DMA((3,)), pltpu.HBM(shard.shape, shard.dtype),
                   pltpu.HBM(full.shape, full.dtype), jax.ShapeDtypeStruct((8, 128), F32)),
        in_specs=(HBM, HBM), out_specs=(SEM, SEM, HBM, HBM, pl.BlockSpec(memory_space=pltpu.VMEM)),
        input_output_aliases={0: 2, 1: 3},
        compiler_params=pltpu.CompilerParams(has_side_effects=pltpu.SideEffectType.DATAFLOW_SIDE_EFFECTING),
    )(pltpu.with_memory_space_constraint(shard, pltpu.HBM),
      pltpu.with_memory_space_constraint(lax.empty(full.shape, full.dtype), pltpu.HBM))


def _gather_wait_body(shard_ref, land_ref, send_sems, recv_sems, after_ref, shard_dead, landed, *, axis):
    del after_ref, shard_dead, landed
    x, y, c, chips = _position()
    for r, (px, py) in enumerate(chips):
        cp = _remote(_half(shard_ref, axis, c), _window(land_ref, axis, 2 * px + py, N_CHIPS, c),
                     send_sems.at[r], recv_sems.at[r], (px, py, c))
        cp.wait_send()
        cp.wait_recv()


def _gather_wait(send_sems, recv_sems, shard_thru, land_thru, after, name):
    return pl.pallas_call(
        functools.partial(_gather_wait_body, axis=SHARD_AXIS[name]), name="gather_wait_" + name,
        out_shape=(pltpu.HBM(shard_thru.shape, shard_thru.dtype), pltpu.HBM(land_thru.shape, land_thru.dtype)),
        in_specs=(HBM, HBM, SEM, SEM, HBM), out_specs=(HBM, HBM), input_output_aliases={0: 0, 1: 1},
        compiler_params=pltpu.CompilerParams(has_side_effects=pltpu.SideEffectType.DATAFLOW_SIDE_EFFECTING),
    )(shard_thru, land_thru, send_sems, recv_sems, after)[1]


def _pass_on_body(*refs, axes, n_own):
    n = len(axes)
    ins, own = refs[:n], refs[n:n + n_own]
    refs = refs[:n] + refs[n + n_own:]
    outs = refs[n:2 * n]
    sbuf, rbuf = refs[2 * n:5 * n], refs[5 * n:8 * n]
    obuf = refs[8 * n:8 * n + n_own]
    send_sems, recv_sems, load_sems, store_sems, own_sems, placed_sems = refs[8 * n + n_own:]
    x, y, c, chips = _position()
    fetched = [pltpu.make_async_copy(own[w], obuf[w], own_sems.at[w]) for w in range(n_own)]
    for cp in fetched:
        cp.start()
    region = lambda ref, w, r, half: _window(ref, axes[w], 2 * chips[r][0] + chips[r][1], N_CHIPS, half)
    pairs = [(w, r) for w in range(n) for r in range(3)]
    loads = [pltpu.make_async_copy(region(ins[w], w, r, c), sbuf[3 * w + r], load_sems.at[3 * w + r]) for w, r in pairs]
    for cp in loads:
        cp.start()
    placed = []
    for w, cp in enumerate(fetched):
        cp.wait()
        placed.append(pltpu.make_async_copy(obuf[w], _window(outs[w], axes[w], 2 * x + y, N_CHIPS), placed_sems.at[w]))
        placed[-1].start()
    sends = []
    for i, cp in enumerate(loads):
        cp.wait()
        sends.append(_remote(sbuf[i], rbuf[i], send_sems.at[i], recv_sems.at[i], (x, y, 1 - c)))
        sends[-1].start()
    stores = []
    for i, (w, r) in enumerate(pairs):
        sends[i].wait_recv()
        stores.append(pltpu.make_async_copy(rbuf[i], region(outs[w], w, r, 1 - c), store_sems.at[i]))
        stores[-1].start()
    for cp in sends:
        cp.wait_send()
    for cp in stores + placed:
        cp.wait()


def _pass_on(fulls, names, own=()):
    axes = tuple(SHARD_AXIS[k] for k in names)
    n = len(names)
    assert len(own) in (0, n)
    bufs = []
    for a, ax in zip(fulls, axes):
        s = list(a.shape)
        s[ax] //= N_CHIPS
        s[1 - ax] //= 2
        bufs += [pltpu.VMEM(tuple(s), a.dtype)] * 3
    return pl.pallas_call(
        functools.partial(_pass_on_body, axes=axes, n_own=len(own)), name="pass_on_" + names[0],
        in_specs=[HBM] * (n + len(own)), out_specs=[HBM] * n,
        out_shape=[jax.ShapeDtypeStruct(a.shape, a.dtype) for a in fulls],
        scratch_shapes=bufs + bufs + [pltpu.VMEM(o.shape, o.dtype) for o in own]
        + [pltpu.SemaphoreType.DMA((3 * n,))] * 4 + [pltpu.SemaphoreType.DMA((n,))] * 2,
        input_output_aliases={i: i for i in range(n)},
        compiler_params=pltpu.CompilerParams(vmem_limit_bytes=VMEM_LIMIT),
    )(*fulls, *own)


def _chip_exchange(pairs, names):
    axes = tuple(SHARD_AXIS[k] for k in names)
    n = len(names)
    out_shape = []
    for p, ax in zip(pairs, axes):
        s = list(p.shape)
        s[ax] //= N_CHIPS
        out_shape.append(jax.ShapeDtypeStruct((4, *s), p.dtype))

    def copies(ins, outs, sems):
        send_sems, recv_sems, local_sems = sems
        x, y, c, chips = _position()
        me = 2 * x + y
        remote, local = [], []
        for w, ax in enumerate(axes):
            for r, (px, py) in enumerate(chips):
                remote.append(_remote(_window(ins[w], ax, 2 * px + py, N_CHIPS), outs[w].at[r],
                                      send_sems.at[3 * w + r], recv_sems.at[3 * w + r], (px, py, c)))
            local.append(pltpu.make_async_copy(_window(ins[w], ax, me, N_CHIPS), outs[w].at[3], local_sems.at[w]))
        return remote, local

    return _Exchange(pairs, out_shape, (3 * n, 3 * n, n), copies)


def _half_tile(n, h, nt, axis):
    return h * nt + n if axis == 1 else 2 * n + h


def _sibling_stream(n, nt, stage, recv, send_sems, recv_sems, credit, produce, consume):
    x, y, c, _ = _position()
    sibling = (x, y, 1 - c)

    def copy(slot):
        return _remote(stage.at[slot], recv.at[slot], send_sems.at[slot], recv_sems.at[slot], sibling)

    @pl.when(n < nt)
    def _():
        slot = n % 2

        @pl.when(n >= 2)
        def _():
            copy(slot).wait_send()
            pl.semaphore_wait(credit, 1)

        stage[slot] = produce().astype(stage.dtype)
        copy(slot).start()

    @pl.when(n >= 1)
    def _():
        slot = (n - 1) % 2
        copy(slot).wait_recv()
        consume(recv[slot])

        @pl.when(n - 1 < nt - 2)
        def _():
            pl.semaphore_signal(credit, 1, device_id=sibling, device_id_type=MESH)

    @pl.when(n == nt)
    def _():
        for slot in range(min(2, nt)):
            copy(slot).wait_send()


def _pair_reduce_body(c_ref, mine_ref, theirs_ref, out_ref, stage, recv, send_sems, recv_sems, credit, *, nt):
    del c_ref

    def consume(got):
        out_ref[...] = (mine_ref[...] + got.astype(F32)).astype(out_ref.dtype)

    _sibling_stream(pl.program_id(0), nt, stage, recv, send_sems, recv_sems, credit,
                    lambda: theirs_ref[...], consume)


def _pair_reduce(grad, core, axis, *, name, rows):
    r, c = grad.shape
    nt = r // 2 // rows
    assert nt * rows * 2 == r and (axis == 1 or rows == r // (2 * N_CHIPS))
    tile = lambda n, h: _half_tile(n, h, nt, axis)
    return pl.pallas_call(
        functools.partial(_pair_reduce_body, nt=nt), name=name,
        grid_spec=pltpu.PrefetchScalarGridSpec(
            num_scalar_prefetch=1, grid=(nt + 1,),
            in_specs=[pl.BlockSpec((rows, c), lambda n, cr: (tile(jnp.maximum(n - 1, 0), cr[0]), 0)),
                      pl.BlockSpec((rows, c), lambda n, cr: (tile(jnp.minimum(n, nt - 1), 1 - cr[0]), 0))],
            out_specs=pl.BlockSpec((rows, c), lambda n, cr: (jnp.maximum(n - 1, 0), 0)),
            scratch_shapes=[pltpu.VMEM((2, rows, c), BF16), pltpu.VMEM((2, rows, c), BF16),
                            pltpu.SemaphoreType.DMA((2,)), pltpu.SemaphoreType.DMA((2,)),
                            pltpu.SemaphoreType.REGULAR]),
        out_shape=jax.ShapeDtypeStruct((r // 2, c), BF16),
        compiler_params=_params(("arbitrary",)),
    )(core, grad, grad)


def _chip_add_share_body(c_ref, s0, s1, s2, s3, mine_ref, other_ref, stage, recv, send_sems, recv_sems, credit, *, nt):
    del c_ref

    def produce():
        total = s3[...].astype(F32) + s0[...].astype(F32) + s1[...].astype(F32) + s2[...].astype(F32)
        mine_ref[...] = total
        return total

    def consume(got):
        other_ref[...] = got

    _sibling_stream(pl.program_id(0), nt, stage, recv, send_sems, recv_sems, credit, produce, consume)


def _chip_add_share(slots, core, *, name, rows):
    _, r, c = slots.shape
    nt = r // rows
    assert nt * rows == r
    send = lambda j: pl.BlockSpec((None, rows, c), lambda n, cr: (j, jnp.minimum(n, nt - 1), 0))
    return pl.pallas_call(
        functools.partial(_chip_add_share_body, nt=nt), name=name,
        grid_spec=pltpu.PrefetchScalarGridSpec(
            num_scalar_prefetch=1, grid=(nt + 1,),
            in_specs=[send(j) for j in range(4)],
            out_specs=[pl.BlockSpec((rows, c), lambda n, cr: (jnp.minimum(n, nt - 1), 0)),
                       pl.BlockSpec((rows, c), lambda n, cr: (jnp.maximum(n - 1, 0), 0))],
            scratch_shapes=[pltpu.VMEM((2, rows, c), F32), pltpu.VMEM((2, rows, c), F32),
                            pltpu.SemaphoreType.DMA((2,)), pltpu.SemaphoreType.DMA((2,)),
                            pltpu.SemaphoreType.REGULAR]),
        out_shape=[jax.ShapeDtypeStruct((r, c), F32)] * 2,
        compiler_params=_params(("arbitrary",)),
    )(core, slots, slots, slots, slots)


PAIR_ROWS = dict(w_in=64, w_glu=128, w_branch_ssm=128, w_branch_attn=128, w_out=256)
SHARE_ROWS = dict(w_in=128, w_glu=128, w_branch_ssm=128, w_branch_attn=128, w_out=64)


def _pair_sums(grads, names, core):
    return [_pair_reduce(grads[k], core, SHARD_AXIS[k], name="pair_reduce_" + k, rows=PAIR_ROWS[k]) for k in names]


def _shard_halves(slots, names, core):
    return {k: _chip_add_share(s, core, name="chip_add_share_" + k, rows=SHARE_ROWS[k]) for k, s in zip(names, slots)}


SMALL = (("ssm_lambda_re", (1, 64, 64)), ("ssm_lambda_im", (1, 64, 64)), ("ssm_b_re", (1, 64, 64, 16)),
         ("ssm_b_im", (1, 64, 64, 16)), ("ssm_c_re", (1, 64, 16, 64)), ("ssm_c_im", (1, 64, 16, 64)),
         ("ssm_d", (1, 1024)), ("ssm_log_step", (1, 64)), ("attn_sinks", (1, 16)), ("rel_bias_table", (32, 16)),
         ("ln_gain", (1, 2048)), ("ln_bias", (1, 2048)))
SMALL_SIZE = sum(int(np.prod(s)) for _, s in SMALL)
PACK_ROWS = -(-(SMALL_SIZE + 1) // (8 * 128)) * 8


def _pack(*groups, extra=None):
    flat = []
    for values in groups:
        flat += [values[k].reshape(-1).astype(F32) for k, _ in SMALL]
        flat.append(jnp.zeros((1,), F32) if extra is None else extra.reshape(1))
        flat.append(jnp.zeros((PACK_ROWS * 128 - SMALL_SIZE - 1,), F32))
    return jnp.concatenate(flat).reshape(len(groups) * PACK_ROWS, 128)


def _unpack(packed):
    flat = packed.reshape(-1)
    out, off = {}, 0
    for k, s in SMALL:
        size = int(np.prod(s))
        out[k] = flat[off:off + size].reshape(s)
        off += size
    return out, flat[off]


def _small_exchange(packed):
    def copies(ins, outs, sems):
        send_sems, recv_sems, local_sems = sems
        x, y, c, _ = _position()
        remote = []
        for r in range(1, 8):
            peer = tuple(1 - v if (r >> s) & 1 else v for v, s in ((x, 2), (y, 1), (c, 0)))
            remote.append(_remote(ins[0], outs[0].at[r], send_sems.at[r - 1], recv_sems.at[r - 1], peer))
        return remote, [pltpu.make_async_copy(ins[0], outs[0].at[0], local_sems.at[0])]

    return _Exchange([packed], [jax.ShapeDtypeStruct((8, *packed.shape), F32)], (7, 7, 1), copies)


def _merge_exchanges(a, b):
    na, ma, sa = len(a.ins), len(a.out_shape), len(a.n_sems)

    def copies(ins, outs, sems):
        ra, la = a.copies(ins[:na], outs[:ma], sems[:sa])
        rb, lb = b.copies(ins[na:], outs[ma:], sems[sa:])
        return ra + rb, la + lb

    return _Exchange(a.ins + b.ins, a.out_shape + b.out_shape, a.n_sems + b.n_sems, copies)


def _small_sum_body(slots_ref, o_ref):
    x, y, c, _ = _position()
    me = 4 * x + 2 * y + c
    acc = slots_ref[me]
    for s in range(1, 8):
        acc = acc + slots_ref[jnp.bitwise_xor(me, s)]
    o_ref[...] = acc


def _small_sum(slots):
    vm = pl.BlockSpec(memory_space=pltpu.VMEM)
    return pl.pallas_call(
        functools.partial(_small_sum_body), name="small_sum",
        in_specs=[vm], out_specs=vm, out_shape=jax.ShapeDtypeStruct(slots.shape[1:], F32),
        compiler_params=pltpu.CompilerParams(vmem_limit_bytes=VMEM_LIMIT),
    )(slots)


LATE = BIG[1:]


def _local_step(x, target, shards, core, chip, lam_re, lam_im, b_re, b_im, c_re, c_im, d_skip,
                log_step, sinks, rel_bias_table, ln_gain, ln_bias):
    t = x.shape[0]
    send_sems, recv_sems, shard_thru, land_thru, token = _gather_start(shards["w_in"], "w_in")
    xb, xbt = _cast_and_transpose(x, token, name="cast_x")
    bucket = jnp.asarray(_bucket_table())
    p_re, p_im, b_bd, c_bd = _ssm_prepare(lam_re, lam_im, b_re, b_im, c_re, c_im, log_step)
    bias = _bias_build(rel_bias_table, bucket)

    act = dict(out_dtype=BF16)
    proj = _proj_shards(xb, shards["w_in"], chip, (0,), name="proj_own")
    landed = _gather_wait(send_sems, recv_sems, shard_thru, land_thru, proj, "w_in")
    (w_in,) = _pass_on([landed], BIG[:1], own=[shards["w_in"]])
    proj, landed = _proj_shards(xb, w_in, chip, (1, 2, 3), name="proj", base=proj,
                                exchange=_gather_exchange(shards, LATE))
    w_glu, w_bs, w_ba, w_out = _pass_on(list(landed), LATE)
    y_ssm, g_in, g_in_t, car_re, car_im = _ssm_forward(proj, p_re, p_im, b_bd, c_bd, d_skip)
    glu = _matmul(g_in, w_glu, "nn", name="glu", tm=1024, tn=2048, **act)
    h_ssm, h_ssm_t = _rowwise(functools.partial(_ssm_gate_fwd_body), "ssm_gate_fwd", t,
                              [(glu, 2 * D_SSM, 0), (proj, D_SSM, C_ZS)],
                              [((t, D_SSM), BF16, D_SSM, 0), ((D_SSM, t), BF16, D_SSM, "T")])
    attn, lse = _attn_forward(proj, sinks, bias)
    h_attn, h_attn_t = _rowwise(functools.partial(_attn_gate_fwd_body), "attn_gate_fwd", t,
                                [(attn, D_ATTN, 0), (proj, D_ATTN, C_ZA)],
                                [((t, D_ATTN), BF16, D_ATTN, 0), ((D_ATTN, t), BF16, D_ATTN, "T")])
    bs = _matmul(h_ssm, w_bs, "nn", name="branch_ssm", tm=1024, tn=2048, **act)
    ba = _matmul(h_attn, w_ba, "nn", name="branch_attn", tm=1024, tn=2048, **act)
    gl_in = (proj, 2 * D_MODEL, C_GL)
    merged, merged_t = _rowwise(functools.partial(_merge_fwd_body), "merge_fwd", t,
                                [(bs, D_MODEL, 0), (ba, D_MODEL, 0), gl_in],
                                [((t, D_MODEL), BF16, D_MODEL, 0), ((D_MODEL, t), BF16, D_MODEL, "T")])
    out = _matmul(merged, w_out, "nn", name="out_proj", tm=1024, tn=1024, **act)
    drb, loss, g_gain, g_bias = _rowwise(
        functools.partial(_ln_loss_body), "ln_loss", t,
        [(x, D_MODEL, 0), (out, D_MODEL, 0), (target, D_MODEL, 0), (ln_gain, None, 0), (ln_bias, None, 0)],
        [((t, D_MODEL), BF16, D_MODEL, 0), ((1, 1), F32, None, 0),
         ((1, D_MODEL), F32, None, 0), ((1, D_MODEL), F32, None, 0)])

    g_w_out = _matmul(merged_t, drb, "nn", name="g_w_out", tm=512, tn=512)
    d_merged = _matmul(drb, w_out, "nt", name="d_merged", tm=1024, tn=1024, **act)
    d_bs, d_ba, dproj = _rowwise(
        functools.partial(_merge_bwd_body), "merge_bwd", t,
        [(d_merged, D_MODEL, 0), (bs, D_MODEL, 0), (ba, D_MODEL, 0), gl_in],
        [((t, D_MODEL), BF16, D_MODEL, 0), ((t, D_MODEL), BF16, D_MODEL, 0),
         ((t, D_IN), BF16, 2 * D_MODEL, C_GL)])
    g_w_bs = _matmul(h_ssm_t, d_bs, "nn", name="g_w_branch_ssm", tm=512, tn=512)
    d_h_ssm = _matmul(d_bs, w_bs, "nt", name="d_h_ssm", tm=1024, tn=1024, **act)
    g_w_ba = _matmul(h_attn_t, d_ba, "nn", name="g_w_branch_attn", tm=512, tn=512)
    d_h_attn = _matmul(d_ba, w_ba, "nt", name="d_h_attn", tm=1024, tn=1024, **act)

    d_attn, dproj = _rowwise(
        functools.partial(_attn_gate_bwd_body), "attn_gate_bwd", t,
        [(d_h_attn, D_ATTN, 0), (attn, D_ATTN, 0), (proj, D_ATTN, C_ZA), (dproj, None, "any")],
        [((t, D_ATTN), BF16, D_ATTN, 0), ((t, D_IN), BF16, D_ATTN, C_ZA)], aliases={3: 1})
    dproj, d_kv, g_bias_full, g_sinks = _attn_backward(proj, sinks, bias, d_attn, attn, lse, dproj)
    (dproj,) = _rowwise(functools.partial(_place_body), "place_dkv", t,
                        [(d_kv, 2 * D_KV, 0), (dproj, None, "any")],
                        [((t, D_IN), BF16, 2 * D_KV, C_K)], aliases={1: 0})
    g_table = _bias_grad(g_bias_full, bucket)

    d_glu, dproj = _rowwise(
        functools.partial(_ssm_gate_bwd_body), "ssm_gate_bwd", t,
        [(d_h_ssm, D_SSM, 0), (glu, 2 * D_SSM, 0), (proj, D_SSM, C_ZS), (dproj, None, "any")],
        [((t, 2 * D_SSM), BF16, 2 * D_SSM, 0), ((t, D_IN), BF16, D_SSM, C_ZS)], aliases={3: 1})
    g_w_glu = _matmul(g_in_t, d_glu, "nn", name="g_w_glu", tm=512, tn=512)
    d_g_in = _matmul(d_glu, w_glu, "nt", name="d_g_in", tm=1024, tn=1024, **act)
    dproj, g_bbd, g_cbd, g_lam_re, g_lam_im, g_d = _ssm_backward(
        proj, d_g_in, y_ssm, car_re, car_im, p_re, p_im, b_bd, c_bd, d_skip, dproj)
    g_lr, g_li, g_br, g_bi, g_cr, g_ci, g_ls = _ssm_param_grads(
        lam_re, lam_im, b_re, b_im, log_step, g_lam_re, g_lam_im, g_bbd, g_cbd)

    late = dict(w_glu=g_w_glu, w_branch_ssm=g_w_bs, w_branch_attn=g_w_ba, w_out=g_w_out)
    g_w_in, late_slots = _matmul(xbt, dproj, "nn", name="g_w_in", tm=512, tn=512,
                                 exchange=_chip_exchange(_pair_sums(late, LATE, core), LATE))
    small = dict(ssm_lambda_re=g_lr, ssm_lambda_im=g_li, ssm_b_re=g_br, ssm_b_im=g_bi, ssm_c_re=g_cr,
                 ssm_c_im=g_ci, ssm_d=g_d, ssm_log_step=g_ls, attn_sinks=g_sinks[:, :N_Q_HEADS],
                 rel_bias_table=g_table, ln_gain=g_gain, ln_bias=g_bias)
    last = _merge_exchanges(_chip_exchange(_pair_sums(dict(w_in=g_w_in), BIG[:1], core), BIG[:1]),
                            _small_exchange(_pack(small, extra=loss)))
    grad_x, (in_slots, small_slots) = _matmul(dproj, w_in, "nt", name="grad_x", tm=512, tn=512, res=drb,
                                              res_scale=ALPHA, exchange=last)
    big = {**_shard_halves([in_slots], BIG[:1], core), **_shard_halves(late_slots, LATE, core)}
    return grad_x, big, _small_sum(small_slots)


WEIGHTS = ("w_in", "ssm_lambda_re", "ssm_lambda_im", "ssm_b_re", "ssm_b_im", "ssm_c_re", "ssm_c_im", "ssm_d",
           "ssm_log_step", "w_glu", "attn_sinks", "rel_bias_table", "w_branch_ssm", "w_branch_attn", "w_out",
           "ln_gain", "ln_bias")


def kernel(x, w_in, ssm_lambda_re, ssm_lambda_im, ssm_b_re, ssm_b_im, ssm_c_re, ssm_c_im, ssm_d, ssm_log_step, w_glu, attn_sinks, rel_bias_table, w_branch_ssm, w_branch_attn, w_out, ln_gain, ln_bias, loss_target, m_w_in, m_ssm_lambda_re, m_ssm_lambda_im, m_ssm_b_re, m_ssm_b_im, m_ssm_c_re, m_ssm_c_im, m_ssm_d, m_ssm_log_step, m_w_glu, m_attn_sinks, m_rel_bias_table, m_w_branch_ssm, m_w_branch_attn, m_w_out, m_ln_gain, m_ln_bias, v_w_in, v_ssm_lambda_re, v_ssm_lambda_im, v_ssm_b_re, v_ssm_b_im, v_ssm_c_re, v_ssm_c_im, v_ssm_d, v_ssm_log_step, v_w_glu, v_attn_sinks, v_rel_bias_table, v_w_branch_ssm, v_w_branch_attn, v_w_out, v_ln_gain, v_ln_bias):
    given = dict(locals())
    w = {k: given[k] for k in WEIGHTS}
    m = {k: given["m_" + k] for k in WEIGHTS}
    v = {k: given["v_" + k] for k in WEIGHTS}

    core = lax.axis_index("c").astype(jnp.int32).reshape(1)
    shards = {k: _cast(w[k][0], BF16, name="cast_" + k) for k in BIG}
    chip = (2 * lax.axis_index("x") + lax.axis_index("y")).astype(jnp.int32).reshape(1)
    grad_x, g_shard, g_packed = _local_step(
        x[0], loss_target[0], shards, core, chip, ssm_lambda_re[0], ssm_lambda_im[0], ssm_b_re[0],
        ssm_b_im[0], ssm_c_re[0], ssm_c_im[0], ssm_d, ssm_log_step, attn_sinks, rel_bias_table, ln_gain, ln_bias)
    loss_sum = _unpack(g_packed)[1]

    grad, delta, new_m, new_v = {}, {}, {}, {}
    for k in BIG:
        grad[k], delta[k], new_m[k], new_v[k] = _adamw_shard(w[k], m[k], v[k], g_shard[k], core,
                                                             name="adamw_" + k, rows=SHARE_ROWS[k])
    gs, ds, ms, vs = _adamw_packed(_pack(w, m, v), g_packed)
    for dst, packed in ((grad, gs), (delta, ds), (new_m, ms), (new_v, vs)):
        dst.update(_unpack(packed)[0])

    return (loss_sum, grad_x[None], *[grad[k] for k in WEIGHTS], *[delta[k] for k in WEIGHTS],
            *[new_m[k] for k in WEIGHTS], *[new_v[k] for k in WEIGHTS])
```

```python
import functools
import math

import numpy as np
import jax
import jax.numpy as jnp
from jax import lax
from jax.experimental import pallas as pl
from jax.experimental.pallas import tpu as pltpu

F32 = jnp.float32
BF16 = jnp.bfloat16

D_MODEL = 2048
D_SSM = 1024
SSM_GROUP = 16
N_GROUPS = 64
SSM_STATE = 64
N_LANES = N_GROUPS * SSM_STATE
N_Q_HEADS = 16
N_KV_HEADS = 4
HEAD_DIM = 64
Q_PER_KV = 4
D_ATTN = 1024
D_KV = 256
WINDOW = 128
BLOCK = 128
N_BUCKETS = 32
MAX_DISTANCE = 128
D_IN = 8704
ALPHA = 2.0 ** 0.25
LN_EPS = 1e-5
NEG_INF = -1e30
ATTN_SCALE = HEAD_DIM ** -0.5

C_U, C_ZS, C_Q, C_K, C_V, C_ZA, C_GL = 0, 1024, 2048, 3072, 3328, 3584, 4608

ADAM_LR = 0.001
ADAM_B1 = 0.9
ADAM_B2 = 0.999
ADAM_EPS = 1e-08
ADAM_WD = 0.01
ADAM_STEP = 10

N_CHIPS = 4
MESH = pl.DeviceIdType.MESH

SSM_CHUNK = 256
SEG_LEN = SSM_CHUNK // 8
SLAB_LANES = 512
N_SLABS = N_LANES // SLAB_LANES
SLAB_CH = D_SSM // N_SLABS

VMEM_LIMIT = 60 * 1024 * 1024


def _params(sem=None, **kw):
    return pltpu.CompilerParams(dimension_semantics=sem, vmem_limit_bytes=VMEM_LIMIT, **kw)


_DIMS = {"nn": (((1,), (0,)), ((), ())), "nt": (((1,), (1,)), ((), ())), "tn": (((0,), (0,)), ((), ()))}


class _Exchange:
    def __init__(self, ins, out_shape, n_sems, copies):
        self.ins, self.out_shape, self.n_sems, self.copies = list(ins), list(out_shape), list(n_sems), copies

    def start(self, ins, outs, sems):
        remote, local = self.copies(ins, outs, sems)
        for cp in remote + local:
            cp.start()

    def finish(self, ins, outs, sems):
        remote, local = self.copies(ins, outs, sems)
        for cp in remote:
            cp.wait_recv()
        for cp in remote:
            cp.wait_send()
        for cp in local:
            cp.wait()


def _mm_body(*refs, dims, nk, res_scale, exchange, grid):
    n_ex_in = len(exchange.ins) if exchange else 0
    n_ex_out = len(exchange.out_shape) if exchange else 0
    n_in = 2 + (res_scale is not None) + n_ex_in
    a_ref, b_ref = refs[0], refs[1]
    r_ref = refs[2] if res_scale is not None else None
    ex_in = refs[n_in - n_ex_in:n_in]
    o_ref, ex_out = refs[n_in], refs[n_in + 1:n_in + 1 + n_ex_out]
    acc_ref, ex_sems = refs[n_in + 1 + n_ex_out], refs[n_in + 2 + n_ex_out:]
    i, j, k = pl.program_id(0), pl.program_id(1), pl.program_id(2)

    if exchange:
        @pl.when(jnp.logical_and(jnp.logical_and(i == 0, j == 0), k == 0))
        def _():
            exchange.start(ex_in, ex_out, ex_sems)

    part = lax.dot_general(a_ref[...].astype(BF16), b_ref[...].astype(BF16), _DIMS[dims],
                           preferred_element_type=F32)

    def finish(acc):
        if r_ref is not None:
            acc = acc + res_scale * r_ref[...].astype(F32)
        o_ref[...] = acc.astype(o_ref.dtype)

    if nk == 1:
        finish(part)
    else:
        @pl.when(k == 0)
        def _():
            acc_ref[...] = part

        @pl.when(k > 0)
        def _():
            acc_ref[...] += part

        @pl.when(k == nk - 1)
        def _():
            finish(acc_ref[...])

    if exchange:
        @pl.when(jnp.logical_and(jnp.logical_and(i == grid[0] - 1, j == grid[1] - 1), k == grid[2] - 1))
        def _():
            exchange.finish(ex_in, ex_out, ex_sems)


def _mm_onto_base_body(*refs, base_index, **kw):
    refs = list(refs)
    del refs[base_index]
    _mm_body(*refs, **kw)


def _matmul(a, b, dims, *, name, out_dtype=F32, tm=512, tn=512, tk=None, res=None, res_scale=None, exchange=None,
            rows=None, base=None):
    if dims == "nn":
        (m, kk), n = a.shape, b.shape[1]
    elif dims == "nt":
        (m, kk), n = a.shape, b.shape[0]
    else:
        (kk, m), n = a.shape, b.shape[1]
    tm, tn = min(tm, m), min(tn, n)
    tk = kk if tk is None else min(tk, kk)
    assert m % tm == 0 and n % tn == 0 and kk % tk == 0, (name, m, n, kk, tm, tn, tk)
    nk = kk // tk
    first, count = rows if rows is not None else (0, m // tm)
    assert rows is None or dims != "tn"
    a_spec = {"nn": pl.BlockSpec((tm, tk), lambda i, j, k: (i + first, k)),
              "nt": pl.BlockSpec((tm, tk), lambda i, j, k: (i + first, k)),
              "tn": pl.BlockSpec((tk, tm), lambda i, j, k: (k, i))}[dims]
    b_spec = {"nn": pl.BlockSpec((tk, tn), lambda i, j, k: (k, j)),
              "nt": pl.BlockSpec((tn, tk), lambda i, j, k: (j, k)),
              "tn": pl.BlockSpec((tk, tn), lambda i, j, k: (k, j))}[dims]
    in_specs, args = [a_spec, b_spec], [a, b]
    if res is not None:
        in_specs.append(pl.BlockSpec((tm, tn), lambda i, j, k: (i + first, j)))
        args.append(res)
    grid = (count, n // tn, nk)
    out_specs = [pl.BlockSpec((tm, tn), lambda i, j, k: (i + first, j))]
    out_shape = [jax.ShapeDtypeStruct((m, n), out_dtype)]
    scratch = [pltpu.VMEM((tm, tn), F32)]
    if exchange:
        in_specs += [pl.BlockSpec(memory_space=pl.ANY)] * len(exchange.ins)
        args += exchange.ins
        out_specs += [pl.BlockSpec(memory_space=pl.ANY)] * len(exchange.out_shape)
        out_shape += exchange.out_shape
        scratch += [pltpu.SemaphoreType.DMA((s,)) for s in exchange.n_sems]
    kw = dict(dims=dims, nk=nk, res_scale=res_scale if res is not None else None, exchange=exchange, grid=grid)
    body, aliases = functools.partial(_mm_body, **kw), {}
    if base is not None:
        body, aliases = functools.partial(_mm_onto_base_body, base_index=len(args), **kw), {len(args): 0}
        in_specs.append(pl.BlockSpec(memory_space=pl.ANY))
        args.append(base)
    outs = pl.pallas_call(
        body, name=name, grid=grid, in_specs=in_specs, out_specs=out_specs, out_shape=out_shape,
        scratch_shapes=scratch, input_output_aliases=aliases,
        compiler_params=_params(("arbitrary",) * 3 if exchange else ("parallel", "parallel", "arbitrary")),
    )(*args)
    return (outs[0], outs[1:]) if exchange else outs[0]


def _proj_shards_body(chip_ref, *refs, n_ex_in, has_base, exchange, grid):
    del chip_ref
    refs = list(refs)
    if has_base:
        del refs[2 + n_ex_in]
    _mm_body(*refs, dims="nn", nk=1, res_scale=None, exchange=exchange, grid=grid)


def _proj_shards(xb, w, chip, offsets, *, name, base=None, exchange=None, tm=1024):
    t, d = xb.shape
    cs = D_IN // N_CHIPS
    own = w.shape[1] == cs
    tm = min(tm, t)
    col = lambda j, cr: (cr[0] + offsets[0] + j) % N_CHIPS
    grid = (len(offsets), t // tm, 1)
    in_specs = [pl.BlockSpec((tm, d), lambda j, i, k, cr: (i, 0)),
                pl.BlockSpec((d, cs), lambda j, i, k, cr: (0, 0 if own else col(j, cr)))]
    args = [xb, w]
    out_specs = [pl.BlockSpec((tm, cs), lambda j, i, k, cr: (i, col(j, cr)))]
    out_shape = [jax.ShapeDtypeStruct((t, D_IN), BF16)]
    scratch = [pltpu.VMEM((8, 128), F32)]
    n_ex_in = 0
    if exchange:
        n_ex_in = len(exchange.ins)
        in_specs += [pl.BlockSpec(memory_space=pl.ANY)] * n_ex_in
        args += exchange.ins
        out_specs += [pl.BlockSpec(memory_space=pl.ANY)] * len(exchange.out_shape)
        out_shape += exchange.out_shape
        scratch += [pltpu.SemaphoreType.DMA((s,)) for s in exchange.n_sems]
    aliases = {}
    if base is not None:
        in_specs.append(pl.BlockSpec(memory_space=pl.ANY))
        args.append(base)
        aliases = {len(args): 0}
    outs = pl.pallas_call(
        functools.partial(_proj_shards_body, n_ex_in=n_ex_in, has_base=base is not None, exchange=exchange, grid=grid),
        name=name,
        grid_spec=pltpu.PrefetchScalarGridSpec(num_scalar_prefetch=1, grid=grid, in_specs=in_specs,
                                               out_specs=out_specs, scratch_shapes=scratch),
        out_shape=out_shape, input_output_aliases=aliases,
        compiler_params=_params(("arbitrary",) * 3),
    )(chip, *args)
    return (outs[0], outs[1:]) if exchange else outs[0]


def _sigmoid(v):
    return 1.0 / (1.0 + jnp.exp(-v))


def _silu_and_grad(z):
    s = _sigmoid(z)
    return z * s, s * (1.0 + z * (1.0 - s))


def _cast_body(x_ref, o_ref):
    o_ref[...] = x_ref[...].astype(o_ref.dtype)


def _cast_and_transpose_body(x_ref, after_ref, o_ref, ot_ref):
    del after_ref
    xb = x_ref[...].astype(BF16)
    o_ref[...] = xb
    ot_ref[...] = xb.T


def _cast_and_transpose(x, after, *, name, rows=512):
    m, n = x.shape
    rows = min(rows, m)
    return pl.pallas_call(
        functools.partial(_cast_and_transpose_body), name=name, grid=(m // rows,),
        in_specs=[pl.BlockSpec((rows, n), lambda i: (i, 0)), pl.BlockSpec(memory_space=pl.ANY)],
        out_specs=[pl.BlockSpec((rows, n), lambda i: (i, 0)), pl.BlockSpec((n, rows), lambda i: (0, i))],
        out_shape=[jax.ShapeDtypeStruct((m, n), BF16), jax.ShapeDtypeStruct((n, m), BF16)],
        compiler_params=_params(("parallel",)),
    )(x, after)


def _cast(x, dtype, *, name, rows=512):
    m, n = x.shape
    rows = min(rows, m)
    return pl.pallas_call(
        functools.partial(_cast_body), name=name, grid=(m // rows,),
        in_specs=[pl.BlockSpec((rows, n), lambda i: (i, 0))],
        out_specs=pl.BlockSpec((rows, n), lambda i: (i, 0)),
        out_shape=jax.ShapeDtypeStruct((m, n), dtype),
        compiler_params=_params(("parallel",)),
    )(x)


def _lam_bar(lr, li, ls):
    step = jnp.exp(ls)
    er = jnp.exp(lr * step)
    return step, er * jnp.cos(li * step), er * jnp.sin(li * step)


def _ssm_pow_body(lr_ref, li_ref, ls_ref, pr_ref, pi_ref):
    _, ar, ai = _lam_bar(lr_ref[...], li_ref[...], ls_ref[...])
    cr, ci = ar, ai
    for i in range(SEG_LEN):
        pr_ref[pl.ds(i, 1), :] = cr
        pi_ref[pl.ds(i, 1), :] = ci
        cr, ci = cr * ar - ci * ai, cr * ai + ci * ar


def _ssm_bbar_body(lr_ref, li_ref, ls_ref, br_ref, bi_ref, or_ref, oi_ref):
    lr, li = lr_ref[...], li_ref[...]
    _, ar, ai = _lam_bar(lr, li, ls_ref[...])
    d = lr * lr + li * li
    ir, ii = lr / d, -li / d
    nr, ni = ar - 1.0, ai
    cr, ci = nr * ir - ni * ii, nr * ii + ni * ir
    br, bi = br_ref[...], bi_ref[...]
    or_ref[...] = cr * br - ci * bi
    oi_ref[...] = cr * bi + ci * br


def _ssm_param_bwd_body(lr_ref, li_ref, ls_ref, br_ref, bi_ref, glr_ref, gli_ref, gbr_ref, gbi_ref,
                        dlr_ref, dli_ref, dls_ref, dbr_ref, dbi_ref):
    lr, li = lr_ref[...], li_ref[...]
    step, ar, ai = _lam_bar(lr, li, ls_ref[...])
    d = lr * lr + li * li
    ir, ii = lr / d, -li / d
    nr, ni = ar - 1.0, ai
    cr, ci = nr * ir - ni * ii, nr * ii + ni * ir
    br, bi, gbr, gbi = br_ref[...], bi_ref[...], gbr_ref[...], gbi_ref[...]
    dbr_ref[...] = cr * gbr + ci * gbi
    dbi_ref[...] = cr * gbi - ci * gbr
    gcr = jnp.sum(br * gbr + bi * gbi, axis=1, keepdims=True)
    gci = jnp.sum(br * gbi - bi * gbr, axis=1, keepdims=True)
    gnr, gni = ir * gcr + ii * gci, ir * gci - ii * gcr
    gir, gii = nr * gcr + ni * gci, nr * gci - ni * gcr
    gtr, gti = glr_ref[...] + gnr, gli_ref[...] + gni
    i2r, i2i = ir * ir - ii * ii, 2.0 * ir * ii
    g1r, g1i = -(i2r * gir + i2i * gii), -(i2r * gii - i2i * gir)
    g2r, g2i = step * (ar * gtr + ai * gti), step * (ar * gti - ai * gtr)
    mr, mi = lr * ar - li * ai, lr * ai + li * ar
    dlr_ref[...] = g1r + g2r
    dli_ref[...] = g1i + g2i
    dls_ref[...] = (mr * gtr + mi * gti) * step


def _whole(shape):
    return pl.BlockSpec(shape, lambda *_: (0,) * len(shape))


def _ssm_prepare(lam_re, lam_im, b_re, b_im, c_re, c_im, log_step):
    row = lambda a: a.reshape(1, N_LANES)
    col = lambda a: a.reshape(N_LANES, 1)
    ls = jnp.repeat(log_step.reshape(N_GROUPS), SSM_STATE)
    p_re, p_im = pl.pallas_call(
        functools.partial(_ssm_pow_body), name="ssm_pow",
        in_specs=[_whole((1, N_LANES))] * 3, out_specs=[_whole((SEG_LEN, N_LANES))] * 2,
        out_shape=[jax.ShapeDtypeStruct((SEG_LEN, N_LANES), F32)] * 2, grid=(1,),
    )(row(lam_re), row(lam_im), row(ls))
    bb_re, bb_im = pl.pallas_call(
        functools.partial(_ssm_bbar_body), name="ssm_bbar",
        in_specs=[_whole((N_LANES, 1))] * 3 + [_whole((N_LANES, SSM_GROUP))] * 2,
        out_specs=[_whole((N_LANES, SSM_GROUP))] * 2,
        out_shape=[jax.ShapeDtypeStruct((N_LANES, SSM_GROUP), F32)] * 2, grid=(1,),
    )(col(lam_re), col(lam_im), col(ls), b_re.reshape(N_LANES, SSM_GROUP), b_im.reshape(N_LANES, SSM_GROUP))
    eye = jnp.eye(8, dtype=F32)

    def b_slabs(bb):
        t = bb.reshape(N_SLABS, 8, SSM_STATE, SSM_GROUP).transpose(0, 1, 3, 2)
        return (t[:, :, :, None, :] * eye[None, :, None, :, None]).reshape(N_SLABS, SLAB_CH, SLAB_LANES)

    def c_slabs(c):
        t = c.reshape(N_SLABS, 8, SSM_GROUP, SSM_STATE).transpose(0, 1, 3, 2)
        return (t[:, :, :, None, :] * eye[None, :, None, :, None]).reshape(N_SLABS, SLAB_LANES, SLAB_CH)

    b_bd = jnp.concatenate([b_slabs(bb_re), b_slabs(bb_im)], axis=2).astype(BF16)
    c_bd = jnp.concatenate([c_slabs(c_re.reshape(N_GROUPS, SSM_GROUP, SSM_STATE)),
                            -c_slabs(c_im.reshape(N_GROUPS, SSM_GROUP, SSM_STATE))], axis=1).astype(BF16)
    return p_re, p_im, b_bd, c_bd


def _diag_blocks_b(g):
    t = g.reshape(N_SLABS, 8, SSM_GROUP, 8, SSM_STATE)
    t = jnp.stack([t[:, i, :, i, :] for i in range(8)], axis=1)
    return t.transpose(0, 1, 3, 2).reshape(N_LANES, SSM_GROUP)


def _diag_blocks_c(g):
    t = g.reshape(N_SLABS, 8, SSM_STATE, 8, SSM_GROUP)
    t = jnp.stack([t[:, i, :, i, :] for i in range(8)], axis=1)
    return t.transpose(0, 1, 3, 2).reshape(N_GROUPS, SSM_GROUP, SSM_STATE)


def _ssm_param_grads(lam_re, lam_im, b_re, b_im, log_step, g_lam_re, g_lam_im, g_bbd, g_cbd):
    col = lambda a: a.reshape(N_LANES, 1)
    ls = jnp.repeat(log_step.reshape(N_GROUPS), SSM_STATE)
    gbr = _diag_blocks_b(g_bbd[:, :, :SLAB_LANES])
    gbi = _diag_blocks_b(g_bbd[:, :, SLAB_LANES:])
    outs = pl.pallas_call(
        functools.partial(_ssm_param_bwd_body), name="ssm_param_bwd", grid=(1,),
        in_specs=[_whole((N_LANES, 1))] * 3 + [_whole((N_LANES, SSM_GROUP))] * 2 + [_whole((N_LANES, 1))] * 2
        + [_whole((N_LANES, SSM_GROUP))] * 2,
        out_specs=[_whole((N_LANES, 1))] * 3 + [_whole((N_LANES, SSM_GROUP))] * 2,
        out_shape=[jax.ShapeDtypeStruct((N_LANES, 1), F32)] * 3 + [jax.ShapeDtypeStruct((N_LANES, SSM_GROUP), F32)] * 2,
    )(col(lam_re), col(lam_im), col(ls), b_re.reshape(N_LANES, SSM_GROUP), b_im.reshape(N_LANES, SSM_GROUP),
      col(g_lam_re), col(g_lam_im), gbr, gbi)
    dlr, dli, dls, dbr, dbi = outs
    d_c_re = _diag_blocks_c(g_cbd[:, :SLAB_LANES, :])
    d_c_im = -_diag_blocks_c(g_cbd[:, SLAB_LANES:, :])
    return (dlr.reshape(1, N_GROUPS, SSM_STATE), dli.reshape(1, N_GROUPS, SSM_STATE),
            dbr.reshape(1, N_GROUPS, SSM_STATE, SSM_GROUP), dbi.reshape(1, N_GROUPS, SSM_STATE, SSM_GROUP),
            d_c_re[None], d_c_im[None], dls.reshape(N_GROUPS, SSM_STATE).sum(axis=1).reshape(1, N_GROUPS))


def _bcast8(v):
    return jnp.broadcast_to(v, (8, v.shape[1]))


def _segment_permutation():
    p = np.zeros((SSM_CHUNK, SSM_CHUNK), np.float32)
    rows = np.arange(SSM_CHUNK)
    p[rows, (rows % 8) * SEG_LEN + rows // 8] = 1.0
    return p


def _permute_exact(perm, val, pieces):
    out, rest = None, val
    for n in range(pieces):
        part = rest.astype(BF16)
        moved = jnp.dot(perm, part, preferred_element_type=F32)
        out = moved if out is None else out + moved
        if n + 1 < pieces:
            rest = rest - part.astype(F32)
    return out


def _scan_buffer():
    return pltpu.VMEM((SSM_CHUNK, N_LANES), F32)


def _lanes(k):
    return pl.ds(k * SLAB_LANES, SLAB_LANES)


def _tile(i):
    return pl.ds(i * 8 if isinstance(i, int) else pl.multiple_of(i * 8, 8), 8)


SCAN_LANES = 1024


def _scan_lanes(k):
    return pl.ds(k * SCAN_LANES, SCAN_LANES)


def _seg_get(ref, k, i):
    return ref[_tile(i), _scan_lanes(k)]


def _seg_put(ref, k, i, val):
    ref[_tile(i), _scan_lanes(k)] = val


def _slab_get(ref, k):
    return ref[:, _lanes(k)]


def _slab_put(ref, k, val):
    ref[:, _lanes(k)] = val


def _scan_forward(s_re, s_im, p_re, p_im, car_re, car_im, sp_re=None, sp_im=None):
    for k in range(N_LANES // SCAN_LANES):
        ln = _scan_lanes(k)
        ar, ai =_bcast8(p_re[pl.ds(0, 1), ln]), _bcast8(p_im[pl.ds(0, 1), ln])

        def step(i, s, k=k, ar=ar, ai=ai):
            sr, si = s
            nr = ar * sr - ai * si + _seg_get(s_re, k, i)
            ni = ar * si + ai * sr + _seg_get(s_im, k, i)
            _seg_put(s_re, k, i, nr)
            _seg_put(s_im, k, i, ni)
            return nr, ni

        zero = jnp.zeros((8, SCAN_LANES), F32)
        er, ei = lax.fori_loop(0, SEG_LEN, step, (zero, zero), unroll=4)
        lr, li = p_re[pl.ds(SEG_LEN - 1, 1), ln], p_im[pl.ds(SEG_LEN - 1, 1), ln]
        cr, ci = car_re[pl.ds(0, 1), ln], car_im[pl.ds(0, 1), ln]
        rows_r, rows_i = [], []
        for r in range(8):
            rows_r.append(cr)
            rows_i.append(ci)
            cr, ci = er[r:r + 1] + lr * cr - li * ci, ei[r:r + 1] + lr * ci + li * cr
        pr8, pi8 = jnp.concatenate(rows_r, axis=0), jnp.concatenate(rows_i, axis=0)
        car_re[:, ln] = _bcast8(cr)
        car_im[:, ln] = _bcast8(ci)
        if sp_re is not None:
            sp_re[:, ln] = pr8
            sp_im[:, ln] = pi8

        def fix(i, _, k=k, ln=ln, pr8=pr8, pi8=pi8):
            qr, qi = _bcast8(p_re[pl.ds(i, 1), ln]), _bcast8(p_im[pl.ds(i, 1), ln])
            _seg_put(s_re, k, i, _seg_get(s_re, k, i) + qr * pr8 - qi * pi8)
            _seg_put(s_im, k, i, _seg_get(s_im, k, i) + qr * pi8 + qi * pr8)
            return 0

        lax.fori_loop(0, SEG_LEN, fix, 0, unroll=4)


def _scan_backward(g_re, g_im, s_re, s_im, sp_re, sp_im, p_re, p_im, car_re, car_im, acc_re, acc_im):
    for k in range(N_LANES // SCAN_LANES):
        ln = _scan_lanes(k)
        ar, ai =_bcast8(p_re[pl.ds(0, 1), ln]), -_bcast8(p_im[pl.ds(0, 1), ln])

        def step(j, s, k=k, ar=ar, ai=ai):
            i = SEG_LEN - 1 - j
            sr, si = s
            nr = ar * sr - ai * si + _seg_get(g_re, k, i)
            ni = ar * si + ai * sr + _seg_get(g_im, k, i)
            _seg_put(g_re, k, i, nr)
            _seg_put(g_im, k, i, ni)
            return nr, ni

        zero = jnp.zeros((8, SCAN_LANES), F32)
        er, ei = lax.fori_loop(0, SEG_LEN, step, (zero, zero), unroll=4)
        lr, li = p_re[pl.ds(SEG_LEN - 1, 1), ln], -p_im[pl.ds(SEG_LEN - 1, 1), ln]
        cr, ci = car_re[pl.ds(0, 1), ln], car_im[pl.ds(0, 1), ln]
        rows_r, rows_i = [None] * 8, [None] * 8
        for r in range(7, -1, -1):
            rows_r[r], rows_i[r] = cr, ci
            cr, ci = er[r:r + 1] + lr * cr - li * ci, ei[r:r + 1] + lr * ci + li * cr
        nr8, ni8 = jnp.concatenate(rows_r, axis=0), jnp.concatenate(rows_i, axis=0)
        car_re[:, ln] = _bcast8(cr)
        car_im[:, ln] = _bcast8(ci)

        def fix(i, acc, k=k, ln=ln, nr8=nr8, ni8=ni8):
            qr = _bcast8(p_re[pl.ds(SEG_LEN - 1 - i, 1), ln])
            qi = -_bcast8(p_im[pl.ds(SEG_LEN - 1 - i, 1), ln])
            gr = _seg_get(g_re, k, i) + qr * nr8 - qi * ni8
            gi = _seg_get(g_im, k, i) + qr * ni8 + qi * nr8
            _seg_put(g_re, k, i, gr)
            _seg_put(g_im, k, i, gi)
            return gr, gi

        def prod(xr, xi, gr, gi):
            return xr * gr + xi * gi, xr * gi - xi * gr

        gr, gi = fix(0, None)
        a_r, a_i = prod(sp_re[:, ln], sp_im[:, ln], gr, gi)

        def fix_acc(i, acc, k=k, fix=fix):
            gr, gi = fix(i, None)
            dr, di = prod(_seg_get(s_re, k, i - 1), _seg_get(s_im, k, i - 1), gr, gi)
            return acc[0] + dr, acc[1] + di

        acc = (a_r, a_i)
        for i in range(1, 4):
            acc = fix_acc(i, acc)
        a_r, a_i = lax.fori_loop(4, SEG_LEN, fix_acc, acc, unroll=4)
        acc_re[:, ln] += a_r
        acc_im[:, ln] += a_i


def _gelu_and_grad(y):
    cdf = 0.5 * (1.0 + lax.erf(y * (2.0 ** -0.5)))
    pdf = jnp.exp(-0.5 * y * y) * (1.0 / math.sqrt(2.0 * math.pi))
    return y * cdf, cdf + y * pdf


def _ssm_fwd_body(u_ref, bbd_ref, cbd_ref, pre_ref, pim_ref, d_ref, perm_ref, unperm_ref,
                  y_ref, gin_ref, gint_ref, cre_out, cim_out, s_re, s_im, car_re, car_im, yp):
    c = pl.program_id(0)

    @pl.when(c == 0)
    def _():
        car_re[...] = jnp.zeros_like(car_re)
        car_im[...] = jnp.zeros_like(car_im)

    cre_out[...] = car_re[...]
    cim_out[...] = car_im[...]
    u = u_ref[...].astype(F32)
    up = jnp.dot(perm_ref[...], u_ref[...].astype(BF16), preferred_element_type=F32).astype(BF16)
    for k in range(N_SLABS):
        bu = jnp.dot(up[:, k * SLAB_CH:(k + 1) * SLAB_CH], bbd_ref[k], preferred_element_type=F32)
        _slab_put(s_re, k, bu[:, :SLAB_LANES])
        _slab_put(s_im, k, bu[:, SLAB_LANES:])
    _scan_forward(s_re, s_im, pre_ref, pim_ref, car_re, car_im)
    for k in range(N_SLABS):
        yp[:, pl.ds(k * SLAB_CH, SLAB_CH)] = (
            jnp.dot(_slab_get(s_re, k).astype(BF16), cbd_ref[k, :SLAB_LANES, :], preferred_element_type=F32)
            + jnp.dot(_slab_get(s_im, k).astype(BF16), cbd_ref[k, SLAB_LANES:, :], preferred_element_type=F32))
    y = _permute_exact(unperm_ref[...], yp[...], 1) + d_ref[...] * u
    y_ref[...] = y.astype(y_ref.dtype)
    _store_with_transpose(_gelu_and_grad(y)[0], gin_ref, gint_ref)


def _ssm_forward(proj, p_re, p_im, b_bd, c_bd, d_skip):
    t = proj.shape[0]
    nc = t // SSM_CHUNK
    perm = _segment_permutation()
    return pl.pallas_call(
        functools.partial(_ssm_fwd_body), name="ssm_fwd", grid=(nc,),
        in_specs=[pl.BlockSpec((SSM_CHUNK, D_SSM), lambda c: (c, C_U // D_SSM)),
                  _whole(b_bd.shape), _whole(c_bd.shape), _whole(p_re.shape), _whole(p_im.shape),
                  _whole((1, D_SSM)), _whole(perm.shape), _whole(perm.shape)],
        out_specs=[pl.BlockSpec((SSM_CHUNK, D_SSM), lambda c: (c, 0)),
                   pl.BlockSpec((SSM_CHUNK, D_SSM), lambda c: (c, 0)),
                   pl.BlockSpec((D_SSM, SSM_CHUNK), lambda c: (0, c)),
                   pl.BlockSpec((None, 8, N_LANES), lambda c: (c, 0, 0)),
                   pl.BlockSpec((None, 8, N_LANES), lambda c: (c, 0, 0))],
        out_shape=[jax.ShapeDtypeStruct((t, D_SSM), BF16), jax.ShapeDtypeStruct((t, D_SSM), BF16),
                   jax.ShapeDtypeStruct((D_SSM, t), BF16),
                   jax.ShapeDtypeStruct((nc, 8, N_LANES), F32), jax.ShapeDtypeStruct((nc, 8, N_LANES), F32)],
        scratch_shapes=[_scan_buffer(), _scan_buffer(),
                        pltpu.VMEM((8, N_LANES), F32), pltpu.VMEM((8, N_LANES), F32),
                        pltpu.VMEM((SSM_CHUNK, D_SSM), F32)],
        compiler_params=_params(("arbitrary",)),
    )(proj, b_bd, c_bd, p_re, p_im, d_skip, jnp.asarray(perm, BF16), jnp.asarray(perm.T, BF16))


def _ssm_bwd_body(u_ref, dgin_ref, y_ref, cre_in, cim_in, bbd_ref, cbd_ref, pre_ref, pim_ref, d_ref, perm_ref,
                  unperm_ref, dproj_in,
                  du_ref, gb_ref, gc_ref, glr_ref, gli_ref, gd_ref,
                  s_re, s_im, g_re, g_im, sp_re, sp_im, car_re, car_im, gcar_re, gcar_im, acc_re, acc_im, dup):
    del dproj_in
    c = pl.program_id(0)
    nc = pl.num_programs(0)

    @pl.when(c == 0)
    def _():
        gcar_re[...] = jnp.zeros_like(gcar_re)
        gcar_im[...] = jnp.zeros_like(gcar_im)
        acc_re[...] = jnp.zeros_like(acc_re)
        acc_im[...] = jnp.zeros_like(acc_im)
        gb_ref[...] = jnp.zeros_like(gb_ref)
        gc_ref[...] = jnp.zeros_like(gc_ref)
        gd_ref[...] = jnp.zeros_like(gd_ref)

    car_re[...] = cre_in[...]
    car_im[...] = cim_in[...]
    u = u_ref[...].astype(F32)
    dy = dgin_ref[...].astype(F32) * _gelu_and_grad(y_ref[...].astype(F32))[1]
    gd_ref[...] += jnp.sum(dy * u, axis=0, keepdims=True)
    up = jnp.dot(perm_ref[...], u.astype(BF16), preferred_element_type=F32).astype(BF16)
    dyp = jnp.dot(perm_ref[...], dy.astype(BF16), preferred_element_type=F32).astype(BF16)
    for k in range(N_SLABS):
        ch = slice(k * SLAB_CH, (k + 1) * SLAB_CH)
        bu = jnp.dot(up[:, ch], bbd_ref[k], preferred_element_type=F32)
        _slab_put(s_re, k, bu[:, :SLAB_LANES])
        _slab_put(s_im, k, bu[:, SLAB_LANES:])
        ds = lax.dot_general(dyp[:, ch], cbd_ref[k], _DIMS["nt"], preferred_element_type=F32)
        _slab_put(g_re, k, ds[:, :SLAB_LANES])
        _slab_put(g_im, k, ds[:, SLAB_LANES:])
    _scan_forward(s_re, s_im, pre_ref, pim_ref, car_re, car_im, sp_re, sp_im)
    _scan_backward(g_re, g_im, s_re, s_im, sp_re, sp_im, pre_ref, pim_ref, gcar_re, gcar_im, acc_re, acc_im)
    for k in range(N_SLABS):
        ch = slice(k * SLAB_CH, (k + 1) * SLAB_CH)
        uk, dyk = up[:, ch], dyp[:, ch]
        sr, si = _slab_get(s_re, k).astype(BF16), _slab_get(s_im, k).astype(BF16)
        gr, gi = _slab_get(g_re, k).astype(BF16), _slab_get(g_im, k).astype(BF16)
        gc_ref[k, :SLAB_LANES, :] += lax.dot_general(sr, dyk, _DIMS["tn"], preferred_element_type=F32)
        gc_ref[k, SLAB_LANES:, :] += lax.dot_general(si, dyk, _DIMS["tn"], preferred_element_type=F32)
        gb_ref[k, :, :SLAB_LANES] += lax.dot_general(uk, gr, _DIMS["tn"], preferred_element_type=F32)
        gb_ref[k, :, SLAB_LANES:] += lax.dot_general(uk, gi, _DIMS["tn"], preferred_element_type=F32)
        dup[:, pl.ds(k * SLAB_CH, SLAB_CH)] = (
            lax.dot_general(gr, bbd_ref[k, :, :SLAB_LANES], _DIMS["nt"], preferred_element_type=F32)
            + lax.dot_general(gi, bbd_ref[k, :, SLAB_LANES:], _DIMS["nt"], preferred_element_type=F32))
    du = _permute_exact(unperm_ref[...], dup[...], 1) + d_ref[...] * dy
    du_ref[...] = du.astype(du_ref.dtype)

    @pl.when(c == nc - 1)
    def _():
        glr_ref[...] = jnp.sum(acc_re[...], axis=0, keepdims=True)
        gli_ref[...] = jnp.sum(acc_im[...], axis=0, keepdims=True)


def _ssm_backward(proj, dg_in, y_ssm, car_re, car_im, p_re, p_im, b_bd, c_bd, d_skip, dproj):
    t = proj.shape[0]
    nc = t // SSM_CHUNK
    rev = lambda c: nc - 1 - c
    big = _scan_buffer
    small = lambda: pltpu.VMEM((8, N_LANES), F32)
    perm = _segment_permutation()
    outs = pl.pallas_call(
        functools.partial(_ssm_bwd_body), name="ssm_bwd", grid=(nc,),
        in_specs=[pl.BlockSpec((SSM_CHUNK, D_SSM), lambda c: (rev(c), C_U // D_SSM)),
                  pl.BlockSpec((SSM_CHUNK, D_SSM), lambda c: (rev(c), 0)),
                  pl.BlockSpec((SSM_CHUNK, D_SSM), lambda c: (rev(c), 0)),
                  pl.BlockSpec((None, 8, N_LANES), lambda c: (rev(c), 0, 0)),
                  pl.BlockSpec((None, 8, N_LANES), lambda c: (rev(c), 0, 0)),
                  _whole(b_bd.shape), _whole(c_bd.shape), _whole(p_re.shape), _whole(p_im.shape),
                  _whole((1, D_SSM)), _whole(perm.shape), _whole(perm.shape), pl.BlockSpec(memory_space=pl.ANY)],
        out_specs=[pl.BlockSpec((SSM_CHUNK, D_SSM), lambda c: (rev(c), C_U // D_SSM)),
                   _whole(b_bd.shape), _whole(c_bd.shape), _whole((1, N_LANES)), _whole((1, N_LANES)),
                   _whole((1, D_SSM))],
        out_shape=[jax.ShapeDtypeStruct(dproj.shape, dproj.dtype),
                   jax.ShapeDtypeStruct(b_bd.shape, F32), jax.ShapeDtypeStruct(c_bd.shape, F32),
                   jax.ShapeDtypeStruct((1, N_LANES), F32), jax.ShapeDtypeStruct((1, N_LANES), F32),
                   jax.ShapeDtypeStruct((1, D_SSM), F32)],
        scratch_shapes=[big(), big(), big(), big()] + [small() for _ in range(8)]
        + [pltpu.VMEM((SSM_CHUNK, D_SSM), F32)],
        input_output_aliases={12: 0},
        compiler_params=_params(("arbitrary",)),
    )(proj, dg_in, y_ssm, car_re, car_im, b_bd, c_bd, p_re, p_im, d_skip, jnp.asarray(perm, BF16),
      jnp.asarray(perm.T, BF16), dproj)
    return outs


def _bucket_table():
    i = np.arange(BLOCK)[:, None]
    j = np.arange(2 * BLOCK)[None, :]
    dist = BLOCK + i - j
    ok = (dist >= 0) & (dist < WINDOW)
    max_exact = N_BUCKETS // 2
    d = np.maximum(dist, 1).astype(np.float32)
    large = max_exact + (np.log(d / max_exact) / math.log(MAX_DISTANCE / max_exact)
                         * (N_BUCKETS - max_exact)).astype(np.int32)
    large = np.minimum(large, N_BUCKETS - 1)
    bucket = np.where(dist < max_exact, dist, large)
    return np.where(ok, bucket, -1).astype(np.int32)


def _bias_build_body(table_ref, bucket_ref, o_ref):
    h = pl.program_id(0)
    bucket = bucket_ref[...]
    acc = jnp.full(bucket.shape, NEG_INF, F32)
    for b in range(N_BUCKETS):
        acc = jnp.where(bucket == b, table_ref[b, h], acc)
    o_ref[...] = acc


def _bias_build(rel_bias_table, bucket):
    return pl.pallas_call(
        functools.partial(_bias_build_body), name="bias_build", grid=(N_Q_HEADS,),
        in_specs=[pl.BlockSpec(memory_space=pltpu.SMEM), _whole(bucket.shape)],
        out_specs=pl.BlockSpec((None, BLOCK, 2 * BLOCK), lambda h: (h, 0, 0)),
        out_shape=jax.ShapeDtypeStruct((N_Q_HEADS, BLOCK, 2 * BLOCK), F32),
        compiler_params=_params(("arbitrary",)),
    )(rel_bias_table, bucket)


def _bias_grad_body(g_ref, bucket_ref, o_ref):
    bucket = bucket_ref[...]
    lane = lax.broadcasted_iota(jnp.int32, (N_BUCKETS, 128), 1)

    def head(h, out):
        g = g_ref[h]
        rows = [jnp.sum(jnp.where(bucket == b, g, 0.0), axis=0, keepdims=True) for b in range(N_BUCKETS)]
        colsum = jnp.sum(jnp.concatenate(rows, axis=0), axis=1, keepdims=True)
        return jnp.where(lane == h, colsum, out)

    o_ref[...] = lax.fori_loop(0, N_Q_HEADS, head, jnp.zeros((N_BUCKETS, 128), F32))


def _bias_grad(g_bias, bucket):
    out = pl.pallas_call(
        functools.partial(_bias_grad_body), name="bias_grad", grid=(1,),
        in_specs=[_whole(g_bias.shape), _whole(bucket.shape)],
        out_specs=_whole((N_BUCKETS, 128)),
        out_shape=jax.ShapeDtypeStruct((N_BUCKETS, 128), F32),
        compiler_params=_params(("arbitrary",)),
    )(g_bias, bucket)
    return out[:, :N_Q_HEADS]


def _head_logits(qh, kk, bias_h, first_block):
    s = lax.dot_general(qh, kk, _DIMS["nt"], preferred_element_type=F32) * ATTN_SCALE + bias_h
    col = lax.broadcasted_iota(jnp.int32, s.shape, 1)
    return jnp.where(jnp.logical_and(first_block, col < BLOCK), NEG_INF, s)


def _attn_fwd_body(sink_ref, q_ref, kp_ref, kc_ref, vp_ref, vc_ref, bias_ref, o_ref, lse_ref):
    n = pl.program_id(0)
    outs, lses = [], []
    for kv in range(N_KV_HEADS):
        cs = slice(kv * HEAD_DIM, (kv + 1) * HEAD_DIM)
        kk = jnp.concatenate([kp_ref[:, cs], kc_ref[:, cs]], axis=0).astype(BF16)
        vv = jnp.concatenate([vp_ref[:, cs], vc_ref[:, cs]], axis=0).astype(BF16)
        for g in range(Q_PER_KV):
            h = kv * Q_PER_KV + g
            qh = q_ref[:, h * HEAD_DIM:(h + 1) * HEAD_DIM].astype(BF16)
            s = _head_logits(qh, kk, bias_ref[h], n == 0)
            sink = sink_ref[0, h]
            m = jnp.maximum(jnp.max(s, axis=1, keepdims=True), sink)
            p = jnp.exp(s - m)
            den = jnp.sum(p, axis=1, keepdims=True) + jnp.exp(sink - m)
            p = p / den
            outs.append(jnp.dot(p.astype(BF16), vv, preferred_element_type=F32))
            lses.append(m + jnp.log(den))
    o_ref[...] = jnp.concatenate(outs, axis=1).astype(o_ref.dtype)
    lse_ref[...] = jnp.concatenate(lses, axis=1)


def _attn_specs(nb):
    prev = lambda n: jnp.maximum(jnp.minimum(n, nb - 1) - 1, 0)
    cur = lambda n: jnp.minimum(n, nb - 1)
    return [pl.BlockSpec((BLOCK, D_ATTN), lambda n: (cur(n), C_Q // D_ATTN)),
            pl.BlockSpec((BLOCK, D_KV), lambda n: (prev(n), C_K // D_KV)),
            pl.BlockSpec((BLOCK, D_KV), lambda n: (cur(n), C_K // D_KV)),
            pl.BlockSpec((BLOCK, D_KV), lambda n: (prev(n), C_V // D_KV)),
            pl.BlockSpec((BLOCK, D_KV), lambda n: (cur(n), C_V // D_KV))]


def _attn_forward(proj, sinks, bias):
    t = proj.shape[0]
    nb = t // BLOCK
    return pl.pallas_call(
        functools.partial(_attn_fwd_body), name="attn_fwd", grid=(nb,),
        in_specs=[pl.BlockSpec(memory_space=pltpu.SMEM)] + _attn_specs(nb) + [_whole(bias.shape)],
        out_specs=[pl.BlockSpec((BLOCK, D_ATTN), lambda n: (n, 0)),
                   pl.BlockSpec((BLOCK, N_Q_HEADS), lambda n: (n, 0))],
        out_shape=[jax.ShapeDtypeStruct((t, D_ATTN), BF16), jax.ShapeDtypeStruct((t, N_Q_HEADS), F32)],
        compiler_params=_params(("parallel",)),
    )(sinks, proj, proj, proj, proj, proj, bias)


def _attn_bwd_body(sink_ref, q_ref, kp_ref, kc_ref, vp_ref, vc_ref, bias_ref, do_ref, o_ref, lse_ref, dproj_in,
                   dq_ref, dkv_ref, gbias_ref, gsink_ref, carry_ref, *, nb):
    del dproj_in
    n = pl.program_id(0)

    @pl.when(n == 0)
    def _():
        gbias_ref[...] = jnp.zeros_like(gbias_ref)
        gsink_ref[...] = jnp.zeros_like(gsink_ref)
        carry_ref[...] = jnp.zeros_like(carry_ref)

    @pl.when(n < nb)
    def _():
        lane = lax.broadcasted_iota(jnp.int32, (1, 128), 1)
        dqs, dks, dvs = [], [], []
        gsink = jnp.zeros((1, 128), F32)
        for kv in range(N_KV_HEADS):
            cs = slice(kv * HEAD_DIM, (kv + 1) * HEAD_DIM)
            kk = jnp.concatenate([kp_ref[:, cs], kc_ref[:, cs]], axis=0).astype(BF16)
            vv = jnp.concatenate([vp_ref[:, cs], vc_ref[:, cs]], axis=0).astype(BF16)
            dk = jnp.zeros((2 * BLOCK, HEAD_DIM), F32)
            dv = jnp.zeros((2 * BLOCK, HEAD_DIM), F32)
            for g in range(Q_PER_KV):
                h = kv * Q_PER_KV + g
                hs = slice(h * HEAD_DIM, (h + 1) * HEAD_DIM)
                qh = q_ref[:, hs].astype(BF16)
                s = _head_logits(qh, kk, bias_ref[h], n == 0)
                lse = lse_ref[:, h:h + 1]
                p = jnp.exp(s - lse)
                do = do_ref[:, hs].astype(F32)
                delta = jnp.sum(do * o_ref[:, hs].astype(F32), axis=1, keepdims=True)
                dob = do.astype(BF16)
                dp = lax.dot_general(dob, vv, _DIMS["nt"], preferred_element_type=F32)
                dl = p * (dp - delta)
                gbias_ref[h] += dl
                psink = jnp.exp(sink_ref[0, h] - lse)
                gsink = gsink + jnp.where(lane == h, -jnp.sum(psink * delta), 0.0)
                dlb = dl.astype(BF16)
                dqs.append(jnp.dot(dlb, kk, preferred_element_type=F32) * ATTN_SCALE)
                dk = dk + lax.dot_general(dlb, qh, _DIMS["tn"], preferred_element_type=F32) * ATTN_SCALE
                dv = dv + lax.dot_general(p.astype(BF16), dob, _DIMS["tn"], preferred_element_type=F32)
            dks.append(dk)
            dvs.append(dv)
        dq_ref[...] = jnp.concatenate(dqs, axis=1).astype(dq_ref.dtype)
        gsink_ref[...] += gsink
        dkv = jnp.concatenate(dks + dvs, axis=1)
        dkv_ref[...] = (carry_ref[...] + dkv[:BLOCK]).astype(dkv_ref.dtype)
        carry_ref[...] = dkv[BLOCK:]

    @pl.when(n == nb)
    def _():
        dkv_ref[...] = carry_ref[...].astype(dkv_ref.dtype)


def _attn_backward(proj, sinks, bias, d_attn, attn, lse, dproj):
    t = proj.shape[0]
    nb = t // BLOCK
    cur = lambda n: jnp.minimum(n, nb - 1)
    return pl.pallas_call(
        functools.partial(_attn_bwd_body, nb=nb), name="attn_bwd", grid=(nb + 1,),
        in_specs=[pl.BlockSpec(memory_space=pltpu.SMEM)] + _attn_specs(nb) + [
            _whole(bias.shape),
            pl.BlockSpec((BLOCK, D_ATTN), lambda n: (cur(n), 0)),
            pl.BlockSpec((BLOCK, D_ATTN), lambda n: (cur(n), 0)),
            pl.BlockSpec((BLOCK, N_Q_HEADS), lambda n: (cur(n), 0)),
            pl.BlockSpec(memory_space=pl.ANY)],
        out_specs=[pl.BlockSpec((BLOCK, D_ATTN), lambda n: (cur(n), C_Q // D_ATTN)),
                   pl.BlockSpec((BLOCK, 2 * D_KV), lambda n: (jnp.maximum(n - 1, 0), 0)),
                   _whole(bias.shape), _whole((1, 128))],
        out_shape=[jax.ShapeDtypeStruct(dproj.shape, dproj.dtype), jax.ShapeDtypeStruct((t, 2 * D_KV), dproj.dtype),
                   jax.ShapeDtypeStruct(bias.shape, F32), jax.ShapeDtypeStruct((1, 128), F32)],
        scratch_shapes=[pltpu.VMEM((BLOCK, 2 * D_KV), F32)],
        input_output_aliases={10: 0},
        compiler_params=_params(("arbitrary",)),
    )(sinks, proj, proj, proj, proj, proj, bias, d_attn, attn, lse, dproj)


ROWS = 512


def _rowwise(body, name, t, ins, outs, aliases=None):
    rows = min(ROWS, t)

    def col_spec(w, c0):
        if c0 == "T":
            return pl.BlockSpec((w, rows), lambda i: (0, i))
        if c0 % w == 0:
            return pl.BlockSpec((rows, w), lambda i: (i, c0 // w))
        return pl.BlockSpec((pl.Element(rows), pl.Element(w)), lambda i: (i * rows, c0))

    in_specs, args = [], []
    for a, w, c0 in ins:
        args.append(a)
        if w is None:
            in_specs.append(pl.BlockSpec(memory_space=pl.ANY) if c0 == "any" else _whole(a.shape))
        else:
            in_specs.append(col_spec(w, c0))
    out_specs, out_shape = [], []
    for shape, dtype, w, c0 in outs:
        out_shape.append(jax.ShapeDtypeStruct(shape, dtype))
        out_specs.append(_whole(shape) if w is None else col_spec(w, c0))
    accum = any(o[2] is None for o in outs)
    return pl.pallas_call(
        body, name=name, grid=(t // rows,), in_specs=in_specs, out_specs=out_specs, out_shape=out_shape,
        input_output_aliases=aliases or {},
        compiler_params=_params(("arbitrary",) if accum else ("parallel",)),
    )(*args)


def _f32(ref, *idx):
    return (ref[idx] if idx else ref[...]).astype(F32)


def _store_with_transpose(val, ref, t_ref):
    val = val.astype(ref.dtype)
    ref[...] = val
    t_ref[...] = val.T


def _ssm_gate_fwd_body(glu_ref, z_ref, h_ref, ht_ref):
    a, b = _f32(glu_ref, slice(None), slice(0, D_SSM)), _f32(glu_ref, slice(None), slice(D_SSM, None))
    _store_with_transpose((a * _sigmoid(b)) * _silu_and_grad(_f32(z_ref))[0], h_ref, ht_ref)


def _ssm_gate_bwd_body(dh_ref, glu_ref, z_ref, dproj_in, dglu_ref, dz_ref):
    del dproj_in
    a, b = _f32(glu_ref, slice(None), slice(0, D_SSM)), _f32(glu_ref, slice(None), slice(D_SSM, None))
    sb = _sigmoid(b)
    silu, dsilu = _silu_and_grad(_f32(z_ref))
    dh = _f32(dh_ref)
    dg = dh * silu
    dz_ref[...] = (dh * (a * sb) * dsilu).astype(dz_ref.dtype)
    dglu_ref[:, :D_SSM] = (dg * sb).astype(dglu_ref.dtype)
    dglu_ref[:, D_SSM:] = (dg * a * sb * (1.0 - sb)).astype(dglu_ref.dtype)


def _attn_gate_fwd_body(attn_ref, z_ref, h_ref, ht_ref):
    _store_with_transpose(_f32(attn_ref) * _silu_and_grad(_f32(z_ref))[0], h_ref, ht_ref)


def _attn_gate_bwd_body(dh_ref, attn_ref, z_ref, dproj_in, dattn_ref, dz_ref):
    del dproj_in
    silu, dsilu = _silu_and_grad(_f32(z_ref))
    dh = _f32(dh_ref)
    dattn_ref[...] = (dh * silu).astype(dattn_ref.dtype)
    dz_ref[...] = (dh * _f32(attn_ref) * dsilu).astype(dz_ref.dtype)


def _merge_fwd_body(bs_ref, ba_ref, gl_ref, m_ref, mt_ref):
    gs = _sigmoid(_f32(gl_ref, slice(None), slice(0, D_MODEL)))
    ga = _sigmoid(_f32(gl_ref, slice(None), slice(D_MODEL, None)))
    _store_with_transpose(gs * _f32(bs_ref) + ga * _f32(ba_ref), m_ref, mt_ref)


def _merge_bwd_body(dm_ref, bs_ref, ba_ref, gl_ref, dbs_ref, dba_ref, dgl_ref):
    gs = _sigmoid(_f32(gl_ref, slice(None), slice(0, D_MODEL)))
    ga = _sigmoid(_f32(gl_ref, slice(None), slice(D_MODEL, None)))
    dm = _f32(dm_ref)
    dbs_ref[...] = (dm * gs).astype(dbs_ref.dtype)
    dba_ref[...] = (dm * ga).astype(dba_ref.dtype)
    dgl_ref[:, :D_MODEL] = (dm * _f32(bs_ref) * gs * (1.0 - gs)).astype(dgl_ref.dtype)
    dgl_ref[:, D_MODEL:] = (dm * _f32(ba_ref) * ga * (1.0 - ga)).astype(dgl_ref.dtype)


def _ln_loss_body(x_ref, o_ref, tgt_ref, gain_ref, bias_ref, dr_ref, loss_ref, dgain_ref, dbias_ref):
    @pl.when(pl.program_id(0) == 0)
    def _():
        loss_ref[...] = jnp.zeros_like(loss_ref)
        dgain_ref[...] = jnp.zeros_like(dgain_ref)
        dbias_ref[...] = jnp.zeros_like(dbias_ref)

    r = ALPHA * x_ref[...] + o_ref[...].astype(F32)
    mu = jnp.mean(r, axis=1, keepdims=True)
    rc = r - mu
    var = jnp.mean(rc * rc, axis=1, keepdims=True)
    rstd = lax.rsqrt(var + LN_EPS)
    xhat = rc * rstd
    gain = gain_ref[...]
    err = xhat * gain + bias_ref[...] - tgt_ref[...]
    loss_ref[...] += 0.5 * jnp.sum(jnp.mean(err * err, axis=1, keepdims=True), axis=0, keepdims=True)
    dy = err * (1.0 / D_MODEL)
    dgain_ref[...] += jnp.sum(dy * xhat, axis=0, keepdims=True)
    dbias_ref[...] += jnp.sum(dy, axis=0, keepdims=True)
    dxhat = dy * gain
    m1 = jnp.mean(dxhat, axis=1, keepdims=True)
    m2 = jnp.mean(dxhat * xhat, axis=1, keepdims=True)
    dr_ref[...] = (rstd * (dxhat - m1 - xhat * m2)).astype(dr_ref.dtype)


def _place_body(piece_ref, dproj_in, o_ref):
    del dproj_in
    o_ref[...] = piece_ref[...]


def _adamw_update(w_ref, m_ref, v_ref, g, g_ref, d_ref, nm_ref, nv_ref):
    m = ADAM_B1 * m_ref[...] + (1.0 - ADAM_B1) * g
    v = ADAM_B2 * v_ref[...] + (1.0 - ADAM_B2) * (g * g)
    m_hat = m / (1.0 - ADAM_B1 ** ADAM_STEP)
    v_hat = v / (1.0 - ADAM_B2 ** ADAM_STEP)
    g_ref[...] = g
    d_ref[...] = -ADAM_LR * (m_hat / (jnp.sqrt(v_hat) + ADAM_EPS) + ADAM_WD * w_ref[...])
    nm_ref[...] = m
    nv_ref[...] = v


def _adamw_body(*refs, n_parts):
    w_ref, m_ref, v_ref = refs[:3]
    parts = refs[3:3 + n_parts]
    g = parts[0][...].astype(F32)
    for p in parts[1:]:
        g = g + p[...].astype(F32)
    _adamw_update(w_ref, m_ref, v_ref, g, *refs[3 + n_parts:])


def _adamw_shard_body(c_ref, w_ref, m_ref, v_ref, mine_ref, other_ref, g_ref, d_ref, nm_ref, nv_ref, *, nth):
    in_mine = pl.program_id(0) // nth == c_ref[0]
    g = jnp.where(in_mine, mine_ref[...], other_ref[...])
    _adamw_update(w_ref, m_ref, v_ref, g, g_ref, d_ref, nm_ref, nv_ref)


def _adamw_shard(w, m, v, halves, core, *, name, rows):
    shape = w.shape
    w2, m2, v2 = (a.reshape(-1, shape[-1]) for a in (w, m, v))
    r, c = w2.shape
    nth = r // 2 // rows
    assert 2 * nth * rows == r and halves[0].shape == (r // 2, c)
    spec = pl.BlockSpec((rows, c), lambda i, cr: (i, 0))
    mine_spec = pl.BlockSpec((rows, c), lambda i, cr: (jnp.where(i // nth == cr[0], i % nth, 0), 0))
    other_spec = pl.BlockSpec((rows, c), lambda i, cr: (jnp.where(i // nth == cr[0], 0, i % nth), 0))
    outs = pl.pallas_call(
        functools.partial(_adamw_shard_body, nth=nth), name=name,
        grid_spec=pltpu.PrefetchScalarGridSpec(
            num_scalar_prefetch=1, grid=(r // rows,),
            in_specs=[spec] * 3 + [mine_spec, other_spec], out_specs=[spec] * 4),
        out_shape=[jax.ShapeDtypeStruct((r, c), F32)] * 4,
        compiler_params=_params(("parallel",)),
    )(core, w2, m2, v2, *halves)
    return tuple(o.reshape(shape) for o in outs)


def _adamw(w, m, v, parts, *, name, rows=256):
    shape = w.shape
    w2, m2, v2 = (a.reshape(-1, shape[-1]) for a in (w, m, v))
    parts = [p.reshape(w2.shape) for p in parts]
    r, c = w2.shape
    rows = rows if r % rows == 0 else r
    spec = pl.BlockSpec((rows, c), lambda i: (i, 0))
    outs = pl.pallas_call(
        functools.partial(_adamw_body, n_parts=len(parts)), name=name, grid=(r // rows,),
        in_specs=[spec] * (3 + len(parts)), out_specs=[spec] * 4,
        out_shape=[jax.ShapeDtypeStruct((r, c), F32)] * 4,
        compiler_params=_params(("parallel",)),
    )(w2, m2, v2, *parts)
    return tuple(o.reshape(shape) for o in outs)


BIG = ("w_in", "w_glu", "w_branch_ssm", "w_branch_attn", "w_out")
SHARD_AXIS = dict(w_in=1, w_glu=1, w_branch_ssm=1, w_branch_attn=1, w_out=0)
HBM = pl.BlockSpec(memory_space=pl.ANY)


def _position():
    x, y, c = lax.axis_index("x"), lax.axis_index("y"), lax.axis_index("c")
    other_chips = [(1 - x, y), (x, 1 - y), (1 - x, 1 - y)]
    return x, y, c, other_chips


def _window(ref, axis, shard, n_shards, half=None):
    rows, cols = ref.shape[-2:]
    sel = [slice(None), slice(None)]
    size = ref.shape[-2 + axis] // n_shards
    sel[axis] = pl.ds(pl.multiple_of(shard * size, 128), size)
    if half is not None:
        hsize = ref.shape[-1 - axis] // 2
        sel[1 - axis] = pl.ds(pl.multiple_of(half * hsize, 128), hsize)
    return ref.at[tuple(sel)]


def _half(ref, axis, half):
    hsize = ref.shape[-1 - axis] // 2
    sel = [slice(None), slice(None)]
    sel[1 - axis] = pl.ds(pl.multiple_of(half * hsize, 128), hsize)
    return ref.at[tuple(sel)]


def _remote(src, dst, send_sem, recv_sem, device):
    return pltpu.make_async_remote_copy(src_ref=src, dst_ref=dst, send_sem=send_sem, recv_sem=recv_sem,
                                        device_id=device, device_id_type=MESH)


def _full_shapes(shards, names):
    out = []
    for k in names:
        s = list(shards[k].shape)
        s[SHARD_AXIS[k]] *= N_CHIPS
        out.append(jax.ShapeDtypeStruct(tuple(s), shards[k].dtype))
    return out


def _gather_exchange(shards, names):
    axes = tuple(SHARD_AXIS[k] for k in names)
    n = len(names)

    def copies(ins, outs, sems):
        send_sems, recv_sems, local_sems = sems
        x, y, c, chips = _position()
        me = 2 * x + y
        remote, local = [], []
        for w, ax in enumerate(axes):
            for r, (px, py) in enumerate(chips):
                remote.append(_remote(_half(ins[w], ax, c), _window(outs[w], ax, me, N_CHIPS, c),
                                      send_sems.at[3 * w + r], recv_sems.at[3 * w + r], (px, py, c)))
            local.append(pltpu.make_async_copy(ins[w], _window(outs[w], ax, me, N_CHIPS), local_sems.at[w]))
        return remote, local

    return _Exchange([shards[k] for k in names], _full_shapes(shards, names), (3 * n, 3 * n, n), copies)


SEM = pl.BlockSpec(memory_space=pltpu.SEMAPHORE)


def _gather_start_body(shard_ref, land_ref, send_sems, recv_sems, shard_thru, land_thru, token, *, axis):
    del shard_thru, land_thru
    x, y, c, chips = _position()
    me = 2 * x + y
    for r, (px, py) in enumerate(chips):
        _remote(_half(shard_ref, axis, c), _window(land_ref, axis, me, N_CHIPS, c),
                send_sems.at[r], recv_sems.at[r], (px, py, c)).start()
    token[...] = jnp.zeros_like(token)


def _gather_start(shard, name):
    axis = SHARD_AXIS[name]
    full = _full_shapes({name: shard}, (name,))[0]
    return pl.pallas_call(
        functools.partial(_gather_start_body, axis=axis), name="gather_start_" + name,
        out_shape=(pltpu.SemaphoreType.DMA((3,)), pltpu.SemaphoreType.DMA((3,)), pltpu.HBM(shard.shape, shard.dtype),
                   pltpu.HBM(full.shape, full.dtype), jax.ShapeDtypeStruct((8, 128), F32)),
        in_specs=(HBM, HBM), out_specs=(SEM, SEM, HBM, HBM, pl.BlockSpec(memory_space=pltpu.VMEM)),
        input_output_aliases={0: 2, 1: 3},
        compiler_params=pltpu.CompilerParams(has_side_effects=pltpu.SideEffectType.DATAFLOW_SIDE_EFFECTING),
    )(pltpu.with_memory_space_constraint(shard, pltpu.HBM),
      pltpu.with_memory_space_constraint(lax.empty(full.shape, full.dtype), pltpu.HBM))


def _gather_wait_body(shard_ref, land_ref, send_sems, recv_sems, after_ref, shard_dead, landed, *, axis):
    del after_ref, shard_dead, landed
    x, y, c, chips = _position()
    for r, (px, py) in enumerate(chips):
        cp = _remote(_half(shard_ref, axis, c), _window(land_ref, axis, 2 * px + py, N_CHIPS, c),
                     send_sems.at[r], recv_sems.at[r], (px, py, c))
        cp.wait_send()
        cp.wait_recv()


def _gather_wait(send_sems, recv_sems, shard_thru, land_thru, after, name):
    return pl.pallas_call(
        functools.partial(_gather_wait_body, axis=SHARD_AXIS[name]), name="gather_wait_" + name,
        out_shape=(pltpu.HBM(shard_thru.shape, shard_thru.dtype), pltpu.HBM(land_thru.shape, land_thru.dtype)),
        in_specs=(HBM, HBM, SEM, SEM, HBM), out_specs=(HBM, HBM), input_output_aliases={0: 0, 1: 1},
        compiler_params=pltpu.CompilerParams(has_side_effects=pltpu.SideEffectType.DATAFLOW_SIDE_EFFECTING),
    )(shard_thru, land_thru, send_sems, recv_sems, after)[1]


def _pass_on_body(*refs, axes, n_own):
    n = len(axes)
    ins, own = refs[:n], refs[n:n + n_own]
    refs = refs[:n] + refs[n + n_own:]
    outs = refs[n:2 * n]
    sbuf, rbuf = refs[2 * n:5 * n], refs[5 * n:8 * n]
    obuf = refs[8 * n:8 * n + n_own]
    send_sems, recv_sems, load_sems, store_sems, own_sems, placed_sems = refs[8 * n + n_own:]
    x, y, c, chips = _position()
    fetched = [pltpu.make_async_copy(own[w], obuf[w], own_sems.at[w]) for w in range(n_own)]
    for cp in fetched:
        cp.start()
    region = lambda ref, w, r, half: _window(ref, axes[w], 2 * chips[r][0] + chips[r][1], N_CHIPS, half)
    pairs = [(w, r) for w in range(n) for r in range(3)]
    loads = [pltpu.make_async_copy(region(ins[w], w, r, c), sbuf[3 * w + r], load_sems.at[3 * w + r]) for w, r in pairs]
    for cp in loads:
        cp.start()
    placed = []
    for w, cp in enumerate(fetched):
        cp.wait()
        placed.append(pltpu.make_async_copy(obuf[w], _window(outs[w], axes[w], 2 * x + y, N_CHIPS), placed_sems.at[w]))
        placed[-1].start()
    sends = []
    for i, cp in enumerate(loads):
        cp.wait()
        sends.append(_remote(sbuf[i], rbuf[i], send_sems.at[i], recv_sems.at[i], (x, y, 1 - c)))
        sends[-1].start()
    stores = []
    for i, (w, r) in enumerate(pairs):
        sends[i].wait_recv()
        stores.append(pltpu.make_async_copy(rbuf[i], region(outs[w], w, r, 1 - c), store_sems.at[i]))
        stores[-1].start()
    for cp in sends:
        cp.wait_send()
    for cp in stores + placed:
        cp.wait()


def _pass_on(fulls, names, own=()):
    axes = tuple(SHARD_AXIS[k] for k in names)
    n = len(names)
    assert len(own) in (0, n)
    bufs = []
    for a, ax in zip(fulls, axes):
        s = list(a.shape)
        s[ax] //= N_CHIPS
        s[1 - ax] //= 2
        bufs += [pltpu.VMEM(tuple(s), a.dtype)] * 3
    return pl.pallas_call(
        functools.partial(_pass_on_body, axes=axes, n_own=len(own)), name="pass_on_" + names[0],
        in_specs=[HBM] * (n + len(own)), out_specs=[HBM] * n,
        out_shape=[jax.ShapeDtypeStruct(a.shape, a.dtype) for a in fulls],
        scratch_shapes=bufs + bufs + [pltpu.VMEM(o.shape, o.dtype) for o in own]
        + [pltpu.SemaphoreType.DMA((3 * n,))] * 4 + [pltpu.SemaphoreType.DMA((n,))] * 2,
        input_output_aliases={i: i for i in range(n)},
        compiler_params=pltpu.CompilerParams(vmem_limit_bytes=VMEM_LIMIT),
    )(*fulls, *own)


def _chip_exchange(pairs, names):
    axes = tuple(SHARD_AXIS[k] for k in names)
    n = len(names)
    out_shape = []
    for p, ax in zip(pairs, axes):
        s = list(p.shape)
        s[ax] //= N_CHIPS
        out_shape.append(jax.ShapeDtypeStruct((4, *s), p.dtype))

    def copies(ins, outs, sems):
        send_sems, recv_sems, local_sems = sems
        x, y, c, chips = _position()
        me = 2 * x + y
        remote, local = [], []
        for w, ax in enumerate(axes):
            for r, (px, py) in enumerate(chips):
                remote.append(_remote(_window(ins[w], ax, 2 * px + py, N_CHIPS), outs[w].at[r],
                                      send_sems.at[3 * w + r], recv_sems.at[3 * w + r], (px, py, c)))
            local.append(pltpu.make_async_copy(_window(ins[w], ax, me, N_CHIPS), outs[w].at[3], local_sems.at[w]))
        return remote, local

    return _Exchange(pairs, out_shape, (3 * n, 3 * n, n), copies)


def _half_tile(n, h, nt, axis):
    return h * nt + n if axis == 1 else 2 * n + h


def _sibling_stream(n, nt, stage, recv, send_sems, recv_sems, credit, produce, consume):
    x, y, c, _ = _position()
    sibling = (x, y, 1 - c)

    def copy(slot):
        return _remote(stage.at[slot], recv.at[slot], send_sems.at[slot], recv_sems.at[slot], sibling)

    @pl.when(n < nt)
    def _():
        slot = n % 2

        @pl.when(n >= 2)
        def _():
            copy(slot).wait_send()
            pl.semaphore_wait(credit, 1)

        stage[slot] = produce().astype(stage.dtype)
        copy(slot).start()

    @pl.when(n >= 1)
    def _():
        slot = (n - 1) % 2
        copy(slot).wait_recv()
        consume(recv[slot])

        @pl.when(n - 1 < nt - 2)
        def _():
            pl.semaphore_signal(credit, 1, device_id=sibling, device_id_type=MESH)

    @pl.when(n == nt)
    def _():
        for slot in range(min(2, nt)):
            copy(slot).wait_send()


def _pair_reduce_body(c_ref, mine_ref, theirs_ref, out_ref, stage, recv, send_sems, recv_sems, credit, *, nt):
    del c_ref

    def consume(got):
        out_ref[...] = (mine_ref[...] + got.astype(F32)).astype(out_ref.dtype)

    _sibling_stream(pl.program_id(0), nt, stage, recv, send_sems, recv_sems, credit,
                    lambda: theirs_ref[...], consume)


def _pair_reduce(grad, core, axis, *, name, rows):
    r, c = grad.shape
    nt = r // 2 // rows
    assert nt * rows * 2 == r and (axis == 1 or rows == r // (2 * N_CHIPS))
    tile = lambda n, h: _half_tile(n, h, nt, axis)
    return pl.pallas_call(
        functools.partial(_pair_reduce_body, nt=nt), name=name,
        grid_spec=pltpu.PrefetchScalarGridSpec(
            num_scalar_prefetch=1, grid=(nt + 1,),
            in_specs=[pl.BlockSpec((rows, c), lambda n, cr: (tile(jnp.maximum(n - 1, 0), cr[0]), 0)),
                      pl.BlockSpec((rows, c), lambda n, cr: (tile(jnp.minimum(n, nt - 1), 1 - cr[0]), 0))],
            out_specs=pl.BlockSpec((rows, c), lambda n, cr: (jnp.maximum(n - 1, 0), 0)),
            scratch_shapes=[pltpu.VMEM((2, rows, c), BF16), pltpu.VMEM((2, rows, c), BF16),
                            pltpu.SemaphoreType.DMA((2,)), pltpu.SemaphoreType.DMA((2,)),
                            pltpu.SemaphoreType.REGULAR]),
        out_shape=jax.ShapeDtypeStruct((r // 2, c), BF16),
        compiler_params=_params(("arbitrary",)),
    )(core, grad, grad)


def _chip_add_share_body(c_ref, s0, s1, s2, s3, mine_ref, other_ref, stage, recv, send_sems, recv_sems, credit, *, nt):
    del c_ref

    def produce():
        total = s3[...].astype(F32) + s0[...].astype(F32) + s1[...].astype(F32) + s2[...].astype(F32)
        mine_ref[...] = total
        return total

    def consume(got):
        other_ref[...] = got

    _sibling_stream(pl.program_id(0), nt, stage, recv, send_sems, recv_sems, credit, produce, consume)


def _chip_add_share(slots, core, *, name, rows):
    _, r, c = slots.shape
    nt = r // rows
    assert nt * rows == r
    send = lambda j: pl.BlockSpec((None, rows, c), lambda n, cr: (j, jnp.minimum(n, nt - 1), 0))
    return pl.pallas_call(
        functools.partial(_chip_add_share_body, nt=nt), name=name,
        grid_spec=pltpu.PrefetchScalarGridSpec(
            num_scalar_prefetch=1, grid=(nt + 1,),
            in_specs=[send(j) for j in range(4)],
            out_specs=[pl.BlockSpec((rows, c), lambda n, cr: (jnp.minimum(n, nt - 1), 0)),
                       pl.BlockSpec((rows, c), lambda n, cr: (jnp.maximum(n - 1, 0), 0))],
            scratch_shapes=[pltpu.VMEM((2, rows, c), F32), pltpu.VMEM((2, rows, c), F32),
                            pltpu.SemaphoreType.DMA((2,)), pltpu.SemaphoreType.DMA((2,)),
                            pltpu.SemaphoreType.REGULAR]),
        out_shape=[jax.ShapeDtypeStruct((r, c), F32)] * 2,
        compiler_params=_params(("arbitrary",)),
    )(core, slots, slots, slots, slots)


PAIR_ROWS = dict(w_in=128,w_glu=128, w_branch_ssm=128, w_branch_attn=128, w_out=256)
SHARE_ROWS = dict(w_in=128, w_glu=128, w_branch_ssm=128, w_branch_attn=128, w_out=64)


def _pair_sums(grads, names, core):
    return [_pair_reduce(grads[k], core, SHARD_AXIS[k], name="pair_reduce_" + k, rows=PAIR_ROWS[k]) for k in names]


def _shard_halves(slots, names, core):
    return {k: _chip_add_share(s, core, name="chip_add_share_" + k, rows=SHARE_ROWS[k]) for k, s in zip(names, slots)}


SMALL = (("ssm_lambda_re", (1, 64, 64)), ("ssm_lambda_im", (1, 64, 64)), ("ssm_b_re", (1, 64, 64, 16)),
         ("ssm_b_im", (1, 64, 64, 16)), ("ssm_c_re", (1, 64, 16, 64)), ("ssm_c_im", (1, 64, 16, 64)),
         ("ssm_d", (1, 1024)), ("ssm_log_step", (1, 64)), ("attn_sinks", (1, 16)), ("rel_bias_table", (32, 16)),
         ("ln_gain", (1, 2048)), ("ln_bias", (1, 2048)))
SMALL_SIZE = sum(int(np.prod(s)) for _, s in SMALL)
PACK_ROWS = -(-(SMALL_SIZE + 1) // (8 * 128)) * 8


def _pack(values, extra=None):
    flat = [values[k].reshape(-1).astype(F32) for k, _ in SMALL]
    flat.append(jnp.zeros((1,), F32) if extra is None else extra.reshape(1))
    flat.append(jnp.zeros((PACK_ROWS * 128 - SMALL_SIZE - 1,), F32))
    return jnp.concatenate(flat).reshape(PACK_ROWS, 128)


def _unpack(packed):
    flat = packed.reshape(-1)
    out, off = {}, 0
    for k, s in SMALL:
        size = int(np.prod(s))
        out[k] = flat[off:off + size].reshape(s)
        off += size
    return out, flat[off]


def _small_exchange(packed):
    def copies(ins, outs, sems):
        send_sems, recv_sems, local_sems = sems
        x, y, c, _ = _position()
        remote = []
        for r in range(1, 8):
            peer = tuple(1 - v if (r >> s) & 1 else v for v, s in ((x, 2), (y, 1), (c, 0)))
            remote.append(_remote(ins[0], outs[0].at[r], send_sems.at[r - 1], recv_sems.at[r - 1], peer))
        return remote, [pltpu.make_async_copy(ins[0], outs[0].at[0], local_sems.at[0])]

    return _Exchange([packed], [jax.ShapeDtypeStruct((8, *packed.shape), F32)], (7, 7, 1), copies)


def _merge_exchanges(a, b):
    na, ma, sa = len(a.ins), len(a.out_shape), len(a.n_sems)

    def copies(ins, outs, sems):
        ra, la = a.copies(ins[:na], outs[:ma], sems[:sa])
        rb, lb = b.copies(ins[na:], outs[ma:], sems[sa:])
        return ra + rb, la + lb

    return _Exchange(a.ins + b.ins, a.out_shape + b.out_shape, a.n_sems + b.n_sems, copies)


def _small_sum_body(slots_ref, o_ref):
    x, y, c, _ = _position()
    me = 4 * x + 2 * y + c
    acc = slots_ref[me]
    for s in range(1, 8):
        acc = acc + slots_ref[jnp.bitwise_xor(me, s)]
    o_ref[...] = acc


def _small_sum(slots):
    vm = pl.BlockSpec(memory_space=pltpu.VMEM)
    return pl.pallas_call(
        functools.partial(_small_sum_body), name="small_sum",
        in_specs=[vm], out_specs=vm, out_shape=jax.ShapeDtypeStruct(slots.shape[1:], F32),
        compiler_params=pltpu.CompilerParams(vmem_limit_bytes=VMEM_LIMIT),
    )(slots)


LATE = BIG[1:]


def _local_step(x, target, shards, core, chip, lam_re, lam_im, b_re, b_im, c_re, c_im, d_skip,
                log_step, sinks, rel_bias_table, ln_gain, ln_bias):
    t = x.shape[0]
    send_sems, recv_sems, shard_thru, land_thru, token = _gather_start(shards["w_in"], "w_in")
    xb, xbt = _cast_and_transpose(x, token, name="cast_x")
    bucket = jnp.asarray(_bucket_table())
    p_re, p_im, b_bd, c_bd = _ssm_prepare(lam_re, lam_im, b_re, b_im, c_re, c_im, log_step)
    bias = _bias_build(rel_bias_table, bucket)

    act = dict(out_dtype=BF16)
    proj = _proj_shards(xb, shards["w_in"], chip, (0,), name="proj_own")
    landed = _gather_wait(send_sems, recv_sems, shard_thru, land_thru, proj, "w_in")
    (w_in,) = _pass_on([landed], BIG[:1], own=[shards["w_in"]])
    proj, landed = _proj_shards(xb, w_in, chip, (1, 2, 3), name="proj", base=proj,
                                exchange=_gather_exchange(shards, LATE))
    w_glu, w_bs, w_ba, w_out = _pass_on(list(landed), LATE)
    y_ssm, g_in, g_in_t, car_re, car_im = _ssm_forward(proj, p_re, p_im, b_bd, c_bd, d_skip)
    glu = _matmul(g_in, w_glu, "nn", name="glu", tm=1024, tn=2048, **act)
    h_ssm, h_ssm_t = _rowwise(functools.partial(_ssm_gate_fwd_body), "ssm_gate_fwd", t,
                              [(glu, 2 * D_SSM, 0), (proj, D_SSM, C_ZS)],
                              [((t, D_SSM), BF16, D_SSM, 0), ((D_SSM, t), BF16, D_SSM, "T")])
    attn, lse = _attn_forward(proj, sinks, bias)
    h_attn, h_attn_t = _rowwise(functools.partial(_attn_gate_fwd_body), "attn_gate_fwd", t,
                                [(attn, D_ATTN, 0), (proj, D_ATTN, C_ZA)],
                                [((t, D_ATTN), BF16, D_ATTN, 0), ((D_ATTN, t), BF16, D_ATTN, "T")])
    bs = _matmul(h_ssm, w_bs, "nn", name="branch_ssm", tm=1024, tn=2048, **act)
    ba = _matmul(h_attn, w_ba, "nn", name="branch_attn", tm=1024, tn=2048, **act)
    gl_in = (proj, 2 * D_MODEL, C_GL)
    merged, merged_t = _rowwise(functools.partial(_merge_fwd_body), "merge_fwd", t,
                                [(bs, D_MODEL, 0), (ba, D_MODEL, 0), gl_in],
                                [((t, D_MODEL), BF16, D_MODEL, 0), ((D_MODEL, t), BF16, D_MODEL, "T")])
    out = _matmul(merged, w_out, "nn", name="out_proj", tm=1024, tn=1024, **act)
    drb, loss, g_gain, g_bias = _rowwise(
        functools.partial(_ln_loss_body), "ln_loss", t,
        [(x, D_MODEL, 0), (out, D_MODEL, 0), (target, D_MODEL, 0), (ln_gain, None, 0), (ln_bias, None, 0)],
        [((t, D_MODEL), BF16, D_MODEL, 0), ((1, 1), F32, None, 0),
         ((1, D_MODEL), F32, None, 0), ((1, D_MODEL), F32, None, 0)])

    g_w_out = _matmul(merged_t, drb, "nn", name="g_w_out", tm=512, tn=512)
    d_merged = _matmul(drb, w_out, "nt", name="d_merged", tm=1024, tn=1024, **act)
    d_bs, d_ba, dproj = _rowwise(
        functools.partial(_merge_bwd_body), "merge_bwd", t,
        [(d_merged, D_MODEL, 0), (bs, D_MODEL, 0), (ba, D_MODEL, 0), gl_in],
        [((t, D_MODEL), BF16, D_MODEL, 0), ((t, D_MODEL), BF16, D_MODEL, 0),
         ((t, D_IN), BF16, 2 * D_MODEL, C_GL)])
    g_w_bs = _matmul(h_ssm_t, d_bs, "nn", name="g_w_branch_ssm", tm=512, tn=512)
    d_h_ssm = _matmul(d_bs, w_bs, "nt", name="d_h_ssm", tm=1024, tn=1024, **act)
    g_w_ba = _matmul(h_attn_t, d_ba, "nn", name="g_w_branch_attn", tm=512, tn=512)
    d_h_attn = _matmul(d_ba, w_ba, "nt", name="d_h_attn", tm=1024, tn=1024, **act)

    d_attn, dproj = _rowwise(
        functools.partial(_attn_gate_bwd_body), "attn_gate_bwd", t,
        [(d_h_attn, D_ATTN, 0), (attn, D_ATTN, 0), (proj, D_ATTN, C_ZA), (dproj, None, "any")],
        [((t, D_ATTN), BF16, D_ATTN, 0), ((t, D_IN), BF16, D_ATTN, C_ZA)], aliases={3: 1})
    dproj, d_kv, g_bias_full, g_sinks = _attn_backward(proj, sinks, bias, d_attn, attn, lse, dproj)
    (dproj,) = _rowwise(functools.partial(_place_body), "place_dkv", t,
                        [(d_kv, 2 * D_KV, 0), (dproj, None, "any")],
                        [((t, D_IN), BF16, 2 * D_KV, C_K)], aliases={1: 0})
    g_table = _bias_grad(g_bias_full, bucket)

    d_glu, dproj = _rowwise(
        functools.partial(_ssm_gate_bwd_body), "ssm_gate_bwd", t,
        [(d_h_ssm, D_SSM, 0), (glu, 2 * D_SSM, 0), (proj, D_SSM, C_ZS), (dproj, None, "any")],
        [((t, 2 * D_SSM), BF16, 2 * D_SSM, 0), ((t, D_IN), BF16, D_SSM, C_ZS)], aliases={3: 1})
    g_w_glu = _matmul(g_in_t, d_glu, "nn", name="g_w_glu", tm=512, tn=512)
    d_g_in = _matmul(d_glu, w_glu, "nt", name="d_g_in", tm=1024, tn=1024, **act)
    dproj, g_bbd, g_cbd, g_lam_re, g_lam_im, g_d = _ssm_backward(
        proj, d_g_in, y_ssm, car_re, car_im, p_re, p_im, b_bd, c_bd, d_skip, dproj)
    g_lr, g_li, g_br, g_bi, g_cr, g_ci, g_ls = _ssm_param_grads(
        lam_re, lam_im, b_re, b_im, log_step, g_lam_re, g_lam_im, g_bbd, g_cbd)

    late = dict(w_glu=g_w_glu, w_branch_ssm=g_w_bs, w_branch_attn=g_w_ba, w_out=g_w_out)
    g_w_in, late_slots = _matmul(xbt, dproj, "nn", name="g_w_in", tm=512, tn=512,
                                 exchange=_chip_exchange(_pair_sums(late, LATE, core), LATE))
    small = dict(ssm_lambda_re=g_lr, ssm_lambda_im=g_li, ssm_b_re=g_br, ssm_b_im=g_bi, ssm_c_re=g_cr,
                 ssm_c_im=g_ci, ssm_d=g_d, ssm_log_step=g_ls, attn_sinks=g_sinks[:, :N_Q_HEADS],
                 rel_bias_table=g_table, ln_gain=g_gain, ln_bias=g_bias)
    last = _merge_exchanges(_chip_exchange(_pair_sums(dict(w_in=g_w_in), BIG[:1], core), BIG[:1]),
                            _small_exchange(_pack(small, loss)))
    n_tiles = t // min(512, t)
    carried = max(n_tiles - max(n_tiles // 4, 1), 1) if n_tiles > 1 else 1
    grad_x, (in_slots, small_slots) = _matmul(dproj, w_in, "nt", name="grad_x", tm=512, tn=512, res=drb,
                                              res_scale=ALPHA, exchange=last, rows=(0, carried))
    if carried < n_tiles:
        grad_x = _matmul(dproj, w_in, "nt", name="grad_x_rest", tm=512, tn=512, res=drb, res_scale=ALPHA,
                         rows=(carried, n_tiles - carried), base=grad_x)
    big = {**_shard_halves([in_slots], BIG[:1], core), **_shard_halves(late_slots, LATE, core)}
    return grad_x, big, _small_sum(small_slots)


WEIGHTS = ("w_in", "ssm_lambda_re", "ssm_lambda_im", "ssm_b_re", "ssm_b_im", "ssm_c_re", "ssm_c_im", "ssm_d",
           "ssm_log_step", "w_glu", "attn_sinks", "rel_bias_table", "w_branch_ssm", "w_branch_attn", "w_out",
           "ln_gain", "ln_bias")


def kernel(x, w_in, ssm_lambda_re, ssm_lambda_im, ssm_b_re, ssm_b_im, ssm_c_re, ssm_c_im, ssm_d, ssm_log_step, w_glu, attn_sinks, rel_bias_table, w_branch_ssm, w_branch_attn, w_out, ln_gain, ln_bias, loss_target, m_w_in, m_ssm_lambda_re, m_ssm_lambda_im, m_ssm_b_re, m_ssm_b_im, m_ssm_c_re, m_ssm_c_im, m_ssm_d, m_ssm_log_step, m_w_glu, m_attn_sinks, m_rel_bias_table, m_w_branch_ssm, m_w_branch_attn, m_w_out, m_ln_gain, m_ln_bias, v_w_in, v_ssm_lambda_re, v_ssm_lambda_im, v_ssm_b_re, v_ssm_b_im, v_ssm_c_re, v_ssm_c_im, v_ssm_d, v_ssm_log_step, v_w_glu, v_attn_sinks, v_rel_bias_table, v_w_branch_ssm, v_w_branch_attn, v_w_out, v_ln_gain, v_ln_bias):
    given = dict(locals())
    w = {k: given[k] for k in WEIGHTS}
    m = {k: given["m_" + k] for k in WEIGHTS}
    v = {k: given["v_" + k] for k in WEIGHTS}

    core = lax.axis_index("c").astype(jnp.int32).reshape(1)
    shards = {k: _cast(w[k][0], BF16, name="cast_" + k) for k in BIG}
    chip = (2 * lax.axis_index("x") + lax.axis_index("y")).astype(jnp.int32).reshape(1)
    grad_x, g_shard, g_packed = _local_step(
        x[0], loss_target[0], shards, core, chip, ssm_lambda_re[0], ssm_lambda_im[0], ssm_b_re[0],
        ssm_b_im[0], ssm_c_re[0], ssm_c_im[0], ssm_d, ssm_log_step, attn_sinks, rel_bias_table, ln_gain, ln_bias)
    loss_sum = _unpack(g_packed)[1]

    grad, delta, new_m, new_v = {}, {}, {}, {}
    for k in BIG:
        grad[k], delta[k], new_m[k], new_v[k] = _adamw_shard(w[k], m[k], v[k], g_shard[k], core,
                                                             name="adamw_" + k, rows=SHARE_ROWS[k])
    gs, ds, ms, vs = _adamw(_pack(w), _pack(m), _pack(v), [g_packed], name="adamw_small")
    for dst, packed in ((grad, gs), (delta, ds), (new_m, ms), (new_v, vs)):
        dst.update(_unpack(packed)[0])

    return (loss_sum, grad_x[None], *[grad[k] for k in WEIGHTS], *[delta[k] for k in WEIGHTS],
            *[new_m[k] for k in WEIGHTS], *[new_v[k] for k in WEIGHTS])
```

```python
import functools
import math

import numpy as np
import jax
import jax.numpy as jnp
from jax import lax
from jax.experimental import pallas as pl
from jax.experimental.pallas import tpu as pltpu

F32 = jnp.float32
BF16 = jnp.bfloat16

D_MODEL = 2048
D_SSM = 1024
SSM_GROUP = 16
N_GROUPS = 64
SSM_STATE = 64
N_LANES = N_GROUPS * SSM_STATE
N_Q_HEADS = 16
N_KV_HEADS = 4
HEAD_DIM = 64
Q_PER_KV = 4
D_ATTN = 1024
D_KV = 256
WINDOW = 128
BLOCK = 128
N_BUCKETS = 32
MAX_DISTANCE = 128
D_IN = 8704
ALPHA = 2.0 ** 0.25
LN_EPS = 1e-5
NEG_INF = -1e30
ATTN_SCALE = HEAD_DIM ** -0.5

C_U, C_ZS, C_Q, C_K, C_V, C_ZA, C_GL = 0, 1024, 2048, 3072, 3328, 3584, 4608

ADAM_LR = 0.001
ADAM_B1 = 0.9
ADAM_B2 = 0.999
ADAM_EPS = 1e-08
ADAM_WD = 0.01
ADAM_STEP = 10

N_CHIPS = 4
MESH = pl.DeviceIdType.MESH

SSM_CHUNK = 256
SEG_LEN = SSM_CHUNK // 8
SLAB_LANES = 512
N_SLABS = N_LANES // SLAB_LANES
SLAB_CH = D_SSM // N_SLABS

VMEM_LIMIT = 60 * 1024 * 1024


def _params(sem=None, **kw):
    return pltpu.CompilerParams(dimension_semantics=sem, vmem_limit_bytes=VMEM_LIMIT, **kw)


_DIMS = {"nn": (((1,), (0,)), ((), ())), "nt": (((1,), (1,)), ((), ())), "tn": (((0,), (0,)), ((), ()))}


class _Exchange:
    def __init__(self, ins, out_shape, n_sems, copies):
        self.ins, self.out_shape, self.n_sems, self.copies = list(ins), list(out_shape), list(n_sems), copies

    def start(self, ins, outs, sems):
        remote, local = self.copies(ins, outs, sems)
        for cp in remote + local:
            cp.start()

    def finish(self, ins, outs, sems):
        remote, local = self.copies(ins, outs, sems)
        for cp in remote:
            cp.wait_recv()
        for cp in remote:
            cp.wait_send()
        for cp in local:
            cp.wait()


def _mm_body(*refs, dims, nk, res_scale, exchange, grid):
    n_ex_in = len(exchange.ins) if exchange else 0
    n_ex_out = len(exchange.out_shape) if exchange else 0
    n_in = 2 + (res_scale is not None) + n_ex_in
    a_ref, b_ref = refs[0], refs[1]
    r_ref = refs[2] if res_scale is not None else None
    ex_in = refs[n_in - n_ex_in:n_in]
    o_ref, ex_out = refs[n_in], refs[n_in + 1:n_in + 1 + n_ex_out]
    acc_ref, ex_sems = refs[n_in + 1 + n_ex_out], refs[n_in + 2 + n_ex_out:]
    i, j, k = pl.program_id(0), pl.program_id(1), pl.program_id(2)

    if exchange:
        @pl.when(jnp.logical_and(jnp.logical_and(i == 0, j == 0), k == 0))
        def _():
            exchange.start(ex_in, ex_out, ex_sems)

    part = lax.dot_general(a_ref[...].astype(BF16), b_ref[...].astype(BF16), _DIMS[dims],
                           preferred_element_type=F32)

    def finish(acc):
        if r_ref is not None:
            acc = acc + res_scale * r_ref[...].astype(F32)
        o_ref[...] = acc.astype(o_ref.dtype)

    if nk == 1:
        finish(part)
    else:
        @pl.when(k == 0)
        def _():
            acc_ref[...] = part

        @pl.when(k > 0)
        def _():
            acc_ref[...] += part

        @pl.when(k == nk - 1)
        def _():
            finish(acc_ref[...])

    if exchange:
        @pl.when(jnp.logical_and(jnp.logical_and(i == grid[0] - 1, j == grid[1] - 1), k == grid[2] - 1))
        def _():
            exchange.finish(ex_in, ex_out, ex_sems)


def _matmul(a, b, dims, *, name, out_dtype=F32, tm=512, tn=512, tk=None, res=None, res_scale=None, exchange=None):
    if dims == "nn":
        (m, kk), n = a.shape, b.shape[1]
    elif dims == "nt":
        (m, kk), n = a.shape, b.shape[0]
    else:
        (kk, m), n = a.shape, b.shape[1]
    tm, tn = min(tm, m), min(tn, n)
    tk = kk if tk is None else min(tk, kk)
    assert m % tm == 0 and n % tn == 0 and kk % tk == 0, (name, m, n, kk, tm, tn, tk)
    nk = kk // tk
    a_spec = {"nn": pl.BlockSpec((tm, tk), lambda i, j, k: (i, k)),
              "nt": pl.BlockSpec((tm, tk), lambda i, j, k: (i, k)),
              "tn": pl.BlockSpec((tk, tm), lambda i, j, k: (k, i))}[dims]
    b_spec = {"nn": pl.BlockSpec((tk, tn), lambda i, j, k: (k, j)),
              "nt": pl.BlockSpec((tn, tk), lambda i, j, k: (j, k)),
              "tn": pl.BlockSpec((tk, tn), lambda i, j, k: (k, j))}[dims]
    in_specs, args = [a_spec, b_spec], [a, b]
    if res is not None:
        in_specs.append(pl.BlockSpec((tm, tn), lambda i, j, k: (i, j)))
        args.append(res)
    grid = (m // tm, n // tn, nk)
    out_specs = [pl.BlockSpec((tm, tn), lambda i, j, k: (i, j))]
    out_shape = [jax.ShapeDtypeStruct((m, n), out_dtype)]
    scratch = [pltpu.VMEM((tm, tn), F32)]
    if exchange:
        in_specs += [pl.BlockSpec(memory_space=pl.ANY)] * len(exchange.ins)
        args += exchange.ins
        out_specs += [pl.BlockSpec(memory_space=pl.ANY)] * len(exchange.out_shape)
        out_shape += exchange.out_shape
        scratch += [pltpu.SemaphoreType.DMA((s,)) for s in exchange.n_sems]
    outs = pl.pallas_call(
        functools.partial(_mm_body, dims=dims, nk=nk, res_scale=res_scale if res is not None else None,
                          exchange=exchange, grid=grid),
        name=name, grid=grid, in_specs=in_specs, out_specs=out_specs, out_shape=out_shape, scratch_shapes=scratch,
        compiler_params=_params(("arbitrary",) * 3 if exchange else ("parallel", "parallel", "arbitrary")),
    )(*args)
    return (outs[0], outs[1:]) if exchange else outs[0]


def _proj_shards_body(chip_ref, *refs, n_ex_in, has_base, exchange, grid):
    del chip_ref
    refs = list(refs)
    if has_base:
        del refs[2 + n_ex_in]
    _mm_body(*refs, dims="nn", nk=1, res_scale=None, exchange=exchange, grid=grid)


def _proj_shards(xb, w, chip, offsets, *, name, base=None, exchange=None, tm=1024):
    t, d = xb.shape
    cs = D_IN // N_CHIPS
    own = w.shape[1] == cs
    tm = min(tm, t)
    col = lambda j, cr: (cr[0] + offsets[0] + j) % N_CHIPS
    grid = (len(offsets), t // tm, 1)
    in_specs = [pl.BlockSpec((tm, d), lambda j, i, k, cr: (i, 0)),
                pl.BlockSpec((d, cs), lambda j, i, k, cr: (0, 0 if own else col(j, cr)))]
    args = [xb, w]
    out_specs = [pl.BlockSpec((tm, cs), lambda j, i, k, cr: (i, col(j, cr)))]
    out_shape = [jax.ShapeDtypeStruct((t, D_IN), BF16)]
    scratch = [pltpu.VMEM((8, 128), F32)]
    n_ex_in = 0
    if exchange:
        n_ex_in = len(exchange.ins)
        in_specs += [pl.BlockSpec(memory_space=pl.ANY)] * n_ex_in
        args += exchange.ins
        out_specs += [pl.BlockSpec(memory_space=pl.ANY)] * len(exchange.out_shape)
        out_shape += exchange.out_shape
        scratch += [pltpu.SemaphoreType.DMA((s,)) for s in exchange.n_sems]
    aliases = {}
    if base is not None:
        in_specs.append(pl.BlockSpec(memory_space=pl.ANY))
        args.append(base)
        aliases = {len(args): 0}
    outs = pl.pallas_call(
        functools.partial(_proj_shards_body, n_ex_in=n_ex_in, has_base=base is not None, exchange=exchange, grid=grid),
        name=name,
        grid_spec=pltpu.PrefetchScalarGridSpec(num_scalar_prefetch=1, grid=grid, in_specs=in_specs,
                                               out_specs=out_specs, scratch_shapes=scratch),
        out_shape=out_shape, input_output_aliases=aliases,
        compiler_params=_params(("arbitrary",) * 3),
    )(chip, *args)
    return (outs[0], outs[1:]) if exchange else outs[0]


def _sigmoid(v):
    return 1.0 / (1.0 + jnp.exp(-v))


def _silu_and_grad(z):
    s = _sigmoid(z)
    return z * s, s * (1.0 + z * (1.0 - s))


def _cast_body(x_ref, o_ref):
    o_ref[...] = x_ref[...].astype(o_ref.dtype)


def _cast_and_transpose_body(x_ref, after_ref, o_ref, ot_ref):
    del after_ref
    xb = x_ref[...].astype(BF16)
    o_ref[...] = xb
    ot_ref[...] = xb.T


def _cast_and_transpose(x, after, *, name, rows=512):
    m, n = x.shape
    rows = min(rows, m)
    return pl.pallas_call(
        functools.partial(_cast_and_transpose_body), name=name, grid=(m // rows,),
        in_specs=[pl.BlockSpec((rows, n), lambda i: (i, 0)), pl.BlockSpec(memory_space=pl.ANY)],
        out_specs=[pl.BlockSpec((rows, n), lambda i: (i, 0)), pl.BlockSpec((n, rows), lambda i: (0, i))],
        out_shape=[jax.ShapeDtypeStruct((m, n), BF16), jax.ShapeDtypeStruct((n, m), BF16)],
        compiler_params=_params(("parallel",)),
    )(x, after)


def _cast(x, dtype, *, name, rows=512):
    m, n = x.shape
    rows = min(rows, m)
    return pl.pallas_call(
        functools.partial(_cast_body), name=name, grid=(m // rows,),
        in_specs=[pl.BlockSpec((rows, n), lambda i: (i, 0))],
        out_specs=pl.BlockSpec((rows, n), lambda i: (i, 0)),
        out_shape=jax.ShapeDtypeStruct((m, n), dtype),
        compiler_params=_params(("parallel",)),
    )(x)


def _lam_bar(lr, li, ls):
    step = jnp.exp(ls)
    er = jnp.exp(lr * step)
    return step, er * jnp.cos(li * step), er * jnp.sin(li * step)


def _ssm_pow_body(lr_ref, li_ref, ls_ref, pr_ref, pi_ref):
    _, ar, ai = _lam_bar(lr_ref[...], li_ref[...], ls_ref[...])
    cr, ci = ar, ai
    for i in range(SEG_LEN):
        pr_ref[pl.ds(i, 1), :] = cr
        pi_ref[pl.ds(i, 1), :] = ci
        cr, ci = cr * ar - ci * ai, cr * ai + ci * ar


def _ssm_bbar_body(lr_ref, li_ref, ls_ref, br_ref, bi_ref, or_ref, oi_ref):
    lr, li = lr_ref[...], li_ref[...]
    _, ar, ai = _lam_bar(lr, li, ls_ref[...])
    d = lr * lr + li * li
    ir, ii = lr / d, -li / d
    nr, ni = ar - 1.0, ai
    cr, ci = nr * ir - ni * ii, nr * ii + ni * ir
    br, bi = br_ref[...], bi_ref[...]
    or_ref[...] = cr * br - ci * bi
    oi_ref[...] = cr * bi + ci * br


def _ssm_param_bwd_body(lr_ref, li_ref, ls_ref, br_ref, bi_ref, glr_ref, gli_ref, gbr_ref, gbi_ref,
                        dlr_ref, dli_ref, dls_ref, dbr_ref, dbi_ref):
    lr, li = lr_ref[...], li_ref[...]
    step, ar, ai = _lam_bar(lr, li, ls_ref[...])
    d = lr * lr + li * li
    ir, ii = lr / d, -li / d
    nr, ni = ar - 1.0, ai
    cr, ci = nr * ir - ni * ii, nr * ii + ni * ir
    br, bi, gbr, gbi = br_ref[...], bi_ref[...], gbr_ref[...], gbi_ref[...]
    dbr_ref[...] = cr * gbr + ci * gbi
    dbi_ref[...] = cr * gbi - ci * gbr
    gcr = jnp.sum(br * gbr + bi * gbi, axis=1, keepdims=True)
    gci = jnp.sum(br * gbi - bi * gbr, axis=1, keepdims=True)
    gnr, gni = ir * gcr + ii * gci, ir * gci - ii * gcr
    gir, gii = nr * gcr + ni * gci, nr * gci - ni * gcr
    gtr, gti = glr_ref[...] + gnr, gli_ref[...] + gni
    i2r, i2i = ir * ir - ii * ii, 2.0 * ir * ii
    g1r, g1i = -(i2r * gir + i2i * gii), -(i2r * gii - i2i * gir)
    g2r, g2i = step * (ar * gtr + ai * gti), step * (ar * gti - ai * gtr)
    mr, mi = lr * ar - li * ai, lr * ai + li * ar
    dlr_ref[...] = g1r + g2r
    dli_ref[...] = g1i + g2i
    dls_ref[...] = (mr * gtr + mi * gti) * step


def _whole(shape):
    return pl.BlockSpec(shape, lambda *_: (0,) * len(shape))


def _ssm_prepare(lam_re, lam_im, b_re, b_im, c_re, c_im, log_step):
    row = lambda a: a.reshape(1, N_LANES)
    col = lambda a: a.reshape(N_LANES, 1)
    ls = jnp.repeat(log_step.reshape(N_GROUPS), SSM_STATE)
    p_re, p_im = pl.pallas_call(
        functools.partial(_ssm_pow_body), name="ssm_pow",
        in_specs=[_whole((1, N_LANES))] * 3, out_specs=[_whole((SEG_LEN, N_LANES))] * 2,
        out_shape=[jax.ShapeDtypeStruct((SEG_LEN, N_LANES), F32)] * 2, grid=(1,),
    )(row(lam_re), row(lam_im), row(ls))
    bb_re, bb_im = pl.pallas_call(
        functools.partial(_ssm_bbar_body), name="ssm_bbar",
        in_specs=[_whole((N_LANES, 1))] * 3 + [_whole((N_LANES, SSM_GROUP))] * 2,
        out_specs=[_whole((N_LANES, SSM_GROUP))] * 2,
        out_shape=[jax.ShapeDtypeStruct((N_LANES, SSM_GROUP), F32)] * 2, grid=(1,),
    )(col(lam_re), col(lam_im), col(ls), b_re.reshape(N_LANES, SSM_GROUP), b_im.reshape(N_LANES, SSM_GROUP))
    eye = jnp.eye(8, dtype=F32)

    def b_slabs(bb):
        t = bb.reshape(N_SLABS, 8, SSM_STATE, SSM_GROUP).transpose(0, 1, 3, 2)
        return (t[:, :, :, None, :] * eye[None, :, None, :, None]).reshape(N_SLABS, SLAB_CH, SLAB_LANES)

    def c_slabs(c):
        t = c.reshape(N_SLABS, 8, SSM_GROUP, SSM_STATE).transpose(0, 1, 3, 2)
        return (t[:, :, :, None, :] * eye[None, :, None, :, None]).reshape(N_SLABS, SLAB_LANES, SLAB_CH)

    b_bd = jnp.concatenate([b_slabs(bb_re), b_slabs(bb_im)], axis=2).astype(BF16)
    c_bd = jnp.concatenate([c_slabs(c_re.reshape(N_GROUPS, SSM_GROUP, SSM_STATE)),
                            -c_slabs(c_im.reshape(N_GROUPS, SSM_GROUP, SSM_STATE))], axis=1).astype(BF16)
    return p_re, p_im, b_bd, c_bd


def _diag_blocks_b(g):
    t = g.reshape(N_SLABS, 8, SSM_GROUP, 8, SSM_STATE)
    t = jnp.stack([t[:, i, :, i, :] for i in range(8)], axis=1)
    return t.transpose(0, 1, 3, 2).reshape(N_LANES, SSM_GROUP)


def _diag_blocks_c(g):
    t = g.reshape(N_SLABS, 8, SSM_STATE, 8, SSM_GROUP)
    t = jnp.stack([t[:, i, :, i, :] for i in range(8)], axis=1)
    return t.transpose(0, 1, 3, 2).reshape(N_GROUPS, SSM_GROUP, SSM_STATE)


def _ssm_param_grads(lam_re, lam_im, b_re, b_im, log_step, g_lam_re, g_lam_im, g_bbd, g_cbd):
    col = lambda a: a.reshape(N_LANES, 1)
    ls = jnp.repeat(log_step.reshape(N_GROUPS), SSM_STATE)
    gbr = _diag_blocks_b(g_bbd[:, :, :SLAB_LANES])
    gbi = _diag_blocks_b(g_bbd[:, :, SLAB_LANES:])
    outs = pl.pallas_call(
        functools.partial(_ssm_param_bwd_body), name="ssm_param_bwd", grid=(1,),
        in_specs=[_whole((N_LANES, 1))] * 3 + [_whole((N_LANES, SSM_GROUP))] * 2 + [_whole((N_LANES, 1))] * 2
        + [_whole((N_LANES, SSM_GROUP))] * 2,
        out_specs=[_whole((N_LANES, 1))] * 3 + [_whole((N_LANES, SSM_GROUP))] * 2,
        out_shape=[jax.ShapeDtypeStruct((N_LANES, 1), F32)] * 3 + [jax.ShapeDtypeStruct((N_LANES, SSM_GROUP), F32)] * 2,
    )(col(lam_re), col(lam_im), col(ls), b_re.reshape(N_LANES, SSM_GROUP), b_im.reshape(N_LANES, SSM_GROUP),
      col(g_lam_re), col(g_lam_im), gbr, gbi)
    dlr, dli, dls, dbr, dbi = outs
    d_c_re = _diag_blocks_c(g_cbd[:, :SLAB_LANES, :])
    d_c_im = -_diag_blocks_c(g_cbd[:, SLAB_LANES:, :])
    return (dlr.reshape(1, N_GROUPS, SSM_STATE), dli.reshape(1, N_GROUPS, SSM_STATE),
            dbr.reshape(1, N_GROUPS, SSM_STATE, SSM_GROUP), dbi.reshape(1, N_GROUPS, SSM_STATE, SSM_GROUP),
            d_c_re[None], d_c_im[None], dls.reshape(N_GROUPS, SSM_STATE).sum(axis=1).reshape(1, N_GROUPS))


def _bcast8(v):
    return jnp.broadcast_to(v, (8, v.shape[1]))


def _segment_permutation():
    p = np.zeros((SSM_CHUNK, SSM_CHUNK), np.float32)
    rows = np.arange(SSM_CHUNK)
    p[rows, (rows % 8) * SEG_LEN + rows // 8] = 1.0
    return p


def _permute_exact(perm, val, pieces):
    out, rest = None, val
    for n in range(pieces):
        part = rest.astype(BF16)
        moved = jnp.dot(perm, part, preferred_element_type=F32)
        out = moved if out is None else out + moved
        if n + 1 < pieces:
            rest = rest - part.astype(F32)
    return out


def _scan_buffer():
    return pltpu.VMEM((SSM_CHUNK, N_LANES), F32)


def _lanes(k):
    return pl.ds(k * SLAB_LANES, SLAB_LANES)


def _tile(i):
    return pl.ds(i * 8 if isinstance(i, int) else pl.multiple_of(i * 8, 8), 8)


SCAN_LANES = 1024


def _scan_lanes(k):
    return pl.ds(k * SCAN_LANES, SCAN_LANES)


def _seg_get(ref, k, i):
    return ref[_tile(i), _scan_lanes(k)]


def _seg_put(ref, k, i, val):
    ref[_tile(i), _scan_lanes(k)] = val


def _slab_get(ref, k):
    return ref[:, _lanes(k)]


def _slab_put(ref, k, val):
    ref[:, _lanes(k)] = val


def _scan_forward(s_re, s_im, p_re, p_im, car_re, car_im, sp_re=None, sp_im=None):
    for k in range(N_LANES // SCAN_LANES):
        ln = _scan_lanes(k)
        ar, ai =_bcast8(p_re[pl.ds(0, 1), ln]), _bcast8(p_im[pl.ds(0, 1), ln])

        def step(i, s, k=k, ar=ar, ai=ai):
            sr, si = s
            nr = ar * sr - ai * si + _seg_get(s_re, k, i)
            ni = ar * si + ai * sr + _seg_get(s_im, k, i)
            _seg_put(s_re, k, i, nr)
            _seg_put(s_im, k, i, ni)
            return nr, ni

        zero = jnp.zeros((8, SCAN_LANES), F32)
        er, ei = lax.fori_loop(0, SEG_LEN, step, (zero, zero), unroll=4)
        lr, li = p_re[pl.ds(SEG_LEN - 1, 1), ln], p_im[pl.ds(SEG_LEN - 1, 1), ln]
        cr, ci = car_re[pl.ds(0, 1), ln], car_im[pl.ds(0, 1), ln]
        rows_r, rows_i = [], []
        for r in range(8):
            rows_r.append(cr)
            rows_i.append(ci)
            cr, ci = er[r:r + 1] + lr * cr - li * ci, ei[r:r + 1] + lr * ci + li * cr
        pr8, pi8 = jnp.concatenate(rows_r, axis=0), jnp.concatenate(rows_i, axis=0)
        car_re[:, ln] = _bcast8(cr)
        car_im[:, ln] = _bcast8(ci)
        if sp_re is not None:
            sp_re[:, ln] = pr8
            sp_im[:, ln] = pi8

        def fix(i, _, k=k, ln=ln, pr8=pr8, pi8=pi8):
            qr, qi = _bcast8(p_re[pl.ds(i, 1), ln]), _bcast8(p_im[pl.ds(i, 1), ln])
            _seg_put(s_re, k, i, _seg_get(s_re, k, i) + qr * pr8 - qi * pi8)
            _seg_put(s_im, k, i, _seg_get(s_im, k, i) + qr * pi8 + qi * pr8)
            return 0

        lax.fori_loop(0, SEG_LEN, fix, 0, unroll=4)


def _scan_backward(g_re, g_im, s_re, s_im, sp_re, sp_im, p_re, p_im, car_re, car_im, acc_re, acc_im):
    for k in range(N_LANES // SCAN_LANES):
        ln = _scan_lanes(k)
        ar, ai =_bcast8(p_re[pl.ds(0, 1), ln]), -_bcast8(p_im[pl.ds(0, 1), ln])

        def step(j, s, k=k, ar=ar, ai=ai):
            i = SEG_LEN - 1 - j
            sr, si = s
            nr = ar * sr - ai * si + _seg_get(g_re, k, i)
            ni = ar * si + ai * sr + _seg_get(g_im, k, i)
            _seg_put(g_re, k, i, nr)
            _seg_put(g_im, k, i, ni)
            return nr, ni

        zero = jnp.zeros((8, SCAN_LANES), F32)
        er, ei = lax.fori_loop(0, SEG_LEN, step, (zero, zero), unroll=4)
        lr, li = p_re[pl.ds(SEG_LEN - 1, 1), ln], -p_im[pl.ds(SEG_LEN - 1, 1), ln]
        cr, ci = car_re[pl.ds(0, 1), ln], car_im[pl.ds(0, 1), ln]
        rows_r, rows_i = [None] * 8, [None] * 8
        for r in range(7, -1, -1):
            rows_r[r], rows_i[r] = cr, ci
            cr, ci = er[r:r + 1] + lr * cr - li * ci, ei[r:r + 1] + lr * ci + li * cr
        nr8, ni8 = jnp.concatenate(rows_r, axis=0), jnp.concatenate(rows_i, axis=0)
        car_re[:, ln] = _bcast8(cr)
        car_im[:, ln] = _bcast8(ci)

        def fix(i, acc, k=k, ln=ln, nr8=nr8, ni8=ni8):
            qr = _bcast8(p_re[pl.ds(SEG_LEN - 1 - i, 1), ln])
            qi = -_bcast8(p_im[pl.ds(SEG_LEN - 1 - i, 1), ln])
            gr = _seg_get(g_re, k, i) + qr * nr8 - qi * ni8
            gi = _seg_get(g_im, k, i) + qr * ni8 + qi * nr8
            _seg_put(g_re, k, i, gr)
            _seg_put(g_im, k, i, gi)
            return gr, gi

        def prod(xr, xi, gr, gi):
            return xr * gr + xi * gi, xr * gi - xi * gr

        gr, gi = fix(0, None)
        a_r, a_i = prod(sp_re[:, ln], sp_im[:, ln], gr, gi)

        def fix_acc(i, acc, k=k, fix=fix):
            gr, gi = fix(i, None)
            dr, di = prod(_seg_get(s_re, k, i - 1), _seg_get(s_im, k, i - 1), gr, gi)
            return acc[0] + dr, acc[1] + di

        acc = (a_r, a_i)
        for i in range(1, 4):
            acc = fix_acc(i, acc)
        a_r, a_i = lax.fori_loop(4, SEG_LEN, fix_acc, acc, unroll=4)
        acc_re[:, ln] += a_r
        acc_im[:, ln] += a_i


def _gelu_and_grad(y):
    cdf = 0.5 * (1.0 + lax.erf(y * (2.0 ** -0.5)))
    pdf = jnp.exp(-0.5 * y * y) * (1.0 / math.sqrt(2.0 * math.pi))
    return y * cdf, cdf + y * pdf


def _ssm_fwd_body(u_ref, bbd_ref, cbd_ref, pre_ref, pim_ref, d_ref, perm_ref, unperm_ref,
                  y_ref, gin_ref, gint_ref, cre_out, cim_out, s_re, s_im, car_re, car_im, yp):
    c = pl.program_id(0)

    @pl.when(c == 0)
    def _():
        car_re[...] = jnp.zeros_like(car_re)
        car_im[...] = jnp.zeros_like(car_im)

    cre_out[...] = car_re[...]
    cim_out[...] = car_im[...]
    u = u_ref[...].astype(F32)
    up = jnp.dot(perm_ref[...], u_ref[...].astype(BF16), preferred_element_type=F32).astype(BF16)
    for k in range(N_SLABS):
        bu = jnp.dot(up[:, k * SLAB_CH:(k + 1) * SLAB_CH], bbd_ref[k], preferred_element_type=F32)
        _slab_put(s_re, k, bu[:, :SLAB_LANES])
        _slab_put(s_im, k, bu[:, SLAB_LANES:])
    _scan_forward(s_re, s_im, pre_ref, pim_ref, car_re, car_im)
    for k in range(N_SLABS):
        yp[:, pl.ds(k * SLAB_CH, SLAB_CH)] = (
            jnp.dot(_slab_get(s_re, k).astype(BF16), cbd_ref[k, :SLAB_LANES, :], preferred_element_type=F32)
            + jnp.dot(_slab_get(s_im, k).astype(BF16), cbd_ref[k, SLAB_LANES:, :], preferred_element_type=F32))
    y = _permute_exact(unperm_ref[...], yp[...], 1) + d_ref[...] * u
    y_ref[...] = y.astype(y_ref.dtype)
    _store_with_transpose(_gelu_and_grad(y)[0], gin_ref, gint_ref)


def _ssm_forward(proj, p_re, p_im, b_bd, c_bd, d_skip):
    t = proj.shape[0]
    nc = t // SSM_CHUNK
    perm = _segment_permutation()
    return pl.pallas_call(
        functools.partial(_ssm_fwd_body), name="ssm_fwd", grid=(nc,),
        in_specs=[pl.BlockSpec((SSM_CHUNK, D_SSM), lambda c: (c, C_U // D_SSM)),
                  _whole(b_bd.shape), _whole(c_bd.shape), _whole(p_re.shape), _whole(p_im.shape),
                  _whole((1, D_SSM)), _whole(perm.shape), _whole(perm.shape)],
        out_specs=[pl.BlockSpec((SSM_CHUNK, D_SSM), lambda c: (c, 0)),
                   pl.BlockSpec((SSM_CHUNK, D_SSM), lambda c: (c, 0)),
                   pl.BlockSpec((D_SSM, SSM_CHUNK), lambda c: (0, c)),
                   pl.BlockSpec((None, 8, N_LANES), lambda c: (c, 0, 0)),
                   pl.BlockSpec((None, 8, N_LANES), lambda c: (c, 0, 0))],
        out_shape=[jax.ShapeDtypeStruct((t, D_SSM), BF16), jax.ShapeDtypeStruct((t, D_SSM), BF16),
                   jax.ShapeDtypeStruct((D_SSM, t), BF16),
                   jax.ShapeDtypeStruct((nc, 8, N_LANES), F32), jax.ShapeDtypeStruct((nc, 8, N_LANES), F32)],
        scratch_shapes=[_scan_buffer(), _scan_buffer(),
                        pltpu.VMEM((8, N_LANES), F32), pltpu.VMEM((8, N_LANES), F32),
                        pltpu.VMEM((SSM_CHUNK, D_SSM), F32)],
        compiler_params=_params(("arbitrary",)),
    )(proj, b_bd, c_bd, p_re, p_im, d_skip, jnp.asarray(perm, BF16), jnp.asarray(perm.T, BF16))


def _ssm_bwd_body(u_ref, dgin_ref, y_ref, cre_in, cim_in, bbd_ref, cbd_ref, pre_ref, pim_ref, d_ref, perm_ref,
                  unperm_ref, dproj_in,
                  du_ref, gb_ref, gc_ref, glr_ref, gli_ref, gd_ref,
                  s_re, s_im, g_re, g_im, sp_re, sp_im, car_re, car_im, gcar_re, gcar_im, acc_re, acc_im, dup):
    del dproj_in
    c = pl.program_id(0)
    nc = pl.num_programs(0)

    @pl.when(c == 0)
    def _():
        gcar_re[...] = jnp.zeros_like(gcar_re)
        gcar_im[...] = jnp.zeros_like(gcar_im)
        acc_re[...] = jnp.zeros_like(acc_re)
        acc_im[...] = jnp.zeros_like(acc_im)
        gb_ref[...] = jnp.zeros_like(gb_ref)
        gc_ref[...] = jnp.zeros_like(gc_ref)
        gd_ref[...] = jnp.zeros_like(gd_ref)

    car_re[...] = cre_in[...]
    car_im[...] = cim_in[...]
    u = u_ref[...].astype(F32)
    dy = dgin_ref[...].astype(F32) * _gelu_and_grad(y_ref[...].astype(F32))[1]
    gd_ref[...] += jnp.sum(dy * u, axis=0, keepdims=True)
    up = jnp.dot(perm_ref[...], u.astype(BF16), preferred_element_type=F32).astype(BF16)
    dyp = jnp.dot(perm_ref[...], dy.astype(BF16), preferred_element_type=F32).astype(BF16)
    for k in range(N_SLABS):
        ch = slice(k * SLAB_CH, (k + 1) * SLAB_CH)
        bu = jnp.dot(up[:, ch], bbd_ref[k], preferred_element_type=F32)
        _slab_put(s_re, k, bu[:, :SLAB_LANES])
        _slab_put(s_im, k, bu[:, SLAB_LANES:])
        ds = lax.dot_general(dyp[:, ch], cbd_ref[k], _DIMS["nt"], preferred_element_type=F32)
        _slab_put(g_re, k, ds[:, :SLAB_LANES])
        _slab_put(g_im, k, ds[:, SLAB_LANES:])
    _scan_forward(s_re, s_im, pre_ref, pim_ref, car_re, car_im, sp_re, sp_im)
    _scan_backward(g_re, g_im, s_re, s_im, sp_re, sp_im, pre_ref, pim_ref, gcar_re, gcar_im, acc_re, acc_im)
    for k in range(N_SLABS):
        ch = slice(k * SLAB_CH, (k + 1) * SLAB_CH)
        uk, dyk = up[:, ch], dyp[:, ch]
        sr, si = _slab_get(s_re, k).astype(BF16), _slab_get(s_im, k).astype(BF16)
        gr, gi = _slab_get(g_re, k).astype(BF16), _slab_get(g_im, k).astype(BF16)
        gc_ref[k, :SLAB_LANES, :] += lax.dot_general(sr, dyk, _DIMS["tn"], preferred_element_type=F32)
        gc_ref[k, SLAB_LANES:, :] += lax.dot_general(si, dyk, _DIMS["tn"], preferred_element_type=F32)
        gb_ref[k, :, :SLAB_LANES] += lax.dot_general(uk, gr, _DIMS["tn"], preferred_element_type=F32)
        gb_ref[k, :, SLAB_LANES:] += lax.dot_general(uk, gi, _DIMS["tn"], preferred_element_type=F32)
        dup[:, pl.ds(k * SLAB_CH, SLAB_CH)] = (
            lax.dot_general(gr, bbd_ref[k, :, :SLAB_LANES], _DIMS["nt"], preferred_element_type=F32)
            + lax.dot_general(gi, bbd_ref[k, :, SLAB_LANES:], _DIMS["nt"], preferred_element_type=F32))
    du = _permute_exact(unperm_ref[...], dup[...], 1) + d_ref[...] * dy
    du_ref[...] = du.astype(du_ref.dtype)

    @pl.when(c == nc - 1)
    def _():
        glr_ref[...] = jnp.sum(acc_re[...], axis=0, keepdims=True)
        gli_ref[...] = jnp.sum(acc_im[...], axis=0, keepdims=True)


def _ssm_backward(proj, dg_in, y_ssm, car_re, car_im, p_re, p_im, b_bd, c_bd, d_skip, dproj):
    t = proj.shape[0]
    nc = t // SSM_CHUNK
    rev = lambda c: nc - 1 - c
    big = _scan_buffer
    small = lambda: pltpu.VMEM((8, N_LANES), F32)
    perm = _segment_permutation()
    outs = pl.pallas_call(
        functools.partial(_ssm_bwd_body), name="ssm_bwd", grid=(nc,),
        in_specs=[pl.BlockSpec((SSM_CHUNK, D_SSM), lambda c: (rev(c), C_U // D_SSM)),
                  pl.BlockSpec((SSM_CHUNK, D_SSM), lambda c: (rev(c), 0)),
                  pl.BlockSpec((SSM_CHUNK, D_SSM), lambda c: (rev(c), 0)),
                  pl.BlockSpec((None, 8, N_LANES), lambda c: (rev(c), 0, 0)),
                  pl.BlockSpec((None, 8, N_LANES), lambda c: (rev(c), 0, 0)),
                  _whole(b_bd.shape), _whole(c_bd.shape), _whole(p_re.shape), _whole(p_im.shape),
                  _whole((1, D_SSM)), _whole(perm.shape), _whole(perm.shape), pl.BlockSpec(memory_space=pl.ANY)],
        out_specs=[pl.BlockSpec((SSM_CHUNK, D_SSM), lambda c: (rev(c), C_U // D_SSM)),
                   _whole(b_bd.shape), _whole(c_bd.shape), _whole((1, N_LANES)), _whole((1, N_LANES)),
                   _whole((1, D_SSM))],
        out_shape=[jax.ShapeDtypeStruct(dproj.shape, dproj.dtype),
                   jax.ShapeDtypeStruct(b_bd.shape, F32), jax.ShapeDtypeStruct(c_bd.shape, F32),
                   jax.ShapeDtypeStruct((1, N_LANES), F32), jax.ShapeDtypeStruct((1, N_LANES), F32),
                   jax.ShapeDtypeStruct((1, D_SSM), F32)],
        scratch_shapes=[big(), big(), big(), big()] + [small() for _ in range(8)]
        + [pltpu.VMEM((SSM_CHUNK, D_SSM), F32)],
        input_output_aliases={12: 0},
        compiler_params=_params(("arbitrary",)),
    )(proj, dg_in, y_ssm, car_re, car_im, b_bd, c_bd, p_re, p_im, d_skip, jnp.asarray(perm, BF16),
      jnp.asarray(perm.T, BF16), dproj)
    return outs


def _bucket_table():
    i = np.arange(BLOCK)[:, None]
    j = np.arange(2 * BLOCK)[None, :]
    dist = BLOCK + i - j
    ok = (dist >= 0) & (dist < WINDOW)
    max_exact = N_BUCKETS // 2
    d = np.maximum(dist, 1).astype(np.float32)
    large = max_exact + (np.log(d / max_exact) / math.log(MAX_DISTANCE / max_exact)
                         * (N_BUCKETS - max_exact)).astype(np.int32)
    large = np.minimum(large, N_BUCKETS - 1)
    bucket = np.where(dist < max_exact, dist, large)
    return np.where(ok, bucket, -1).astype(np.int32)


def _bias_build_body(table_ref, bucket_ref, o_ref):
    h = pl.program_id(0)
    bucket = bucket_ref[...]
    acc = jnp.full(bucket.shape, NEG_INF, F32)
    for b in range(N_BUCKETS):
        acc = jnp.where(bucket == b, table_ref[b, h], acc)
    o_ref[...] = acc


def _bias_build(rel_bias_table, bucket):
    return pl.pallas_call(
        functools.partial(_bias_build_body), name="bias_build", grid=(N_Q_HEADS,),
        in_specs=[pl.BlockSpec(memory_space=pltpu.SMEM), _whole(bucket.shape)],
        out_specs=pl.BlockSpec((None, BLOCK, 2 * BLOCK), lambda h: (h, 0, 0)),
        out_shape=jax.ShapeDtypeStruct((N_Q_HEADS, BLOCK, 2 * BLOCK), F32),
        compiler_params=_params(("arbitrary",)),
    )(rel_bias_table, bucket)


def _bias_grad_body(g_ref, bucket_ref, o_ref):
    bucket = bucket_ref[...]
    lane = lax.broadcasted_iota(jnp.int32, (N_BUCKETS, 128), 1)

    def head(h, out):
        g = g_ref[h]
        rows = [jnp.sum(jnp.where(bucket == b, g, 0.0), axis=0, keepdims=True) for b in range(N_BUCKETS)]
        colsum = jnp.sum(jnp.concatenate(rows, axis=0), axis=1, keepdims=True)
        return jnp.where(lane == h, colsum, out)

    o_ref[...] = lax.fori_loop(0, N_Q_HEADS, head, jnp.zeros((N_BUCKETS, 128), F32))


def _bias_grad(g_bias, bucket):
    out = pl.pallas_call(
        functools.partial(_bias_grad_body), name="bias_grad", grid=(1,),
        in_specs=[_whole(g_bias.shape), _whole(bucket.shape)],
        out_specs=_whole((N_BUCKETS, 128)),
        out_shape=jax.ShapeDtypeStruct((N_BUCKETS, 128), F32),
        compiler_params=_params(("arbitrary",)),
    )(g_bias, bucket)
    return out[:, :N_Q_HEADS]


def _head_logits(qh, kk, bias_h, first_block):
    s = lax.dot_general(qh, kk, _DIMS["nt"], preferred_element_type=F32) * ATTN_SCALE + bias_h
    col = lax.broadcasted_iota(jnp.int32, s.shape, 1)
    return jnp.where(jnp.logical_and(first_block, col < BLOCK), NEG_INF, s)


def _attn_fwd_body(sink_ref, q_ref, kp_ref, kc_ref, vp_ref, vc_ref, bias_ref, o_ref, lse_ref):
    n = pl.program_id(0)
    outs, lses = [], []
    for kv in range(N_KV_HEADS):
        cs = slice(kv * HEAD_DIM, (kv + 1) * HEAD_DIM)
        kk = jnp.concatenate([kp_ref[:, cs], kc_ref[:, cs]], axis=0).astype(BF16)
        vv = jnp.concatenate([vp_ref[:, cs], vc_ref[:, cs]], axis=0).astype(BF16)
        for g in range(Q_PER_KV):
            h = kv * Q_PER_KV + g
            qh = q_ref[:, h * HEAD_DIM:(h + 1) * HEAD_DIM].astype(BF16)
            s = _head_logits(qh, kk, bias_ref[h], n == 0)
            sink = sink_ref[0, h]
            m = jnp.maximum(jnp.max(s, axis=1, keepdims=True), sink)
            p = jnp.exp(s - m)
            den = jnp.sum(p, axis=1, keepdims=True) + jnp.exp(sink - m)
            p = p / den
            outs.append(jnp.dot(p.astype(BF16), vv, preferred_element_type=F32))
            lses.append(m + jnp.log(den))
    o_ref[...] = jnp.concatenate(outs, axis=1).astype(o_ref.dtype)
    lse_ref[...] = jnp.concatenate(lses, axis=1)


def _attn_specs(nb):
    prev = lambda n: jnp.maximum(jnp.minimum(n, nb - 1) - 1, 0)
    cur = lambda n: jnp.minimum(n, nb - 1)
    return [pl.BlockSpec((BLOCK, D_ATTN), lambda n: (cur(n), C_Q // D_ATTN)),
            pl.BlockSpec((BLOCK, D_KV), lambda n: (prev(n), C_K // D_KV)),
            pl.BlockSpec((BLOCK, D_KV), lambda n: (cur(n), C_K // D_KV)),
            pl.BlockSpec((BLOCK, D_KV), lambda n: (prev(n), C_V // D_KV)),
            pl.BlockSpec((BLOCK, D_KV), lambda n: (cur(n), C_V // D_KV))]


def _attn_forward(proj, sinks, bias):
    t = proj.shape[0]
    nb = t // BLOCK
    return pl.pallas_call(
        functools.partial(_attn_fwd_body), name="attn_fwd", grid=(nb,),
        in_specs=[pl.BlockSpec(memory_space=pltpu.SMEM)] + _attn_specs(nb) + [_whole(bias.shape)],
        out_specs=[pl.BlockSpec((BLOCK, D_ATTN), lambda n: (n, 0)),
                   pl.BlockSpec((BLOCK, N_Q_HEADS), lambda n: (n, 0))],
        out_shape=[jax.ShapeDtypeStruct((t, D_ATTN), BF16), jax.ShapeDtypeStruct((t, N_Q_HEADS), F32)],
        compiler_params=_params(("parallel",)),
    )(sinks, proj, proj, proj, proj, proj, bias)


def _attn_bwd_body(sink_ref, q_ref, kp_ref, kc_ref, vp_ref, vc_ref, bias_ref, do_ref, o_ref, lse_ref, dproj_in,
                   dq_ref, dkv_ref, gbias_ref, gsink_ref, carry_ref, *, nb):
    del dproj_in
    n = pl.program_id(0)

    @pl.when(n == 0)
    def _():
        gbias_ref[...] = jnp.zeros_like(gbias_ref)
        gsink_ref[...] = jnp.zeros_like(gsink_ref)
        carry_ref[...] = jnp.zeros_like(carry_ref)

    @pl.when(n < nb)
    def _():
        lane = lax.broadcasted_iota(jnp.int32, (1, 128), 1)
        dqs, dks, dvs = [], [], []
        gsink = jnp.zeros((1, 128), F32)
        for kv in range(N_KV_HEADS):
            cs = slice(kv * HEAD_DIM, (kv + 1) * HEAD_DIM)
            kk = jnp.concatenate([kp_ref[:, cs], kc_ref[:, cs]], axis=0).astype(BF16)
            vv = jnp.concatenate([vp_ref[:, cs], vc_ref[:, cs]], axis=0).astype(BF16)
            dk = jnp.zeros((2 * BLOCK, HEAD_DIM), F32)
            dv = jnp.zeros((2 * BLOCK, HEAD_DIM), F32)
            for g in range(Q_PER_KV):
                h = kv * Q_PER_KV + g
                hs = slice(h * HEAD_DIM, (h + 1) * HEAD_DIM)
                qh = q_ref[:, hs].astype(BF16)
                s = _head_logits(qh, kk, bias_ref[h], n == 0)
                lse = lse_ref[:, h:h + 1]
                p = jnp.exp(s - lse)
                do = do_ref[:, hs].astype(F32)
                delta = jnp.sum(do * o_ref[:, hs].astype(F32), axis=1, keepdims=True)
                dob = do.astype(BF16)
                dp = lax.dot_general(dob, vv, _DIMS["nt"], preferred_element_type=F32)
                dl = p * (dp - delta)
                gbias_ref[h] += dl
                psink = jnp.exp(sink_ref[0, h] - lse)
                gsink = gsink + jnp.where(lane == h, -jnp.sum(psink * delta), 0.0)
                dlb = dl.astype(BF16)
                dqs.append(jnp.dot(dlb, kk, preferred_element_type=F32) * ATTN_SCALE)
                dk = dk + lax.dot_general(dlb, qh, _DIMS["tn"], preferred_element_type=F32) * ATTN_SCALE
                dv = dv + lax.dot_general(p.astype(BF16), dob, _DIMS["tn"], preferred_element_type=F32)
            dks.append(dk)
            dvs.append(dv)
        dq_ref[...] = jnp.concatenate(dqs, axis=1).astype(dq_ref.dtype)
        gsink_ref[...] += gsink
        dkv = jnp.concatenate(dks + dvs, axis=1)
        dkv_ref[...] = (carry_ref[...] + dkv[:BLOCK]).astype(dkv_ref.dtype)
        carry_ref[...] = dkv[BLOCK:]

    @pl.when(n == nb)
    def _():
        dkv_ref[...] = carry_ref[...].astype(dkv_ref.dtype)


def _attn_backward(proj, sinks, bias, d_attn, attn, lse, dproj):
    t = proj.shape[0]
    nb = t // BLOCK
    cur = lambda n: jnp.minimum(n, nb - 1)
    return pl.pallas_call(
        functools.partial(_attn_bwd_body, nb=nb), name="attn_bwd", grid=(nb + 1,),
        in_specs=[pl.BlockSpec(memory_space=pltpu.SMEM)] + _attn_specs(nb) + [
            _whole(bias.shape),
            pl.BlockSpec((BLOCK, D_ATTN), lambda n: (cur(n), 0)),
            pl.BlockSpec((BLOCK, D_ATTN), lambda n: (cur(n), 0)),
            pl.BlockSpec((BLOCK, N_Q_HEADS), lambda n: (cur(n), 0)),
            pl.BlockSpec(memory_space=pl.ANY)],
        out_specs=[pl.BlockSpec((BLOCK, D_ATTN), lambda n: (cur(n), C_Q // D_ATTN)),
                   pl.BlockSpec((BLOCK, 2 * D_KV), lambda n: (jnp.maximum(n - 1, 0), 0)),
                   _whole(bias.shape), _whole((1, 128))],
        out_shape=[jax.ShapeDtypeStruct(dproj.shape, dproj.dtype), jax.ShapeDtypeStruct((t, 2 * D_KV), dproj.dtype),
                   jax.ShapeDtypeStruct(bias.shape, F32), jax.ShapeDtypeStruct((1, 128), F32)],
        scratch_shapes=[pltpu.VMEM((BLOCK, 2 * D_KV), F32)],
        input_output_aliases={10: 0},
        compiler_params=_params(("arbitrary",)),
    )(sinks, proj, proj, proj, proj, proj, bias, d_attn, attn, lse, dproj)


ROWS = 512


def _rowwise(body, name, t, ins, outs, aliases=None):
    rows = min(ROWS, t)

    def col_spec(w, c0):
        if c0 == "T":
            return pl.BlockSpec((w, rows), lambda i: (0, i))
        if c0 % w == 0:
            return pl.BlockSpec((rows, w), lambda i: (i, c0 // w))
        return pl.BlockSpec((pl.Element(rows), pl.Element(w)), lambda i: (i * rows, c0))

    in_specs, args = [], []
    for a, w, c0 in ins:
        args.append(a)
        if w is None:
            in_specs.append(pl.BlockSpec(memory_space=pl.ANY) if c0 == "any" else _whole(a.shape))
        else:
            in_specs.append(col_spec(w, c0))
    out_specs, out_shape = [], []
    for shape, dtype, w, c0 in outs:
        out_shape.append(jax.ShapeDtypeStruct(shape, dtype))
        out_specs.append(_whole(shape) if w is None else col_spec(w, c0))
    accum = any(o[2] is None for o in outs)
    return pl.pallas_call(
        body, name=name, grid=(t // rows,), in_specs=in_specs, out_specs=out_specs, out_shape=out_shape,
        input_output_aliases=aliases or {},
        compiler_params=_params(("arbitrary",) if accum else ("parallel",)),
    )(*args)


def _f32(ref, *idx):
    return (ref[idx] if idx else ref[...]).astype(F32)


def _store_with_transpose(val, ref, t_ref):
    val = val.astype(ref.dtype)
    ref[...] = val
    t_ref[...] = val.T


def _ssm_gate_fwd_body(glu_ref, z_ref, h_ref, ht_ref):
    a, b = _f32(glu_ref, slice(None), slice(0, D_SSM)), _f32(glu_ref, slice(None), slice(D_SSM, None))
    _store_with_transpose((a * _sigmoid(b)) * _silu_and_grad(_f32(z_ref))[0], h_ref, ht_ref)


def _ssm_gate_bwd_body(dh_ref, glu_ref, z_ref, dproj_in, dglu_ref, dz_ref):
    del dproj_in
    a, b = _f32(glu_ref, slice(None), slice(0, D_SSM)), _f32(glu_ref, slice(None), slice(D_SSM, None))
    sb = _sigmoid(b)
    silu, dsilu = _silu_and_grad(_f32(z_ref))
    dh = _f32(dh_ref)
    dg = dh * silu
    dz_ref[...] = (dh * (a * sb) * dsilu).astype(dz_ref.dtype)
    dglu_ref[:, :D_SSM] = (dg * sb).astype(dglu_ref.dtype)
    dglu_ref[:, D_SSM:] = (dg * a * sb * (1.0 - sb)).astype(dglu_ref.dtype)


def _attn_gate_fwd_body(attn_ref, z_ref, h_ref, ht_ref):
    _store_with_transpose(_f32(attn_ref) * _silu_and_grad(_f32(z_ref))[0], h_ref, ht_ref)


def _attn_gate_bwd_body(dh_ref, attn_ref, z_ref, dproj_in, dattn_ref, dz_ref):
    del dproj_in
    silu, dsilu = _silu_and_grad(_f32(z_ref))
    dh = _f32(dh_ref)
    dattn_ref[...] = (dh * silu).astype(dattn_ref.dtype)
    dz_ref[...] = (dh * _f32(attn_ref) * dsilu).astype(dz_ref.dtype)


def _merge_fwd_body(bs_ref, ba_ref, gl_ref, m_ref, mt_ref):
    gs = _sigmoid(_f32(gl_ref, slice(None), slice(0, D_MODEL)))
    ga = _sigmoid(_f32(gl_ref, slice(None), slice(D_MODEL, None)))
    _store_with_transpose(gs * _f32(bs_ref) + ga * _f32(ba_ref), m_ref, mt_ref)


def _merge_bwd_body(dm_ref, bs_ref, ba_ref, gl_ref, dbs_ref, dba_ref, dgl_ref):
    gs = _sigmoid(_f32(gl_ref, slice(None), slice(0, D_MODEL)))
    ga = _sigmoid(_f32(gl_ref, slice(None), slice(D_MODEL, None)))
    dm = _f32(dm_ref)
    dbs_ref[...] = (dm * gs).astype(dbs_ref.dtype)
    dba_ref[...] = (dm * ga).astype(dba_ref.dtype)
    dgl_ref[:, :D_MODEL] = (dm * _f32(bs_ref) * gs * (1.0 - gs)).astype(dgl_ref.dtype)
    dgl_ref[:, D_MODEL:] = (dm * _f32(ba_ref) * ga * (1.0 - ga)).astype(dgl_ref.dtype)


def _ln_loss_body(x_ref, o_ref, tgt_ref, gain_ref, bias_ref, dr_ref, loss_ref, dgain_ref, dbias_ref):
    @pl.when(pl.program_id(0) == 0)
    def _():
        loss_ref[...] = jnp.zeros_like(loss_ref)
        dgain_ref[...] = jnp.zeros_like(dgain_ref)
        dbias_ref[...] = jnp.zeros_like(dbias_ref)

    r = ALPHA * x_ref[...] + o_ref[...].astype(F32)
    mu = jnp.mean(r, axis=1, keepdims=True)
    rc = r - mu
    var = jnp.mean(rc * rc, axis=1, keepdims=True)
    rstd = lax.rsqrt(var + LN_EPS)
    xhat = rc * rstd
    gain = gain_ref[...]
    err = xhat * gain + bias_ref[...] - tgt_ref[...]
    loss_ref[...] += 0.5 * jnp.sum(jnp.mean(err * err, axis=1, keepdims=True), axis=0, keepdims=True)
    dy = err * (1.0 / D_MODEL)
    dgain_ref[...] += jnp.sum(dy * xhat, axis=0, keepdims=True)
    dbias_ref[...] += jnp.sum(dy, axis=0, keepdims=True)
    dxhat = dy * gain
    m1 = jnp.mean(dxhat, axis=1, keepdims=True)
    m2 = jnp.mean(dxhat * xhat, axis=1, keepdims=True)
    dr_ref[...] = (rstd * (dxhat - m1 - xhat * m2)).astype(dr_ref.dtype)


def _place_body(piece_ref, dproj_in, o_ref):
    del dproj_in
    o_ref[...] = piece_ref[...]


def _adamw_update(w_ref, m_ref, v_ref, g, g_ref, d_ref, nm_ref, nv_ref):
    m = ADAM_B1 * m_ref[...] + (1.0 - ADAM_B1) * g
    v = ADAM_B2 * v_ref[...] + (1.0 - ADAM_B2) * (g * g)
    m_hat = m / (1.0 - ADAM_B1 ** ADAM_STEP)
    v_hat = v / (1.0 - ADAM_B2 ** ADAM_STEP)
    g_ref[...] = g
    d_ref[...] = -ADAM_LR * (m_hat / (jnp.sqrt(v_hat) + ADAM_EPS) + ADAM_WD * w_ref[...])
    nm_ref[...] = m
    nv_ref[...] = v


def _adamw_body(*refs, n_parts):
    w_ref, m_ref, v_ref = refs[:3]
    parts = refs[3:3 + n_parts]
    g = parts[0][...].astype(F32)
    for p in parts[1:]:
        g = g + p[...].astype(F32)
    _adamw_update(w_ref, m_ref, v_ref, g, *refs[3 + n_parts:])


def _adamw_shard_body(c_ref, w_ref, m_ref, v_ref, mine_ref, other_ref, g_ref, d_ref, nm_ref, nv_ref, *, nth):
    in_mine = pl.program_id(0) // nth == c_ref[0]
    g = jnp.where(in_mine, mine_ref[...], other_ref[...])
    _adamw_update(w_ref, m_ref, v_ref, g, g_ref, d_ref, nm_ref, nv_ref)


def _adamw_shard(w, m, v, halves, core, *, name, rows):
    shape = w.shape
    w2, m2, v2 = (a.reshape(-1, shape[-1]) for a in (w, m, v))
    r, c = w2.shape
    nth = r // 2 // rows
    assert 2 * nth * rows == r and halves[0].shape == (r // 2, c)
    spec = pl.BlockSpec((rows, c), lambda i, cr: (i, 0))
    mine_spec = pl.BlockSpec((rows, c), lambda i, cr: (jnp.where(i // nth == cr[0], i % nth, 0), 0))
    other_spec = pl.BlockSpec((rows, c), lambda i, cr: (jnp.where(i // nth == cr[0], 0, i % nth), 0))
    outs = pl.pallas_call(
        functools.partial(_adamw_shard_body, nth=nth), name=name,
        grid_spec=pltpu.PrefetchScalarGridSpec(
            num_scalar_prefetch=1, grid=(r // rows,),
            in_specs=[spec] * 3 + [mine_spec, other_spec], out_specs=[spec] * 4),
        out_shape=[jax.ShapeDtypeStruct((r, c), F32)] * 4,
        compiler_params=_params(("parallel",)),
    )(core, w2, m2, v2, *halves)
    return tuple(o.reshape(shape) for o in outs)


def _adamw(w, m, v, parts, *, name, rows=256):
    shape = w.shape
    w2, m2, v2 = (a.reshape(-1, shape[-1]) for a in (w, m, v))
    parts = [p.reshape(w2.shape) for p in parts]
    r, c = w2.shape
    rows = rows if r % rows == 0 else r
    spec = pl.BlockSpec((rows, c), lambda i: (i, 0))
    outs = pl.pallas_call(
        functools.partial(_adamw_body, n_parts=len(parts)), name=name, grid=(r // rows,),
        in_specs=[spec] * (3 + len(parts)), out_specs=[spec] * 4,
        out_shape=[jax.ShapeDtypeStruct((r, c), F32)] * 4,
        compiler_params=_params(("parallel",)),
    )(w2, m2, v2, *parts)
    return tuple(o.reshape(shape) for o in outs)


BIG = ("w_in", "w_glu", "w_branch_ssm", "w_branch_attn", "w_out")
SHARD_AXIS = dict(w_in=1, w_glu=1, w_branch_ssm=1, w_branch_attn=1, w_out=0)
HBM = pl.BlockSpec(memory_space=pl.ANY)


def _position():
    x, y, c = lax.axis_index("x"), lax.axis_index("y"), lax.axis_index("c")
    other_chips = [(1 - x, y), (x, 1 - y), (1 - x, 1 - y)]
    return x, y, c, other_chips


def _window(ref, axis, shard, n_shards, half=None):
    rows, cols = ref.shape[-2:]
    sel = [slice(None), slice(None)]
    size = ref.shape[-2 + axis] // n_shards
    sel[axis] = pl.ds(pl.multiple_of(shard * size, 128), size)
    if half is not None:
        hsize = ref.shape[-1 - axis] // 2
        sel[1 - axis] = pl.ds(pl.multiple_of(half * hsize, 128), hsize)
    return ref.at[tuple(sel)]


def _half(ref, axis, half):
    hsize = ref.shape[-1 - axis] // 2
    sel = [slice(None), slice(None)]
    sel[1 - axis] = pl.ds(pl.multiple_of(half * hsize, 128), hsize)
    return ref.at[tuple(sel)]


def _remote(src, dst, send_sem, recv_sem, device):
    return pltpu.make_async_remote_copy(src_ref=src, dst_ref=dst, send_sem=send_sem, recv_sem=recv_sem,
                                        device_id=device, device_id_type=MESH)


def _full_shapes(shards, names):
    out = []
    for k in names:
        s = list(shards[k].shape)
        s[SHARD_AXIS[k]] *= N_CHIPS
        out.append(jax.ShapeDtypeStruct(tuple(s), shards[k].dtype))
    return out


def _gather_exchange(shards, names):
    axes = tuple(SHARD_AXIS[k] for k in names)
    n = len(names)

    def copies(ins, outs, sems):
        send_sems, recv_sems, local_sems = sems
        x, y, c, chips = _position()
        me = 2 * x + y
        remote, local = [], []
        for w, ax in enumerate(axes):
            for r, (px, py) in enumerate(chips):
                remote.append(_remote(_half(ins[w], ax, c), _window(outs[w], ax, me, N_CHIPS, c),
                                      send_sems.at[3 * w + r], recv_sems.at[3 * w + r], (px, py, c)))
            local.append(pltpu.make_async_copy(ins[w], _window(outs[w], ax, me, N_CHIPS), local_sems.at[w]))
        return remote, local

    return _Exchange([shards[k] for k in names], _full_shapes(shards, names), (3 * n, 3 * n, n), copies)


SEM = pl.BlockSpec(memory_space=pltpu.SEMAPHORE)


def _gather_start_body(shard_ref, land_ref, send_sems, recv_sems, shard_thru, land_thru, token, *, axis):
    del shard_thru, land_thru
    x, y, c, chips = _position()
    me = 2 * x + y
    for r, (px, py) in enumerate(chips):
        _remote(_half(shard_ref, axis, c), _window(land_ref, axis, me, N_CHIPS, c),
                send_sems.at[r], recv_sems.at[r], (px, py, c)).start()
    token[...] = jnp.zeros_like(token)


def _gather_start(shard, name):
    axis = SHARD_AXIS[name]
    full = _full_shapes({name: shard}, (name,))[0]
    return pl.pallas_call(
        functools.partial(_gather_start_body, axis=axis), name="gather_start_" + name,
        out_shape=(pltpu.SemaphoreType.DMA((3,)), pltpu.SemaphoreType.DMA((3,)), pltpu.HBM(shard.shape, shard.dtype),
                   pltpu.HBM(full.shape, full.dtype), jax.ShapeDtypeStruct((8, 128), F32)),
        in_specs=(HBM, HBM), out_specs=(SEM, SEM, HBM, HBM, pl.BlockSpec(memory_space=pltpu.VMEM)),
        input_output_aliases={0: 2, 1: 3},
        compiler_params=pltpu.CompilerParams(has_side_effects=pltpu.SideEffectType.DATAFLOW_SIDE_EFFECTING),
    )(pltpu.with_memory_space_constraint(shard, pltpu.HBM),
      pltpu.with_memory_space_constraint(lax.empty(full.shape, full.dtype), pltpu.HBM))


def _gather_wait_body(shard_ref, land_ref, send_sems, recv_sems, after_ref, shard_dead, landed, *, axis):
    del after_ref, shard_dead, landed
    x, y, c, chips = _position()
    for r, (px, py) in enumerate(chips):
        cp = _remote(_half(shard_ref, axis, c), _window(land_ref, axis, 2 * px + py, N_CHIPS, c),
                     send_sems.at[r], recv_sems.at[r], (px, py, c))
        cp.wait_send()
        cp.wait_recv()


def _gather_wait(send_sems, recv_sems, shard_thru, land_thru, after, name):
    return pl.pallas_call(
        functools.partial(_gather_wait_body, axis=SHARD_AXIS[name]), name="gather_wait_" + name,
        out_shape=(pltpu.HBM(shard_thru.shape, shard_thru.dtype), pltpu.HBM(land_thru.shape, land_thru.dtype)),
        in_specs=(HBM, HBM, SEM, SEM, HBM), out_specs=(HBM, HBM), input_output_aliases={0: 0, 1: 1},
        compiler_params=pltpu.CompilerParams(has_side_effects=pltpu.SideEffectType.DATAFLOW_SIDE_EFFECTING),
    )(shard_thru, land_thru, send_sems, recv_sems, after)[1]


def _pass_on_body(*refs, axes, n_own):
    n = len(axes)
    ins, own = refs[:n], refs[n:n + n_own]
    refs = refs[:n] + refs[n + n_own:]
    outs = refs[n:2 * n]
    sbuf, rbuf = refs[2 * n:5 * n], refs[5 * n:8 * n]
    obuf = refs[8 * n:8 * n + n_own]
    send_sems, recv_sems, load_sems, store_sems, own_sems, placed_sems = refs[8 * n + n_own:]
    x, y, c, chips = _position()
    fetched = [pltpu.make_async_copy(own[w], obuf[w], own_sems.at[w]) for w in range(n_own)]
    for cp in fetched:
        cp.start()
    region = lambda ref, w, r, half: _window(ref, axes[w], 2 * chips[r][0] + chips[r][1], N_CHIPS, half)
    pairs = [(w, r) for w in range(n) for r in range(3)]
    loads = [pltpu.make_async_copy(region(ins[w], w, r, c), sbuf[3 * w + r], load_sems.at[3 * w + r]) for w, r in pairs]
    for cp in loads:
        cp.start()
    placed = []
    for w, cp in enumerate(fetched):
        cp.wait()
        placed.append(pltpu.make_async_copy(obuf[w], _window(outs[w], axes[w], 2 * x + y, N_CHIPS), placed_sems.at[w]))
        placed[-1].start()
    sends = []
    for i, cp in enumerate(loads):
        cp.wait()
        sends.append(_remote(sbuf[i], rbuf[i], send_sems.at[i], recv_sems.at[i], (x, y, 1 - c)))
        sends[-1].start()
    stores = []
    for i, (w, r) in enumerate(pairs):
        sends[i].wait_recv()
        stores.append(pltpu.make_async_copy(rbuf[i], region(outs[w], w, r, 1 - c), store_sems.at[i]))
        stores[-1].start()
    for cp in sends:
        cp.wait_send()
    for cp in stores + placed:
        cp.wait()


def _pass_on(fulls, names, own=()):
    axes = tuple(SHARD_AXIS[k] for k in names)
    n = len(names)
    assert len(own) in (0, n)
    bufs = []
    for a, ax in zip(fulls, axes):
        s = list(a.shape)
        s[ax] //= N_CHIPS
        s[1 - ax] //= 2
        bufs += [pltpu.VMEM(tuple(s), a.dtype)] * 3
    return pl.pallas_call(
        functools.partial(_pass_on_body, axes=axes, n_own=len(own)), name="pass_on_" + names[0],
        in_specs=[HBM] * (n + len(own)), out_specs=[HBM] * n,
        out_shape=[jax.ShapeDtypeStruct(a.shape, a.dtype) for a in fulls],
        scratch_shapes=bufs + bufs + [pltpu.VMEM(o.shape, o.dtype) for o in own]
        + [pltpu.SemaphoreType.DMA((3 * n,))] * 4 + [pltpu.SemaphoreType.DMA((n,))] * 2,
        input_output_aliases={i: i for i in range(n)},
        compiler_params=pltpu.CompilerParams(vmem_limit_bytes=VMEM_LIMIT),
    )(*fulls, *own)


def _chip_exchange(pairs, names):
    axes = tuple(SHARD_AXIS[k] for k in names)
    n = len(names)
    out_shape = []
    for p, ax in zip(pairs, axes):
        s = list(p.shape)
        s[ax] //= N_CHIPS
        out_shape.append(jax.ShapeDtypeStruct((4, *s), p.dtype))

    def copies(ins, outs, sems):
        send_sems, recv_sems, local_sems = sems
        x, y, c, chips = _position()
        me = 2 * x + y
        remote, local = [], []
        for w, ax in enumerate(axes):
            for r, (px, py) in enumerate(chips):
                remote.append(_remote(_window(ins[w], ax, 2 * px + py, N_CHIPS), outs[w].at[r],
                                      send_sems.at[3 * w + r], recv_sems.at[3 * w + r], (px, py, c)))
            local.append(pltpu.make_async_copy(_window(ins[w], ax, me, N_CHIPS), outs[w].at[3], local_sems.at[w]))
        return remote, local

    return _Exchange(pairs, out_shape, (3 * n, 3 * n, n), copies)


def _half_tile(n, h, nt, axis):
    return h * nt + n if axis == 1 else 2 * n + h


def _sibling_stream(n, nt, stage, recv, send_sems, recv_sems, credit, produce, consume):
    x, y, c, _ = _position()
    sibling = (x, y, 1 - c)

    def copy(slot):
        return _remote(stage.at[slot], recv.at[slot], send_sems.at[slot], recv_sems.at[slot], sibling)

    @pl.when(n < nt)
    def _():
        slot = n % 2

        @pl.when(n >= 2)
        def _():
            copy(slot).wait_send()
            pl.semaphore_wait(credit, 1)

        stage[slot] = produce().astype(stage.dtype)
        copy(slot).start()

    @pl.when(n >= 1)
    def _():
        slot = (n - 1) % 2
        copy(slot).wait_recv()
        consume(recv[slot])

        @pl.when(n - 1 < nt - 2)
        def _():
            pl.semaphore_signal(credit, 1, device_id=sibling, device_id_type=MESH)

    @pl.when(n == nt)
    def _():
        for slot in range(min(2, nt)):
            copy(slot).wait_send()


def _pair_reduce_body(c_ref, mine_ref, theirs_ref, out_ref, stage, recv, send_sems, recv_sems, credit, *, nt):
    del c_ref

    def consume(got):
        out_ref[...] = (mine_ref[...] + got.astype(F32)).astype(out_ref.dtype)

    _sibling_stream(pl.program_id(0), nt, stage, recv, send_sems, recv_sems, credit,
                    lambda: theirs_ref[...], consume)


def _pair_reduce(grad, core, axis, *, name, rows):
    r, c = grad.shape
    nt = r // 2 // rows
    assert nt * rows * 2 == r and (axis == 1 or rows == r // (2 * N_CHIPS))
    tile = lambda n, h: _half_tile(n, h, nt, axis)
    return pl.pallas_call(
        functools.partial(_pair_reduce_body, nt=nt), name=name,
        grid_spec=pltpu.PrefetchScalarGridSpec(
            num_scalar_prefetch=1, grid=(nt + 1,),
            in_specs=[pl.BlockSpec((rows, c), lambda n, cr: (tile(jnp.maximum(n - 1, 0), cr[0]), 0)),
                      pl.BlockSpec((rows, c), lambda n, cr: (tile(jnp.minimum(n, nt - 1), 1 - cr[0]), 0))],
            out_specs=pl.BlockSpec((rows, c), lambda n, cr: (jnp.maximum(n - 1, 0), 0)),
            scratch_shapes=[pltpu.VMEM((2, rows, c), BF16), pltpu.VMEM((2, rows, c), BF16),
                            pltpu.SemaphoreType.DMA((2,)), pltpu.SemaphoreType.DMA((2,)),
                            pltpu.SemaphoreType.REGULAR]),
        out_shape=jax.ShapeDtypeStruct((r // 2, c), BF16),
        compiler_params=_params(("arbitrary",)),
    )(core, grad, grad)


def _chip_add_share_body(c_ref, s0, s1, s2, s3, mine_ref, other_ref, stage, recv, send_sems, recv_sems, credit, *, nt):
    del c_ref

    def produce():
        total = s3[...].astype(F32) + s0[...].astype(F32) + s1[...].astype(F32) + s2[...].astype(F32)
        mine_ref[...] = total
        return total

    def consume(got):
        other_ref[...] = got

    _sibling_stream(pl.program_id(0), nt, stage, recv, send_sems, recv_sems, credit, produce, consume)


def _chip_add_share(slots, core, *, name, rows):
    _, r, c = slots.shape
    nt = r // rows
    assert nt * rows == r
    send = lambda j: pl.BlockSpec((None, rows, c), lambda n, cr: (j, jnp.minimum(n, nt - 1), 0))
    return pl.pallas_call(
        functools.partial(_chip_add_share_body, nt=nt), name=name,
        grid_spec=pltpu.PrefetchScalarGridSpec(
            num_scalar_prefetch=1, grid=(nt + 1,),
            in_specs=[send(j) for j in range(4)],
            out_specs=[pl.BlockSpec((rows, c), lambda n, cr: (jnp.minimum(n, nt - 1), 0)),
                       pl.BlockSpec((rows, c), lambda n, cr: (jnp.maximum(n - 1, 0), 0))],
            scratch_shapes=[pltpu.VMEM((2, rows, c), F32), pltpu.VMEM((2, rows, c), F32),
                            pltpu.SemaphoreType.DMA((2,)), pltpu.SemaphoreType.DMA((2,)),
                            pltpu.SemaphoreType.REGULAR]),
        out_shape=[jax.ShapeDtypeStruct((r, c), F32)] * 2,
        compiler_params=_params(("arbitrary",)),
    )(core, slots, slots, slots, slots)


PAIR_ROWS = dict(w_in=64, w_glu=128, w_branch_ssm=128, w_branch_attn=128, w_out=256)
SHARE_ROWS = dict(w_in=128, w_glu=128, w_branch_ssm=128, w_branch_attn=128, w_out=64)


def _pair_sums(grads, names, core):
    return [_pair_reduce(grads[k], core, SHARD_AXIS[k], name="pair_reduce_" + k, rows=PAIR_ROWS[k]) for k in names]


def _shard_halves(slots, names, core):
    return {k: _chip_add_share(s, core, name="chip_add_share_" + k, rows=SHARE_ROWS[k]) for k, s in zip(names, slots)}


SMALL = (("ssm_lambda_re", (1, 64, 64)), ("ssm_lambda_im", (1, 64, 64)), ("ssm_b_re", (1, 64, 64, 16)),
         ("ssm_b_im", (1, 64, 64, 16)), ("ssm_c_re", (1, 64, 16, 64)), ("ssm_c_im", (1, 64, 16, 64)),
         ("ssm_d", (1, 1024)), ("ssm_log_step", (1, 64)), ("attn_sinks", (1, 16)), ("rel_bias_table", (32, 16)),
         ("ln_gain", (1, 2048)), ("ln_bias", (1, 2048)))
SMALL_SIZE = sum(int(np.prod(s)) for _, s in SMALL)
PACK_ROWS = -(-(SMALL_SIZE + 1) // (8 * 128)) * 8


def _pack(values, extra=None):
    flat = [values[k].reshape(-1).astype(F32) for k, _ in SMALL]
    flat.append(jnp.zeros((1,), F32) if extra is None else extra.reshape(1))
    flat.append(jnp.zeros((PACK_ROWS * 128 - SMALL_SIZE - 1,), F32))
    return jnp.concatenate(flat).reshape(PACK_ROWS, 128)


def _unpack(packed):
    flat = packed.reshape(-1)
    out, off = {}, 0
    for k, s in SMALL:
        size = int(np.prod(s))
        out[k] = flat[off:off + size].reshape(s)
        off += size
    return out, flat[off]


def _small_exchange(packed):
    def copies(ins, outs, sems):
        send_sems, recv_sems, local_sems = sems
        x, y, c, _ = _position()
        remote = []
        for r in range(1, 8):
            peer = tuple(1 - v if (r >> s) & 1 else v for v, s in ((x, 2), (y, 1), (c, 0)))
            remote.append(_remote(ins[0], outs[0].at[r], send_sems.at[r - 1], recv_sems.at[r - 1], peer))
        return remote, [pltpu.make_async_copy(ins[0], outs[0].at[0], local_sems.at[0])]

    return _Exchange([packed], [jax.ShapeDtypeStruct((8, *packed.shape), F32)], (7, 7, 1), copies)


def _merge_exchanges(a, b):
    na, ma, sa = len(a.ins), len(a.out_shape), len(a.n_sems)

    def copies(ins, outs, sems):
        ra, la = a.copies(ins[:na], outs[:ma], sems[:sa])
        rb, lb = b.copies(ins[na:], outs[ma:], sems[sa:])
        return ra + rb, la + lb

    return _Exchange(a.ins + b.ins, a.out_shape + b.out_shape, a.n_sems + b.n_sems, copies)


def _small_sum_body(slots_ref, o_ref):
    x, y, c, _ = _position()
    me = 4 * x + 2 * y + c
    acc = slots_ref[me]
    for s in range(1, 8):
        acc = acc + slots_ref[jnp.bitwise_xor(me, s)]
    o_ref[...] = acc


def _small_sum(slots):
    vm = pl.BlockSpec(memory_space=pltpu.VMEM)
    return pl.pallas_call(
        functools.partial(_small_sum_body), name="small_sum",
        in_specs=[vm], out_specs=vm, out_shape=jax.ShapeDtypeStruct(slots.shape[1:], F32),
        compiler_params=pltpu.CompilerParams(vmem_limit_bytes=VMEM_LIMIT),
    )(slots)


LATE = BIG[1:]


def _local_step(x, target, shards, core, chip, lam_re, lam_im, b_re, b_im, c_re, c_im, d_skip,
                log_step, sinks, rel_bias_table, ln_gain, ln_bias):
    t = x.shape[0]
    send_sems, recv_sems, shard_thru, land_thru, token = _gather_start(shards["w_in"], "w_in")
    xb, xbt = _cast_and_transpose(x, token, name="cast_x")
    bucket = jnp.asarray(_bucket_table())
    p_re, p_im, b_bd, c_bd = _ssm_prepare(lam_re, lam_im, b_re, b_im, c_re, c_im, log_step)
    bias = _bias_build(rel_bias_table, bucket)

    act = dict(out_dtype=BF16)
    proj = _proj_shards(xb, shards["w_in"], chip, (0,), name="proj_own")
    landed = _gather_wait(send_sems, recv_sems, shard_thru, land_thru, proj, "w_in")
    (w_in,) = _pass_on([landed], BIG[:1], own=[shards["w_in"]])
    proj, landed = _proj_shards(xb, w_in, chip, (1, 2, 3), name="proj", base=proj,
                                exchange=_gather_exchange(shards, LATE))
    w_glu, w_bs, w_ba, w_out = _pass_on(list(landed), LATE)
    y_ssm, g_in, g_in_t, car_re, car_im = _ssm_forward(proj, p_re, p_im, b_bd, c_bd, d_skip)
    glu = _matmul(g_in, w_glu, "nn", name="glu", tm=1024, tn=2048, **act)
    h_ssm, h_ssm_t = _rowwise(functools.partial(_ssm_gate_fwd_body), "ssm_gate_fwd", t,
                              [(glu, 2 * D_SSM, 0), (proj, D_SSM, C_ZS)],
                              [((t, D_SSM), BF16, D_SSM, 0), ((D_SSM, t), BF16, D_SSM, "T")])
    attn, lse = _attn_forward(proj, sinks, bias)
    h_attn, h_attn_t = _rowwise(functools.partial(_attn_gate_fwd_body), "attn_gate_fwd", t,
                                [(attn, D_ATTN, 0), (proj, D_ATTN, C_ZA)],
                                [((t, D_ATTN), BF16, D_ATTN, 0), ((D_ATTN, t), BF16, D_ATTN, "T")])
    bs = _matmul(h_ssm, w_bs, "nn", name="branch_ssm", tm=1024, tn=2048, **act)
    ba = _matmul(h_attn, w_ba, "nn", name="branch_attn", tm=1024, tn=2048, **act)
    gl_in = (proj, 2 * D_MODEL, C_GL)
    merged, merged_t = _rowwise(functools.partial(_merge_fwd_body), "merge_fwd", t,
                                [(bs, D_MODEL, 0), (ba, D_MODEL, 0), gl_in],
                                [((t, D_MODEL), BF16, D_MODEL, 0), ((D_MODEL, t), BF16, D_MODEL, "T")])
    out = _matmul(merged, w_out, "nn", name="out_proj", tm=1024, tn=1024, **act)
    drb, loss, g_gain, g_bias = _rowwise(
        functools.partial(_ln_loss_body), "ln_loss", t,
        [(x, D_MODEL, 0), (out, D_MODEL, 0), (target, D_MODEL, 0), (ln_gain, None, 0), (ln_bias, None, 0)],
        [((t, D_MODEL), BF16, D_MODEL, 0), ((1, 1), F32, None, 0),
         ((1, D_MODEL), F32, None, 0), ((1, D_MODEL), F32, None, 0)])

    g_w_out = _matmul(merged_t, drb, "nn", name="g_w_out", tm=512, tn=512)
    d_merged = _matmul(drb, w_out, "nt", name="d_merged", tm=1024, tn=1024, **act)
    d_bs, d_ba, dproj = _rowwise(
        functools.partial(_merge_bwd_body), "merge_bwd", t,
        [(d_merged, D_MODEL, 0), (bs, D_MODEL, 0), (ba, D_MODEL, 0), gl_in],
        [((t, D_MODEL), BF16, D_MODEL, 0), ((t, D_MODEL), BF16, D_MODEL, 0),
         ((t, D_IN), BF16, 2 * D_MODEL, C_GL)])
    g_w_bs = _matmul(h_ssm_t, d_bs, "nn", name="g_w_branch_ssm", tm=512, tn=512)
    d_h_ssm = _matmul(d_bs, w_bs, "nt", name="d_h_ssm", tm=1024, tn=1024, **act)
    g_w_ba = _matmul(h_attn_t, d_ba, "nn", name="g_w_branch_attn", tm=512, tn=512)
    d_h_attn = _matmul(d_ba, w_ba, "nt", name="d_h_attn", tm=1024, tn=1024, **act)

    d_attn, dproj = _rowwise(
        functools.partial(_attn_gate_bwd_body), "attn_gate_bwd", t,
        [(d_h_attn, D_ATTN, 0), (attn, D_ATTN, 0), (proj, D_ATTN, C_ZA), (dproj, None, "any")],
        [((t, D_ATTN), BF16, D_ATTN, 0), ((t, D_IN), BF16, D_ATTN, C_ZA)], aliases={3: 1})
    dproj, d_kv, g_bias_full, g_sinks = _attn_backward(proj, sinks, bias, d_attn, attn, lse, dproj)
    (dproj,) = _rowwise(functools.partial(_place_body), "place_dkv", t,
                        [(d_kv, 2 * D_KV, 0), (dproj, None, "any")],
                        [((t, D_IN), BF16, 2 * D_KV, C_K)], aliases={1: 0})
    g_table = _bias_grad(g_bias_full, bucket)

    d_glu, dproj = _rowwise(
        functools.partial(_ssm_gate_bwd_body), "ssm_gate_bwd", t,
        [(d_h_ssm, D_SSM, 0), (glu, 2 * D_SSM, 0), (proj, D_SSM, C_ZS), (dproj, None, "any")],
        [((t, 2 * D_SSM), BF16, 2 * D_SSM, 0), ((t, D_IN), BF16, D_SSM, C_ZS)], aliases={3: 1})
    g_w_glu = _matmul(g_in_t, d_glu, "nn", name="g_w_glu", tm=512, tn=512)
    d_g_in = _matmul(d_glu, w_glu, "nt", name="d_g_in", tm=1024, tn=1024, **act)
    dproj, g_bbd, g_cbd, g_lam_re, g_lam_im, g_d = _ssm_backward(
        proj, d_g_in, y_ssm, car_re, car_im, p_re, p_im, b_bd, c_bd, d_skip, dproj)
    g_lr, g_li, g_br, g_bi, g_cr, g_ci, g_ls = _ssm_param_grads(
        lam_re, lam_im, b_re, b_im, log_step, g_lam_re, g_lam_im, g_bbd, g_cbd)

    late = dict(w_glu=g_w_glu, w_branch_ssm=g_w_bs, w_branch_attn=g_w_ba, w_out=g_w_out)
    g_w_in, late_slots = _matmul(xbt, dproj, "nn", name="g_w_in", tm=1024, tn=512,
                                 exchange=_chip_exchange(_pair_sums(late, LATE, core), LATE))
    small = dict(ssm_lambda_re=g_lr, ssm_lambda_im=g_li, ssm_b_re=g_br, ssm_b_im=g_bi, ssm_c_re=g_cr,
                 ssm_c_im=g_ci, ssm_d=g_d, ssm_log_step=g_ls, attn_sinks=g_sinks[:, :N_Q_HEADS],
                 rel_bias_table=g_table, ln_gain=g_gain, ln_bias=g_bias)
    last = _merge_exchanges(_chip_exchange(_pair_sums(dict(w_in=g_w_in), BIG[:1], core), BIG[:1]),
                            _small_exchange(_pack(small, loss)))
    grad_x, (in_slots, small_slots) = _matmul(dproj, w_in, "nt", name="grad_x", tm=512, tn=512, res=drb,
                                              res_scale=ALPHA, exchange=last)
    big = {**_shard_halves([in_slots], BIG[:1], core), **_shard_halves(late_slots, LATE, core)}
    return grad_x, big, _small_sum(small_slots)


WEIGHTS = ("w_in", "ssm_lambda_re", "ssm_lambda_im", "ssm_b_re", "ssm_b_im", "ssm_c_re", "ssm_c_im", "ssm_d",
           "ssm_log_step", "w_glu", "attn_sinks", "rel_bias_table", "w_branch_ssm", "w_branch_attn", "w_out",
           "ln_gain", "ln_bias")


def kernel(x, w_in, ssm_lambda_re, ssm_lambda_im, ssm_b_re, ssm_b_im, ssm_c_re, ssm_c_im, ssm_d, ssm_log_step, w_glu, attn_sinks, rel_bias_table, w_branch_ssm, w_branch_attn, w_out, ln_gain, ln_bias, loss_target, m_w_in, m_ssm_lambda_re, m_ssm_lambda_im, m_ssm_b_re, m_ssm_b_im, m_ssm_c_re, m_ssm_c_im, m_ssm_d, m_ssm_log_step, m_w_glu, m_attn_sinks, m_rel_bias_table, m_w_branch_ssm, m_w_branch_attn, m_w_out, m_ln_gain, m_ln_bias, v_w_in, v_ssm_lambda_re, v_ssm_lambda_im, v_ssm_b_re, v_ssm_b_im, v_ssm_c_re, v_ssm_c_im, v_ssm_d, v_ssm_log_step, v_w_glu, v_attn_sinks, v_rel_bias_table, v_w_branch_ssm, v_w_branch_attn, v_w_out, v_ln_gain, v_ln_bias):
    given = dict(locals())
    w = {k: given[k] for k in WEIGHTS}
    m = {k: given["m_" + k] for k in WEIGHTS}
    v = {k: given["v_" + k] for k in WEIGHTS}

    core = lax.axis_index("c").astype(jnp.int32).reshape(1)
    shards = {k: _cast(w[k][0], BF16, name="cast_" + k) for k in BIG}
    chip = (2 * lax.axis_index("x") + lax.axis_index("y")).astype(jnp.int32).reshape(1)
    grad_x, g_shard, g_packed = _local_step(
        x[0], loss_target[0], shards, core, chip, ssm_lambda_re[0], ssm_lambda_im[0], ssm_b_re[0],
        ssm_b_im[0], ssm_c_re[0], ssm_c_im[0], ssm_d, ssm_log_step, attn_sinks, rel_bias_table, ln_gain, ln_bias)
    loss_sum = _unpack(g_packed)[1]

    grad, delta, new_m, new_v = {}, {}, {}, {}
    for k in BIG:
        grad[k], delta[k], new_m[k], new_v[k] = _adamw_shard(w[k], m[k], v[k], g_shard[k], core,
                                                             name="adamw_" + k, rows=SHARE_ROWS[k])
    gs, ds, ms, vs = _adamw(_pack(w), _pack(m), _pack(v), [g_packed], name="adamw_small")
    for dst, packed in ((grad, gs), (delta, ds), (new_m, ms), (new_v, vs)):
        dst.update(_unpack(packed)[0])

    return (loss_sum, grad_x[None], *[grad[k] for k in WEIGHTS], *[delta[k] for k in WEIGHTS],
            *[new_m[k] for k in WEIGHTS], *[new_v[k] for k in WEIGHTS])
```

```python
import functools
import math

import numpy as np
import jax
import jax.numpy as jnp
from jax import lax
from jax.experimental import pallas as pl
from jax.experimental.pallas import tpu as pltpu

F32 = jnp.float32
BF16 = jnp.bfloat16

D_MODEL = 2048
D_SSM = 1024
SSM_GROUP = 16
N_GROUPS = 64
SSM_STATE = 64
N_LANES = N_GROUPS * SSM_STATE
N_Q_HEADS = 16
N_KV_HEADS = 4
HEAD_DIM = 64
Q_PER_KV = 4
D_ATTN = 1024
D_KV = 256
WINDOW = 128
BLOCK = 128
N_BUCKETS = 32
MAX_DISTANCE = 128
D_IN = 8704
ALPHA = 2.0 ** 0.25
LN_EPS = 1e-5
NEG_INF = -1e30
ATTN_SCALE = HEAD_DIM ** -0.5

C_U, C_ZS, C_Q, C_K, C_V, C_ZA, C_GL = 0, 1024, 2048, 3072, 3328, 3584, 4608

ADAM_LR = 0.001
ADAM_B1 = 0.9
ADAM_B2 = 0.999
ADAM_EPS = 1e-08
ADAM_WD = 0.01
ADAM_STEP = 10

N_CHIPS = 4
MESH = pl.DeviceIdType.MESH

SSM_CHUNK = 256
SEG_LEN = SSM_CHUNK // 8
SLAB_LANES = 512
N_SLABS = N_LANES // SLAB_LANES
SLAB_CH = D_SSM // N_SLABS

VMEM_LIMIT = 60 * 1024 * 1024


def _params(sem=None, **kw):
    return pltpu.CompilerParams(dimension_semantics=sem, vmem_limit_bytes=VMEM_LIMIT, **kw)


_DIMS = {"nn": (((1,), (0,)), ((), ())), "nt": (((1,), (1,)), ((), ())), "tn": (((0,), (0,)), ((), ()))}


class _Exchange:
    def __init__(self, ins, out_shape, n_sems, copies):
        self.ins, self.out_shape, self.n_sems, self.copies = list(ins), list(out_shape), list(n_sems), copies

    def start(self, ins, outs, sems):
        remote, local = self.copies(ins, outs, sems)
        for cp in remote + local:
            cp.start()

    def finish(self, ins, outs, sems):
        remote, local = self.copies(ins, outs, sems)
        for cp in remote:
            cp.wait_recv()
        for cp in remote:
            cp.wait_send()
        for cp in local:
            cp.wait()


def _mm_body(*refs, dims, nk, res_scale, exchange, grid):
    n_ex_in = len(exchange.ins) if exchange else 0
    n_ex_out = len(exchange.out_shape) if exchange else 0
    n_in = 2 + (res_scale is not None) + n_ex_in
    a_ref, b_ref = refs[0], refs[1]
    r_ref = refs[2] if res_scale is not None else None
    ex_in = refs[n_in - n_ex_in:n_in]
    o_ref, ex_out = refs[n_in], refs[n_in + 1:n_in + 1 + n_ex_out]
    acc_ref, ex_sems = refs[n_in + 1 + n_ex_out], refs[n_in + 2 + n_ex_out:]
    i, j, k = pl.program_id(0), pl.program_id(1), pl.program_id(2)

    if exchange:
        @pl.when(jnp.logical_and(jnp.logical_and(i == 0, j == 0), k == 0))
        def _():
            exchange.start(ex_in, ex_out, ex_sems)

    part = lax.dot_general(a_ref[...].astype(BF16), b_ref[...].astype(BF16), _DIMS[dims],
                           preferred_element_type=F32)

    def finish(acc):
        if r_ref is not None:
            acc = acc + res_scale * r_ref[...].astype(F32)
        o_ref[...] = acc.astype(o_ref.dtype)

    if nk == 1:
        finish(part)
    else:
        @pl.when(k == 0)
        def _():
            acc_ref[...] = part

        @pl.when(k > 0)
        def _():
            acc_ref[...] += part

        @pl.when(k == nk - 1)
        def _():
            finish(acc_ref[...])

    if exchange:
        @pl.when(jnp.logical_and(jnp.logical_and(i == grid[0] - 1, j == grid[1] - 1), k == grid[2] - 1))
        def _():
            exchange.finish(ex_in, ex_out, ex_sems)


def _matmul(a, b, dims, *, name, out_dtype=F32, tm=512, tn=512, tk=None, res=None, res_scale=None, exchange=None):
    if dims == "nn":
        (m, kk), n = a.shape, b.shape[1]
    elif dims == "nt":
        (m, kk), n = a.shape, b.shape[0]
    else:
        (kk, m), n = a.shape, b.shape[1]
    tm, tn = min(tm, m), min(tn, n)
    tk = kk if tk is None else min(tk, kk)
    assert m % tm == 0 and n % tn == 0 and kk % tk == 0, (name, m, n, kk, tm, tn, tk)
    nk = kk // tk
    a_spec = {"nn": pl.BlockSpec((tm, tk), lambda i, j, k: (i, k)),
              "nt": pl.BlockSpec((tm, tk), lambda i, j, k: (i, k)),
              "tn": pl.BlockSpec((tk, tm), lambda i, j, k: (k, i))}[dims]
    b_spec = {"nn": pl.BlockSpec((tk, tn), lambda i, j, k: (k, j)),
              "nt": pl.BlockSpec((tn, tk), lambda i, j, k: (j, k)),
              "tn": pl.BlockSpec((tk, tn), lambda i, j, k: (k, j))}[dims]
    in_specs, args = [a_spec, b_spec], [a, b]
    if res is not None:
        in_specs.append(pl.BlockSpec((tm, tn), lambda i, j, k: (i, j)))
        args.append(res)
    grid = (m // tm, n // tn, nk)
    out_specs = [pl.BlockSpec((tm, tn), lambda i, j, k: (i, j))]
    out_shape = [jax.ShapeDtypeStruct((m, n), out_dtype)]
    scratch = [pltpu.VMEM((tm, tn), F32)]
    if exchange:
        in_specs += [pl.BlockSpec(memory_space=pl.ANY)] * len(exchange.ins)
        args += exchange.ins
        out_specs += [pl.BlockSpec(memory_space=pl.ANY)] * len(exchange.out_shape)
        out_shape += exchange.out_shape
        scratch += [pltpu.SemaphoreType.DMA((s,)) for s in exchange.n_sems]
    outs = pl.pallas_call(
        functools.partial(_mm_body, dims=dims, nk=nk, res_scale=res_scale if res is not None else None,
                          exchange=exchange, grid=grid),
        name=name, grid=grid, in_specs=in_specs, out_specs=out_specs, out_shape=out_shape, scratch_shapes=scratch,
        compiler_params=_params(("arbitrary",) * 3 if exchange else ("parallel", "parallel", "arbitrary")),
    )(*args)
    return (outs[0], outs[1:]) if exchange else outs[0]


def _proj_shards_body(chip_ref, *refs, n_ex_in, has_base, exchange, grid):
    del chip_ref
    refs = list(refs)
    if has_base:
        del refs[2 + n_ex_in]
    _mm_body(*refs, dims="nn", nk=1, res_scale=None, exchange=exchange, grid=grid)


def _proj_shards(xb, w, chip, offsets, *, name, base=None, exchange=None, tm=1024):
    t, d = xb.shape
    cs = D_IN // N_CHIPS
    own = w.shape[1] == cs
    tm = min(tm, t)
    col = lambda j, cr: (cr[0] + offsets[0] + j) % N_CHIPS
    grid = (len(offsets), t // tm, 1)
    in_specs = [pl.BlockSpec((tm, d), lambda j, i, k, cr: (i, 0)),
                pl.BlockSpec((d, cs), lambda j, i, k, cr: (0, 0 if own else col(j, cr)))]
    args = [xb, w]
    out_specs = [pl.BlockSpec((tm, cs), lambda j, i, k, cr: (i, col(j, cr)))]
    out_shape = [jax.ShapeDtypeStruct((t, D_IN), BF16)]
    scratch = [pltpu.VMEM((8, 128), F32)]
    n_ex_in = 0
    if exchange:
        n_ex_in = len(exchange.ins)
        in_specs += [pl.BlockSpec(memory_space=pl.ANY)] * n_ex_in
        args += exchange.ins
        out_specs += [pl.BlockSpec(memory_space=pl.ANY)] * len(exchange.out_shape)
        out_shape += exchange.out_shape
        scratch += [pltpu.SemaphoreType.DMA((s,)) for s in exchange.n_sems]
    aliases = {}
    if base is not None:
        in_specs.append(pl.BlockSpec(memory_space=pl.ANY))
        args.append(base)
        aliases = {len(args): 0}
    outs = pl.pallas_call(
        functools.partial(_proj_shards_body, n_ex_in=n_ex_in, has_base=base is not None, exchange=exchange, grid=grid),
        name=name,
        grid_spec=pltpu.PrefetchScalarGridSpec(num_scalar_prefetch=1, grid=grid, in_specs=in_specs,
                                               out_specs=out_specs, scratch_shapes=scratch),
        out_shape=out_shape, input_output_aliases=aliases,
        compiler_params=_params(("arbitrary",) * 3),
    )(chip, *args)
    return (outs[0], outs[1:]) if exchange else outs[0]


def _sigmoid(v):
    return 1.0 / (1.0 + jnp.exp(-v))


def _silu_and_grad(z):
    s = _sigmoid(z)
    return z * s, s * (1.0 + z * (1.0 - s))


def _cast_body(x_ref, o_ref):
    o_ref[...] = x_ref[...].astype(o_ref.dtype)


def _cast_and_transpose_body(x_ref, after_ref, o_ref, ot_ref):
    del after_ref
    xb = x_ref[...].astype(BF16)
    o_ref[...] = xb
    ot_ref[...] = xb.T


def _cast_and_transpose(x, after, *, name, rows=512):
    m, n = x.shape
    rows = min(rows, m)
    return pl.pallas_call(
        functools.partial(_cast_and_transpose_body), name=name, grid=(m // rows,),
        in_specs=[pl.BlockSpec((rows, n), lambda i: (i, 0)), pl.BlockSpec(memory_space=pl.ANY)],
        out_specs=[pl.BlockSpec((rows, n), lambda i: (i, 0)), pl.BlockSpec((n, rows), lambda i: (0, i))],
        out_shape=[jax.ShapeDtypeStruct((m, n), BF16), jax.ShapeDtypeStruct((n, m), BF16)],
        compiler_params=_params(("parallel",)),
    )(x, after)


def _cast(x, dtype, *, name, rows=512):
    m, n = x.shape
    rows = min(rows, m)
    return pl.pallas_call(
        functools.partial(_cast_body), name=name, grid=(m // rows,),
        in_specs=[pl.BlockSpec((rows, n), lambda i: (i, 0))],
        out_specs=pl.BlockSpec((rows, n), lambda i: (i, 0)),
        out_shape=jax.ShapeDtypeStruct((m, n), dtype),
        compiler_params=_params(("parallel",)),
    )(x)


def _lam_bar(lr, li, ls):
    step = jnp.exp(ls)
    er = jnp.exp(lr * step)
    return step, er * jnp.cos(li * step), er * jnp.sin(li * step)


def _ssm_pow_body(lr_ref, li_ref, ls_ref, pr_ref, pi_ref):
    _, ar, ai = _lam_bar(lr_ref[...], li_ref[...], ls_ref[...])
    cr, ci = ar, ai
    for i in range(SEG_LEN):
        pr_ref[pl.ds(i, 1), :] = cr
        pi_ref[pl.ds(i, 1), :] = ci
        cr, ci = cr * ar - ci * ai, cr * ai + ci * ar


def _ssm_bbar_body(lr_ref, li_ref, ls_ref, br_ref, bi_ref, or_ref, oi_ref):
    lr, li = lr_ref[...], li_ref[...]
    _, ar, ai = _lam_bar(lr, li, ls_ref[...])
    d = lr * lr + li * li
    ir, ii = lr / d, -li / d
    nr, ni = ar - 1.0, ai
    cr, ci = nr * ir - ni * ii, nr * ii + ni * ir
    br, bi = br_ref[...], bi_ref[...]
    or_ref[...] = cr * br - ci * bi
    oi_ref[...] = cr * bi + ci * br


def _ssm_param_bwd_body(lr_ref, li_ref, ls_ref, br_ref, bi_ref, glr_ref, gli_ref, gbr_ref, gbi_ref,
                        dlr_ref, dli_ref, dls_ref, dbr_ref, dbi_ref):
    lr, li = lr_ref[...], li_ref[...]
    step, ar, ai = _lam_bar(lr, li, ls_ref[...])
    d = lr * lr + li * li
    ir, ii = lr / d, -li / d
    nr, ni = ar - 1.0, ai
    cr, ci = nr * ir - ni * ii, nr * ii + ni * ir
    br, bi, gbr, gbi = br_ref[...], bi_ref[...], gbr_ref[...], gbi_ref[...]
    dbr_ref[...] = cr * gbr + ci * gbi
    dbi_ref[...] = cr * gbi - ci * gbr
    gcr = jnp.sum(br * gbr + bi * gbi, axis=1, keepdims=True)
    gci = jnp.sum(br * gbi - bi * gbr, axis=1, keepdims=True)
    gnr, gni = ir * gcr + ii * gci, ir * gci - ii * gcr
    gir, gii = nr * gcr + ni * gci, nr * gci - ni * gcr
    gtr, gti = glr_ref[...] + gnr, gli_ref[...] + gni
    i2r, i2i = ir * ir - ii * ii, 2.0 * ir * ii
    g1r, g1i = -(i2r * gir + i2i * gii), -(i2r * gii - i2i * gir)
    g2r, g2i = step * (ar * gtr + ai * gti), step * (ar * gti - ai * gtr)
    mr, mi = lr * ar - li * ai, lr * ai + li * ar
    dlr_ref[...] = g1r + g2r
    dli_ref[...] = g1i + g2i
    dls_ref[...] = (mr * gtr + mi * gti) * step


def _whole(shape):
    return pl.BlockSpec(shape, lambda *_: (0,) * len(shape))


def _ssm_prepare(lam_re, lam_im, b_re, b_im, c_re, c_im, log_step):
    row = lambda a: a.reshape(1, N_LANES)
    col = lambda a: a.reshape(N_LANES, 1)
    ls = jnp.repeat(log_step.reshape(N_GROUPS), SSM_STATE)
    p_re, p_im = pl.pallas_call(
        functools.partial(_ssm_pow_body), name="ssm_pow",
        in_specs=[_whole((1, N_LANES))] * 3, out_specs=[_whole((SEG_LEN, N_LANES))] * 2,
        out_shape=[jax.ShapeDtypeStruct((SEG_LEN, N_LANES), F32)] * 2, grid=(1,),
    )(row(lam_re), row(lam_im), row(ls))
    bb_re, bb_im = pl.pallas_call(
        functools.partial(_ssm_bbar_body), name="ssm_bbar",
        in_specs=[_whole((N_LANES, 1))] * 3 + [_whole((N_LANES, SSM_GROUP))] * 2,
        out_specs=[_whole((N_LANES, SSM_GROUP))] * 2,
        out_shape=[jax.ShapeDtypeStruct((N_LANES, SSM_GROUP), F32)] * 2, grid=(1,),
    )(col(lam_re), col(lam_im), col(ls), b_re.reshape(N_LANES, SSM_GROUP), b_im.reshape(N_LANES, SSM_GROUP))
    eye = jnp.eye(8, dtype=F32)

    def b_slabs(bb):
        t = bb.reshape(N_SLABS, 8, SSM_STATE, SSM_GROUP).transpose(0, 1, 3, 2)
        return (t[:, :, :, None, :] * eye[None, :, None, :, None]).reshape(N_SLABS, SLAB_CH, SLAB_LANES)

    def c_slabs(c):
        t = c.reshape(N_SLABS, 8, SSM_GROUP, SSM_STATE).transpose(0, 1, 3, 2)
        return (t[:, :, :, None, :] * eye[None, :, None, :, None]).reshape(N_SLABS, SLAB_LANES, SLAB_CH)

    b_bd = jnp.concatenate([b_slabs(bb_re), b_slabs(bb_im)], axis=2).astype(BF16)
    c_bd = jnp.concatenate([c_slabs(c_re.reshape(N_GROUPS, SSM_GROUP, SSM_STATE)),
                            -c_slabs(c_im.reshape(N_GROUPS, SSM_GROUP, SSM_STATE))], axis=1).astype(BF16)
    return p_re, p_im, b_bd, c_bd


def _diag_blocks_b(g):
    t = g.reshape(N_SLABS, 8, SSM_GROUP, 8, SSM_STATE)
    t = jnp.stack([t[:, i, :, i, :] for i in range(8)], axis=1)
    return t.transpose(0, 1, 3, 2).reshape(N_LANES, SSM_GROUP)


def _diag_blocks_c(g):
    t = g.reshape(N_SLABS, 8, SSM_STATE, 8, SSM_GROUP)
    t = jnp.stack([t[:, i, :, i, :] for i in range(8)], axis=1)
    return t.transpose(0, 1, 3, 2).reshape(N_GROUPS, SSM_GROUP, SSM_STATE)


def _ssm_param_grads(lam_re, lam_im, b_re, b_im, log_step, g_lam_re, g_lam_im, g_bbd, g_cbd):
    col = lambda a: a.reshape(N_LANES, 1)
    ls = jnp.repeat(log_step.reshape(N_GROUPS), SSM_STATE)
    gbr = _diag_blocks_b(g_bbd[:, :, :SLAB_LANES])
    gbi = _diag_blocks_b(g_bbd[:, :, SLAB_LANES:])
    outs = pl.pallas_call(
        functools.partial(_ssm_param_bwd_body), name="ssm_param_bwd", grid=(1,),
        in_specs=[_whole((N_LANES, 1))] * 3 + [_whole((N_LANES, SSM_GROUP))] * 2 + [_whole((N_LANES, 1))] * 2
        + [_whole((N_LANES, SSM_GROUP))] * 2,
        out_specs=[_whole((N_LANES, 1))] * 3 + [_whole((N_LANES, SSM_GROUP))] * 2,
        out_shape=[jax.ShapeDtypeStruct((N_LANES, 1), F32)] * 3 + [jax.ShapeDtypeStruct((N_LANES, SSM_GROUP), F32)] * 2,
    )(col(lam_re), col(lam_im), col(ls), b_re.reshape(N_LANES, SSM_GROUP), b_im.reshape(N_LANES, SSM_GROUP),
      col(g_lam_re), col(g_lam_im), gbr, gbi)
    dlr, dli, dls, dbr, dbi = outs
    d_c_re = _diag_blocks_c(g_cbd[:, :SLAB_LANES, :])
    d_c_im = -_diag_blocks_c(g_cbd[:, SLAB_LANES:, :])
    return (dlr.reshape(1, N_GROUPS, SSM_STATE), dli.reshape(1, N_GROUPS, SSM_STATE),
            dbr.reshape(1, N_GROUPS, SSM_STATE, SSM_GROUP), dbi.reshape(1, N_GROUPS, SSM_STATE, SSM_GROUP),
            d_c_re[None], d_c_im[None], dls.reshape(N_GROUPS, SSM_STATE).sum(axis=1).reshape(1, N_GROUPS))


def _bcast8(v):
    return jnp.broadcast_to(v, (8, v.shape[1]))


def _segment_permutation():
    p = np.zeros((SSM_CHUNK, SSM_CHUNK), np.float32)
    rows = np.arange(SSM_CHUNK)
    p[rows, (rows % 8) * SEG_LEN + rows // 8] = 1.0
    return p


def _permute_exact(perm, val, pieces):
    out, rest = None, val
    for n in range(pieces):
        part = rest.astype(BF16)
        moved = jnp.dot(perm, part, preferred_element_type=F32)
        out = moved if out is None else out + moved
        if n + 1 < pieces:
            rest = rest - part.astype(F32)
    return out


def _scan_buffer():
    return pltpu.VMEM((SSM_CHUNK, N_LANES), F32)


def _lanes(k):
    return pl.ds(k * SLAB_LANES, SLAB_LANES)


def _tile(i):
    return pl.ds(i * 8 if isinstance(i, int) else pl.multiple_of(i * 8, 8), 8)


SCAN_LANES = 1024


def _scan_lanes(k):
    return pl.ds(k * SCAN_LANES, SCAN_LANES)


def _seg_get(ref, k, i):
    return ref[_tile(i), _scan_lanes(k)]


def _seg_put(ref, k, i, val):
    ref[_tile(i), _scan_lanes(k)] = val


def _slab_get(ref, k):
    return ref[:, _lanes(k)]


def _slab_put(ref, k, val):
    ref[:, _lanes(k)] = val


def _scan_forward(s_re, s_im, p_re, p_im, car_re, car_im, sp_re=None, sp_im=None):
    for k in range(N_LANES // SCAN_LANES):
        ln = _scan_lanes(k)
        ar, ai =_bcast8(p_re[pl.ds(0, 1), ln]), _bcast8(p_im[pl.ds(0, 1), ln])

        def step(i, s, k=k, ar=ar, ai=ai):
            sr, si = s
            nr = ar * sr - ai * si + _seg_get(s_re, k, i)
            ni = ar * si + ai * sr + _seg_get(s_im, k, i)
            _seg_put(s_re, k, i, nr)
            _seg_put(s_im, k, i, ni)
            return nr, ni

        zero = jnp.zeros((8, SCAN_LANES), F32)
        er, ei = lax.fori_loop(0, SEG_LEN, step, (zero, zero), unroll=4)
        lr, li = p_re[pl.ds(SEG_LEN - 1, 1), ln], p_im[pl.ds(SEG_LEN - 1, 1), ln]
        cr, ci = car_re[pl.ds(0, 1), ln], car_im[pl.ds(0, 1), ln]
        rows_r, rows_i = [], []
        for r in range(8):
            rows_r.append(cr)
            rows_i.append(ci)
            cr, ci = er[r:r + 1] + lr * cr - li * ci, ei[r:r + 1] + lr * ci + li * cr
        pr8, pi8 = jnp.concatenate(rows_r, axis=0), jnp.concatenate(rows_i, axis=0)
        car_re[:, ln] = _bcast8(cr)
        car_im[:, ln] = _bcast8(ci)
        if sp_re is not None:
            sp_re[:, ln] = pr8
            sp_im[:, ln] = pi8

        def fix(i, _, k=k, ln=ln, pr8=pr8, pi8=pi8):
            qr, qi = _bcast8(p_re[pl.ds(i, 1), ln]), _bcast8(p_im[pl.ds(i, 1), ln])
            _seg_put(s_re, k, i, _seg_get(s_re, k, i) + qr * pr8 - qi * pi8)
            _seg_put(s_im, k, i, _seg_get(s_im, k, i) + qr * pi8 + qi * pr8)
            return 0

        lax.fori_loop(0, SEG_LEN, fix, 0, unroll=4)


def _scan_backward(g_re, g_im, s_re, s_im, sp_re, sp_im, p_re, p_im, car_re, car_im, acc_re, acc_im):
    for k in range(N_LANES // SCAN_LANES):
        ln = _scan_lanes(k)
        ar, ai =_bcast8(p_re[pl.ds(0, 1), ln]), -_bcast8(p_im[pl.ds(0, 1), ln])

        def step(j, s, k=k, ar=ar, ai=ai):
            i = SEG_LEN - 1 - j
            sr, si = s
            nr = ar * sr - ai * si + _seg_get(g_re, k, i)
            ni = ar * si + ai * sr + _seg_get(g_im, k, i)
            _seg_put(g_re, k, i, nr)
            _seg_put(g_im, k, i, ni)
            return nr, ni

        zero = jnp.zeros((8, SCAN_LANES), F32)
        er, ei = lax.fori_loop(0, SEG_LEN, step, (zero, zero), unroll=4)
        lr, li = p_re[pl.ds(SEG_LEN - 1, 1), ln], -p_im[pl.ds(SEG_LEN - 1, 1), ln]
        cr, ci = car_re[pl.ds(0, 1), ln], car_im[pl.ds(0, 1), ln]
        rows_r, rows_i = [None] * 8, [None] * 8
        for r in range(7, -1, -1):
            rows_r[r], rows_i[r] = cr, ci
            cr, ci = er[r:r + 1] + lr * cr - li * ci, ei[r:r + 1] + lr * ci + li * cr
        nr8, ni8 = jnp.concatenate(rows_r, axis=0), jnp.concatenate(rows_i, axis=0)
        car_re[:, ln] = _bcast8(cr)
        car_im[:, ln] = _bcast8(ci)

        def fix(i, acc, k=k, ln=ln, nr8=nr8, ni8=ni8):
            qr = _bcast8(p_re[pl.ds(SEG_LEN - 1 - i, 1), ln])
            qi = -_bcast8(p_im[pl.ds(SEG_LEN - 1 - i, 1), ln])
            gr = _seg_get(g_re, k, i) + qr * nr8 - qi * ni8
            gi = _seg_get(g_im, k, i) + qr * ni8 + qi * nr8
            _seg_put(g_re, k, i, gr)
            _seg_put(g_im, k, i, gi)
            return gr, gi

        def prod(xr, xi, gr, gi):
            return xr * gr + xi * gi, xr * gi - xi * gr

        gr, gi = fix(0, None)
        a_r, a_i = prod(sp_re[:, ln], sp_im[:, ln], gr, gi)

        def fix_acc(i, acc, k=k, fix=fix):
            gr, gi = fix(i, None)
            dr, di = prod(_seg_get(s_re, k, i - 1), _seg_get(s_im, k, i - 1), gr, gi)
            return acc[0] + dr, acc[1] + di

        acc = (a_r, a_i)
        for i in range(1, 4):
            acc = fix_acc(i, acc)
        a_r, a_i = lax.fori_loop(4, SEG_LEN, fix_acc, acc, unroll=4)
        acc_re[:, ln] += a_r
        acc_im[:, ln] += a_i


def _gelu_and_grad(y):
    cdf = 0.5 * (1.0 + lax.erf(y * (2.0 ** -0.5)))
    pdf = jnp.exp(-0.5 * y * y) * (1.0 / math.sqrt(2.0 * math.pi))
    return y * cdf, cdf + y * pdf


def _ssm_fwd_body(u_ref, bbd_ref, cbd_ref, pre_ref, pim_ref, d_ref, perm_ref, unperm_ref,
                  y_ref, gin_ref, gint_ref, cre_out, cim_out, s_re, s_im, car_re, car_im, yp):
    c = pl.program_id(0)

    @pl.when(c == 0)
    def _():
        car_re[...] = jnp.zeros_like(car_re)
        car_im[...] = jnp.zeros_like(car_im)

    cre_out[...] = car_re[...]
    cim_out[...] = car_im[...]
    u = u_ref[...].astype(F32)
    up = jnp.dot(perm_ref[...], u_ref[...].astype(BF16), preferred_element_type=F32).astype(BF16)
    for k in range(N_SLABS):
        bu = jnp.dot(up[:, k * SLAB_CH:(k + 1) * SLAB_CH], bbd_ref[k], preferred_element_type=F32)
        _slab_put(s_re, k, bu[:, :SLAB_LANES])
        _slab_put(s_im, k, bu[:, SLAB_LANES:])
    _scan_forward(s_re, s_im, pre_ref, pim_ref, car_re, car_im)
    for k in range(N_SLABS):
        yp[:, pl.ds(k * SLAB_CH, SLAB_CH)] = (
            jnp.dot(_slab_get(s_re, k).astype(BF16), cbd_ref[k, :SLAB_LANES, :], preferred_element_type=F32)
            + jnp.dot(_slab_get(s_im, k).astype(BF16), cbd_ref[k, SLAB_LANES:, :], preferred_element_type=F32))
    y = _permute_exact(unperm_ref[...], yp[...], 1) + d_ref[...] * u
    y_ref[...] = y.astype(y_ref.dtype)
    _store_with_transpose(_gelu_and_grad(y)[0], gin_ref, gint_ref)


def _ssm_forward(proj, p_re, p_im, b_bd, c_bd, d_skip):
    t = proj.shape[0]
    nc = t // SSM_CHUNK
    perm = _segment_permutation()
    return pl.pallas_call(
        functools.partial(_ssm_fwd_body), name="ssm_fwd", grid=(nc,),
        in_specs=[pl.BlockSpec((SSM_CHUNK, D_SSM), lambda c: (c, C_U // D_SSM)),
                  _whole(b_bd.shape), _whole(c_bd.shape), _whole(p_re.shape), _whole(p_im.shape),
                  _whole((1, D_SSM)), _whole(perm.shape), _whole(perm.shape)],
        out_specs=[pl.BlockSpec((SSM_CHUNK, D_SSM), lambda c: (c, 0)),
                   pl.BlockSpec((SSM_CHUNK, D_SSM), lambda c: (c, 0)),
                   pl.BlockSpec((D_SSM, SSM_CHUNK), lambda c: (0, c)),
                   pl.BlockSpec((None, 8, N_LANES), lambda c: (c, 0, 0)),
                   pl.BlockSpec((None, 8, N_LANES), lambda c: (c, 0, 0))],
        out_shape=[jax.ShapeDtypeStruct((t, D_SSM), BF16), jax.ShapeDtypeStruct((t, D_SSM), BF16),
                   jax.ShapeDtypeStruct((D_SSM, t), BF16),
                   jax.ShapeDtypeStruct((nc, 8, N_LANES), F32), jax.ShapeDtypeStruct((nc, 8, N_LANES), F32)],
        scratch_shapes=[_scan_buffer(), _scan_buffer(),
                        pltpu.VMEM((8, N_LANES), F32), pltpu.VMEM((8, N_LANES), F32),
                        pltpu.VMEM((SSM_CHUNK, D_SSM), F32)],
        compiler_params=_params(("arbitrary",)),
    )(proj, b_bd, c_bd, p_re, p_im, d_skip, jnp.asarray(perm, BF16), jnp.asarray(perm.T, BF16))


def _ssm_bwd_body(u_ref, dgin_ref, y_ref, cre_in, cim_in, bbd_ref, cbd_ref, pre_ref, pim_ref, d_ref, perm_ref,
                  unperm_ref, dproj_in,
                  du_ref, gb_ref, gc_ref, glr_ref, gli_ref, gd_ref,
                  s_re, s_im, g_re, g_im, sp_re, sp_im, car_re, car_im, gcar_re, gcar_im, acc_re, acc_im, dup):
    del dproj_in
    c = pl.program_id(0)
    nc = pl.num_programs(0)

    @pl.when(c == 0)
    def _():
        gcar_re[...] = jnp.zeros_like(gcar_re)
        gcar_im[...] = jnp.zeros_like(gcar_im)
        acc_re[...] = jnp.zeros_like(acc_re)
        acc_im[...] = jnp.zeros_like(acc_im)
        gb_ref[...] = jnp.zeros_like(gb_ref)
        gc_ref[...] = jnp.zeros_like(gc_ref)
        gd_ref[...] = jnp.zeros_like(gd_ref)

    car_re[...] = cre_in[...]
    car_im[...] = cim_in[...]
    u = u_ref[...].astype(F32)
    dy = dgin_ref[...].astype(F32) * _gelu_and_grad(y_ref[...].astype(F32))[1]
    gd_ref[...] += jnp.sum(dy * u, axis=0, keepdims=True)
    up = jnp.dot(perm_ref[...], u.astype(BF16), preferred_element_type=F32).astype(BF16)
    dyp = jnp.dot(perm_ref[...], dy.astype(BF16), preferred_element_type=F32).astype(BF16)
    for k in range(N_SLABS):
        ch = slice(k * SLAB_CH, (k + 1) * SLAB_CH)
        bu = jnp.dot(up[:, ch], bbd_ref[k], preferred_element_type=F32)
        _slab_put(s_re, k, bu[:, :SLAB_LANES])
        _slab_put(s_im, k, bu[:, SLAB_LANES:])
        ds = lax.dot_general(dyp[:, ch], cbd_ref[k], _DIMS["nt"], preferred_element_type=F32)
        _slab_put(g_re, k, ds[:, :SLAB_LANES])
        _slab_put(g_im, k, ds[:, SLAB_LANES:])
    _scan_forward(s_re, s_im, pre_ref, pim_ref, car_re, car_im, sp_re, sp_im)
    _scan_backward(g_re, g_im, s_re, s_im, sp_re, sp_im, pre_ref, pim_ref, gcar_re, gcar_im, acc_re, acc_im)
    for k in range(N_SLABS):
        ch = slice(k * SLAB_CH, (k + 1) * SLAB_CH)
        uk, dyk = up[:, ch], dyp[:, ch]
        sr, si = _slab_get(s_re, k).astype(BF16), _slab_get(s_im, k).astype(BF16)
        gr, gi = _slab_get(g_re, k).astype(BF16), _slab_get(g_im, k).astype(BF16)
        gc_ref[k, :SLAB_LANES, :] += lax.dot_general(sr, dyk, _DIMS["tn"], preferred_element_type=F32)
        gc_ref[k, SLAB_LANES:, :] += lax.dot_general(si, dyk, _DIMS["tn"], preferred_element_type=F32)
        gb_ref[k, :, :SLAB_LANES] += lax.dot_general(uk, gr, _DIMS["tn"], preferred_element_type=F32)
        gb_ref[k, :, SLAB_LANES:] += lax.dot_general(uk, gi, _DIMS["tn"], preferred_element_type=F32)
        dup[:, pl.ds(k * SLAB_CH, SLAB_CH)] = (
            lax.dot_general(gr, bbd_ref[k, :, :SLAB_LANES], _DIMS["nt"], preferred_element_type=F32)
            + lax.dot_general(gi, bbd_ref[k, :, SLAB_LANES:], _DIMS["nt"], preferred_element_type=F32))
    du = _permute_exact(unperm_ref[...], dup[...], 1) + d_ref[...] * dy
    du_ref[...] = du.astype(du_ref.dtype)

    @pl.when(c == nc - 1)
    def _():
        glr_ref[...] = jnp.sum(acc_re[...], axis=0, keepdims=True)
        gli_ref[...] = jnp.sum(acc_im[...], axis=0, keepdims=True)


def _ssm_backward(proj, dg_in, y_ssm, car_re, car_im, p_re, p_im, b_bd, c_bd, d_skip, dproj):
    t = proj.shape[0]
    nc = t // SSM_CHUNK
    rev = lambda c: nc - 1 - c
    big = _scan_buffer
    small = lambda: pltpu.VMEM((8, N_LANES), F32)
    perm = _segment_permutation()
    outs = pl.pallas_call(
        functools.partial(_ssm_bwd_body), name="ssm_bwd", grid=(nc,),
        in_specs=[pl.BlockSpec((SSM_CHUNK, D_SSM), lambda c: (rev(c), C_U // D_SSM)),
                  pl.BlockSpec((SSM_CHUNK, D_SSM), lambda c: (rev(c), 0)),
                  pl.BlockSpec((SSM_CHUNK, D_SSM), lambda c: (rev(c), 0)),
                  pl.BlockSpec((None, 8, N_LANES), lambda c: (rev(c), 0, 0)),
                  pl.BlockSpec((None, 8, N_LANES), lambda c: (rev(c), 0, 0)),
                  _whole(b_bd.shape), _whole(c_bd.shape), _whole(p_re.shape), _whole(p_im.shape),
                  _whole((1, D_SSM)), _whole(perm.shape), _whole(perm.shape), pl.BlockSpec(memory_space=pl.ANY)],
        out_specs=[pl.BlockSpec((SSM_CHUNK, D_SSM), lambda c: (rev(c), C_U // D_SSM)),
                   _whole(b_bd.shape), _whole(c_bd.shape), _whole((1, N_LANES)), _whole((1, N_LANES)),
                   _whole((1, D_SSM))],
        out_shape=[jax.ShapeDtypeStruct(dproj.shape, dproj.dtype),
                   jax.ShapeDtypeStruct(b_bd.shape, F32), jax.ShapeDtypeStruct(c_bd.shape, F32),
                   jax.ShapeDtypeStruct((1, N_LANES), F32), jax.ShapeDtypeStruct((1, N_LANES), F32),
                   jax.ShapeDtypeStruct((1, D_SSM), F32)],
        scratch_shapes=[big(), big(), big(), big()] + [small() for _ in range(8)]
        + [pltpu.VMEM((SSM_CHUNK, D_SSM), F32)],
        input_output_aliases={12: 0},
        compiler_params=_params(("arbitrary",)),
    )(proj, dg_in, y_ssm, car_re, car_im, b_bd, c_bd, p_re, p_im, d_skip, jnp.asarray(perm, BF16),
      jnp.asarray(perm.T, BF16), dproj)
    return outs


def _bucket_table():
    i = np.arange(BLOCK)[:, None]
    j = np.arange(2 * BLOCK)[None, :]
    dist = BLOCK + i - j
    ok = (dist >= 0) & (dist < WINDOW)
    max_exact = N_BUCKETS // 2
    d = np.maximum(dist, 1).astype(np.float32)
    large = max_exact + (np.log(d / max_exact) / math.log(MAX_DISTANCE / max_exact)
                         * (N_BUCKETS - max_exact)).astype(np.int32)
    large = np.minimum(large, N_BUCKETS - 1)
    bucket = np.where(dist < max_exact, dist, large)
    return np.where(ok, bucket, -1).astype(np.int32)


def _bias_build_body(table_ref, bucket_ref, o_ref):
    h = pl.program_id(0)
    bucket = bucket_ref[...]
    acc = jnp.full(bucket.shape, NEG_INF, F32)
    for b in range(N_BUCKETS):
        acc = jnp.where(bucket == b, table_ref[b, h], acc)
    o_ref[...] = acc


def _bias_build(rel_bias_table, bucket):
    return pl.pallas_call(
        functools.partial(_bias_build_body), name="bias_build", grid=(N_Q_HEADS,),
        in_specs=[pl.BlockSpec(memory_space=pltpu.SMEM), _whole(bucket.shape)],
        out_specs=pl.BlockSpec((None, BLOCK, 2 * BLOCK), lambda h: (h, 0, 0)),
        out_shape=jax.ShapeDtypeStruct((N_Q_HEADS, BLOCK, 2 * BLOCK), F32),
        compiler_params=_params(("arbitrary",)),
    )(rel_bias_table, bucket)


def _bias_grad_body(g_ref, bucket_ref, o_ref):
    bucket = bucket_ref[...]
    lane = lax.broadcasted_iota(jnp.int32, (N_BUCKETS, 128), 1)

    def head(h, out):
        g = g_ref[h]
        rows = [jnp.sum(jnp.where(bucket == b, g, 0.0), axis=0, keepdims=True) for b in range(N_BUCKETS)]
        colsum = jnp.sum(jnp.concatenate(rows, axis=0), axis=1, keepdims=True)
        return jnp.where(lane == h, colsum, out)

    o_ref[...] = lax.fori_loop(0, N_Q_HEADS, head, jnp.zeros((N_BUCKETS, 128), F32))


def _bias_grad(g_bias, bucket):
    out = pl.pallas_call(
        functools.partial(_bias_grad_body), name="bias_grad", grid=(1,),
        in_specs=[_whole(g_bias.shape), _whole(bucket.shape)],
        out_specs=_whole((N_BUCKETS, 128)),
        out_shape=jax.ShapeDtypeStruct((N_BUCKETS, 128), F32),
        compiler_params=_params(("arbitrary",)),
    )(g_bias, bucket)
    return out[:, :N_Q_HEADS]


def _head_logits(qh, kk, bias_h, first_block):
    s = lax.dot_general(qh, kk, _DIMS["nt"], preferred_element_type=F32) * ATTN_SCALE + bias_h
    col = lax.broadcasted_iota(jnp.int32, s.shape, 1)
    return jnp.where(jnp.logical_and(first_block, col < BLOCK), NEG_INF, s)


def _attn_fwd_body(sink_ref, q_ref, kp_ref, kc_ref, vp_ref, vc_ref, bias_ref, o_ref, lse_ref):
    n = pl.program_id(0)
    outs, lses = [], []
    for kv in range(N_KV_HEADS):
        cs = slice(kv * HEAD_DIM, (kv + 1) * HEAD_DIM)
        kk = jnp.concatenate([kp_ref[:, cs], kc_ref[:, cs]], axis=0).astype(BF16)
        vv = jnp.concatenate([vp_ref[:, cs], vc_ref[:, cs]], axis=0).astype(BF16)
        for g in range(Q_PER_KV):
            h = kv * Q_PER_KV + g
            qh = q_ref[:, h * HEAD_DIM:(h + 1) * HEAD_DIM].astype(BF16)
            s = _head_logits(qh, kk, bias_ref[h], n == 0)
            sink = sink_ref[0, h]
            m = jnp.maximum(jnp.max(s, axis=1, keepdims=True), sink)
            p = jnp.exp(s - m)
            den = jnp.sum(p, axis=1, keepdims=True) + jnp.exp(sink - m)
            p = p / den
            outs.append(jnp.dot(p.astype(BF16), vv, preferred_element_type=F32))
            lses.append(m + jnp.log(den))
    o_ref[...] = jnp.concatenate(outs, axis=1).astype(o_ref.dtype)
    lse_ref[...] = jnp.concatenate(lses, axis=1)


def _attn_specs(nb):
    prev = lambda n: jnp.maximum(jnp.minimum(n, nb - 1) - 1, 0)
    cur = lambda n: jnp.minimum(n, nb - 1)
    return [pl.BlockSpec((BLOCK, D_ATTN), lambda n: (cur(n), C_Q // D_ATTN)),
            pl.BlockSpec((BLOCK, D_KV), lambda n: (prev(n), C_K // D_KV)),
            pl.BlockSpec((BLOCK, D_KV), lambda n: (cur(n), C_K // D_KV)),
            pl.BlockSpec((BLOCK, D_KV), lambda n: (prev(n), C_V // D_KV)),
            pl.BlockSpec((BLOCK, D_KV), lambda n: (cur(n), C_V // D_KV))]


def _attn_forward(proj, sinks, bias):
    t = proj.shape[0]
    nb = t // BLOCK
    return pl.pallas_call(
        functools.partial(_attn_fwd_body), name="attn_fwd", grid=(nb,),
        in_specs=[pl.BlockSpec(memory_space=pltpu.SMEM)] + _attn_specs(nb) + [_whole(bias.shape)],
        out_specs=[pl.BlockSpec((BLOCK, D_ATTN), lambda n: (n, 0)),
                   pl.BlockSpec((BLOCK, N_Q_HEADS), lambda n: (n, 0))],
        out_shape=[jax.ShapeDtypeStruct((t, D_ATTN), BF16), jax.ShapeDtypeStruct((t, N_Q_HEADS), F32)],
        compiler_params=_params(("parallel",)),
    )(sinks, proj, proj, proj, proj, proj, bias)


def _attn_bwd_body(sink_ref, q_ref, kp_ref, kc_ref, vp_ref, vc_ref, bias_ref, do_ref, o_ref, lse_ref, dproj_in,
                   dq_ref, dkv_ref, gbias_ref, gsink_ref, carry_ref, *, nb):
    del dproj_in
    n = pl.program_id(0)

    @pl.when(n == 0)
    def _():
        gbias_ref[...] = jnp.zeros_like(gbias_ref)
        gsink_ref[...] = jnp.zeros_like(gsink_ref)
        carry_ref[...] = jnp.zeros_like(carry_ref)

    @pl.when(n < nb)
    def _():
        lane = lax.broadcasted_iota(jnp.int32, (1, 128), 1)
        dqs, dks, dvs = [], [], []
        gsink = jnp.zeros((1, 128), F32)
        for kv in range(N_KV_HEADS):
            cs = slice(kv * HEAD_DIM, (kv + 1) * HEAD_DIM)
            kk = jnp.concatenate([kp_ref[:, cs], kc_ref[:, cs]], axis=0).astype(BF16)
            vv = jnp.concatenate([vp_ref[:, cs], vc_ref[:, cs]], axis=0).astype(BF16)
            dk = jnp.zeros((2 * BLOCK, HEAD_DIM), F32)
            dv = jnp.zeros((2 * BLOCK, HEAD_DIM), F32)
            for g in range(Q_PER_KV):
                h = kv * Q_PER_KV + g
                hs = slice(h * HEAD_DIM, (h + 1) * HEAD_DIM)
                qh = q_ref[:, hs].astype(BF16)
                s = _head_logits(qh, kk, bias_ref[h], n == 0)
                lse = lse_ref[:, h:h + 1]
                p = jnp.exp(s - lse)
                do = do_ref[:, hs].astype(F32)
                delta = jnp.sum(do * o_ref[:, hs].astype(F32), axis=1, keepdims=True)
                dob = do.astype(BF16)
                dp = lax.dot_general(dob, vv, _DIMS["nt"], preferred_element_type=F32)
                dl = p * (dp - delta)
                gbias_ref[h] += dl
                psink = jnp.exp(sink_ref[0, h] - lse)
                gsink = gsink + jnp.where(lane == h, -jnp.sum(psink * delta), 0.0)
                dlb = dl.astype(BF16)
                dqs.append(jnp.dot(dlb, kk, preferred_element_type=F32) * ATTN_SCALE)
                dk = dk + lax.dot_general(dlb, qh, _DIMS["tn"], preferred_element_type=F32) * ATTN_SCALE
                dv = dv + lax.dot_general(p.astype(BF16), dob, _DIMS["tn"], preferred_element_type=F32)
            dks.append(dk)
            dvs.append(dv)
        dq_ref[...] = jnp.concatenate(dqs, axis=1).astype(dq_ref.dtype)
        gsink_ref[...] += gsink
        dkv = jnp.concatenate(dks + dvs, axis=1)
        dkv_ref[...] = (carry_ref[...] + dkv[:BLOCK]).astype(dkv_ref.dtype)
        carry_ref[...] = dkv[BLOCK:]

    @pl.when(n == nb)
    def _():
        dkv_ref[...] = carry_ref[...].astype(dkv_ref.dtype)


def _attn_backward(proj, sinks, bias, d_attn, attn, lse, dproj):
    t = proj.shape[0]
    nb = t // BLOCK
    cur = lambda n: jnp.minimum(n, nb - 1)
    return pl.pallas_call(
        functools.partial(_attn_bwd_body, nb=nb), name="attn_bwd", grid=(nb + 1,),
        in_specs=[pl.BlockSpec(memory_space=pltpu.SMEM)] + _attn_specs(nb) + [
            _whole(bias.shape),
            pl.BlockSpec((BLOCK, D_ATTN), lambda n: (cur(n), 0)),
            pl.BlockSpec((BLOCK, D_ATTN), lambda n: (cur(n), 0)),
            pl.BlockSpec((BLOCK, N_Q_HEADS), lambda n: (cur(n), 0)),
            pl.BlockSpec(memory_space=pl.ANY)],
        out_specs=[pl.BlockSpec((BLOCK, D_ATTN), lambda n: (cur(n), C_Q // D_ATTN)),
                   pl.BlockSpec((BLOCK, 2 * D_KV), lambda n: (jnp.maximum(n - 1, 0), 0)),
                   _whole(bias.shape), _whole((1, 128))],
        out_shape=[jax.ShapeDtypeStruct(dproj.shape, dproj.dtype), jax.ShapeDtypeStruct((t, 2 * D_KV), dproj.dtype),
                   jax.ShapeDtypeStruct(bias.shape, F32), jax.ShapeDtypeStruct((1, 128), F32)],
        scratch_shapes=[pltpu.VMEM((BLOCK, 2 * D_KV), F32)],
        input_output_aliases={10: 0},
        compiler_params=_params(("arbitrary",)),
    )(sinks, proj, proj, proj, proj, proj, bias, d_attn, attn, lse, dproj)


ROWS = 512


def _rowwise(body, name, t, ins, outs, aliases=None):
    rows = min(ROWS, t)

    def col_spec(w, c0):
        if c0 == "T":
            return pl.BlockSpec((w, rows), lambda i: (0, i))
        if c0 % w == 0:
            return pl.BlockSpec((rows, w), lambda i: (i, c0 // w))
        return pl.BlockSpec((pl.Element(rows), pl.Element(w)), lambda i: (i * rows, c0))

    in_specs, args = [], []
    for a, w, c0 in ins:
        args.append(a)
        if w is None:
            in_specs.append(pl.BlockSpec(memory_space=pl.ANY) if c0 == "any" else _whole(a.shape))
        else:
            in_specs.append(col_spec(w, c0))
    out_specs, out_shape = [], []
    for shape, dtype, w, c0 in outs:
        out_shape.append(jax.ShapeDtypeStruct(shape, dtype))
        out_specs.append(_whole(shape) if w is None else col_spec(w, c0))
    accum = any(o[2] is None for o in outs)
    return pl.pallas_call(
        body, name=name, grid=(t // rows,), in_specs=in_specs, out_specs=out_specs, out_shape=out_shape,
        input_output_aliases=aliases or {},
        compiler_params=_params(("arbitrary",) if accum else ("parallel",)),
    )(*args)


def _f32(ref, *idx):
    return (ref[idx] if idx else ref[...]).astype(F32)


def _store_with_transpose(val, ref, t_ref):
    val = val.astype(ref.dtype)
    ref[...] = val
    t_ref[...] = val.T


def _ssm_gate_fwd_body(glu_ref, z_ref, h_ref, ht_ref):
    a, b = _f32(glu_ref, slice(None), slice(0, D_SSM)), _f32(glu_ref, slice(None), slice(D_SSM, None))
    _store_with_transpose((a * _sigmoid(b)) * _silu_and_grad(_f32(z_ref))[0], h_ref, ht_ref)


def _ssm_gate_bwd_body(dh_ref, glu_ref, z_ref, dproj_in, dglu_ref, dz_ref):
    del dproj_in
    a, b = _f32(glu_ref, slice(None), slice(0, D_SSM)), _f32(glu_ref, slice(None), slice(D_SSM, None))
    sb = _sigmoid(b)
    silu, dsilu = _silu_and_grad(_f32(z_ref))
    dh = _f32(dh_ref)
    dg = dh * silu
    dz_ref[...] = (dh * (a * sb) * dsilu).astype(dz_ref.dtype)
    dglu_ref[:, :D_SSM] = (dg * sb).astype(dglu_ref.dtype)
    dglu_ref[:, D_SSM:] = (dg * a * sb * (1.0 - sb)).astype(dglu_ref.dtype)


def _attn_gate_fwd_body(attn_ref, z_ref, h_ref, ht_ref):
    _store_with_transpose(_f32(attn_ref) * _silu_and_grad(_f32(z_ref))[0], h_ref, ht_ref)


def _attn_gate_bwd_body(dh_ref, attn_ref, z_ref, dproj_in, dattn_ref, dz_ref):
    del dproj_in
    silu, dsilu = _silu_and_grad(_f32(z_ref))
    dh = _f32(dh_ref)
    dattn_ref[...] = (dh * silu).astype(dattn_ref.dtype)
    dz_ref[...] = (dh * _f32(attn_ref) * dsilu).astype(dz_ref.dtype)


def _merge_fwd_body(bs_ref, ba_ref, gl_ref, m_ref, mt_ref):
    gs = _sigmoid(_f32(gl_ref, slice(None), slice(0, D_MODEL)))
    ga = _sigmoid(_f32(gl_ref, slice(None), slice(D_MODEL, None)))
    _store_with_transpose(gs * _f32(bs_ref) + ga * _f32(ba_ref), m_ref, mt_ref)


def _merge_bwd_body(dm_ref, bs_ref, ba_ref, gl_ref, dbs_ref, dba_ref, dgl_ref):
    gs = _sigmoid(_f32(gl_ref, slice(None), slice(0, D_MODEL)))
    ga = _sigmoid(_f32(gl_ref, slice(None), slice(D_MODEL, None)))
    dm = _f32(dm_ref)
    dbs_ref[...] = (dm * gs).astype(dbs_ref.dtype)
    dba_ref[...] = (dm * ga).astype(dba_ref.dtype)
    dgl_ref[:, :D_MODEL] = (dm * _f32(bs_ref) * gs * (1.0 - gs)).astype(dgl_ref.dtype)
    dgl_ref[:, D_MODEL:] = (dm * _f32(ba_ref) * ga * (1.0 - ga)).astype(dgl_ref.dtype)


def _ln_loss_body(x_ref, o_ref, tgt_ref, gain_ref, bias_ref, dr_ref, loss_ref, dgain_ref, dbias_ref):
    @pl.when(pl.program_id(0) == 0)
    def _():
        loss_ref[...] = jnp.zeros_like(loss_ref)
        dgain_ref[...] = jnp.zeros_like(dgain_ref)
        dbias_ref[...] = jnp.zeros_like(dbias_ref)

    r = ALPHA * x_ref[...] + o_ref[...].astype(F32)
    mu = jnp.mean(r, axis=1, keepdims=True)
    rc = r - mu
    var = jnp.mean(rc * rc, axis=1, keepdims=True)
    rstd = lax.rsqrt(var + LN_EPS)
    xhat = rc * rstd
    gain = gain_ref[...]
    err = xhat * gain + bias_ref[...] - tgt_ref[...]
    loss_ref[...] += 0.5 * jnp.sum(jnp.mean(err * err, axis=1, keepdims=True), axis=0, keepdims=True)
    dy = err * (1.0 / D_MODEL)
    dgain_ref[...] += jnp.sum(dy * xhat, axis=0, keepdims=True)
    dbias_ref[...] += jnp.sum(dy, axis=0, keepdims=True)
    dxhat = dy * gain
    m1 = jnp.mean(dxhat, axis=1, keepdims=True)
    m2 = jnp.mean(dxhat * xhat, axis=1, keepdims=True)
    dr_ref[...] = (rstd * (dxhat - m1 - xhat * m2)).astype(dr_ref.dtype)


def _place_body(piece_ref, dproj_in, o_ref):
    del dproj_in
    o_ref[...] = piece_ref[...]


def _adamw_update(w_ref, m_ref, v_ref, g, g_ref, d_ref, nm_ref, nv_ref):
    m = ADAM_B1 * m_ref[...] + (1.0 - ADAM_B1) * g
    v = ADAM_B2 * v_ref[...] + (1.0 - ADAM_B2) * (g * g)
    m_hat = m / (1.0 - ADAM_B1 ** ADAM_STEP)
    v_hat = v / (1.0 - ADAM_B2 ** ADAM_STEP)
    g_ref[...] = g
    d_ref[...] = -ADAM_LR * (m_hat / (jnp.sqrt(v_hat) + ADAM_EPS) + ADAM_WD * w_ref[...])
    nm_ref[...] = m
    nv_ref[...] = v


def _adamw_body(*refs, n_parts):
    w_ref, m_ref, v_ref = refs[:3]
    parts = refs[3:3 + n_parts]
    g = parts[0][...].astype(F32)
    for p in parts[1:]:
        g = g + p[...].astype(F32)
    _adamw_update(w_ref, m_ref, v_ref, g, *refs[3 + n_parts:])


def _adamw_shard_body(c_ref, w_ref, m_ref, v_ref, mine_ref, other_ref, g_ref, d_ref, nm_ref, nv_ref, *, nth):
    in_mine = pl.program_id(0) // nth == c_ref[0]
    g = jnp.where(in_mine, mine_ref[...], other_ref[...])
    _adamw_update(w_ref, m_ref, v_ref, g, g_ref, d_ref, nm_ref, nv_ref)


def _adamw_shard(w, m, v, halves, core, *, name, rows):
    shape = w.shape
    w2, m2, v2 = (a.reshape(-1, shape[-1]) for a in (w, m, v))
    r, c = w2.shape
    nth = r // 2 // rows
    assert 2 * nth * rows == r and halves[0].shape == (r // 2, c)
    spec = pl.BlockSpec((rows, c), lambda i, cr: (i, 0))
    mine_spec = pl.BlockSpec((rows, c), lambda i, cr: (jnp.where(i // nth == cr[0], i % nth, 0), 0))
    other_spec = pl.BlockSpec((rows, c), lambda i, cr: (jnp.where(i // nth == cr[0], 0, i % nth), 0))
    outs = pl.pallas_call(
        functools.partial(_adamw_shard_body, nth=nth), name=name,
        grid_spec=pltpu.PrefetchScalarGridSpec(
            num_scalar_prefetch=1, grid=(r // rows,),
            in_specs=[spec] * 3 + [mine_spec, other_spec], out_specs=[spec] * 4),
        out_shape=[jax.ShapeDtypeStruct((r, c), F32)] * 4,
        compiler_params=_params(("parallel",)),
    )(core, w2, m2, v2, *halves)
    return tuple(o.reshape(shape) for o in outs)


def _adamw(w, m, v, parts, *, name, rows=256):
    shape = w.shape
    w2, m2, v2 = (a.reshape(-1, shape[-1]) for a in (w, m, v))
    parts = [p.reshape(w2.shape) for p in parts]
    r, c = w2.shape
    rows = rows if r % rows == 0 else r
    spec = pl.BlockSpec((rows, c), lambda i: (i, 0))
    outs = pl.pallas_call(
        functools.partial(_adamw_body, n_parts=len(parts)), name=name, grid=(r // rows,),
        in_specs=[spec] * (3 + len(parts)), out_specs=[spec] * 4,
        out_shape=[jax.ShapeDtypeStruct((r, c), F32)] * 4,
        compiler_params=_params(("parallel",)),
    )(w2, m2, v2, *parts)
    return tuple(o.reshape(shape) for o in outs)


BIG = ("w_in", "w_glu", "w_branch_ssm", "w_branch_attn", "w_out")
SHARD_AXIS = dict(w_in=1, w_glu=1, w_branch_ssm=1, w_branch_attn=1, w_out=0)
HBM = pl.BlockSpec(memory_space=pl.ANY)


def _position():
    x, y, c = lax.axis_index("x"), lax.axis_index("y"), lax.axis_index("c")
    other_chips = [(1 - x, y), (x, 1 - y), (1 - x, 1 - y)]
    return x, y, c, other_chips


def _window(ref, axis, shard, n_shards, half=None):
    rows, cols = ref.shape[-2:]
    sel = [slice(None), slice(None)]
    size = ref.shape[-2 + axis] // n_shards
    sel[axis] = pl.ds(pl.multiple_of(shard * size, 128), size)
    if half is not None:
        hsize = ref.shape[-1 - axis] // 2
        sel[1 - axis] = pl.ds(pl.multiple_of(half * hsize, 128), hsize)
    return ref.at[tuple(sel)]


def _half(ref, axis, half):
    hsize = ref.shape[-1 - axis] // 2
    sel = [slice(None), slice(None)]
    sel[1 - axis] = pl.ds(pl.multiple_of(half * hsize, 128), hsize)
    return ref.at[tuple(sel)]


def _remote(src, dst, send_sem, recv_sem, device):
    return pltpu.make_async_remote_copy(src_ref=src, dst_ref=dst, send_sem=send_sem, recv_sem=recv_sem,
                                        device_id=device, device_id_type=MESH)


def _full_shapes(shards, names):
    out = []
    for k in names:
        s = list(shards[k].shape)
        s[SHARD_AXIS[k]] *= N_CHIPS
        out.append(jax.ShapeDtypeStruct(tuple(s), shards[k].dtype))
    return out


def _gather_exchange(shards, names):
    axes = tuple(SHARD_AXIS[k] for k in names)
    n = len(names)

    def copies(ins, outs, sems):
        send_sems, recv_sems, local_sems = sems
        x, y, c, chips = _position()
        me = 2 * x + y
        remote, local = [], []
        for w, ax in enumerate(axes):
            for r, (px, py) in enumerate(chips):
                remote.append(_remote(_half(ins[w], ax, c), _window(outs[w], ax, me, N_CHIPS, c),
                                      send_sems.at[3 * w + r], recv_sems.at[3 * w + r], (px, py, c)))
            local.append(pltpu.make_async_copy(ins[w], _window(outs[w], ax, me, N_CHIPS), local_sems.at[w]))
        return remote, local

    return _Exchange([shards[k] for k in names], _full_shapes(shards, names), (3 * n, 3 * n, n), copies)


SEM = pl.BlockSpec(memory_space=pltpu.SEMAPHORE)


def _gather_start_body(shard_ref, land_ref, send_sems, recv_sems, shard_thru, land_thru, token, *, axis):
    del shard_thru, land_thru
    x, y, c, chips = _position()
    me = 2 * x + y
    for r, (px, py) in enumerate(chips):
        _remote(_half(shard_ref, axis, c), _window(land_ref, axis, me, N_CHIPS, c),
                send_sems.at[r], recv_sems.at[r], (px, py, c)).start()
    token[...] = jnp.zeros_like(token)


def _gather_start(shard, name):
    axis = SHARD_AXIS[name]
    full = _full_shapes({name: shard}, (name,))[0]
    return pl.pallas_call(
        functools.partial(_gather_start_body, axis=axis), name="gather_start_" + name,
        out_shape=(pltpu.SemaphoreType.DMA((3,)), pltpu.SemaphoreType.DMA((3,)), pltpu.HBM(shard.shape, shard.dtype),
                   pltpu.HBM(full.shape, full.dtype), jax.ShapeDtypeStruct((8, 128), F32)),
        in_specs=(HBM, HBM), out_specs=(SEM, SEM, HBM, HBM, pl.BlockSpec(memory_space=pltpu.VMEM)),
        input_output_aliases={0: 2, 1: 3},
        compiler_params=pltpu.CompilerParams(has_side_effects=pltpu.SideEffectType.DATAFLOW_SIDE_EFFECTING),
    )(pltpu.with_memory_space_constraint(shard, pltpu.HBM),
      pltpu.with_memory_space_constraint(lax.empty(full.shape, full.dtype), pltpu.HBM))


def _gather_wait_body(shard_ref, land_ref, send_sems, recv_sems, after_ref, shard_dead, landed, *, axis):
    del after_ref, shard_dead, landed
    x, y, c, chips = _position()
    for r, (px, py) in enumerate(chips):
        cp = _remote(_half(shard_ref, axis, c), _window(land_ref, axis, 2 * px + py, N_CHIPS, c),
                     send_sems.at[r], recv_sems.at[r], (px, py, c))
        cp.wait_send()
        cp.wait_recv()


def _gather_wait(send_sems, recv_sems, shard_thru, land_thru, after, name):
    return pl.pallas_call(
        functools.partial(_gather_wait_body, axis=SHARD_AXIS[name]), name="gather_wait_" + name,
        out_shape=(pltpu.HBM(shard_thru.shape, shard_thru.dtype), pltpu.HBM(land_thru.shape, land_thru.dtype)),
        in_specs=(HBM, HBM, SEM, SEM, HBM), out_specs=(HBM, HBM), input_output_aliases={0: 0, 1: 1},
        compiler_params=pltpu.CompilerParams(has_side_effects=pltpu.SideEffectType.DATAFLOW_SIDE_EFFECTING),
    )(shard_thru, land_thru, send_sems, recv_sems, after)[1]


def _pass_on_body(*refs, axes, n_own):
    n = len(axes)
    ins, own = refs[:n], refs[n:n + n_own]
    refs = refs[:n] + refs[n + n_own:]
    outs = refs[n:2 * n]
    sbuf, rbuf = refs[2 * n:5 * n], refs[5 * n:8 * n]
    obuf = refs[8 * n:8 * n + n_own]
    send_sems, recv_sems, load_sems, store_sems, own_sems, placed_sems = refs[8 * n + n_own:]
    x, y, c, chips = _position()
    fetched = [pltpu.make_async_copy(own[w], obuf[w], own_sems.at[w]) for w in range(n_own)]
    for cp in fetched:
        cp.start()
    region = lambda ref, w, r, half: _window(ref, axes[w], 2 * chips[r][0] + chips[r][1], N_CHIPS, half)
    pairs = [(w, r) for w in range(n) for r in range(3)]
    loads = [pltpu.make_async_copy(region(ins[w], w, r, c), sbuf[3 * w + r], load_sems.at[3 * w + r]) for w, r in pairs]
    for cp in loads:
        cp.start()
    placed = []
    for w, cp in enumerate(fetched):
        cp.wait()
        placed.append(pltpu.make_async_copy(obuf[w], _window(outs[w], axes[w], 2 * x + y, N_CHIPS), placed_sems.at[w]))
        placed[-1].start()
    sends = []
    for i, cp in enumerate(loads):
        cp.wait()
        sends.append(_remote(sbuf[i], rbuf[i], send_sems.at[i], recv_sems.at[i], (x, y, 1 - c)))
        sends[-1].start()
    stores = []
    for i, (w, r) in enumerate(pairs):
        sends[i].wait_recv()
        stores.append(pltpu.make_async_copy(rbuf[i], region(outs[w], w, r, 1 - c), store_sems.at[i]))
        stores[-1].start()
    for cp in sends:
        cp.wait_send()
    for cp in stores + placed:
        cp.wait()


def _pass_on(fulls, names, own=()):
    axes = tuple(SHARD_AXIS[k] for k in names)
    n = len(names)
    assert len(own) in (0, n)
    bufs = []
    for a, ax in zip(fulls, axes):
        s = list(a.shape)
        s[ax] //= N_CHIPS
        s[1 - ax] //= 2
        bufs += [pltpu.VMEM(tuple(s), a.dtype)] * 3
    return pl.pallas_call(
        functools.partial(_pass_on_body, axes=axes, n_own=len(own)), name="pass_on_" + names[0],
        in_specs=[HBM] * (n + len(own)), out_specs=[HBM] * n,
        out_shape=[jax.ShapeDtypeStruct(a.shape, a.dtype) for a in fulls],
        scratch_shapes=bufs + bufs + [pltpu.VMEM(o.shape, o.dtype) for o in own]
        + [pltpu.SemaphoreType.DMA((3 * n,))] * 4 + [pltpu.SemaphoreType.DMA((n,))] * 2,
        input_output_aliases={i: i for i in range(n)},
        compiler_params=pltpu.CompilerParams(vmem_limit_bytes=VMEM_LIMIT),
    )(*fulls, *own)


def _chip_exchange(pairs, names):
    axes = tuple(SHARD_AXIS[k] for k in names)
    n = len(names)
    out_shape = []
    for p, ax in zip(pairs, axes):
        s = list(p.shape)
        s[ax] //= N_CHIPS
        out_shape.append(jax.ShapeDtypeStruct((4, *s), p.dtype))

    def copies(ins, outs, sems):
        send_sems, recv_sems, local_sems = sems
        x, y, c, chips = _position()
        me = 2 * x + y
        remote, local = [], []
        for w, ax in enumerate(axes):
            for r, (px, py) in enumerate(chips):
                remote.append(_remote(_window(ins[w], ax, 2 * px + py, N_CHIPS), outs[w].at[r],
                                      send_sems.at[3 * w + r], recv_sems.at[3 * w + r], (px, py, c)))
            local.append(pltpu.make_async_copy(_window(ins[w], ax, me, N_CHIPS), outs[w].at[3], local_sems.at[w]))
        return remote, local

    return _Exchange(pairs, out_shape, (3 * n, 3 * n, n), copies)


def _half_tile(n, h, nt, axis):
    return h * nt + n if axis == 1 else 2 * n + h


def _sibling_stream(n, nt, stage, recv, send_sems, recv_sems, credit, produce, consume):
    x, y, c, _ = _position()
    sibling = (x, y, 1 - c)

    def copy(slot):
        return _remote(stage.at[slot], recv.at[slot], send_sems.at[slot], recv_sems.at[slot], sibling)

    @pl.when(n < nt)
    def _():
        slot = n % 2

        @pl.when(n >= 2)
        def _():
            copy(slot).wait_send()
            pl.semaphore_wait(credit, 1)

        stage[slot] = produce().astype(stage.dtype)
        copy(slot).start()

    @pl.when(n >= 1)
    def _():
        slot = (n - 1) % 2
        copy(slot).wait_recv()
        consume(recv[slot])

        @pl.when(n - 1 < nt - 2)
        def _():
            pl.semaphore_signal(credit, 1, device_id=sibling, device_id_type=MESH)

    @pl.when(n == nt)
    def _():
        for slot in range(min(2, nt)):
            copy(slot).wait_send()


def _pair_reduce_body(c_ref, mine_ref, theirs_ref, out_ref, stage, recv, send_sems, recv_sems, credit, *, nt):
    del c_ref

    def consume(got):
        out_ref[...] = (mine_ref[...] + got.astype(F32)).astype(out_ref.dtype)

    _sibling_stream(pl.program_id(0), nt, stage, recv, send_sems, recv_sems, credit,
                    lambda: theirs_ref[...], consume)


def _pair_reduce(grad, core, axis, *, name, rows):
    r, c = grad.shape
    nt = r // 2 // rows
    assert nt * rows * 2 == r and (axis == 1 or rows == r // (2 * N_CHIPS))
    tile = lambda n, h: _half_tile(n, h, nt, axis)
    return pl.pallas_call(
        functools.partial(_pair_reduce_body, nt=nt), name=name,
        grid_spec=pltpu.PrefetchScalarGridSpec(
            num_scalar_prefetch=1, grid=(nt + 1,),
            in_specs=[pl.BlockSpec((rows, c), lambda n, cr: (tile(jnp.maximum(n - 1, 0), cr[0]), 0)),
                      pl.BlockSpec((rows, c), lambda n, cr: (tile(jnp.minimum(n, nt - 1), 1 - cr[0]), 0))],
            out_specs=pl.BlockSpec((rows, c), lambda n, cr: (jnp.maximum(n - 1, 0), 0)),
            scratch_shapes=[pltpu.VMEM((2, rows, c), BF16), pltpu.VMEM((2, rows, c), BF16),
                            pltpu.SemaphoreType.DMA((2,)), pltpu.SemaphoreType.DMA((2,)),
                            pltpu.SemaphoreType.REGULAR]),
        out_shape=jax.ShapeDtypeStruct((r // 2, c), BF16),
        compiler_params=_params(("arbitrary",)),
    )(core, grad, grad)


def _chip_add_share_body(c_ref, s0, s1, s2, s3, mine_ref, other_ref, stage, recv, send_sems, recv_sems, credit, *, nt):
    del c_ref

    def produce():
        total = s3[...].astype(F32) + s0[...].astype(F32) + s1[...].astype(F32) + s2[...].astype(F32)
        mine_ref[...] = total
        return total

    def consume(got):
        other_ref[...] = got

    _sibling_stream(pl.program_id(0), nt, stage, recv, send_sems, recv_sems, credit, produce, consume)


def _chip_add_share(slots, core, *, name, rows):
    _, r, c = slots.shape
    nt = r // rows
    assert nt * rows == r
    send = lambda j: pl.BlockSpec((None, rows, c), lambda n, cr: (j, jnp.minimum(n, nt - 1), 0))
    return pl.pallas_call(
        functools.partial(_chip_add_share_body, nt=nt), name=name,
        grid_spec=pltpu.PrefetchScalarGridSpec(
            num_scalar_prefetch=1, grid=(nt + 1,),
            in_specs=[send(j) for j in range(4)],
            out_specs=[pl.BlockSpec((rows, c), lambda n, cr: (jnp.minimum(n, nt - 1), 0)),
                       pl.BlockSpec((rows, c), lambda n, cr: (jnp.maximum(n - 1, 0), 0))],
            scratch_shapes=[pltpu.VMEM((2, rows, c), F32), pltpu.VMEM((2, rows, c), F32),
                            pltpu.SemaphoreType.DMA((2,)), pltpu.SemaphoreType.DMA((2,)),
                            pltpu.SemaphoreType.REGULAR]),
        out_shape=[jax.ShapeDtypeStruct((r, c), F32)] * 2,
        compiler_params=_params(("arbitrary",)),
    )(core, slots, slots, slots, slots)


PAIR_ROWS = dict(w_in=64, w_glu=128, w_branch_ssm=128, w_branch_attn=128, w_out=256)
SHARE_ROWS = dict(w_in=128, w_glu=128, w_branch_ssm=128, w_branch_attn=128, w_out=64)


def _pair_sums(grads, names, core):
    return [_pair_reduce(grads[k], core, SHARD_AXIS[k], name="pair_reduce_" + k, rows=PAIR_ROWS[k]) for k in names]


def _shard_halves(slots, names, core):
    return {k: _chip_add_share(s, core, name="chip_add_share_" + k, rows=SHARE_ROWS[k]) for k, s in zip(names, slots)}


SMALL = (("ssm_lambda_re", (1, 64, 64)), ("ssm_lambda_im", (1, 64, 64)), ("ssm_b_re", (1, 64, 64, 16)),
         ("ssm_b_im", (1, 64, 64, 16)), ("ssm_c_re", (1, 64, 16, 64)), ("ssm_c_im", (1, 64, 16, 64)),
         ("ssm_d", (1, 1024)), ("ssm_log_step", (1, 64)), ("attn_sinks", (1, 16)), ("rel_bias_table", (32, 16)),
         ("ln_gain", (1, 2048)), ("ln_bias", (1, 2048)))
SMALL_SIZE = sum(int(np.prod(s)) for _, s in SMALL)
PACK_ROWS = -(-(SMALL_SIZE + 1) // (8 * 128)) * 8


def _pack(values, extra=None):
    flat = [values[k].reshape(-1).astype(F32) for k, _ in SMALL]
    flat.append(jnp.zeros((1,), F32) if extra is None else extra.reshape(1))
    flat.append(jnp.zeros((PACK_ROWS * 128 - SMALL_SIZE - 1,), F32))
    return jnp.concatenate(flat).reshape(PACK_ROWS, 128)


def _unpack(packed):
    flat = packed.reshape(-1)
    out, off = {}, 0
    for k, s in SMALL:
        size = int(np.prod(s))
        out[k] = flat[off:off + size].reshape(s)
        off += size
    return out, flat[off]


def _small_exchange(packed):
    def copies(ins, outs, sems):
        send_sems, recv_sems, local_sems = sems
        x, y, c, _ = _position()
        remote = []
        for r in range(1, 8):
            peer = tuple(1 - v if (r >> s) & 1 else v for v, s in ((x, 2), (y, 1), (c, 0)))
            remote.append(_remote(ins[0], outs[0].at[r], send_sems.at[r - 1], recv_sems.at[r - 1], peer))
        return remote, [pltpu.make_async_copy(ins[0], outs[0].at[0], local_sems.at[0])]

    return _Exchange([packed], [jax.ShapeDtypeStruct((8, *packed.shape), F32)], (7, 7, 1), copies)


def _merge_exchanges(a, b):
    na, ma, sa = len(a.ins), len(a.out_shape), len(a.n_sems)

    def copies(ins, outs, sems):
        ra, la = a.copies(ins[:na], outs[:ma], sems[:sa])
        rb, lb = b.copies(ins[na:], outs[ma:], sems[sa:])
        return ra + rb, la + lb

    return _Exchange(a.ins + b.ins, a.out_shape + b.out_shape, a.n_sems + b.n_sems, copies)


def _small_sum_body(slots_ref, o_ref):
    x, y, c, _ = _position()
    me = 4 * x + 2 * y + c
    acc = slots_ref[me]
    for s in range(1, 8):
        acc = acc + slots_ref[jnp.bitwise_xor(me, s)]
    o_ref[...] = acc


def _small_sum(slots):
    vm = pl.BlockSpec(memory_space=pltpu.VMEM)
    return pl.pallas_call(
        functools.partial(_small_sum_body), name="small_sum",
        in_specs=[vm], out_specs=vm, out_shape=jax.ShapeDtypeStruct(slots.shape[1:], F32),
        compiler_params=pltpu.CompilerParams(vmem_limit_bytes=VMEM_LIMIT),
    )(slots)


LATE = BIG[1:]


def _local_step(x, target, shards, core, chip, lam_re, lam_im, b_re, b_im, c_re, c_im, d_skip,
                log_step, sinks, rel_bias_table, ln_gain, ln_bias):
    t = x.shape[0]
    send_sems, recv_sems, shard_thru, land_thru, token = _gather_start(shards["w_in"], "w_in")
    xb, xbt = _cast_and_transpose(x, token, name="cast_x")
    bucket = jnp.asarray(_bucket_table())
    p_re, p_im, b_bd, c_bd = _ssm_prepare(lam_re, lam_im, b_re, b_im, c_re, c_im, log_step)
    bias = _bias_build(rel_bias_table, bucket)

    act = dict(out_dtype=BF16)
    proj = _proj_shards(xb, shards["w_in"], chip, (0,), name="proj_own")
    landed = _gather_wait(send_sems, recv_sems, shard_thru, land_thru, proj, "w_in")
    (w_in,) = _pass_on([landed], BIG[:1], own=[shards["w_in"]])
    proj, landed = _proj_shards(xb, w_in, chip, (1, 2, 3), name="proj", base=proj,
                                exchange=_gather_exchange(shards, LATE))
    w_glu, w_bs, w_ba, w_out = _pass_on(list(landed), LATE)
    y_ssm, g_in, g_in_t, car_re, car_im = _ssm_forward(proj, p_re, p_im, b_bd, c_bd, d_skip)
    glu = _matmul(g_in, w_glu, "nn", name="glu", tm=1024, tn=2048, **act)
    h_ssm, h_ssm_t = _rowwise(functools.partial(_ssm_gate_fwd_body), "ssm_gate_fwd", t,
                              [(glu, 2 * D_SSM, 0), (proj, D_SSM, C_ZS)],
                              [((t, D_SSM), BF16, D_SSM, 0), ((D_SSM, t), BF16, D_SSM, "T")])
    attn, lse = _attn_forward(proj, sinks, bias)
    h_attn, h_attn_t = _rowwise(functools.partial(_attn_gate_fwd_body), "attn_gate_fwd", t,
                                [(attn, D_ATTN, 0), (proj, D_ATTN, C_ZA)],
                                [((t, D_ATTN), BF16, D_ATTN, 0), ((D_ATTN, t), BF16, D_ATTN, "T")])
    bs = _matmul(h_ssm, w_bs, "nn", name="branch_ssm", tm=1024, tn=2048, **act)
    ba = _matmul(h_attn, w_ba, "nn", name="branch_attn", tm=1024, tn=2048, **act)
    gl_in = (proj, 2 * D_MODEL, C_GL)
    merged, merged_t = _rowwise(functools.partial(_merge_fwd_body), "merge_fwd", t,
                                [(bs, D_MODEL, 0), (ba, D_MODEL, 0), gl_in],
                                [((t, D_MODEL), BF16, D_MODEL, 0), ((D_MODEL, t), BF16, D_MODEL, "T")])
    out = _matmul(merged, w_out, "nn", name="out_proj", tm=1024, tn=1024, **act)
    drb, loss, g_gain, g_bias = _rowwise(
        functools.partial(_ln_loss_body), "ln_loss", t,
        [(x, D_MODEL, 0), (out, D_MODEL, 0), (target, D_MODEL, 0), (ln_gain, None, 0), (ln_bias, None, 0)],
        [((t, D_MODEL), BF16, D_MODEL, 0), ((1, 1), F32, None, 0),
         ((1, D_MODEL), F32, None, 0), ((1, D_MODEL), F32, None, 0)])

    g_w_out = _matmul(merged_t, drb, "nn", name="g_w_out", tm=1024, tn=512)
    d_merged = _matmul(drb, w_out, "nt", name="d_merged", tm=1024, tn=1024, **act)
    d_bs, d_ba, dproj = _rowwise(
        functools.partial(_merge_bwd_body), "merge_bwd", t,
        [(d_merged, D_MODEL, 0), (bs, D_MODEL, 0), (ba, D_MODEL, 0), gl_in],
        [((t, D_MODEL), BF16, D_MODEL, 0), ((t, D_MODEL), BF16, D_MODEL, 0),
         ((t, D_IN), BF16, 2 * D_MODEL, C_GL)])
    g_w_bs = _matmul(h_ssm_t, d_bs, "nn", name="g_w_branch_ssm", tm=1024, tn=512)
    d_h_ssm = _matmul(d_bs, w_bs, "nt", name="d_h_ssm", tm=1024, tn=1024, **act)
    g_w_ba = _matmul(h_attn_t, d_ba, "nn", name="g_w_branch_attn", tm=1024, tn=512)
    d_h_attn = _matmul(d_ba, w_ba, "nt", name="d_h_attn", tm=1024, tn=1024, **act)

    d_attn, dproj = _rowwise(
        functools.partial(_attn_gate_bwd_body), "attn_gate_bwd", t,
        [(d_h_attn, D_ATTN, 0), (attn, D_ATTN, 0), (proj, D_ATTN, C_ZA), (dproj, None, "any")],
        [((t, D_ATTN), BF16, D_ATTN, 0), ((t, D_IN), BF16, D_ATTN, C_ZA)], aliases={3: 1})
    dproj, d_kv, g_bias_full, g_sinks = _attn_backward(proj, sinks, bias, d_attn, attn, lse, dproj)
    (dproj,) = _rowwise(functools.partial(_place_body), "place_dkv", t,
                        [(d_kv, 2 * D_KV, 0), (dproj, None, "any")],
                        [((t, D_IN), BF16, 2 * D_KV, C_K)], aliases={1: 0})
    g_table = _bias_grad(g_bias_full, bucket)

    d_glu, dproj = _rowwise(
        functools.partial(_ssm_gate_bwd_body), "ssm_gate_bwd", t,
        [(d_h_ssm, D_SSM, 0), (glu, 2 * D_SSM, 0), (proj, D_SSM, C_ZS), (dproj, None, "any")],
        [((t, 2 * D_SSM), BF16, 2 * D_SSM, 0), ((t, D_IN), BF16, D_SSM, C_ZS)], aliases={3: 1})
    g_w_glu = _matmul(g_in_t, d_glu, "nn", name="g_w_glu", tm=1024, tn=512)
    d_g_in = _matmul(d_glu, w_glu, "nt", name="d_g_in", tm=1024, tn=1024, **act)
    dproj, g_bbd, g_cbd, g_lam_re, g_lam_im, g_d = _ssm_backward(
        proj, d_g_in, y_ssm, car_re, car_im, p_re, p_im, b_bd, c_bd, d_skip, dproj)
    g_lr, g_li, g_br, g_bi, g_cr, g_ci, g_ls = _ssm_param_grads(
        lam_re, lam_im, b_re, b_im, log_step, g_lam_re, g_lam_im, g_bbd, g_cbd)

    late = dict(w_glu=g_w_glu, w_branch_ssm=g_w_bs, w_branch_attn=g_w_ba, w_out=g_w_out)
    g_w_in, late_slots = _matmul(xbt, dproj, "nn", name="g_w_in", tm=1024, tn=512,
                                 exchange=_chip_exchange(_pair_sums(late, LATE, core), LATE))
    small = dict(ssm_lambda_re=g_lr, ssm_lambda_im=g_li, ssm_b_re=g_br, ssm_b_im=g_bi, ssm_c_re=g_cr,
                 ssm_c_im=g_ci, ssm_d=g_d, ssm_log_step=g_ls, attn_sinks=g_sinks[:, :N_Q_HEADS],
                 rel_bias_table=g_table, ln_gain=g_gain, ln_bias=g_bias)
    last = _merge_exchanges(_chip_exchange(_pair_sums(dict(w_in=g_w_in), BIG[:1], core), BIG[:1]),
                            _small_exchange(_pack(small, loss)))
    grad_x, (in_slots, small_slots) = _matmul(dproj, w_in, "nt", name="grad_x", tm=512, tn=512, res=drb,
                                              res_scale=ALPHA, exchange=last)
    big = {**_shard_halves([in_slots], BIG[:1], core), **_shard_halves(late_slots, LATE, core)}
    return grad_x, big, _small_sum(small_slots)


WEIGHTS = ("w_in", "ssm_lambda_re", "ssm_lambda_im", "ssm_b_re", "ssm_b_im", "ssm_c_re", "ssm_c_im", "ssm_d",
           "ssm_log_step", "w_glu", "attn_sinks", "rel_bias_table", "w_branch_ssm", "w_branch_attn", "w_out",
           "ln_gain", "ln_bias")


def kernel(x, w_in, ssm_lambda_re, ssm_lambda_im, ssm_b_re, ssm_b_im, ssm_c_re, ssm_c_im, ssm_d, ssm_log_step, w_glu, attn_sinks, rel_bias_table, w_branch_ssm, w_branch_attn, w_out, ln_gain, ln_bias, loss_target, m_w_in, m_ssm_lambda_re, m_ssm_lambda_im, m_ssm_b_re, m_ssm_b_im, m_ssm_c_re, m_ssm_c_im, m_ssm_d, m_ssm_log_step, m_w_glu, m_attn_sinks, m_rel_bias_table, m_w_branch_ssm, m_w_branch_attn, m_w_out, m_ln_gain, m_ln_bias, v_w_in, v_ssm_lambda_re, v_ssm_lambda_im, v_ssm_b_re, v_ssm_b_im, v_ssm_c_re, v_ssm_c_im, v_ssm_d, v_ssm_log_step, v_w_glu, v_attn_sinks, v_rel_bias_table, v_w_branch_ssm, v_w_branch_attn, v_w_out, v_ln_gain, v_ln_bias):
    given = dict(locals())
    w = {k: given[k] for k in WEIGHTS}
    m = {k: given["m_" + k] for k in WEIGHTS}
    v = {k: given["v_" + k] for k in WEIGHTS}

    core = lax.axis_index("c").astype(jnp.int32).reshape(1)
    shards = {k: _cast(w[k][0], BF16, name="cast_" + k) for k in BIG}
    chip = (2 * lax.axis_index("x") + lax.axis_index("y")).astype(jnp.int32).reshape(1)
    grad_x, g_shard, g_packed = _local_step(
        x[0], loss_target[0], shards, core, chip, ssm_lambda_re[0], ssm_lambda_im[0], ssm_b_re[0],
        ssm_b_im[0], ssm_c_re[0], ssm_c_im[0], ssm_d, ssm_log_step, attn_sinks, rel_bias_table, ln_gain, ln_bias)
    loss_sum = _unpack(g_packed)[1]

    grad, delta, new_m, new_v = {}, {}, {}, {}
    for k in BIG:
        grad[k], delta[k], new_m[k], new_v[k] = _adamw_shard(w[k], m[k], v[k], g_shard[k], core,
                                                             name="adamw_" + k, rows=SHARE_ROWS[k])
    gs, ds, ms, vs = _adamw(_pack(w), _pack(m), _pack(v), [g_packed], name="adamw_small")
    for dst, packed in ((grad, gs), (delta, ds), (new_m, ms), (new_v, vs)):
        dst.update(_unpack(packed)[0])

    return (loss_sum, grad_x[None], *[grad[k] for k in WEIGHTS], *[delta[k] for k in WEIGHTS],
            *[new_m[k] for k in WEIGHTS], *[new_v[k] for k in WEIGHTS])
```
